```python
import jax, jax.numpy as jnp
from jax import lax
import numpy as np

D_MODEL = 1024
BATCH = 8
SEQ = 4096
DEPTH = 4

CHUNK = 64
Q_BLOCK = 128
PLE_DIM = 256

MLA_HEADS = 16
MLA_Q_LORA = D_MODEL // 2
MLA_KV_LORA = D_MODEL // 4
MLA_NOPE = 128
MLA_ROPE = 64
MLA_V = 128
ROPE_THETA = 10000.0

GLA_HEADS = 4
GLA_QK = D_MODEL // 2
GLA_VD = D_MODEL
GLA_DK = GLA_QK // GLA_HEADS
GLA_DV = GLA_VD // GLA_HEADS
GLA_GATE_RANK = 16
GLA_TAU = 16.0

D_FF = 2816
CONV_W = 3

N_MLA = (DEPTH + 1) // 2
N_GLA = DEPTH // 2
DN_ALPHA = (2 * DEPTH) ** 0.25
DN_BETA = (8 * DEPTH) ** -0.25
EPS = 1e-5
NEG_INF = -1e30

kernel_name = "hybrid_mla_gla_deepnorm_convffn_ple"


def layer_norm(x, g, b):
    xf = x.astype(jnp.float32)
    mu = jnp.mean(xf, -1, keepdims=True)
    var = jnp.mean(jnp.square(xf - mu), -1, keepdims=True)
    return ((xf - mu) * lax.rsqrt(var + EPS) * g + b).astype(x.dtype)


def rms_norm(x, g):
    xf = x.astype(jnp.float32)
    return (xf * lax.rsqrt(jnp.mean(jnp.square(xf), -1, keepdims=True) + EPS) * g).astype(x.dtype)


def rope_tables(positions):
    inv = 1.0 / (ROPE_THETA ** (jnp.arange(0, MLA_ROPE, 2, dtype=jnp.float32) / MLA_ROPE))
    ang = positions.astype(jnp.float32)[..., None] * inv
    return jnp.cos(ang), jnp.sin(ang)


def apply_rope(x, cos, sin):
    x1, x2 = jnp.split(x, 2, axis=-1)
    c = cos.astype(x.dtype)
    s = sin.astype(x.dtype)
    return jnp.concatenate([x1 * c - x2 * s, x1 * s + x2 * c], axis=-1)


def mla_mixer(x, positions, cos, sin, w_in, q_norm, kv_norm, w_uq, w_uk, w_uv, w_o):
    B, S, _ = x.shape
    h = x @ w_in
    c_q, c_kv, k_r = jnp.split(h, [MLA_Q_LORA, MLA_Q_LORA + MLA_KV_LORA], axis=-1)
    c_q = rms_norm(c_q, q_norm)
    c_kv = rms_norm(c_kv, kv_norm)
    q = (c_q @ w_uq).reshape(B, S, MLA_HEADS, MLA_NOPE + MLA_ROPE)
    q_nope = q[..., :MLA_NOPE]
    q_rope = apply_rope(q[..., MLA_NOPE:], cos[:, :, None], sin[:, :, None])
    k_rope = apply_rope(k_r, cos, sin)
    k_nope = (c_kv @ w_uk).reshape(B, S, MLA_HEADS, MLA_NOPE)
    v = (c_kv @ w_uv).reshape(B, S, MLA_HEADS, MLA_V)
    scale = (MLA_NOPE + MLA_ROPE) ** -0.5
    chunk_id = positions // CHUNK
    outs = []
    for blk in range(S // Q_BLOCK):
        q0 = blk * Q_BLOCK
        q1 = q0 + Q_BLOCK
        s = (jnp.einsum('bqhd,bkhd->bhqk', q_nope[:, q0:q1], k_nope[:, :q1])
             + jnp.einsum('bqhr,bkr->bhqk', q_rope[:, q0:q1], k_rope[:, :q1])).astype(jnp.float32) * scale
        mask = chunk_id[:, None, :q1] <= chunk_id[:, q0:q1, None]
        s = jnp.where(mask[:, None], s, NEG_INF)
        pr = jax.nn.softmax(s, axis=-1).astype(v.dtype)
        outs.append(jnp.einsum('bhqk,bkhd->bqhd', pr, v[:, :q1]))
    o = jnp.concatenate(outs, axis=1).reshape(B, S, MLA_HEADS * MLA_V)
    return (o @ w_o).astype(x.dtype)


def gla_mixer(x, w_in, w_a2, b_a, o_norm, w_o):
    B, S, _ = x.shape
    N = S // CHUNK
    h = x @ w_in
    q, k, v, r, a = jnp.split(h, [GLA_QK, 2 * GLA_QK, 2 * GLA_QK + GLA_VD, 2 * GLA_QK + 2 * GLA_VD], axis=-1)
    log_a = jax.nn.log_sigmoid((a @ w_a2 + b_a).astype(jnp.float32)) / GLA_TAU
    q = q.reshape(B, N, CHUNK, GLA_HEADS, GLA_DK) * (GLA_DK ** -0.5)
    k = k.reshape(B, N, CHUNK, GLA_HEADS, GLA_DK)
    v = v.reshape(B, N, CHUNK, GLA_HEADS, GLA_DV)
    log_a = log_a.reshape(B, N, CHUNK, GLA_HEADS, GLA_DK)
    cum = jnp.cumsum(log_a, axis=2)
    tot = cum[:, :, -1]
    k_dec = k * jnp.exp(tot[:, :, None] - cum).astype(k.dtype)
    upd = jnp.einsum('bnchk,bnchv->nbhkv', k_dec, v).astype(jnp.float32)
    decay = jnp.exp(jnp.moveaxis(tot, 1, 0))
    q_n = jnp.moveaxis(q, 1, 0)

    def step(state, inp):
        g, u, qc = inp
        state = state * g[..., None] + u
        return state, jnp.einsum('bchk,bhkv->bchv', qc, state)

    state0 = jnp.zeros((B, GLA_HEADS, GLA_DK, GLA_DV), jnp.float32)
    _, o = lax.scan(step, state0, (decay, upd, q_n))
    o = jnp.moveaxis(o, 0, 1).reshape(B, S, GLA_HEADS, GLA_DV)
    mu = jnp.mean(o, -1, keepdims=True)
    var = jnp.mean(jnp.square(o - mu), -1, keepdims=True)
    o = (o - mu) * lax.rsqrt(var + EPS) * o_norm.reshape(GLA_HEADS, GLA_DV)
    o = o.reshape(B, S, GLA_VD).astype(x.dtype) * jax.nn.silu(r)
    return (o @ w_o).astype(x.dtype)


def conv_ffn(x, w_up, conv_w, conv_b, w_down):
    S = x.shape[1]
    h = x @ w_up
    hp = jnp.pad(h, ((0, 0), (CONV_W - 1, 0), (0, 0)))
    h = hp[:, 0:S] * conv_w[0] + hp[:, 1:S + 1] * conv_w[1] + hp[:, 2:S + 2] * conv_w[2] + conv_b
    u, g = jnp.split(h, 2, axis=-1)
    return ((u * jax.nn.gelu(g)) @ w_down).astype(x.dtype)


def _fwd_setup_inputs(seed: int = 0) -> dict:
    key = jax.random.key(seed)
    ks = iter(jax.random.split(key, 40))
    f32 = jnp.float32

    def nrm(shape, scale):
        return jax.random.normal(next(ks), shape, f32) * scale

    def gain(shape):
        return 1.0 + nrm(shape, 0.01)

    x = jax.random.normal(next(ks), (BATCH, SEQ, D_MODEL), f32)
    p = jax.random.normal(next(ks), (DEPTH, BATCH, SEQ, PLE_DIM), f32)
    offsets = jax.random.randint(next(ks), (BATCH, 1), 0, 16, dtype=jnp.int32) * CHUNK
    positions = (jnp.arange(SEQ, dtype=jnp.int32)[None, :] + offsets).astype(jnp.int32)

    mla_in = MLA_Q_LORA + MLA_KV_LORA + MLA_ROPE
    gla_in = 2 * GLA_QK + 2 * GLA_VD + GLA_GATE_RANK
    return {
        "x": x,
        "p": p,
        "positions": positions,
        "mla_w_in": nrm((N_MLA, D_MODEL, mla_in), D_MODEL ** -0.5),
        "mla_q_norm": gain((N_MLA, MLA_Q_LORA)),
        "mla_kv_norm": gain((N_MLA, MLA_KV_LORA)),
        "mla_w_uq": nrm((N_MLA, MLA_Q_LORA, MLA_HEADS * (MLA_NOPE + MLA_ROPE)), MLA_Q_LORA ** -0.5),
        "mla_w_uk": nrm((N_MLA, MLA_KV_LORA, MLA_HEADS * MLA_NOPE), MLA_KV_LORA ** -0.5),
        "mla_w_uv": nrm((N_MLA, MLA_KV_LORA, MLA_HEADS * MLA_V), DN_BETA * MLA_KV_LORA ** -0.5),
        "mla_w_o": nrm((N_MLA, MLA_HEADS * MLA_V, D_MODEL), DN_BETA * (MLA_HEADS * MLA_V) ** -0.5),
        "gla_w_in": nrm((N_GLA, D_MODEL, gla_in), D_MODEL ** -0.5),
        "gla_w_a2": nrm((N_GLA, GLA_GATE_RANK, GLA_QK), GLA_GATE_RANK ** -0.5),
        "gla_b_a": nrm((N_GLA, GLA_QK), 0.1),
        "gla_o_norm": gain((N_GLA, GLA_VD)),
        "gla_w_o": nrm((N_GLA, GLA_VD, D_MODEL), DN_BETA * GLA_VD ** -0.5),
        "ln1_g": gain((DEPTH, D_MODEL)),
        "ln1_b": nrm((DEPTH, D_MODEL), 0.01),
        "ln2_g": gain((DEPTH, D_MODEL)),
        "ln2_b": nrm((DEPTH, D_MODEL), 0.01),
        "ffn_w_up": nrm((DEPTH, D_MODEL, 2 * D_FF), D_MODEL ** -0.5),
        "ffn_conv_w": nrm((DEPTH, CONV_W, 2 * D_FF), CONV_W ** -0.5),
        "ffn_conv_b": nrm((DEPTH, 2 * D_FF), 0.01),
        "ffn_w_down": nrm((DEPTH, D_FF, D_MODEL), DN_BETA * D_FF ** -0.5),
        "ple_w_proj": nrm((DEPTH, PLE_DIM, D_MODEL), PLE_DIM ** -0.5),
        "ple_w_gate": nrm((DEPTH, D_MODEL, D_MODEL), D_MODEL ** -0.5),
        "ple_b_gate": nrm((DEPTH, D_MODEL), 0.01),
    }


def _fwd_reference(x, p, positions, mla_w_in, mla_q_norm, mla_kv_norm, mla_w_uq, mla_w_uk, mla_w_uv, mla_w_o,
              gla_w_in, gla_w_a2, gla_b_a, gla_o_norm, gla_w_o, ln1_g, ln1_b, ln2_g, ln2_b,
              ffn_w_up, ffn_conv_w, ffn_conv_b, ffn_w_down, ple_w_proj, ple_w_gate, ple_b_gate):
    cos, sin = rope_tables(positions)
    for i in range(DEPTH):
        j = i // 2
        if i % 2 == 0:
            m = mla_mixer(x, positions, cos, sin, mla_w_in[j], mla_q_norm[j], mla_kv_norm[j],
                          mla_w_uq[j], mla_w_uk[j], mla_w_uv[j], mla_w_o[j])
        else:
            m = gla_mixer(x, gla_w_in[j], gla_w_a2[j], gla_b_a[j], gla_o_norm[j], gla_w_o[j])
        x = layer_norm(DN_ALPHA * x + m, ln1_g[i], ln1_b[i])
        x = layer_norm(DN_ALPHA * x + conv_ffn(x, ffn_w_up[i], ffn_conv_w[i], ffn_conv_b[i], ffn_w_down[i]),
                       ln2_g[i], ln2_b[i])
        gate = jax.nn.sigmoid(x @ ple_w_gate[i] + ple_b_gate[i])
        x = x + gate * (p[i] @ ple_w_proj[i])
    return x


import jax as _jax
import jax.numpy as _jnp

TWIN_FORMAT = 'train_step'
FWD_PARAMS = ['x', 'p', 'positions', 'mla_w_in', 'mla_q_norm', 'mla_kv_norm', 'mla_w_uq', 'mla_w_uk', 'mla_w_uv', 'mla_w_o', 'gla_w_in', 'gla_w_a2', 'gla_b_a', 'gla_o_norm', 'gla_w_o', 'ln1_g', 'ln1_b', 'ln2_g', 'ln2_b', 'ffn_w_up', 'ffn_conv_w', 'ffn_conv_b', 'ffn_w_down', 'ple_w_proj', 'ple_w_gate', 'ple_b_gate']
TWIN_WEIGHTS = ['mla_w_in', 'mla_q_norm', 'mla_kv_norm', 'mla_w_uq', 'mla_w_uk', 'mla_w_uv', 'mla_w_o', 'gla_w_in', 'gla_w_a2', 'gla_b_a', 'gla_o_norm', 'gla_w_o', 'ln1_g', 'ln1_b', 'ln2_g', 'ln2_b', 'ffn_w_up', 'ffn_conv_w', 'ffn_conv_b', 'ffn_w_down', 'ple_w_proj', 'ple_w_gate', 'ple_b_gate']
TWIN_DIFF_INPUT = 'x'
TWIN_INPUTS = ['x', 'p', 'positions', 'mla_w_in', 'mla_q_norm', 'mla_kv_norm', 'mla_w_uq', 'mla_w_uk', 'mla_w_uv', 'mla_w_o', 'gla_w_in', 'gla_w_a2', 'gla_b_a', 'gla_o_norm', 'gla_w_o', 'ln1_g', 'ln1_b', 'ln2_g', 'ln2_b', 'ffn_w_up', 'ffn_conv_w', 'ffn_conv_b', 'ffn_w_down', 'ple_w_proj', 'ple_w_gate', 'ple_b_gate', 'loss_target', 'm_mla_w_in', 'm_mla_q_norm', 'm_mla_kv_norm', 'm_mla_w_uq', 'm_mla_w_uk', 'm_mla_w_uv', 'm_mla_w_o', 'm_gla_w_in', 'm_gla_w_a2', 'm_gla_b_a', 'm_gla_o_norm', 'm_gla_w_o', 'm_ln1_g', 'm_ln1_b', 'm_ln2_g', 'm_ln2_b', 'm_ffn_w_up', 'm_ffn_conv_w', 'm_ffn_conv_b', 'm_ffn_w_down', 'm_ple_w_proj', 'm_ple_w_gate', 'm_ple_b_gate', 'v_mla_w_in', 'v_mla_q_norm', 'v_mla_kv_norm', 'v_mla_w_uq', 'v_mla_w_uk', 'v_mla_w_uv', 'v_mla_w_o', 'v_gla_w_in', 'v_gla_w_a2', 'v_gla_b_a', 'v_gla_o_norm', 'v_gla_w_o', 'v_ln1_g', 'v_ln1_b', 'v_ln2_g', 'v_ln2_b', 'v_ffn_w_up', 'v_ffn_conv_w', 'v_ffn_conv_b', 'v_ffn_w_down', 'v_ple_w_proj', 'v_ple_w_gate', 'v_ple_b_gate']
TWIN_OUTPUTS = ['loss', 'grad_x', 'grad_mla_w_in', 'grad_mla_q_norm', 'grad_mla_kv_norm', 'grad_mla_w_uq', 'grad_mla_w_uk', 'grad_mla_w_uv', 'grad_mla_w_o', 'grad_gla_w_in', 'grad_gla_w_a2', 'grad_gla_b_a', 'grad_gla_o_norm', 'grad_gla_w_o', 'grad_ln1_g', 'grad_ln1_b', 'grad_ln2_g', 'grad_ln2_b', 'grad_ffn_w_up', 'grad_ffn_conv_w', 'grad_ffn_conv_b', 'grad_ffn_w_down', 'grad_ple_w_proj', 'grad_ple_w_gate', 'grad_ple_b_gate', 'delta_mla_w_in', 'delta_mla_q_norm', 'delta_mla_kv_norm', 'delta_mla_w_uq', 'delta_mla_w_uk', 'delta_mla_w_uv', 'delta_mla_w_o', 'delta_gla_w_in', 'delta_gla_w_a2', 'delta_gla_b_a', 'delta_gla_o_norm', 'delta_gla_w_o', 'delta_ln1_g', 'delta_ln1_b', 'delta_ln2_g', 'delta_ln2_b', 'delta_ffn_w_up', 'delta_ffn_conv_w', 'delta_ffn_conv_b', 'delta_ffn_w_down', 'delta_ple_w_proj', 'delta_ple_w_gate', 'delta_ple_b_gate', 'new_m_mla_w_in', 'new_m_mla_q_norm', 'new_m_mla_kv_norm', 'new_m_mla_w_uq', 'new_m_mla_w_uk', 'new_m_mla_w_uv', 'new_m_mla_w_o', 'new_m_gla_w_in', 'new_m_gla_w_a2', 'new_m_gla_b_a', 'new_m_gla_o_norm', 'new_m_gla_w_o', 'new_m_ln1_g', 'new_m_ln1_b', 'new_m_ln2_g', 'new_m_ln2_b', 'new_m_ffn_w_up', 'new_m_ffn_conv_w', 'new_m_ffn_conv_b', 'new_m_ffn_w_down', 'new_m_ple_w_proj', 'new_m_ple_w_gate', 'new_m_ple_b_gate', 'new_v_mla_w_in', 'new_v_mla_q_norm', 'new_v_mla_kv_norm', 'new_v_mla_w_uq', 'new_v_mla_w_uk', 'new_v_mla_w_uv', 'new_v_mla_w_o', 'new_v_gla_w_in', 'new_v_gla_w_a2', 'new_v_gla_b_a', 'new_v_gla_o_norm', 'new_v_gla_w_o', 'new_v_ln1_g', 'new_v_ln1_b', 'new_v_ln2_g', 'new_v_ln2_b', 'new_v_ffn_w_up', 'new_v_ffn_conv_w', 'new_v_ffn_conv_b', 'new_v_ffn_w_down', 'new_v_ple_w_proj', 'new_v_ple_w_gate', 'new_v_ple_b_gate']
TWIN_LEAF_KINDS = {'loss': 'loss', 'grad_x': 'grad_x', 'grad_mla_w_in': 'grad_w', 'grad_mla_q_norm': 'grad_w', 'grad_mla_kv_norm': 'grad_w', 'grad_mla_w_uq': 'grad_w', 'grad_mla_w_uk': 'grad_w', 'grad_mla_w_uv': 'grad_w', 'grad_mla_w_o': 'grad_w', 'grad_gla_w_in': 'grad_w', 'grad_gla_w_a2': 'grad_w', 'grad_gla_b_a': 'grad_w', 'grad_gla_o_norm': 'grad_w', 'grad_gla_w_o': 'grad_w', 'grad_ln1_g': 'grad_w', 'grad_ln1_b': 'grad_w', 'grad_ln2_g': 'grad_w', 'grad_ln2_b': 'grad_w', 'grad_ffn_w_up': 'grad_w', 'grad_ffn_conv_w': 'grad_w', 'grad_ffn_conv_b': 'grad_w', 'grad_ffn_w_down': 'grad_w', 'grad_ple_w_proj': 'grad_w', 'grad_ple_w_gate': 'grad_w', 'grad_ple_b_gate': 'grad_w', 'delta_mla_w_in': 'delta_w', 'delta_mla_q_norm': 'delta_w', 'delta_mla_kv_norm': 'delta_w', 'delta_mla_w_uq': 'delta_w', 'delta_mla_w_uk': 'delta_w', 'delta_mla_w_uv': 'delta_w', 'delta_mla_w_o': 'delta_w', 'delta_gla_w_in': 'delta_w', 'delta_gla_w_a2': 'delta_w', 'delta_gla_b_a': 'delta_w', 'delta_gla_o_norm': 'delta_w', 'delta_gla_w_o': 'delta_w', 'delta_ln1_g': 'delta_w', 'delta_ln1_b': 'delta_w', 'delta_ln2_g': 'delta_w', 'delta_ln2_b': 'delta_w', 'delta_ffn_w_up': 'delta_w', 'delta_ffn_conv_w': 'delta_w', 'delta_ffn_conv_b': 'delta_w', 'delta_ffn_w_down': 'delta_w', 'delta_ple_w_proj': 'delta_w', 'delta_ple_w_gate': 'delta_w', 'delta_ple_b_gate': 'delta_w', 'new_m_mla_w_in': 'new_m', 'new_m_mla_q_norm': 'new_m', 'new_m_mla_kv_norm': 'new_m', 'new_m_mla_w_uq': 'new_m', 'new_m_mla_w_uk': 'new_m', 'new_m_mla_w_uv': 'new_m', 'new_m_mla_w_o': 'new_m', 'new_m_gla_w_in': 'new_m', 'new_m_gla_w_a2': 'new_m', 'new_m_gla_b_a': 'new_m', 'new_m_gla_o_norm': 'new_m', 'new_m_gla_w_o': 'new_m', 'new_m_ln1_g': 'new_m', 'new_m_ln1_b': 'new_m', 'new_m_ln2_g': 'new_m', 'new_m_ln2_b': 'new_m', 'new_m_ffn_w_up': 'new_m', 'new_m_ffn_conv_w': 'new_m', 'new_m_ffn_conv_b': 'new_m', 'new_m_ffn_w_down': 'new_m', 'new_m_ple_w_proj': 'new_m', 'new_m_ple_w_gate': 'new_m', 'new_m_ple_b_gate': 'new_m', 'new_v_mla_w_in': 'new_v', 'new_v_mla_q_norm': 'new_v', 'new_v_mla_kv_norm': 'new_v', 'new_v_mla_w_uq': 'new_v', 'new_v_mla_w_uk': 'new_v', 'new_v_mla_w_uv': 'new_v', 'new_v_mla_w_o': 'new_v', 'new_v_gla_w_in': 'new_v', 'new_v_gla_w_a2': 'new_v', 'new_v_gla_b_a': 'new_v', 'new_v_gla_o_norm': 'new_v', 'new_v_gla_w_o': 'new_v', 'new_v_ln1_g': 'new_v', 'new_v_ln1_b': 'new_v', 'new_v_ln2_g': 'new_v', 'new_v_ln2_b': 'new_v', 'new_v_ffn_w_up': 'new_v', 'new_v_ffn_conv_w': 'new_v', 'new_v_ffn_conv_b': 'new_v', 'new_v_ffn_w_down': 'new_v', 'new_v_ple_w_proj': 'new_v', 'new_v_ple_w_gate': 'new_v', 'new_v_ple_b_gate': 'new_v'}


def _forward(args):
    return _fwd_reference(*[args[k] for k in FWD_PARAMS])


def _output_shape():
    out = _jax.eval_shape(lambda: _forward(_fwd_setup_inputs(0)))
    return out.shape, out.dtype

N_MICROBATCH = 1
ADAM_LR = 0.001
ADAM_B1 = 0.9
ADAM_B2 = 0.999
ADAM_EPS = 1e-08
ADAM_WD = 0.01
ADAM_STEP = 10
PER_EXAMPLE_BATCH_AXIS = {'x': 0, 'p': 1, 'positions': 0, 'loss_target': 0}
SHARED_INPUTS = []
_WEIGHT_DTYPES = {'mla_w_in': _jnp.float32, 'mla_q_norm': _jnp.float32, 'mla_kv_norm': _jnp.float32, 'mla_w_uq': _jnp.float32, 'mla_w_uk': _jnp.float32, 'mla_w_uv': _jnp.float32, 'mla_w_o': _jnp.float32, 'gla_w_in': _jnp.float32, 'gla_w_a2': _jnp.float32, 'gla_b_a': _jnp.float32, 'gla_o_norm': _jnp.float32, 'gla_w_o': _jnp.float32, 'ln1_g': _jnp.float32, 'ln1_b': _jnp.float32, 'ln2_g': _jnp.float32, 'ln2_b': _jnp.float32, 'ffn_w_up': _jnp.float32, 'ffn_conv_w': _jnp.float32, 'ffn_conv_b': _jnp.float32, 'ffn_w_down': _jnp.float32, 'ple_w_proj': _jnp.float32, 'ple_w_gate': _jnp.float32, 'ple_b_gate': _jnp.float32}
MOMENT_SCALE = {'mla_w_in': 5.668225e-03, 'mla_q_norm': 4.062321e-03, 'mla_kv_norm': 8.767466e-03, 'mla_w_uq': 1.636895e-03, 'mla_w_uk': 1.650050e-03, 'mla_w_uv': 5.804049e-03, 'mla_w_o': 8.290558e-03, 'gla_w_in': 3.563553e-02, 'gla_w_a2': 5.486947e-03, 'gla_b_a': 1.875869e-02, 'gla_o_norm': 3.335195e-02, 'gla_w_o': 7.149119e-02, 'ln1_g': 3.170993e-01, 'ln1_b': 2.667177e+00, 'ln2_g': 1.649513e+01, 'ln2_b': 2.733467e+00, 'ffn_w_up': 1.944314e-02, 'ffn_conv_w': 1.944577e-02, 'ffn_conv_b': 7.126695e-02, 'ffn_w_down': 7.552458e-02, 'ple_w_proj': 2.107943e-01, 'ple_w_gate': 5.071354e-02, 'ple_b_gate': 1.630048e+00}


def _to_microbatches(a, axis):
    t = _jnp.moveaxis(a, axis, 0)
    t = t.reshape((N_MICROBATCH, t.shape[0] // N_MICROBATCH) + t.shape[1:])
    return _jnp.moveaxis(t, 1, axis + 1)


def setup_inputs(seed: int = 0) -> dict:
    inp = _fwd_setup_inputs(seed)
    key = _jax.random.fold_in(_jax.random.key(seed), 7919)
    shape, _ = _output_shape()
    out = dict(inp)
    out["loss_target"] = _jax.random.normal(_jax.random.fold_in(key, 0), shape, _jnp.float32)
    for i, name in enumerate(TWIN_WEIGHTS):
        w = inp[name].astype(_jnp.float32)
        if MOMENT_SCALE is None:
            s = _jnp.sqrt(_jnp.mean(_jnp.square(w)) + 1e-30)
        else:
            s = MOMENT_SCALE[name]
        km, kv = _jax.random.split(_jax.random.fold_in(key, i + 1))
        out[name] = w
        out["m_" + name] = s * _jax.random.normal(km, w.shape, _jnp.float32)
        out["v_" + name] = (s * s) * _jax.random.uniform(kv, w.shape, _jnp.float32, 0.5, 1.5)
    if N_MICROBATCH > 1:
        for name, axis in PER_EXAMPLE_BATCH_AXIS.items():
            out[name] = _to_microbatches(out[name], axis)
    return {'x': out['x'], 'p': out['p'], 'positions': out['positions'], 'mla_w_in': out['mla_w_in'], 'mla_q_norm': out['mla_q_norm'], 'mla_kv_norm': out['mla_kv_norm'], 'mla_w_uq': out['mla_w_uq'], 'mla_w_uk': out['mla_w_uk'], 'mla_w_uv': out['mla_w_uv'], 'mla_w_o': out['mla_w_o'], 'gla_w_in': out['gla_w_in'], 'gla_w_a2': out['gla_w_a2'], 'gla_b_a': out['gla_b_a'], 'gla_o_norm': out['gla_o_norm'], 'gla_w_o': out['gla_w_o'], 'ln1_g': out['ln1_g'], 'ln1_b': out['ln1_b'], 'ln2_g': out['ln2_g'], 'ln2_b': out['ln2_b'], 'ffn_w_up': out['ffn_w_up'], 'ffn_conv_w': out['ffn_conv_w'], 'ffn_conv_b': out['ffn_conv_b'], 'ffn_w_down': out['ffn_w_down'], 'ple_w_proj': out['ple_w_proj'], 'ple_w_gate': out['ple_w_gate'], 'ple_b_gate': out['ple_b_gate'], 'loss_target': out['loss_target'], 'm_mla_w_in': out['m_mla_w_in'], 'm_mla_q_norm': out['m_mla_q_norm'], 'm_mla_kv_norm': out['m_mla_kv_norm'], 'm_mla_w_uq': out['m_mla_w_uq'], 'm_mla_w_uk': out['m_mla_w_uk'], 'm_mla_w_uv': out['m_mla_w_uv'], 'm_mla_w_o': out['m_mla_w_o'], 'm_gla_w_in': out['m_gla_w_in'], 'm_gla_w_a2': out['m_gla_w_a2'], 'm_gla_b_a': out['m_gla_b_a'], 'm_gla_o_norm': out['m_gla_o_norm'], 'm_gla_w_o': out['m_gla_w_o'], 'm_ln1_g': out['m_ln1_g'], 'm_ln1_b': out['m_ln1_b'], 'm_ln2_g': out['m_ln2_g'], 'm_ln2_b': out['m_ln2_b'], 'm_ffn_w_up': out['m_ffn_w_up'], 'm_ffn_conv_w': out['m_ffn_conv_w'], 'm_ffn_conv_b': out['m_ffn_conv_b'], 'm_ffn_w_down': out['m_ffn_w_down'], 'm_ple_w_proj': out['m_ple_w_proj'], 'm_ple_w_gate': out['m_ple_w_gate'], 'm_ple_b_gate': out['m_ple_b_gate'], 'v_mla_w_in': out['v_mla_w_in'], 'v_mla_q_norm': out['v_mla_q_norm'], 'v_mla_kv_norm': out['v_mla_kv_norm'], 'v_mla_w_uq': out['v_mla_w_uq'], 'v_mla_w_uk': out['v_mla_w_uk'], 'v_mla_w_uv': out['v_mla_w_uv'], 'v_mla_w_o': out['v_mla_w_o'], 'v_gla_w_in': out['v_gla_w_in'], 'v_gla_w_a2': out['v_gla_w_a2'], 'v_gla_b_a': out['v_gla_b_a'], 'v_gla_o_norm': out['v_gla_o_norm'], 'v_gla_w_o': out['v_gla_w_o'], 'v_ln1_g': out['v_ln1_g'], 'v_ln1_b': out['v_ln1_b'], 'v_ln2_g': out['v_ln2_g'], 'v_ln2_b': out['v_ln2_b'], 'v_ffn_w_up': out['v_ffn_w_up'], 'v_ffn_conv_w': out['v_ffn_conv_w'], 'v_ffn_conv_b': out['v_ffn_conv_b'], 'v_ffn_w_down': out['v_ffn_w_down'], 'v_ple_w_proj': out['v_ple_w_proj'], 'v_ple_w_gate': out['v_ple_w_gate'], 'v_ple_b_gate': out['v_ple_b_gate']}


def _loss(weights, diff, rest, loss_target):
    with _jax.named_scope("forward"):
        args = {**rest, TWIN_DIFF_INPUT: diff, **{k: w.astype(_WEIGHT_DTYPES[k]) for k, w in weights.items()}}
        y = _forward(args)
    with _jax.named_scope("loss_head"):
        err = _jnp.square(y.astype(_jnp.float32) - loss_target)
        return 0.5 * _jnp.sum(_jnp.mean(err, axis=-1)) if err.ndim else 0.5 * err


def _adamw(w, g, m, v):
    m = ADAM_B1 * m + (1.0 - ADAM_B1) * g
    v = ADAM_B2 * v + (1.0 - ADAM_B2) * _jnp.square(g)
    m_hat = m / (1.0 - ADAM_B1 ** ADAM_STEP)
    v_hat = v / (1.0 - ADAM_B2 ** ADAM_STEP)
    delta = -ADAM_LR * (m_hat / (_jnp.sqrt(v_hat) + ADAM_EPS) + ADAM_WD * w)
    return delta, m, v


def reference(x, p, positions, mla_w_in, mla_q_norm, mla_kv_norm, mla_w_uq, mla_w_uk, mla_w_uv, mla_w_o, gla_w_in, gla_w_a2, gla_b_a, gla_o_norm, gla_w_o, ln1_g, ln1_b, ln2_g, ln2_b, ffn_w_up, ffn_conv_w, ffn_conv_b, ffn_w_down, ple_w_proj, ple_w_gate, ple_b_gate, loss_target, m_mla_w_in, m_mla_q_norm, m_mla_kv_norm, m_mla_w_uq, m_mla_w_uk, m_mla_w_uv, m_mla_w_o, m_gla_w_in, m_gla_w_a2, m_gla_b_a, m_gla_o_norm, m_gla_w_o, m_ln1_g, m_ln1_b, m_ln2_g, m_ln2_b, m_ffn_w_up, m_ffn_conv_w, m_ffn_conv_b, m_ffn_w_down, m_ple_w_proj, m_ple_w_gate, m_ple_b_gate, v_mla_w_in, v_mla_q_norm, v_mla_kv_norm, v_mla_w_uq, v_mla_w_uk, v_mla_w_uv, v_mla_w_o, v_gla_w_in, v_gla_w_a2, v_gla_b_a, v_gla_o_norm, v_gla_w_o, v_ln1_g, v_ln1_b, v_ln2_g, v_ln2_b, v_ffn_w_up, v_ffn_conv_w, v_ffn_conv_b, v_ffn_w_down, v_ple_w_proj, v_ple_w_gate, v_ple_b_gate):
    given = dict(x=x, p=p, positions=positions, mla_w_in=mla_w_in, mla_q_norm=mla_q_norm, mla_kv_norm=mla_kv_norm, mla_w_uq=mla_w_uq, mla_w_uk=mla_w_uk, mla_w_uv=mla_w_uv, mla_w_o=mla_w_o, gla_w_in=gla_w_in, gla_w_a2=gla_w_a2, gla_b_a=gla_b_a, gla_o_norm=gla_o_norm, gla_w_o=gla_w_o, ln1_g=ln1_g, ln1_b=ln1_b, ln2_g=ln2_g, ln2_b=ln2_b, ffn_w_up=ffn_w_up, ffn_conv_w=ffn_conv_w, ffn_conv_b=ffn_conv_b, ffn_w_down=ffn_w_down, ple_w_proj=ple_w_proj, ple_w_gate=ple_w_gate, ple_b_gate=ple_b_gate, loss_target=loss_target, m_mla_w_in=m_mla_w_in, m_mla_q_norm=m_mla_q_norm, m_mla_kv_norm=m_mla_kv_norm, m_mla_w_uq=m_mla_w_uq, m_mla_w_uk=m_mla_w_uk, m_mla_w_uv=m_mla_w_uv, m_mla_w_o=m_mla_w_o, m_gla_w_in=m_gla_w_in, m_gla_w_a2=m_gla_w_a2, m_gla_b_a=m_gla_b_a, m_gla_o_norm=m_gla_o_norm, m_gla_w_o=m_gla_w_o, m_ln1_g=m_ln1_g, m_ln1_b=m_ln1_b, m_ln2_g=m_ln2_g, m_ln2_b=m_ln2_b, m_ffn_w_up=m_ffn_w_up, m_ffn_conv_w=m_ffn_conv_w, m_ffn_conv_b=m_ffn_conv_b, m_ffn_w_down=m_ffn_w_down, m_ple_w_proj=m_ple_w_proj, m_ple_w_gate=m_ple_w_gate, m_ple_b_gate=m_ple_b_gate, v_mla_w_in=v_mla_w_in, v_mla_q_norm=v_mla_q_norm, v_mla_kv_norm=v_mla_kv_norm, v_mla_w_uq=v_mla_w_uq, v_mla_w_uk=v_mla_w_uk, v_mla_w_uv=v_mla_w_uv, v_mla_w_o=v_mla_w_o, v_gla_w_in=v_gla_w_in, v_gla_w_a2=v_gla_w_a2, v_gla_b_a=v_gla_b_a, v_gla_o_norm=v_gla_o_norm, v_gla_w_o=v_gla_w_o, v_ln1_g=v_ln1_g, v_ln1_b=v_ln1_b, v_ln2_g=v_ln2_g, v_ln2_b=v_ln2_b, v_ffn_w_up=v_ffn_w_up, v_ffn_conv_w=v_ffn_conv_w, v_ffn_conv_b=v_ffn_conv_b, v_ffn_w_down=v_ffn_w_down, v_ple_w_proj=v_ple_w_proj, v_ple_w_gate=v_ple_w_gate, v_ple_b_gate=v_ple_b_gate)
    weights = {n: given[n] for n in TWIN_WEIGHTS}
    shared = {n: given[n] for n in SHARED_INPUTS}
    per_example = {n: given[n] for n in ['x', 'p', 'positions']}
    grad_fn = _jax.value_and_grad(_loss, argnums=(0, 1))

    def one_microbatch(ex, loss_target):
        ex = dict(ex)
        diff = ex.pop(TWIN_DIFF_INPUT)
        return grad_fn(weights, diff, {**shared, **ex}, loss_target)

    if N_MICROBATCH == 1:
        loss, (grad_w, grad_x) = one_microbatch(per_example, given["loss_target"])
    else:
        def body(carry, xs):
            loss_sum, grad_sum = carry
            l_k, (gw_k, gx_k) = one_microbatch(xs[0], xs[1])
            with _jax.named_scope("update"):
                return (loss_sum + l_k, _jax.tree.map(_jnp.add, grad_sum, gw_k)), gx_k

        init = (_jnp.zeros((), _jnp.float32), _jax.tree.map(_jnp.zeros_like, weights))
        (loss, grad_w), grad_x = _jax.lax.scan(body, init, (per_example, given["loss_target"]))
    with _jax.named_scope("update"):
        delta_w, new_m, new_v = {}, {}, {}
        for n in TWIN_WEIGHTS:
            delta_w[n], new_m[n], new_v[n] = _adamw(weights[n], grad_w[n], given["m_" + n], given["v_" + n])
    return (loss, grad_x, *[grad_w[n] for n in TWIN_WEIGHTS], *[delta_w[n] for n in TWIN_WEIGHTS],
            *[new_m[n] for n in TWIN_WEIGHTS], *[new_v[n] for n in TWIN_WEIGHTS])
```

```python
import functools
import itertools

import jax
import jax.numpy as jnp
from jax import lax
from jax.experimental import pallas as pl
from jax.experimental.pallas import tpu as pltpu

F32 = jnp.float32
BF16 = jnp.bfloat16
MESH = pl.DeviceIdType.MESH
N_DEV = 8

EPS = 1e-5
NEG_INF = -1e30
CHUNK = 64
Q_BLOCK = 128
MLA_NOPE = 128
MLA_ROPE = 64
MLA_V = 128
MLA_QK_PAD = 256
ROPE_THETA = 10000.0
GLA_DK = 128
GLA_DV = 256
GLA_RANK = 16
GLA_TAU = 16.0
CONV_W = 3
ADAM_LR = 0.001
ADAM_B1 = 0.9
ADAM_B2 = 0.999
ADAM_EPS = 1e-08
ADAM_WD = 0.01
ADAM_STEP = 10

LANES = 128
PACK_COLS = 1024
VMEM_LIMIT = 48 * 1024 * 1024
MM_VMEM_BUDGET = 30 * 1024 * 1024

WEIGHTS = ['mla_w_in', 'mla_q_norm', 'mla_kv_norm', 'mla_w_uq', 'mla_w_uk', 'mla_w_uv', 'mla_w_o', 'gla_w_in',
           'gla_w_a2', 'gla_b_a', 'gla_o_norm', 'gla_w_o', 'ln1_g', 'ln1_b', 'ln2_g', 'ln2_b', 'ffn_w_up',
           'ffn_conv_w', 'ffn_conv_b', 'ffn_w_down', 'ple_w_proj', 'ple_w_gate', 'ple_b_gate']
SHARD_AXIS = {'mla_w_in': 1, 'mla_q_norm': None, 'mla_kv_norm': None, 'mla_w_uq': 2, 'mla_w_uk': 2, 'mla_w_uv': 2,
              'mla_w_o': 1, 'gla_w_in': 2, 'gla_w_a2': 2, 'gla_b_a': 1, 'gla_o_norm': 1, 'gla_w_o': 1,
              'ln1_g': None, 'ln1_b': None, 'ln2_g': None, 'ln2_b': None, 'ffn_w_up': 2, 'ffn_conv_w': 2,
              'ffn_conv_b': None, 'ffn_w_down': 1, 'ple_w_proj': 2, 'ple_w_gate': 1, 'ple_b_gate': None}
BIG = ['mla_w_in', 'mla_w_uq', 'mla_w_uk', 'mla_w_uv', 'mla_w_o', 'gla_w_in', 'gla_w_o', 'ffn_w_up', 'ffn_w_down',
       'ple_w_proj', 'ple_w_gate']
SMALL = ['gla_w_a2', 'gla_b_a', 'gla_o_norm', 'ffn_conv_w']
REPL = [n for n in WEIGHTS if SHARD_AXIS[n] is None]

_uid = [itertools.count()]


def _nm(base):
    return f"{base}_{next(_uid[0])}"


def _cp(sem=None):
    return pltpu.CompilerParams(dimension_semantics=sem, vmem_limit_bytes=VMEM_LIMIT)


def _divisor_tiles(n, cap):
    if n % LANES:
        return [n]
    out = [t for t in range(LANES, min(n, cap) + 1, LANES) if n % t == 0]
    return out or [n]


def _mm_tiles(M, N, K, abytes, bbytes, obytes):
    best = None
    for tm in _divisor_tiles(M, 1024):
        for tn in _divisor_tiles(N, 1536):
            for tk in _divisor_tiles(K, 2048):
                vmem = 2 * (tm * tk * abytes + tk * tn * bbytes + tm * tn * obytes) + tm * tn * 4
                if vmem > MM_VMEM_BUDGET:
                    continue
                key = (tm * tn * tk, tk)
                if best is None or key > best[0]:
                    best = (key, (tm, tn, tk))
    assert best is not None, (M, N, K)
    return best[1]


def _mm(a, b, mode, out_dtype, base="mm"):
    if mode == "nn":
        (M, K), N = a.shape, b.shape[1]
    elif mode == "nt":
        (M, K), N = a.shape, b.shape[0]
    else:
        (K, M), N = a.shape, b.shape[1]
    tm, tn, tk = _mm_tiles(M, N, K, a.dtype.itemsize, b.dtype.itemsize, jnp.dtype(out_dtype).itemsize)
    nk = K // tk
    if mode == "nn":
        a_spec = pl.BlockSpec((tm, tk), lambda i, j, k: (i, k))
        b_spec = pl.BlockSpec((tk, tn), lambda i, j, k: (k, j))
        dims = (((1,), (0,)), ((), ()))
    elif mode == "nt":
        a_spec = pl.BlockSpec((tm, tk), lambda i, j, k: (i, k))
        b_spec = pl.BlockSpec((tn, tk), lambda i, j, k: (j, k))
        dims = (((1,), (1,)), ((), ()))
    else:
        a_spec = pl.BlockSpec((tk, tm), lambda i, j, k: (k, i))
        b_spec = pl.BlockSpec((tk, tn), lambda i, j, k: (k, j))
        dims = (((0,), (0,)), ((), ()))

    def body(a_ref, b_ref, o_ref, acc_ref):
        part = lax.dot_general(a_ref[...].astype(BF16), b_ref[...].astype(BF16), dims, preferred_element_type=F32)
        if nk == 1:
            o_ref[...] = part.astype(o_ref.dtype)
        else:
            k = pl.program_id(2)

            @pl.when(k == 0)
            def _():
                acc_ref[...] = part

            @pl.when(k > 0)
            def _():
                acc_ref[...] += part

            @pl.when(k == nk - 1)
            def _():
                o_ref[...] = acc_ref[...].astype(o_ref.dtype)

    return pl.pallas_call(
        body, name=_nm(base), grid=(M // tm, N // tn, nk),
        out_shape=jax.ShapeDtypeStruct((M, N), out_dtype),
        in_specs=[a_spec, b_spec], out_specs=pl.BlockSpec((tm, tn), lambda i, j, k: (i, j)),
        scratch_shapes=[pltpu.VMEM((tm, tn) if nk > 1 else (8, LANES), F32)],
        compiler_params=_cp(("parallel", "parallel", "arbitrary")),
    )(a, b)


@functools.partial(jax.custom_vjp, nondiff_argnums=(2,))
def linear(a, w, out_dtype):
    return _mm(a, w, "nn", out_dtype, "lin_fwd")


def _linear_fwd(a, w, out_dtype):
    return _mm(a, w, "nn", out_dtype, "lin_fwd"), (a, w)


def _linear_bwd(out_dtype, res, dy):
    a, w = res
    return _mm(dy, w, "nt", a.dtype, "lin_dx"), _mm(a, dy, "tn", w.dtype, "lin_dw")


linear.defvjp(_linear_fwd, _linear_bwd)


def _row_tile(S, width):
    tr = 512 if width <= 1024 else 256
    return min(tr, S)


def _rw_fwd(f, rows, params, out_dtypes, base):
    S = rows[0][0].shape[0]
    tr = _row_tile(S, max(w for _, w, _ in rows))
    n_in = len(rows) + len(params)
    avals = [jax.ShapeDtypeStruct((tr, w), F32) for _, w, _ in rows] + [jax.ShapeDtypeStruct(p.shape, F32) for p in params]
    outs = jax.eval_shape(f, *avals)

    def body(*refs):
        vals = [r[...].astype(F32) for r in refs[:n_in]]
        for o_ref, r in zip(refs[n_in:], f(*vals)):
            o_ref[...] = r.astype(o_ref.dtype)

    in_specs = [pl.BlockSpec((tr, w), functools.partial(lambda i, cb: (i, cb), cb=cb)) for _, w, cb in rows]
    in_specs += [pl.BlockSpec(p.shape, lambda i: (0, 0)) for p in params]
    return pl.pallas_call(
        body, name=_nm(base), grid=(S // tr,),
        out_shape=tuple(jax.ShapeDtypeStruct((S, o.shape[1]), dt) for o, dt in zip(outs, out_dtypes)),
        in_specs=in_specs, out_specs=tuple(pl.BlockSpec((tr, o.shape[1]), lambda i: (i, 0)) for o in outs),
        compiler_params=_cp(("parallel",)),
    )(*[a for a, _, _ in rows], *params)


def _rw_bwd(f, rows, params, cts, row_grad_dtypes, base):
    S = rows[0][0].shape[0]
    tr = _row_tile(S, max(w for _, w, _ in rows))
    n_rows, n_par, n_ct = len(rows), len(params), len(cts)
    want = [k for k, dt in enumerate(row_grad_dtypes) if dt is not None]

    def body(*refs):
        in_refs = refs[:n_rows + n_par]
        ct_refs = refs[n_rows + n_par:n_rows + n_par + n_ct]
        out_refs = refs[n_rows + n_par + n_ct:]
        vals = [r[...].astype(F32) for r in in_refs]
        _, vjp_fn = jax.vjp(f, *vals)
        grads = vjp_fn(tuple(c[...].astype(F32) for c in ct_refs))
        for o_ref, k in zip(out_refs[:len(want)], want):
            o_ref[...] = grads[k].astype(o_ref.dtype)
        i = pl.program_id(0)
        for o_ref, g in zip(out_refs[len(want):], grads[n_rows:]):
            @pl.when(i == 0)
            def _(o_ref=o_ref, g=g):
                o_ref[...] = g

            @pl.when(i > 0)
            def _(o_ref=o_ref, g=g):
                o_ref[...] += g

    in_specs = [pl.BlockSpec((tr, w), functools.partial(lambda i, cb: (i, cb), cb=cb)) for _, w, cb in rows]
    in_specs += [pl.BlockSpec(p.shape, lambda i: (0, 0)) for p in params]
    in_specs += [pl.BlockSpec((tr, c.shape[1]), lambda i: (i, 0)) for c in cts]
    out_shape = [jax.ShapeDtypeStruct((S, rows[k][1]), row_grad_dtypes[k]) for k in want]
    out_specs = [pl.BlockSpec((tr, rows[k][1]), lambda i: (i, 0)) for k in want]
    out_shape += [jax.ShapeDtypeStruct(p.shape, F32) for p in params]
    out_specs += [pl.BlockSpec(p.shape, lambda i: (0, 0)) for p in params]
    res = pl.pallas_call(
        body, name=_nm(base), grid=(S // tr,), out_shape=tuple(out_shape),
        in_specs=in_specs, out_specs=tuple(out_specs), compiler_params=_cp(("arbitrary",)),
    )(*[a for a, _, _ in rows], *params, *cts)
    row_grads = [None] * n_rows
    for k, g in zip(want, res[:len(want)]):
        row_grads[k] = g
    return row_grads, list(res[len(want):])


def rw_op(f, base, n_rows, out_dtypes):
    @jax.custom_vjp
    def op(*args):
        return fwd(*args)[0]

    def split(args):
        rows = [(a, a.shape[1], 0) for a in args[:n_rows]]
        return rows, list(args[n_rows:])

    def fwd(*args):
        rows, params = split(args)
        return tuple(_rw_fwd(f, rows, params, out_dtypes, base + "_fwd")), args

    def bwd(args, cts):
        rows, params = split(args)
        rg, pg = _rw_bwd(f, rows, params, list(cts), [a.dtype for a, _, _ in rows], base + "_bwd")
        return tuple(rg) + tuple(g.astype(p.dtype) for g, p in zip(pg, params))

    op.defvjp(fwd, bwd)
    return op


def _ln_res_fn(alpha):
    def f(x, m, g, b):
        z = alpha * x + m
        mu = jnp.mean(z, -1, keepdims=True)
        zc = z - mu
        var = jnp.mean(zc * zc, -1, keepdims=True)
        return (zc * lax.rsqrt(var + EPS) * g + b,)
    return f


def _rms(x, g):
    return x * lax.rsqrt(jnp.mean(x * x, -1, keepdims=True) + EPS) * g


def _mla_norm_fn(q_lora, kv_lora):
    def f(h, qn, kvn):
        return (_rms(h[:, :q_lora], qn), _rms(h[:, q_lora:q_lora + kv_lora], kvn),
                h[:, q_lora + kv_lora:q_lora + kv_lora + LANES])
    return f


def _log_sigmoid(z):
    return jnp.minimum(z, 0.0) - jnp.log(1.0 + jnp.exp(-jnp.abs(z)))


def _gla_gate_fn(z, b):
    return (_log_sigmoid(z + b) / GLA_TAU,)


def _ple_fn(x, glog, pp, b):
    return (x + jax.nn.sigmoid(glog + b) * pp,)


def _gla_out_fn(heads):
    def f(o, r, g):
        parts = []
        for h in range(heads):
            oh = o[:, h * GLA_DV:(h + 1) * GLA_DV]
            mu = jnp.mean(oh, -1, keepdims=True)
            oc = oh - mu
            var = jnp.mean(oc * oc, -1, keepdims=True)
            parts.append(oc * lax.rsqrt(var + EPS) * g[:, h * GLA_DV:(h + 1) * GLA_DV])
        return (jnp.concatenate(parts, axis=1) * (r * jax.nn.sigmoid(r)),)
    return f


def _rope_call(x, tabs, roped, out_dtype, base):
    S, C = x.shape
    nb = C // LANES
    tr = min(512 if C <= 1024 else 256, S)

    def body(x_ref, a_ref, b1_ref, b2_ref, o_ref):
        a, b1, b2 = a_ref[...], b1_ref[...], b2_ref[...]
        for blk in range(nb):
            v = x_ref[:, blk * LANES:(blk + 1) * LANES].astype(F32)
            if roped(blk):
                v = v * a + pltpu.roll(v, 96, 1) * b1 + pltpu.roll(v, 32, 1) * b2
            o_ref[:, blk * LANES:(blk + 1) * LANES] = v.astype(o_ref.dtype)

    row = lambda w: pl.BlockSpec((tr, w), lambda i: (i, 0))
    return pl.pallas_call(
        body, name=_nm(base), grid=(S // tr,), out_shape=jax.ShapeDtypeStruct((S, C), out_dtype),
        in_specs=[row(C), row(LANES), row(LANES), row(LANES)], out_specs=row(C),
        compiler_params=_cp(("parallel",)),
    )(x, *tabs)


def make_rope(tabs_fwd, tabs_bwd, roped, out_dtype):
    @jax.custom_vjp
    def rope(x):
        return _rope_call(x, tabs_fwd, roped, out_dtype, "rope_fwd")

    def fwd(x):
        return _rope_call(x, tabs_fwd, roped, out_dtype, "rope_fwd"), jnp.zeros((), x.dtype)

    def bwd(tok, dy):
        return (_rope_call(dy, tabs_bwd, roped, tok.dtype, "rope_bwd"),)

    rope.defvjp(fwd, bwd)
    return rope


def _attn_tile(S):
    return min(512, S)


def _attn_mask(cidq_ref, cidk_ref, i, j, t):
    qrow = i * t + lax.broadcasted_iota(jnp.int32, (t, 1), 0)
    kcol = j * t + lax.broadcasted_iota(jnp.int32, (1, t), 1)
    qlim = (qrow // Q_BLOCK + 1) * Q_BLOCK
    return (cidk_ref[...] <= cidq_ref[...]) & (kcol < qlim)


def _attn_scores(q_ref, kn_ref, kr_ref, cidq_ref, cidk_ref, i, j, t, scale):
    k = jnp.concatenate([kn_ref[...], kr_ref[...]], axis=1)
    s = lax.dot_general(q_ref[...], k, (((1,), (1,)), ((), ())), preferred_element_type=F32) * scale
    return jnp.where(_attn_mask(cidq_ref, cidk_ref, i, j, t), s, NEG_INF), k


def _attn_fwd_call(q, kn, v, kr, cidq, cidk, heads):
    S = q.shape[0]
    t = _attn_tile(S)
    n = S // t
    scale = (MLA_NOPE + MLA_ROPE) ** -0.5

    def body(q_ref, kn_ref, v_ref, kr_ref, cidq_ref, cidk_ref, o_ref, lse_ref, m_ref, l_ref, acc_ref):
        i, j = pl.program_id(1), pl.program_id(2)

        @pl.when(j == 0)
        def _():
            m_ref[...] = jnp.full(m_ref.shape, NEG_INF, F32)
            l_ref[...] = jnp.zeros(l_ref.shape, F32)
            acc_ref[...] = jnp.zeros(acc_ref.shape, F32)

        @pl.when(j <= i)
        def _():
            s, _ = _attn_scores(q_ref, kn_ref, kr_ref, cidq_ref, cidk_ref, i, j, t, scale)
            m_prev = m_ref[...]
            m_new = jnp.maximum(m_prev, jnp.max(s, axis=1, keepdims=True))
            alpha = jnp.exp(m_prev - m_new)
            p = jnp.exp(s - m_new[:, :1])
            l_ref[...] = alpha * l_ref[...] + jnp.sum(p, axis=1, keepdims=True)
            acc_ref[...] = alpha * acc_ref[...] + jnp.dot(p.astype(BF16), v_ref[...], preferred_element_type=F32)
            m_ref[...] = m_new

        @pl.when(j == i)
        def _():
            o_ref[...] = (acc_ref[...] / l_ref[...]).astype(o_ref.dtype)
            lse_ref[...] = m_ref[...] + jnp.log(l_ref[...])

    kmap = lambda h, i, j: (jnp.minimum(j, i), h)
    return pl.pallas_call(
        body, name=_nm("attn_fwd"), grid=(heads, n, n),
        out_shape=(jax.ShapeDtypeStruct((S, heads * MLA_V), BF16), jax.ShapeDtypeStruct((S, heads * LANES), F32)),
        in_specs=[pl.BlockSpec((t, MLA_QK_PAD), lambda h, i, j: (i, h)),
                  pl.BlockSpec((t, MLA_NOPE), kmap),
                  pl.BlockSpec((t, MLA_V), kmap),
                  pl.BlockSpec((t, LANES), lambda h, i, j: (jnp.minimum(j, i), 0)),
                  pl.BlockSpec((t, 1), lambda h, i, j: (i, 0)),
                  pl.BlockSpec((1, t), lambda h, i, j: (0, jnp.minimum(j, i)))],
        out_specs=(pl.BlockSpec((t, MLA_V), lambda h, i, j: (i, h)), pl.BlockSpec((t, LANES), lambda h, i, j: (i, h))),
        scratch_shapes=[pltpu.VMEM((t, LANES), F32), pltpu.VMEM((t, LANES), F32), pltpu.VMEM((t, MLA_V), F32)],
        compiler_params=_cp(("parallel", "parallel", "arbitrary")),
    )(q, kn, v, kr, cidq, cidk)


def _attn_probs(q_ref, kn_ref, kr_ref, v_ref, do_ref, o_ref, lse_ref, cidq_ref, cidk_ref, i, j, t, scale):
    s, k = _attn_scores(q_ref, kn_ref, kr_ref, cidq_ref, cidk_ref, i, j, t, scale)
    p = jnp.exp(s - lse_ref[:, :1])
    do = do_ref[...]
    dp = lax.dot_general(do, v_ref[...], (((1,), (1,)), ((), ())), preferred_element_type=F32)
    dsum = jnp.sum(do.astype(F32) * o_ref[...].astype(F32), axis=1, keepdims=True)
    ds = p * (dp - dsum) * scale
    return p, ds, k, do


def _attn_dq_call(q, kn, v, kr, cidq, cidk, o, lse, do, heads):
    S = q.shape[0]
    t = _attn_tile(S)
    n = S // t
    scale = (MLA_NOPE + MLA_ROPE) ** -0.5

    def body(q_ref, kn_ref, v_ref, kr_ref, cidq_ref, cidk_ref, o_ref, lse_ref, do_ref, dq_ref, acc_ref):
        i, j = pl.program_id(1), pl.program_id(2)

        @pl.when(j == 0)
        def _():
            acc_ref[...] = jnp.zeros(acc_ref.shape, F32)

        @pl.when(j <= i)
        def _():
            _, ds, k, _ = _attn_probs(q_ref, kn_ref, kr_ref, v_ref, do_ref, o_ref, lse_ref, cidq_ref, cidk_ref, i, j, t, scale)
            acc_ref[...] += jnp.dot(ds.astype(BF16), k, preferred_element_type=F32)

        @pl.when(j == i)
        def _():
            dq_ref[...] = acc_ref[...].astype(dq_ref.dtype)

    kmap = lambda h, i, j: (jnp.minimum(j, i), h)
    qmap = lambda h, i, j: (i, h)
    return pl.pallas_call(
        body, name=_nm("attn_dq"), grid=(heads, n, n),
        out_shape=jax.ShapeDtypeStruct((S, heads * MLA_QK_PAD), BF16),
        in_specs=[pl.BlockSpec((t, MLA_QK_PAD), qmap), pl.BlockSpec((t, MLA_NOPE), kmap), pl.BlockSpec((t, MLA_V), kmap),
                  pl.BlockSpec((t, LANES), lambda h, i, j: (jnp.minimum(j, i), 0)),
                  pl.BlockSpec((t, 1), lambda h, i, j: (i, 0)),
                  pl.BlockSpec((1, t), lambda h, i, j: (0, jnp.minimum(j, i))),
                  pl.BlockSpec((t, MLA_V), qmap), pl.BlockSpec((t, LANES), qmap), pl.BlockSpec((t, MLA_V), qmap)],
        out_specs=pl.BlockSpec((t, MLA_QK_PAD), qmap),
        scratch_shapes=[pltpu.VMEM((t, MLA_QK_PAD), F32)],
        compiler_params=_cp(("parallel", "parallel", "arbitrary")),
    )(q, kn, v, kr, cidq, cidk, o, lse, do)


def _attn_dkv_call(q, kn, v, kr, cidq, cidk, o, lse, do, heads):
    S = q.shape[0]
    t = _attn_tile(S)
    n = S // t
    scale = (MLA_NOPE + MLA_ROPE) ** -0.5

    def body(q_ref, kn_ref, v_ref, kr_ref, cidq_ref, cidk_ref, o_ref, lse_ref, do_ref,
             dkn_ref, dv_ref, dkr_ref, dk_acc, dv_acc):
        j, h, i = pl.program_id(0), pl.program_id(1), pl.program_id(2)

        @pl.when((h == 0) & (i == 0))
        def _():
            dkr_ref[...] = jnp.zeros(dkr_ref.shape, F32)

        @pl.when(i == 0)
        def _():
            dk_acc[...] = jnp.zeros(dk_acc.shape, F32)
            dv_acc[...] = jnp.zeros(dv_acc.shape, F32)

        @pl.when(i >= j)
        def _():
            p, ds, _, do = _attn_probs(q_ref, kn_ref, kr_ref, v_ref, do_ref, o_ref, lse_ref, cidq_ref, cidk_ref, i, j, t, scale)
            tn_dims = (((0,), (0,)), ((), ()))
            dv_acc[...] += lax.dot_general(p.astype(BF16), do, tn_dims, preferred_element_type=F32)
            dk_acc[...] += lax.dot_general(ds.astype(BF16), q_ref[...], tn_dims, preferred_element_type=F32)

        @pl.when(i == n - 1)
        def _():
            dkn_ref[...] = dk_acc[:, :MLA_NOPE].astype(dkn_ref.dtype)
            dv_ref[...] = dv_acc[...].astype(dv_ref.dtype)
            dkr_ref[...] += dk_acc[:, MLA_NOPE:]

    qmap = lambda j, h, i: (jnp.maximum(i, j), h)
    kmap = lambda j, h, i: (j, h)
    return pl.pallas_call(
        body, name=_nm("attn_dkv"), grid=(n, heads, n),
        out_shape=(jax.ShapeDtypeStruct((S, heads * MLA_NOPE), BF16), jax.ShapeDtypeStruct((S, heads * MLA_V), BF16),
                   jax.ShapeDtypeStruct((S, LANES), F32)),
        in_specs=[pl.BlockSpec((t, MLA_QK_PAD), qmap), pl.BlockSpec((t, MLA_NOPE), kmap), pl.BlockSpec((t, MLA_V), kmap),
                  pl.BlockSpec((t, LANES), lambda j, h, i: (j, 0)),
                  pl.BlockSpec((t, 1), lambda j, h, i: (jnp.maximum(i, j), 0)),
                  pl.BlockSpec((1, t), lambda j, h, i: (0, j)),
                  pl.BlockSpec((t, MLA_V), qmap), pl.BlockSpec((t, LANES), qmap), pl.BlockSpec((t, MLA_V), qmap)],
        out_specs=(pl.BlockSpec((t, MLA_NOPE), kmap), pl.BlockSpec((t, MLA_V), kmap),
                   pl.BlockSpec((t, LANES), lambda j, h, i: (j, 0))),
        scratch_shapes=[pltpu.VMEM((t, MLA_QK_PAD), F32), pltpu.VMEM((t, MLA_V), F32)],
        compiler_params=_cp(("parallel", "arbitrary", "arbitrary")),
    )(q, kn, v, kr, cidq, cidk, o, lse, do)


def make_attention(cidq, cidk, heads):
    @jax.custom_vjp
    def attn(q, kn, v, kr):
        return _attn_fwd_call(q, kn, v, kr, cidq, cidk, heads)[0]

    def fwd(q, kn, v, kr):
        o, lse = _attn_fwd_call(q, kn, v, kr, cidq, cidk, heads)
        return o, (q, kn, v, kr, o, lse)

    def bwd(res, do):
        q, kn, v, kr, o, lse = res
        dq = _attn_dq_call(q, kn, v, kr, cidq, cidk, o, lse, do, heads)
        dkn, dv, dkr = _attn_dkv_call(q, kn, v, kr, cidq, cidk, o, lse, do, heads)
        return dq, dkn, dv, dkr.astype(kr.dtype)

    attn.defvjp(fwd, bwd)
    return attn


GLA_ROWS = 256


def _tri(lower):
    r = lax.broadcasted_iota(jnp.int32, (CHUNK, CHUNK), 0)
    c = lax.broadcasted_iota(jnp.int32, (CHUNK, CHUNK), 1)
    return jnp.where((c <= r) if lower else (c >= r), 1.0, 0.0).astype(F32)


def _gla_chunk(q_ref, k_ref, v_ref, la_ref, sl):
    la = la_ref[sl, :]
    cum = jnp.dot(_tri(True), la, preferred_element_type=F32, precision=lax.Precision.HIGHEST)
    tot = cum[CHUNK - 1:CHUNK, :]
    e = jnp.exp(tot - cum)
    k = k_ref[sl, :]
    kdec = k * e
    v = v_ref[sl, :]
    upd_t = lax.dot_general(v.astype(BF16), kdec.astype(BF16), (((0,), (0,)), ((), ())), preferred_element_type=F32)
    qs = (q_ref[sl, :] * (GLA_DK ** -0.5)).astype(BF16)
    return e, k, kdec, v, upd_t, jnp.exp(tot), qs


def _gla_specs(heads, rows_map):
    return [pl.BlockSpec((GLA_ROWS, GLA_DK), lambda h, b: (rows_map(b), h)),
            pl.BlockSpec((GLA_ROWS, GLA_DK), lambda h, b: (rows_map(b), heads + h)),
            pl.BlockSpec((GLA_ROWS, GLA_DV), lambda h, b: (rows_map(b), heads + h)),
            pl.BlockSpec((GLA_ROWS, GLA_DK), lambda h, b: (rows_map(b), h))]


def _gla_fwd_call(hm, la, heads):
    S = hm.shape[0]
    assert S % GLA_ROWS == 0
    nb, cpb = S // GLA_ROWS, GLA_ROWS // CHUNK

    def body(q_ref, k_ref, v_ref, la_ref, o_ref, sp_ref, st_ref):
        @pl.when(pl.program_id(1) == 0)
        def _():
            st_ref[...] = jnp.zeros(st_ref.shape, F32)

        for c in range(cpb):
            sl = slice(c * CHUNK, (c + 1) * CHUNK)
            _, _, _, _, upd_t, decay, qs = _gla_chunk(q_ref, k_ref, v_ref, la_ref, sl)
            state = st_ref[...]
            sp_ref[0, c] = state
            state = state * decay + upd_t
            st_ref[...] = state
            o_ref[sl, :] = lax.dot_general(qs, state.astype(BF16), (((1,), (1,)), ((), ())), preferred_element_type=F32)

    return pl.pallas_call(
        body, name=_nm("gla_fwd"), grid=(heads, nb),
        out_shape=(jax.ShapeDtypeStruct((S, heads * GLA_DV), F32),
                   jax.ShapeDtypeStruct((heads, S // CHUNK, GLA_DV, GLA_DK), F32)),
        in_specs=_gla_specs(heads, lambda b: b),
        out_specs=(pl.BlockSpec((GLA_ROWS, GLA_DV), lambda h, b: (b, h)),
                   pl.BlockSpec((1, cpb, GLA_DV, GLA_DK), lambda h, b: (h, b, 0, 0))),
        scratch_shapes=[pltpu.VMEM((GLA_DV, GLA_DK), F32)],
        compiler_params=_cp(("parallel", "arbitrary")),
    )(hm, hm, hm, la)


def _gla_bwd_call(hm, la, sprev, do, heads):
    S = hm.shape[0]
    nb, cpb = S // GLA_ROWS, GLA_ROWS // CHUNK
    scale = GLA_DK ** -0.5

    def body(q_ref, k_ref, v_ref, la_ref, sp_ref, do_ref, dq_ref, dk_ref, dv_ref, dla_ref, carry_ref):
        @pl.when(pl.program_id(1) == 0)
        def _():
            carry_ref[...] = jnp.zeros(carry_ref.shape, F32)

        for c in reversed(range(cpb)):
            sl = slice(c * CHUNK, (c + 1) * CHUNK)
            e, k, kdec, v, upd_t, decay, qs = _gla_chunk(q_ref, k_ref, v_ref, la_ref, sl)
            sp = sp_ref[0, c]
            s_n = sp * decay + upd_t
            dob = do_ref[sl, :].astype(BF16)
            g = carry_ref[...] + lax.dot_general(dob, qs, (((0,), (0,)), ((), ())), preferred_element_type=F32)
            gb = g.astype(BF16)
            dq_ref[sl, :] = jnp.dot(dob, s_n.astype(BF16), preferred_element_type=F32) * scale
            ddecay = jnp.sum(g * sp, axis=0, keepdims=True)
            dkdec = jnp.dot(v.astype(BF16), gb, preferred_element_type=F32)
            dv_ref[sl, :] = lax.dot_general(kdec.astype(BF16), gb, (((1,), (1,)), ((), ())), preferred_element_type=F32)
            dk_ref[sl, :] = dkdec * e
            w = dkdec * k * e
            dtot = jnp.sum(w, axis=0, keepdims=True) + ddecay * decay
            last = lax.broadcasted_iota(jnp.int32, (CHUNK, 1), 0) == CHUNK - 1
            dcum = jnp.where(last, dtot - w, -w)
            dla_ref[sl, :] = jnp.dot(_tri(False), dcum, preferred_element_type=F32, precision=lax.Precision.HIGHEST)
            carry_ref[...] = g * decay

    rev = lambda b: nb - 1 - b
    return pl.pallas_call(
        body, name=_nm("gla_bwd"), grid=(heads, nb),
        out_shape=(jax.ShapeDtypeStruct((S, heads * GLA_DK), F32), jax.ShapeDtypeStruct((S, heads * GLA_DK), F32),
                   jax.ShapeDtypeStruct((S, heads * GLA_DV), F32), jax.ShapeDtypeStruct((S, heads * GLA_DK), F32)),
        in_specs=_gla_specs(heads, rev) + [
            pl.BlockSpec((1, cpb, GLA_DV, GLA_DK), lambda h, b: (h, rev(b), 0, 0)),
            pl.BlockSpec((GLA_ROWS, GLA_DV), lambda h, b: (rev(b), h))],
        out_specs=(pl.BlockSpec((GLA_ROWS, GLA_DK), lambda h, b: (rev(b), h)),
                   pl.BlockSpec((GLA_ROWS, GLA_DK), lambda h, b: (rev(b), h)),
                   pl.BlockSpec((GLA_ROWS, GLA_DV), lambda h, b: (rev(b), h)),
                   pl.BlockSpec((GLA_ROWS, GLA_DK), lambda h, b: (rev(b), h))),
        scratch_shapes=[pltpu.VMEM((GLA_DV, GLA_DK), F32)],
        compiler_params=_cp(("parallel", "arbitrary")),
    )(hm, hm, hm, la, sprev, do)


@functools.partial(jax.custom_vjp, nondiff_argnums=(3,))
def gla_core(hm, la, o_norm, heads):
    return _gla_core_fwd(hm, la, o_norm, heads)[0]


def _gla_core_fwd(hm, la, o_norm, heads):
    o, sprev = _gla_fwd_call(hm, la, heads)
    vd = heads * GLA_DV
    rows = [(o, vd, 0), (hm, vd, 2 * heads * GLA_DK // vd + 1)]
    (y,) = _rw_fwd(_gla_out_fn(heads), rows, [o_norm], [BF16], "gla_out_fwd")
    return y, (hm, la, o_norm, o, sprev)


def _gla_core_bwd(heads, res, dy):
    hm, la, o_norm, o, sprev = res
    vd = heads * GLA_DV
    rows = [(o, vd, 0), (hm, vd, 2 * heads * GLA_DK // vd + 1)]
    (do, dr), (dg,) = _rw_bwd(_gla_out_fn(heads), rows, [o_norm], [dy], [F32, F32], "gla_out_bwd")
    dq, dk, dv, dla = _gla_bwd_call(hm, la, sprev, do, heads)
    return jnp.concatenate([dq, dk, dv, dr], axis=1), dla, dg


gla_core.defvjp(_gla_core_fwd, _gla_core_bwd)


CONV_COLS = 256
HALO = 8


def _conv_rows(S):
    return min(512, S)


def _conv_taps(main_ref, halo_ref, i):
    R = main_ref.shape[0]
    prev = jnp.where(i > 0, halo_ref[...], 0.0)
    full = jnp.concatenate([prev, main_ref[...]], axis=0)
    return full[HALO:], pltpu.roll(full, 1, 0)[HALO:], pltpu.roll(full, 2, 0)[HALO:]


def _conv_apply(taps, w_ref, b_ref):
    x0, x1, x2 = taps
    return x2 * w_ref[0:1, :] + x1 * w_ref[1:2, :] + x0 * w_ref[2:3, :] + b_ref[...]


def _gelu_gate(uc, gc):
    return uc * jax.nn.gelu(gc)


def _conv_in_specs(R, nj):
    hpr = R // HALO
    main = lambda off: pl.BlockSpec((R, CONV_COLS), lambda j, i: (i, j + off))
    halo = lambda off: pl.BlockSpec((HALO, CONV_COLS), lambda j, i: (jnp.maximum(i * hpr - 1, 0), j + off))
    par = lambda rows, off: pl.BlockSpec((rows, CONV_COLS), lambda j, i: (0, j + off))
    return [main(0), halo(0), main(0), halo(0), par(CONV_W, 0), par(CONV_W, nj), par(1, 0), par(1, nj)]


def _conv_fwd_call(hu, hg, cw, cb):
    S, dff = hu.shape
    R, nj = _conv_rows(S), dff // CONV_COLS

    def body(u_ref, uh_ref, g_ref, gh_ref, wu_ref, wg_ref, bu_ref, bg_ref, a_ref):
        i = pl.program_id(1)
        uc = _conv_apply(_conv_taps(u_ref, uh_ref, i), wu_ref, bu_ref)
        gc = _conv_apply(_conv_taps(g_ref, gh_ref, i), wg_ref, bg_ref)
        a_ref[...] = _gelu_gate(uc, gc).astype(a_ref.dtype)

    return pl.pallas_call(
        body, name=_nm("conv_fwd"), grid=(nj, S // R), out_shape=jax.ShapeDtypeStruct((S, dff), BF16),
        in_specs=_conv_in_specs(R, nj), out_specs=pl.BlockSpec((R, CONV_COLS), lambda j, i: (i, j)),
        compiler_params=_cp(("parallel", "parallel")),
    )(hu, hu, hg, hg, cw, cw, cb, cb)


def _conv_bwd_gate_call(hu, hg, cw, cb, da):
    S, dff = hu.shape
    R, nj = _conv_rows(S), dff // CONV_COLS

    def body(u_ref, uh_ref, g_ref, gh_ref, wu_ref, wg_ref, bu_ref, bg_ref, da_ref,
             du_ref, dg_ref, dwu_ref, dwg_ref, dbu_ref, dbg_ref):
        i = pl.program_id(1)
        ut, gt = _conv_taps(u_ref, uh_ref, i), _conv_taps(g_ref, gh_ref, i)
        uc, gc = _conv_apply(ut, wu_ref, bu_ref), _conv_apply(gt, wg_ref, bg_ref)
        _, vjp_fn = jax.vjp(_gelu_gate, uc, gc)
        du, dg = vjp_fn(da_ref[...].astype(F32))
        du_ref[...] = du
        dg_ref[...] = dg

        @pl.when(i == 0)
        def _():
            for r in (dwu_ref, dwg_ref, dbu_ref, dbg_ref):
                r[...] = jnp.zeros(r.shape, F32)

        for d, taps, dw_ref, db_ref in ((du, ut, dwu_ref, dbu_ref), (dg, gt, dwg_ref, dbg_ref)):
            x0, x1, x2 = taps
            dw_ref[0:1, :] += jnp.sum(d * x2, axis=0, keepdims=True)
            dw_ref[1:2, :] += jnp.sum(d * x1, axis=0, keepdims=True)
            dw_ref[2:3, :] += jnp.sum(d * x0, axis=0, keepdims=True)
            db_ref[...] += jnp.sum(d, axis=0, keepdims=True)

    tile = pl.BlockSpec((R, CONV_COLS), lambda j, i: (i, j))
    par = lambda rows: pl.BlockSpec((rows, CONV_COLS), lambda j, i: (0, j))
    return pl.pallas_call(
        body, name=_nm("conv_bwd_gate"), grid=(nj, S // R),
        out_shape=(jax.ShapeDtypeStruct((S, dff), F32), jax.ShapeDtypeStruct((S, dff), F32),
                   jax.ShapeDtypeStruct((CONV_W, dff), F32), jax.ShapeDtypeStruct((CONV_W, dff), F32),
                   jax.ShapeDtypeStruct((1, dff), F32), jax.ShapeDtypeStruct((1, dff), F32)),
        in_specs=_conv_in_specs(R, nj) + [tile],
        out_specs=(tile, tile, par(CONV_W), par(CONV_W), par(1), par(1)),
        compiler_params=_cp(("parallel", "arbitrary")),
    )(hu, hu, hg, hg, cw, cw, cb, cb, da)


def _conv_bwd_shift_call(dc, cw, col_off, out_dtype):
    S, dff = dc.shape
    R, nj = _conv_rows(S), dff // CONV_COLS
    hpr, last = R // HALO, S // HALO - 1
    ni = S // R

    def body(d_ref, nx_ref, w_ref, o_ref):
        i = pl.program_id(1)
        nxt = jnp.where(i < ni - 1, nx_ref[...], 0.0)
        full = jnp.concatenate([d_ref[...], nxt], axis=0)
        n = R + HALO
        y1, y2 = pltpu.roll(full, n - 1, 0)[:R], pltpu.roll(full, n - 2, 0)[:R]
        o_ref[...] = (full[:R] * w_ref[2:3, :] + y1 * w_ref[1:2, :] + y2 * w_ref[0:1, :]).astype(o_ref.dtype)

    return pl.pallas_call(
        body, name=_nm("conv_bwd_shift"), grid=(nj, ni), out_shape=jax.ShapeDtypeStruct((S, dff), out_dtype),
        in_specs=[pl.BlockSpec((R, CONV_COLS), lambda j, i: (i, j)),
                  pl.BlockSpec((HALO, CONV_COLS), lambda j, i: (jnp.minimum((i + 1) * hpr, last), j)),
                  pl.BlockSpec((CONV_W, CONV_COLS), lambda j, i: (0, j + col_off))],
        out_specs=pl.BlockSpec((R, CONV_COLS), lambda j, i: (i, j)),
        compiler_params=_cp(("parallel", "parallel")),
    )(dc, dc, cw)


@jax.custom_vjp
def conv_gate(hu, hg, cw, cb):
    return _conv_fwd_call(hu, hg, cw, cb)


def _conv_gate_fwd(hu, hg, cw, cb):
    return _conv_fwd_call(hu, hg, cw, cb), (hu, hg, cw, cb)


def _conv_gate_bwd(res, da):
    hu, hg, cw, cb = res
    nj = hu.shape[1] // CONV_COLS
    du, dg, dwu, dwg, dbu, dbg = _conv_bwd_gate_call(hu, hg, cw, cb, da)
    dhu = _conv_bwd_shift_call(du, cw, 0, hu.dtype)
    dhg = _conv_bwd_shift_call(dg, cw, nj, hg.dtype)
    return dhu, dhg, jnp.concatenate([dwu, dwg], axis=1), jnp.concatenate([dbu, dbg], axis=1)


conv_gate.defvjp(_conv_gate_fwd, _conv_gate_bwd)


def _loss_call(y, target):
    S, D = y.shape
    tr = _row_tile(S, D)

    def body(y_ref, t_ref, sq_ref, dy_ref):
        diff = y_ref[...] - t_ref[...]
        dy_ref[...] = diff * (1.0 / D)
        part = jnp.sum(diff * diff, axis=0, keepdims=True)
        i = pl.program_id(0)

        @pl.when(i == 0)
        def _():
            sq_ref[...] = part

        @pl.when(i > 0)
        def _():
            sq_ref[...] += part

    row = pl.BlockSpec((tr, D), lambda i: (i, 0))
    return pl.pallas_call(
        body, name=_nm("loss"), grid=(S // tr,),
        out_shape=(jax.ShapeDtypeStruct((1, D), F32), jax.ShapeDtypeStruct((S, D), F32)),
        in_specs=[row, row], out_specs=(pl.BlockSpec((1, D), lambda i: (0, 0)), row),
        compiler_params=_cp(("arbitrary",)),
    )(y, target)


def _adamw_call(w, g, m, v):
    shape = w.shape
    w2, g2, m2, v2 = (a.reshape(-1, shape[-1]) for a in (w, g, m, v))
    rows, cols = w2.shape
    tr = rows
    for cand in (512, 256, 128, 64, 32, 16, 8):
        if rows % cand == 0:
            tr = cand
            break

    def body(w_ref, g_ref, m_ref, v_ref, d_ref, nm_ref, nv_ref):
        g_ = g_ref[...]
        m_ = ADAM_B1 * m_ref[...] + (1.0 - ADAM_B1) * g_
        v_ = ADAM_B2 * v_ref[...] + (1.0 - ADAM_B2) * (g_ * g_)
        m_hat = m_ / (1.0 - ADAM_B1 ** ADAM_STEP)
        v_hat = v_ / (1.0 - ADAM_B2 ** ADAM_STEP)
        d_ref[...] = -ADAM_LR * (m_hat / (jnp.sqrt(v_hat) + ADAM_EPS) + ADAM_WD * w_ref[...])
        nm_ref[...] = m_
        nv_ref[...] = v_

    blk = pl.BlockSpec((tr, cols), lambda i: (i, 0))
    outs = pl.pallas_call(
        body, name=_nm("adamw"), grid=(rows // tr,),
        out_shape=tuple(jax.ShapeDtypeStruct((rows, cols), F32) for _ in range(3)),
        in_specs=[blk] * 4, out_specs=(blk,) * 3, compiler_params=_cp(("parallel",)),
    )(w2, g2, m2, v2)
    return tuple(o.reshape(shape) for o in outs)


ANY = pl.BlockSpec(memory_space=pl.ANY)


def _place():
    return lax.axis_index("x"), lax.axis_index("y"), lax.axis_index("c")


def all_gather(shard):
    R, C = shard.shape

    def body(x_ref, out_ref, send_sems, recv_sems, local_sem):
        x, y, c = _place()
        me, sibling = (x, y, c), (x, y, 1 - c)
        chips = [(1 - x, y), (x, 1 - y), (1 - x, 1 - y)]

        def slot(px, py, pc):
            return out_ref.at[4 * px + 2 * py + pc]

        def copy(k, block, to, src=None):
            return pltpu.make_async_remote_copy(
                src_ref=slot(*block) if src is None else src, dst_ref=slot(*block),
                send_sem=send_sems.at[k], recv_sem=recv_sems.at[k], device_id=to, device_id_type=MESH)

        mine = pltpu.make_async_copy(x_ref, slot(*me), local_sem)
        mine.start()
        first = [copy(0, me, sibling, src=x_ref)]
        first += [copy(1 + j, me, (*chip, c), src=x_ref) for j, chip in enumerate(chips)]
        for cp in first:
            cp.start()
        passed = [copy(4 + j, (*chip, c), sibling) for j, chip in enumerate(chips)]
        for j, chip in enumerate(chips):
            copy(1 + j, (*chip, c), me).wait_recv()
            passed[j].start()
        copy(0, sibling, me).wait_recv()
        for j, chip in enumerate(chips):
            copy(4 + j, (*chip, 1 - c), me).wait_recv()
        for cp in first + passed:
            cp.wait_send()
        mine.wait()

    return pl.pallas_call(
        body, name=_nm("all_gather"), out_shape=jax.ShapeDtypeStruct((N_DEV, R, C), shard.dtype),
        in_specs=[ANY], out_specs=ANY,
        scratch_shapes=[pltpu.SemaphoreType.DMA((7,)), pltpu.SemaphoreType.DMA((7,)), pltpu.SemaphoreType.DMA],
    )(shard)


def _rs_pair_exchange(g):
    _, R, C = g.shape

    def body(g_ref, recv_ref, send_sems, recv_sems):
        x, y, c = _place()
        copies = [pltpu.make_async_remote_copy(
            src_ref=g_ref.at[2 * j + (1 - c)], dst_ref=recv_ref.at[j], send_sem=send_sems.at[j],
            recv_sem=recv_sems.at[j], device_id=(x, y, 1 - c), device_id_type=MESH) for j in range(4)]
        for cp in copies:
            cp.start()
        for cp in copies:
            cp.wait_recv()
        for cp in copies:
            cp.wait_send()

    return pl.pallas_call(
        body, name=_nm("rs_pair"), out_shape=jax.ShapeDtypeStruct((4, R, C), g.dtype),
        in_specs=[ANY], out_specs=ANY,
        scratch_shapes=[pltpu.SemaphoreType.DMA((4,)), pltpu.SemaphoreType.DMA((4,))],
    )(g)


def _rs_chip_exchange(p1):
    _, R, C = p1.shape

    def body(p_ref, recv_ref, send_sems, recv_sems):
        x, y, c = _place()
        chips = [(1 - x, y), (x, 1 - y), (1 - x, 1 - y)]
        copies = [pltpu.make_async_remote_copy(
            src_ref=p_ref.at[2 * cx + cy], dst_ref=recv_ref.at[k], send_sem=send_sems.at[k],
            recv_sem=recv_sems.at[k], device_id=(cx, cy, c), device_id_type=MESH) for k, (cx, cy) in enumerate(chips)]
        for cp in copies:
            cp.start()
        for cp in copies:
            cp.wait_recv()
        for cp in copies:
            cp.wait_send()

    return pl.pallas_call(
        body, name=_nm("rs_chip"), out_shape=jax.ShapeDtypeStruct((3, R, C), p1.dtype),
        in_specs=[ANY], out_specs=ANY,
        scratch_shapes=[pltpu.SemaphoreType.DMA((3,)), pltpu.SemaphoreType.DMA((3,))],
    )(p1)


def _sum_tile(R):
    for cand in (512, 256, 128, 64, 32, 16, 8):
        if R % cand == 0:
            return cand
    return R


def _rs_pair_add(g, recv, c_idx):
    _, R, C = g.shape
    tr = _sum_tile(R)

    def body(c_ref, g_ref, r_ref, o_ref):
        o_ref[...] = (g_ref[...].astype(F32) + r_ref[...].astype(F32)).astype(o_ref.dtype)

    grid_spec = pltpu.PrefetchScalarGridSpec(
        num_scalar_prefetch=1, grid=(4, R // tr),
        in_specs=[pl.BlockSpec((1, tr, C), lambda j, i, c_ref: (2 * j + c_ref[0], i, 0)),
                  pl.BlockSpec((1, tr, C), lambda j, i, c_ref: (j, i, 0))],
        out_specs=pl.BlockSpec((1, tr, C), lambda j, i, c_ref: (j, i, 0)))
    return pl.pallas_call(
        body, name=_nm("rs_pair_add"), grid_spec=grid_spec, out_shape=jax.ShapeDtypeStruct((4, R, C), g.dtype),
        compiler_params=_cp(("parallel", "parallel")),
    )(c_idx, g, recv)


def _rs_final_add(p1, recv, chip_idx):
    _, R, C = p1.shape
    tr = _sum_tile(R)

    def body(chip_ref, p_ref, r_ref, o_ref):
        acc = p_ref[0].astype(F32)
        for k in range(3):
            acc = acc + r_ref[k].astype(F32)
        o_ref[...] = acc

    grid_spec = pltpu.PrefetchScalarGridSpec(
        num_scalar_prefetch=1, grid=(R // tr,),
        in_specs=[pl.BlockSpec((1, tr, C), lambda i, chip_ref: (chip_ref[0], i, 0)),
                  pl.BlockSpec((3, tr, C), lambda i, chip_ref: (0, i, 0))],
        out_specs=pl.BlockSpec((tr, C), lambda i, chip_ref: (i, 0)))
    return pl.pallas_call(
        body, name=_nm("rs_final_add"), grid_spec=grid_spec, out_shape=jax.ShapeDtypeStruct((R, C), F32),
        compiler_params=_cp(("parallel",)),
    )(chip_idx, p1, recv)


def reduce_scatter(g):
    x, y, c = _place()
    recv1 = _rs_pair_exchange(g)
    p1 = _rs_pair_add(g, recv1, jnp.reshape(c, (1,)).astype(jnp.int32))
    recv2 = _rs_chip_exchange(p1)
    return _rs_final_add(p1, recv2, jnp.reshape(2 * x + y, (1,)).astype(jnp.int32))


def all_reduce_small(v):
    r, C = v.shape

    def body(v_ref, out_ref, buf_ref, send_sems, recv_sems):
        x, y, c = _place()
        my_id = 4 * x + 2 * y + c
        buf_ref[my_id] = v_ref[...]
        copies = []
        for k in range(1, N_DEV):
            fx, fy, fc = (k >> 2) & 1, (k >> 1) & 1, k & 1
            peer = (x ^ fx, y ^ fy, c ^ fc)
            copies.append(pltpu.make_async_remote_copy(
                src_ref=v_ref, dst_ref=buf_ref.at[my_id], send_sem=send_sems.at[k - 1], recv_sem=recv_sems.at[k - 1],
                device_id=peer, device_id_type=MESH))
        for cp in copies:
            cp.start()
        for cp in copies:
            cp.wait_recv()
        for cp in copies:
            cp.wait_send()
        acc = buf_ref[0]
        for d in range(1, N_DEV):
            acc = acc + buf_ref[d]
        out_ref[...] = acc

    vm = pl.BlockSpec(memory_space=pltpu.VMEM)
    return pl.pallas_call(
        body, name=_nm("all_reduce_small"), out_shape=jax.ShapeDtypeStruct((r, C), F32),
        in_specs=[vm], out_specs=vm,
        scratch_shapes=[pltpu.VMEM((N_DEV, r, C), F32), pltpu.SemaphoreType.DMA((7,)), pltpu.SemaphoreType.DMA((7,))],
    )(v)


def _pack(arrays, dtype, row_align):
    lead = arrays[0].shape[:-1]
    quantum = row_align * PACK_COLS
    parts, sizes = [], []
    for a in arrays:
        n = a.shape[-1]
        padded = -(-n // quantum) * quantum
        a = a.astype(dtype)
        if padded != n:
            a = jnp.pad(a, [(0, 0)] * len(lead) + [(0, padded - n)])
        parts.append(a.reshape(*lead, padded // PACK_COLS, PACK_COLS))
        sizes.append((n, padded // PACK_COLS))
    return jnp.concatenate(parts, axis=len(lead)), sizes


def _unpack(packed, sizes):
    lead = packed.shape[:-2]
    out, row = [], 0
    for n, rows in sizes:
        part = lax.slice_in_dim(packed, row, row + rows, axis=len(lead))
        out.append(part.reshape(*lead, rows * PACK_COLS)[..., :n])
        row += rows
    return out


def _unshard(gathered, axis):
    _, L, a, b = gathered.shape
    if axis == 1:
        return [gathered[:, l].reshape(N_DEV * a, b) for l in range(L)]
    return [jnp.transpose(gathered[:, l], (1, 0, 2)).reshape(a, N_DEV * b) for l in range(L)]


def _reshard(fulls, axis):
    blocks = []
    for f in fulls:
        A, B = f.shape
        if axis == 1:
            blocks.append(f.reshape(N_DEV, A // N_DEV, B))
        else:
            blocks.append(jnp.transpose(f.reshape(A, N_DEV, B // N_DEV), (1, 0, 2)))
    return jnp.stack(blocks, axis=1)


def _as3(a):
    return a if a.ndim == 3 else a[:, None, :]


def _rope_tables(positions):
    inv = 1.0 / (ROPE_THETA ** (jnp.arange(0, MLA_ROPE, 2, dtype=F32) / MLA_ROPE))
    ang = positions.astype(F32)[:, None] * inv
    cos, sin = jnp.cos(ang), jnp.sin(ang)
    one, zero = jnp.ones_like(cos), jnp.zeros_like(cos)
    a = jnp.concatenate([cos, cos, one, one], axis=1)
    up = jnp.concatenate([sin, zero, zero, zero], axis=1)
    down = jnp.concatenate([zero, sin, zero, zero], axis=1)
    return (a, -up, down), (a, up, -down)


def _forward(x, wl, p, aux):
    depth = len(wl['ln1_g'])
    alpha = (2 * depth) ** 0.25
    ln_res = rw_op(_ln_res_fn(alpha), "ln_res", 2, [F32])
    ple_op = rw_op(_ple_fn, "ple", 3, [F32])
    gla_gate = rw_op(_gla_gate_fn, "gla_gate", 1, [F32])
    for i in range(depth):
        j = i // 2
        if i % 2 == 0:
            w_in, w_uq, w_uk, w_uv, w_o = (wl[n][j] for n in ('mla_w_in', 'mla_w_uq', 'mla_w_uk', 'mla_w_uv', 'mla_w_o'))
            q_lora, kv_lora = w_uq.shape[0], w_uk.shape[0]
            heads = w_uk.shape[1] // MLA_NOPE
            w_in_p = jnp.pad(w_in, ((0, 0), (0, q_lora + kv_lora + LANES - w_in.shape[1])))
            w_uq_p = jnp.pad(w_uq.reshape(q_lora, heads, MLA_NOPE + MLA_ROPE),
                             ((0, 0), (0, 0), (0, MLA_QK_PAD - MLA_NOPE - MLA_ROPE))).reshape(q_lora, heads * MLA_QK_PAD)
            h = linear(x, w_in_p, F32)
            mla_norm = rw_op(_mla_norm_fn(q_lora, kv_lora), "mla_norm", 1, [BF16, BF16, F32])
            cq, ckv, kr_raw = mla_norm(h, wl['mla_q_norm'][j], wl['mla_kv_norm'][j])
            kr = make_rope(*aux['rope'], lambda blk: True, BF16)(kr_raw)
            q = make_rope(*aux['rope'], lambda blk: blk % 2 == 1, BF16)(linear(cq, w_uq_p, F32))
            kn = linear(ckv, w_uk, BF16)
            v = linear(ckv, w_uv, BF16)
            o = make_attention(aux['cidq'], aux['cidk'], heads)(q, kn, v, kr)
            m = linear(o, w_o, F32)
        else:
            w_in, w_a2, w_o = wl['gla_w_in'][j], wl['gla_w_a2'][j], wl['gla_w_o'][j]
            heads = w_o.shape[0] // GLA_DV
            n_main = 2 * heads * GLA_DK + 2 * heads * GLA_DV
            w_main = w_in[:, :n_main]
            w_a = jnp.pad(w_in[:, n_main:], ((0, 0), (0, LANES - GLA_RANK)))
            w_a2_p = jnp.pad(w_a2, ((0, LANES - GLA_RANK), (0, 0))).astype(BF16)
            hm = linear(x, w_main, F32)
            ha = linear(x, w_a, BF16)
            (la,) = gla_gate(linear(ha, w_a2_p, F32), wl['gla_b_a'][j])
            yg = gla_core(hm, la, wl['gla_o_norm'][j], heads)
            m = linear(yg, w_o, F32)
        (x1,) = ln_res(x, m, wl['ln1_g'][i], wl['ln1_b'][i])
        w_up = wl['ffn_w_up'][i]
        dff = w_up.shape[1] // 2
        hu = linear(x1, w_up[:, :dff], F32)
        hg = linear(x1, w_up[:, dff:], F32)
        a = conv_gate(hu, hg, wl['ffn_conv_w'][i], wl['ffn_conv_b'][i])
        f = linear(a, wl['ffn_w_down'][i], F32)
        (x2,) = ln_res(x1, f, wl['ln2_g'][i], wl['ln2_b'][i])
        glog = linear(x2, wl['ple_w_gate'][i], F32)
        pp = linear(p[i], wl['ple_w_proj'][i], F32)
        (x,) = ple_op(x2, glog, pp, wl['ple_b_gate'][i])
    return x


def kernel(x, p, positions, mla_w_in, mla_q_norm, mla_kv_norm, mla_w_uq, mla_w_uk, mla_w_uv, mla_w_o, gla_w_in, gla_w_a2, gla_b_a, gla_o_norm, gla_w_o, ln1_g, ln1_b, ln2_g, ln2_b, ffn_w_up, ffn_conv_w, ffn_conv_b, ffn_w_down, ple_w_proj, ple_w_gate, ple_b_gate, loss_target, m_mla_w_in, m_mla_q_norm, m_mla_kv_norm, m_mla_w_uq, m_mla_w_uk, m_mla_w_uv, m_mla_w_o, m_gla_w_in, m_gla_w_a2, m_gla_b_a, m_gla_o_norm, m_gla_w_o, m_ln1_g, m_ln1_b, m_ln2_g, m_ln2_b, m_ffn_w_up, m_ffn_conv_w, m_ffn_conv_b, m_ffn_w_down, m_ple_w_proj, m_ple_w_gate, m_ple_b_gate, v_mla_w_in, v_mla_q_norm, v_mla_kv_norm, v_mla_w_uq, v_mla_w_uk, v_mla_w_uv, v_mla_w_o, v_gla_w_in, v_gla_w_a2, v_gla_b_a, v_gla_o_norm, v_gla_w_o, v_ln1_g, v_ln1_b, v_ln2_g, v_ln2_b, v_ffn_w_up, v_ffn_conv_w, v_ffn_conv_b, v_ffn_w_down, v_ple_w_proj, v_ple_w_gate, v_ple_b_gate):
    w = dict(zip(WEIGHTS, (mla_w_in, mla_q_norm, mla_kv_norm, mla_w_uq, mla_w_uk, mla_w_uv, mla_w_o, gla_w_in, gla_w_a2,
                           gla_b_a, gla_o_norm, gla_w_o, ln1_g, ln1_b, ln2_g, ln2_b, ffn_w_up, ffn_conv_w, ffn_conv_b,
                           ffn_w_down, ple_w_proj, ple_w_gate, ple_b_gate)))
    m_in = dict(zip(WEIGHTS, (m_mla_w_in, m_mla_q_norm, m_mla_kv_norm, m_mla_w_uq, m_mla_w_uk, m_mla_w_uv, m_mla_w_o,
                              m_gla_w_in, m_gla_w_a2, m_gla_b_a, m_gla_o_norm, m_gla_w_o, m_ln1_g, m_ln1_b, m_ln2_g,
                              m_ln2_b, m_ffn_w_up, m_ffn_conv_w, m_ffn_conv_b, m_ffn_w_down, m_ple_w_proj, m_ple_w_gate,
                              m_ple_b_gate)))
    v_in = dict(zip(WEIGHTS, (v_mla_w_in, v_mla_q_norm, v_mla_kv_norm, v_mla_w_uq, v_mla_w_uk, v_mla_w_uv, v_mla_w_o,
                              v_gla_w_in, v_gla_w_a2, v_gla_b_a, v_gla_o_norm, v_gla_w_o, v_ln1_g, v_ln1_b, v_ln2_g,
                              v_ln2_b, v_ffn_w_up, v_ffn_conv_w, v_ffn_conv_b, v_ffn_w_down, v_ple_w_proj, v_ple_w_gate,
                              v_ple_b_gate)))
    _uid[0] = itertools.count()
    x2d, target, pos = x[0], loss_target[0], positions[0]
    p3 = p[:, 0]
    S = x2d.shape[0]

    def gather_group(names, dtype, row_align):
        shards3 = [_as3(w[n]) for n in names]
        packed, sizes = _pack([s.reshape(1, -1) for s in shards3], dtype, row_align)
        gathered = all_gather(packed[0])
        flats = _unpack(gathered, sizes)
        fulls = {}
        for n, s3, flat in zip(names, shards3, flats):
            fulls[n] = _unshard(flat.reshape(N_DEV, *s3.shape), SHARD_AXIS[n] if w[n].ndim == 3 else 2)
        return fulls, sizes

    big, big_sizes = gather_group(BIG, BF16, 16)
    small, small_sizes = gather_group(SMALL, F32, 8)
    wl = {**big, **small}
    for n in REPL:
        wl[n] = [w[n][l][None, :] for l in range(w[n].shape[0])]

    aux = {'rope': _rope_tables(pos), 'cidq': (pos // CHUNK)[:, None], 'cidk': (pos // CHUNK)[None, :]}

    y, vjp_fn = jax.vjp(lambda x_, wl_: _forward(x_, wl_, p3, aux), x2d, wl)
    sq, dy = _loss_call(y, target)
    dx, dwl = vjp_fn(dy)

    def scatter_group(names, sizes, dtype, row_align):
        blocks = []
        for n in names:
            axis = SHARD_AXIS[n] if w[n].ndim == 3 else 2
            blocks.append(_reshard(dwl[n], axis).reshape(N_DEV, -1))
        packed, _ = _pack(blocks, dtype, row_align)
        flats = _unpack(reduce_scatter(packed), sizes)
        return {n: f.reshape(w[n].shape) for n, f in zip(names, flats)}

    grads = {**scatter_group(BIG, big_sizes, BF16, 16), **scatter_group(SMALL, small_sizes, F32, 8)}

    repl_flat = [jnp.concatenate([g.reshape(-1) for g in dwl[n]]).reshape(1, -1) for n in REPL]
    loss_part = 0.5 * jnp.sum(sq) / sq.shape[1]
    packed, repl_sizes = _pack(repl_flat + [loss_part.reshape(1, 1)], F32, 8)
    reduced = _unpack(all_reduce_small(packed[0])[None], repl_sizes)
    for n, f in zip(REPL, reduced[:-1]):
        grads[n] = f.reshape(w[n].shape)
    loss = reduced[-1].reshape(())

    delta, new_m, new_v = {}, {}, {}
    for n in WEIGHTS:
        delta[n], new_m[n], new_v[n] = _adamw_call(w[n], grads[n], m_in[n], v_in[n])
    return (loss, dx[None], *[grads[n] for n in WEIGHTS], *[delta[n] for n in WEIGHTS],
            *[new_m[n] for n in WEIGHTS], *[new_v[n] for n in WEIGHTS])
```

```python
import functools
import itertools

import jax
import jax.numpy as jnp
from jax import lax
from jax.experimental import pallas as pl
from jax.experimental.pallas import tpu as pltpu

F32 = jnp.float32
BF16 = jnp.bfloat16
MESH = pl.DeviceIdType.MESH
N_DEV = 8

EPS = 1e-5
NEG_INF = -1e30
CHUNK = 64
Q_BLOCK = 128
MLA_NOPE = 128
MLA_ROPE = 64
MLA_V = 128
MLA_QK_PAD = 256
ROPE_THETA = 10000.0
GLA_DK = 128
GLA_DV = 256
GLA_RANK = 16
GLA_TAU = 16.0
CONV_W = 3
ADAM_LR = 0.001
ADAM_B1 = 0.9
ADAM_B2 = 0.999
ADAM_EPS = 1e-08
ADAM_WD = 0.01
ADAM_STEP = 10
LOG2E = 1.4426950408889634

LANES = 128
PACK_COLS = 1024
VMEM_LIMIT = 48 * 1024 * 1024
MM_VMEM_BUDGET = 30 * 1024 * 1024

WEIGHTS = ['mla_w_in', 'mla_q_norm', 'mla_kv_norm', 'mla_w_uq', 'mla_w_uk', 'mla_w_uv', 'mla_w_o', 'gla_w_in',
           'gla_w_a2', 'gla_b_a', 'gla_o_norm', 'gla_w_o', 'ln1_g', 'ln1_b', 'ln2_g', 'ln2_b', 'ffn_w_up',
           'ffn_conv_w', 'ffn_conv_b', 'ffn_w_down', 'ple_w_proj', 'ple_w_gate', 'ple_b_gate']
SHARD_AXIS = {'mla_w_in': 1, 'mla_q_norm': None, 'mla_kv_norm': None, 'mla_w_uq': 2, 'mla_w_uk': 2, 'mla_w_uv': 2,
              'mla_w_o': 1, 'gla_w_in': 2, 'gla_w_a2': 2, 'gla_b_a': 1, 'gla_o_norm': 1, 'gla_w_o': 1,
              'ln1_g': None, 'ln1_b': None, 'ln2_g': None, 'ln2_b': None, 'ffn_w_up': 2, 'ffn_conv_w': 2,
              'ffn_conv_b': None, 'ffn_w_down': 1, 'ple_w_proj': 2, 'ple_w_gate': 1, 'ple_b_gate': None}
BIG = ['mla_w_in', 'mla_w_uq', 'mla_w_uk', 'mla_w_uv', 'mla_w_o', 'gla_w_in', 'gla_w_o', 'ffn_w_up', 'ffn_w_down',
       'ple_w_proj', 'ple_w_gate']
SMALL = ['gla_w_a2', 'gla_b_a', 'gla_o_norm', 'ffn_conv_w']
REPL = [n for n in WEIGHTS if SHARD_AXIS[n] is None]

_uid = [itertools.count()]


def _nm(base):
    return f"{base}_{next(_uid[0])}"


def _cp(sem=None):
    return pltpu.CompilerParams(dimension_semantics=sem, vmem_limit_bytes=VMEM_LIMIT)


def _round_up(n, m):
    return -(-n // m) * m


def _divisor_tiles(n, cap):
    if n % LANES:
        return [n]
    out = [t for t in range(LANES, min(n, cap) + 1, LANES) if n % t == 0]
    return out or [n]


def _mm_tiles(M, N, K, abytes, bbytes, obytes, tn_fixed=None, tk_fixed=None):
    best = None
    for tm in _divisor_tiles(M, 1024):
        for tn in ([tn_fixed] if tn_fixed else _divisor_tiles(N, 1536)):
            for tk in ([tk_fixed] if tk_fixed else _divisor_tiles(K, 2048)):
                vmem = 2 * (tm * tk * abytes + tk * tn * bbytes + tm * tn * obytes) + tm * tn * 4
                if vmem > MM_VMEM_BUDGET:
                    continue
                key = (tm * tn * tk, tk)
                if best is None or key > best[0]:
                    best = (key, (tm, tn, tk))
    assert best is not None, (M, N, K)
    return best[1]


def _mm(a, b, mode, out_dtype, base="mm", blocks=None):
    blk0, nblk = blocks if blocks else (0, 1)
    tn_fixed = tk_fixed = None
    if mode == "nn":
        M, K = a.shape
        N = nblk * b.shape[2] if blocks else b.shape[1]
        tn_fixed = b.shape[2] if blocks else None
    elif mode == "nt":
        M, K = a.shape
        N = b.shape[1] if blocks else b.shape[0]
        tk_fixed = b.shape[2] if blocks else None
        assert not blocks or K == nblk * b.shape[2]
    else:
        (K, M), N = a.shape, b.shape[1]
        tn_fixed = N // nblk if blocks else None
    tm, tn, tk = _mm_tiles(M, N, K, a.dtype.itemsize, b.dtype.itemsize, jnp.dtype(out_dtype).itemsize, tn_fixed, tk_fixed)
    nk = K // tk
    out_shape = jax.ShapeDtypeStruct((M, N), out_dtype)
    out_spec = pl.BlockSpec((tm, tn), lambda i, j, k: (i, j))
    if mode == "nn":
        a_spec = pl.BlockSpec((tm, tk), lambda i, j, k: (i, k))
        b_spec = (pl.BlockSpec((None, tk, tn), lambda i, j, k: (blk0 + j, k, 0)) if blocks
                  else pl.BlockSpec((tk, tn), lambda i, j, k: (k, j)))
        dims = (((1,), (0,)), ((), ()))
    elif mode == "nt":
        a_spec = pl.BlockSpec((tm, tk), lambda i, j, k: (i, k))
        b_spec = (pl.BlockSpec((None, tn, tk), lambda i, j, k: (blk0 + k, j, 0)) if blocks
                  else pl.BlockSpec((tn, tk), lambda i, j, k: (j, k)))
        dims = (((1,), (1,)), ((), ()))
    else:
        a_spec = pl.BlockSpec((tk, tm), lambda i, j, k: (k, i))
        b_spec = pl.BlockSpec((tk, tn), lambda i, j, k: (k, j))
        dims = (((0,), (0,)), ((), ()))
        if blocks:
            out_shape = jax.ShapeDtypeStruct((nblk, M, tn), out_dtype)
            out_spec = pl.BlockSpec((None, tm, tn), lambda i, j, k: (j, i, 0))

    def body(a_ref, b_ref, o_ref, acc_ref):
        part = lax.dot_general(a_ref[...].astype(BF16), b_ref[...].astype(BF16), dims, preferred_element_type=F32)
        if nk == 1:
            o_ref[...] = part.astype(o_ref.dtype)
        else:
            k = pl.program_id(2)

            @pl.when(k == 0)
            def _():
                acc_ref[...] = part

            @pl.when(k > 0)
            def _():
                acc_ref[...] += part

            @pl.when(k == nk - 1)
            def _():
                o_ref[...] = acc_ref[...].astype(o_ref.dtype)

    return pl.pallas_call(
        body, name=_nm(base), grid=(M // tm, N // tn, nk), out_shape=out_shape,
        in_specs=[a_spec, b_spec], out_specs=out_spec,
        scratch_shapes=[pltpu.VMEM((tm, tn) if nk > 1 else (8, LANES), F32)],
        compiler_params=_cp(("parallel", "parallel", "arbitrary")),
    )(a, b)


def _all_blocks(w):
    return (0, w.shape[0]) if w.ndim == 3 else None


@functools.partial(jax.custom_vjp, nondiff_argnums=(2,))
def linear(a, w, out_dtype):
    return _mm(a, w, "nn", out_dtype, "lin_fwd", _all_blocks(w))


def _linear_fwd(a, w, out_dtype):
    return _mm(a, w, "nn", out_dtype, "lin_fwd", _all_blocks(w)), (a, w)


def _linear_bwd(out_dtype, res, dy):
    a, w = res
    return (_mm(dy, w, "nt", a.dtype, "lin_dx", _all_blocks(w)), _mm(a, dy, "tn", w.dtype, "lin_dw", _all_blocks(w)))


linear.defvjp(_linear_fwd, _linear_bwd)


def _row_tile(S, width):
    tr = 512 if width <= 1024 else 256
    return min(tr, S)


def _rw_fwd(f, rows, params, out_dtypes, base):
    S = rows[0][0].shape[0]
    tr = _row_tile(S, max(w for _, w, _ in rows))
    n_in = len(rows) + len(params)
    avals = [jax.ShapeDtypeStruct((tr, w), F32) for _, w, _ in rows] + [jax.ShapeDtypeStruct(p.shape, F32) for p in params]
    outs = jax.eval_shape(f, *avals)

    def body(*refs):
        vals = [r[...].astype(F32) for r in refs[:n_in]]
        for o_ref, r in zip(refs[n_in:], f(*vals)):
            o_ref[...] = r.astype(o_ref.dtype)

    in_specs = [pl.BlockSpec((tr, w), functools.partial(lambda i, cb: (i, cb), cb=cb)) for _, w, cb in rows]
    in_specs += [pl.BlockSpec(p.shape, lambda i: (0, 0)) for p in params]
    return pl.pallas_call(
        body, name=_nm(base), grid=(S // tr,),
        out_shape=tuple(jax.ShapeDtypeStruct((S, o.shape[1]), dt) for o, dt in zip(outs, out_dtypes)),
        in_specs=in_specs, out_specs=tuple(pl.BlockSpec((tr, o.shape[1]), lambda i: (i, 0)) for o in outs),
        compiler_params=_cp(("parallel",)),
    )(*[a for a, _, _ in rows], *params)


def _rw_bwd(f, rows, params, cts, row_grad_dtypes, base):
    S = rows[0][0].shape[0]
    tr = _row_tile(S, max(w for _, w, _ in rows))
    n_rows, n_par, n_ct = len(rows), len(params), len(cts)
    want = [k for k, dt in enumerate(row_grad_dtypes) if dt is not None]

    def body(*refs):
        in_refs = refs[:n_rows + n_par]
        ct_refs = refs[n_rows + n_par:n_rows + n_par + n_ct]
        out_refs = refs[n_rows + n_par + n_ct:]
        vals = [r[...].astype(F32) for r in in_refs]
        _, vjp_fn = jax.vjp(f, *vals)
        grads = vjp_fn(tuple(c[...].astype(F32) for c in ct_refs))
        for o_ref, k in zip(out_refs[:len(want)], want):
            o_ref[...] = grads[k].astype(o_ref.dtype)
        i = pl.program_id(0)
        for o_ref, g in zip(out_refs[len(want):], grads[n_rows:]):
            @pl.when(i == 0)
            def _(o_ref=o_ref, g=g):
                o_ref[...] = g

            @pl.when(i > 0)
            def _(o_ref=o_ref, g=g):
                o_ref[...] += g

    in_specs = [pl.BlockSpec((tr, w), functools.partial(lambda i, cb: (i, cb), cb=cb)) for _, w, cb in rows]
    in_specs += [pl.BlockSpec(p.shape, lambda i: (0, 0)) for p in params]
    in_specs += [pl.BlockSpec((tr, c.shape[1]), lambda i: (i, 0)) for c in cts]
    out_shape = [jax.ShapeDtypeStruct((S, rows[k][1]), row_grad_dtypes[k]) for k in want]
    out_specs = [pl.BlockSpec((tr, rows[k][1]), lambda i: (i, 0)) for k in want]
    out_shape += [jax.ShapeDtypeStruct(p.shape, F32) for p in params]
    out_specs += [pl.BlockSpec(p.shape, lambda i: (0, 0)) for p in params]
    res = pl.pallas_call(
        body, name=_nm(base), grid=(S // tr,), out_shape=tuple(out_shape),
        in_specs=in_specs, out_specs=tuple(out_specs), compiler_params=_cp(("arbitrary",)),
    )(*[a for a, _, _ in rows], *params, *cts)
    row_grads = [None] * n_rows
    for k, g in zip(want, res[:len(want)]):
        row_grads[k] = g
    return row_grads, list(res[len(want):])


def rw_op(f, base, n_rows, out_dtypes):
    @jax.custom_vjp
    def op(*args):
        return fwd(*args)[0]

    def split(args):
        rows = [(a, a.shape[1], 0) for a in args[:n_rows]]
        return rows, list(args[n_rows:])

    def fwd(*args):
        rows, params = split(args)
        return tuple(_rw_fwd(f, rows, params, out_dtypes, base + "_fwd")), args

    def bwd(args, cts):
        rows, params = split(args)
        rg, pg = _rw_bwd(f, rows, params, list(cts), [a.dtype for a, _, _ in rows], base + "_bwd")
        return tuple(rg) + tuple(g.astype(p.dtype) for g, p in zip(pg, params))

    op.defvjp(fwd, bwd)
    return op


def _ln_res_fn(alpha):
    def f(x, m, g, b):
        z = alpha * x + m
        mu = jnp.mean(z, -1, keepdims=True)
        zc = z - mu
        var = jnp.mean(zc * zc, -1, keepdims=True)
        return (zc * lax.rsqrt(var + EPS) * g + b,)
    return f


def _rms(x, g):
    return x * lax.rsqrt(jnp.mean(x * x, -1, keepdims=True) + EPS) * g


def _mla_norm_fn(q_lora, kv_lora):
    def f(h, qn, kvn):
        return (_rms(h[:, :q_lora], qn), _rms(h[:, q_lora:q_lora + kv_lora], kvn),
                h[:, q_lora + kv_lora:q_lora + kv_lora + LANES])
    return f


def _log_sigmoid(z):
    return jnp.minimum(z, 0.0) - jnp.log(1.0 + jnp.exp(-jnp.abs(z)))


def _gla_gate_fn(z, b):
    return (_log_sigmoid(z + b) / GLA_TAU,)


def _ple_fn(x, glog, pp, b):
    return (x + jax.nn.sigmoid(glog + b) * pp,)


def _gla_out_fn(heads):
    def f(o, r, g):
        parts = []
        for h in range(heads):
            oh = o[:, h * GLA_DV:(h + 1) * GLA_DV]
            mu = jnp.mean(oh, -1, keepdims=True)
            oc = oh - mu
            var = jnp.mean(oc * oc, -1, keepdims=True)
            parts.append(oc * lax.rsqrt(var + EPS) * g[:, h * GLA_DV:(h + 1) * GLA_DV])
        return (jnp.concatenate(parts, axis=1) * (r * jax.nn.sigmoid(r)),)
    return f


def _rope_call(x, tabs, roped, out_dtype, base, fold=False):
    S, C = x.shape
    nb = C // LANES
    tr = min(512 if C <= 1024 else 256, S)
    out_c = LANES if fold else C

    def rot(v, a, b1, b2):
        return v * a + pltpu.roll(v, 96, 1) * b1 + pltpu.roll(v, 32, 1) * b2

    def body(x_ref, a_ref, b1_ref, b2_ref, o_ref):
        a, b1, b2 = a_ref[...], b1_ref[...], b2_ref[...]
        if fold:
            v = x_ref[:, 0:LANES].astype(F32)
            for blk in range(1, nb):
                v = v + x_ref[:, blk * LANES:(blk + 1) * LANES].astype(F32)
            o_ref[...] = rot(v, a, b1, b2).astype(o_ref.dtype)
            return
        for blk in range(nb):
            v = x_ref[:, blk * LANES:(blk + 1) * LANES].astype(F32)
            if roped(blk):
                v = rot(v, a, b1, b2)
            o_ref[:, blk * LANES:(blk + 1) * LANES] = v.astype(o_ref.dtype)

    row = lambda w: pl.BlockSpec((tr, w), lambda i: (i, 0))
    return pl.pallas_call(
        body, name=_nm(base), grid=(S // tr,), out_shape=jax.ShapeDtypeStruct((S, out_c), out_dtype),
        in_specs=[row(C), row(LANES), row(LANES), row(LANES)], out_specs=row(out_c),
        compiler_params=_cp(("parallel",)),
    )(x, *tabs)


def _attn_tile(S):
    return min(512, S)


def _tri_schedule(n, by_key):
    pairs = ([(i, j) for j in range(n) for i in range(j, n)] if by_key
             else [(i, j) for i in range(n) for j in range(i + 1)])
    return (jnp.asarray([p[0] for p in pairs], jnp.int32), jnp.asarray([p[1] for p in pairs], jnp.int32))


def _mask_table(cid, t):
    n = cid.shape[0] // t
    blocks = cid.reshape(n, t)
    cmin_q, cmax_k = jnp.min(blocks, axis=1), jnp.max(blocks, axis=1)
    need = (cmax_k[None, :] > cmin_q[:, None]) | jnp.eye(n, dtype=bool)
    return need.astype(jnp.int32).reshape(n * n)


def _attn_mask(cidq_ref, cidk_ref, i, j, t):
    qrow = i * t + lax.broadcasted_iota(jnp.int32, (t, 1), 0)
    kcol = j * t + lax.broadcasted_iota(jnp.int32, (1, t), 1)
    qlim = (qrow // Q_BLOCK + 1) * Q_BLOCK
    return (cidk_ref[...] <= cidq_ref[...]) & (kcol < qlim)


def _attn_fwd_call(q, kn, v, kr, aux, heads):
    S = q.shape[0]
    t = _attn_tile(S)
    n = S // t
    qi_tab, kj_tab = _tri_schedule(n, False)
    scale2 = (MLA_NOPE + MLA_ROPE) ** -0.5 * LOG2E

    def body(qi_ref, kj_ref, need_ref, q_ref, kn_ref, v_ref, kr_ref, cidq_ref, cidk_ref, o_ref, lse_ref,
             m_ref, l_ref, acc_ref):
        st = pl.program_id(1)
        i, j = qi_ref[st], kj_ref[st]

        @pl.when(j == 0)
        def _():
            m_ref[...] = jnp.full(m_ref.shape, NEG_INF, F32)
            l_ref[...] = jnp.zeros(l_ref.shape, F32)
            acc_ref[...] = jnp.zeros(acc_ref.shape, F32)

        k = jnp.concatenate([kn_ref[...], kr_ref[...]], axis=1)
        s = lax.dot_general(q_ref[...], k, (((1,), (1,)), ((), ())), preferred_element_type=F32) * scale2

        def update(s):
            m_prev = m_ref[...]
            m_new = jnp.maximum(m_prev, jnp.max(s, axis=1, keepdims=True))
            alpha = jnp.exp2(m_prev - m_new)
            p = jnp.exp2(s - m_new[:, :1])
            l_ref[...] = alpha * l_ref[...] + jnp.sum(p, axis=1, keepdims=True)
            acc_ref[...] = alpha * acc_ref[...] + jnp.dot(p.astype(BF16), v_ref[...], preferred_element_type=F32)
            m_ref[...] = m_new

        need = need_ref[i * n + j]

        @pl.when(need != 0)
        def _():
            update(jnp.where(_attn_mask(cidq_ref, cidk_ref, i, j, t), s, NEG_INF))

        @pl.when(need == 0)
        def _():
            update(s)

        @pl.when(j == i)
        def _():
            o_ref[...] = (acc_ref[...] / l_ref[...]).astype(o_ref.dtype)
            lse_ref[...] = m_ref[...] + jnp.log(l_ref[...]) * LOG2E

    qmap = lambda h, s, qi, kj, need: (qi[s], h)
    kmap = lambda h, s, qi, kj, need: (kj[s], h)
    grid_spec = pltpu.PrefetchScalarGridSpec(
        num_scalar_prefetch=3, grid=(heads, qi_tab.shape[0]),
        in_specs=[pl.BlockSpec((t, MLA_QK_PAD), qmap), pl.BlockSpec((t, MLA_NOPE), kmap), pl.BlockSpec((t, MLA_V), kmap),
                  pl.BlockSpec((t, LANES), lambda h, s, qi, kj, need: (kj[s], 0)),
                  pl.BlockSpec((t, 1), lambda h, s, qi, kj, need: (qi[s], 0)),
                  pl.BlockSpec((1, t), lambda h, s, qi, kj, need: (0, kj[s]))],
        out_specs=(pl.BlockSpec((t, MLA_V), qmap), pl.BlockSpec((t, LANES), qmap)),
        scratch_shapes=[pltpu.VMEM((t, LANES), F32), pltpu.VMEM((t, LANES), F32), pltpu.VMEM((t, MLA_V), F32)])
    return pl.pallas_call(
        body, name=_nm("attn_fwd"), grid_spec=grid_spec,
        out_shape=(jax.ShapeDtypeStruct((S, heads * MLA_V), BF16), jax.ShapeDtypeStruct((S, heads * LANES), F32)),
        compiler_params=_cp(("parallel", "arbitrary")),
    )(qi_tab, kj_tab, aux['need'], q, kn, v, kr, aux['cidq'], aux['cidk'])


def _attn_bwd_call(q, kn, v, kr, o, lse, do, aux, heads):
    S = q.shape[0]
    t = _attn_tile(S)
    n = S // t
    qi_tab, kj_tab = _tri_schedule(n, True)
    scale = (MLA_NOPE + MLA_ROPE) ** -0.5
    scale2 = scale * LOG2E
    nt_dims = (((1,), (1,)), ((), ()))
    tn_dims = (((0,), (0,)), ((), ()))

    def body(qi_ref, kj_ref, need_ref, q_ref, kn_ref, v_ref, kr_ref, cidq_ref, cidk_ref, o_ref, lse_ref, do_ref,
             dq_ref, dkn_ref, dv_ref, dkr_ref, dk_acc, dv_acc):
        st = pl.program_id(1)
        i, j = qi_ref[st], kj_ref[st]

        @pl.when(st == 0)
        def _():
            dq_ref[...] = jnp.zeros(dq_ref.shape, F32)

        @pl.when(i == j)
        def _():
            dk_acc[...] = jnp.zeros(dk_acc.shape, F32)
            dv_acc[...] = jnp.zeros(dv_acc.shape, F32)

        k = jnp.concatenate([kn_ref[...], kr_ref[...]], axis=1)
        qt, do = q_ref[...], do_ref[...]
        s = lax.dot_general(qt, k, nt_dims, preferred_element_type=F32) * scale2
        dp = lax.dot_general(do, v_ref[...], nt_dims, preferred_element_type=F32)
        dsum = jnp.sum(do.astype(F32) * o_ref[...].astype(F32), axis=1, keepdims=True)
        rows = pl.ds(pl.multiple_of(i * t, t), t)

        def grads(s):
            p = jnp.exp2(s - lse_ref[:, :1])
            ds = (p * (dp - dsum) * scale).astype(BF16)
            dv_acc[...] += lax.dot_general(p.astype(BF16), do, tn_dims, preferred_element_type=F32)
            dk_acc[...] += lax.dot_general(ds, qt, tn_dims, preferred_element_type=F32)
            dq_ref[rows, :] += jnp.dot(ds, k, preferred_element_type=F32)

        need = need_ref[i * n + j]

        @pl.when(need != 0)
        def _():
            grads(jnp.where(_attn_mask(cidq_ref, cidk_ref, i, j, t), s, NEG_INF))

        @pl.when(need == 0)
        def _():
            grads(s)

        @pl.when(i == n - 1)
        def _():
            dkn_ref[...] = dk_acc[:, :MLA_NOPE].astype(dkn_ref.dtype)
            dv_ref[...] = dv_acc[...].astype(dv_ref.dtype)
            dkr_ref[...] = dk_acc[:, MLA_NOPE:]

    qmap = lambda h, s, qi, kj, need: (qi[s], h)
    kmap = lambda h, s, qi, kj, need: (kj[s], h)
    grid_spec = pltpu.PrefetchScalarGridSpec(
        num_scalar_prefetch=3, grid=(heads, qi_tab.shape[0]),
        in_specs=[pl.BlockSpec((t, MLA_QK_PAD), qmap), pl.BlockSpec((t, MLA_NOPE), kmap), pl.BlockSpec((t, MLA_V), kmap),
                  pl.BlockSpec((t, LANES), lambda h, s, qi, kj, need: (kj[s], 0)),
                  pl.BlockSpec((t, 1), lambda h, s, qi, kj, need: (qi[s], 0)),
                  pl.BlockSpec((1, t), lambda h, s, qi, kj, need: (0, kj[s])),
                  pl.BlockSpec((t, MLA_V), qmap), pl.BlockSpec((t, LANES), qmap), pl.BlockSpec((t, MLA_V), qmap)],
        out_specs=(pl.BlockSpec((S, MLA_QK_PAD), lambda h, s, qi, kj, need: (0, h)),
                   pl.BlockSpec((t, MLA_NOPE), kmap), pl.BlockSpec((t, MLA_V), kmap), pl.BlockSpec((t, LANES), kmap)),
        scratch_shapes=[pltpu.VMEM((t, MLA_QK_PAD), F32), pltpu.VMEM((t, MLA_V), F32)])
    return pl.pallas_call(
        body, name=_nm("attn_bwd"), grid_spec=grid_spec,
        out_shape=(jax.ShapeDtypeStruct((S, heads * MLA_QK_PAD), F32), jax.ShapeDtypeStruct((S, heads * MLA_NOPE), BF16),
                   jax.ShapeDtypeStruct((S, heads * MLA_V), BF16), jax.ShapeDtypeStruct((S, heads * LANES), F32)),
        compiler_params=_cp(("parallel", "arbitrary")),
    )(qi_tab, kj_tab, aux['need'], q, kn, v, kr, aux['cidq'], aux['cidk'], o, lse, do)


def make_attention(aux, heads):
    tabs_f, tabs_b = aux['rope']
    odd, every = (lambda blk: blk % 2 == 1), (lambda blk: True)

    def run(q_raw, kn, v, kr_raw):
        q = _rope_call(q_raw, tabs_f, odd, BF16, "rope_q")
        kr = _rope_call(kr_raw, tabs_f, every, BF16, "rope_k")
        o, lse = _attn_fwd_call(q, kn, v, kr, aux, heads)
        return o, (q, kn, v, kr, o, lse)

    @jax.custom_vjp
    def attn(q_raw, kn, v, kr_raw):
        return run(q_raw, kn, v, kr_raw)[0]

    def bwd(res, do):
        q, kn, v, kr, o, lse = res
        dq, dkn, dv, dkr = _attn_bwd_call(q, kn, v, kr, o, lse, do, aux, heads)
        return (_rope_call(dq, tabs_b, odd, F32, "rope_dq"), dkn, dv,
                _rope_call(dkr, tabs_b, every, F32, "rope_dk", fold=True))

    attn.defvjp(run, bwd)
    return attn


GLA_ROWS = 256


def _tri(lower):
    r = lax.broadcasted_iota(jnp.int32, (CHUNK, CHUNK), 0)
    c = lax.broadcasted_iota(jnp.int32, (CHUNK, CHUNK), 1)
    return jnp.where((c <= r) if lower else (c >= r), 1.0, 0.0).astype(F32)


def _gla_chunk(q_ref, k_ref, v_ref, la_ref, sl):
    la = la_ref[sl, :]
    cum = jnp.dot(_tri(True), la, preferred_element_type=F32, precision=lax.Precision.HIGHEST)
    tot = cum[CHUNK - 1:CHUNK, :]
    e = jnp.exp(tot - cum)
    k = k_ref[sl, :]
    kdec = k * e
    v = v_ref[sl, :]
    upd_t = lax.dot_general(v.astype(BF16), kdec.astype(BF16), (((0,), (0,)), ((), ())), preferred_element_type=F32)
    qs = (q_ref[sl, :] * (GLA_DK ** -0.5)).astype(BF16)
    return e, k, kdec, v, upd_t, jnp.exp(tot), qs


def _gla_specs(heads, rows_map):
    return [pl.BlockSpec((GLA_ROWS, GLA_DK), lambda h, b: (rows_map(b), h)),
            pl.BlockSpec((GLA_ROWS, GLA_DK), lambda h, b: (rows_map(b), heads + h)),
            pl.BlockSpec((GLA_ROWS, GLA_DV), lambda h, b: (rows_map(b), heads + h)),
            pl.BlockSpec((GLA_ROWS, GLA_DK), lambda h, b: (rows_map(b), h))]


def _gla_fwd_call(hm, la, heads):
    S = hm.shape[0]
    assert S % GLA_ROWS == 0
    nb, cpb = S // GLA_ROWS, GLA_ROWS // CHUNK

    def body(q_ref, k_ref, v_ref, la_ref, o_ref, sp_ref, st_ref):
        @pl.when(pl.program_id(1) == 0)
        def _():
            st_ref[...] = jnp.zeros(st_ref.shape, F32)

        for c in range(cpb):
            sl = slice(c * CHUNK, (c + 1) * CHUNK)
            _, _, _, _, upd_t, decay, qs = _gla_chunk(q_ref, k_ref, v_ref, la_ref, sl)
            state = st_ref[...]
            sp_ref[0, c] = state
            state = state * decay + upd_t
            st_ref[...] = state
            o_ref[sl, :] = lax.dot_general(qs, state.astype(BF16), (((1,), (1,)), ((), ())), preferred_element_type=F32)

    return pl.pallas_call(
        body, name=_nm("gla_fwd"), grid=(heads, nb),
        out_shape=(jax.ShapeDtypeStruct((S, heads * GLA_DV), F32),
                   jax.ShapeDtypeStruct((heads, S // CHUNK, GLA_DV, GLA_DK), F32)),
        in_specs=_gla_specs(heads, lambda b: b),
        out_specs=(pl.BlockSpec((GLA_ROWS, GLA_DV), lambda h, b: (b, h)),
                   pl.BlockSpec((1, cpb, GLA_DV, GLA_DK), lambda h, b: (h, b, 0, 0))),
        scratch_shapes=[pltpu.VMEM((GLA_DV, GLA_DK), F32)],
        compiler_params=_cp(("parallel", "arbitrary")),
    )(hm, hm, hm, la)


def _gla_bwd_call(hm, la, sprev, do, heads):
    S = hm.shape[0]
    nb, cpb = S // GLA_ROWS, GLA_ROWS // CHUNK
    scale = GLA_DK ** -0.5

    def body(q_ref, k_ref, v_ref, la_ref, sp_ref, do_ref, dq_ref, dk_ref, dv_ref, dla_ref, carry_ref):
        @pl.when(pl.program_id(1) == 0)
        def _():
            carry_ref[...] = jnp.zeros(carry_ref.shape, F32)

        for c in reversed(range(cpb)):
            sl = slice(c * CHUNK, (c + 1) * CHUNK)
            e, k, kdec, v, upd_t, decay, qs = _gla_chunk(q_ref, k_ref, v_ref, la_ref, sl)
            sp = sp_ref[0, c]
            s_n = sp * decay + upd_t
            dob = do_ref[sl, :].astype(BF16)
            g = carry_ref[...] + lax.dot_general(dob, qs, (((0,), (0,)), ((), ())), preferred_element_type=F32)
            gb = g.astype(BF16)
            dq_ref[sl, :] = jnp.dot(dob, s_n.astype(BF16), preferred_element_type=F32) * scale
            ddecay = jnp.sum(g * sp, axis=0, keepdims=True)
            dkdec = jnp.dot(v.astype(BF16), gb, preferred_element_type=F32)
            dv_ref[sl, :] = lax.dot_general(kdec.astype(BF16), gb, (((1,), (1,)), ((), ())), preferred_element_type=F32)
            dk_ref[sl, :] = dkdec * e
            w = dkdec * k * e
            dtot = jnp.sum(w, axis=0, keepdims=True) + ddecay * decay
            last = lax.broadcasted_iota(jnp.int32, (CHUNK, 1), 0) == CHUNK - 1
            dcum = jnp.where(last, dtot - w, -w)
            dla_ref[sl, :] = jnp.dot(_tri(False), dcum, preferred_element_type=F32, precision=lax.Precision.HIGHEST)
            carry_ref[...] = g * decay

    rev = lambda b: nb - 1 - b
    return pl.pallas_call(
        body, name=_nm("gla_bwd"), grid=(heads, nb),
        out_shape=(jax.ShapeDtypeStruct((S, heads * GLA_DK), F32), jax.ShapeDtypeStruct((S, heads * GLA_DK), F32),
                   jax.ShapeDtypeStruct((S, heads * GLA_DV), F32), jax.ShapeDtypeStruct((S, heads * GLA_DK), F32)),
        in_specs=_gla_specs(heads, rev) + [
            pl.BlockSpec((1, cpb, GLA_DV, GLA_DK), lambda h, b: (h, rev(b), 0, 0)),
            pl.BlockSpec((GLA_ROWS, GLA_DV), lambda h, b: (rev(b), h))],
        out_specs=(pl.BlockSpec((GLA_ROWS, GLA_DK), lambda h, b: (rev(b), h)),
                   pl.BlockSpec((GLA_ROWS, GLA_DK), lambda h, b: (rev(b), h)),
                   pl.BlockSpec((GLA_ROWS, GLA_DV), lambda h, b: (rev(b), h)),
                   pl.BlockSpec((GLA_ROWS, GLA_DK), lambda h, b: (rev(b), h))),
        scratch_shapes=[pltpu.VMEM((GLA_DV, GLA_DK), F32)],
        compiler_params=_cp(("parallel", "arbitrary")),
    )(hm, hm, hm, la, sprev, do)


@functools.partial(jax.custom_vjp, nondiff_argnums=(3,))
def gla_core(hm, la, o_norm, heads):
    return _gla_core_fwd(hm, la, o_norm, heads)[0]


def _gla_core_fwd(hm, la, o_norm, heads):
    o, sprev = _gla_fwd_call(hm, la, heads)
    vd = heads * GLA_DV
    rows = [(o, vd, 0), (hm, vd, 2 * heads * GLA_DK // vd + 1)]
    (y,) = _rw_fwd(_gla_out_fn(heads), rows, [o_norm], [BF16], "gla_out_fwd")
    return y, (hm, la, o_norm, o, sprev)


def _gla_core_bwd(heads, res, dy):
    hm, la, o_norm, o, sprev = res
    vd = heads * GLA_DV
    rows = [(o, vd, 0), (hm, vd, 2 * heads * GLA_DK // vd + 1)]
    (do, dr), (dg,) = _rw_bwd(_gla_out_fn(heads), rows, [o_norm], [dy], [F32, F32], "gla_out_bwd")
    dq, dk, dv, dla = _gla_bwd_call(hm, la, sprev, do, heads)
    return jnp.concatenate([dq, dk, dv, dr], axis=1), dla, dg


gla_core.defvjp(_gla_core_fwd, _gla_core_bwd)


CONV_COLS = 256
HALO = 8


def _conv_rows(S):
    return min(512, S)


def _conv_taps(main_ref, halo_ref, i):
    prev = jnp.where(i > 0, halo_ref[...], 0.0)
    full = jnp.concatenate([prev, main_ref[...]], axis=0)
    return full[HALO:], pltpu.roll(full, 1, 0)[HALO:], pltpu.roll(full, 2, 0)[HALO:]


def _conv_apply(taps, w_ref, b_ref):
    x0, x1, x2 = taps
    return x2 * w_ref[0:1, :] + x1 * w_ref[1:2, :] + x0 * w_ref[2:3, :] + b_ref[...]


def _gelu_gate(uc, gc):
    return uc * jax.nn.gelu(gc)


def _conv_in_specs(R, nj):
    hpr = R // HALO
    main = pl.BlockSpec((R, CONV_COLS), lambda j, i: (i, j))
    halo = pl.BlockSpec((HALO, CONV_COLS), lambda j, i: (jnp.maximum(i * hpr - 1, 0), j))
    par = lambda rows, off: pl.BlockSpec((rows, CONV_COLS), lambda j, i: (0, j + off))
    return [main, halo, main, halo, par(CONV_W, 0), par(CONV_W, nj), par(1, 0), par(1, nj)]


def _conv_fwd_call(hu, hg, cw, cb):
    S, dff = hu.shape
    R, nj = _conv_rows(S), dff // CONV_COLS

    def body(u_ref, uh_ref, g_ref, gh_ref, wu_ref, wg_ref, bu_ref, bg_ref, a_ref):
        i = pl.program_id(1)
        uc = _conv_apply(_conv_taps(u_ref, uh_ref, i), wu_ref, bu_ref)
        gc = _conv_apply(_conv_taps(g_ref, gh_ref, i), wg_ref, bg_ref)
        a_ref[...] = _gelu_gate(uc, gc).astype(a_ref.dtype)

    return pl.pallas_call(
        body, name=_nm("conv_fwd"), grid=(nj, S // R), out_shape=jax.ShapeDtypeStruct((S, dff), BF16),
        in_specs=_conv_in_specs(R, nj), out_specs=pl.BlockSpec((R, CONV_COLS), lambda j, i: (i, j)),
        compiler_params=_cp(("parallel", "parallel")),
    )(hu, hu, hg, hg, cw, cw, cb, cb)


def _conv_bwd_gate_call(hu, hg, cw, cb, da):
    S, dff = hu.shape
    R, nj = _conv_rows(S), dff // CONV_COLS

    def body(u_ref, uh_ref, g_ref, gh_ref, wu_ref, wg_ref, bu_ref, bg_ref, da_ref,
             du_ref, dg_ref, dwu_ref, dwg_ref, dbu_ref, dbg_ref):
        i = pl.program_id(1)
        ut, gt = _conv_taps(u_ref, uh_ref, i), _conv_taps(g_ref, gh_ref, i)
        uc, gc = _conv_apply(ut, wu_ref, bu_ref), _conv_apply(gt, wg_ref, bg_ref)
        _, vjp_fn = jax.vjp(_gelu_gate, uc, gc)
        du, dg = vjp_fn(da_ref[...].astype(F32))
        du_ref[...] = du
        dg_ref[...] = dg

        @pl.when(i == 0)
        def _():
            for r in (dwu_ref, dwg_ref, dbu_ref, dbg_ref):
                r[...] = jnp.zeros(r.shape, F32)

        for d, taps, dw_ref, db_ref in ((du, ut, dwu_ref, dbu_ref), (dg, gt, dwg_ref, dbg_ref)):
            x0, x1, x2 = taps
            dw_ref[0:1, :] += jnp.sum(d * x2, axis=0, keepdims=True)
            dw_ref[1:2, :] += jnp.sum(d * x1, axis=0, keepdims=True)
            dw_ref[2:3, :] += jnp.sum(d * x0, axis=0, keepdims=True)
            db_ref[...] += jnp.sum(d, axis=0, keepdims=True)

    tile = pl.BlockSpec((R, CONV_COLS), lambda j, i: (i, j))
    par = lambda rows: pl.BlockSpec((rows, CONV_COLS), lambda j, i: (0, j))
    return pl.pallas_call(
        body, name=_nm("conv_bwd_gate"), grid=(nj, S // R),
        out_shape=(jax.ShapeDtypeStruct((S, dff), F32), jax.ShapeDtypeStruct((S, dff), F32),
                   jax.ShapeDtypeStruct((CONV_W, dff), F32), jax.ShapeDtypeStruct((CONV_W, dff), F32),
                   jax.ShapeDtypeStruct((1, dff), F32), jax.ShapeDtypeStruct((1, dff), F32)),
        in_specs=_conv_in_specs(R, nj) + [tile],
        out_specs=(tile, tile, par(CONV_W), par(CONV_W), par(1), par(1)),
        compiler_params=_cp(("parallel", "arbitrary")),
    )(hu, hu, hg, hg, cw, cw, cb, cb, da)


def _conv_bwd_shift_call(dc, cw, col_off, out_dtype):
    S, dff = dc.shape
    R, nj = _conv_rows(S), dff // CONV_COLS
    hpr, last = R // HALO, S // HALO - 1
    ni = S // R

    def body(d_ref, nx_ref, w_ref, o_ref):
        i = pl.program_id(1)
        nxt = jnp.where(i < ni - 1, nx_ref[...], 0.0)
        full = jnp.concatenate([d_ref[...], nxt], axis=0)
        n = R + HALO
        y1, y2 = pltpu.roll(full, n - 1, 0)[:R], pltpu.roll(full, n - 2, 0)[:R]
        o_ref[...] = (full[:R] * w_ref[2:3, :] + y1 * w_ref[1:2, :] + y2 * w_ref[0:1, :]).astype(o_ref.dtype)

    return pl.pallas_call(
        body, name=_nm("conv_bwd_shift"), grid=(nj, ni), out_shape=jax.ShapeDtypeStruct((S, dff), out_dtype),
        in_specs=[pl.BlockSpec((R, CONV_COLS), lambda j, i: (i, j)),
                  pl.BlockSpec((HALO, CONV_COLS), lambda j, i: (jnp.minimum((i + 1) * hpr, last), j)),
                  pl.BlockSpec((CONV_W, CONV_COLS), lambda j, i: (0, j + col_off))],
        out_specs=pl.BlockSpec((R, CONV_COLS), lambda j, i: (i, j)),
        compiler_params=_cp(("parallel", "parallel")),
    )(dc, dc, cw)


@jax.custom_vjp
def ffn_hidden(x1, w3, cw, cb):
    return _ffn_hidden_fwd(x1, w3, cw, cb)[0]


def _ffn_hidden_fwd(x1, w3, cw, cb):
    half = w3.shape[0] // 2
    hu = _mm(x1, w3, "nn", F32, "up_u", (0, half))
    hg = _mm(x1, w3, "nn", F32, "up_g", (half, half))
    return _conv_fwd_call(hu, hg, cw, cb), (x1, w3, cw, cb, hu, hg)


def _ffn_hidden_bwd(res, da):
    x1, w3, cw, cb, hu, hg = res
    half = w3.shape[0] // 2
    nj = hu.shape[1] // CONV_COLS
    du, dg, dwu, dwg, dbu, dbg = _conv_bwd_gate_call(hu, hg, cw, cb, da)
    dhu = _conv_bwd_shift_call(du, cw, 0, BF16)
    dhg = _conv_bwd_shift_call(dg, cw, nj, BF16)
    dx = _mm(dhu, w3, "nt", F32, "up_dx_u", (0, half)) + _mm(dhg, w3, "nt", F32, "up_dx_g", (half, half))
    dw3 = jnp.concatenate([_mm(x1, dhu, "tn", w3.dtype, "up_dw_u", (0, half)),
                           _mm(x1, dhg, "tn", w3.dtype, "up_dw_g", (half, half))], axis=0)
    return dx, dw3, jnp.concatenate([dwu, dwg], axis=1), jnp.concatenate([dbu, dbg], axis=1)


ffn_hidden.defvjp(_ffn_hidden_fwd, _ffn_hidden_bwd)


def _loss_call(y, target):
    S, D = y.shape
    tr = _row_tile(S, D)

    def body(y_ref, t_ref, sq_ref, dy_ref):
        diff = y_ref[...] - t_ref[...]
        dy_ref[...] = diff * (1.0 / D)
        part = jnp.sum(diff * diff, axis=0, keepdims=True)
        i = pl.program_id(0)

        @pl.when(i == 0)
        def _():
            sq_ref[...] = part

        @pl.when(i > 0)
        def _():
            sq_ref[...] += part

    row = pl.BlockSpec((tr, D), lambda i: (i, 0))
    return pl.pallas_call(
        body, name=_nm("loss"), grid=(S // tr,),
        out_shape=(jax.ShapeDtypeStruct((1, D), F32), jax.ShapeDtypeStruct((S, D), F32)),
        in_specs=[row, row], out_specs=(pl.BlockSpec((1, D), lambda i: (0, 0)), row),
        compiler_params=_cp(("arbitrary",)),
    )(y, target)


def _row_divisor(rows):
    for cand in range(min(rows, 512), 15, -1):
        if rows % cand == 0 and cand % 16 == 0:
            return cand
    return rows


def _adamw_call(w, g, m, v):
    shape = w.shape
    w2, g2, m2, v2 = (a.reshape(-1, shape[-1]) for a in (w, g, m, v))
    rows, cols = w2.shape
    tr = _row_divisor(rows)

    def body(w_ref, g_ref, m_ref, v_ref, d_ref, nm_ref, nv_ref):
        g_ = g_ref[...]
        m_ = ADAM_B1 * m_ref[...] + (1.0 - ADAM_B1) * g_
        v_ = ADAM_B2 * v_ref[...] + (1.0 - ADAM_B2) * (g_ * g_)
        m_hat = m_ / (1.0 - ADAM_B1 ** ADAM_STEP)
        v_hat = v_ / (1.0 - ADAM_B2 ** ADAM_STEP)
        d_ref[...] = -ADAM_LR * (m_hat / (jnp.sqrt(v_hat) + ADAM_EPS) + ADAM_WD * w_ref[...])
        nm_ref[...] = m_
        nv_ref[...] = v_

    blk = pl.BlockSpec((tr, cols), lambda i: (i, 0))
    outs = pl.pallas_call(
        body, name=_nm("adamw"), grid=(rows // tr,),
        out_shape=tuple(jax.ShapeDtypeStruct((rows, cols), F32) for _ in range(3)),
        in_specs=[blk] * 4, out_specs=(blk,) * 3, compiler_params=_cp(("parallel",)),
    )(w2, g2, m2, v2)
    return tuple(o.reshape(shape) for o in outs)


ANY = pl.BlockSpec(memory_space=pl.ANY)


def _place():
    return lax.axis_index("x"), lax.axis_index("y"), lax.axis_index("c")


def all_gather(shards):
    n = len(shards)

    def body(*refs):
        x_refs, out_refs = refs[:n], refs[n:2 * n]
        send_sems, recv_sems, local_sems = refs[2 * n:]
        x, y, c = _place()
        me, sibling = (x, y, c), (x, y, 1 - c)
        chips = [(1 - x, y), (x, 1 - y), (1 - x, 1 - y)]

        def slot(a, px, py, pc):
            return out_refs[a].at[:, 4 * px + 2 * py + pc]

        def copy(a, k, block, to, own=False):
            return pltpu.make_async_remote_copy(
                src_ref=x_refs[a] if own else slot(a, *block), dst_ref=slot(a, *block),
                send_sem=send_sems.at[7 * a + k], recv_sem=recv_sems.at[7 * a + k], device_id=to, device_id_type=MESH)

        mine = [pltpu.make_async_copy(x_refs[a], slot(a, *me), local_sems.at[a]) for a in range(n)]
        first = []
        for a in range(n):
            mine[a].start()
            first.append(copy(a, 0, me, sibling, own=True))
            first += [copy(a, 1 + j, me, (*chip, c), own=True) for j, chip in enumerate(chips)]
        for cp in first:
            cp.start()
        passed = []
        for j, chip in enumerate(chips):
            for a in range(n):
                copy(a, 1 + j, (*chip, c), me).wait_recv()
                passed.append(copy(a, 4 + j, (*chip, c), sibling))
                passed[-1].start()
        for a in range(n):
            copy(a, 0, sibling, me).wait_recv()
        for j, chip in enumerate(chips):
            for a in range(n):
                copy(a, 4 + j, (*chip, 1 - c), me).wait_recv()
        for cp in first + passed:
            cp.wait_send()
        for cp in mine:
            cp.wait()

    return pl.pallas_call(
        body, name=_nm("all_gather"),
        out_shape=tuple(jax.ShapeDtypeStruct((s.shape[0], N_DEV) + s.shape[1:], s.dtype) for s in shards),
        in_specs=[ANY] * n, out_specs=(ANY,) * n,
        scratch_shapes=[pltpu.SemaphoreType.DMA((7 * n,)), pltpu.SemaphoreType.DMA((7 * n,)), pltpu.SemaphoreType.DMA((n,))],
    )(*shards)


def _rs_pair_exchange(gs):
    n = len(gs)

    def body(*refs):
        g_refs, recv_refs = refs[:n], refs[n:2 * n]
        send_sems, recv_sems = refs[2 * n:]
        x, y, c = _place()
        copies = [pltpu.make_async_remote_copy(
            src_ref=g_refs[a].at[:, 2 * j + (1 - c)], dst_ref=recv_refs[a].at[j], send_sem=send_sems.at[4 * a + j],
            recv_sem=recv_sems.at[4 * a + j], device_id=(x, y, 1 - c), device_id_type=MESH)
            for a in range(n) for j in range(4)]
        for cp in copies:
            cp.start()
        for cp in copies:
            cp.wait_recv()
        for cp in copies:
            cp.wait_send()

    return pl.pallas_call(
        body, name=_nm("rs_pair"),
        out_shape=tuple(jax.ShapeDtypeStruct((4, g.shape[0]) + g.shape[2:], g.dtype) for g in gs),
        in_specs=[ANY] * n, out_specs=(ANY,) * n,
        scratch_shapes=[pltpu.SemaphoreType.DMA((4 * n,)), pltpu.SemaphoreType.DMA((4 * n,))],
    )(*gs)


def _rs_chip_exchange(ps):
    n = len(ps)

    def body(*refs):
        p_refs, recv_refs = refs[:n], refs[n:2 * n]
        send_sems, recv_sems = refs[2 * n:]
        x, y, c = _place()
        chips = [(1 - x, y), (x, 1 - y), (1 - x, 1 - y)]
        copies = [pltpu.make_async_remote_copy(
            src_ref=p_refs[a].at[2 * cx + cy], dst_ref=recv_refs[a].at[k], send_sem=send_sems.at[3 * a + k],
            recv_sem=recv_sems.at[3 * a + k], device_id=(cx, cy, c), device_id_type=MESH)
            for a in range(n) for k, (cx, cy) in enumerate(chips)]
        for cp in copies:
            cp.start()
        for cp in copies:
            cp.wait_recv()
        for cp in copies:
            cp.wait_send()

    return pl.pallas_call(
        body, name=_nm("rs_chip"),
        out_shape=tuple(jax.ShapeDtypeStruct((3,) + p.shape[1:], p.dtype) for p in ps),
        in_specs=[ANY] * n, out_specs=(ANY,) * n,
        scratch_shapes=[pltpu.SemaphoreType.DMA((3 * n,)), pltpu.SemaphoreType.DMA((3 * n,))],
    )(*ps)


def _rs_pair_add(g, recv, c_idx):
    L, _, a, b = g.shape
    ta = _row_divisor(a)

    def body(c_ref, g_ref, r_ref, o_ref):
        o_ref[...] = (g_ref[...].astype(F32) + r_ref[...].astype(F32)).astype(o_ref.dtype)

    grid_spec = pltpu.PrefetchScalarGridSpec(
        num_scalar_prefetch=1, grid=(4, L, a // ta),
        in_specs=[pl.BlockSpec((None, None, ta, b), lambda j, l, i, c_ref: (l, 2 * j + c_ref[0], i, 0)),
                  pl.BlockSpec((None, None, ta, b), lambda j, l, i, c_ref: (j, l, i, 0))],
        out_specs=pl.BlockSpec((None, None, ta, b), lambda j, l, i, c_ref: (j, l, i, 0)))
    return pl.pallas_call(
        body, name=_nm("rs_pair_add"), grid_spec=grid_spec, out_shape=jax.ShapeDtypeStruct((4, L, a, b), g.dtype),
        compiler_params=_cp(("parallel", "parallel", "parallel")),
    )(c_idx, g, recv)


def _rs_final_add(p1, recv, chip_idx):
    _, L, a, b = p1.shape
    ta = _row_divisor(a)

    def body(chip_ref, p_ref, r_ref, o_ref):
        acc = p_ref[...].astype(F32)
        for k in range(3):
            acc = acc + r_ref[k].astype(F32)
        o_ref[...] = acc

    grid_spec = pltpu.PrefetchScalarGridSpec(
        num_scalar_prefetch=1, grid=(L, a // ta),
        in_specs=[pl.BlockSpec((None, None, ta, b), lambda l, i, chip_ref: (chip_ref[0], l, i, 0)),
                  pl.BlockSpec((3, None, ta, b), lambda l, i, chip_ref: (0, l, i, 0))],
        out_specs=pl.BlockSpec((None, ta, b), lambda l, i, chip_ref: (l, i, 0)))
    return pl.pallas_call(
        body, name=_nm("rs_final_add"), grid_spec=grid_spec, out_shape=jax.ShapeDtypeStruct((L, a, b), F32),
        compiler_params=_cp(("parallel", "parallel")),
    )(chip_idx, p1, recv)


def reduce_scatter(gs):
    x, y, c = _place()
    c_idx = jnp.reshape(c, (1,)).astype(jnp.int32)
    chip_idx = jnp.reshape(2 * x + y, (1,)).astype(jnp.int32)
    recv1 = _rs_pair_exchange(gs)
    p1 = [_rs_pair_add(g, r, c_idx) for g, r in zip(gs, recv1)]
    recv2 = _rs_chip_exchange(p1)
    return [_rs_final_add(p, r, chip_idx) for p, r in zip(p1, recv2)]


def all_reduce_small(v):
    r, C = v.shape

    def body(v_ref, out_ref, buf_ref, send_sems, recv_sems):
        x, y, c = _place()
        my_id = 4 * x + 2 * y + c
        buf_ref[my_id] = v_ref[...]
        copies = []
        for k in range(1, N_DEV):
            fx, fy, fc = (k >> 2) & 1, (k >> 1) & 1, k & 1
            peer = (x ^ fx, y ^ fy, c ^ fc)
            copies.append(pltpu.make_async_remote_copy(
                src_ref=v_ref, dst_ref=buf_ref.at[my_id], send_sem=send_sems.at[k - 1], recv_sem=recv_sems.at[k - 1],
                device_id=peer, device_id_type=MESH))
        for cp in copies:
            cp.start()
        for cp in copies:
            cp.wait_recv()
        for cp in copies:
            cp.wait_send()
        acc = buf_ref[0]
        for d in range(1, N_DEV):
            acc = acc + buf_ref[d]
        out_ref[...] = acc

    vm = pl.BlockSpec(memory_space=pltpu.VMEM)
    return pl.pallas_call(
        body, name=_nm("all_reduce_small"), out_shape=jax.ShapeDtypeStruct((r, C), F32),
        in_specs=[vm], out_specs=vm,
        scratch_shapes=[pltpu.VMEM((N_DEV, r, C), F32), pltpu.SemaphoreType.DMA((7,)), pltpu.SemaphoreType.DMA((7,))],
    )(v)


def _pack(arrays, dtype, row_align):
    lead = arrays[0].shape[:-1]
    quantum = row_align * PACK_COLS
    parts, sizes = [], []
    for a in arrays:
        n = a.shape[-1]
        padded = _round_up(n, quantum)
        a = a.astype(dtype)
        if padded != n:
            a = jnp.pad(a, [(0, 0)] * len(lead) + [(0, padded - n)])
        parts.append(a.reshape(*lead, padded // PACK_COLS, PACK_COLS))
        sizes.append((n, padded // PACK_COLS))
    return jnp.concatenate(parts, axis=len(lead)), sizes


def _unpack(packed, sizes):
    lead = packed.shape[:-2]
    out, row = [], 0
    for n, rows in sizes:
        part = lax.slice_in_dim(packed, row, row + rows, axis=len(lead))
        out.append(part.reshape(*lead, rows * PACK_COLS)[..., :n])
        row += rows
    return out


def _unshard(gathered, axis):
    _, L, a, b = gathered.shape
    if axis == 1:
        return [gathered[:, l].reshape(N_DEV * a, b) for l in range(L)]
    return [jnp.transpose(gathered[:, l], (1, 0, 2)).reshape(a, N_DEV * b) for l in range(L)]


def _reshard(fulls, axis):
    blocks = []
    for f in fulls:
        A, B = f.shape
        if axis == 1:
            blocks.append(f.reshape(N_DEV, A // N_DEV, B))
        else:
            blocks.append(jnp.transpose(f.reshape(A, N_DEV, B // N_DEV), (1, 0, 2)))
    return jnp.stack(blocks, axis=1)


def _as3(a):
    return a if a.ndim == 3 else a[:, None, :]


def _prep_big(name, w, dims):
    w = w.astype(BF16)
    L, a, b = w.shape
    if name == 'mla_w_in':
        return jnp.pad(w, ((0, 0), (0, 0), (0, dims['h_width'] - b)))
    if name == 'mla_w_uq':
        hd = MLA_NOPE + MLA_ROPE
        w = jnp.pad(w.reshape(L, a, b // hd, hd), ((0, 0), (0, 0), (0, 0), (0, MLA_QK_PAD - hd)))
        return w.reshape(L, a, b // hd * MLA_QK_PAD)
    if name == 'ffn_w_up':
        return jnp.pad(w, ((0, 0), (0, 0), (0, _round_up(b, CONV_COLS) - b)))
    return w


def _unprep_big(name, g, shape):
    L, a, b = shape
    if name == 'mla_w_uq':
        hd = MLA_NOPE + MLA_ROPE
        return g.reshape(L, a, b // hd, MLA_QK_PAD)[..., :hd].reshape(L, a, b)
    return g[:, :, :b]


def _rope_tables(positions):
    inv = 1.0 / (ROPE_THETA ** (jnp.arange(0, MLA_ROPE, 2, dtype=F32) / MLA_ROPE))
    ang = positions.astype(F32)[:, None] * inv
    cos, sin = jnp.cos(ang), jnp.sin(ang)
    one, zero = jnp.ones_like(cos), jnp.zeros_like(cos)
    a = jnp.concatenate([cos, cos, one, one], axis=1)
    up = jnp.concatenate([sin, zero, zero, zero], axis=1)
    down = jnp.concatenate([zero, sin, zero, zero], axis=1)
    return (a, -up, down), (a, up, -down)


def _rows_full(w):
    return w.reshape(w.shape[0] * w.shape[1], w.shape[2])


def _forward(x, wl, p, aux):
    depth = len(wl['ln1_g'])
    alpha = (2 * depth) ** 0.25
    ln_res = rw_op(_ln_res_fn(alpha), "ln_res", 2, [F32])
    ple_op = rw_op(_ple_fn, "ple", 3, [F32])
    gla_gate = rw_op(_gla_gate_fn, "gla_gate", 1, [F32])
    for i in range(depth):
        j = i // 2
        if i % 2 == 0:
            w_uq, w_uk, w_uv = wl['mla_w_uq'][j], wl['mla_w_uk'][j], wl['mla_w_uv'][j]
            q_lora, kv_lora = w_uq.shape[1], w_uk.shape[1]
            heads = N_DEV * w_uk.shape[2] // MLA_NOPE
            h = linear(x, _rows_full(wl['mla_w_in'][j]), F32)
            mla_norm = rw_op(_mla_norm_fn(q_lora, kv_lora), "mla_norm", 1, [BF16, BF16, F32])
            cq, ckv, kr_raw = mla_norm(h, wl['mla_q_norm'][j], wl['mla_kv_norm'][j])
            q_raw = linear(cq, w_uq, F32)
            kn = linear(ckv, w_uk, BF16)
            v = linear(ckv, w_uv, BF16)
            o = make_attention(aux, heads)(q_raw, kn, v, kr_raw)
            m = linear(o, _rows_full(wl['mla_w_o'][j]), F32)
        else:
            w_in3, w_a2 = wl['gla_w_in'][j], wl['gla_w_a2'][j]
            w_o = _rows_full(wl['gla_w_o'][j])
            w_in = jnp.transpose(w_in3, (1, 0, 2)).reshape(w_in3.shape[1], N_DEV * w_in3.shape[2])
            heads = w_o.shape[0] // GLA_DV
            n_main = 2 * heads * GLA_DK + 2 * heads * GLA_DV
            w_main = w_in[:, :n_main]
            w_a = jnp.pad(w_in[:, n_main:], ((0, 0), (0, LANES - GLA_RANK)))
            w_a2_p = jnp.pad(w_a2, ((0, LANES - GLA_RANK), (0, 0))).astype(BF16)
            hm = linear(x, w_main, F32)
            ha = linear(x, w_a, BF16)
            (la,) = gla_gate(linear(ha, w_a2_p, F32), wl['gla_b_a'][j])
            yg = gla_core(hm, la, wl['gla_o_norm'][j], heads)
            m = linear(yg, w_o, F32)
        (x1,) = ln_res(x, m, wl['ln1_g'][i], wl['ln1_b'][i])
        w_up3, w_down3 = wl['ffn_w_up'][i], wl['ffn_w_down'][i]
        bp = w_up3.shape[2]
        cw, cb = wl['ffn_conv_w'][i], wl['ffn_conv_b'][i]
        bu = cw.shape[1] // N_DEV
        cwp = jnp.pad(cw.reshape(CONV_W, N_DEV, bu), ((0, 0), (0, 0), (0, bp - bu))).reshape(CONV_W, N_DEV * bp)
        cbp = jnp.pad(cb.reshape(1, N_DEV, bu), ((0, 0), (0, 0), (0, bp - bu))).reshape(1, N_DEV * bp)
        a = ffn_hidden(x1, w_up3, cwp, cbp)
        d_model = w_down3.shape[2]
        w_down = jnp.pad(w_down3.reshape(N_DEV // 2, bu, d_model), ((0, 0), (0, bp - bu), (0, 0)))
        f = linear(a, w_down.reshape(N_DEV // 2 * bp, d_model), F32)
        (x2,) = ln_res(x1, f, wl['ln2_g'][i], wl['ln2_b'][i])
        glog = linear(x2, _rows_full(wl['ple_w_gate'][i]), F32)
        pp = linear(p[i], wl['ple_w_proj'][i], F32)
        (x,) = ple_op(x2, glog, pp, wl['ple_b_gate'][i])
    return x


def kernel(x, p, positions, mla_w_in, mla_q_norm, mla_kv_norm, mla_w_uq, mla_w_uk, mla_w_uv, mla_w_o, gla_w_in, gla_w_a2, gla_b_a, gla_o_norm, gla_w_o, ln1_g, ln1_b, ln2_g, ln2_b, ffn_w_up, ffn_conv_w, ffn_conv_b, ffn_w_down, ple_w_proj, ple_w_gate, ple_b_gate, loss_target, m_mla_w_in, m_mla_q_norm, m_mla_kv_norm, m_mla_w_uq, m_mla_w_uk, m_mla_w_uv, m_mla_w_o, m_gla_w_in, m_gla_w_a2, m_gla_b_a, m_gla_o_norm, m_gla_w_o, m_ln1_g, m_ln1_b, m_ln2_g, m_ln2_b, m_ffn_w_up, m_ffn_conv_w, m_ffn_conv_b, m_ffn_w_down, m_ple_w_proj, m_ple_w_gate, m_ple_b_gate, v_mla_w_in, v_mla_q_norm, v_mla_kv_norm, v_mla_w_uq, v_mla_w_uk, v_mla_w_uv, v_mla_w_o, v_gla_w_in, v_gla_w_a2, v_gla_b_a, v_gla_o_norm, v_gla_w_o, v_ln1_g, v_ln1_b, v_ln2_g, v_ln2_b, v_ffn_w_up, v_ffn_conv_w, v_ffn_conv_b, v_ffn_w_down, v_ple_w_proj, v_ple_w_gate, v_ple_b_gate):
    w = dict(zip(WEIGHTS, (mla_w_in, mla_q_norm, mla_kv_norm, mla_w_uq, mla_w_uk, mla_w_uv, mla_w_o, gla_w_in, gla_w_a2,
                           gla_b_a, gla_o_norm, gla_w_o, ln1_g, ln1_b, ln2_g, ln2_b, ffn_w_up, ffn_conv_w, ffn_conv_b,
                           ffn_w_down, ple_w_proj, ple_w_gate, ple_b_gate)))
    m_in = dict(zip(WEIGHTS, (m_mla_w_in, m_mla_q_norm, m_mla_kv_norm, m_mla_w_uq, m_mla_w_uk, m_mla_w_uv, m_mla_w_o,
                              m_gla_w_in, m_gla_w_a2, m_gla_b_a, m_gla_o_norm, m_gla_w_o, m_ln1_g, m_ln1_b, m_ln2_g,
                              m_ln2_b, m_ffn_w_up, m_ffn_conv_w, m_ffn_conv_b, m_ffn_w_down, m_ple_w_proj, m_ple_w_gate,
                              m_ple_b_gate)))
    v_in = dict(zip(WEIGHTS, (v_mla_w_in, v_mla_q_norm, v_mla_kv_norm, v_mla_w_uq, v_mla_w_uk, v_mla_w_uv, v_mla_w_o,
                              v_gla_w_in, v_gla_w_a2, v_gla_b_a, v_gla_o_norm, v_gla_w_o, v_ln1_g, v_ln1_b, v_ln2_g,
                              v_ln2_b, v_ffn_w_up, v_ffn_conv_w, v_ffn_conv_b, v_ffn_w_down, v_ple_w_proj, v_ple_w_gate,
                              v_ple_b_gate)))
    _uid[0] = itertools.count()
    x2d, target, pos = x[0], loss_target[0], positions[0]
    p3 = p[:, 0]
    dims = {'h_width': mla_w_uq.shape[1] + mla_w_uk.shape[1] + LANES}

    small3 = [_as3(w[n]) for n in SMALL]
    small_packed, small_sizes = _pack([s.reshape(1, -1) for s in small3], F32, 8)
    gathered = all_gather([_prep_big(n, w[n], dims) for n in BIG] + [small_packed])
    wl = {n: [g[l] for l in range(g.shape[0])] for n, g in zip(BIG, gathered)}
    for n, s3, flat in zip(SMALL, small3, _unpack(gathered[-1][0], small_sizes)):
        wl[n] = _unshard(flat.reshape(N_DEV, *s3.shape), SHARD_AXIS[n] if w[n].ndim == 3 else 2)
    for n in REPL:
        wl[n] = [w[n][l][None, :] for l in range(w[n].shape[0])]

    cid = pos // CHUNK
    aux = {'rope': _rope_tables(pos), 'cidq': cid[:, None], 'cidk': cid[None, :],
           'need': _mask_table(cid, _attn_tile(pos.shape[0]))}

    y, vjp_fn = jax.vjp(lambda x_, wl_: _forward(x_, wl_, p3, aux), x2d, wl)
    sq, dy = _loss_call(y, target)
    dx, dwl = vjp_fn(dy)

    small_blocks = [_reshard(dwl[n], SHARD_AXIS[n] if w[n].ndim == 3 else 2).reshape(N_DEV, -1) for n in SMALL]
    small_grad_packed, _ = _pack(small_blocks, F32, 8)
    reduced = reduce_scatter([jnp.stack(dwl[n], axis=0) for n in BIG] + [small_grad_packed[None]])
    grads = {n: _unprep_big(n, g, w[n].shape) for n, g in zip(BIG, reduced)}
    for n, f in zip(SMALL, _unpack(reduced[-1][0], small_sizes)):
        grads[n] = f.reshape(w[n].shape)

    repl_flat = [jnp.concatenate([g.reshape(-1) for g in dwl[n]]).reshape(1, -1) for n in REPL]
    loss_part = 0.5 * jnp.sum(sq) / sq.shape[1]
    packed, repl_sizes = _pack(repl_flat + [loss_part.reshape(1, 1)], F32, 8)
    summed = _unpack(all_reduce_small(packed[0])[None], repl_sizes)
    for n, f in zip(REPL, summed[:-1]):
        grads[n] = f.reshape(w[n].shape)
    loss = summed[-1].reshape(())

    delta, new_m, new_v = {}, {}, {}
    for n in WEIGHTS:
        delta[n], new_m[n], new_v[n] = _adamw_call(w[n], grads[n], m_in[n], v_in[n])
    return (loss, dx[None], *[grads[n] for n in WEIGHTS], *[delta[n] for n in WEIGHTS],
            *[new_m[n] for n in WEIGHTS], *[new_v[n] for n in WEIGHTS])
```

```python
import functools
import itertools

import jax
import jax.numpy as jnp
from jax import lax
from jax.experimental import pallas as pl
from jax.experimental.pallas import tpu as pltpu

F32 = jnp.float32
BF16 = jnp.bfloat16
MESH = pl.DeviceIdType.MESH
N_DEV = 8

EPS = 1e-5
NEG_INF = -1e30
CHUNK = 64
Q_BLOCK = 128
MLA_NOPE = 128
MLA_ROPE = 64
MLA_V = 128
MLA_QK_PAD = 256
ROPE_THETA = 10000.0
GLA_DK = 128
GLA_DV = 256
GLA_RANK = 16
GLA_TAU = 16.0
CONV_W = 3
ADAM_LR = 0.001
ADAM_B1 = 0.9
ADAM_B2 = 0.999
ADAM_EPS = 1e-08
ADAM_WD = 0.01
ADAM_STEP = 10
LOG2E = 1.4426950408889634

LANES = 128
PACK_COLS = 1024
VMEM_LIMIT = 48 * 1024 * 1024
MM_VMEM_BUDGET = 30 * 1024 * 1024

WEIGHTS = ['mla_w_in', 'mla_q_norm', 'mla_kv_norm', 'mla_w_uq', 'mla_w_uk', 'mla_w_uv', 'mla_w_o', 'gla_w_in',
           'gla_w_a2', 'gla_b_a', 'gla_o_norm', 'gla_w_o', 'ln1_g', 'ln1_b', 'ln2_g', 'ln2_b', 'ffn_w_up',
           'ffn_conv_w', 'ffn_conv_b', 'ffn_w_down', 'ple_w_proj', 'ple_w_gate', 'ple_b_gate']
SHARD_AXIS = {'mla_w_in': 1, 'mla_q_norm': None, 'mla_kv_norm': None, 'mla_w_uq': 2, 'mla_w_uk': 2, 'mla_w_uv': 2,
              'mla_w_o': 1, 'gla_w_in': 2, 'gla_w_a2': 2, 'gla_b_a': 1, 'gla_o_norm': 1, 'gla_w_o': 1,
              'ln1_g': None, 'ln1_b': None, 'ln2_g': None, 'ln2_b': None, 'ffn_w_up': 2, 'ffn_conv_w': 2,
              'ffn_conv_b': None, 'ffn_w_down': 1, 'ple_w_proj': 2, 'ple_w_gate': 1, 'ple_b_gate': None}
BIG = ['mla_w_in', 'mla_w_uq', 'mla_w_uk', 'mla_w_uv', 'mla_w_o', 'gla_w_in', 'gla_w_o', 'ffn_w_up', 'ffn_w_down',
       'ple_w_proj', 'ple_w_gate']
SMALL = ['gla_w_a2', 'gla_b_a', 'gla_o_norm', 'ffn_conv_w']
REPL = [n for n in WEIGHTS if SHARD_AXIS[n] is None]

_uid = [itertools.count()]


def _nm(base):
    return f"{base}_{next(_uid[0])}"


def _cp(sem=None):
    return pltpu.CompilerParams(dimension_semantics=sem, vmem_limit_bytes=VMEM_LIMIT)


def _round_up(n, m):
    return -(-n // m) * m


class Comm:
    def __init__(self, inputs, out_shapes, aliases, n_send, n_recv, n_local, build):
        self.inputs, self.out_shapes, self.aliases = list(inputs), list(out_shapes), dict(aliases)
        self.n_send, self.n_recv, self.n_local, self.build = n_send, n_recv, n_local, build


def _pcall(body, name, grid, n_prefetch, in_specs, out_specs, out_shape, scratch, sem, args, comm=None):
    in_specs, out_specs, out_shape, scratch, args = list(in_specs), list(out_specs), list(out_shape), list(scratch), list(args)
    n_in, n_out, n_scr = len(in_specs), len(out_specs), len(scratch)
    aliases = {}
    kernel_body = body
    if comm is not None:
        ci, co = len(comm.inputs), len(comm.out_shapes)
        any_spec = pl.BlockSpec(memory_space=pl.ANY)
        in_specs += [any_spec] * ci
        out_specs += [any_spec] * co
        out_shape += comm.out_shapes
        scratch += [pltpu.SemaphoreType.DMA((comm.n_send,)), pltpu.SemaphoreType.DMA((comm.n_recv,)),
                    pltpu.SemaphoreType.DMA((comm.n_local,))]
        aliases = {n_prefetch + n_in + k: n_out + v for k, v in comm.aliases.items()}
        args += comm.inputs
        sem = ("arbitrary",) * len(grid)

        def kernel_body(*refs):
            pre, r = refs[:n_prefetch], refs[n_prefetch:]
            ins, cin = r[:n_in], r[n_in:n_in + ci]
            outs, cout = r[n_in + ci:n_in + ci + n_out], r[n_in + ci + n_out:n_in + ci + n_out + co]
            scr = r[n_in + ci + n_out + co:n_in + ci + n_out + co + n_scr]
            send_sems, recv_sems, local_sems = r[-3:]
            first = functools.reduce(lambda a, b: a & b, [pl.program_id(d) == 0 for d in range(len(grid))])
            last = functools.reduce(lambda a, b: a & b, [pl.program_id(d) == grid[d] - 1 for d in range(len(grid))])
            starts, waits = comm.build(cin, cout, send_sems, recv_sems, local_sems)

            @pl.when(first)
            def _():
                for cp in starts:
                    cp.start()

            body(*pre, *ins, *outs, *scr)

            @pl.when(last)
            def _():
                for wait in waits:
                    wait()

    grid_spec = pltpu.PrefetchScalarGridSpec(num_scalar_prefetch=n_prefetch, grid=grid, in_specs=in_specs,
                                             out_specs=out_specs, scratch_shapes=scratch)
    return pl.pallas_call(kernel_body, name=_nm(name), grid_spec=grid_spec, out_shape=tuple(out_shape),
                          input_output_aliases=aliases, compiler_params=_cp(sem))(*args)


def _divisor_tiles(n, cap):
    if n % LANES:
        return [n]
    out = [t for t in range(LANES, min(n, cap) + 1, LANES) if n % t == 0]
    return out or [n]


def _mm_tiles(M, N, K, abytes, bbytes, obytes, tn_fixed=None, tk_fixed=None):
    best = None
    for tm in _divisor_tiles(M, 1024):
        for tn in ([tn_fixed] if tn_fixed else _divisor_tiles(N, 1536)):
            for tk in ([tk_fixed] if tk_fixed else _divisor_tiles(K, 2048)):
                vmem = 2 * (tm * tk * abytes + tk * tn * bbytes + tm * tn * obytes) + tm * tn * 4
                if vmem > MM_VMEM_BUDGET:
                    continue
                key = (tm * tn * tk, tk)
                if best is None or key > best[0]:
                    best = (key, (tm, tn, tk))
    assert best is not None, (M, N, K)
    return best[1]


def _mm(a, b, mode, out_dtype, base="mm", blocks=None):
    blk0, nblk = blocks if blocks else (0, 1)
    tn_fixed = tk_fixed = None
    if mode == "nn":
        M, K = a.shape
        N = nblk * b.shape[2] if blocks else b.shape[1]
        tn_fixed = b.shape[2] if blocks else None
    elif mode == "nt":
        M, K = a.shape
        N = b.shape[1] if blocks else b.shape[0]
        tk_fixed = b.shape[2] if blocks else None
        assert not blocks or K == nblk * b.shape[2]
    else:
        (K, M), N = a.shape, b.shape[1]
        tn_fixed = N // nblk if blocks else None
    tm, tn, tk = _mm_tiles(M, N, K, a.dtype.itemsize, b.dtype.itemsize, jnp.dtype(out_dtype).itemsize, tn_fixed, tk_fixed)
    nk = K // tk
    out_shape = jax.ShapeDtypeStruct((M, N), out_dtype)
    out_spec = pl.BlockSpec((tm, tn), lambda i, j, k: (i, j))
    if mode == "nn":
        a_spec = pl.BlockSpec((tm, tk), lambda i, j, k: (i, k))
        b_spec = (pl.BlockSpec((None, tk, tn), lambda i, j, k: (blk0 + j, k, 0)) if blocks
                  else pl.BlockSpec((tk, tn), lambda i, j, k: (k, j)))
        dims = (((1,), (0,)), ((), ()))
    elif mode == "nt":
        a_spec = pl.BlockSpec((tm, tk), lambda i, j, k: (i, k))
        b_spec = (pl.BlockSpec((None, tn, tk), lambda i, j, k: (blk0 + k, j, 0)) if blocks
                  else pl.BlockSpec((tn, tk), lambda i, j, k: (j, k)))
        dims = (((1,), (1,)), ((), ()))
    else:
        a_spec = pl.BlockSpec((tk, tm), lambda i, j, k: (k, i))
        b_spec = pl.BlockSpec((tk, tn), lambda i, j, k: (k, j))
        dims = (((0,), (0,)), ((), ()))
        if blocks:
            out_shape = jax.ShapeDtypeStruct((nblk, M, tn), out_dtype)
            out_spec = pl.BlockSpec((None, tm, tn), lambda i, j, k: (j, i, 0))

    def body(a_ref, b_ref, o_ref, acc_ref):
        part = lax.dot_general(a_ref[...].astype(BF16), b_ref[...].astype(BF16), dims, preferred_element_type=F32)
        if nk == 1:
            o_ref[...] = part.astype(o_ref.dtype)
        else:
            k = pl.program_id(2)

            @pl.when(k == 0)
            def _():
                acc_ref[...] = part

            @pl.when(k > 0)
            def _():
                acc_ref[...] += part

            @pl.when(k == nk - 1)
            def _():
                o_ref[...] = acc_ref[...].astype(o_ref.dtype)

    return pl.pallas_call(
        body, name=_nm(base), grid=(M // tm, N // tn, nk), out_shape=out_shape,
        in_specs=[a_spec, b_spec], out_specs=out_spec,
        scratch_shapes=[pltpu.VMEM((tm, tn) if nk > 1 else (8, LANES), F32)],
        compiler_params=_cp(("parallel", "parallel", "arbitrary")),
    )(a, b)


def _all_blocks(w):
    return (0, w.shape[0]) if w.ndim == 3 else None


@functools.partial(jax.custom_vjp, nondiff_argnums=(2,))
def linear(a, w, out_dtype):
    return _mm(a, w, "nn", out_dtype, "lin_fwd", _all_blocks(w))


def _linear_fwd(a, w, out_dtype):
    return _mm(a, w, "nn", out_dtype, "lin_fwd", _all_blocks(w)), (a, w)


def _linear_bwd(out_dtype, res, dy):
    a, w = res
    return (_mm(dy, w, "nt", a.dtype, "lin_dx", _all_blocks(w)), _mm(a, dy, "tn", w.dtype, "lin_dw", _all_blocks(w)))


linear.defvjp(_linear_fwd, _linear_bwd)


def _row_tile(S, width):
    tr = 512 if width <= 1024 else 256
    return min(tr, S)


def _rw_fwd(f, rows, params, out_dtypes, base):
    S = rows[0][0].shape[0]
    tr = _row_tile(S, max(w for _, w, _ in rows))
    n_in = len(rows) + len(params)
    avals = [jax.ShapeDtypeStruct((tr, w), F32) for _, w, _ in rows] + [jax.ShapeDtypeStruct(p.shape, F32) for p in params]
    outs = jax.eval_shape(f, *avals)

    def body(*refs):
        vals = [r[...].astype(F32) for r in refs[:n_in]]
        for o_ref, r in zip(refs[n_in:], f(*vals)):
            o_ref[...] = r.astype(o_ref.dtype)

    in_specs = [pl.BlockSpec((tr, w), functools.partial(lambda i, cb: (i, cb), cb=cb)) for _, w, cb in rows]
    in_specs += [pl.BlockSpec(p.shape, lambda i: (0, 0)) for p in params]
    return pl.pallas_call(
        body, name=_nm(base), grid=(S // tr,),
        out_shape=tuple(jax.ShapeDtypeStruct((S, o.shape[1]), dt) for o, dt in zip(outs, out_dtypes)),
        in_specs=in_specs, out_specs=tuple(pl.BlockSpec((tr, o.shape[1]), lambda i: (i, 0)) for o in outs),
        compiler_params=_cp(("parallel",)),
    )(*[a for a, _, _ in rows], *params)


def _rw_bwd(f, rows, params, cts, row_grad_dtypes, base):
    S = rows[0][0].shape[0]
    tr = _row_tile(S, max(w for _, w, _ in rows))
    n_rows, n_par, n_ct = len(rows), len(params), len(cts)
    want = [k for k, dt in enumerate(row_grad_dtypes) if dt is not None]

    def body(*refs):
        in_refs = refs[:n_rows + n_par]
        ct_refs = refs[n_rows + n_par:n_rows + n_par + n_ct]
        out_refs = refs[n_rows + n_par + n_ct:]
        vals = [r[...].astype(F32) for r in in_refs]
        _, vjp_fn = jax.vjp(f, *vals)
        grads = vjp_fn(tuple(c[...].astype(F32) for c in ct_refs))
        for o_ref, k in zip(out_refs[:len(want)], want):
            o_ref[...] = grads[k].astype(o_ref.dtype)
        i = pl.program_id(0)
        for o_ref, g in zip(out_refs[len(want):], grads[n_rows:]):
            @pl.when(i == 0)
            def _(o_ref=o_ref, g=g):
                o_ref[...] = g

            @pl.when(i > 0)
            def _(o_ref=o_ref, g=g):
                o_ref[...] += g

    in_specs = [pl.BlockSpec((tr, w), functools.partial(lambda i, cb: (i, cb), cb=cb)) for _, w, cb in rows]
    in_specs += [pl.BlockSpec(p.shape, lambda i: (0, 0)) for p in params]
    in_specs += [pl.BlockSpec((tr, c.shape[1]), lambda i: (i, 0)) for c in cts]
    out_shape = [jax.ShapeDtypeStruct((S, rows[k][1]), row_grad_dtypes[k]) for k in want]
    out_specs = [pl.BlockSpec((tr, rows[k][1]), lambda i: (i, 0)) for k in want]
    out_shape += [jax.ShapeDtypeStruct(p.shape, F32) for p in params]
    out_specs += [pl.BlockSpec(p.shape, lambda i: (0, 0)) for p in params]
    res = pl.pallas_call(
        body, name=_nm(base), grid=(S // tr,), out_shape=tuple(out_shape),
        in_specs=in_specs, out_specs=tuple(out_specs), compiler_params=_cp(("arbitrary",)),
    )(*[a for a, _, _ in rows], *params, *cts)
    row_grads = [None] * n_rows
    for k, g in zip(want, res[:len(want)]):
        row_grads[k] = g
    return row_grads, list(res[len(want):])


def rw_op(f, base, n_rows, out_dtypes):
    @jax.custom_vjp
    def op(*args):
        return fwd(*args)[0]

    def split(args):
        rows = [(a, a.shape[1], 0) for a in args[:n_rows]]
        return rows, list(args[n_rows:])

    def fwd(*args):
        rows, params = split(args)
        return tuple(_rw_fwd(f, rows, params, out_dtypes, base + "_fwd")), args

    def bwd(args, cts):
        rows, params = split(args)
        rg, pg = _rw_bwd(f, rows, params, list(cts), [a.dtype for a, _, _ in rows], base + "_bwd")
        return tuple(rg) + tuple(g.astype(p.dtype) for g, p in zip(pg, params))

    op.defvjp(fwd, bwd)
    return op


def _ln_res_fn(alpha):
    def f(x, m, g, b):
        z = alpha * x + m
        mu = jnp.mean(z, -1, keepdims=True)
        zc = z - mu
        var = jnp.mean(zc * zc, -1, keepdims=True)
        return (zc * lax.rsqrt(var + EPS) * g + b,)
    return f


def _rms(x, g):
    return x * lax.rsqrt(jnp.mean(x * x, -1, keepdims=True) + EPS) * g


def _mla_norm_fn(q_lora, kv_lora):
    def f(h, qn, kvn):
        return (_rms(h[:, :q_lora], qn), _rms(h[:, q_lora:q_lora + kv_lora], kvn),
                h[:, q_lora + kv_lora:q_lora + kv_lora + LANES])
    return f


def _log_sigmoid(z):
    return jnp.minimum(z, 0.0) - jnp.log(1.0 + jnp.exp(-jnp.abs(z)))


def _gla_gate_fn(z, b):
    return (_log_sigmoid(z + b) / GLA_TAU,)


def _ple_fn(x, glog, pp, b):
    return (x + jax.nn.sigmoid(glog + b) * pp,)


def _gla_out_fn(heads):
    def f(o, r, g):
        parts = []
        for h in range(heads):
            oh = o[:, h * GLA_DV:(h + 1) * GLA_DV]
            mu = jnp.mean(oh, -1, keepdims=True)
            oc = oh - mu
            var = jnp.mean(oc * oc, -1, keepdims=True)
            parts.append(oc * lax.rsqrt(var + EPS) * g[:, h * GLA_DV:(h + 1) * GLA_DV])
        return (jnp.concatenate(parts, axis=1) * (r * jax.nn.sigmoid(r)),)
    return f


def _rope_call(x, tabs, roped, out_dtype, base, fold=False):
    S, C = x.shape
    nb = C // LANES
    tr = min(512 if C <= 1024 else 256, S)
    out_c = LANES if fold else C

    def rot(v, a, b1, b2):
        return v * a + pltpu.roll(v, 96, 1) * b1 + pltpu.roll(v, 32, 1) * b2

    def body(x_ref, a_ref, b1_ref, b2_ref, o_ref):
        a, b1, b2 = a_ref[...], b1_ref[...], b2_ref[...]
        if fold:
            v = x_ref[:, 0:LANES].astype(F32)
            for blk in range(1, nb):
                v = v + x_ref[:, blk * LANES:(blk + 1) * LANES].astype(F32)
            o_ref[...] = rot(v, a, b1, b2).astype(o_ref.dtype)
            return
        for blk in range(nb):
            v = x_ref[:, blk * LANES:(blk + 1) * LANES].astype(F32)
            if roped(blk):
                v = rot(v, a, b1, b2)
            o_ref[:, blk * LANES:(blk + 1) * LANES] = v.astype(o_ref.dtype)

    row = lambda w: pl.BlockSpec((tr, w), lambda i: (i, 0))
    return pl.pallas_call(
        body, name=_nm(base), grid=(S // tr,), out_shape=jax.ShapeDtypeStruct((S, out_c), out_dtype),
        in_specs=[row(C), row(LANES), row(LANES), row(LANES)], out_specs=row(out_c),
        compiler_params=_cp(("parallel",)),
    )(x, *tabs)


def _attn_tile(S):
    return min(512, S)


def _tri_schedule(n, by_key):
    pairs = ([(i, j) for j in range(n) for i in range(j, n)] if by_key
             else [(i, j) for i in range(n) for j in range(i + 1)])
    return (jnp.asarray([p[0] for p in pairs], jnp.int32), jnp.asarray([p[1] for p in pairs], jnp.int32))


def _mask_table(cid, t):
    n = cid.shape[0] // t
    blocks = cid.reshape(n, t)
    cmin_q, cmax_k = jnp.min(blocks, axis=1), jnp.max(blocks, axis=1)
    need = (cmax_k[None, :] > cmin_q[:, None]) | jnp.eye(n, dtype=bool)
    return need.astype(jnp.int32).reshape(n * n)


def _attn_mask(cidq_ref, cidk_ref, i, j, t):
    qrow = i * t + lax.broadcasted_iota(jnp.int32, (t, 1), 0)
    kcol = j * t + lax.broadcasted_iota(jnp.int32, (1, t), 1)
    qlim = (qrow // Q_BLOCK + 1) * Q_BLOCK
    return (cidk_ref[...] <= cidq_ref[...]) & (kcol < qlim)


ATTN_FWD_HEADS = 2


def _attn_fwd_call(q, kn, v, kr, aux, heads, comm=None):
    S = q.shape[0]
    t = _attn_tile(S)
    n = S // t
    hp = ATTN_FWD_HEADS if heads % ATTN_FWD_HEADS == 0 else 1
    qi_tab, kj_tab = _tri_schedule(n, False)
    scale2 = (MLA_NOPE + MLA_ROPE) ** -0.5 * LOG2E

    def body(qi_ref, kj_ref, need_ref, q_ref, kn_ref, v_ref, kr_ref, cidq_ref, cidk_ref, o_ref, lse_ref,
             m_ref, l_ref, acc_ref):
        st = pl.program_id(1)
        i, j = qi_ref[st], kj_ref[st]

        @pl.when(j == 0)
        def _():
            m_ref[...] = jnp.full(m_ref.shape, NEG_INF, F32)
            l_ref[...] = jnp.zeros(l_ref.shape, F32)
            acc_ref[...] = jnp.zeros(acc_ref.shape, F32)

        def update(masked):
            mask = _attn_mask(cidq_ref, cidk_ref, i, j, t) if masked else None
            for hh in range(hp):
                lanes = slice(hh * LANES, (hh + 1) * LANES)
                k = jnp.concatenate([kn_ref[:, lanes], kr_ref[...]], axis=1)
                qh = q_ref[:, hh * MLA_QK_PAD:(hh + 1) * MLA_QK_PAD]
                s = lax.dot_general(qh, k, (((1,), (1,)), ((), ())), preferred_element_type=F32) * scale2
                if masked:
                    s = jnp.where(mask, s, NEG_INF)
                m_prev = m_ref[:, lanes]
                m_new = jnp.maximum(m_prev, jnp.max(s, axis=1, keepdims=True))
                alpha = jnp.exp2(m_prev - m_new)
                p = jnp.exp2(s - m_new[:, :1])
                l_ref[:, lanes] = alpha * l_ref[:, lanes] + jnp.sum(p, axis=1, keepdims=True)
                acc_ref[:, lanes] = alpha * acc_ref[:, lanes] + jnp.dot(p.astype(BF16), v_ref[:, lanes],
                                                                        preferred_element_type=F32)
                m_ref[:, lanes] = m_new

        need = need_ref[i * n + j]

        @pl.when(need != 0)
        def _():
            update(True)

        @pl.when(need == 0)
        def _():
            update(False)

        @pl.when(j == i)
        def _():
            o_ref[...] = (acc_ref[...] / l_ref[...]).astype(o_ref.dtype)
            lse_ref[...] = m_ref[...] + jnp.log(l_ref[...]) * LOG2E

    qmap = lambda h, s, qi, kj, need: (qi[s], h)
    kmap = lambda h, s, qi, kj, need: (kj[s], h)
    return _pcall(
        body, "attn_fwd", (heads // hp, qi_tab.shape[0]), 3,
        in_specs=[pl.BlockSpec((t, hp * MLA_QK_PAD), qmap), pl.BlockSpec((t, hp * MLA_NOPE), kmap),
                  pl.BlockSpec((t, hp * MLA_V), kmap),
                  pl.BlockSpec((t, LANES), lambda h, s, qi, kj, need: (kj[s], 0)),
                  pl.BlockSpec((t, 1), lambda h, s, qi, kj, need: (qi[s], 0)),
                  pl.BlockSpec((1, t), lambda h, s, qi, kj, need: (0, kj[s]))],
        out_specs=[pl.BlockSpec((t, hp * MLA_V), qmap), pl.BlockSpec((t, hp * LANES), qmap)],
        out_shape=[jax.ShapeDtypeStruct((S, heads * MLA_V), BF16), jax.ShapeDtypeStruct((S, heads * LANES), F32)],
        scratch=[pltpu.VMEM((t, hp * LANES), F32), pltpu.VMEM((t, hp * LANES), F32), pltpu.VMEM((t, hp * MLA_V), F32)],
        sem=("parallel", "arbitrary"),
        args=[qi_tab, kj_tab, aux['need'], q, kn, v, kr, aux['cidq'], aux['cidk']], comm=comm)


def _attn_bwd_call(q, kn, v, kr, o, lse, do, aux, heads, comm=None):
    S = q.shape[0]
    t = _attn_tile(S)
    n = S // t
    qi_tab, kj_tab = _tri_schedule(n, True)
    scale = (MLA_NOPE + MLA_ROPE) ** -0.5
    scale2 = scale * LOG2E
    nt_dims = (((1,), (1,)), ((), ()))
    tn_dims = (((0,), (0,)), ((), ()))

    def body(qi_ref, kj_ref, need_ref, q_ref, kn_ref, v_ref, kr_ref, cidq_ref, cidk_ref, o_ref, lse_ref, do_ref,
             dq_ref, dkn_ref, dv_ref, dkr_ref, dk_acc, dv_acc):
        st = pl.program_id(1)
        i, j = qi_ref[st], kj_ref[st]

        @pl.when(st == 0)
        def _():
            dq_ref[...] = jnp.zeros(dq_ref.shape, F32)

        @pl.when(i == j)
        def _():
            dk_acc[...] = jnp.zeros(dk_acc.shape, F32)
            dv_acc[...] = jnp.zeros(dv_acc.shape, F32)

        k = jnp.concatenate([kn_ref[...], kr_ref[...]], axis=1)
        qt, do = q_ref[...], do_ref[...]
        s = lax.dot_general(qt, k, nt_dims, preferred_element_type=F32) * scale2
        dp = lax.dot_general(do, v_ref[...], nt_dims, preferred_element_type=F32)
        dsum = jnp.sum(do.astype(F32) * o_ref[...].astype(F32), axis=1, keepdims=True)
        rows = pl.ds(pl.multiple_of(i * t, t), t)

        def grads(s):
            p = jnp.exp2(s - lse_ref[:, :1])
            ds = (p * (dp - dsum) * scale).astype(BF16)
            dv_acc[...] += lax.dot_general(p.astype(BF16), do, tn_dims, preferred_element_type=F32)
            dk_acc[...] += lax.dot_general(ds, qt, tn_dims, preferred_element_type=F32)
            dq_ref[rows, :] += jnp.dot(ds, k, preferred_element_type=F32)

        need = need_ref[i * n + j]

        @pl.when(need != 0)
        def _():
            grads(jnp.where(_attn_mask(cidq_ref, cidk_ref, i, j, t), s, NEG_INF))

        @pl.when(need == 0)
        def _():
            grads(s)

        @pl.when(i == n - 1)
        def _():
            dkn_ref[...] = dk_acc[:, :MLA_NOPE].astype(dkn_ref.dtype)
            dv_ref[...] = dv_acc[...].astype(dv_ref.dtype)
            dkr_ref[...] = dk_acc[:, MLA_NOPE:]

    qmap = lambda h, s, qi, kj, need: (qi[s], h)
    kmap = lambda h, s, qi, kj, need: (kj[s], h)
    return _pcall(
        body, "attn_bwd", (heads, qi_tab.shape[0]), 3,
        in_specs=[pl.BlockSpec((t, MLA_QK_PAD), qmap), pl.BlockSpec((t, MLA_NOPE), kmap), pl.BlockSpec((t, MLA_V), kmap),
                  pl.BlockSpec((t, LANES), lambda h, s, qi, kj, need: (kj[s], 0)),
                  pl.BlockSpec((t, 1), lambda h, s, qi, kj, need: (qi[s], 0)),
                  pl.BlockSpec((1, t), lambda h, s, qi, kj, need: (0, kj[s])),
                  pl.BlockSpec((t, MLA_V), qmap), pl.BlockSpec((t, LANES), qmap), pl.BlockSpec((t, MLA_V), qmap)],
        out_specs=[pl.BlockSpec((S, MLA_QK_PAD), lambda h, s, qi, kj, need: (0, h)),
                   pl.BlockSpec((t, MLA_NOPE), kmap), pl.BlockSpec((t, MLA_V), kmap), pl.BlockSpec((t, LANES), kmap)],
        out_shape=[jax.ShapeDtypeStruct((S, heads * MLA_QK_PAD), F32), jax.ShapeDtypeStruct((S, heads * MLA_NOPE), BF16),
                   jax.ShapeDtypeStruct((S, heads * MLA_V), BF16), jax.ShapeDtypeStruct((S, heads * LANES), F32)],
        scratch=[pltpu.VMEM((t, MLA_QK_PAD), F32), pltpu.VMEM((t, MLA_V), F32)],
        sem=("parallel", "arbitrary"),
        args=[qi_tab, kj_tab, aux['need'], q, kn, v, kr, aux['cidq'], aux['cidk'], o, lse, do], comm=comm)


def make_attention(aux, heads):
    tabs_f, tabs_b = aux['rope']
    odd, every = (lambda blk: blk % 2 == 1), (lambda blk: True)

    def run_host(q_raw, kn, v, kr_raw, comm=None):
        q = _rope_call(q_raw, tabs_f, odd, BF16, "rope_q")
        kr = _rope_call(kr_raw, tabs_f, every, BF16, "rope_k")
        o, lse, *carried = _attn_fwd_call(q, kn, v, kr, aux, heads, comm)
        return o, (q, kn, v, kr, o, lse), carried

    def bwd_host(res, do, comm=None):
        q, kn, v, kr, o, lse = res
        dq, dkn, dv, dkr, *carried = _attn_bwd_call(q, kn, v, kr, o, lse, do, aux, heads, comm)
        return (_rope_call(dq, tabs_b, odd, F32, "rope_dq"), dkn, dv,
                _rope_call(dkr, tabs_b, every, F32, "rope_dk", fold=True)), carried

    @jax.custom_vjp
    def attn(q_raw, kn, v, kr_raw):
        return run_host(q_raw, kn, v, kr_raw)[0]

    attn.defvjp(lambda *a: run_host(*a)[:2], lambda res, do: bwd_host(res, do)[0])
    return attn, run_host, bwd_host


GLA_ROWS = 256


def _tri(lower):
    r = lax.broadcasted_iota(jnp.int32, (CHUNK, CHUNK), 0)
    c = lax.broadcasted_iota(jnp.int32, (CHUNK, CHUNK), 1)
    return jnp.where((c <= r) if lower else (c >= r), 1.0, 0.0).astype(F32)


def _gla_chunk(q_ref, k_ref, v_ref, la_ref, sl):
    la = la_ref[sl, :]
    cum = jnp.dot(_tri(True), la, preferred_element_type=F32, precision=lax.Precision.HIGHEST)
    tot = cum[CHUNK - 1:CHUNK, :]
    e = jnp.exp(tot - cum)
    k = k_ref[sl, :]
    kdec = k * e
    v = v_ref[sl, :]
    upd_t = lax.dot_general(v.astype(BF16), kdec.astype(BF16), (((0,), (0,)), ((), ())), preferred_element_type=F32)
    qs = (q_ref[sl, :] * (GLA_DK ** -0.5)).astype(BF16)
    return e, k, kdec, v, upd_t, jnp.exp(tot), qs


def _gla_specs(heads, rows_map):
    return [pl.BlockSpec((GLA_ROWS, GLA_DK), lambda h, b: (rows_map(b), h)),
            pl.BlockSpec((GLA_ROWS, GLA_DK), lambda h, b: (rows_map(b), heads + h)),
            pl.BlockSpec((GLA_ROWS, GLA_DV), lambda h, b: (rows_map(b), heads + h)),
            pl.BlockSpec((GLA_ROWS, GLA_DK), lambda h, b: (rows_map(b), h))]


def _gla_fwd_call(hm, la, heads):
    S = hm.shape[0]
    assert S % GLA_ROWS == 0
    nb, cpb = S // GLA_ROWS, GLA_ROWS // CHUNK

    def body(q_ref, k_ref, v_ref, la_ref, o_ref, sp_ref, st_ref):
        @pl.when(pl.program_id(1) == 0)
        def _():
            st_ref[...] = jnp.zeros(st_ref.shape, F32)

        for c in range(cpb):
            sl = slice(c * CHUNK, (c + 1) * CHUNK)
            _, _, _, _, upd_t, decay, qs = _gla_chunk(q_ref, k_ref, v_ref, la_ref, sl)
            state = st_ref[...]
            sp_ref[0, c] = state
            state = state * decay + upd_t
            st_ref[...] = state
            o_ref[sl, :] = lax.dot_general(qs, state.astype(BF16), (((1,), (1,)), ((), ())), preferred_element_type=F32)

    return pl.pallas_call(
        body, name=_nm("gla_fwd"), grid=(heads, nb),
        out_shape=(jax.ShapeDtypeStruct((S, heads * GLA_DV), F32),
                   jax.ShapeDtypeStruct((heads, S // CHUNK, GLA_DV, GLA_DK), F32)),
        in_specs=_gla_specs(heads, lambda b: b),
        out_specs=(pl.BlockSpec((GLA_ROWS, GLA_DV), lambda h, b: (b, h)),
                   pl.BlockSpec((1, cpb, GLA_DV, GLA_DK), lambda h, b: (h, b, 0, 0))),
        scratch_shapes=[pltpu.VMEM((GLA_DV, GLA_DK), F32)],
        compiler_params=_cp(("parallel", "arbitrary")),
    )(hm, hm, hm, la)


def _gla_bwd_call(hm, la, sprev, do, heads):
    S = hm.shape[0]
    nb, cpb = S // GLA_ROWS, GLA_ROWS // CHUNK
    scale = GLA_DK ** -0.5

    def body(q_ref, k_ref, v_ref, la_ref, sp_ref, do_ref, dq_ref, dk_ref, dv_ref, dla_ref, carry_ref):
        @pl.when(pl.program_id(1) == 0)
        def _():
            carry_ref[...] = jnp.zeros(carry_ref.shape, F32)

        for c in reversed(range(cpb)):
            sl = slice(c * CHUNK, (c + 1) * CHUNK)
            e, k, kdec, v, upd_t, decay, qs = _gla_chunk(q_ref, k_ref, v_ref, la_ref, sl)
            sp = sp_ref[0, c]
            s_n = sp * decay + upd_t
            dob = do_ref[sl, :].astype(BF16)
            g = carry_ref[...] + lax.dot_general(dob, qs, (((0,), (0,)), ((), ())), preferred_element_type=F32)
            gb = g.astype(BF16)
            dq_ref[sl, :] = jnp.dot(dob, s_n.astype(BF16), preferred_element_type=F32) * scale
            ddecay = jnp.sum(g * sp, axis=0, keepdims=True)
            dkdec = jnp.dot(v.astype(BF16), gb, preferred_element_type=F32)
            dv_ref[sl, :] = lax.dot_general(kdec.astype(BF16), gb, (((1,), (1,)), ((), ())), preferred_element_type=F32)
            dk_ref[sl, :] = dkdec * e
            w = dkdec * k * e
            dtot = jnp.sum(w, axis=0, keepdims=True) + ddecay * decay
            last = lax.broadcasted_iota(jnp.int32, (CHUNK, 1), 0) == CHUNK - 1
            dcum = jnp.where(last, dtot - w, -w)
            dla_ref[sl, :] = jnp.dot(_tri(False), dcum, preferred_element_type=F32, precision=lax.Precision.HIGHEST)
            carry_ref[...] = g * decay

    rev = lambda b: nb - 1 - b
    return pl.pallas_call(
        body, name=_nm("gla_bwd"), grid=(heads, nb),
        out_shape=(jax.ShapeDtypeStruct((S, heads * GLA_DK), F32), jax.ShapeDtypeStruct((S, heads * GLA_DK), F32),
                   jax.ShapeDtypeStruct((S, heads * GLA_DV), F32), jax.ShapeDtypeStruct((S, heads * GLA_DK), F32)),
        in_specs=_gla_specs(heads, rev) + [
            pl.BlockSpec((1, cpb, GLA_DV, GLA_DK), lambda h, b: (h, rev(b), 0, 0)),
            pl.BlockSpec((GLA_ROWS, GLA_DV), lambda h, b: (rev(b), h))],
        out_specs=(pl.BlockSpec((GLA_ROWS, GLA_DK), lambda h, b: (rev(b), h)),
                   pl.BlockSpec((GLA_ROWS, GLA_DK), lambda h, b: (rev(b), h)),
                   pl.BlockSpec((GLA_ROWS, GLA_DV), lambda h, b: (rev(b), h)),
                   pl.BlockSpec((GLA_ROWS, GLA_DK), lambda h, b: (rev(b), h))),
        scratch_shapes=[pltpu.VMEM((GLA_DV, GLA_DK), F32)],
        compiler_params=_cp(("parallel", "arbitrary")),
    )(hm, hm, hm, la, sprev, do)


@functools.partial(jax.custom_vjp, nondiff_argnums=(3,))
def gla_core(hm, la, o_norm, heads):
    return _gla_core_fwd(hm, la, o_norm, heads)[0]


def _gla_core_fwd(hm, la, o_norm, heads):
    o, sprev = _gla_fwd_call(hm, la, heads)
    vd = heads * GLA_DV
    rows = [(o, vd, 0), (hm, vd, 2 * heads * GLA_DK // vd + 1)]
    (y,) = _rw_fwd(_gla_out_fn(heads), rows, [o_norm], [BF16], "gla_out_fwd")
    return y, (hm, la, o_norm, o, sprev)


def _gla_core_bwd(heads, res, dy):
    hm, la, o_norm, o, sprev = res
    vd = heads * GLA_DV
    rows = [(o, vd, 0), (hm, vd, 2 * heads * GLA_DK // vd + 1)]
    (do, dr), (dg,) = _rw_bwd(_gla_out_fn(heads), rows, [o_norm], [dy], [F32, F32], "gla_out_bwd")
    dq, dk, dv, dla = _gla_bwd_call(hm, la, sprev, do, heads)
    return jnp.concatenate([dq, dk, dv, dr], axis=1), dla, dg


gla_core.defvjp(_gla_core_fwd, _gla_core_bwd)


CONV_COLS = 256
HALO = 8


def _conv_rows(S):
    return min(512, S)


def _conv_taps(main_ref, halo_ref, i):
    prev = jnp.where(i > 0, halo_ref[...], 0.0)
    full = jnp.concatenate([prev, main_ref[...]], axis=0)
    return full[HALO:], pltpu.roll(full, 1, 0)[HALO:], pltpu.roll(full, 2, 0)[HALO:]


def _conv_apply(taps, w_ref, b_ref):
    x0, x1, x2 = taps
    return x2 * w_ref[0:1, :] + x1 * w_ref[1:2, :] + x0 * w_ref[2:3, :] + b_ref[...]


def _gelu_gate(uc, gc):
    return uc * jax.nn.gelu(gc)


def _conv_in_specs(R, nj):
    hpr = R // HALO
    main = pl.BlockSpec((R, CONV_COLS), lambda j, i: (i, j))
    halo = pl.BlockSpec((HALO, CONV_COLS), lambda j, i: (jnp.maximum(i * hpr - 1, 0), j))
    par = lambda rows, off: pl.BlockSpec((rows, CONV_COLS), lambda j, i: (0, j + off))
    return [main, halo, main, halo, par(CONV_W, 0), par(CONV_W, nj), par(1, 0), par(1, nj)]


def _conv_fwd_call(hu, hg, cw, cb, comm=None):
    S, dff = hu.shape
    R, nj = _conv_rows(S), dff // CONV_COLS

    def body(u_ref, uh_ref, g_ref, gh_ref, wu_ref, wg_ref, bu_ref, bg_ref, a_ref):
        i = pl.program_id(1)
        uc = _conv_apply(_conv_taps(u_ref, uh_ref, i), wu_ref, bu_ref)
        gc = _conv_apply(_conv_taps(g_ref, gh_ref, i), wg_ref, bg_ref)
        a_ref[...] = _gelu_gate(uc, gc).astype(a_ref.dtype)

    return _pcall(
        body, "conv_fwd", (nj, S // R), 0, in_specs=_conv_in_specs(R, nj),
        out_specs=[pl.BlockSpec((R, CONV_COLS), lambda j, i: (i, j))], out_shape=[jax.ShapeDtypeStruct((S, dff), BF16)],
        scratch=[], sem=("parallel", "parallel"), args=[hu, hu, hg, hg, cw, cw, cb, cb], comm=comm)


def _conv_bwd_gate_call(hu, hg, cw, cb, da, comm=None):
    S, dff = hu.shape
    R, nj = _conv_rows(S), dff // CONV_COLS

    def body(u_ref, uh_ref, g_ref, gh_ref, wu_ref, wg_ref, bu_ref, bg_ref, da_ref,
             du_ref, dg_ref, dwu_ref, dwg_ref, dbu_ref, dbg_ref):
        i = pl.program_id(1)
        ut, gt = _conv_taps(u_ref, uh_ref, i), _conv_taps(g_ref, gh_ref, i)
        uc, gc = _conv_apply(ut, wu_ref, bu_ref), _conv_apply(gt, wg_ref, bg_ref)
        _, vjp_fn = jax.vjp(_gelu_gate, uc, gc)
        du, dg = vjp_fn(da_ref[...].astype(F32))
        du_ref[...] = du
        dg_ref[...] = dg

        @pl.when(i == 0)
        def _():
            for r in (dwu_ref, dwg_ref, dbu_ref, dbg_ref):
                r[...] = jnp.zeros(r.shape, F32)

        for d, taps, dw_ref, db_ref in ((du, ut, dwu_ref, dbu_ref), (dg, gt, dwg_ref, dbg_ref)):
            x0, x1, x2 = taps
            dw_ref[0:1, :] += jnp.sum(d * x2, axis=0, keepdims=True)
            dw_ref[1:2, :] += jnp.sum(d * x1, axis=0, keepdims=True)
            dw_ref[2:3, :] += jnp.sum(d * x0, axis=0, keepdims=True)
            db_ref[...] += jnp.sum(d, axis=0, keepdims=True)

    tile = pl.BlockSpec((R, CONV_COLS), lambda j, i: (i, j))
    par = lambda rows: pl.BlockSpec((rows, CONV_COLS), lambda j, i: (0, j))
    return _pcall(
        body, "conv_bwd_gate", (nj, S // R), 0, in_specs=_conv_in_specs(R, nj) + [tile],
        out_specs=[tile, tile, par(CONV_W), par(CONV_W), par(1), par(1)],
        out_shape=[jax.ShapeDtypeStruct((S, dff), F32), jax.ShapeDtypeStruct((S, dff), F32),
                   jax.ShapeDtypeStruct((CONV_W, dff), F32), jax.ShapeDtypeStruct((CONV_W, dff), F32),
                   jax.ShapeDtypeStruct((1, dff), F32), jax.ShapeDtypeStruct((1, dff), F32)],
        scratch=[], sem=("parallel", "arbitrary"), args=[hu, hu, hg, hg, cw, cw, cb, cb, da], comm=comm)


def _conv_bwd_shift_call(dc, cw, col_off, out_dtype):
    S, dff = dc.shape
    R, nj = _conv_rows(S), dff // CONV_COLS
    hpr, last = R // HALO, S // HALO - 1
    ni = S // R

    def body(d_ref, nx_ref, w_ref, o_ref):
        i = pl.program_id(1)
        nxt = jnp.where(i < ni - 1, nx_ref[...], 0.0)
        full = jnp.concatenate([d_ref[...], nxt], axis=0)
        n = R + HALO
        y1, y2 = pltpu.roll(full, n - 1, 0)[:R], pltpu.roll(full, n - 2, 0)[:R]
        o_ref[...] = (full[:R] * w_ref[2:3, :] + y1 * w_ref[1:2, :] + y2 * w_ref[0:1, :]).astype(o_ref.dtype)

    return pl.pallas_call(
        body, name=_nm("conv_bwd_shift"), grid=(nj, ni), out_shape=jax.ShapeDtypeStruct((S, dff), out_dtype),
        in_specs=[pl.BlockSpec((R, CONV_COLS), lambda j, i: (i, j)),
                  pl.BlockSpec((HALO, CONV_COLS), lambda j, i: (jnp.minimum((i + 1) * hpr, last), j)),
                  pl.BlockSpec((CONV_W, CONV_COLS), lambda j, i: (0, j + col_off))],
        out_specs=pl.BlockSpec((R, CONV_COLS), lambda j, i: (i, j)),
        compiler_params=_cp(("parallel", "parallel")),
    )(dc, dc, cw)


@jax.custom_vjp
def ffn_hidden(x1, w3, cw, cb):
    return _ffn_hidden_fwd(x1, w3, cw, cb)[0]


def _ffn_hidden_fwd(x1, w3, cw, cb, comm=None):
    half = w3.shape[0] // 2
    hu = _mm(x1, w3, "nn", F32, "up_u", (0, half))
    hg = _mm(x1, w3, "nn", F32, "up_g", (half, half))
    a, *carried = _conv_fwd_call(hu, hg, cw, cb, comm)
    return (a, (x1, w3, cw, cb, hu, hg)) + ((carried,) if comm is not None else ())


def _ffn_hidden_bwd(res, da, comm=None):
    x1, w3, cw, cb, hu, hg = res
    half = w3.shape[0] // 2
    nj = hu.shape[1] // CONV_COLS
    du, dg, dwu, dwg, dbu, dbg, *carried = _conv_bwd_gate_call(hu, hg, cw, cb, da, comm)
    dhu = _conv_bwd_shift_call(du, cw, 0, BF16)
    dhg = _conv_bwd_shift_call(dg, cw, nj, BF16)
    dx = _mm(dhu, w3, "nt", F32, "up_dx_u", (0, half)) + _mm(dhg, w3, "nt", F32, "up_dx_g", (half, half))
    dw3 = jnp.concatenate([_mm(x1, dhu, "tn", w3.dtype, "up_dw_u", (0, half)),
                           _mm(x1, dhg, "tn", w3.dtype, "up_dw_g", (half, half))], axis=0)
    grads = (dx, dw3, jnp.concatenate([dwu, dwg], axis=1), jnp.concatenate([dbu, dbg], axis=1))
    return (grads, carried) if comm is not None else grads


ffn_hidden.defvjp(_ffn_hidden_fwd, _ffn_hidden_bwd)


def _loss_call(y, target):
    S, D = y.shape
    tr = _row_tile(S, D)

    def body(y_ref, t_ref, sq_ref, dy_ref):
        diff = y_ref[...] - t_ref[...]
        dy_ref[...] = diff * (1.0 / D)
        part = jnp.sum(diff * diff, axis=0, keepdims=True)
        i = pl.program_id(0)

        @pl.when(i == 0)
        def _():
            sq_ref[...] = part

        @pl.when(i > 0)
        def _():
            sq_ref[...] += part

    row = pl.BlockSpec((tr, D), lambda i: (i, 0))
    return pl.pallas_call(
        body, name=_nm("loss"), grid=(S // tr,),
        out_shape=(jax.ShapeDtypeStruct((1, D), F32), jax.ShapeDtypeStruct((S, D), F32)),
        in_specs=[row, row], out_specs=(pl.BlockSpec((1, D), lambda i: (0, 0)), row),
        compiler_params=_cp(("arbitrary",)),
    )(y, target)


def _row_divisor(rows):
    for cand in range(min(rows, 512), 15, -1):
        if rows % cand == 0 and cand % 16 == 0:
            return cand
    return rows


def _adamw_call(w, g, m, v):
    shape = w.shape
    w2, g2, m2, v2 = (a.reshape(-1, shape[-1]) for a in (w, g, m, v))
    rows, cols = w2.shape
    tr = _row_divisor(rows)

    def body(w_ref, g_ref, m_ref, v_ref, d_ref, nm_ref, nv_ref):
        g_ = g_ref[...]
        m_ = ADAM_B1 * m_ref[...] + (1.0 - ADAM_B1) * g_
        v_ = ADAM_B2 * v_ref[...] + (1.0 - ADAM_B2) * (g_ * g_)
        m_hat = m_ / (1.0 - ADAM_B1 ** ADAM_STEP)
        v_hat = v_ / (1.0 - ADAM_B2 ** ADAM_STEP)
        d_ref[...] = -ADAM_LR * (m_hat / (jnp.sqrt(v_hat) + ADAM_EPS) + ADAM_WD * w_ref[...])
        nm_ref[...] = m_
        nv_ref[...] = v_

    blk = pl.BlockSpec((tr, cols), lambda i: (i, 0))
    outs = pl.pallas_call(
        body, name=_nm("adamw"), grid=(rows // tr,),
        out_shape=tuple(jax.ShapeDtypeStruct((rows, cols), F32) for _ in range(3)),
        in_specs=[blk] * 4, out_specs=(blk,) * 3, compiler_params=_cp(("parallel",)),
    )(w2, g2, m2, v2)
    return tuple(o.reshape(shape) for o in outs)


ANY = pl.BlockSpec(memory_space=pl.ANY)


def _place():
    return lax.axis_index("x"), lax.axis_index("y"), lax.axis_index("c")


def all_gather(shards):
    n = len(shards)

    def body(*refs):
        x_refs, out_refs = refs[:n], refs[n:2 * n]
        send_sems, recv_sems, local_sems = refs[2 * n:]
        x, y, c = _place()
        me, sibling = (x, y, c), (x, y, 1 - c)
        chips = [(1 - x, y), (x, 1 - y), (1 - x, 1 - y)]

        def slot(a, px, py, pc):
            return out_refs[a].at[:, 4 * px + 2 * py + pc]

        def copy(a, k, block, to, own=False):
            return pltpu.make_async_remote_copy(
                src_ref=x_refs[a] if own else slot(a, *block), dst_ref=slot(a, *block),
                send_sem=send_sems.at[7 * a + k], recv_sem=recv_sems.at[7 * a + k], device_id=to, device_id_type=MESH)

        mine = [pltpu.make_async_copy(x_refs[a], slot(a, *me), local_sems.at[a]) for a in range(n)]
        first = []
        for a in range(n):
            mine[a].start()
            first.append(copy(a, 0, me, sibling, own=True))
            first += [copy(a, 1 + j, me, (*chip, c), own=True) for j, chip in enumerate(chips)]
        for cp in first:
            cp.start()
        passed = []
        for j, chip in enumerate(chips):
            for a in range(n):
                copy(a, 1 + j, (*chip, c), me).wait_recv()
                passed.append(copy(a, 4 + j, (*chip, c), sibling))
                passed[-1].start()
        for a in range(n):
            copy(a, 0, sibling, me).wait_recv()
        for j, chip in enumerate(chips):
            for a in range(n):
                copy(a, 4 + j, (*chip, 1 - c), me).wait_recv()
        for cp in first + passed:
            cp.wait_send()
        for cp in mine:
            cp.wait()

    return pl.pallas_call(
        body, name=_nm("all_gather"),
        out_shape=tuple(jax.ShapeDtypeStruct((s.shape[0], N_DEV) + s.shape[1:], s.dtype) for s in shards),
        in_specs=[ANY] * n, out_specs=(ANY,) * n,
        scratch_shapes=[pltpu.SemaphoreType.DMA((7 * n,)), pltpu.SemaphoreType.DMA((7 * n,)), pltpu.SemaphoreType.DMA((n,))],
    )(*shards)


def _rs_pair_exchange(gs):
    n = len(gs)

    def body(*refs):
        g_refs, recv_refs = refs[:n], refs[n:2 * n]
        send_sems, recv_sems = refs[2 * n:]
        x, y, c = _place()
        copies = [pltpu.make_async_remote_copy(
            src_ref=g_refs[a].at[:, 2 * j + (1 - c)], dst_ref=recv_refs[a].at[j], send_sem=send_sems.at[4 * a + j],
            recv_sem=recv_sems.at[4 * a + j], device_id=(x, y, 1 - c), device_id_type=MESH)
            for a in range(n) for j in range(4)]
        for cp in copies:
            cp.start()
        for cp in copies:
            cp.wait_recv()
        for cp in copies:
            cp.wait_send()

    return pl.pallas_call(
        body, name=_nm("rs_pair"),
        out_shape=tuple(jax.ShapeDtypeStruct((4, g.shape[0]) + g.shape[2:], g.dtype) for g in gs),
        in_specs=[ANY] * n, out_specs=(ANY,) * n,
        scratch_shapes=[pltpu.SemaphoreType.DMA((4 * n,)), pltpu.SemaphoreType.DMA((4 * n,))],
    )(*gs)


def _rs_chip_exchange(ps):
    n = len(ps)

    def body(*refs):
        p_refs, recv_refs = refs[:n], refs[n:2 * n]
        send_sems, recv_sems = refs[2 * n:]
        x, y, c = _place()
        chips = [(1 - x, y), (x, 1 - y), (1 - x, 1 - y)]
        copies = [pltpu.make_async_remote_copy(
            src_ref=p_refs[a].at[2 * cx + cy], dst_ref=recv_refs[a].at[k], send_sem=send_sems.at[3 * a + k],
            recv_sem=recv_sems.at[3 * a + k], device_id=(cx, cy, c), device_id_type=MESH)
            for a in range(n) for k, (cx, cy) in enumerate(chips)]
        for cp in copies:
            cp.start()
        for cp in copies:
            cp.wait_recv()
        for cp in copies:
            cp.wait_send()

    return pl.pallas_call(
        body, name=_nm("rs_chip"),
        out_shape=tuple(jax.ShapeDtypeStruct((3,) + p.shape[1:], p.dtype) for p in ps),
        in_specs=[ANY] * n, out_specs=(ANY,) * n,
        scratch_shapes=[pltpu.SemaphoreType.DMA((3 * n,)), pltpu.SemaphoreType.DMA((3 * n,))],
    )(*ps)


def _rs_pair_add(g, recv, c_idx):
    L, _, a, b = g.shape
    ta = _row_divisor(a)

    def body(c_ref, g_ref, r_ref, o_ref):
        o_ref[...] = (g_ref[...].astype(F32) + r_ref[...].astype(F32)).astype(o_ref.dtype)

    grid_spec = pltpu.PrefetchScalarGridSpec(
        num_scalar_prefetch=1, grid=(4, L, a // ta),
        in_specs=[pl.BlockSpec((None, None, ta, b), lambda j, l, i, c_ref: (l, 2 * j + c_ref[0], i, 0)),
                  pl.BlockSpec((None, None, ta, b), lambda j, l, i, c_ref: (j, l, i, 0))],
        out_specs=pl.BlockSpec((None, None, ta, b), lambda j, l, i, c_ref: (j, l, i, 0)))
    return pl.pallas_call(
        body, name=_nm("rs_pair_add"), grid_spec=grid_spec, out_shape=jax.ShapeDtypeStruct((4, L, a, b), g.dtype),
        compiler_params=_cp(("parallel", "parallel", "parallel")),
    )(c_idx, g, recv)


def _rs_final_add(p1, recv, chip_idx):
    _, L, a, b = p1.shape
    ta = _row_divisor(a)

    def body(chip_ref, p_ref, r_ref, o_ref):
        acc = p_ref[...].astype(F32)
        for k in range(3):
            acc = acc + r_ref[k].astype(F32)
        o_ref[...] = acc

    grid_spec = pltpu.PrefetchScalarGridSpec(
        num_scalar_prefetch=1, grid=(L, a // ta),
        in_specs=[pl.BlockSpec((None, None, ta, b), lambda l, i, chip_ref: (chip_ref[0], l, i, 0)),
                  pl.BlockSpec((3, None, ta, b), lambda l, i, chip_ref: (0, l, i, 0))],
        out_specs=pl.BlockSpec((None, ta, b), lambda l, i, chip_ref: (l, i, 0)))
    return pl.pallas_call(
        body, name=_nm("rs_final_add"), grid_spec=grid_spec, out_shape=jax.ShapeDtypeStruct((L, a, b), F32),
        compiler_params=_cp(("parallel", "parallel")),
    )(chip_idx, p1, recv)


def reduce_scatter(gs):
    x, y, c = _place()
    c_idx = jnp.reshape(c, (1,)).astype(jnp.int32)
    chip_idx = jnp.reshape(2 * x + y, (1,)).astype(jnp.int32)
    recv1 = _rs_pair_exchange(gs)
    p1 = [_rs_pair_add(g, r, c_idx) for g, r in zip(gs, recv1)]
    recv2 = _rs_chip_exchange(p1)
    return [_rs_final_add(p, r, chip_idx) for p, r in zip(p1, recv2)]


def all_reduce_small(v):
    r, C = v.shape

    def body(v_ref, out_ref, buf_ref, send_sems, recv_sems):
        x, y, c = _place()
        my_id = 4 * x + 2 * y + c
        buf_ref[my_id] = v_ref[...]
        copies = []
        for k in range(1, N_DEV):
            fx, fy, fc = (k >> 2) & 1, (k >> 1) & 1, k & 1
            peer = (x ^ fx, y ^ fy, c ^ fc)
            copies.append(pltpu.make_async_remote_copy(
                src_ref=v_ref, dst_ref=buf_ref.at[my_id], send_sem=send_sems.at[k - 1], recv_sem=recv_sems.at[k - 1],
                device_id=peer, device_id_type=MESH))
        for cp in copies:
            cp.start()
        for cp in copies:
            cp.wait_recv()
        for cp in copies:
            cp.wait_send()
        acc = buf_ref[0]
        for d in range(1, N_DEV):
            acc = acc + buf_ref[d]
        out_ref[...] = acc

    vm = pl.BlockSpec(memory_space=pltpu.VMEM)
    return pl.pallas_call(
        body, name=_nm("all_reduce_small"), out_shape=jax.ShapeDtypeStruct((r, C), F32),
        in_specs=[vm], out_specs=vm,
        scratch_shapes=[pltpu.VMEM((N_DEV, r, C), F32), pltpu.SemaphoreType.DMA((7,)), pltpu.SemaphoreType.DMA((7,))],
    )(v)


def _slot(ref, px, py, pc):
    return ref.at[:, 4 * px + 2 * py + pc]


def comm_gather_own(shards):
    n = len(shards)

    def build(cin, cout, send_sems, recv_sems, local_sems):
        x, y, c = _place()
        me = (x, y, c)
        peers = [(x, y, 1 - c), (1 - x, y, c), (x, 1 - y, c), (1 - x, 1 - y, c)]
        starts, waits = [], []
        for a in range(n):
            local = pltpu.make_async_copy(cin[a], _slot(cout[a], *me), local_sems.at[a])
            starts.append(local)
            waits.append(local.wait)
            for k, peer in enumerate(peers):
                send = pltpu.make_async_remote_copy(
                    src_ref=cin[a], dst_ref=_slot(cout[a], *me), send_sem=send_sems.at[4 * a + k],
                    recv_sem=recv_sems.at[4 * a + k], device_id=peer, device_id_type=MESH)
                arrive = pltpu.make_async_remote_copy(
                    src_ref=cin[a], dst_ref=_slot(cout[a], *peer), send_sem=send_sems.at[4 * a + k],
                    recv_sem=recv_sems.at[4 * a + k], device_id=peer, device_id_type=MESH)
                starts.append(send)
                waits += [arrive.wait_recv, send.wait_send]
        return starts, waits

    out_shapes = [jax.ShapeDtypeStruct((s.shape[0], N_DEV) + s.shape[1:], s.dtype) for s in shards]
    return Comm(shards, out_shapes, {}, 4 * n, 4 * n, n, build)


def comm_gather_pass(partial):
    n = len(partial)

    def build(cin, cout, send_sems, recv_sems, local_sems):
        x, y, c = _place()
        chips = [(1 - x, y), (x, 1 - y), (1 - x, 1 - y)]
        starts, waits = [], []
        for a in range(n):
            for j, chip in enumerate(chips):
                send = pltpu.make_async_remote_copy(
                    src_ref=_slot(cout[a], *chip, c), dst_ref=_slot(cout[a], *chip, c), send_sem=send_sems.at[3 * a + j],
                    recv_sem=recv_sems.at[3 * a + j], device_id=(x, y, 1 - c), device_id_type=MESH)
                arrive = pltpu.make_async_remote_copy(
                    src_ref=_slot(cout[a], *chip, c), dst_ref=_slot(cout[a], *chip, 1 - c),
                    send_sem=send_sems.at[3 * a + j], recv_sem=recv_sems.at[3 * a + j],
                    device_id=(x, y, 1 - c), device_id_type=MESH)
                starts.append(send)
                waits += [arrive.wait_recv, send.wait_send]
        return starts, waits

    out_shapes = [jax.ShapeDtypeStruct(p.shape, p.dtype) for p in partial]
    return Comm(partial, out_shapes, {a: a for a in range(n)}, 3 * n, 3 * n, 1, build)


def comm_rs_pair(gs):
    n = len(gs)

    def build(cin, cout, send_sems, recv_sems, local_sems):
        x, y, c = _place()
        starts, waits = [], []
        for a in range(n):
            for j in range(4):
                cp = pltpu.make_async_remote_copy(
                    src_ref=cin[a].at[:, 2 * j + (1 - c)], dst_ref=cout[a].at[j], send_sem=send_sems.at[4 * a + j],
                    recv_sem=recv_sems.at[4 * a + j], device_id=(x, y, 1 - c), device_id_type=MESH)
                starts.append(cp)
                waits += [cp.wait_recv, cp.wait_send]
        return starts, waits

    out_shapes = [jax.ShapeDtypeStruct((4, g.shape[0]) + g.shape[2:], g.dtype) for g in gs]
    return Comm(gs, out_shapes, {}, 4 * n, 4 * n, 1, build)


def comm_rs_chip(ps):
    n = len(ps)

    def build(cin, cout, send_sems, recv_sems, local_sems):
        x, y, c = _place()
        chips = [(1 - x, y), (x, 1 - y), (1 - x, 1 - y)]
        starts, waits = [], []
        for a in range(n):
            for k, (cx, cy) in enumerate(chips):
                cp = pltpu.make_async_remote_copy(
                    src_ref=cin[a].at[2 * cx + cy], dst_ref=cout[a].at[k], send_sem=send_sems.at[3 * a + k],
                    recv_sem=recv_sems.at[3 * a + k], device_id=(cx, cy, c), device_id_type=MESH)
                starts.append(cp)
                waits += [cp.wait_recv, cp.wait_send]
        return starts, waits

    out_shapes = [jax.ShapeDtypeStruct((3,) + p.shape[1:], p.dtype) for p in ps]
    return Comm(ps, out_shapes, {}, 3 * n, 3 * n, 1, build)


def _pack(arrays, dtype, row_align):
    lead = arrays[0].shape[:-1]
    quantum = row_align * PACK_COLS
    parts, sizes = [], []
    for a in arrays:
        n = a.shape[-1]
        padded = _round_up(n, quantum)
        a = a.astype(dtype)
        if padded != n:
            a = jnp.pad(a, [(0, 0)] * len(lead) + [(0, padded - n)])
        parts.append(a.reshape(*lead, padded // PACK_COLS, PACK_COLS))
        sizes.append((n, padded // PACK_COLS))
    return jnp.concatenate(parts, axis=len(lead)), sizes


def _unpack(packed, sizes):
    lead = packed.shape[:-2]
    out, row = [], 0
    for n, rows in sizes:
        part = lax.slice_in_dim(packed, row, row + rows, axis=len(lead))
        out.append(part.reshape(*lead, rows * PACK_COLS)[..., :n])
        row += rows
    return out


def _unshard(gathered, axis):
    _, L, a, b = gathered.shape
    if axis == 1:
        return [gathered[:, l].reshape(N_DEV * a, b) for l in range(L)]
    return [jnp.transpose(gathered[:, l], (1, 0, 2)).reshape(a, N_DEV * b) for l in range(L)]


def _reshard(fulls, axis):
    blocks = []
    for f in fulls:
        A, B = f.shape
        if axis == 1:
            blocks.append(f.reshape(N_DEV, A // N_DEV, B))
        else:
            blocks.append(jnp.transpose(f.reshape(A, N_DEV, B // N_DEV), (1, 0, 2)))
    return jnp.stack(blocks, axis=1)


def _as3(a):
    return a if a.ndim == 3 else a[:, None, :]


def _prep_big(name, w, dims):
    w = w.astype(BF16)
    L, a, b = w.shape
    if name == 'mla_w_in':
        return jnp.pad(w, ((0, 0), (0, 0), (0, dims['h_width'] - b)))
    if name == 'mla_w_uq':
        hd = MLA_NOPE + MLA_ROPE
        w = jnp.pad(w.reshape(L, a, b // hd, hd), ((0, 0), (0, 0), (0, 0), (0, MLA_QK_PAD - hd)))
        return w.reshape(L, a, b // hd * MLA_QK_PAD)
    if name == 'ffn_w_up':
        return jnp.pad(w, ((0, 0), (0, 0), (0, _round_up(b, CONV_COLS) - b)))
    return w


def _unprep_big(name, g, shape):
    L, a, b = shape
    if name == 'mla_w_uq':
        hd = MLA_NOPE + MLA_ROPE
        return g.reshape(L, a, b // hd, MLA_QK_PAD)[..., :hd].reshape(L, a, b)
    return g[:, :, :b]


def _rope_tables(positions):
    inv = 1.0 / (ROPE_THETA ** (jnp.arange(0, MLA_ROPE, 2, dtype=F32) / MLA_ROPE))
    ang = positions.astype(F32)[:, None] * inv
    cos, sin = jnp.cos(ang), jnp.sin(ang)
    one, zero = jnp.ones_like(cos), jnp.zeros_like(cos)
    a = jnp.concatenate([cos, cos, one, one], axis=1)
    up = jnp.concatenate([sin, zero, zero, zero], axis=1)
    down = jnp.concatenate([zero, sin, zero, zero], axis=1)
    return (a, -up, down), (a, up, -down)


def _rows_full(w):
    return w.reshape(w.shape[0] * w.shape[1], w.shape[2])


def _ops(depth):
    alpha = (2 * depth) ** 0.25
    return {'ln_res': rw_op(_ln_res_fn(alpha), "ln_res", 2, [F32]), 'ple': rw_op(_ple_fn, "ple", 3, [F32]),
            'gla_gate': rw_op(_gla_gate_fn, "gla_gate", 1, [F32])}


MLA_PRE_W = ['mla_w_in', 'mla_q_norm', 'mla_kv_norm', 'mla_w_uq', 'mla_w_uk', 'mla_w_uv']
FFN_IN_W = ['mla_w_o', 'ln1_g', 'ln1_b', 'ffn_conv_w', 'ffn_conv_b']
FFN_OUT_W = ['ffn_w_down', 'ln2_g', 'ln2_b', 'ple_w_gate', 'ple_w_proj', 'ple_b_gate']


def _mla_heads(wl, j):
    return N_DEV * wl['mla_w_uk'][j].shape[2] // MLA_NOPE


def _mla_pre(x, wl, j):
    w_uq, w_uk, w_uv = wl['mla_w_uq'][j], wl['mla_w_uk'][j], wl['mla_w_uv'][j]
    h = linear(x, _rows_full(wl['mla_w_in'][j]), F32)
    mla_norm = rw_op(_mla_norm_fn(w_uq.shape[1], w_uk.shape[1]), "mla_norm", 1, [BF16, BF16, F32])
    cq, ckv, kr_raw = mla_norm(h, wl['mla_q_norm'][j], wl['mla_kv_norm'][j])
    return linear(cq, w_uq, F32), linear(ckv, w_uk, BF16), linear(ckv, w_uv, BF16), kr_raw


def _gla_mixer(x, wl, j, ops):
    w_in3, w_a2 = wl['gla_w_in'][j], wl['gla_w_a2'][j]
    w_o = _rows_full(wl['gla_w_o'][j])
    w_in = jnp.transpose(w_in3, (1, 0, 2)).reshape(w_in3.shape[1], N_DEV * w_in3.shape[2])
    heads = w_o.shape[0] // GLA_DV
    n_main = 2 * heads * GLA_DK + 2 * heads * GLA_DV
    w_a = jnp.pad(w_in[:, n_main:], ((0, 0), (0, LANES - GLA_RANK)))
    w_a2_p = jnp.pad(w_a2, ((0, LANES - GLA_RANK), (0, 0))).astype(BF16)
    hm = linear(x, w_in[:, :n_main], F32)
    ha = linear(x, w_a, BF16)
    (la,) = ops['gla_gate'](linear(ha, w_a2_p, F32), wl['gla_b_a'][j])
    return linear(gla_core(hm, la, wl['gla_o_norm'][j], heads), w_o, F32)


def _ffn_in(x, m, wl, i, ops, bp):
    (x1,) = ops['ln_res'](x, m, wl['ln1_g'][i], wl['ln1_b'][i])
    cw, cb = wl['ffn_conv_w'][i], wl['ffn_conv_b'][i]
    bu = cw.shape[1] // N_DEV
    cwp = jnp.pad(cw.reshape(CONV_W, N_DEV, bu), ((0, 0), (0, 0), (0, bp - bu))).reshape(CONV_W, N_DEV * bp)
    cbp = jnp.pad(cb.reshape(1, N_DEV, bu), ((0, 0), (0, 0), (0, bp - bu))).reshape(1, N_DEV * bp)
    return x1, cwp, cbp


def _ffn_out(x1, a, wl, p_i, i, ops):
    w_down3 = wl['ffn_w_down'][i]
    half, bu, d_model = N_DEV // 2, 2 * w_down3.shape[1], w_down3.shape[2]
    bp = a.shape[1] // half
    w_down = jnp.pad(w_down3.reshape(half, bu, d_model), ((0, 0), (0, bp - bu), (0, 0))).reshape(half * bp, d_model)
    f = linear(a, w_down, F32)
    (x2,) = ops['ln_res'](x1, f, wl['ln2_g'][i], wl['ln2_b'][i])
    glog = linear(x2, _rows_full(wl['ple_w_gate'][i]), F32)
    pp = linear(p_i, wl['ple_w_proj'][i], F32)
    (x,) = ops['ple'](x2, glog, pp, wl['ple_b_gate'][i])
    return x


def _layer(x, wl, p_i, aux, i, ops):
    j = i // 2
    if i % 2 == 0:
        q_raw, kn, v, kr_raw = _mla_pre(x, wl, j)
        o = make_attention(aux, _mla_heads(wl, j))[0](q_raw, kn, v, kr_raw)
        m = linear(o, _rows_full(wl['mla_w_o'][j]), F32)
    else:
        m = _gla_mixer(x, wl, j, ops)
    x1, cwp, cbp = _ffn_in(x, m, wl, i, ops, wl['ffn_w_up'][i].shape[2])
    return _ffn_out(x1, ffn_hidden(x1, wl['ffn_w_up'][i], cwp, cbp), wl, p_i, i, ops)


def kernel(x, p, positions, mla_w_in, mla_q_norm, mla_kv_norm, mla_w_uq, mla_w_uk, mla_w_uv, mla_w_o, gla_w_in, gla_w_a2, gla_b_a, gla_o_norm, gla_w_o, ln1_g, ln1_b, ln2_g, ln2_b, ffn_w_up, ffn_conv_w, ffn_conv_b, ffn_w_down, ple_w_proj, ple_w_gate, ple_b_gate, loss_target, m_mla_w_in, m_mla_q_norm, m_mla_kv_norm, m_mla_w_uq, m_mla_w_uk, m_mla_w_uv, m_mla_w_o, m_gla_w_in, m_gla_w_a2, m_gla_b_a, m_gla_o_norm, m_gla_w_o, m_ln1_g, m_ln1_b, m_ln2_g, m_ln2_b, m_ffn_w_up, m_ffn_conv_w, m_ffn_conv_b, m_ffn_w_down, m_ple_w_proj, m_ple_w_gate, m_ple_b_gate, v_mla_w_in, v_mla_q_norm, v_mla_kv_norm, v_mla_w_uq, v_mla_w_uk, v_mla_w_uv, v_mla_w_o, v_gla_w_in, v_gla_w_a2, v_gla_b_a, v_gla_o_norm, v_gla_w_o, v_ln1_g, v_ln1_b, v_ln2_g, v_ln2_b, v_ffn_w_up, v_ffn_conv_w, v_ffn_conv_b, v_ffn_w_down, v_ple_w_proj, v_ple_w_gate, v_ple_b_gate):
    w = dict(zip(WEIGHTS, (mla_w_in, mla_q_norm, mla_kv_norm, mla_w_uq, mla_w_uk, mla_w_uv, mla_w_o, gla_w_in, gla_w_a2,
                           gla_b_a, gla_o_norm, gla_w_o, ln1_g, ln1_b, ln2_g, ln2_b, ffn_w_up, ffn_conv_w, ffn_conv_b,
                           ffn_w_down, ple_w_proj, ple_w_gate, ple_b_gate)))
    m_in = dict(zip(WEIGHTS, (m_mla_w_in, m_mla_q_norm, m_mla_kv_norm, m_mla_w_uq, m_mla_w_uk, m_mla_w_uv, m_mla_w_o,
                              m_gla_w_in, m_gla_w_a2, m_gla_b_a, m_gla_o_norm, m_gla_w_o, m_ln1_g, m_ln1_b, m_ln2_g,
                              m_ln2_b, m_ffn_w_up, m_ffn_conv_w, m_ffn_conv_b, m_ffn_w_down, m_ple_w_proj, m_ple_w_gate,
                              m_ple_b_gate)))
    v_in = dict(zip(WEIGHTS, (v_mla_w_in, v_mla_q_norm, v_mla_kv_norm, v_mla_w_uq, v_mla_w_uk, v_mla_w_uv, v_mla_w_o,
                              v_gla_w_in, v_gla_w_a2, v_gla_b_a, v_gla_o_norm, v_gla_w_o, v_ln1_g, v_ln1_b, v_ln2_g,
                              v_ln2_b, v_ffn_w_up, v_ffn_conv_w, v_ffn_conv_b, v_ffn_w_down, v_ple_w_proj, v_ple_w_gate,
                              v_ple_b_gate)))
    _uid[0] = itertools.count()
    x2d, target, pos = x[0], loss_target[0], positions[0]
    p3 = p[:, 0]
    dims = {'h_width': mla_w_uq.shape[1] + mla_w_uk.shape[1] + LANES}

    depth = ln1_g.shape[0]
    ops = _ops(depth)
    cid = pos // CHUNK
    aux = {'rope': _rope_tables(pos), 'cidq': cid[:, None], 'cidk': cid[None, :],
           'need': _mask_table(cid, _attn_tile(pos.shape[0]))}

    in_layer0 = set(MLA_PRE_W + FFN_IN_W + FFN_OUT_W + ['ffn_w_up'])
    prepped = {n: _prep_big(n, w[n], dims) for n in BIG}
    first_names = [n for n in BIG if n in in_layer0]
    rest_names = [n for n in BIG if prepped[n].shape[0] > (1 if n in in_layer0 else 0)]
    rest_from = {n: (1 if n in in_layer0 else 0) for n in rest_names}
    small3 = [_as3(w[n]) for n in SMALL]
    small_packed, small_sizes = _pack([s.reshape(1, -1) for s in small3], F32, 8)
    first = all_gather([prepped[n][:1] for n in first_names] + [small_packed])
    wl = {n: [None] * prepped[n].shape[0] for n in BIG}
    for n, g in zip(first_names, first):
        wl[n][0] = g[0]
    for n, s3, flat in zip(SMALL, small3, _unpack(first[-1][0], small_sizes)):
        wl[n] = _unshard(flat.reshape(N_DEV, *s3.shape), SHARD_AXIS[n] if w[n].ndim == 3 else 2)
    for n in REPL:
        wl[n] = [w[n][l][None, :] for l in range(w[n].shape[0])]

    def pick(names, layer0):
        return {n: [wl[n][l] if (l == 0 and n in in_layer0) == layer0 else None for l in range(len(wl[n]))] for n in names}

    heads0 = _mla_heads(wl, 0)
    _, attn_run, attn_bwd = make_attention(aux, heads0)
    pre, vjp_pre = jax.vjp(lambda x_, wl_: _mla_pre(x_, wl_, 0), x2d, pick(MLA_PRE_W, True))
    o, attn_res, partial = attn_run(*pre, comm_gather_own([prepped[n][rest_from[n]:] for n in rest_names]))
    w_up0 = wl['ffn_w_up'][0]

    def ffn_in(x_, o_, wl_):
        return _ffn_in(x_, linear(o_, _rows_full(wl_['mla_w_o'][0]), F32), wl_, 0, ops, w_up0.shape[2])

    (x1, cwp, cbp), vjp_in = jax.vjp(ffn_in, x2d, o, pick(FFN_IN_W, True))
    a, ffn_res, rest = _ffn_hidden_fwd(x1, w_up0, cwp, cbp, comm_gather_pass(partial))
    for n, g in zip(rest_names, rest):
        for l in range(g.shape[0]):
            wl[n][rest_from[n] + l] = g[l]
    x_l0, vjp_out = jax.vjp(lambda x1_, a_, wl_: _ffn_out(x1_, a_, wl_, p3[0], 0, ops), x1, a, pick(FFN_OUT_W, True))

    def tail(x_, wl_):
        for i in range(1, depth):
            x_ = _layer(x_, wl_, p3[i], aux, i, ops)
        return x_

    y, vjp_tail = jax.vjp(tail, x_l0, pick(WEIGHTS, False))
    sq, dy = _loss_call(y, target)

    dwl = {n: [None] * len(wl[n]) for n in WEIGHTS}

    def keep(part):
        for n, per_layer in part.items():
            for l, g in enumerate(per_layer):
                if g is not None:
                    dwl[n][l] = g

    x_c, y_c, c_place = _place()
    c_idx = jnp.reshape(c_place, (1,)).astype(jnp.int32)
    chip_idx = jnp.reshape(2 * x_c + y_c, (1,)).astype(jnp.int32)
    dx_l0, d_tail = vjp_tail(dy)
    keep(d_tail)
    g_rest = [jnp.stack(dwl[n][rest_from[n]:], axis=0) for n in rest_names]
    dx1_out, da, d_out = vjp_out(dx_l0)
    keep(d_out)
    (dx1_ffn, dw_up0, dcwp, dcbp), recv1 = _ffn_hidden_bwd(ffn_res, da, comm_rs_pair(g_rest))
    dwl['ffn_w_up'][0] = dw_up0
    p1 = [_rs_pair_add(g, r, c_idx) for g, r in zip(g_rest, recv1)]
    dx_in, do, d_in = vjp_in((dx1_out + dx1_ffn, dcwp, dcbp))
    keep(d_in)
    d_pre_in, recv2 = attn_bwd(attn_res, do, comm_rs_chip(p1))
    red_rest = [_rs_final_add(p_, r, chip_idx) for p_, r in zip(p1, recv2)]
    dx_pre, d_pre = vjp_pre(d_pre_in)
    keep(d_pre)
    dx = dx_pre + dx_in

    small_blocks = [_reshard(dwl[n], SHARD_AXIS[n] if w[n].ndim == 3 else 2).reshape(N_DEV, -1) for n in SMALL]
    small_grad_packed, _ = _pack(small_blocks, F32, 8)
    red_first = reduce_scatter([dwl[n][0][None] for n in first_names] + [small_grad_packed[None]])
    by_layer = {n: [] for n in BIG}
    for n, g in zip(first_names, red_first):
        by_layer[n].append(g)
    for n, g in zip(rest_names, red_rest):
        by_layer[n].append(g)
    grads = {n: _unprep_big(n, jnp.concatenate(by_layer[n], axis=0), w[n].shape) for n in BIG}
    for n, f in zip(SMALL, _unpack(red_first[-1][0], small_sizes)):
        grads[n] = f.reshape(w[n].shape)

    repl_flat = [jnp.concatenate([g.reshape(-1) for g in dwl[n]]).reshape(1, -1) for n in REPL]
    loss_part = 0.5 * jnp.sum(sq) / sq.shape[1]
    packed, repl_sizes = _pack(repl_flat + [loss_part.reshape(1, 1)], F32, 8)
    summed = _unpack(all_reduce_small(packed[0])[None], repl_sizes)
    for n, f in zip(REPL, summed[:-1]):
        grads[n] = f.reshape(w[n].shape)
    loss = summed[-1].reshape(())

    delta, new_m, new_v = {}, {}, {}
    for n in WEIGHTS:
        delta[n], new_m[n], new_v[n] = _adamw_call(w[n], grads[n], m_in[n], v_in[n])
    return (loss, dx[None], *[grads[n] for n in WEIGHTS], *[delta[n] for n in WEIGHTS],
            *[new_m[n] for n in WEIGHTS], *[new_v[n] for n in WEIGHTS])
```

```python
import functools
import itertools

import jax
import jax.numpy as jnp
from jax import lax
from jax.experimental import pallas as pl
from jax.experimental.pallas import tpu as pltpu

F32 = jnp.float32
BF16 = jnp.bfloat16
MESH = pl.DeviceIdType.MESH
N_DEV = 8

EPS = 1e-5
NEG_INF = -1e30
CHUNK = 64
Q_BLOCK = 128
MLA_NOPE = 128
MLA_ROPE = 64
MLA_V = 128
MLA_QK_PAD = 256
ROPE_THETA = 10000.0
GLA_DK = 128
GLA_DV = 256
GLA_RANK = 16
GLA_TAU = 16.0
CONV_W = 3
ADAM_LR = 0.001
ADAM_B1 = 0.9
ADAM_B2 = 0.999
ADAM_EPS = 1e-08
ADAM_WD = 0.01
ADAM_STEP = 10
LOG2E = 1.4426950408889634

LANES = 128
PACK_COLS = 1024
VMEM_LIMIT = 48 * 1024 * 1024
MM_VMEM_BUDGET = 30 * 1024 * 1024

WEIGHTS = ['mla_w_in', 'mla_q_norm', 'mla_kv_norm', 'mla_w_uq', 'mla_w_uk', 'mla_w_uv', 'mla_w_o', 'gla_w_in',
           'gla_w_a2', 'gla_b_a', 'gla_o_norm', 'gla_w_o', 'ln1_g', 'ln1_b', 'ln2_g', 'ln2_b', 'ffn_w_up',
           'ffn_conv_w', 'ffn_conv_b', 'ffn_w_down', 'ple_w_proj', 'ple_w_gate', 'ple_b_gate']
SHARD_AXIS = {'mla_w_in': 1, 'mla_q_norm': None, 'mla_kv_norm': None, 'mla_w_uq': 2, 'mla_w_uk': 2, 'mla_w_uv': 2,
              'mla_w_o': 1, 'gla_w_in': 2, 'gla_w_a2': 2, 'gla_b_a': 1, 'gla_o_norm': 1, 'gla_w_o': 1,
              'ln1_g': None, 'ln1_b': None, 'ln2_g': None, 'ln2_b': None, 'ffn_w_up': 2, 'ffn_conv_w': 2,
              'ffn_conv_b': None, 'ffn_w_down': 1, 'ple_w_proj': 2, 'ple_w_gate': 1, 'ple_b_gate': None}
BIG = ['mla_w_in', 'mla_w_uq', 'mla_w_uk', 'mla_w_uv', 'mla_w_o', 'gla_w_in', 'gla_w_o', 'ffn_w_up', 'ffn_w_down',
       'ple_w_proj', 'ple_w_gate']
SMALL = ['gla_w_a2', 'gla_b_a', 'gla_o_norm', 'ffn_conv_w']
REPL = [n for n in WEIGHTS if SHARD_AXIS[n] is None]

_uid = [itertools.count()]


def _nm(base):
    return f"{base}_{next(_uid[0])}"


def _cp(sem=None):
    return pltpu.CompilerParams(dimension_semantics=sem, vmem_limit_bytes=VMEM_LIMIT)


def _round_up(n, m):
    return -(-n // m) * m


class Comm:
    def __init__(self, inputs, out_shapes, aliases, n_send, n_recv, n_local, build):
        self.inputs, self.out_shapes, self.aliases = list(inputs), list(out_shapes), dict(aliases)
        self.n_send, self.n_recv, self.n_local, self.build = n_send, n_recv, n_local, build


def _pcall(body, name, grid, n_prefetch, in_specs, out_specs, out_shape, scratch, sem, args, comm=None):
    in_specs, out_specs, out_shape, scratch, args = list(in_specs), list(out_specs), list(out_shape), list(scratch), list(args)
    n_in, n_out, n_scr = len(in_specs), len(out_specs), len(scratch)
    aliases = {}
    kernel_body = body
    if comm is not None:
        ci, co = len(comm.inputs), len(comm.out_shapes)
        any_spec = pl.BlockSpec(memory_space=pl.ANY)
        in_specs += [any_spec] * ci
        out_specs += [any_spec] * co
        out_shape += comm.out_shapes
        scratch += [pltpu.SemaphoreType.DMA((comm.n_send,)), pltpu.SemaphoreType.DMA((comm.n_recv,)),
                    pltpu.SemaphoreType.DMA((comm.n_local,))]
        aliases = {n_prefetch + n_in + k: n_out + v for k, v in comm.aliases.items()}
        args += comm.inputs
        sem = ("arbitrary",) * len(grid)

        def kernel_body(*refs):
            pre, r = refs[:n_prefetch], refs[n_prefetch:]
            ins, cin = r[:n_in], r[n_in:n_in + ci]
            outs, cout = r[n_in + ci:n_in + ci + n_out], r[n_in + ci + n_out:n_in + ci + n_out + co]
            scr = r[n_in + ci + n_out + co:n_in + ci + n_out + co + n_scr]
            send_sems, recv_sems, local_sems = r[-3:]
            first = functools.reduce(lambda a, b: a & b, [pl.program_id(d) == 0 for d in range(len(grid))])
            last = functools.reduce(lambda a, b: a & b, [pl.program_id(d) == grid[d] - 1 for d in range(len(grid))])
            starts, waits = comm.build(cin, cout, send_sems, recv_sems, local_sems)

            @pl.when(first)
            def _():
                for cp in starts:
                    cp.start()

            body(*pre, *ins, *outs, *scr)

            @pl.when(last)
            def _():
                for wait in waits:
                    wait()

    grid_spec = pltpu.PrefetchScalarGridSpec(num_scalar_prefetch=n_prefetch, grid=grid, in_specs=in_specs,
                                             out_specs=out_specs, scratch_shapes=scratch)
    return pl.pallas_call(kernel_body, name=_nm(name), grid_spec=grid_spec, out_shape=tuple(out_shape),
                          input_output_aliases=aliases, compiler_params=_cp(sem))(*args)


def _divisor_tiles(n, cap):
    if n % LANES:
        return [n]
    out = [t for t in range(LANES, min(n, cap) + 1, LANES) if n % t == 0]
    return out or [n]


def _mm_tiles(M, N, K, abytes, bbytes, obytes, tn_fixed=None, tk_fixed=None):
    best = None
    for tm in _divisor_tiles(M, 1024):
        for tn in ([tn_fixed] if tn_fixed else _divisor_tiles(N, 1536)):
            for tk in ([tk_fixed] if tk_fixed else _divisor_tiles(K, 2048)):
                vmem = 2 * (tm * tk * abytes + tk * tn * bbytes + tm * tn * obytes) + tm * tn * 4
                if vmem > MM_VMEM_BUDGET:
                    continue
                key = (tm * tn * tk, tk)
                if best is None or key > best[0]:
                    best = (key, (tm, tn, tk))
    assert best is not None, (M, N, K)
    return best[1]


def _mm(a, b, mode, out_dtype, base="mm", blocks=None):
    blk0, nblk = blocks if blocks else (0, 1)
    tn_fixed = tk_fixed = None
    if mode == "nn":
        M, K = a.shape
        N = nblk * b.shape[2] if blocks else b.shape[1]
        tn_fixed = b.shape[2] if blocks else None
    elif mode == "nt":
        M, K = a.shape
        N = b.shape[1] if blocks else b.shape[0]
        tk_fixed = b.shape[2] if blocks else None
        assert not blocks or K == nblk * b.shape[2]
    else:
        (K, M), N = a.shape, b.shape[1]
        tn_fixed = N // nblk if blocks else None
    tm, tn, tk = _mm_tiles(M, N, K, a.dtype.itemsize, b.dtype.itemsize, jnp.dtype(out_dtype).itemsize, tn_fixed, tk_fixed)
    nk = K // tk
    out_shape = jax.ShapeDtypeStruct((M, N), out_dtype)
    out_spec = pl.BlockSpec((tm, tn), lambda i, j, k: (i, j))
    if mode == "nn":
        a_spec = pl.BlockSpec((tm, tk), lambda i, j, k: (i, k))
        b_spec = (pl.BlockSpec((None, tk, tn), lambda i, j, k: (blk0 + j, k, 0)) if blocks
                  else pl.BlockSpec((tk, tn), lambda i, j, k: (k, j)))
        dims = (((1,), (0,)), ((), ()))
    elif mode == "nt":
        a_spec = pl.BlockSpec((tm, tk), lambda i, j, k: (i, k))
        b_spec = (pl.BlockSpec((None, tn, tk), lambda i, j, k: (blk0 + k, j, 0)) if blocks
                  else pl.BlockSpec((tn, tk), lambda i, j, k: (j, k)))
        dims = (((1,), (1,)), ((), ()))
    else:
        a_spec = pl.BlockSpec((tk, tm), lambda i, j, k: (k, i))
        b_spec = pl.BlockSpec((tk, tn), lambda i, j, k: (k, j))
        dims = (((0,), (0,)), ((), ()))
        if blocks:
            out_shape = jax.ShapeDtypeStruct((nblk, M, tn), out_dtype)
            out_spec = pl.BlockSpec((None, tm, tn), lambda i, j, k: (j, i, 0))

    def body(a_ref, b_ref, o_ref, acc_ref):
        part = lax.dot_general(a_ref[...].astype(BF16), b_ref[...].astype(BF16), dims, preferred_element_type=F32)
        if nk == 1:
            o_ref[...] = part.astype(o_ref.dtype)
        else:
            k = pl.program_id(2)

            @pl.when(k == 0)
            def _():
                acc_ref[...] = part

            @pl.when(k > 0)
            def _():
                acc_ref[...] += part

            @pl.when(k == nk - 1)
            def _():
                o_ref[...] = acc_ref[...].astype(o_ref.dtype)

    return pl.pallas_call(
        body, name=_nm(base), grid=(M // tm, N // tn, nk), out_shape=out_shape,
        in_specs=[a_spec, b_spec], out_specs=out_spec,
        scratch_shapes=[pltpu.VMEM((tm, tn) if nk > 1 else (8, LANES), F32)],
        compiler_params=_cp(("parallel", "parallel", "arbitrary")),
    )(a, b)


def _all_blocks(w):
    return (0, w.shape[0]) if w.ndim == 3 else None


@functools.partial(jax.custom_vjp, nondiff_argnums=(2,))
def linear(a, w, out_dtype):
    return _mm(a, w, "nn", out_dtype, "lin_fwd", _all_blocks(w))


def _linear_fwd(a, w, out_dtype):
    return _mm(a, w, "nn", out_dtype, "lin_fwd", _all_blocks(w)), (a, w)


def _linear_bwd(out_dtype, res, dy):
    a, w = res
    return (_mm(dy, w, "nt", a.dtype, "lin_dx", _all_blocks(w)), _mm(a, dy, "tn", w.dtype, "lin_dw", _all_blocks(w)))


linear.defvjp(_linear_fwd, _linear_bwd)


def _row_tile(S, width):
    tr = 512 if width <= 1024 else 256
    return min(tr, S)


def _rw_fwd(f, rows, params, out_dtypes, base):
    S = rows[0][0].shape[0]
    tr = _row_tile(S, max(w for _, w, _ in rows))
    n_in = len(rows) + len(params)
    avals = [jax.ShapeDtypeStruct((tr, w), F32) for _, w, _ in rows] + [jax.ShapeDtypeStruct(p.shape, F32) for p in params]
    outs = jax.eval_shape(f, *avals)

    def body(*refs):
        vals = [r[...].astype(F32) for r in refs[:n_in]]
        for o_ref, r in zip(refs[n_in:], f(*vals)):
            o_ref[...] = r.astype(o_ref.dtype)

    in_specs = [pl.BlockSpec((tr, w), functools.partial(lambda i, cb: (i, cb), cb=cb)) for _, w, cb in rows]
    in_specs += [pl.BlockSpec(p.shape, lambda i: (0, 0)) for p in params]
    return pl.pallas_call(
        body, name=_nm(base), grid=(S // tr,),
        out_shape=tuple(jax.ShapeDtypeStruct((S, o.shape[1]), dt) for o, dt in zip(outs, out_dtypes)),
        in_specs=in_specs, out_specs=tuple(pl.BlockSpec((tr, o.shape[1]), lambda i: (i, 0)) for o in outs),
        compiler_params=_cp(("parallel",)),
    )(*[a for a, _, _ in rows], *params)


def _rw_bwd(f, rows, params, cts, row_grad_dtypes, base):
    S = rows[0][0].shape[0]
    tr = _row_tile(S, max(w for _, w, _ in rows))
    n_rows, n_par, n_ct = len(rows), len(params), len(cts)
    want = [k for k, dt in enumerate(row_grad_dtypes) if dt is not None]

    def body(*refs):
        in_refs = refs[:n_rows + n_par]
        ct_refs = refs[n_rows + n_par:n_rows + n_par + n_ct]
        out_refs = refs[n_rows + n_par + n_ct:]
        vals = [r[...].astype(F32) for r in in_refs]
        _, vjp_fn = jax.vjp(f, *vals)
        grads = vjp_fn(tuple(c[...].astype(F32) for c in ct_refs))
        for o_ref, k in zip(out_refs[:len(want)], want):
            o_ref[...] = grads[k].astype(o_ref.dtype)
        i = pl.program_id(0)
        for o_ref, g in zip(out_refs[len(want):], grads[n_rows:]):
            @pl.when(i == 0)
            def _(o_ref=o_ref, g=g):
                o_ref[...] = g

            @pl.when(i > 0)
            def _(o_ref=o_ref, g=g):
                o_ref[...] += g

    in_specs = [pl.BlockSpec((tr, w), functools.partial(lambda i, cb: (i, cb), cb=cb)) for _, w, cb in rows]
    in_specs += [pl.BlockSpec(p.shape, lambda i: (0, 0)) for p in params]
    in_specs += [pl.BlockSpec((tr, c.shape[1]), lambda i: (i, 0)) for c in cts]
    out_shape = [jax.ShapeDtypeStruct((S, rows[k][1]), row_grad_dtypes[k]) for k in want]
    out_specs = [pl.BlockSpec((tr, rows[k][1]), lambda i: (i, 0)) for k in want]
    out_shape += [jax.ShapeDtypeStruct(p.shape, F32) for p in params]
    out_specs += [pl.BlockSpec(p.shape, lambda i: (0, 0)) for p in params]
    res = pl.pallas_call(
        body, name=_nm(base), grid=(S // tr,), out_shape=tuple(out_shape),
        in_specs=in_specs, out_specs=tuple(out_specs), compiler_params=_cp(("arbitrary",)),
    )(*[a for a, _, _ in rows], *params, *cts)
    row_grads = [None] * n_rows
    for k, g in zip(want, res[:len(want)]):
        row_grads[k] = g
    return row_grads, list(res[len(want):])


def rw_op(f, base, n_rows, out_dtypes):
    @jax.custom_vjp
    def op(*args):
        return fwd(*args)[0]

    def split(args):
        rows = [(a, a.shape[1], 0) for a in args[:n_rows]]
        return rows, list(args[n_rows:])

    def fwd(*args):
        rows, params = split(args)
        return tuple(_rw_fwd(f, rows, params, out_dtypes, base + "_fwd")), args

    def bwd(args, cts):
        rows, params = split(args)
        rg, pg = _rw_bwd(f, rows, params, list(cts), [a.dtype for a, _, _ in rows], base + "_bwd")
        return tuple(rg) + tuple(g.astype(p.dtype) for g, p in zip(pg, params))

    op.defvjp(fwd, bwd)
    return op


def _ln_res_fn(alpha):
    def f(x, m, g, b):
        z = alpha * x + m
        mu = jnp.mean(z, -1, keepdims=True)
        zc = z - mu
        var = jnp.mean(zc * zc, -1, keepdims=True)
        return (zc * lax.rsqrt(var + EPS) * g + b,)
    return f


def _rms(x, g):
    return x * lax.rsqrt(jnp.mean(x * x, -1, keepdims=True) + EPS) * g


def _mla_norm_fn(q_lora, kv_lora):
    def f(h, qn, kvn):
        return (_rms(h[:, :q_lora], qn), _rms(h[:, q_lora:q_lora + kv_lora], kvn),
                h[:, q_lora + kv_lora:q_lora + kv_lora + LANES])
    return f


def _log_sigmoid(z):
    return jnp.minimum(z, 0.0) - jnp.log(1.0 + jnp.exp(-jnp.abs(z)))


def _gla_gate_fn(z, b):
    return (_log_sigmoid(z + b) / GLA_TAU,)


def _ple_fn(x, glog, pp, b):
    return (x + jax.nn.sigmoid(glog + b) * pp,)


def _gla_out_fn(heads):
    def f(o, r, g):
        parts = []
        for h in range(heads):
            oh = o[:, h * GLA_DV:(h + 1) * GLA_DV]
            mu = jnp.mean(oh, -1, keepdims=True)
            oc = oh - mu
            var = jnp.mean(oc * oc, -1, keepdims=True)
            parts.append(oc * lax.rsqrt(var + EPS) * g[:, h * GLA_DV:(h + 1) * GLA_DV])
        return (jnp.concatenate(parts, axis=1) * (r * jax.nn.sigmoid(r)),)
    return f


def _rope_call(x, tabs, roped, out_dtype, base, fold=False):
    S, C = x.shape
    nb = C // LANES
    tr = min(512 if C <= 1024 else 256, S)
    out_c = LANES if fold else C

    def rot(v, a, b1, b2):
        return v * a + pltpu.roll(v, 96, 1) * b1 + pltpu.roll(v, 32, 1) * b2

    def body(x_ref, a_ref, b1_ref, b2_ref, o_ref):
        a, b1, b2 = a_ref[...], b1_ref[...], b2_ref[...]
        if fold:
            v = x_ref[:, 0:LANES].astype(F32)
            for blk in range(1, nb):
                v = v + x_ref[:, blk * LANES:(blk + 1) * LANES].astype(F32)
            o_ref[...] = rot(v, a, b1, b2).astype(o_ref.dtype)
            return
        for blk in range(nb):
            v = x_ref[:, blk * LANES:(blk + 1) * LANES].astype(F32)
            if roped(blk):
                v = rot(v, a, b1, b2)
            o_ref[:, blk * LANES:(blk + 1) * LANES] = v.astype(o_ref.dtype)

    row = lambda w: pl.BlockSpec((tr, w), lambda i: (i, 0))
    return pl.pallas_call(
        body, name=_nm(base), grid=(S // tr,), out_shape=jax.ShapeDtypeStruct((S, out_c), out_dtype),
        in_specs=[row(C), row(LANES), row(LANES), row(LANES)], out_specs=row(out_c),
        compiler_params=_cp(("parallel",)),
    )(x, *tabs)


def _attn_tile(S):
    return min(512, S)


def _tri_schedule(n, by_key):
    pairs = ([(i, j) for j in range(n) for i in range(j, n)] if by_key
             else [(i, j) for i in range(n) for j in range(i + 1)])
    return (jnp.asarray([p[0] for p in pairs], jnp.int32), jnp.asarray([p[1] for p in pairs], jnp.int32))


def _mask_table(cid, t):
    n = cid.shape[0] // t
    blocks = cid.reshape(n, t)
    cmin_q, cmax_k = jnp.min(blocks, axis=1), jnp.max(blocks, axis=1)
    need = (cmax_k[None, :] > cmin_q[:, None]) | jnp.eye(n, dtype=bool)
    return need.astype(jnp.int32).reshape(n * n)


def _attn_mask(cidq_ref, cidk_ref, i, j, t):
    qrow = i * t + lax.broadcasted_iota(jnp.int32, (t, 1), 0)
    kcol = j * t + lax.broadcasted_iota(jnp.int32, (1, t), 1)
    qlim = (qrow // Q_BLOCK + 1) * Q_BLOCK
    return (cidk_ref[...] <= cidq_ref[...]) & (kcol < qlim)


ATTN_FWD_HEADS = 2


def _attn_fwd_call(q, kn, v, kr, aux, heads, comm=None):
    S = q.shape[0]
    t = _attn_tile(S)
    n = S // t
    hp = ATTN_FWD_HEADS if heads % ATTN_FWD_HEADS == 0 else 1
    qi_tab, kj_tab = _tri_schedule(n, False)
    scale2 = (MLA_NOPE + MLA_ROPE) ** -0.5 * LOG2E

    def body(qi_ref, kj_ref, need_ref, q_ref, kn_ref, v_ref, kr_ref, cidq_ref, cidk_ref, o_ref, lse_ref,
             m_ref, l_ref, acc_ref):
        st = pl.program_id(1)
        i, j = qi_ref[st], kj_ref[st]

        @pl.when(j == 0)
        def _():
            m_ref[...] = jnp.full(m_ref.shape, NEG_INF, F32)
            l_ref[...] = jnp.zeros(l_ref.shape, F32)
            acc_ref[...] = jnp.zeros(acc_ref.shape, F32)

        def update(masked):
            mask = _attn_mask(cidq_ref, cidk_ref, i, j, t) if masked else None
            for hh in range(hp):
                lanes = slice(hh * LANES, (hh + 1) * LANES)
                k = jnp.concatenate([kn_ref[:, lanes], kr_ref[...]], axis=1)
                qh = q_ref[:, hh * MLA_QK_PAD:(hh + 1) * MLA_QK_PAD]
                s = lax.dot_general(qh, k, (((1,), (1,)), ((), ())), preferred_element_type=F32) * scale2
                if masked:
                    s = jnp.where(mask, s, NEG_INF)
                m_prev = m_ref[:, lanes]
                m_new = jnp.maximum(m_prev, jnp.max(s, axis=1, keepdims=True))
                alpha = jnp.exp2(m_prev - m_new)
                p = jnp.exp2(s - m_new[:, :1])
                l_ref[:, lanes] = alpha * l_ref[:, lanes] + jnp.sum(p, axis=1, keepdims=True)
                acc_ref[:, lanes] = alpha * acc_ref[:, lanes] + jnp.dot(p.astype(BF16), v_ref[:, lanes],
                                                                        preferred_element_type=F32)
                m_ref[:, lanes] = m_new

        need = need_ref[i * n + j]

        @pl.when(need != 0)
        def _():
            update(True)

        @pl.when(need == 0)
        def _():
            update(False)

        @pl.when(j == i)
        def _():
            o_ref[...] = (acc_ref[...] / l_ref[...]).astype(o_ref.dtype)
            lse_ref[...] = m_ref[...] + jnp.log(l_ref[...]) * LOG2E

    qmap = lambda h, s, qi, kj, need: (qi[s], h)
    kmap = lambda h, s, qi, kj, need: (kj[s], h)
    return _pcall(
        body, "attn_fwd", (heads // hp, qi_tab.shape[0]), 3,
        in_specs=[pl.BlockSpec((t, hp * MLA_QK_PAD), qmap), pl.BlockSpec((t, hp * MLA_NOPE), kmap),
                  pl.BlockSpec((t, hp * MLA_V), kmap),
                  pl.BlockSpec((t, LANES), lambda h, s, qi, kj, need: (kj[s], 0)),
                  pl.BlockSpec((t, 1), lambda h, s, qi, kj, need: (qi[s], 0)),
                  pl.BlockSpec((1, t), lambda h, s, qi, kj, need: (0, kj[s]))],
        out_specs=[pl.BlockSpec((t, hp * MLA_V), qmap), pl.BlockSpec((t, hp * LANES), qmap)],
        out_shape=[jax.ShapeDtypeStruct((S, heads * MLA_V), BF16), jax.ShapeDtypeStruct((S, heads * LANES), F32)],
        scratch=[pltpu.VMEM((t, hp * LANES), F32), pltpu.VMEM((t, hp * LANES), F32), pltpu.VMEM((t, hp * MLA_V), F32)],
        sem=("parallel", "arbitrary"),
        args=[qi_tab, kj_tab, aux['need'], q, kn, v, kr, aux['cidq'], aux['cidk']], comm=comm)


def _attn_bwd_call(q, kn, v, kr, o, lse, do, aux, heads, comm=None):
    S = q.shape[0]
    t = _attn_tile(S)
    n = S // t
    qi_tab, kj_tab = _tri_schedule(n, True)
    scale = (MLA_NOPE + MLA_ROPE) ** -0.5
    scale2 = scale * LOG2E
    nt_dims = (((1,), (1,)), ((), ()))
    tn_dims = (((0,), (0,)), ((), ()))

    n_steps = n * (n + 1) // 2

    def body(qi_ref, kj_ref, need_ref, q_ref, kn_ref, v_ref, kr_ref, cidq_ref, cidk_ref, o_ref, lse_ref, do_ref,
             ta_ref, tb1_ref, tb2_ref, dq_ref, dkn_ref, dv_ref, dkr_ref, dk_acc, dv_acc, dq_acc):
        st = pl.program_id(1)
        i, j = qi_ref[st], kj_ref[st]

        @pl.when(st == 0)
        def _():
            dq_acc[...] = jnp.zeros(dq_acc.shape, F32)

        @pl.when(i == j)
        def _():
            dk_acc[...] = jnp.zeros(dk_acc.shape, F32)
            dv_acc[...] = jnp.zeros(dv_acc.shape, F32)

        k = jnp.concatenate([kn_ref[...], kr_ref[...]], axis=1)
        qt, do = q_ref[...], do_ref[...]
        s = lax.dot_general(qt, k, nt_dims, preferred_element_type=F32) * scale2
        dp = lax.dot_general(do, v_ref[...], nt_dims, preferred_element_type=F32)
        dsum = jnp.sum(do.astype(F32) * o_ref[...].astype(F32), axis=1, keepdims=True)
        rows = pl.ds(pl.multiple_of(i * t, t), t)

        def grads(s):
            p = jnp.exp2(s - lse_ref[:, :1])
            ds = (p * (dp - dsum) * scale).astype(BF16)
            dv_acc[...] += lax.dot_general(p.astype(BF16), do, tn_dims, preferred_element_type=F32)
            dk_acc[...] += lax.dot_general(ds, qt, tn_dims, preferred_element_type=F32)
            dq_acc[rows, :] += jnp.dot(ds, k, preferred_element_type=F32)

        need = need_ref[i * n + j]

        @pl.when(need != 0)
        def _():
            grads(jnp.where(_attn_mask(cidq_ref, cidk_ref, i, j, t), s, NEG_INF))

        @pl.when(need == 0)
        def _():
            grads(s)

        @pl.when(i == n - 1)
        def _():
            dkn_ref[...] = dk_acc[:, :MLA_NOPE].astype(dkn_ref.dtype)
            dv_ref[...] = dv_acc[...].astype(dv_ref.dtype)
            dkr_ref[...] = dk_acc[:, MLA_NOPE:]

        @pl.when(st == n_steps - 1)
        def _():
            for r in range(n):
                rs = slice(r * t, (r + 1) * t)
                dq_ref[rs, :MLA_NOPE] = dq_acc[rs, :MLA_NOPE].astype(dq_ref.dtype)
                g = dq_acc[rs, MLA_NOPE:]
                g = g * ta_ref[rs, :] + pltpu.roll(g, 96, 1) * tb1_ref[rs, :] + pltpu.roll(g, 32, 1) * tb2_ref[rs, :]
                dq_ref[rs, MLA_NOPE:] = g.astype(dq_ref.dtype)

    qmap = lambda h, s, qi, kj, need: (qi[s], h)
    kmap = lambda h, s, qi, kj, need: (kj[s], h)
    whole = pl.BlockSpec((S, LANES), lambda h, s, qi, kj, need: (0, 0))
    return _pcall(
        body, "attn_bwd", (heads, qi_tab.shape[0]), 3,
        in_specs=[pl.BlockSpec((t, MLA_QK_PAD), qmap), pl.BlockSpec((t, MLA_NOPE), kmap), pl.BlockSpec((t, MLA_V), kmap),
                  pl.BlockSpec((t, LANES), lambda h, s, qi, kj, need: (kj[s], 0)),
                  pl.BlockSpec((t, 1), lambda h, s, qi, kj, need: (qi[s], 0)),
                  pl.BlockSpec((1, t), lambda h, s, qi, kj, need: (0, kj[s])),
                  pl.BlockSpec((t, MLA_V), qmap), pl.BlockSpec((t, LANES), qmap), pl.BlockSpec((t, MLA_V), qmap),
                  whole, whole, whole],
        out_specs=[pl.BlockSpec((S, MLA_QK_PAD), lambda h, s, qi, kj, need: (0, h)),
                   pl.BlockSpec((t, MLA_NOPE), kmap), pl.BlockSpec((t, MLA_V), kmap), pl.BlockSpec((t, LANES), kmap)],
        out_shape=[jax.ShapeDtypeStruct((S, heads * MLA_QK_PAD), BF16), jax.ShapeDtypeStruct((S, heads * MLA_NOPE), BF16),
                   jax.ShapeDtypeStruct((S, heads * MLA_V), BF16), jax.ShapeDtypeStruct((S, heads * LANES), F32)],
        scratch=[pltpu.VMEM((t, MLA_QK_PAD), F32), pltpu.VMEM((t, MLA_V), F32), pltpu.VMEM((S, MLA_QK_PAD), F32)],
        sem=("parallel", "arbitrary"),
        args=[qi_tab, kj_tab, aux['need'], q, kn, v, kr, aux['cidq'], aux['cidk'], o, lse, do, *aux['rope'][1]],
        comm=comm)


def make_attention(aux, heads):
    tabs_f, tabs_b = aux['rope']
    odd, every = (lambda blk: blk % 2 == 1), (lambda blk: True)

    def run_host(q_raw, kn, v, kr_raw, comm=None):
        q = _rope_call(q_raw, tabs_f, odd, BF16, "rope_q")
        kr = _rope_call(kr_raw, tabs_f, every, BF16, "rope_k")
        o, lse, *carried = _attn_fwd_call(q, kn, v, kr, aux, heads, comm)
        return o, (q, kn, v, kr, o, lse), carried

    def bwd_host(res, do, comm=None):
        q, kn, v, kr, o, lse = res
        dq_raw, dkn, dv, dkr, *carried = _attn_bwd_call(q, kn, v, kr, o, lse, do, aux, heads, comm)
        return (dq_raw, dkn, dv, _rope_call(dkr, tabs_b, every, F32, "rope_dk", fold=True)), carried

    @jax.custom_vjp
    def attn(q_raw, kn, v, kr_raw):
        return run_host(q_raw, kn, v, kr_raw)[0]

    attn.defvjp(lambda *a: run_host(*a)[:2], lambda res, do: bwd_host(res, do)[0])
    return attn, run_host, bwd_host


GLA_ROWS = 256


def _tri(lower):
    r = lax.broadcasted_iota(jnp.int32, (CHUNK, CHUNK), 0)
    c = lax.broadcasted_iota(jnp.int32, (CHUNK, CHUNK), 1)
    return jnp.where((c <= r) if lower else (c >= r), 1.0, 0.0).astype(F32)


def _gla_chunk(q_ref, k_ref, v_ref, la_ref, sl):
    la = la_ref[sl, :]
    cum = jnp.dot(_tri(True), la, preferred_element_type=F32, precision=lax.Precision.HIGHEST)
    tot = cum[CHUNK - 1:CHUNK, :]
    e = jnp.exp(tot - cum)
    k = k_ref[sl, :].astype(F32)
    kdec = k * e
    v = v_ref[sl, :]
    upd_t = lax.dot_general(v.astype(BF16), kdec.astype(BF16), (((0,), (0,)), ((), ())), preferred_element_type=F32)
    qs = (q_ref[sl, :].astype(F32) * (GLA_DK ** -0.5)).astype(BF16)
    return e, k, kdec, v, upd_t, jnp.exp(tot), qs


def _gla_specs(heads, rows_map):
    return [pl.BlockSpec((GLA_ROWS, GLA_DK), lambda h, b: (rows_map(b), h)),
            pl.BlockSpec((GLA_ROWS, GLA_DK), lambda h, b: (rows_map(b), heads + h)),
            pl.BlockSpec((GLA_ROWS, GLA_DV), lambda h, b: (rows_map(b), heads + h)),
            pl.BlockSpec((GLA_ROWS, GLA_DK), lambda h, b: (rows_map(b), h))]


def _gla_fwd_call(hm, la, heads):
    S = hm.shape[0]
    assert S % GLA_ROWS == 0
    nb, cpb = S // GLA_ROWS, GLA_ROWS // CHUNK

    def body(q_ref, k_ref, v_ref, la_ref, o_ref, sp_ref, st_ref):
        @pl.when(pl.program_id(1) == 0)
        def _():
            st_ref[...] = jnp.zeros(st_ref.shape, F32)

        for c in range(cpb):
            sl = slice(c * CHUNK, (c + 1) * CHUNK)
            _, _, _, _, upd_t, decay, qs = _gla_chunk(q_ref, k_ref, v_ref, la_ref, sl)
            state = st_ref[...]
            sp_ref[0, c] = state
            state = state * decay + upd_t
            st_ref[...] = state
            o_ref[sl, :] = lax.dot_general(qs, state.astype(BF16), (((1,), (1,)), ((), ())), preferred_element_type=F32)

    return pl.pallas_call(
        body, name=_nm("gla_fwd"), grid=(heads, nb),
        out_shape=(jax.ShapeDtypeStruct((S, heads * GLA_DV), F32),
                   jax.ShapeDtypeStruct((heads, S // CHUNK, GLA_DV, GLA_DK), F32)),
        in_specs=_gla_specs(heads, lambda b: b),
        out_specs=(pl.BlockSpec((GLA_ROWS, GLA_DV), lambda h, b: (b, h)),
                   pl.BlockSpec((1, cpb, GLA_DV, GLA_DK), lambda h, b: (h, b, 0, 0))),
        scratch_shapes=[pltpu.VMEM((GLA_DV, GLA_DK), F32)],
        compiler_params=_cp(("parallel", "arbitrary")),
    )(hm, hm, hm, la)


def _gla_bwd_call(hm, la, sprev, do, heads):
    S = hm.shape[0]
    nb, cpb = S // GLA_ROWS, GLA_ROWS // CHUNK
    scale = GLA_DK ** -0.5

    def body(q_ref, k_ref, v_ref, la_ref, sp_ref, do_ref, dq_ref, dk_ref, dv_ref, dla_ref, carry_ref):
        @pl.when(pl.program_id(1) == 0)
        def _():
            carry_ref[...] = jnp.zeros(carry_ref.shape, F32)

        for c in reversed(range(cpb)):
            sl = slice(c * CHUNK, (c + 1) * CHUNK)
            e, k, kdec, v, upd_t, decay, qs = _gla_chunk(q_ref, k_ref, v_ref, la_ref, sl)
            sp = sp_ref[0, c]
            s_n = sp * decay + upd_t
            dob = do_ref[sl, :].astype(BF16)
            g = carry_ref[...] + lax.dot_general(dob, qs, (((0,), (0,)), ((), ())), preferred_element_type=F32)
            gb = g.astype(BF16)
            dq_ref[sl, :] = (jnp.dot(dob, s_n.astype(BF16), preferred_element_type=F32) * scale).astype(dq_ref.dtype)
            ddecay = jnp.sum(g * sp, axis=0, keepdims=True)
            dkdec = jnp.dot(v.astype(BF16), gb, preferred_element_type=F32)
            dv_ref[sl, :] = lax.dot_general(kdec.astype(BF16), gb, (((1,), (1,)), ((), ())),
                                            preferred_element_type=F32).astype(dv_ref.dtype)
            dk_ref[sl, :] = (dkdec * e).astype(dk_ref.dtype)
            w = dkdec * k * e
            dtot = jnp.sum(w, axis=0, keepdims=True) + ddecay * decay
            last = lax.broadcasted_iota(jnp.int32, (CHUNK, 1), 0) == CHUNK - 1
            dcum = jnp.where(last, dtot - w, -w)
            dla_ref[sl, :] = jnp.dot(_tri(False), dcum, preferred_element_type=F32, precision=lax.Precision.HIGHEST)
            carry_ref[...] = g * decay

    rev = lambda b: nb - 1 - b
    return pl.pallas_call(
        body, name=_nm("gla_bwd"), grid=(heads, nb),
        out_shape=(jax.ShapeDtypeStruct((S, heads * GLA_DK), hm.dtype), jax.ShapeDtypeStruct((S, heads * GLA_DK), hm.dtype),
                   jax.ShapeDtypeStruct((S, heads * GLA_DV), hm.dtype), jax.ShapeDtypeStruct((S, heads * GLA_DK), F32)),
        in_specs=_gla_specs(heads, rev) + [
            pl.BlockSpec((1, cpb, GLA_DV, GLA_DK), lambda h, b: (h, rev(b), 0, 0)),
            pl.BlockSpec((GLA_ROWS, GLA_DV), lambda h, b: (rev(b), h))],
        out_specs=(pl.BlockSpec((GLA_ROWS, GLA_DK), lambda h, b: (rev(b), h)),
                   pl.BlockSpec((GLA_ROWS, GLA_DK), lambda h, b: (rev(b), h)),
                   pl.BlockSpec((GLA_ROWS, GLA_DV), lambda h, b: (rev(b), h)),
                   pl.BlockSpec((GLA_ROWS, GLA_DK), lambda h, b: (rev(b), h))),
        scratch_shapes=[pltpu.VMEM((GLA_DV, GLA_DK), F32)],
        compiler_params=_cp(("parallel", "arbitrary")),
    )(hm, hm, hm, la, sprev, do)


@functools.partial(jax.custom_vjp, nondiff_argnums=(3,))
def gla_core(hm, la, o_norm, heads):
    return _gla_core_fwd(hm, la, o_norm, heads)[0]


def _gla_core_fwd(hm, la, o_norm, heads):
    o, sprev = _gla_fwd_call(hm, la, heads)
    vd = heads * GLA_DV
    rows = [(o, vd, 0), (hm, vd, 2 * heads * GLA_DK // vd + 1)]
    (y,) = _rw_fwd(_gla_out_fn(heads), rows, [o_norm], [BF16], "gla_out_fwd")
    return y, (hm, la, o_norm, o, sprev)


def _gla_core_bwd(heads, res, dy):
    hm, la, o_norm, o, sprev = res
    vd = heads * GLA_DV
    rows = [(o, vd, 0), (hm, vd, 2 * heads * GLA_DK // vd + 1)]
    (do, dr), (dg,) = _rw_bwd(_gla_out_fn(heads), rows, [o_norm], [dy], [F32, hm.dtype], "gla_out_bwd")
    dq, dk, dv, dla = _gla_bwd_call(hm, la, sprev, do, heads)
    return jnp.concatenate([dq, dk, dv, dr], axis=1), dla, dg


gla_core.defvjp(_gla_core_fwd, _gla_core_bwd)


CONV_COLS = 256
HALO = 16


def _conv_rows(S):
    return min(512, S)


def _conv_taps(main_ref, halo_ref, i):
    prev = jnp.where(i > 0, halo_ref[...].astype(F32), 0.0)
    full = jnp.concatenate([prev, main_ref[...].astype(F32)], axis=0)
    return full[HALO:], pltpu.roll(full, 1, 0)[HALO:], pltpu.roll(full, 2, 0)[HALO:]


def _conv_apply(taps, w_ref, b_ref):
    x0, x1, x2 = taps
    return x2 * w_ref[0:1, :] + x1 * w_ref[1:2, :] + x0 * w_ref[2:3, :] + b_ref[...]


def _gelu_gate(uc, gc):
    return uc * jax.nn.gelu(gc)


def _conv_in_specs(R, nj):
    hpr = R // HALO
    main = pl.BlockSpec((R, CONV_COLS), lambda j, i: (i, j))
    halo = pl.BlockSpec((HALO, CONV_COLS), lambda j, i: (jnp.maximum(i * hpr - 1, 0), j))
    par = lambda rows, off: pl.BlockSpec((rows, CONV_COLS), lambda j, i: (0, j + off))
    return [main, halo, main, halo, par(CONV_W, 0), par(CONV_W, nj), par(1, 0), par(1, nj)]


def _conv_fwd_call(hu, hg, cw, cb, comm=None):
    S, dff = hu.shape
    R, nj = _conv_rows(S), dff // CONV_COLS

    def body(u_ref, uh_ref, g_ref, gh_ref, wu_ref, wg_ref, bu_ref, bg_ref, a_ref):
        i = pl.program_id(1)
        uc = _conv_apply(_conv_taps(u_ref, uh_ref, i), wu_ref, bu_ref)
        gc = _conv_apply(_conv_taps(g_ref, gh_ref, i), wg_ref, bg_ref)
        a_ref[...] = _gelu_gate(uc, gc).astype(a_ref.dtype)

    return _pcall(
        body, "conv_fwd", (nj, S // R), 0, in_specs=_conv_in_specs(R, nj),
        out_specs=[pl.BlockSpec((R, CONV_COLS), lambda j, i: (i, j))], out_shape=[jax.ShapeDtypeStruct((S, dff), BF16)],
        scratch=[], sem=("parallel", "parallel"), args=[hu, hu, hg, hg, cw, cw, cb, cb], comm=comm)


def _conv_bwd_gate_call(hu, hg, cw, cb, da, comm=None):
    S, dff = hu.shape
    R, nj = _conv_rows(S), dff // CONV_COLS

    def body(u_ref, uh_ref, g_ref, gh_ref, wu_ref, wg_ref, bu_ref, bg_ref, da_ref,
             du_ref, dg_ref, dwu_ref, dwg_ref, dbu_ref, dbg_ref):
        i = pl.program_id(1)
        ut, gt = _conv_taps(u_ref, uh_ref, i), _conv_taps(g_ref, gh_ref, i)
        uc, gc = _conv_apply(ut, wu_ref, bu_ref), _conv_apply(gt, wg_ref, bg_ref)
        _, vjp_fn = jax.vjp(_gelu_gate, uc, gc)
        du, dg = vjp_fn(da_ref[...].astype(F32))
        du_ref[...] = du.astype(du_ref.dtype)
        dg_ref[...] = dg.astype(dg_ref.dtype)

        @pl.when(i == 0)
        def _():
            for r in (dwu_ref, dwg_ref, dbu_ref, dbg_ref):
                r[...] = jnp.zeros(r.shape, F32)

        for d, taps, dw_ref, db_ref in ((du, ut, dwu_ref, dbu_ref), (dg, gt, dwg_ref, dbg_ref)):
            x0, x1, x2 = taps
            dw_ref[0:1, :] += jnp.sum(d * x2, axis=0, keepdims=True)
            dw_ref[1:2, :] += jnp.sum(d * x1, axis=0, keepdims=True)
            dw_ref[2:3, :] += jnp.sum(d * x0, axis=0, keepdims=True)
            db_ref[...] += jnp.sum(d, axis=0, keepdims=True)

    tile = pl.BlockSpec((R, CONV_COLS), lambda j, i: (i, j))
    par = lambda rows: pl.BlockSpec((rows, CONV_COLS), lambda j, i: (0, j))
    return _pcall(
        body, "conv_bwd_gate", (nj, S // R), 0, in_specs=_conv_in_specs(R, nj) + [tile],
        out_specs=[tile, tile, par(CONV_W), par(CONV_W), par(1), par(1)],
        out_shape=[jax.ShapeDtypeStruct((S, dff), BF16), jax.ShapeDtypeStruct((S, dff), BF16),
                   jax.ShapeDtypeStruct((CONV_W, dff), F32), jax.ShapeDtypeStruct((CONV_W, dff), F32),
                   jax.ShapeDtypeStruct((1, dff), F32), jax.ShapeDtypeStruct((1, dff), F32)],
        scratch=[], sem=("parallel", "arbitrary"), args=[hu, hu, hg, hg, cw, cw, cb, cb, da], comm=comm)


def _conv_bwd_shift_call(dc, cw, col_off, out_dtype):
    S, dff = dc.shape
    R, nj = _conv_rows(S), dff // CONV_COLS
    hpr, last = R // HALO, S // HALO - 1
    ni = S // R

    def body(d_ref, nx_ref, w_ref, o_ref):
        i = pl.program_id(1)
        nxt = jnp.where(i < ni - 1, nx_ref[...].astype(F32), 0.0)
        full = jnp.concatenate([d_ref[...].astype(F32), nxt], axis=0)
        n = R + HALO
        y1, y2 = pltpu.roll(full, n - 1, 0)[:R], pltpu.roll(full, n - 2, 0)[:R]
        o_ref[...] = (full[:R] * w_ref[2:3, :] + y1 * w_ref[1:2, :] + y2 * w_ref[0:1, :]).astype(o_ref.dtype)

    return pl.pallas_call(
        body, name=_nm("conv_bwd_shift"), grid=(nj, ni), out_shape=jax.ShapeDtypeStruct((S, dff), out_dtype),
        in_specs=[pl.BlockSpec((R, CONV_COLS), lambda j, i: (i, j)),
                  pl.BlockSpec((HALO, CONV_COLS), lambda j, i: (jnp.minimum((i + 1) * hpr, last), j)),
                  pl.BlockSpec((CONV_W, CONV_COLS), lambda j, i: (0, j + col_off))],
        out_specs=pl.BlockSpec((R, CONV_COLS), lambda j, i: (i, j)),
        compiler_params=_cp(("parallel", "parallel")),
    )(dc, dc, cw)


@jax.custom_vjp
def ffn_hidden(x1, w3, cw, cb):
    return _ffn_hidden_fwd(x1, w3, cw, cb)[0]


def _ffn_hidden_fwd(x1, w3, cw, cb, comm=None):
    half = w3.shape[0] // 2
    hu = _mm(x1, w3, "nn", BF16, "up_u", (0, half))
    hg = _mm(x1, w3, "nn", BF16, "up_g", (half, half))
    a, *carried = _conv_fwd_call(hu, hg, cw, cb, comm)
    return (a, (x1, w3, cw, cb, hu, hg)) + ((carried,) if comm is not None else ())


def _ffn_hidden_bwd(res, da, comm=None):
    x1, w3, cw, cb, hu, hg = res
    half = w3.shape[0] // 2
    nj = hu.shape[1] // CONV_COLS
    du, dg, dwu, dwg, dbu, dbg, *carried = _conv_bwd_gate_call(hu, hg, cw, cb, da, comm)
    dhu = _conv_bwd_shift_call(du, cw, 0, BF16)
    dhg = _conv_bwd_shift_call(dg, cw, nj, BF16)
    dx = _mm(dhu, w3, "nt", F32, "up_dx_u", (0, half)) + _mm(dhg, w3, "nt", F32, "up_dx_g", (half, half))
    dw3 = jnp.concatenate([_mm(x1, dhu, "tn", w3.dtype, "up_dw_u", (0, half)),
                           _mm(x1, dhg, "tn", w3.dtype, "up_dw_g", (half, half))], axis=0)
    grads = (dx, dw3, jnp.concatenate([dwu, dwg], axis=1), jnp.concatenate([dbu, dbg], axis=1))
    return (grads, carried) if comm is not None else grads


ffn_hidden.defvjp(_ffn_hidden_fwd, _ffn_hidden_bwd)


def _loss_call(y, target):
    S, D = y.shape
    tr = _row_tile(S, D)

    def body(y_ref, t_ref, sq_ref, dy_ref):
        diff = y_ref[...] - t_ref[...]
        dy_ref[...] = diff * (1.0 / D)
        part = jnp.sum(diff * diff, axis=0, keepdims=True)
        i = pl.program_id(0)

        @pl.when(i == 0)
        def _():
            sq_ref[...] = part

        @pl.when(i > 0)
        def _():
            sq_ref[...] += part

    row = pl.BlockSpec((tr, D), lambda i: (i, 0))
    return pl.pallas_call(
        body, name=_nm("loss"), grid=(S // tr,),
        out_shape=(jax.ShapeDtypeStruct((1, D), F32), jax.ShapeDtypeStruct((S, D), F32)),
        in_specs=[row, row], out_specs=(pl.BlockSpec((1, D), lambda i: (0, 0)), row),
        compiler_params=_cp(("arbitrary",)),
    )(y, target)


def _row_divisor(rows):
    for cand in range(min(rows, 512), 15, -1):
        if rows % cand == 0 and cand % 16 == 0:
            return cand
    return rows


def _adamw_call(w, g, m, v):
    shape = w.shape
    w2, g2, m2, v2 = (a.reshape(-1, shape[-1]) for a in (w, g, m, v))
    rows, cols = w2.shape
    tr = _row_divisor(rows)

    def body(w_ref, g_ref, m_ref, v_ref, d_ref, nm_ref, nv_ref):
        g_ = g_ref[...]
        m_ = ADAM_B1 * m_ref[...] + (1.0 - ADAM_B1) * g_
        v_ = ADAM_B2 * v_ref[...] + (1.0 - ADAM_B2) * (g_ * g_)
        m_hat = m_ / (1.0 - ADAM_B1 ** ADAM_STEP)
        v_hat = v_ / (1.0 - ADAM_B2 ** ADAM_STEP)
        d_ref[...] = -ADAM_LR * (m_hat / (jnp.sqrt(v_hat) + ADAM_EPS) + ADAM_WD * w_ref[...])
        nm_ref[...] = m_
        nv_ref[...] = v_

    blk = pl.BlockSpec((tr, cols), lambda i: (i, 0))
    outs = pl.pallas_call(
        body, name=_nm("adamw"), grid=(rows // tr,),
        out_shape=tuple(jax.ShapeDtypeStruct((rows, cols), F32) for _ in range(3)),
        in_specs=[blk] * 4, out_specs=(blk,) * 3, compiler_params=_cp(("parallel",)),
    )(w2, g2, m2, v2)
    return tuple(o.reshape(shape) for o in outs)


ANY = pl.BlockSpec(memory_space=pl.ANY)


def _place():
    return lax.axis_index("x"), lax.axis_index("y"), lax.axis_index("c")


def all_gather(shards):
    n = len(shards)

    def body(*refs):
        x_refs, out_refs = refs[:n], refs[n:2 * n]
        send_sems, recv_sems, local_sems = refs[2 * n:]
        x, y, c = _place()
        me, sibling = (x, y, c), (x, y, 1 - c)
        chips = [(1 - x, y), (x, 1 - y), (1 - x, 1 - y)]

        def slot(a, px, py, pc):
            return out_refs[a].at[:, 4 * px + 2 * py + pc]

        def copy(a, k, block, to, own=False):
            return pltpu.make_async_remote_copy(
                src_ref=x_refs[a] if own else slot(a, *block), dst_ref=slot(a, *block),
                send_sem=send_sems.at[7 * a + k], recv_sem=recv_sems.at[7 * a + k], device_id=to, device_id_type=MESH)

        mine = [pltpu.make_async_copy(x_refs[a], slot(a, *me), local_sems.at[a]) for a in range(n)]
        first = []
        for a in range(n):
            mine[a].start()
            first.append(copy(a, 0, me, sibling, own=True))
            first += [copy(a, 1 + j, me, (*chip, c), own=True) for j, chip in enumerate(chips)]
        for cp in first:
            cp.start()
        passed = []
        for j, chip in enumerate(chips):
            for a in range(n):
                copy(a, 1 + j, (*chip, c), me).wait_recv()
                passed.append(copy(a, 4 + j, (*chip, c), sibling))
                passed[-1].start()
        for a in range(n):
            copy(a, 0, sibling, me).wait_recv()
        for j, chip in enumerate(chips):
            for a in range(n):
                copy(a, 4 + j, (*chip, 1 - c), me).wait_recv()
        for cp in first + passed:
            cp.wait_send()
        for cp in mine:
            cp.wait()

    return pl.pallas_call(
        body, name=_nm("all_gather"),
        out_shape=tuple(jax.ShapeDtypeStruct((s.shape[0], N_DEV) + s.shape[1:], s.dtype) for s in shards),
        in_specs=[ANY] * n, out_specs=(ANY,) * n,
        scratch_shapes=[pltpu.SemaphoreType.DMA((7 * n,)), pltpu.SemaphoreType.DMA((7 * n,)), pltpu.SemaphoreType.DMA((n,))],
    )(*shards)


def _rs_pair_exchange(gs):
    n = len(gs)

    def body(*refs):
        g_refs, recv_refs = refs[:n], refs[n:2 * n]
        send_sems, recv_sems = refs[2 * n:]
        x, y, c = _place()
        copies = [pltpu.make_async_remote_copy(
            src_ref=g_refs[a].at[:, 2 * j + (1 - c)], dst_ref=recv_refs[a].at[j], send_sem=send_sems.at[4 * a + j],
            recv_sem=recv_sems.at[4 * a + j], device_id=(x, y, 1 - c), device_id_type=MESH)
            for a in range(n) for j in range(4)]
        for cp in copies:
            cp.start()
        for cp in copies:
            cp.wait_recv()
        for cp in copies:
            cp.wait_send()

    return pl.pallas_call(
        body, name=_nm("rs_pair"),
        out_shape=tuple(jax.ShapeDtypeStruct((4, g.shape[0]) + g.shape[2:], g.dtype) for g in gs),
        in_specs=[ANY] * n, out_specs=(ANY,) * n,
        scratch_shapes=[pltpu.SemaphoreType.DMA((4 * n,)), pltpu.SemaphoreType.DMA((4 * n,))],
    )(*gs)


def _rs_chip_exchange(ps):
    n = len(ps)

    def body(*refs):
        p_refs, recv_refs = refs[:n], refs[n:2 * n]
        send_sems, recv_sems = refs[2 * n:]
        x, y, c = _place()
        chips = [(1 - x, y), (x, 1 - y), (1 - x, 1 - y)]
        copies = [pltpu.make_async_remote_copy(
            src_ref=p_refs[a].at[2 * cx + cy], dst_ref=recv_refs[a].at[k], send_sem=send_sems.at[3 * a + k],
            recv_sem=recv_sems.at[3 * a + k], device_id=(cx, cy, c), device_id_type=MESH)
            for a in range(n) for k, (cx, cy) in enumerate(chips)]
        for cp in copies:
            cp.start()
        for cp in copies:
            cp.wait_recv()
        for cp in copies:
            cp.wait_send()

    return pl.pallas_call(
        body, name=_nm("rs_chip"),
        out_shape=tuple(jax.ShapeDtypeStruct((3,) + p.shape[1:], p.dtype) for p in ps),
        in_specs=[ANY] * n, out_specs=(ANY,) * n,
        scratch_shapes=[pltpu.SemaphoreType.DMA((3 * n,)), pltpu.SemaphoreType.DMA((3 * n,))],
    )(*ps)


def _rs_pair_add(g, recv, c_idx):
    L, _, a, b = g.shape
    ta = _row_divisor(a)

    def body(c_ref, g_ref, r_ref, o_ref):
        o_ref[...] = (g_ref[...].astype(F32) + r_ref[...].astype(F32)).astype(o_ref.dtype)

    grid_spec = pltpu.PrefetchScalarGridSpec(
        num_scalar_prefetch=1, grid=(4, L, a // ta),
        in_specs=[pl.BlockSpec((None, None, ta, b), lambda j, l, i, c_ref: (l, 2 * j + c_ref[0], i, 0)),
                  pl.BlockSpec((None, None, ta, b), lambda j, l, i, c_ref: (j, l, i, 0))],
        out_specs=pl.BlockSpec((None, None, ta, b), lambda j, l, i, c_ref: (j, l, i, 0)))
    return pl.pallas_call(
        body, name=_nm("rs_pair_add"), grid_spec=grid_spec, out_shape=jax.ShapeDtypeStruct((4, L, a, b), g.dtype),
        compiler_params=_cp(("parallel", "parallel", "parallel")),
    )(c_idx, g, recv)


def _rs_final_add(p1, recv, chip_idx):
    _, L, a, b = p1.shape
    ta = _row_divisor(a)

    def body(chip_ref, p_ref, r_ref, o_ref):
        acc = p_ref[...].astype(F32)
        for k in range(3):
            acc = acc + r_ref[k].astype(F32)
        o_ref[...] = acc

    grid_spec = pltpu.PrefetchScalarGridSpec(
        num_scalar_prefetch=1, grid=(L, a // ta),
        in_specs=[pl.BlockSpec((None, None, ta, b), lambda l, i, chip_ref: (chip_ref[0], l, i, 0)),
                  pl.BlockSpec((3, None, ta, b), lambda l, i, chip_ref: (0, l, i, 0))],
        out_specs=pl.BlockSpec((None, ta, b), lambda l, i, chip_ref: (l, i, 0)))
    return pl.pallas_call(
        body, name=_nm("rs_final_add"), grid_spec=grid_spec, out_shape=jax.ShapeDtypeStruct((L, a, b), F32),
        compiler_params=_cp(("parallel", "parallel")),
    )(chip_idx, p1, recv)


def reduce_scatter(gs):
    x, y, c = _place()
    c_idx = jnp.reshape(c, (1,)).astype(jnp.int32)
    chip_idx = jnp.reshape(2 * x + y, (1,)).astype(jnp.int32)
    recv1 = _rs_pair_exchange(gs)
    p1 = [_rs_pair_add(g, r, c_idx) for g, r in zip(gs, recv1)]
    recv2 = _rs_chip_exchange(p1)
    return [_rs_final_add(p, r, chip_idx) for p, r in zip(p1, recv2)]


def all_reduce_small(v):
    r, C = v.shape

    def body(v_ref, out_ref, buf_ref, send_sems, recv_sems):
        x, y, c = _place()
        my_id = 4 * x + 2 * y + c
        buf_ref[my_id] = v_ref[...]
        copies = []
        for k in range(1, N_DEV):
            fx, fy, fc = (k >> 2) & 1, (k >> 1) & 1, k & 1
            peer = (x ^ fx, y ^ fy, c ^ fc)
            copies.append(pltpu.make_async_remote_copy(
                src_ref=v_ref, dst_ref=buf_ref.at[my_id], send_sem=send_sems.at[k - 1], recv_sem=recv_sems.at[k - 1],
                device_id=peer, device_id_type=MESH))
        for cp in copies:
            cp.start()
        for cp in copies:
            cp.wait_recv()
        for cp in copies:
            cp.wait_send()
        acc = buf_ref[0]
        for d in range(1, N_DEV):
            acc = acc + buf_ref[d]
        out_ref[...] = acc

    vm = pl.BlockSpec(memory_space=pltpu.VMEM)
    return pl.pallas_call(
        body, name=_nm("all_reduce_small"), out_shape=jax.ShapeDtypeStruct((r, C), F32),
        in_specs=[vm], out_specs=vm,
        scratch_shapes=[pltpu.VMEM((N_DEV, r, C), F32), pltpu.SemaphoreType.DMA((7,)), pltpu.SemaphoreType.DMA((7,))],
    )(v)


def _slot(ref, px, py, pc):
    return ref.at[:, 4 * px + 2 * py + pc]


def comm_gather_own(shards):
    n = len(shards)

    def build(cin, cout, send_sems, recv_sems, local_sems):
        x, y, c = _place()
        me = (x, y, c)
        peers = [(x, y, 1 - c), (1 - x, y, c), (x, 1 - y, c), (1 - x, 1 - y, c)]
        starts, waits = [], []
        for a in range(n):
            local = pltpu.make_async_copy(cin[a], _slot(cout[a], *me), local_sems.at[a])
            starts.append(local)
            waits.append(local.wait)
            for k, peer in enumerate(peers):
                send = pltpu.make_async_remote_copy(
                    src_ref=cin[a], dst_ref=_slot(cout[a], *me), send_sem=send_sems.at[4 * a + k],
                    recv_sem=recv_sems.at[4 * a + k], device_id=peer, device_id_type=MESH)
                arrive = pltpu.make_async_remote_copy(
                    src_ref=cin[a], dst_ref=_slot(cout[a], *peer), send_sem=send_sems.at[4 * a + k],
                    recv_sem=recv_sems.at[4 * a + k], device_id=peer, device_id_type=MESH)
                starts.append(send)
                waits += [arrive.wait_recv, send.wait_send]
        return starts, waits

    out_shapes = [jax.ShapeDtypeStruct((s.shape[0], N_DEV) + s.shape[1:], s.dtype) for s in shards]
    return Comm(shards, out_shapes, {}, 4 * n, 4 * n, n, build)


def comm_gather_pass(partial):
    n = len(partial)

    def build(cin, cout, send_sems, recv_sems, local_sems):
        x, y, c = _place()
        chips = [(1 - x, y), (x, 1 - y), (1 - x, 1 - y)]
        starts, waits = [], []
        for a in range(n):
            for j, chip in enumerate(chips):
                send = pltpu.make_async_remote_copy(
                    src_ref=_slot(cout[a], *chip, c), dst_ref=_slot(cout[a], *chip, c), send_sem=send_sems.at[3 * a + j],
                    recv_sem=recv_sems.at[3 * a + j], device_id=(x, y, 1 - c), device_id_type=MESH)
                arrive = pltpu.make_async_remote_copy(
                    src_ref=_slot(cout[a], *chip, c), dst_ref=_slot(cout[a], *chip, 1 - c),
                    send_sem=send_sems.at[3 * a + j], recv_sem=recv_sems.at[3 * a + j],
                    device_id=(x, y, 1 - c), device_id_type=MESH)
                starts.append(send)
                waits += [arrive.wait_recv, send.wait_send]
        return starts, waits

    out_shapes = [jax.ShapeDtypeStruct(p.shape, p.dtype) for p in partial]
    return Comm(partial, out_shapes, {a: a for a in range(n)}, 3 * n, 3 * n, 1, build)


def comm_rs_pair(gs):
    n = len(gs)

    def build(cin, cout, send_sems, recv_sems, local_sems):
        x, y, c = _place()
        starts, waits = [], []
        for a in range(n):
            for j in range(4):
                cp = pltpu.make_async_remote_copy(
                    src_ref=cin[a].at[:, 2 * j + (1 - c)], dst_ref=cout[a].at[j], send_sem=send_sems.at[4 * a + j],
                    recv_sem=recv_sems.at[4 * a + j], device_id=(x, y, 1 - c), device_id_type=MESH)
                starts.append(cp)
                waits += [cp.wait_recv, cp.wait_send]
        return starts, waits

    out_shapes = [jax.ShapeDtypeStruct((4, g.shape[0]) + g.shape[2:], g.dtype) for g in gs]
    return Comm(gs, out_shapes, {}, 4 * n, 4 * n, 1, build)


def comm_rs_chip(ps):
    n = len(ps)

    def build(cin, cout, send_sems, recv_sems, local_sems):
        x, y, c = _place()
        chips = [(1 - x, y), (x, 1 - y), (1 - x, 1 - y)]
        starts, waits = [], []
        for a in range(n):
            for k, (cx, cy) in enumerate(chips):
                cp = pltpu.make_async_remote_copy(
                    src_ref=cin[a].at[2 * cx + cy], dst_ref=cout[a].at[k], send_sem=send_sems.at[3 * a + k],
                    recv_sem=recv_sems.at[3 * a + k], device_id=(cx, cy, c), device_id_type=MESH)
                starts.append(cp)
                waits += [cp.wait_recv, cp.wait_send]
        return starts, waits

    out_shapes = [jax.ShapeDtypeStruct((3,) + p.shape[1:], p.dtype) for p in ps]
    return Comm(ps, out_shapes, {}, 3 * n, 3 * n, 1, build)


def _pack(arrays, dtype, row_align):
    lead = arrays[0].shape[:-1]
    quantum = row_align * PACK_COLS
    parts, sizes = [], []
    for a in arrays:
        n = a.shape[-1]
        padded = _round_up(n, quantum)
        a = a.astype(dtype)
        if padded != n:
            a = jnp.pad(a, [(0, 0)] * len(lead) + [(0, padded - n)])
        parts.append(a.reshape(*lead, padded // PACK_COLS, PACK_COLS))
        sizes.append((n, padded // PACK_COLS))
    return jnp.concatenate(parts, axis=len(lead)), sizes


def _unpack(packed, sizes):
    lead = packed.shape[:-2]
    out, row = [], 0
    for n, rows in sizes:
        part = lax.slice_in_dim(packed, row, row + rows, axis=len(lead))
        out.append(part.reshape(*lead, rows * PACK_COLS)[..., :n])
        row += rows
    return out


def _unshard(gathered, axis):
    _, L, a, b = gathered.shape
    if axis == 1:
        return [gathered[:, l].reshape(N_DEV * a, b) for l in range(L)]
    return [jnp.transpose(gathered[:, l], (1, 0, 2)).reshape(a, N_DEV * b) for l in range(L)]


def _reshard(fulls, axis):
    blocks = []
    for f in fulls:
        A, B = f.shape
        if axis == 1:
            blocks.append(f.reshape(N_DEV, A // N_DEV, B))
        else:
            blocks.append(jnp.transpose(f.reshape(A, N_DEV, B // N_DEV), (1, 0, 2)))
    return jnp.stack(blocks, axis=1)


def _as3(a):
    return a if a.ndim == 3 else a[:, None, :]


def _prep_big(name, w, dims):
    w = w.astype(BF16)
    L, a, b = w.shape
    if name == 'mla_w_in':
        return jnp.pad(w, ((0, 0), (0, 0), (0, dims['h_width'] - b)))
    if name == 'mla_w_uq':
        hd = MLA_NOPE + MLA_ROPE
        w = jnp.pad(w.reshape(L, a, b // hd, hd), ((0, 0), (0, 0), (0, 0), (0, MLA_QK_PAD - hd)))
        return w.reshape(L, a, b // hd * MLA_QK_PAD)
    if name == 'ffn_w_up':
        return jnp.pad(w, ((0, 0), (0, 0), (0, _round_up(b, CONV_COLS) - b)))
    return w


def _unprep_big(name, g, shape):
    L, a, b = shape
    if name == 'mla_w_uq':
        hd = MLA_NOPE + MLA_ROPE
        return g.reshape(L, a, b // hd, MLA_QK_PAD)[..., :hd].reshape(L, a, b)
    return g[:, :, :b]


def _rope_tables(positions):
    inv = 1.0 / (ROPE_THETA ** (jnp.arange(0, MLA_ROPE, 2, dtype=F32) / MLA_ROPE))
    ang = positions.astype(F32)[:, None] * inv
    cos, sin = jnp.cos(ang), jnp.sin(ang)
    one, zero = jnp.ones_like(cos), jnp.zeros_like(cos)
    a = jnp.concatenate([cos, cos, one, one], axis=1)
    up = jnp.concatenate([sin, zero, zero, zero], axis=1)
    down = jnp.concatenate([zero, sin, zero, zero], axis=1)
    return (a, -up, down), (a, up, -down)


def _rows_full(w):
    return w.reshape(w.shape[0] * w.shape[1], w.shape[2])


def _ops(depth):
    alpha = (2 * depth) ** 0.25
    return {'ln_res': rw_op(_ln_res_fn(alpha), "ln_res", 2, [F32]), 'ple': rw_op(_ple_fn, "ple", 3, [F32]),
            'gla_gate': rw_op(_gla_gate_fn, "gla_gate", 1, [F32])}


MLA_PRE_W = ['mla_w_in', 'mla_q_norm', 'mla_kv_norm', 'mla_w_uq', 'mla_w_uk', 'mla_w_uv']
FFN_IN_W = ['mla_w_o', 'ln1_g', 'ln1_b', 'ffn_conv_w', 'ffn_conv_b']
FFN_OUT_W = ['ffn_w_down', 'ln2_g', 'ln2_b', 'ple_w_gate', 'ple_w_proj', 'ple_b_gate']


def _mla_heads(wl, j):
    return N_DEV * wl['mla_w_uk'][j].shape[2] // MLA_NOPE


def _mla_pre(x, wl, j):
    w_uq, w_uk, w_uv = wl['mla_w_uq'][j], wl['mla_w_uk'][j], wl['mla_w_uv'][j]
    h = linear(x, _rows_full(wl['mla_w_in'][j]), F32)
    mla_norm = rw_op(_mla_norm_fn(w_uq.shape[1], w_uk.shape[1]), "mla_norm", 1, [BF16, BF16, F32])
    cq, ckv, kr_raw = mla_norm(h, wl['mla_q_norm'][j], wl['mla_kv_norm'][j])
    return linear(cq, w_uq, BF16), linear(ckv, w_uk, BF16), linear(ckv, w_uv, BF16), kr_raw


def _gla_mixer(x, wl, j, ops):
    w_in3, w_a2 = wl['gla_w_in'][j], wl['gla_w_a2'][j]
    w_o = _rows_full(wl['gla_w_o'][j])
    w_in = jnp.transpose(w_in3, (1, 0, 2)).reshape(w_in3.shape[1], N_DEV * w_in3.shape[2])
    heads = w_o.shape[0] // GLA_DV
    n_main = 2 * heads * GLA_DK + 2 * heads * GLA_DV
    w_a = jnp.pad(w_in[:, n_main:], ((0, 0), (0, LANES - GLA_RANK)))
    w_a2_p = jnp.pad(w_a2, ((0, LANES - GLA_RANK), (0, 0))).astype(BF16)
    hm = linear(x, w_in[:, :n_main], BF16)
    ha = linear(x, w_a, BF16)
    (la,) = ops['gla_gate'](linear(ha, w_a2_p, F32), wl['gla_b_a'][j])
    return linear(gla_core(hm, la, wl['gla_o_norm'][j], heads), w_o, F32)


def _ffn_in(x, m, wl, i, ops, bp):
    (x1,) = ops['ln_res'](x, m, wl['ln1_g'][i], wl['ln1_b'][i])
    cw, cb = wl['ffn_conv_w'][i], wl['ffn_conv_b'][i]
    bu = cw.shape[1] // N_DEV
    cwp = jnp.pad(cw.reshape(CONV_W, N_DEV, bu), ((0, 0), (0, 0), (0, bp - bu))).reshape(CONV_W, N_DEV * bp)
    cbp = jnp.pad(cb.reshape(1, N_DEV, bu), ((0, 0), (0, 0), (0, bp - bu))).reshape(1, N_DEV * bp)
    return x1, cwp, cbp


def _ffn_out(x1, a, wl, p_i, i, ops):
    w_down3 = wl['ffn_w_down'][i]
    half, bu, d_model = N_DEV // 2, 2 * w_down3.shape[1], w_down3.shape[2]
    bp = a.shape[1] // half
    w_down = jnp.pad(w_down3.reshape(half, bu, d_model), ((0, 0), (0, bp - bu), (0, 0))).reshape(half * bp, d_model)
    f = linear(a, w_down, F32)
    (x2,) = ops['ln_res'](x1, f, wl['ln2_g'][i], wl['ln2_b'][i])
    glog = linear(x2, _rows_full(wl['ple_w_gate'][i]), F32)
    pp = linear(p_i, wl['ple_w_proj'][i], F32)
    (x,) = ops['ple'](x2, glog, pp, wl['ple_b_gate'][i])
    return x


def _layer(x, wl, p_i, aux, i, ops):
    j = i // 2
    if i % 2 == 0:
        q_raw, kn, v, kr_raw = _mla_pre(x, wl, j)
        o = make_attention(aux, _mla_heads(wl, j))[0](q_raw, kn, v, kr_raw)
        m = linear(o, _rows_full(wl['mla_w_o'][j]), F32)
    else:
        m = _gla_mixer(x, wl, j, ops)
    x1, cwp, cbp = _ffn_in(x, m, wl, i, ops, wl['ffn_w_up'][i].shape[2])
    return _ffn_out(x1, ffn_hidden(x1, wl['ffn_w_up'][i], cwp, cbp), wl, p_i, i, ops)


def kernel(x, p, positions, mla_w_in, mla_q_norm, mla_kv_norm, mla_w_uq, mla_w_uk, mla_w_uv, mla_w_o, gla_w_in, gla_w_a2, gla_b_a, gla_o_norm, gla_w_o, ln1_g, ln1_b, ln2_g, ln2_b, ffn_w_up, ffn_conv_w, ffn_conv_b, ffn_w_down, ple_w_proj, ple_w_gate, ple_b_gate, loss_target, m_mla_w_in, m_mla_q_norm, m_mla_kv_norm, m_mla_w_uq, m_mla_w_uk, m_mla_w_uv, m_mla_w_o, m_gla_w_in, m_gla_w_a2, m_gla_b_a, m_gla_o_norm, m_gla_w_o, m_ln1_g, m_ln1_b, m_ln2_g, m_ln2_b, m_ffn_w_up, m_ffn_conv_w, m_ffn_conv_b, m_ffn_w_down, m_ple_w_proj, m_ple_w_gate, m_ple_b_gate, v_mla_w_in, v_mla_q_norm, v_mla_kv_norm, v_mla_w_uq, v_mla_w_uk, v_mla_w_uv, v_mla_w_o, v_gla_w_in, v_gla_w_a2, v_gla_b_a, v_gla_o_norm, v_gla_w_o, v_ln1_g, v_ln1_b, v_ln2_g, v_ln2_b, v_ffn_w_up, v_ffn_conv_w, v_ffn_conv_b, v_ffn_w_down, v_ple_w_proj, v_ple_w_gate, v_ple_b_gate):
    w = dict(zip(WEIGHTS, (mla_w_in, mla_q_norm, mla_kv_norm, mla_w_uq, mla_w_uk, mla_w_uv, mla_w_o, gla_w_in, gla_w_a2,
                           gla_b_a, gla_o_norm, gla_w_o, ln1_g, ln1_b, ln2_g, ln2_b, ffn_w_up, ffn_conv_w, ffn_conv_b,
                           ffn_w_down, ple_w_proj, ple_w_gate, ple_b_gate)))
    m_in = dict(zip(WEIGHTS, (m_mla_w_in, m_mla_q_norm, m_mla_kv_norm, m_mla_w_uq, m_mla_w_uk, m_mla_w_uv, m_mla_w_o,
                              m_gla_w_in, m_gla_w_a2, m_gla_b_a, m_gla_o_norm, m_gla_w_o, m_ln1_g, m_ln1_b, m_ln2_g,
                              m_ln2_b, m_ffn_w_up, m_ffn_conv_w, m_ffn_conv_b, m_ffn_w_down, m_ple_w_proj, m_ple_w_gate,
                              m_ple_b_gate)))
    v_in = dict(zip(WEIGHTS, (v_mla_w_in, v_mla_q_norm, v_mla_kv_norm, v_mla_w_uq, v_mla_w_uk, v_mla_w_uv, v_mla_w_o,
                              v_gla_w_in, v_gla_w_a2, v_gla_b_a, v_gla_o_norm, v_gla_w_o, v_ln1_g, v_ln1_b, v_ln2_g,
                              v_ln2_b, v_ffn_w_up, v_ffn_conv_w, v_ffn_conv_b, v_ffn_w_down, v_ple_w_proj, v_ple_w_gate,
                              v_ple_b_gate)))
    _uid[0] = itertools.count()
    x2d, target, pos = x[0], loss_target[0], positions[0]
    p3 = p[:, 0]
    dims = {'h_width': mla_w_uq.shape[1] + mla_w_uk.shape[1] + LANES}

    depth = ln1_g.shape[0]
    ops = _ops(depth)
    cid = pos // CHUNK
    aux = {'rope': _rope_tables(pos), 'cidq': cid[:, None], 'cidk': cid[None, :],
           'need': _mask_table(cid, _attn_tile(pos.shape[0]))}

    in_layer0 = set(MLA_PRE_W + FFN_IN_W + FFN_OUT_W + ['ffn_w_up'])
    prepped = {n: _prep_big(n, w[n], dims) for n in BIG}
    first_names = [n for n in BIG if n in in_layer0]
    rest_names = [n for n in BIG if prepped[n].shape[0] > (1 if n in in_layer0 else 0)]
    rest_from = {n: (1 if n in in_layer0 else 0) for n in rest_names}
    small3 = [_as3(w[n]) for n in SMALL]
    small_packed, small_sizes = _pack([s.reshape(1, -1) for s in small3], F32, 8)
    first = all_gather([prepped[n][:1] for n in first_names] + [small_packed])
    wl = {n: [None] * prepped[n].shape[0] for n in BIG}
    for n, g in zip(first_names, first):
        wl[n][0] = g[0]
    for n, s3, flat in zip(SMALL, small3, _unpack(first[-1][0], small_sizes)):
        wl[n] = _unshard(flat.reshape(N_DEV, *s3.shape), SHARD_AXIS[n] if w[n].ndim == 3 else 2)
    for n in REPL:
        wl[n] = [w[n][l][None, :] for l in range(w[n].shape[0])]

    def pick(names, layer0):
        return {n: [wl[n][l] if (l == 0 and n in in_layer0) == layer0 else None for l in range(len(wl[n]))] for n in names}

    heads0 = _mla_heads(wl, 0)
    _, attn_run, attn_bwd = make_attention(aux, heads0)
    pre, vjp_pre = jax.vjp(lambda x_, wl_: _mla_pre(x_, wl_, 0), x2d, pick(MLA_PRE_W, True))
    o, attn_res, partial = attn_run(*pre, comm_gather_own([prepped[n][rest_from[n]:] for n in rest_names]))
    w_up0 = wl['ffn_w_up'][0]

    def ffn_in(x_, o_, wl_):
        return _ffn_in(x_, linear(o_, _rows_full(wl_['mla_w_o'][0]), F32), wl_, 0, ops, w_up0.shape[2])

    (x1, cwp, cbp), vjp_in = jax.vjp(ffn_in, x2d, o, pick(FFN_IN_W, True))
    a, ffn_res, rest = _ffn_hidden_fwd(x1, w_up0, cwp, cbp, comm_gather_pass(partial))
    for n, g in zip(rest_names, rest):
        for l in range(g.shape[0]):
            wl[n][rest_from[n] + l] = g[l]
    x_l0, vjp_out = jax.vjp(lambda x1_, a_, wl_: _ffn_out(x1_, a_, wl_, p3[0], 0, ops), x1, a, pick(FFN_OUT_W, True))

    def tail(x_, wl_):
        for i in range(1, depth):
            x_ = _layer(x_, wl_, p3[i], aux, i, ops)
        return x_

    y, vjp_tail = jax.vjp(tail, x_l0, pick(WEIGHTS, False))
    sq, dy = _loss_call(y, target)

    dwl = {n: [None] * len(wl[n]) for n in WEIGHTS}

    def keep(part):
        for n, per_layer in part.items():
            for l, g in enumerate(per_layer):
                if g is not None:
                    dwl[n][l] = g

    x_c, y_c, c_place = _place()
    c_idx = jnp.reshape(c_place, (1,)).astype(jnp.int32)
    chip_idx = jnp.reshape(2 * x_c + y_c, (1,)).astype(jnp.int32)
    dx_l0, d_tail = vjp_tail(dy)
    keep(d_tail)
    g_rest = [jnp.stack(dwl[n][rest_from[n]:], axis=0) for n in rest_names]
    dx1_out, da, d_out = vjp_out(dx_l0)
    keep(d_out)
    (dx1_ffn, dw_up0, dcwp, dcbp), recv1 = _ffn_hidden_bwd(ffn_res, da, comm_rs_pair(g_rest))
    dwl['ffn_w_up'][0] = dw_up0
    p1 = [_rs_pair_add(g, r, c_idx) for g, r in zip(g_rest, recv1)]
    dx_in, do, d_in = vjp_in((dx1_out + dx1_ffn, dcwp, dcbp))
    keep(d_in)
    d_pre_in, recv2 = attn_bwd(attn_res, do, comm_rs_chip(p1))
    red_rest = [_rs_final_add(p_, r, chip_idx) for p_, r in zip(p1, recv2)]
    dx_pre, d_pre = vjp_pre(d_pre_in)
    keep(d_pre)
    dx = dx_pre + dx_in

    small_blocks = [_reshard(dwl[n], SHARD_AXIS[n] if w[n].ndim == 3 else 2).reshape(N_DEV, -1) for n in SMALL]
    small_grad_packed, _ = _pack(small_blocks, F32, 8)
    red_first = reduce_scatter([dwl[n][0][None] for n in first_names] + [small_grad_packed[None]])
    by_layer = {n: [] for n in BIG}
    for n, g in zip(first_names, red_first):
        by_layer[n].append(g)
    for n, g in zip(rest_names, red_rest):
        by_layer[n].append(g)
    grads = {n: _unprep_big(n, jnp.concatenate(by_layer[n], axis=0), w[n].shape) for n in BIG}
    for n, f in zip(SMALL, _unpack(red_first[-1][0], small_sizes)):
        grads[n] = f.reshape(w[n].shape)

    repl_flat = [jnp.concatenate([g.reshape(-1) for g in dwl[n]]).reshape(1, -1) for n in REPL]
    loss_part = 0.5 * jnp.sum(sq) / sq.shape[1]
    packed, repl_sizes = _pack(repl_flat + [loss_part.reshape(1, 1)], F32, 8)
    summed = _unpack(all_reduce_small(packed[0])[None], repl_sizes)
    for n, f in zip(REPL, summed[:-1]):
        grads[n] = f.reshape(w[n].shape)
    loss = summed[-1].reshape(())

    delta, new_m, new_v = {}, {}, {}
    for n in WEIGHTS:
        delta[n], new_m[n], new_v[n] = _adamw_call(w[n], grads[n], m_in[n], v_in[n])
    return (loss, dx[None], *[grads[n] for n in WEIGHTS], *[delta[n] for n in WEIGHTS],
            *[new_m[n] for n in WEIGHTS], *[new_v[n] for n in WEIGHTS])
```

```python
import functools
import itertools

import jax
import jax.numpy as jnp
from jax import lax
from jax.experimental import pallas as pl
from jax.experimental.pallas import tpu as pltpu

F32 = jnp.float32
BF16 = jnp.bfloat16
MESH = pl.DeviceIdType.MESH
N_DEV = 8

EPS = 1e-5
NEG_INF = -1e30
CHUNK = 64
Q_BLOCK = 128
MLA_NOPE = 128
MLA_ROPE = 64
MLA_V = 128
MLA_QK_PAD = 256
ROPE_THETA = 10000.0
GLA_DK = 128
GLA_DV = 256
GLA_RANK = 16
GLA_TAU = 16.0
CONV_W = 3
ADAM_LR = 0.001
ADAM_B1 = 0.9
ADAM_B2 = 0.999
ADAM_EPS = 1e-08
ADAM_WD = 0.01
ADAM_STEP = 10
LOG2E = 1.4426950408889634

LANES = 128
PACK_COLS = 1024
VMEM_LIMIT = 48 * 1024 * 1024
MM_VMEM_BUDGET = 30 * 1024 * 1024

WEIGHTS = ['mla_w_in', 'mla_q_norm', 'mla_kv_norm', 'mla_w_uq', 'mla_w_uk', 'mla_w_uv', 'mla_w_o', 'gla_w_in',
           'gla_w_a2', 'gla_b_a', 'gla_o_norm', 'gla_w_o', 'ln1_g', 'ln1_b', 'ln2_g', 'ln2_b', 'ffn_w_up',
           'ffn_conv_w', 'ffn_conv_b', 'ffn_w_down', 'ple_w_proj', 'ple_w_gate', 'ple_b_gate']
SHARD_AXIS = {'mla_w_in': 1, 'mla_q_norm': None, 'mla_kv_norm': None, 'mla_w_uq': 2, 'mla_w_uk': 2, 'mla_w_uv': 2,
              'mla_w_o': 1, 'gla_w_in': 2, 'gla_w_a2': 2, 'gla_b_a': 1, 'gla_o_norm': 1, 'gla_w_o': 1,
              'ln1_g': None, 'ln1_b': None, 'ln2_g': None, 'ln2_b': None, 'ffn_w_up': 2, 'ffn_conv_w': 2,
              'ffn_conv_b': None, 'ffn_w_down': 1, 'ple_w_proj': 2, 'ple_w_gate': 1, 'ple_b_gate': None}
BIG = ['mla_w_in', 'mla_w_uq', 'mla_w_uk', 'mla_w_uv', 'mla_w_o', 'gla_w_in', 'gla_w_o', 'ffn_w_up', 'ffn_w_down',
       'ple_w_proj', 'ple_w_gate']
SMALL = ['gla_w_a2', 'gla_b_a', 'gla_o_norm', 'ffn_conv_w']
REPL = [n for n in WEIGHTS if SHARD_AXIS[n] is None]

_uid = [itertools.count()]


def _nm(base):
    return f"{base}_{next(_uid[0])}"


def _cp(sem=None):
    return pltpu.CompilerParams(dimension_semantics=sem, vmem_limit_bytes=VMEM_LIMIT)


def _round_up(n, m):
    return -(-n // m) * m


class Comm:
    def __init__(self, inputs, out_shapes, aliases, n_send, n_recv, n_local, build):
        self.inputs, self.out_shapes, self.aliases = list(inputs), list(out_shapes), dict(aliases)
        self.n_send, self.n_recv, self.n_local, self.build = n_send, n_recv, n_local, build


def _pcall(body, name, grid, n_prefetch, in_specs, out_specs, out_shape, scratch, sem, args, comm=None):
    in_specs, out_specs, out_shape, scratch, args = list(in_specs), list(out_specs), list(out_shape), list(scratch), list(args)
    n_in, n_out, n_scr = len(in_specs), len(out_specs), len(scratch)
    aliases = {}
    kernel_body = body
    if comm is not None:
        ci, co = len(comm.inputs), len(comm.out_shapes)
        any_spec = pl.BlockSpec(memory_space=pl.ANY)
        in_specs += [any_spec] * ci
        out_specs += [any_spec] * co
        out_shape += comm.out_shapes
        scratch += [pltpu.SemaphoreType.DMA((comm.n_send,)), pltpu.SemaphoreType.DMA((comm.n_recv,)),
                    pltpu.SemaphoreType.DMA((comm.n_local,))]
        aliases = {n_prefetch + n_in + k: n_out + v for k, v in comm.aliases.items()}
        args += comm.inputs
        sem = ("arbitrary",) * len(grid)

        def kernel_body(*refs):
            pre, r = refs[:n_prefetch], refs[n_prefetch:]
            ins, cin = r[:n_in], r[n_in:n_in + ci]
            outs, cout = r[n_in + ci:n_in + ci + n_out], r[n_in + ci + n_out:n_in + ci + n_out + co]
            scr = r[n_in + ci + n_out + co:n_in + ci + n_out + co + n_scr]
            send_sems, recv_sems, local_sems = r[-3:]
            first = functools.reduce(lambda a, b: a & b, [pl.program_id(d) == 0 for d in range(len(grid))])
            last = functools.reduce(lambda a, b: a & b, [pl.program_id(d) == grid[d] - 1 for d in range(len(grid))])
            starts, waits = comm.build(cin, cout, send_sems, recv_sems, local_sems)

            @pl.when(first)
            def _():
                for cp in starts:
                    cp.start()

            body(*pre, *ins, *outs, *scr)

            @pl.when(last)
            def _():
                for wait in waits:
                    wait()

    grid_spec = pltpu.PrefetchScalarGridSpec(num_scalar_prefetch=n_prefetch, grid=grid, in_specs=in_specs,
                                             out_specs=out_specs, scratch_shapes=scratch)
    return pl.pallas_call(kernel_body, name=_nm(name), grid_spec=grid_spec, out_shape=tuple(out_shape),
                          input_output_aliases=aliases, compiler_params=_cp(sem))(*args)


def _divisor_tiles(n, cap):
    if n % LANES:
        return [n]
    out = [t for t in range(LANES, min(n, cap) + 1, LANES) if n % t == 0]
    return out or [n]


def _mm_tiles(M, N, K, abytes, bbytes, obytes, tn_fixed=None, tk_fixed=None):
    best = None
    for tm in _divisor_tiles(M, 1024):
        for tn in ([tn_fixed] if tn_fixed else _divisor_tiles(N, 1536)):
            for tk in ([tk_fixed] if tk_fixed else _divisor_tiles(K, 2048)):
                vmem = 2 * (tm * tk * abytes + tk * tn * bbytes + tm * tn * obytes) + tm * tn * 4
                if vmem > MM_VMEM_BUDGET:
                    continue
                key = (tm * tn * tk, tk)
                if best is None or key > best[0]:
                    best = (key, (tm, tn, tk))
    assert best is not None, (M, N, K)
    return best[1]


def _mm(a, b, mode, out_dtype, base="mm", blocks=None):
    blk0, nblk = blocks if blocks else (0, 1)
    tn_fixed = tk_fixed = None
    if mode == "nn":
        M, K = a.shape
        N = nblk * b.shape[2] if blocks else b.shape[1]
        tn_fixed = b.shape[2] if blocks else None
    elif mode == "nt":
        M, K = a.shape
        N = b.shape[1] if blocks else b.shape[0]
        tk_fixed = b.shape[2] if blocks else None
        assert not blocks or K == nblk * b.shape[2]
    else:
        (K, M), N = a.shape, b.shape[1]
        tn_fixed = N // nblk if blocks else None
    tm, tn, tk = _mm_tiles(M, N, K, a.dtype.itemsize, b.dtype.itemsize, jnp.dtype(out_dtype).itemsize, tn_fixed, tk_fixed)
    nk = K // tk
    out_shape = jax.ShapeDtypeStruct((M, N), out_dtype)
    out_spec = pl.BlockSpec((tm, tn), lambda i, j, k: (i, j))
    if mode == "nn":
        a_spec = pl.BlockSpec((tm, tk), lambda i, j, k: (i, k))
        b_spec = (pl.BlockSpec((None, tk, tn), lambda i, j, k: (blk0 + j, k, 0)) if blocks
                  else pl.BlockSpec((tk, tn), lambda i, j, k: (k, j)))
        dims = (((1,), (0,)), ((), ()))
    elif mode == "nt":
        a_spec = pl.BlockSpec((tm, tk), lambda i, j, k: (i, k))
        b_spec = (pl.BlockSpec((None, tn, tk), lambda i, j, k: (blk0 + k, j, 0)) if blocks
                  else pl.BlockSpec((tn, tk), lambda i, j, k: (j, k)))
        dims = (((1,), (1,)), ((), ()))
    else:
        a_spec = pl.BlockSpec((tk, tm), lambda i, j, k: (k, i))
        b_spec = pl.BlockSpec((tk, tn), lambda i, j, k: (k, j))
        dims = (((0,), (0,)), ((), ()))
        if blocks:
            out_shape = jax.ShapeDtypeStruct((nblk, M, tn), out_dtype)
            out_spec = pl.BlockSpec((None, tm, tn), lambda i, j, k: (j, i, 0))

    def body(a_ref, b_ref, o_ref, acc_ref):
        part = lax.dot_general(a_ref[...].astype(BF16), b_ref[...].astype(BF16), dims, preferred_element_type=F32)
        if nk == 1:
            o_ref[...] = part.astype(o_ref.dtype)
        else:
            k = pl.program_id(2)

            @pl.when(k == 0)
            def _():
                acc_ref[...] = part

            @pl.when(k > 0)
            def _():
                acc_ref[...] += part

            @pl.when(k == nk - 1)
            def _():
                o_ref[...] = acc_ref[...].astype(o_ref.dtype)

    return pl.pallas_call(
        body, name=_nm(base), grid=(M // tm, N // tn, nk), out_shape=out_shape,
        in_specs=[a_spec, b_spec], out_specs=out_spec,
        scratch_shapes=[pltpu.VMEM((tm, tn) if nk > 1 else (8, LANES), F32)],
        compiler_params=_cp(("parallel", "parallel", "arbitrary")),
    )(a, b)


def _all_blocks(w):
    return (0, w.shape[0]) if w.ndim == 3 else None


@functools.partial(jax.custom_vjp, nondiff_argnums=(2,))
def linear(a, w, out_dtype):
    return _mm(a, w, "nn", out_dtype, "lin_fwd", _all_blocks(w))


def _linear_fwd(a, w, out_dtype):
    return _mm(a, w, "nn", out_dtype, "lin_fwd", _all_blocks(w)), (a, w)


def _linear_bwd(out_dtype, res, dy):
    a, w = res
    return (_mm(dy, w, "nt", a.dtype, "lin_dx", _all_blocks(w)), _mm(a, dy, "tn", w.dtype, "lin_dw", _all_blocks(w)))


linear.defvjp(_linear_fwd, _linear_bwd)


def _row_tile(S, width):
    tr = 512 if width <= 1024 else 256
    return min(tr, S)


def _rw_fwd(f, rows, params, out_dtypes, base):
    S = rows[0][0].shape[0]
    tr = _row_tile(S, max(w for _, w, _ in rows))
    n_in = len(rows) + len(params)
    avals = [jax.ShapeDtypeStruct((tr, w), F32) for _, w, _ in rows] + [jax.ShapeDtypeStruct(p.shape, F32) for p in params]
    outs = jax.eval_shape(f, *avals)

    def body(*refs):
        vals = [r[...].astype(F32) for r in refs[:n_in]]
        for o_ref, r in zip(refs[n_in:], f(*vals)):
            o_ref[...] = r.astype(o_ref.dtype)

    in_specs = [pl.BlockSpec((tr, w), functools.partial(lambda i, cb: (i, cb), cb=cb)) for _, w, cb in rows]
    in_specs += [pl.BlockSpec(p.shape, lambda i: (0, 0)) for p in params]
    return pl.pallas_call(
        body, name=_nm(base), grid=(S // tr,),
        out_shape=tuple(jax.ShapeDtypeStruct((S, o.shape[1]), dt) for o, dt in zip(outs, out_dtypes)),
        in_specs=in_specs, out_specs=tuple(pl.BlockSpec((tr, o.shape[1]), lambda i: (i, 0)) for o in outs),
        compiler_params=_cp(("parallel",)),
    )(*[a for a, _, _ in rows], *params)


def _rw_bwd(f, rows, params, cts, row_grad_dtypes, base):
    S = rows[0][0].shape[0]
    tr = _row_tile(S, max(w for _, w, _ in rows))
    n_rows, n_par, n_ct = len(rows), len(params), len(cts)
    want = [k for k, dt in enumerate(row_grad_dtypes) if dt is not None]

    def body(*refs):
        in_refs = refs[:n_rows + n_par]
        ct_refs = refs[n_rows + n_par:n_rows + n_par + n_ct]
        out_refs = refs[n_rows + n_par + n_ct:]
        vals = [r[...].astype(F32) for r in in_refs]
        _, vjp_fn = jax.vjp(f, *vals)
        grads = vjp_fn(tuple(c[...].astype(F32) for c in ct_refs))
        for o_ref, k in zip(out_refs[:len(want)], want):
            o_ref[...] = grads[k].astype(o_ref.dtype)
        i = pl.program_id(0)
        for o_ref, g in zip(out_refs[len(want):], grads[n_rows:]):
            @pl.when(i == 0)
            def _(o_ref=o_ref, g=g):
                o_ref[...] = g

            @pl.when(i > 0)
            def _(o_ref=o_ref, g=g):
                o_ref[...] += g

    in_specs = [pl.BlockSpec((tr, w), functools.partial(lambda i, cb: (i, cb), cb=cb)) for _, w, cb in rows]
    in_specs += [pl.BlockSpec(p.shape, lambda i: (0, 0)) for p in params]
    in_specs += [pl.BlockSpec((tr, c.shape[1]), lambda i: (i, 0)) for c in cts]
    out_shape = [jax.ShapeDtypeStruct((S, rows[k][1]), row_grad_dtypes[k]) for k in want]
    out_specs = [pl.BlockSpec((tr, rows[k][1]), lambda i: (i, 0)) for k in want]
    out_shape += [jax.ShapeDtypeStruct(p.shape, F32) for p in params]
    out_specs += [pl.BlockSpec(p.shape, lambda i: (0, 0)) for p in params]
    res = pl.pallas_call(
        body, name=_nm(base), grid=(S // tr,), out_shape=tuple(out_shape),
        in_specs=in_specs, out_specs=tuple(out_specs), compiler_params=_cp(("arbitrary",)),
    )(*[a for a, _, _ in rows], *params, *cts)
    row_grads = [None] * n_rows
    for k, g in zip(want, res[:len(want)]):
        row_grads[k] = g
    return row_grads, list(res[len(want):])


def rw_op(f, base, n_rows, out_dtypes):
    @jax.custom_vjp
    def op(*args):
        return fwd(*args)[0]

    def split(args):
        rows = [(a, a.shape[1], 0) for a in args[:n_rows]]
        return rows, list(args[n_rows:])

    def fwd(*args):
        rows, params = split(args)
        return tuple(_rw_fwd(f, rows, params, out_dtypes, base + "_fwd")), args

    def bwd(args, cts):
        rows, params = split(args)
        rg, pg = _rw_bwd(f, rows, params, list(cts), [a.dtype for a, _, _ in rows], base + "_bwd")
        return tuple(rg) + tuple(g.astype(p.dtype) for g, p in zip(pg, params))

    op.defvjp(fwd, bwd)
    return op


def _ln_res_fn(alpha):
    def f(x, m, g, b):
        z = alpha * x + m
        mu = jnp.mean(z, -1, keepdims=True)
        zc = z - mu
        var = jnp.mean(zc * zc, -1, keepdims=True)
        return (zc * lax.rsqrt(var + EPS) * g + b,)
    return f


def _rms(x, g):
    return x * lax.rsqrt(jnp.mean(x * x, -1, keepdims=True) + EPS) * g


def _mla_norm_fn(q_lora, kv_lora):
    def f(h, qn, kvn):
        return (_rms(h[:, :q_lora], qn), _rms(h[:, q_lora:q_lora + kv_lora], kvn),
                h[:, q_lora + kv_lora:q_lora + kv_lora + LANES])
    return f


def _log_sigmoid(z):
    return jnp.minimum(z, 0.0) - jnp.log(1.0 + jnp.exp(-jnp.abs(z)))


def _gla_gate_fn(z, b):
    return (_log_sigmoid(z + b) / GLA_TAU,)


def _ple_fn(x, glog, pp, b):
    return (x + jax.nn.sigmoid(glog + b) * pp,)


def _gla_out_fn(heads):
    def f(o, r, g):
        parts = []
        for h in range(heads):
            oh = o[:, h * GLA_DV:(h + 1) * GLA_DV]
            mu = jnp.mean(oh, -1, keepdims=True)
            oc = oh - mu
            var = jnp.mean(oc * oc, -1, keepdims=True)
            parts.append(oc * lax.rsqrt(var + EPS) * g[:, h * GLA_DV:(h + 1) * GLA_DV])
        return (jnp.concatenate(parts, axis=1) * (r * jax.nn.sigmoid(r)),)
    return f


def _rope_call(x, tabs, roped, out_dtype, base, fold=False):
    S, C = x.shape
    nb = C // LANES
    tr = min(512 if C <= 1024 else 256, S)
    out_c = LANES if fold else C

    def rot(v, a, b1, b2):
        return v * a + pltpu.roll(v, 96, 1) * b1 + pltpu.roll(v, 32, 1) * b2

    def body(x_ref, a_ref, b1_ref, b2_ref, o_ref):
        a, b1, b2 = a_ref[...], b1_ref[...], b2_ref[...]
        if fold:
            v = x_ref[:, 0:LANES].astype(F32)
            for blk in range(1, nb):
                v = v + x_ref[:, blk * LANES:(blk + 1) * LANES].astype(F32)
            o_ref[...] = rot(v, a, b1, b2).astype(o_ref.dtype)
            return
        for blk in range(nb):
            v = x_ref[:, blk * LANES:(blk + 1) * LANES].astype(F32)
            if roped(blk):
                v = rot(v, a, b1, b2)
            o_ref[:, blk * LANES:(blk + 1) * LANES] = v.astype(o_ref.dtype)

    row = lambda w: pl.BlockSpec((tr, w), lambda i: (i, 0))
    return pl.pallas_call(
        body, name=_nm(base), grid=(S // tr,), out_shape=jax.ShapeDtypeStruct((S, out_c), out_dtype),
        in_specs=[row(C), row(LANES), row(LANES), row(LANES)], out_specs=row(out_c),
        compiler_params=_cp(("parallel",)),
    )(x, *tabs)


def _attn_tile(S):
    return min(512, S)


def _tri_schedule(n, by_key):
    pairs = ([(i, j) for j in range(n) for i in range(j, n)] if by_key
             else [(i, j) for i in range(n) for j in range(i + 1)])
    return (jnp.asarray([p[0] for p in pairs], jnp.int32), jnp.asarray([p[1] for p in pairs], jnp.int32))


def _mask_table(cid, t):
    n = cid.shape[0] // t
    blocks = cid.reshape(n, t)
    cmin_q, cmax_k = jnp.min(blocks, axis=1), jnp.max(blocks, axis=1)
    need = (cmax_k[None, :] > cmin_q[:, None]) | jnp.eye(n, dtype=bool)
    return need.astype(jnp.int32).reshape(n * n)


def _attn_mask(cidq_ref, cidk_ref, i, j, t):
    qrow = i * t + lax.broadcasted_iota(jnp.int32, (t, 1), 0)
    kcol = j * t + lax.broadcasted_iota(jnp.int32, (1, t), 1)
    qlim = (qrow // Q_BLOCK + 1) * Q_BLOCK
    return (cidk_ref[...] <= cidq_ref[...]) & (kcol < qlim)


ATTN_FWD_HEADS = 4
ATTN_BWD_HEADS = 2


def _attn_fwd_call(q, kn, v, kr, aux, heads, comm=None):
    S = q.shape[0]
    t = _attn_tile(S)
    n = S // t
    hp = ATTN_FWD_HEADS if heads % ATTN_FWD_HEADS == 0 else 1
    qi_tab, kj_tab = _tri_schedule(n, False)
    scale2 = (MLA_NOPE + MLA_ROPE) ** -0.5 * LOG2E

    def body(qi_ref, kj_ref, need_ref, q_ref, kn_ref, v_ref, kr_ref, cidq_ref, cidk_ref, o_ref, lse_ref,
             m_ref, l_ref, acc_ref):
        st = pl.program_id(1)
        i, j = qi_ref[st], kj_ref[st]

        @pl.when(j == 0)
        def _():
            m_ref[...] = jnp.full(m_ref.shape, NEG_INF, F32)
            l_ref[...] = jnp.zeros(l_ref.shape, F32)
            acc_ref[...] = jnp.zeros(acc_ref.shape, F32)

        def update(masked):
            mask = _attn_mask(cidq_ref, cidk_ref, i, j, t) if masked else None
            for hh in range(hp):
                lanes = slice(hh * LANES, (hh + 1) * LANES)
                k = jnp.concatenate([kn_ref[:, lanes], kr_ref[...]], axis=1)
                qh = q_ref[:, hh * MLA_QK_PAD:(hh + 1) * MLA_QK_PAD]
                s = lax.dot_general(qh, k, (((1,), (1,)), ((), ())), preferred_element_type=F32) * scale2
                if masked:
                    s = jnp.where(mask, s, NEG_INF)
                m_prev = m_ref[:, lanes]
                m_new = jnp.maximum(m_prev, jnp.max(s, axis=1, keepdims=True))
                alpha = jnp.exp2(m_prev - m_new)
                p = jnp.exp2(s - m_new[:, :1])
                l_ref[:, lanes] = alpha * l_ref[:, lanes] + jnp.sum(p, axis=1, keepdims=True)
                acc_ref[:, lanes] = alpha * acc_ref[:, lanes] + jnp.dot(p.astype(BF16), v_ref[:, lanes],
                                                                        preferred_element_type=F32)
                m_ref[:, lanes] = m_new

        need = need_ref[i * n + j]

        @pl.when(need != 0)
        def _():
            update(True)

        @pl.when(need == 0)
        def _():
            update(False)

        @pl.when(j == i)
        def _():
            o_ref[...] = (acc_ref[...] / l_ref[...]).astype(o_ref.dtype)
            lse_ref[...] = m_ref[...] + jnp.log(l_ref[...]) * LOG2E

    qmap = lambda h, s, qi, kj, need: (qi[s], h)
    kmap = lambda h, s, qi, kj, need: (kj[s], h)
    return _pcall(
        body, "attn_fwd", (heads // hp, qi_tab.shape[0]), 3,
        in_specs=[pl.BlockSpec((t, hp * MLA_QK_PAD), qmap), pl.BlockSpec((t, hp * MLA_NOPE), kmap),
                  pl.BlockSpec((t, hp * MLA_V), kmap),
                  pl.BlockSpec((t, LANES), lambda h, s, qi, kj, need: (kj[s], 0)),
                  pl.BlockSpec((t, 1), lambda h, s, qi, kj, need: (qi[s], 0)),
                  pl.BlockSpec((1, t), lambda h, s, qi, kj, need: (0, kj[s]))],
        out_specs=[pl.BlockSpec((t, hp * MLA_V), qmap), pl.BlockSpec((t, hp * LANES), qmap)],
        out_shape=[jax.ShapeDtypeStruct((S, heads * MLA_V), BF16), jax.ShapeDtypeStruct((S, heads * LANES), F32)],
        scratch=[pltpu.VMEM((t, hp * LANES), F32), pltpu.VMEM((t, hp * LANES), F32), pltpu.VMEM((t, hp * MLA_V), F32)],
        sem=("parallel", "arbitrary"),
        args=[qi_tab, kj_tab, aux['need'], q, kn, v, kr, aux['cidq'], aux['cidk']], comm=comm)


def _attn_bwd_call(q, kn, v, kr, o, lse, do, aux, heads, comm=None):
    S = q.shape[0]
    t = _attn_tile(S)
    n = S // t
    qi_tab, kj_tab = _tri_schedule(n, True)
    scale = (MLA_NOPE + MLA_ROPE) ** -0.5
    scale2 = scale * LOG2E
    nt_dims = (((1,), (1,)), ((), ()))
    tn_dims = (((0,), (0,)), ((), ()))

    n_steps = n * (n + 1) // 2
    hp = ATTN_BWD_HEADS if heads % ATTN_BWD_HEADS == 0 else 1

    def body(qi_ref, kj_ref, need_ref, q_ref, kn_ref, v_ref, kr_ref, cidq_ref, cidk_ref, o_ref, lse_ref, do_ref,
             ta_ref, tb1_ref, tb2_ref, dq_ref, dkn_ref, dv_ref, dkr_ref, dk_acc, dv_acc, dq_acc):
        st = pl.program_id(1)
        i, j = qi_ref[st], kj_ref[st]

        @pl.when(st == 0)
        def _():
            dq_acc[...] = jnp.zeros(dq_acc.shape, F32)

        @pl.when(i == j)
        def _():
            dk_acc[...] = jnp.zeros(dk_acc.shape, F32)
            dv_acc[...] = jnp.zeros(dv_acc.shape, F32)

        rows = pl.ds(pl.multiple_of(i * t, t), t)

        def grads(masked):
            mask = _attn_mask(cidq_ref, cidk_ref, i, j, t) if masked else None
            for hh in range(hp):
                lanes = slice(hh * LANES, (hh + 1) * LANES)
                wide = slice(hh * MLA_QK_PAD, (hh + 1) * MLA_QK_PAD)
                k = jnp.concatenate([kn_ref[:, lanes], kr_ref[...]], axis=1)
                qt, do = q_ref[:, wide], do_ref[:, lanes]
                s = lax.dot_general(qt, k, nt_dims, preferred_element_type=F32) * scale2
                if masked:
                    s = jnp.where(mask, s, NEG_INF)
                dp = lax.dot_general(do, v_ref[:, lanes], nt_dims, preferred_element_type=F32)
                dsum = jnp.sum(do.astype(F32) * o_ref[:, lanes].astype(F32), axis=1, keepdims=True)
                p = jnp.exp2(s - lse_ref[:, hh * LANES:hh * LANES + 1])
                ds = (p * (dp - dsum) * scale).astype(BF16)
                dv_acc[:, lanes] += lax.dot_general(p.astype(BF16), do, tn_dims, preferred_element_type=F32)
                dk_acc[:, wide] += lax.dot_general(ds, qt, tn_dims, preferred_element_type=F32)
                dq_acc[rows, wide] += jnp.dot(ds, k, preferred_element_type=F32)

        need = need_ref[i * n + j]

        @pl.when(need != 0)
        def _():
            grads(True)

        @pl.when(need == 0)
        def _():
            grads(False)

        @pl.when(i == n - 1)
        def _():
            for hh in range(hp):
                lanes = slice(hh * LANES, (hh + 1) * LANES)
                off = hh * MLA_QK_PAD
                dkn_ref[:, lanes] = dk_acc[:, off:off + MLA_NOPE].astype(dkn_ref.dtype)
                dkr_ref[:, lanes] = dk_acc[:, off + MLA_NOPE:off + MLA_QK_PAD]
            dv_ref[...] = dv_acc[...].astype(dv_ref.dtype)

        @pl.when(st == n_steps - 1)
        def _():
            for r in range(n):
                rs = slice(r * t, (r + 1) * t)
                for hh in range(hp):
                    off = hh * MLA_QK_PAD
                    dq_ref[rs, off:off + MLA_NOPE] = dq_acc[rs, off:off + MLA_NOPE].astype(dq_ref.dtype)
                    g = dq_acc[rs, off + MLA_NOPE:off + MLA_QK_PAD]
                    g = g * ta_ref[rs, :] + pltpu.roll(g, 96, 1) * tb1_ref[rs, :] + pltpu.roll(g, 32, 1) * tb2_ref[rs, :]
                    dq_ref[rs, off + MLA_NOPE:off + MLA_QK_PAD] = g.astype(dq_ref.dtype)

    qmap = lambda h, s, qi, kj, need: (qi[s], h)
    kmap = lambda h, s, qi, kj, need: (kj[s], h)
    whole = pl.BlockSpec((S, LANES), lambda h, s, qi, kj, need: (0, 0))
    return _pcall(
        body, "attn_bwd", (heads // hp, qi_tab.shape[0]), 3,
        in_specs=[pl.BlockSpec((t, hp * MLA_QK_PAD), qmap), pl.BlockSpec((t, hp * MLA_NOPE), kmap),
                  pl.BlockSpec((t, hp * MLA_V), kmap),
                  pl.BlockSpec((t, LANES), lambda h, s, qi, kj, need: (kj[s], 0)),
                  pl.BlockSpec((t, 1), lambda h, s, qi, kj, need: (qi[s], 0)),
                  pl.BlockSpec((1, t), lambda h, s, qi, kj, need: (0, kj[s])),
                  pl.BlockSpec((t, hp * MLA_V), qmap), pl.BlockSpec((t, hp * LANES), qmap),
                  pl.BlockSpec((t, hp * MLA_V), qmap), whole, whole, whole],
        out_specs=[pl.BlockSpec((S, hp * MLA_QK_PAD), lambda h, s, qi, kj, need: (0, h)),
                   pl.BlockSpec((t, hp * MLA_NOPE), kmap), pl.BlockSpec((t, hp * MLA_V), kmap),
                   pl.BlockSpec((t, hp * LANES), kmap)],
        out_shape=[jax.ShapeDtypeStruct((S, heads * MLA_QK_PAD), BF16), jax.ShapeDtypeStruct((S, heads * MLA_NOPE), BF16),
                   jax.ShapeDtypeStruct((S, heads * MLA_V), BF16), jax.ShapeDtypeStruct((S, heads * LANES), F32)],
        scratch=[pltpu.VMEM((t, hp * MLA_QK_PAD), F32), pltpu.VMEM((t, hp * MLA_V), F32),
                 pltpu.VMEM((S, hp * MLA_QK_PAD), F32)],
        sem=("parallel", "arbitrary"),
        args=[qi_tab, kj_tab, aux['need'], q, kn, v, kr, aux['cidq'], aux['cidk'], o, lse, do, *aux['rope'][1]],
        comm=comm)


def make_attention(aux, heads):
    tabs_f, tabs_b = aux['rope']
    odd, every = (lambda blk: blk % 2 == 1), (lambda blk: True)

    def run_host(q_raw, kn, v, kr_raw, comm=None):
        q = _rope_call(q_raw, tabs_f, odd, BF16, "rope_q")
        kr = _rope_call(kr_raw, tabs_f, every, BF16, "rope_k")
        o, lse, *carried = _attn_fwd_call(q, kn, v, kr, aux, heads, comm)
        return o, (q, kn, v, kr, o, lse), carried

    def bwd_host(res, do, comm=None):
        q, kn, v, kr, o, lse = res
        dq_raw, dkn, dv, dkr, *carried = _attn_bwd_call(q, kn, v, kr, o, lse, do, aux, heads, comm)
        return (dq_raw, dkn, dv, _rope_call(dkr, tabs_b, every, F32, "rope_dk", fold=True)), carried

    @jax.custom_vjp
    def attn(q_raw, kn, v, kr_raw):
        return run_host(q_raw, kn, v, kr_raw)[0]

    attn.defvjp(lambda *a: run_host(*a)[:2], lambda res, do: bwd_host(res, do)[0])
    return attn, run_host, bwd_host


GLA_ROWS = 256


def _tri(lower):
    r = lax.broadcasted_iota(jnp.int32, (CHUNK, CHUNK), 0)
    c = lax.broadcasted_iota(jnp.int32, (CHUNK, CHUNK), 1)
    return jnp.where((c <= r) if lower else (c >= r), 1.0, 0.0).astype(F32)


def _gla_chunk(q_ref, k_ref, v_ref, la_ref, sl):
    la = la_ref[sl, :]
    cum = jnp.dot(_tri(True), la, preferred_element_type=F32, precision=lax.Precision.HIGHEST)
    tot = cum[CHUNK - 1:CHUNK, :]
    e = jnp.exp(tot - cum)
    k = k_ref[sl, :].astype(F32)
    kdec = k * e
    v = v_ref[sl, :]
    upd_t = lax.dot_general(v.astype(BF16), kdec.astype(BF16), (((0,), (0,)), ((), ())), preferred_element_type=F32)
    qs = (q_ref[sl, :].astype(F32) * (GLA_DK ** -0.5)).astype(BF16)
    return e, k, kdec, v, upd_t, jnp.exp(tot), qs


def _gla_specs(heads, rows_map):
    return [pl.BlockSpec((GLA_ROWS, GLA_DK), lambda h, b: (rows_map(b), h)),
            pl.BlockSpec((GLA_ROWS, GLA_DK), lambda h, b: (rows_map(b), heads + h)),
            pl.BlockSpec((GLA_ROWS, GLA_DV), lambda h, b: (rows_map(b), heads + h)),
            pl.BlockSpec((GLA_ROWS, GLA_DK), lambda h, b: (rows_map(b), h))]


def _gla_fwd_call(hm, la, heads):
    S = hm.shape[0]
    assert S % GLA_ROWS == 0
    nb, cpb = S // GLA_ROWS, GLA_ROWS // CHUNK

    def body(q_ref, k_ref, v_ref, la_ref, o_ref, sp_ref, st_ref):
        @pl.when(pl.program_id(1) == 0)
        def _():
            st_ref[...] = jnp.zeros(st_ref.shape, F32)

        for c in range(cpb):
            sl = slice(c * CHUNK, (c + 1) * CHUNK)
            _, _, _, _, upd_t, decay, qs = _gla_chunk(q_ref, k_ref, v_ref, la_ref, sl)
            state = st_ref[...]
            sp_ref[0, c] = state
            state = state * decay + upd_t
            st_ref[...] = state
            o_ref[sl, :] = lax.dot_general(qs, state.astype(BF16), (((1,), (1,)), ((), ())), preferred_element_type=F32)

    return pl.pallas_call(
        body, name=_nm("gla_fwd"), grid=(heads, nb),
        out_shape=(jax.ShapeDtypeStruct((S, heads * GLA_DV), F32),
                   jax.ShapeDtypeStruct((heads, S // CHUNK, GLA_DV, GLA_DK), F32)),
        in_specs=_gla_specs(heads, lambda b: b),
        out_specs=(pl.BlockSpec((GLA_ROWS, GLA_DV), lambda h, b: (b, h)),
                   pl.BlockSpec((1, cpb, GLA_DV, GLA_DK), lambda h, b: (h, b, 0, 0))),
        scratch_shapes=[pltpu.VMEM((GLA_DV, GLA_DK), F32)],
        compiler_params=_cp(("parallel", "arbitrary")),
    )(hm, hm, hm, la)


def _gla_bwd_call(hm, la, sprev, do, heads):
    S = hm.shape[0]
    nb, cpb = S // GLA_ROWS, GLA_ROWS // CHUNK
    scale = GLA_DK ** -0.5

    def body(q_ref, k_ref, v_ref, la_ref, sp_ref, do_ref, dq_ref, dk_ref, dv_ref, dla_ref, carry_ref):
        @pl.when(pl.program_id(1) == 0)
        def _():
            carry_ref[...] = jnp.zeros(carry_ref.shape, F32)

        for c in reversed(range(cpb)):
            sl = slice(c * CHUNK, (c + 1) * CHUNK)
            e, k, kdec, v, upd_t, decay, qs = _gla_chunk(q_ref, k_ref, v_ref, la_ref, sl)
            sp = sp_ref[0, c]
            s_n = sp * decay + upd_t
            dob = do_ref[sl, :].astype(BF16)
            g = carry_ref[...] + lax.dot_general(dob, qs, (((0,), (0,)), ((), ())), preferred_element_type=F32)
            gb = g.astype(BF16)
            dq_ref[sl, :] = (jnp.dot(dob, s_n.astype(BF16), preferred_element_type=F32) * scale).astype(dq_ref.dtype)
            ddecay = jnp.sum(g * sp, axis=0, keepdims=True)
            dkdec = jnp.dot(v.astype(BF16), gb, preferred_element_type=F32)
            dv_ref[sl, :] = lax.dot_general(kdec.astype(BF16), gb, (((1,), (1,)), ((), ())),
                                            preferred_element_type=F32).astype(dv_ref.dtype)
            dk_ref[sl, :] = (dkdec * e).astype(dk_ref.dtype)
            w = dkdec * k * e
            dtot = jnp.sum(w, axis=0, keepdims=True) + ddecay * decay
            last = lax.broadcasted_iota(jnp.int32, (CHUNK, 1), 0) == CHUNK - 1
            dcum = jnp.where(last, dtot - w, -w)
            dla_ref[sl, :] = jnp.dot(_tri(False), dcum, preferred_element_type=F32, precision=lax.Precision.HIGHEST)
            carry_ref[...] = g * decay

    rev = lambda b: nb - 1 - b
    return pl.pallas_call(
        body, name=_nm("gla_bwd"), grid=(heads, nb),
        out_shape=(jax.ShapeDtypeStruct((S, heads * GLA_DK), hm.dtype), jax.ShapeDtypeStruct((S, heads * GLA_DK), hm.dtype),
                   jax.ShapeDtypeStruct((S, heads * GLA_DV), hm.dtype), jax.ShapeDtypeStruct((S, heads * GLA_DK), F32)),
        in_specs=_gla_specs(heads, rev) + [
            pl.BlockSpec((1, cpb, GLA_DV, GLA_DK), lambda h, b: (h, rev(b), 0, 0)),
            pl.BlockSpec((GLA_ROWS, GLA_DV), lambda h, b: (rev(b), h))],
        out_specs=(pl.BlockSpec((GLA_ROWS, GLA_DK), lambda h, b: (rev(b), h)),
                   pl.BlockSpec((GLA_ROWS, GLA_DK), lambda h, b: (rev(b), h)),
                   pl.BlockSpec((GLA_ROWS, GLA_DV), lambda h, b: (rev(b), h)),
                   pl.BlockSpec((GLA_ROWS, GLA_DK), lambda h, b: (rev(b), h))),
        scratch_shapes=[pltpu.VMEM((GLA_DV, GLA_DK), F32)],
        compiler_params=_cp(("parallel", "arbitrary")),
    )(hm, hm, hm, la, sprev, do)


@functools.partial(jax.custom_vjp, nondiff_argnums=(3,))
def gla_core(hm, la, o_norm, heads):
    return _gla_core_fwd(hm, la, o_norm, heads)[0]


def _gla_core_fwd(hm, la, o_norm, heads):
    o, sprev = _gla_fwd_call(hm, la, heads)
    vd = heads * GLA_DV
    rows = [(o, vd, 0), (hm, vd, 2 * heads * GLA_DK // vd + 1)]
    (y,) = _rw_fwd(_gla_out_fn(heads), rows, [o_norm], [BF16], "gla_out_fwd")
    return y, (hm, la, o_norm, o, sprev)


def _gla_core_bwd(heads, res, dy):
    hm, la, o_norm, o, sprev = res
    vd = heads * GLA_DV
    rows = [(o, vd, 0), (hm, vd, 2 * heads * GLA_DK // vd + 1)]
    (do, dr), (dg,) = _rw_bwd(_gla_out_fn(heads), rows, [o_norm], [dy], [F32, hm.dtype], "gla_out_bwd")
    dq, dk, dv, dla = _gla_bwd_call(hm, la, sprev, do, heads)
    return jnp.concatenate([dq, dk, dv, dr], axis=1), dla, dg


gla_core.defvjp(_gla_core_fwd, _gla_core_bwd)


CONV_COLS = 256
HALO = 16


def _conv_rows(S):
    return min(512, S)


def _conv_taps(main_ref, halo_ref, i):
    prev = jnp.where(i > 0, halo_ref[...].astype(F32), 0.0)
    full = jnp.concatenate([prev, main_ref[...].astype(F32)], axis=0)
    return full[HALO:], pltpu.roll(full, 1, 0)[HALO:], pltpu.roll(full, 2, 0)[HALO:]


def _conv_apply(taps, w_ref, b_ref):
    x0, x1, x2 = taps
    return x2 * w_ref[0:1, :] + x1 * w_ref[1:2, :] + x0 * w_ref[2:3, :] + b_ref[...]


def _gelu_gate(uc, gc):
    return uc * jax.nn.gelu(gc)


def _conv_cols(dff, pref):
    return max(c for c in range(LANES, pref + 1, LANES) if dff % c == 0)


def _conv_in_specs(R, C, nj):
    hpr = R // HALO
    main = pl.BlockSpec((R, C), lambda j, i: (i, j))
    halo = pl.BlockSpec((HALO, C), lambda j, i: (jnp.maximum(i * hpr - 1, 0), j))
    par = lambda rows, off: pl.BlockSpec((rows, C), lambda j, i: (0, j + off))
    return [main, halo, main, halo, par(CONV_W, 0), par(CONV_W, nj), par(1, 0), par(1, nj)]


def _conv_fwd_call(hu, hg, cw, cb, comm=None):
    S, dff = hu.shape
    R, C = _conv_rows(S), _conv_cols(dff, 768)
    nj = dff // C

    def body(u_ref, uh_ref, g_ref, gh_ref, wu_ref, wg_ref, bu_ref, bg_ref, a_ref):
        i = pl.program_id(1)
        uc = _conv_apply(_conv_taps(u_ref, uh_ref, i), wu_ref, bu_ref)
        gc = _conv_apply(_conv_taps(g_ref, gh_ref, i), wg_ref, bg_ref)
        a_ref[...] = _gelu_gate(uc, gc).astype(a_ref.dtype)

    return _pcall(
        body, "conv_fwd", (nj, S // R), 0, in_specs=_conv_in_specs(R, C, nj),
        out_specs=[pl.BlockSpec((R, C), lambda j, i: (i, j))], out_shape=[jax.ShapeDtypeStruct((S, dff), BF16)],
        scratch=[], sem=("parallel", "parallel"), args=[hu, hu, hg, hg, cw, cw, cb, cb], comm=comm)


def _conv_bwd_gate_call(hu, hg, cw, cb, da, comm=None):
    S, dff = hu.shape
    R, C = _conv_rows(S), _conv_cols(dff, 512)
    nj = dff // C

    def body(u_ref, uh_ref, g_ref, gh_ref, wu_ref, wg_ref, bu_ref, bg_ref, da_ref,
             du_ref, dg_ref, dwu_ref, dwg_ref, dbu_ref, dbg_ref):
        i = pl.program_id(1)
        ut, gt = _conv_taps(u_ref, uh_ref, i), _conv_taps(g_ref, gh_ref, i)
        uc, gc = _conv_apply(ut, wu_ref, bu_ref), _conv_apply(gt, wg_ref, bg_ref)
        _, vjp_fn = jax.vjp(_gelu_gate, uc, gc)
        du, dg = vjp_fn(da_ref[...].astype(F32))
        du_ref[...] = du.astype(du_ref.dtype)
        dg_ref[...] = dg.astype(dg_ref.dtype)

        @pl.when(i == 0)
        def _():
            for r in (dwu_ref, dwg_ref, dbu_ref, dbg_ref):
                r[...] = jnp.zeros(r.shape, F32)

        for d, taps, dw_ref, db_ref in ((du, ut, dwu_ref, dbu_ref), (dg, gt, dwg_ref, dbg_ref)):
            x0, x1, x2 = taps
            dw_ref[0:1, :] += jnp.sum(d * x2, axis=0, keepdims=True)
            dw_ref[1:2, :] += jnp.sum(d * x1, axis=0, keepdims=True)
            dw_ref[2:3, :] += jnp.sum(d * x0, axis=0, keepdims=True)
            db_ref[...] += jnp.sum(d, axis=0, keepdims=True)

    tile = pl.BlockSpec((R, C), lambda j, i: (i, j))
    par = lambda rows: pl.BlockSpec((rows, C), lambda j, i: (0, j))
    return _pcall(
        body, "conv_bwd_gate", (nj, S // R), 0, in_specs=_conv_in_specs(R, C, nj) + [tile],
        out_specs=[tile, tile, par(CONV_W), par(CONV_W), par(1), par(1)],
        out_shape=[jax.ShapeDtypeStruct((S, dff), BF16), jax.ShapeDtypeStruct((S, dff), BF16),
                   jax.ShapeDtypeStruct((CONV_W, dff), F32), jax.ShapeDtypeStruct((CONV_W, dff), F32),
                   jax.ShapeDtypeStruct((1, dff), F32), jax.ShapeDtypeStruct((1, dff), F32)],
        scratch=[], sem=("parallel", "arbitrary"), args=[hu, hu, hg, hg, cw, cw, cb, cb, da], comm=comm)


def _conv_bwd_shift_call(dc, cw, second_half, out_dtype):
    S, dff = dc.shape
    R, C = _conv_rows(S), _conv_cols(dff, 1024)
    nj = dff // C
    col_off = nj if second_half else 0
    hpr, last = R // HALO, S // HALO - 1
    ni = S // R

    def body(d_ref, nx_ref, w_ref, o_ref):
        i = pl.program_id(1)
        nxt = jnp.where(i < ni - 1, nx_ref[...].astype(F32), 0.0)
        full = jnp.concatenate([d_ref[...].astype(F32), nxt], axis=0)
        n = R + HALO
        y1, y2 = pltpu.roll(full, n - 1, 0)[:R], pltpu.roll(full, n - 2, 0)[:R]
        o_ref[...] = (full[:R] * w_ref[2:3, :] + y1 * w_ref[1:2, :] + y2 * w_ref[0:1, :]).astype(o_ref.dtype)

    return pl.pallas_call(
        body, name=_nm("conv_bwd_shift"), grid=(nj, ni), out_shape=jax.ShapeDtypeStruct((S, dff), out_dtype),
        in_specs=[pl.BlockSpec((R, C), lambda j, i: (i, j)),
                  pl.BlockSpec((HALO, C), lambda j, i: (jnp.minimum((i + 1) * hpr, last), j)),
                  pl.BlockSpec((CONV_W, C), lambda j, i: (0, j + col_off))],
        out_specs=pl.BlockSpec((R, C), lambda j, i: (i, j)),
        compiler_params=_cp(("parallel", "parallel")),
    )(dc, dc, cw)


@jax.custom_vjp
def ffn_hidden(x1, w3, cw, cb):
    return _ffn_hidden_fwd(x1, w3, cw, cb)[0]


def _ffn_hidden_fwd(x1, w3, cw, cb, comm=None):
    half = w3.shape[0] // 2
    hu = _mm(x1, w3, "nn", BF16, "up_u", (0, half))
    hg = _mm(x1, w3, "nn", BF16, "up_g", (half, half))
    a, *carried = _conv_fwd_call(hu, hg, cw, cb, comm)
    return (a, (x1, w3, cw, cb, hu, hg)) + ((carried,) if comm is not None else ())


def _ffn_hidden_bwd(res, da, comm=None):
    x1, w3, cw, cb, hu, hg = res
    half = w3.shape[0] // 2
    du, dg, dwu, dwg, dbu, dbg, *carried = _conv_bwd_gate_call(hu, hg, cw, cb, da, comm)
    dhu = _conv_bwd_shift_call(du, cw, False, BF16)
    dhg = _conv_bwd_shift_call(dg, cw, True, BF16)
    dx = _mm(dhu, w3, "nt", F32, "up_dx_u", (0, half)) + _mm(dhg, w3, "nt", F32, "up_dx_g", (half, half))
    dw3 = jnp.concatenate([_mm(x1, dhu, "tn", w3.dtype, "up_dw_u", (0, half)),
                           _mm(x1, dhg, "tn", w3.dtype, "up_dw_g", (half, half))], axis=0)
    grads = (dx, dw3, jnp.concatenate([dwu, dwg], axis=1), jnp.concatenate([dbu, dbg], axis=1))
    return (grads, carried) if comm is not None else grads


ffn_hidden.defvjp(_ffn_hidden_fwd, _ffn_hidden_bwd)


def _loss_call(y, target):
    S, D = y.shape
    tr = _row_tile(S, D)

    def body(y_ref, t_ref, sq_ref, dy_ref):
        diff = y_ref[...] - t_ref[...]
        dy_ref[...] = diff * (1.0 / D)
        part = jnp.sum(diff * diff, axis=0, keepdims=True)
        i = pl.program_id(0)

        @pl.when(i == 0)
        def _():
            sq_ref[...] = part

        @pl.when(i > 0)
        def _():
            sq_ref[...] += part

    row = pl.BlockSpec((tr, D), lambda i: (i, 0))
    return pl.pallas_call(
        body, name=_nm("loss"), grid=(S // tr,),
        out_shape=(jax.ShapeDtypeStruct((1, D), F32), jax.ShapeDtypeStruct((S, D), F32)),
        in_specs=[row, row], out_specs=(pl.BlockSpec((1, D), lambda i: (0, 0)), row),
        compiler_params=_cp(("arbitrary",)),
    )(y, target)


def _row_divisor(rows):
    for cand in range(min(rows, 512), 15, -1):
        if rows % cand == 0 and cand % 16 == 0:
            return cand
    return rows


def _adamw_call(w, g, m, v):
    shape = w.shape
    w2, g2, m2, v2 = (a.reshape(-1, shape[-1]) for a in (w, g, m, v))
    rows, cols = w2.shape
    tr = _row_divisor(rows)

    def body(w_ref, g_ref, m_ref, v_ref, d_ref, nm_ref, nv_ref):
        g_ = g_ref[...]
        m_ = ADAM_B1 * m_ref[...] + (1.0 - ADAM_B1) * g_
        v_ = ADAM_B2 * v_ref[...] + (1.0 - ADAM_B2) * (g_ * g_)
        m_hat = m_ / (1.0 - ADAM_B1 ** ADAM_STEP)
        v_hat = v_ / (1.0 - ADAM_B2 ** ADAM_STEP)
        d_ref[...] = -ADAM_LR * (m_hat / (jnp.sqrt(v_hat) + ADAM_EPS) + ADAM_WD * w_ref[...])
        nm_ref[...] = m_
        nv_ref[...] = v_

    blk = pl.BlockSpec((tr, cols), lambda i: (i, 0))
    outs = pl.pallas_call(
        body, name=_nm("adamw"), grid=(rows // tr,),
        out_shape=tuple(jax.ShapeDtypeStruct((rows, cols), F32) for _ in range(3)),
        in_specs=[blk] * 4, out_specs=(blk,) * 3, compiler_params=_cp(("parallel",)),
    )(w2, g2, m2, v2)
    return tuple(o.reshape(shape) for o in outs)


ANY = pl.BlockSpec(memory_space=pl.ANY)


def _place():
    return lax.axis_index("x"), lax.axis_index("y"), lax.axis_index("c")


def all_gather(shards):
    n = len(shards)

    def body(*refs):
        x_refs, out_refs = refs[:n], refs[n:2 * n]
        send_sems, recv_sems, local_sems = refs[2 * n:]
        x, y, c = _place()
        me, sibling = (x, y, c), (x, y, 1 - c)
        chips = [(1 - x, y), (x, 1 - y), (1 - x, 1 - y)]

        def slot(a, px, py, pc):
            return out_refs[a].at[:, 4 * px + 2 * py + pc]

        def copy(a, k, block, to, own=False):
            return pltpu.make_async_remote_copy(
                src_ref=x_refs[a] if own else slot(a, *block), dst_ref=slot(a, *block),
                send_sem=send_sems.at[7 * a + k], recv_sem=recv_sems.at[7 * a + k], device_id=to, device_id_type=MESH)

        mine = [pltpu.make_async_copy(x_refs[a], slot(a, *me), local_sems.at[a]) for a in range(n)]
        first = []
        for a in range(n):
            mine[a].start()
            first.append(copy(a, 0, me, sibling, own=True))
            first += [copy(a, 1 + j, me, (*chip, c), own=True) for j, chip in enumerate(chips)]
        for cp in first:
            cp.start()
        passed = []
        for j, chip in enumerate(chips):
            for a in range(n):
                copy(a, 1 + j, (*chip, c), me).wait_recv()
                passed.append(copy(a, 4 + j, (*chip, c), sibling))
                passed[-1].start()
        for a in range(n):
            copy(a, 0, sibling, me).wait_recv()
        for j, chip in enumerate(chips):
            for a in range(n):
                copy(a, 4 + j, (*chip, 1 - c), me).wait_recv()
        for cp in first + passed:
            cp.wait_send()
        for cp in mine:
            cp.wait()

    return pl.pallas_call(
        body, name=_nm("all_gather"),
        out_shape=tuple(jax.ShapeDtypeStruct((s.shape[0], N_DEV) + s.shape[1:], s.dtype) for s in shards),
        in_specs=[ANY] * n, out_specs=(ANY,) * n,
        scratch_shapes=[pltpu.SemaphoreType.DMA((7 * n,)), pltpu.SemaphoreType.DMA((7 * n,)), pltpu.SemaphoreType.DMA((n,))],
    )(*shards)


def _rs_pair_exchange(gs):
    n = len(gs)

    def body(*refs):
        g_refs, recv_refs = refs[:n], refs[n:2 * n]
        send_sems, recv_sems = refs[2 * n:]
        x, y, c = _place()
        copies = [pltpu.make_async_remote_copy(
            src_ref=g_refs[a].at[:, 2 * j + (1 - c)], dst_ref=recv_refs[a].at[j], send_sem=send_sems.at[4 * a + j],
            recv_sem=recv_sems.at[4 * a + j], device_id=(x, y, 1 - c), device_id_type=MESH)
            for a in range(n) for j in range(4)]
        for cp in copies:
            cp.start()
        for cp in copies:
            cp.wait_recv()
        for cp in copies:
            cp.wait_send()

    return pl.pallas_call(
        body, name=_nm("rs_pair"),
        out_shape=tuple(jax.ShapeDtypeStruct((4, g.shape[0]) + g.shape[2:], g.dtype) for g in gs),
        in_specs=[ANY] * n, out_specs=(ANY,) * n,
        scratch_shapes=[pltpu.SemaphoreType.DMA((4 * n,)), pltpu.SemaphoreType.DMA((4 * n,))],
    )(*gs)


def _rs_chip_exchange(ps):
    n = len(ps)

    def body(*refs):
        p_refs, recv_refs = refs[:n], refs[n:2 * n]
        send_sems, recv_sems = refs[2 * n:]
        x, y, c = _place()
        chips = [(1 - x, y), (x, 1 - y), (1 - x, 1 - y)]
        copies = [pltpu.make_async_remote_copy(
            src_ref=p_refs[a].at[2 * cx + cy], dst_ref=recv_refs[a].at[k], send_sem=send_sems.at[3 * a + k],
            recv_sem=recv_sems.at[3 * a + k], device_id=(cx, cy, c), device_id_type=MESH)
            for a in range(n) for k, (cx, cy) in enumerate(chips)]
        for cp in copies:
            cp.start()
        for cp in copies:
            cp.wait_recv()
        for cp in copies:
            cp.wait_send()

    return pl.pallas_call(
        body, name=_nm("rs_chip"),
        out_shape=tuple(jax.ShapeDtypeStruct((3,) + p.shape[1:], p.dtype) for p in ps),
        in_specs=[ANY] * n, out_specs=(ANY,) * n,
        scratch_shapes=[pltpu.SemaphoreType.DMA((3 * n,)), pltpu.SemaphoreType.DMA((3 * n,))],
    )(*ps)


def _rs_pair_add(g, recv, c_idx):
    L, _, a, b = g.shape
    ta = _row_divisor(a)

    def body(c_ref, g_ref, r_ref, o_ref):
        o_ref[...] = (g_ref[...].astype(F32) + r_ref[...].astype(F32)).astype(o_ref.dtype)

    grid_spec = pltpu.PrefetchScalarGridSpec(
        num_scalar_prefetch=1, grid=(4, L, a // ta),
        in_specs=[pl.BlockSpec((None, None, ta, b), lambda j, l, i, c_ref: (l, 2 * j + c_ref[0], i, 0)),
                  pl.BlockSpec((None, None, ta, b), lambda j, l, i, c_ref: (j, l, i, 0))],
        out_specs=pl.BlockSpec((None, None, ta, b), lambda j, l, i, c_ref: (j, l, i, 0)))
    return pl.pallas_call(
        body, name=_nm("rs_pair_add"), grid_spec=grid_spec, out_shape=jax.ShapeDtypeStruct((4, L, a, b), g.dtype),
        compiler_params=_cp(("parallel", "parallel", "parallel")),
    )(c_idx, g, recv)


def _rs_final_add(p1, recv, chip_idx):
    _, L, a, b = p1.shape
    ta = _row_divisor(a)

    def body(chip_ref, p_ref, r_ref, o_ref):
        acc = p_ref[...].astype(F32)
        for k in range(3):
            acc = acc + r_ref[k].astype(F32)
        o_ref[...] = acc

    grid_spec = pltpu.PrefetchScalarGridSpec(
        num_scalar_prefetch=1, grid=(L, a // ta),
        in_specs=[pl.BlockSpec((None, None, ta, b), lambda l, i, chip_ref: (chip_ref[0], l, i, 0)),
                  pl.BlockSpec((3, None, ta, b), lambda l, i, chip_ref: (0, l, i, 0))],
        out_specs=pl.BlockSpec((None, ta, b), lambda l, i, chip_ref: (l, i, 0)))
    return pl.pallas_call(
        body, name=_nm("rs_final_add"), grid_spec=grid_spec, out_shape=jax.ShapeDtypeStruct((L, a, b), F32),
        compiler_params=_cp(("parallel", "parallel")),
    )(chip_idx, p1, recv)


def reduce_scatter(gs):
    x, y, c = _place()
    c_idx = jnp.reshape(c, (1,)).astype(jnp.int32)
    chip_idx = jnp.reshape(2 * x + y, (1,)).astype(jnp.int32)
    recv1 = _rs_pair_exchange(gs)
    p1 = [_rs_pair_add(g, r, c_idx) for g, r in zip(gs, recv1)]
    recv2 = _rs_chip_exchange(p1)
    return [_rs_final_add(p, r, chip_idx) for p, r in zip(p1, recv2)]


def all_reduce_small(v):
    r, C = v.shape

    def body(v_ref, out_ref, buf_ref, send_sems, recv_sems):
        x, y, c = _place()
        my_id = 4 * x + 2 * y + c
        buf_ref[my_id] = v_ref[...]
        copies = []
        for k in range(1, N_DEV):
            fx, fy, fc = (k >> 2) & 1, (k >> 1) & 1, k & 1
            peer = (x ^ fx, y ^ fy, c ^ fc)
            copies.append(pltpu.make_async_remote_copy(
                src_ref=v_ref, dst_ref=buf_ref.at[my_id], send_sem=send_sems.at[k - 1], recv_sem=recv_sems.at[k - 1],
                device_id=peer, device_id_type=MESH))
        for cp in copies:
            cp.start()
        for cp in copies:
            cp.wait_recv()
        for cp in copies:
            cp.wait_send()
        acc = buf_ref[0]
        for d in range(1, N_DEV):
            acc = acc + buf_ref[d]
        out_ref[...] = acc

    vm = pl.BlockSpec(memory_space=pltpu.VMEM)
    return pl.pallas_call(
        body, name=_nm("all_reduce_small"), out_shape=jax.ShapeDtypeStruct((r, C), F32),
        in_specs=[vm], out_specs=vm,
        scratch_shapes=[pltpu.VMEM((N_DEV, r, C), F32), pltpu.SemaphoreType.DMA((7,)), pltpu.SemaphoreType.DMA((7,))],
    )(v)


def _slot(ref, px, py, pc):
    return ref.at[:, 4 * px + 2 * py + pc]


def comm_gather_own(shards):
    n = len(shards)

    def build(cin, cout, send_sems, recv_sems, local_sems):
        x, y, c = _place()
        me = (x, y, c)
        peers = [(x, y, 1 - c), (1 - x, y, c), (x, 1 - y, c), (1 - x, 1 - y, c)]
        starts, waits = [], []
        for a in range(n):
            local = pltpu.make_async_copy(cin[a], _slot(cout[a], *me), local_sems.at[a])
            starts.append(local)
            waits.append(local.wait)
            for k, peer in enumerate(peers):
                send = pltpu.make_async_remote_copy(
                    src_ref=cin[a], dst_ref=_slot(cout[a], *me), send_sem=send_sems.at[4 * a + k],
                    recv_sem=recv_sems.at[4 * a + k], device_id=peer, device_id_type=MESH)
                arrive = pltpu.make_async_remote_copy(
                    src_ref=cin[a], dst_ref=_slot(cout[a], *peer), send_sem=send_sems.at[4 * a + k],
                    recv_sem=recv_sems.at[4 * a + k], device_id=peer, device_id_type=MESH)
                starts.append(send)
                waits += [arrive.wait_recv, send.wait_send]
        return starts, waits

    out_shapes = [jax.ShapeDtypeStruct((s.shape[0], N_DEV) + s.shape[1:], s.dtype) for s in shards]
    return Comm(shards, out_shapes, {}, 4 * n, 4 * n, n, build)


def comm_gather_pass(partial):
    n = len(partial)

    def build(cin, cout, send_sems, recv_sems, local_sems):
        x, y, c = _place()
        chips = [(1 - x, y), (x, 1 - y), (1 - x, 1 - y)]
        starts, waits = [], []
        for a in range(n):
            for j, chip in enumerate(chips):
                send = pltpu.make_async_remote_copy(
                    src_ref=_slot(cout[a], *chip, c), dst_ref=_slot(cout[a], *chip, c), send_sem=send_sems.at[3 * a + j],
                    recv_sem=recv_sems.at[3 * a + j], device_id=(x, y, 1 - c), device_id_type=MESH)
                arrive = pltpu.make_async_remote_copy(
                    src_ref=_slot(cout[a], *chip, c), dst_ref=_slot(cout[a], *chip, 1 - c),
                    send_sem=send_sems.at[3 * a + j], recv_sem=recv_sems.at[3 * a + j],
                    device_id=(x, y, 1 - c), device_id_type=MESH)
                starts.append(send)
                waits += [arrive.wait_recv, send.wait_send]
        return starts, waits

    out_shapes = [jax.ShapeDtypeStruct(p.shape, p.dtype) for p in partial]
    return Comm(partial, out_shapes, {a: a for a in range(n)}, 3 * n, 3 * n, 1, build)


def comm_rs_pair(gs):
    n = len(gs)

    def build(cin, cout, send_sems, recv_sems, local_sems):
        x, y, c = _place()
        starts, waits = [], []
        for a in range(n):
            for j in range(4):
                cp = pltpu.make_async_remote_copy(
                    src_ref=cin[a].at[:, 2 * j + (1 - c)], dst_ref=cout[a].at[j], send_sem=send_sems.at[4 * a + j],
                    recv_sem=recv_sems.at[4 * a + j], device_id=(x, y, 1 - c), device_id_type=MESH)
                starts.append(cp)
                waits += [cp.wait_recv, cp.wait_send]
        return starts, waits

    out_shapes = [jax.ShapeDtypeStruct((4, g.shape[0]) + g.shape[2:], g.dtype) for g in gs]
    return Comm(gs, out_shapes, {}, 4 * n, 4 * n, 1, build)


def comm_rs_chip(ps):
    n = len(ps)

    def build(cin, cout, send_sems, recv_sems, local_sems):
        x, y, c = _place()
        chips = [(1 - x, y), (x, 1 - y), (1 - x, 1 - y)]
        starts, waits = [], []
        for a in range(n):
            for k, (cx, cy) in enumerate(chips):
                cp = pltpu.make_async_remote_copy(
                    src_ref=cin[a].at[2 * cx + cy], dst_ref=cout[a].at[k], send_sem=send_sems.at[3 * a + k],
                    recv_sem=recv_sems.at[3 * a + k], device_id=(cx, cy, c), device_id_type=MESH)
                starts.append(cp)
                waits += [cp.wait_recv, cp.wait_send]
        return starts, waits

    out_shapes = [jax.ShapeDtypeStruct((3,) + p.shape[1:], p.dtype) for p in ps]
    return Comm(ps, out_shapes, {}, 3 * n, 3 * n, 1, build)


def _pack(arrays, dtype, row_align):
    lead = arrays[0].shape[:-1]
    quantum = row_align * PACK_COLS
    parts, sizes = [], []
    for a in arrays:
        n = a.shape[-1]
        padded = _round_up(n, quantum)
        a = a.astype(dtype)
        if padded != n:
            a = jnp.pad(a, [(0, 0)] * len(lead) + [(0, padded - n)])
        parts.append(a.reshape(*lead, padded // PACK_COLS, PACK_COLS))
        sizes.append((n, padded // PACK_COLS))
    return jnp.concatenate(parts, axis=len(lead)), sizes


def _unpack(packed, sizes):
    lead = packed.shape[:-2]
    out, row = [], 0
    for n, rows in sizes:
        part = lax.slice_in_dim(packed, row, row + rows, axis=len(lead))
        out.append(part.reshape(*lead, rows * PACK_COLS)[..., :n])
        row += rows
    return out


def _unshard(gathered, axis):
    _, L, a, b = gathered.shape
    if axis == 1:
        return [gathered[:, l].reshape(N_DEV * a, b) for l in range(L)]
    return [jnp.transpose(gathered[:, l], (1, 0, 2)).reshape(a, N_DEV * b) for l in range(L)]


def _reshard(fulls, axis):
    blocks = []
    for f in fulls:
        A, B = f.shape
        if axis == 1:
            blocks.append(f.reshape(N_DEV, A // N_DEV, B))
        else:
            blocks.append(jnp.transpose(f.reshape(A, N_DEV, B // N_DEV), (1, 0, 2)))
    return jnp.stack(blocks, axis=1)


def _as3(a):
    return a if a.ndim == 3 else a[:, None, :]


def _prep_big(name, w, dims):
    w = w.astype(BF16)
    L, a, b = w.shape
    if name == 'mla_w_in':
        return jnp.pad(w, ((0, 0), (0, 0), (0, dims['h_width'] - b)))
    if name == 'mla_w_uq':
        hd = MLA_NOPE + MLA_ROPE
        w = jnp.pad(w.reshape(L, a, b // hd, hd), ((0, 0), (0, 0), (0, 0), (0, MLA_QK_PAD - hd)))
        return w.reshape(L, a, b // hd * MLA_QK_PAD)
    if name == 'ffn_w_up':
        return jnp.pad(w, ((0, 0), (0, 0), (0, _round_up(b, CONV_COLS) - b)))
    return w


def _unprep_big(name, g, shape):
    L, a, b = shape
    if name == 'mla_w_uq':
        hd = MLA_NOPE + MLA_ROPE
        return g.reshape(L, a, b // hd, MLA_QK_PAD)[..., :hd].reshape(L, a, b)
    return g[:, :, :b]


def _rope_tables(positions):
    inv = 1.0 / (ROPE_THETA ** (jnp.arange(0, MLA_ROPE, 2, dtype=F32) / MLA_ROPE))
    ang = positions.astype(F32)[:, None] * inv
    cos, sin = jnp.cos(ang), jnp.sin(ang)
    one, zero = jnp.ones_like(cos), jnp.zeros_like(cos)
    a = jnp.concatenate([cos, cos, one, one], axis=1)
    up = jnp.concatenate([sin, zero, zero, zero], axis=1)
    down = jnp.concatenate([zero, sin, zero, zero], axis=1)
    return (a, -up, down), (a, up, -down)


def _rows_full(w):
    return w.reshape(w.shape[0] * w.shape[1], w.shape[2])


def _ops(depth):
    alpha = (2 * depth) ** 0.25
    return {'ln_res': rw_op(_ln_res_fn(alpha), "ln_res", 2, [F32]), 'ple': rw_op(_ple_fn, "ple", 3, [F32]),
            'gla_gate': rw_op(_gla_gate_fn, "gla_gate", 1, [F32])}


MLA_PRE_W = ['mla_w_in', 'mla_q_norm', 'mla_kv_norm', 'mla_w_uq', 'mla_w_uk', 'mla_w_uv']
FFN_IN_W = ['mla_w_o', 'ln1_g', 'ln1_b', 'ffn_conv_w', 'ffn_conv_b']
FFN_OUT_W = ['ffn_w_down', 'ln2_g', 'ln2_b', 'ple_w_gate', 'ple_w_proj', 'ple_b_gate']


def _mla_heads(wl, j):
    return N_DEV * wl['mla_w_uk'][j].shape[2] // MLA_NOPE


def _mla_pre(x, wl, j):
    w_uq, w_uk, w_uv = wl['mla_w_uq'][j], wl['mla_w_uk'][j], wl['mla_w_uv'][j]
    h = linear(x, _rows_full(wl['mla_w_in'][j]), F32)
    mla_norm = rw_op(_mla_norm_fn(w_uq.shape[1], w_uk.shape[1]), "mla_norm", 1, [BF16, BF16, F32])
    cq, ckv, kr_raw = mla_norm(h, wl['mla_q_norm'][j], wl['mla_kv_norm'][j])
    return linear(cq, w_uq, BF16), linear(ckv, w_uk, BF16), linear(ckv, w_uv, BF16), kr_raw


def _gla_mixer(x, wl, j, ops):
    w_in3, w_a2 = wl['gla_w_in'][j], wl['gla_w_a2'][j]
    w_o = _rows_full(wl['gla_w_o'][j])
    w_in = jnp.transpose(w_in3, (1, 0, 2)).reshape(w_in3.shape[1], N_DEV * w_in3.shape[2])
    heads = w_o.shape[0] // GLA_DV
    n_main = 2 * heads * GLA_DK + 2 * heads * GLA_DV
    w_a = jnp.pad(w_in[:, n_main:], ((0, 0), (0, LANES - GLA_RANK)))
    w_a2_p = jnp.pad(w_a2, ((0, LANES - GLA_RANK), (0, 0))).astype(BF16)
    hm = linear(x, w_in[:, :n_main], BF16)
    ha = linear(x, w_a, BF16)
    (la,) = ops['gla_gate'](linear(ha, w_a2_p, F32), wl['gla_b_a'][j])
    return linear(gla_core(hm, la, wl['gla_o_norm'][j], heads), w_o, F32)


def _ffn_in(x, m, wl, i, ops, bp):
    (x1,) = ops['ln_res'](x, m, wl['ln1_g'][i], wl['ln1_b'][i])
    cw, cb = wl['ffn_conv_w'][i], wl['ffn_conv_b'][i]
    bu = cw.shape[1] // N_DEV
    cwp = jnp.pad(cw.reshape(CONV_W, N_DEV, bu), ((0, 0), (0, 0), (0, bp - bu))).reshape(CONV_W, N_DEV * bp)
    cbp = jnp.pad(cb.reshape(1, N_DEV, bu), ((0, 0), (0, 0), (0, bp - bu))).reshape(1, N_DEV * bp)
    return x1, cwp, cbp


def _ffn_out(x1, a, wl, p_i, i, ops):
    w_down3 = wl['ffn_w_down'][i]
    half, bu, d_model = N_DEV // 2, 2 * w_down3.shape[1], w_down3.shape[2]
    bp = a.shape[1] // half
    w_down = jnp.pad(w_down3.reshape(half, bu, d_model), ((0, 0), (0, bp - bu), (0, 0))).reshape(half * bp, d_model)
    f = linear(a, w_down, F32)
    (x2,) = ops['ln_res'](x1, f, wl['ln2_g'][i], wl['ln2_b'][i])
    glog = linear(x2, _rows_full(wl['ple_w_gate'][i]), F32)
    pp = linear(p_i, wl['ple_w_proj'][i], F32)
    (x,) = ops['ple'](x2, glog, pp, wl['ple_b_gate'][i])
    return x


def _layer(x, wl, p_i, aux, i, ops):
    j = i // 2
    if i % 2 == 0:
        q_raw, kn, v, kr_raw = _mla_pre(x, wl, j)
        o = make_attention(aux, _mla_heads(wl, j))[0](q_raw, kn, v, kr_raw)
        m = linear(o, _rows_full(wl['mla_w_o'][j]), F32)
    else:
        m = _gla_mixer(x, wl, j, ops)
    x1, cwp, cbp = _ffn_in(x, m, wl, i, ops, wl['ffn_w_up'][i].shape[2])
    return _ffn_out(x1, ffn_hidden(x1, wl['ffn_w_up'][i], cwp, cbp), wl, p_i, i, ops)


def kernel(x, p, positions, mla_w_in, mla_q_norm, mla_kv_norm, mla_w_uq, mla_w_uk, mla_w_uv, mla_w_o, gla_w_in, gla_w_a2, gla_b_a, gla_o_norm, gla_w_o, ln1_g, ln1_b, ln2_g, ln2_b, ffn_w_up, ffn_conv_w, ffn_conv_b, ffn_w_down, ple_w_proj, ple_w_gate, ple_b_gate, loss_target, m_mla_w_in, m_mla_q_norm, m_mla_kv_norm, m_mla_w_uq, m_mla_w_uk, m_mla_w_uv, m_mla_w_o, m_gla_w_in, m_gla_w_a2, m_gla_b_a, m_gla_o_norm, m_gla_w_o, m_ln1_g, m_ln1_b, m_ln2_g, m_ln2_b, m_ffn_w_up, m_ffn_conv_w, m_ffn_conv_b, m_ffn_w_down, m_ple_w_proj, m_ple_w_gate, m_ple_b_gate, v_mla_w_in, v_mla_q_norm, v_mla_kv_norm, v_mla_w_uq, v_mla_w_uk, v_mla_w_uv, v_mla_w_o, v_gla_w_in, v_gla_w_a2, v_gla_b_a, v_gla_o_norm, v_gla_w_o, v_ln1_g, v_ln1_b, v_ln2_g, v_ln2_b, v_ffn_w_up, v_ffn_conv_w, v_ffn_conv_b, v_ffn_w_down, v_ple_w_proj, v_ple_w_gate, v_ple_b_gate):
    w = dict(zip(WEIGHTS, (mla_w_in, mla_q_norm, mla_kv_norm, mla_w_uq, mla_w_uk, mla_w_uv, mla_w_o, gla_w_in, gla_w_a2,
                           gla_b_a, gla_o_norm, gla_w_o, ln1_g, ln1_b, ln2_g, ln2_b, ffn_w_up, ffn_conv_w, ffn_conv_b,
                           ffn_w_down, ple_w_proj, ple_w_gate, ple_b_gate)))
    m_in = dict(zip(WEIGHTS, (m_mla_w_in, m_mla_q_norm, m_mla_kv_norm, m_mla_w_uq, m_mla_w_uk, m_mla_w_uv, m_mla_w_o,
                              m_gla_w_in, m_gla_w_a2, m_gla_b_a, m_gla_o_norm, m_gla_w_o, m_ln1_g, m_ln1_b, m_ln2_g,
                              m_ln2_b, m_ffn_w_up, m_ffn_conv_w, m_ffn_conv_b, m_ffn_w_down, m_ple_w_proj, m_ple_w_gate,
                              m_ple_b_gate)))
    v_in = dict(zip(WEIGHTS, (v_mla_w_in, v_mla_q_norm, v_mla_kv_norm, v_mla_w_uq, v_mla_w_uk, v_mla_w_uv, v_mla_w_o,
                              v_gla_w_in, v_gla_w_a2, v_gla_b_a, v_gla_o_norm, v_gla_w_o, v_ln1_g, v_ln1_b, v_ln2_g,
                              v_ln2_b, v_ffn_w_up, v_ffn_conv_w, v_ffn_conv_b, v_ffn_w_down, v_ple_w_proj, v_ple_w_gate,
                              v_ple_b_gate)))
    _uid[0] = itertools.count()
    x2d, target, pos = x[0], loss_target[0], positions[0]
    p3 = p[:, 0]
    dims = {'h_width': mla_w_uq.shape[1] + mla_w_uk.shape[1] + LANES}

    depth = ln1_g.shape[0]
    ops = _ops(depth)
    cid = pos // CHUNK
    aux = {'rope': _rope_tables(pos), 'cidq': cid[:, None], 'cidk': cid[None, :],
           'need': _mask_table(cid, _attn_tile(pos.shape[0]))}

    in_layer0 = set(MLA_PRE_W + FFN_IN_W + FFN_OUT_W + ['ffn_w_up'])
    prepped = {n: _prep_big(n, w[n], dims) for n in BIG}
    first_names = [n for n in BIG if n in in_layer0]
    rest_names = [n for n in BIG if prepped[n].shape[0] > (1 if n in in_layer0 else 0)]
    rest_from = {n: (1 if n in in_layer0 else 0) for n in rest_names}
    small3 = [_as3(w[n]) for n in SMALL]
    small_packed, small_sizes = _pack([s.reshape(1, -1) for s in small3], F32, 8)
    first = all_gather([prepped[n][:1] for n in first_names] + [small_packed])
    wl = {n: [None] * prepped[n].shape[0] for n in BIG}
    for n, g in zip(first_names, first):
        wl[n][0] = g[0]
    for n, s3, flat in zip(SMALL, small3, _unpack(first[-1][0], small_sizes)):
        wl[n] = _unshard(flat.reshape(N_DEV, *s3.shape), SHARD_AXIS[n] if w[n].ndim == 3 else 2)
    for n in REPL:
        wl[n] = [w[n][l][None, :] for l in range(w[n].shape[0])]

    def pick(names, layer0):
        return {n: [wl[n][l] if (l == 0 and n in in_layer0) == layer0 else None for l in range(len(wl[n]))] for n in names}

    heads0 = _mla_heads(wl, 0)
    _, attn_run, attn_bwd = make_attention(aux, heads0)
    pre, vjp_pre = jax.vjp(lambda x_, wl_: _mla_pre(x_, wl_, 0), x2d, pick(MLA_PRE_W, True))
    o, attn_res, partial = attn_run(*pre, comm_gather_own([prepped[n][rest_from[n]:] for n in rest_names]))
    w_up0 = wl['ffn_w_up'][0]

    def ffn_in(x_, o_, wl_):
        return _ffn_in(x_, linear(o_, _rows_full(wl_['mla_w_o'][0]), F32), wl_, 0, ops, w_up0.shape[2])

    (x1, cwp, cbp), vjp_in = jax.vjp(ffn_in, x2d, o, pick(FFN_IN_W, True))
    a, ffn_res, rest = _ffn_hidden_fwd(x1, w_up0, cwp, cbp, comm_gather_pass(partial))
    for n, g in zip(rest_names, rest):
        for l in range(g.shape[0]):
            wl[n][rest_from[n] + l] = g[l]
    x_l0, vjp_out = jax.vjp(lambda x1_, a_, wl_: _ffn_out(x1_, a_, wl_, p3[0], 0, ops), x1, a, pick(FFN_OUT_W, True))

    def tail(x_, wl_):
        for i in range(1, depth):
            x_ = _layer(x_, wl_, p3[i], aux, i, ops)
        return x_

    y, vjp_tail = jax.vjp(tail, x_l0, pick(WEIGHTS, False))
    sq, dy = _loss_call(y, target)

    dwl = {n: [None] * len(wl[n]) for n in WEIGHTS}

    def keep(part):
        for n, per_layer in part.items():
            for l, g in enumerate(per_layer):
                if g is not None:
                    dwl[n][l] = g

    x_c, y_c, c_place = _place()
    c_idx = jnp.reshape(c_place, (1,)).astype(jnp.int32)
    chip_idx = jnp.reshape(2 * x_c + y_c, (1,)).astype(jnp.int32)
    dx_l0, d_tail = vjp_tail(dy)
    keep(d_tail)
    g_rest = [jnp.stack(dwl[n][rest_from[n]:], axis=0) for n in rest_names]
    dx1_out, da, d_out = vjp_out(dx_l0)
    keep(d_out)
    (dx1_ffn, dw_up0, dcwp, dcbp), recv1 = _ffn_hidden_bwd(ffn_res, da, comm_rs_pair(g_rest))
    dwl['ffn_w_up'][0] = dw_up0
    p1 = [_rs_pair_add(g, r, c_idx) for g, r in zip(g_rest, recv1)]
    dx_in, do, d_in = vjp_in((dx1_out + dx1_ffn, dcwp, dcbp))
    keep(d_in)
    d_pre_in, recv2 = attn_bwd(attn_res, do, comm_rs_chip(p1))
    red_rest = [_rs_final_add(p_, r, chip_idx) for p_, r in zip(p1, recv2)]
    dx_pre, d_pre = vjp_pre(d_pre_in)
    keep(d_pre)
    dx = dx_pre + dx_in

    small_blocks = [_reshard(dwl[n], SHARD_AXIS[n] if w[n].ndim == 3 else 2).reshape(N_DEV, -1) for n in SMALL]
    small_grad_packed, _ = _pack(small_blocks, F32, 8)
    red_first = reduce_scatter([dwl[n][0][None] for n in first_names] + [small_grad_packed[None]])
    by_layer = {n: [] for n in BIG}
    for n, g in zip(first_names, red_first):
        by_layer[n].append(g)
    for n, g in zip(rest_names, red_rest):
        by_layer[n].append(g)
    grads = {n: _unprep_big(n, jnp.concatenate(by_layer[n], axis=0), w[n].shape) for n in BIG}
    for n, f in zip(SMALL, _unpack(red_first[-1][0], small_sizes)):
        grads[n] = f.reshape(w[n].shape)

    repl_flat = [jnp.concatenate([g.reshape(-1) for g in dwl[n]]).reshape(1, -1) for n in REPL]
    loss_part = 0.5 * jnp.sum(sq) / sq.shape[1]
    packed, repl_sizes = _pack(repl_flat + [loss_part.reshape(1, 1)], F32, 8)
    summed = _unpack(all_reduce_small(packed[0])[None], repl_sizes)
    for n, f in zip(REPL, summed[:-1]):
        grads[n] = f.reshape(w[n].shape)
    loss = summed[-1].reshape(())

    delta, new_m, new_v = {}, {}, {}
    for n in WEIGHTS:
        delta[n], new_m[n], new_v[n] = _adamw_call(w[n], grads[n], m_in[n], v_in[n])
    return (loss, dx[None], *[grads[n] for n in WEIGHTS], *[delta[n] for n in WEIGHTS],
            *[new_m[n] for n in WEIGHTS], *[new_v[n] for n in WEIGHTS])
```

```python
import functools
import itertools

import jax
import jax.numpy as jnp
from jax import lax
from jax.experimental import pallas as pl
from jax.experimental.pallas import tpu as pltpu

F32 = jnp.float32
BF16 = jnp.bfloat16
MESH = pl.DeviceIdType.MESH
N_DEV = 8

EPS = 1e-5
NEG_INF = -1e30
CHUNK = 64
Q_BLOCK = 128
MLA_NOPE = 128
MLA_ROPE = 64
MLA_V = 128
MLA_QK_PAD = 256
ROPE_THETA = 10000.0
GLA_DK = 128
GLA_DV = 256
GLA_RANK = 16
GLA_TAU = 16.0
CONV_W = 3
ADAM_LR = 0.001
ADAM_B1 = 0.9
ADAM_B2 = 0.999
ADAM_EPS = 1e-08
ADAM_WD = 0.01
ADAM_STEP = 10
LOG2E = 1.4426950408889634

LANES = 128
PACK_COLS = 1024
VMEM_LIMIT = 48 * 1024 * 1024
MM_VMEM_BUDGET = 30 * 1024 * 1024

WEIGHTS = ['mla_w_in', 'mla_q_norm', 'mla_kv_norm', 'mla_w_uq', 'mla_w_uk', 'mla_w_uv', 'mla_w_o', 'gla_w_in',
           'gla_w_a2', 'gla_b_a', 'gla_o_norm', 'gla_w_o', 'ln1_g', 'ln1_b', 'ln2_g', 'ln2_b', 'ffn_w_up',
           'ffn_conv_w', 'ffn_conv_b', 'ffn_w_down', 'ple_w_proj', 'ple_w_gate', 'ple_b_gate']
SHARD_AXIS = {'mla_w_in': 1, 'mla_q_norm': None, 'mla_kv_norm': None, 'mla_w_uq': 2, 'mla_w_uk': 2, 'mla_w_uv': 2,
              'mla_w_o': 1, 'gla_w_in': 2, 'gla_w_a2': 2, 'gla_b_a': 1, 'gla_o_norm': 1, 'gla_w_o': 1,
              'ln1_g': None, 'ln1_b': None, 'ln2_g': None, 'ln2_b': None, 'ffn_w_up': 2, 'ffn_conv_w': 2,
              'ffn_conv_b': None, 'ffn_w_down': 1, 'ple_w_proj': 2, 'ple_w_gate': 1, 'ple_b_gate': None}
BIG = ['mla_w_in', 'mla_w_uq', 'mla_w_uk', 'mla_w_uv', 'mla_w_o', 'gla_w_in', 'gla_w_o', 'ffn_w_up', 'ffn_w_down',
       'ple_w_proj', 'ple_w_gate']
SMALL = ['gla_w_a2', 'gla_b_a', 'gla_o_norm', 'ffn_conv_w']
REPL = [n for n in WEIGHTS if SHARD_AXIS[n] is None]

_uid = [itertools.count()]


def _nm(base):
    return f"{base}_{next(_uid[0])}"


def _cp(sem=None):
    return pltpu.CompilerParams(dimension_semantics=sem, vmem_limit_bytes=VMEM_LIMIT)


def _round_up(n, m):
    return -(-n // m) * m


class Comm:
    def __init__(self, inputs, out_shapes, aliases, n_send, n_recv, n_local, build):
        self.inputs, self.out_shapes, self.aliases = list(inputs), list(out_shapes), dict(aliases)
        self.n_send, self.n_recv, self.n_local, self.build = n_send, n_recv, n_local, build


def _pcall(body, name, grid, n_prefetch, in_specs, out_specs, out_shape, scratch, sem, args, comm=None):
    in_specs, out_specs, out_shape, scratch, args = list(in_specs), list(out_specs), list(out_shape), list(scratch), list(args)
    n_in, n_out, n_scr = len(in_specs), len(out_specs), len(scratch)
    aliases = {}
    kernel_body = body
    if comm is not None:
        ci, co = len(comm.inputs), len(comm.out_shapes)
        any_spec = pl.BlockSpec(memory_space=pl.ANY)
        in_specs += [any_spec] * ci
        out_specs += [any_spec] * co
        out_shape += comm.out_shapes
        scratch += [pltpu.SemaphoreType.DMA((comm.n_send,)), pltpu.SemaphoreType.DMA((comm.n_recv,)),
                    pltpu.SemaphoreType.DMA((comm.n_local,))]
        aliases = {n_prefetch + n_in + k: n_out + v for k, v in comm.aliases.items()}
        args += comm.inputs
        sem = ("arbitrary",) * len(grid)

        def kernel_body(*refs):
            pre, r = refs[:n_prefetch], refs[n_prefetch:]
            ins, cin = r[:n_in], r[n_in:n_in + ci]
            outs, cout = r[n_in + ci:n_in + ci + n_out], r[n_in + ci + n_out:n_in + ci + n_out + co]
            scr = r[n_in + ci + n_out + co:n_in + ci + n_out + co + n_scr]
            send_sems, recv_sems, local_sems = r[-3:]
            first = functools.reduce(lambda a, b: a & b, [pl.program_id(d) == 0 for d in range(len(grid))])
            last = functools.reduce(lambda a, b: a & b, [pl.program_id(d) == grid[d] - 1 for d in range(len(grid))])
            starts, waits = comm.build(cin, cout, send_sems, recv_sems, local_sems)

            @pl.when(first)
            def _():
                for cp in starts:
                    cp.start()

            body(*pre, *ins, *outs, *scr)

            @pl.when(last)
            def _():
                for wait in waits:
                    wait()

    grid_spec = pltpu.PrefetchScalarGridSpec(num_scalar_prefetch=n_prefetch, grid=grid, in_specs=in_specs,
                                             out_specs=out_specs, scratch_shapes=scratch)
    return pl.pallas_call(kernel_body, name=_nm(name), grid_spec=grid_spec, out_shape=tuple(out_shape),
                          input_output_aliases=aliases, compiler_params=_cp(sem))(*args)


def _run_comm(comm, name):
    return _pcall(lambda: None, name, (1,), 0, [], [], [], [], ("arbitrary",), [], comm)


def _divisor_tiles(n, cap):
    if n % LANES:
        return [n]
    out = [t for t in range(LANES, min(n, cap) + 1, LANES) if n % t == 0]
    return out or [n]


def _mm_tiles(M, N, K, abytes, bbytes, obytes, tn_fixed=None, tk_fixed=None):
    best = None
    for tm in _divisor_tiles(M, 1024):
        for tn in ([tn_fixed] if tn_fixed else _divisor_tiles(N, 1536)):
            for tk in ([tk_fixed] if tk_fixed else _divisor_tiles(K, 2048)):
                vmem = 2 * (tm * tk * abytes + tk * tn * bbytes + tm * tn * obytes) + tm * tn * 4
                if vmem > MM_VMEM_BUDGET:
                    continue
                key = (tm * tn * tk, tk)
                if best is None or key > best[0]:
                    best = (key, (tm, tn, tk))
    assert best is not None, (M, N, K)
    return best[1]


def _mm(a, b, mode, out_dtype, base="mm", blocks=None):
    blk0, nblk = blocks if blocks else (0, 1)
    tn_fixed = tk_fixed = None
    if mode == "nn":
        M, K = a.shape
        N = nblk * b.shape[2] if blocks else b.shape[1]
        tn_fixed = b.shape[2] if blocks else None
    elif mode == "nt":
        M, K = a.shape
        N = b.shape[1] if blocks else b.shape[0]
        tk_fixed = b.shape[2] if blocks else None
        assert not blocks or K == nblk * b.shape[2]
    else:
        (K, M), N = a.shape, b.shape[1]
        tn_fixed = N // nblk if blocks else None
    tm, tn, tk = _mm_tiles(M, N, K, a.dtype.itemsize, b.dtype.itemsize, jnp.dtype(out_dtype).itemsize, tn_fixed, tk_fixed)
    nk = K // tk
    out_shape = jax.ShapeDtypeStruct((M, N), out_dtype)
    out_spec = pl.BlockSpec((tm, tn), lambda i, j, k: (i, j))
    if mode == "nn":
        a_spec = pl.BlockSpec((tm, tk), lambda i, j, k: (i, k))
        b_spec = (pl.BlockSpec((None, tk, tn), lambda i, j, k: (blk0 + j, k, 0)) if blocks
                  else pl.BlockSpec((tk, tn), lambda i, j, k: (k, j)))
        dims = (((1,), (0,)), ((), ()))
    elif mode == "nt":
        a_spec = pl.BlockSpec((tm, tk), lambda i, j, k: (i, k))
        b_spec = (pl.BlockSpec((None, tn, tk), lambda i, j, k: (blk0 + k, j, 0)) if blocks
                  else pl.BlockSpec((tn, tk), lambda i, j, k: (j, k)))
        dims = (((1,), (1,)), ((), ()))
    else:
        a_spec = pl.BlockSpec((tk, tm), lambda i, j, k: (k, i))
        b_spec = pl.BlockSpec((tk, tn), lambda i, j, k: (k, j))
        dims = (((0,), (0,)), ((), ()))
        if blocks:
            out_shape = jax.ShapeDtypeStruct((nblk, M, tn), out_dtype)
            out_spec = pl.BlockSpec((None, tm, tn), lambda i, j, k: (j, i, 0))

    def body(a_ref, b_ref, o_ref, acc_ref):
        part = lax.dot_general(a_ref[...].astype(BF16), b_ref[...].astype(BF16), dims, preferred_element_type=F32)
        if nk == 1:
            o_ref[...] = part.astype(o_ref.dtype)
        else:
            k = pl.program_id(2)

            @pl.when(k == 0)
            def _():
                acc_ref[...] = part

            @pl.when(k > 0)
            def _():
                acc_ref[...] += part

            @pl.when(k == nk - 1)
            def _():
                o_ref[...] = acc_ref[...].astype(o_ref.dtype)

    return pl.pallas_call(
        body, name=_nm(base), grid=(M // tm, N // tn, nk), out_shape=out_shape,
        in_specs=[a_spec, b_spec], out_specs=out_spec,
        scratch_shapes=[pltpu.VMEM((tm, tn) if nk > 1 else (8, LANES), F32)],
        compiler_params=_cp(("parallel", "parallel", "arbitrary")),
    )(a, b)


def _all_blocks(w):
    return (0, w.shape[0]) if w.ndim == 3 else None


@functools.partial(jax.custom_vjp, nondiff_argnums=(2,))
def linear(a, w, out_dtype):
    return _mm(a, w, "nn", out_dtype, "lin_fwd", _all_blocks(w))


def _linear_fwd(a, w, out_dtype):
    return _mm(a, w, "nn", out_dtype, "lin_fwd", _all_blocks(w)), (a, w)


def _linear_bwd(out_dtype, res, dy):
    a, w = res
    return (_mm(dy, w, "nt", a.dtype, "lin_dx", _all_blocks(w)), _mm(a, dy, "tn", w.dtype, "lin_dw", _all_blocks(w)))


linear.defvjp(_linear_fwd, _linear_bwd)


def _row_tile(S, width):
    tr = 512 if width <= 1024 else 256
    return min(tr, S)


def _rw_fwd(f, rows, params, out_dtypes, base):
    S = rows[0][0].shape[0]
    tr = _row_tile(S, max(w for _, w, _ in rows))
    n_in = len(rows) + len(params)
    avals = [jax.ShapeDtypeStruct((tr, w), F32) for _, w, _ in rows] + [jax.ShapeDtypeStruct(p.shape, F32) for p in params]
    outs = jax.eval_shape(f, *avals)

    def body(*refs):
        vals = [r[...].astype(F32) for r in refs[:n_in]]
        for o_ref, r in zip(refs[n_in:], f(*vals)):
            o_ref[...] = r.astype(o_ref.dtype)

    in_specs = [pl.BlockSpec((tr, w), functools.partial(lambda i, cb: (i, cb), cb=cb)) for _, w, cb in rows]
    in_specs += [pl.BlockSpec(p.shape, lambda i: (0, 0)) for p in params]
    return pl.pallas_call(
        body, name=_nm(base), grid=(S // tr,),
        out_shape=tuple(jax.ShapeDtypeStruct((S, o.shape[1]), dt) for o, dt in zip(outs, out_dtypes)),
        in_specs=in_specs, out_specs=tuple(pl.BlockSpec((tr, o.shape[1]), lambda i: (i, 0)) for o in outs),
        compiler_params=_cp(("parallel",)),
    )(*[a for a, _, _ in rows], *params)


def _rw_bwd(f, rows, params, cts, row_grad_dtypes, base):
    S = rows[0][0].shape[0]
    tr = _row_tile(S, max(w for _, w, _ in rows))
    n_rows, n_par, n_ct = len(rows), len(params), len(cts)
    want = [k for k, dt in enumerate(row_grad_dtypes) if dt is not None]

    def body(*refs):
        in_refs = refs[:n_rows + n_par]
        ct_refs = refs[n_rows + n_par:n_rows + n_par + n_ct]
        out_refs = refs[n_rows + n_par + n_ct:]
        vals = [r[...].astype(F32) for r in in_refs]
        _, vjp_fn = jax.vjp(f, *vals)
        grads = vjp_fn(tuple(c[...].astype(F32) for c in ct_refs))
        for o_ref, k in zip(out_refs[:len(want)], want):
            o_ref[...] = grads[k].astype(o_ref.dtype)
        i = pl.program_id(0)
        for o_ref, g in zip(out_refs[len(want):], grads[n_rows:]):
            @pl.when(i == 0)
            def _(o_ref=o_ref, g=g):
                o_ref[...] = g

            @pl.when(i > 0)
            def _(o_ref=o_ref, g=g):
                o_ref[...] += g

    in_specs = [pl.BlockSpec((tr, w), functools.partial(lambda i, cb: (i, cb), cb=cb)) for _, w, cb in rows]
    in_specs += [pl.BlockSpec(p.shape, lambda i: (0, 0)) for p in params]
    in_specs += [pl.BlockSpec((tr, c.shape[1]), lambda i: (i, 0)) for c in cts]
    out_shape = [jax.ShapeDtypeStruct((S, rows[k][1]), row_grad_dtypes[k]) for k in want]
    out_specs = [pl.BlockSpec((tr, rows[k][1]), lambda i: (i, 0)) for k in want]
    out_shape += [jax.ShapeDtypeStruct(p.shape, F32) for p in params]
    out_specs += [pl.BlockSpec(p.shape, lambda i: (0, 0)) for p in params]
    res = pl.pallas_call(
        body, name=_nm(base), grid=(S // tr,), out_shape=tuple(out_shape),
        in_specs=in_specs, out_specs=tuple(out_specs), compiler_params=_cp(("arbitrary",)),
    )(*[a for a, _, _ in rows], *params, *cts)
    row_grads = [None] * n_rows
    for k, g in zip(want, res[:len(want)]):
        row_grads[k] = g
    return row_grads, list(res[len(want):])


def rw_op(f, base, n_rows, out_dtypes):
    @jax.custom_vjp
    def op(*args):
        return fwd(*args)[0]

    def split(args):
        rows = [(a, a.shape[1], 0) for a in args[:n_rows]]
        return rows, list(args[n_rows:])

    def fwd(*args):
        rows, params = split(args)
        return tuple(_rw_fwd(f, rows, params, out_dtypes, base + "_fwd")), args

    def bwd(args, cts):
        rows, params = split(args)
        rg, pg = _rw_bwd(f, rows, params, list(cts), [a.dtype for a, _, _ in rows], base + "_bwd")
        return tuple(rg) + tuple(g.astype(p.dtype) for g, p in zip(pg, params))

    op.defvjp(fwd, bwd)
    return op


def _ln_res_fn(alpha):
    def f(x, m, g, b):
        z = alpha * x + m
        mu = jnp.mean(z, -1, keepdims=True)
        zc = z - mu
        var = jnp.mean(zc * zc, -1, keepdims=True)
        return (zc * lax.rsqrt(var + EPS) * g + b,)
    return f


def _rms(x, g):
    return x * lax.rsqrt(jnp.mean(x * x, -1, keepdims=True) + EPS) * g


def _mla_norm_fn(q_lora, kv_lora):
    def f(h, qn, kvn):
        return (_rms(h[:, :q_lora], qn), _rms(h[:, q_lora:q_lora + kv_lora], kvn),
                h[:, q_lora + kv_lora:q_lora + kv_lora + LANES])
    return f


def _log_sigmoid(z):
    return jnp.minimum(z, 0.0) - jnp.log(1.0 + jnp.exp(-jnp.abs(z)))


def _gla_gate_fn(z, b):
    return (_log_sigmoid(z + b) / GLA_TAU,)


def _ple_fn(x, glog, pp, b):
    return (x + jax.nn.sigmoid(glog + b) * pp,)


def _gla_out_fn(heads):
    def f(o, r, g):
        parts = []
        for h in range(heads):
            oh = o[:, h * GLA_DV:(h + 1) * GLA_DV]
            mu = jnp.mean(oh, -1, keepdims=True)
            oc = oh - mu
            var = jnp.mean(oc * oc, -1, keepdims=True)
            parts.append(oc * lax.rsqrt(var + EPS) * g[:, h * GLA_DV:(h + 1) * GLA_DV])
        return (jnp.concatenate(parts, axis=1) * (r * jax.nn.sigmoid(r)),)
    return f


def _rope_call(x, tabs, roped, out_dtype, base, fold=False):
    S, C = x.shape
    nb = C // LANES
    tr = min(512 if C <= 1024 else 256, S)
    out_c = LANES if fold else C

    def rot(v, a, b1, b2):
        return v * a + pltpu.roll(v, 96, 1) * b1 + pltpu.roll(v, 32, 1) * b2

    def body(x_ref, a_ref, b1_ref, b2_ref, o_ref):
        a, b1, b2 = a_ref[...], b1_ref[...], b2_ref[...]
        if fold:
            v = x_ref[:, 0:LANES].astype(F32)
            for blk in range(1, nb):
                v = v + x_ref[:, blk * LANES:(blk + 1) * LANES].astype(F32)
            o_ref[...] = rot(v, a, b1, b2).astype(o_ref.dtype)
            return
        for blk in range(nb):
            v = x_ref[:, blk * LANES:(blk + 1) * LANES].astype(F32)
            if roped(blk):
                v = rot(v, a, b1, b2)
            o_ref[:, blk * LANES:(blk + 1) * LANES] = v.astype(o_ref.dtype)

    row = lambda w: pl.BlockSpec((tr, w), lambda i: (i, 0))
    return pl.pallas_call(
        body, name=_nm(base), grid=(S // tr,), out_shape=jax.ShapeDtypeStruct((S, out_c), out_dtype),
        in_specs=[row(C), row(LANES), row(LANES), row(LANES)], out_specs=row(out_c),
        compiler_params=_cp(("parallel",)),
    )(x, *tabs)


def _attn_tile(S):
    return min(512, S)


def _tri_schedule(n, by_key):
    pairs = ([(i, j) for j in range(n) for i in range(j, n)] if by_key
             else [(i, j) for i in range(n) for j in range(i + 1)])
    return (jnp.asarray([p[0] for p in pairs], jnp.int32), jnp.asarray([p[1] for p in pairs], jnp.int32))


def _mask_table(cid, t):
    n = cid.shape[0] // t
    blocks = cid.reshape(n, t)
    cmin_q, cmax_k = jnp.min(blocks, axis=1), jnp.max(blocks, axis=1)
    need = (cmax_k[None, :] > cmin_q[:, None]) | jnp.eye(n, dtype=bool)
    return need.astype(jnp.int32).reshape(n * n)


def _attn_mask(cidq_ref, cidk_ref, i, j, t):
    qrow = i * t + lax.broadcasted_iota(jnp.int32, (t, 1), 0)
    kcol = j * t + lax.broadcasted_iota(jnp.int32, (1, t), 1)
    qlim = (qrow // Q_BLOCK + 1) * Q_BLOCK
    return (cidk_ref[...] <= cidq_ref[...]) & (kcol < qlim)


ATTN_FWD_HEADS = 4
ATTN_BWD_HEADS = 2


def _attn_fwd_call(q, kn, v, kr, aux, heads, comm=None):
    S = q.shape[0]
    t = _attn_tile(S)
    n = S // t
    hp = ATTN_FWD_HEADS if heads % ATTN_FWD_HEADS == 0 else 1
    qi_tab, kj_tab = _tri_schedule(n, False)
    scale2 = (MLA_NOPE + MLA_ROPE) ** -0.5 * LOG2E

    def body(qi_ref, kj_ref, need_ref, q_ref, kn_ref, v_ref, kr_ref, cidq_ref, cidk_ref, o_ref, lse_ref,
             m_ref, l_ref, acc_ref):
        st = pl.program_id(1)
        i, j = qi_ref[st], kj_ref[st]

        @pl.when(j == 0)
        def _():
            m_ref[...] = jnp.full(m_ref.shape, NEG_INF, F32)
            l_ref[...] = jnp.zeros(l_ref.shape, F32)
            acc_ref[...] = jnp.zeros(acc_ref.shape, F32)

        def update(masked):
            mask = _attn_mask(cidq_ref, cidk_ref, i, j, t) if masked else None
            for hh in range(hp):
                lanes = slice(hh * LANES, (hh + 1) * LANES)
                k = jnp.concatenate([kn_ref[:, lanes], kr_ref[...]], axis=1)
                qh = q_ref[:, hh * MLA_QK_PAD:(hh + 1) * MLA_QK_PAD]
                s = lax.dot_general(qh, k, (((1,), (1,)), ((), ())), preferred_element_type=F32) * scale2
                if masked:
                    s = jnp.where(mask, s, NEG_INF)
                m_prev = m_ref[:, lanes]
                m_new = jnp.maximum(m_prev, jnp.max(s, axis=1, keepdims=True))
                alpha = jnp.exp2(m_prev - m_new)
                p = jnp.exp2(s - m_new[:, :1])
                l_ref[:, lanes] = alpha * l_ref[:, lanes] + jnp.sum(p, axis=1, keepdims=True)
                acc_ref[:, lanes] = alpha * acc_ref[:, lanes] + jnp.dot(p.astype(BF16), v_ref[:, lanes],
                                                                        preferred_element_type=F32)
                m_ref[:, lanes] = m_new

        need = need_ref[i * n + j]

        @pl.when(need != 0)
        def _():
            update(True)

        @pl.when(need == 0)
        def _():
            update(False)

        @pl.when(j == i)
        def _():
            o_ref[...] = (acc_ref[...] / l_ref[...]).astype(o_ref.dtype)
            lse_ref[...] = m_ref[...] + jnp.log(l_ref[...]) * LOG2E

    qmap = lambda h, s, qi, kj, need: (qi[s], h)
    kmap = lambda h, s, qi, kj, need: (kj[s], h)
    return _pcall(
        body, "attn_fwd", (heads // hp, qi_tab.shape[0]), 3,
        in_specs=[pl.BlockSpec((t, hp * MLA_QK_PAD), qmap), pl.BlockSpec((t, hp * MLA_NOPE), kmap),
                  pl.BlockSpec((t, hp * MLA_V), kmap),
                  pl.BlockSpec((t, LANES), lambda h, s, qi, kj, need: (kj[s], 0)),
                  pl.BlockSpec((t, 1), lambda h, s, qi, kj, need: (qi[s], 0)),
                  pl.BlockSpec((1, t), lambda h, s, qi, kj, need: (0, kj[s]))],
        out_specs=[pl.BlockSpec((t, hp * MLA_V), qmap), pl.BlockSpec((t, hp * LANES), qmap)],
        out_shape=[jax.ShapeDtypeStruct((S, heads * MLA_V), BF16), jax.ShapeDtypeStruct((S, heads * LANES), F32)],
        scratch=[pltpu.VMEM((t, hp * LANES), F32), pltpu.VMEM((t, hp * LANES), F32), pltpu.VMEM((t, hp * MLA_V), F32)],
        sem=("parallel", "arbitrary"),
        args=[qi_tab, kj_tab, aux['need'], q, kn, v, kr, aux['cidq'], aux['cidk']], comm=comm)


def _attn_bwd_call(q, kn, v, kr, o, lse, do, aux, heads, comm=None):
    S = q.shape[0]
    t = _attn_tile(S)
    n = S // t
    qi_tab, kj_tab = _tri_schedule(n, True)
    scale = (MLA_NOPE + MLA_ROPE) ** -0.5
    scale2 = scale * LOG2E
    nt_dims = (((1,), (1,)), ((), ()))
    tn_dims = (((0,), (0,)), ((), ()))

    n_steps = n * (n + 1) // 2
    hp = ATTN_BWD_HEADS if heads % ATTN_BWD_HEADS == 0 else 1

    def body(qi_ref, kj_ref, need_ref, q_ref, kn_ref, v_ref, kr_ref, cidq_ref, cidk_ref, o_ref, lse_ref, do_ref,
             ta_ref, tb1_ref, tb2_ref, dq_ref, dkn_ref, dv_ref, dkr_ref, dk_acc, dv_acc, dq_acc):
        st = pl.program_id(1)
        i, j = qi_ref[st], kj_ref[st]

        @pl.when(st == 0)
        def _():
            dq_acc[...] = jnp.zeros(dq_acc.shape, F32)

        @pl.when(i == j)
        def _():
            dk_acc[...] = jnp.zeros(dk_acc.shape, F32)
            dv_acc[...] = jnp.zeros(dv_acc.shape, F32)

        rows = pl.ds(pl.multiple_of(i * t, t), t)

        def grads(masked):
            mask = _attn_mask(cidq_ref, cidk_ref, i, j, t) if masked else None
            for hh in range(hp):
                lanes = slice(hh * LANES, (hh + 1) * LANES)
                wide = slice(hh * MLA_QK_PAD, (hh + 1) * MLA_QK_PAD)
                k = jnp.concatenate([kn_ref[:, lanes], kr_ref[...]], axis=1)
                qt, do = q_ref[:, wide], do_ref[:, lanes]
                s = lax.dot_general(qt, k, nt_dims, preferred_element_type=F32) * scale2
                if masked:
                    s = jnp.where(mask, s, NEG_INF)
                dp = lax.dot_general(do, v_ref[:, lanes], nt_dims, preferred_element_type=F32)
                dsum = jnp.sum(do.astype(F32) * o_ref[:, lanes].astype(F32), axis=1, keepdims=True)
                p = jnp.exp2(s - lse_ref[:, hh * LANES:hh * LANES + 1])
                ds = (p * (dp - dsum) * scale).astype(BF16)
                dv_acc[:, lanes] += lax.dot_general(p.astype(BF16), do, tn_dims, preferred_element_type=F32)
                dk_acc[:, wide] += lax.dot_general(ds, qt, tn_dims, preferred_element_type=F32)
                dq_acc[rows, wide] += jnp.dot(ds, k, preferred_element_type=F32)

        need = need_ref[i * n + j]

        @pl.when(need != 0)
        def _():
            grads(True)

        @pl.when(need == 0)
        def _():
            grads(False)

        @pl.when(i == n - 1)
        def _():
            for hh in range(hp):
                lanes = slice(hh * LANES, (hh + 1) * LANES)
                off = hh * MLA_QK_PAD
                dkn_ref[:, lanes] = dk_acc[:, off:off + MLA_NOPE].astype(dkn_ref.dtype)
                dkr_ref[:, lanes] = dk_acc[:, off + MLA_NOPE:off + MLA_QK_PAD]
            dv_ref[...] = dv_acc[...].astype(dv_ref.dtype)

        @pl.when(st == n_steps - 1)
        def _():
            for r in range(n):
                rs = slice(r * t, (r + 1) * t)
                for hh in range(hp):
                    off = hh * MLA_QK_PAD
                    dq_ref[rs, off:off + MLA_NOPE] = dq_acc[rs, off:off + MLA_NOPE].astype(dq_ref.dtype)
                    g = dq_acc[rs, off + MLA_NOPE:off + MLA_QK_PAD]
                    g = g * ta_ref[rs, :] + pltpu.roll(g, 96, 1) * tb1_ref[rs, :] + pltpu.roll(g, 32, 1) * tb2_ref[rs, :]
                    dq_ref[rs, off + MLA_NOPE:off + MLA_QK_PAD] = g.astype(dq_ref.dtype)

    qmap = lambda h, s, qi, kj, need: (qi[s], h)
    kmap = lambda h, s, qi, kj, need: (kj[s], h)
    whole = pl.BlockSpec((S, LANES), lambda h, s, qi, kj, need: (0, 0))
    return _pcall(
        body, "attn_bwd", (heads // hp, qi_tab.shape[0]), 3,
        in_specs=[pl.BlockSpec((t, hp * MLA_QK_PAD), qmap), pl.BlockSpec((t, hp * MLA_NOPE), kmap),
                  pl.BlockSpec((t, hp * MLA_V), kmap),
                  pl.BlockSpec((t, LANES), lambda h, s, qi, kj, need: (kj[s], 0)),
                  pl.BlockSpec((t, 1), lambda h, s, qi, kj, need: (qi[s], 0)),
                  pl.BlockSpec((1, t), lambda h, s, qi, kj, need: (0, kj[s])),
                  pl.BlockSpec((t, hp * MLA_V), qmap), pl.BlockSpec((t, hp * LANES), qmap),
                  pl.BlockSpec((t, hp * MLA_V), qmap), whole, whole, whole],
        out_specs=[pl.BlockSpec((S, hp * MLA_QK_PAD), lambda h, s, qi, kj, need: (0, h)),
                   pl.BlockSpec((t, hp * MLA_NOPE), kmap), pl.BlockSpec((t, hp * MLA_V), kmap),
                   pl.BlockSpec((t, hp * LANES), kmap)],
        out_shape=[jax.ShapeDtypeStruct((S, heads * MLA_QK_PAD), BF16), jax.ShapeDtypeStruct((S, heads * MLA_NOPE), BF16),
                   jax.ShapeDtypeStruct((S, heads * MLA_V), BF16), jax.ShapeDtypeStruct((S, heads * LANES), F32)],
        scratch=[pltpu.VMEM((t, hp * MLA_QK_PAD), F32), pltpu.VMEM((t, hp * MLA_V), F32),
                 pltpu.VMEM((S, hp * MLA_QK_PAD), F32)],
        sem=("parallel", "arbitrary"),
        args=[qi_tab, kj_tab, aux['need'], q, kn, v, kr, aux['cidq'], aux['cidk'], o, lse, do, *aux['rope'][1]],
        comm=comm)


def make_attention(aux, heads):
    tabs_f, tabs_b = aux['rope']
    odd, every = (lambda blk: blk % 2 == 1), (lambda blk: True)

    def run_host(q_raw, kn, v, kr_raw, comm=None):
        q = _rope_call(q_raw, tabs_f, odd, BF16, "rope_q")
        kr = _rope_call(kr_raw, tabs_f, every, BF16, "rope_k")
        o, lse, *carried = _attn_fwd_call(q, kn, v, kr, aux, heads, comm)
        return o, (q, kn, v, kr, o, lse), carried

    def bwd_host(res, do, comm=None):
        q, kn, v, kr, o, lse = res
        dq_raw, dkn, dv, dkr, *carried = _attn_bwd_call(q, kn, v, kr, o, lse, do, aux, heads, comm)
        return (dq_raw, dkn, dv, _rope_call(dkr, tabs_b, every, F32, "rope_dk", fold=True)), carried

    @jax.custom_vjp
    def attn(q_raw, kn, v, kr_raw):
        return run_host(q_raw, kn, v, kr_raw)[0]

    attn.defvjp(lambda *a: run_host(*a)[:2], lambda res, do: bwd_host(res, do)[0])
    return attn, run_host, bwd_host


GLA_ROWS = 256


def _tri(lower):
    r = lax.broadcasted_iota(jnp.int32, (CHUNK, CHUNK), 0)
    c = lax.broadcasted_iota(jnp.int32, (CHUNK, CHUNK), 1)
    return jnp.where((c <= r) if lower else (c >= r), 1.0, 0.0).astype(F32)


def _gla_chunk(q_ref, k_ref, v_ref, la_ref, sl):
    la = la_ref[sl, :]
    cum = jnp.dot(_tri(True), la, preferred_element_type=F32, precision=lax.Precision.HIGHEST)
    tot = cum[CHUNK - 1:CHUNK, :]
    e = jnp.exp(tot - cum)
    k = k_ref[sl, :].astype(F32)
    kdec = k * e
    v = v_ref[sl, :]
    upd_t = lax.dot_general(v.astype(BF16), kdec.astype(BF16), (((0,), (0,)), ((), ())), preferred_element_type=F32)
    qs = (q_ref[sl, :].astype(F32) * (GLA_DK ** -0.5)).astype(BF16)
    return e, k, kdec, v, upd_t, jnp.exp(tot), qs


def _gla_specs(heads, rows_map):
    return [pl.BlockSpec((GLA_ROWS, GLA_DK), lambda h, b: (rows_map(b), h)),
            pl.BlockSpec((GLA_ROWS, GLA_DK), lambda h, b: (rows_map(b), heads + h)),
            pl.BlockSpec((GLA_ROWS, GLA_DV), lambda h, b: (rows_map(b), heads + h)),
            pl.BlockSpec((GLA_ROWS, GLA_DK), lambda h, b: (rows_map(b), h))]


def _gla_fwd_call(hm, la, heads):
    S = hm.shape[0]
    assert S % GLA_ROWS == 0
    nb, cpb = S // GLA_ROWS, GLA_ROWS // CHUNK

    def body(q_ref, k_ref, v_ref, la_ref, o_ref, sp_ref, st_ref):
        @pl.when(pl.program_id(1) == 0)
        def _():
            st_ref[...] = jnp.zeros(st_ref.shape, F32)

        for c in range(cpb):
            sl = slice(c * CHUNK, (c + 1) * CHUNK)
            _, _, _, _, upd_t, decay, qs = _gla_chunk(q_ref, k_ref, v_ref, la_ref, sl)
            state = st_ref[...]
            sp_ref[0, c] = state
            state = state * decay + upd_t
            st_ref[...] = state
            o_ref[sl, :] = lax.dot_general(qs, state.astype(BF16), (((1,), (1,)), ((), ())), preferred_element_type=F32)

    return pl.pallas_call(
        body, name=_nm("gla_fwd"), grid=(heads, nb),
        out_shape=(jax.ShapeDtypeStruct((S, heads * GLA_DV), F32),
                   jax.ShapeDtypeStruct((heads, S // CHUNK, GLA_DV, GLA_DK), F32)),
        in_specs=_gla_specs(heads, lambda b: b),
        out_specs=(pl.BlockSpec((GLA_ROWS, GLA_DV), lambda h, b: (b, h)),
                   pl.BlockSpec((1, cpb, GLA_DV, GLA_DK), lambda h, b: (h, b, 0, 0))),
        scratch_shapes=[pltpu.VMEM((GLA_DV, GLA_DK), F32)],
        compiler_params=_cp(("parallel", "arbitrary")),
    )(hm, hm, hm, la)


def _gla_bwd_call(hm, la, sprev, do, heads):
    S = hm.shape[0]
    nb, cpb = S // GLA_ROWS, GLA_ROWS // CHUNK
    scale = GLA_DK ** -0.5

    def body(q_ref, k_ref, v_ref, la_ref, sp_ref, do_ref, dq_ref, dk_ref, dv_ref, dla_ref, carry_ref):
        @pl.when(pl.program_id(1) == 0)
        def _():
            carry_ref[...] = jnp.zeros(carry_ref.shape, F32)

        for c in reversed(range(cpb)):
            sl = slice(c * CHUNK, (c + 1) * CHUNK)
            e, k, kdec, v, upd_t, decay, qs = _gla_chunk(q_ref, k_ref, v_ref, la_ref, sl)
            sp = sp_ref[0, c]
            s_n = sp * decay + upd_t
            dob = do_ref[sl, :].astype(BF16)
            g = carry_ref[...] + lax.dot_general(dob, qs, (((0,), (0,)), ((), ())), preferred_element_type=F32)
            gb = g.astype(BF16)
            dq_ref[sl, :] = (jnp.dot(dob, s_n.astype(BF16), preferred_element_type=F32) * scale).astype(dq_ref.dtype)
            ddecay = jnp.sum(g * sp, axis=0, keepdims=True)
            dkdec = jnp.dot(v.astype(BF16), gb, preferred_element_type=F32)
            dv_ref[sl, :] = lax.dot_general(kdec.astype(BF16), gb, (((1,), (1,)), ((), ())),
                                            preferred_element_type=F32).astype(dv_ref.dtype)
            dk_ref[sl, :] = (dkdec * e).astype(dk_ref.dtype)
            w = dkdec * k * e
            dtot = jnp.sum(w, axis=0, keepdims=True) + ddecay * decay
            last = lax.broadcasted_iota(jnp.int32, (CHUNK, 1), 0) == CHUNK - 1
            dcum = jnp.where(last, dtot - w, -w)
            dla_ref[sl, :] = jnp.dot(_tri(False), dcum, preferred_element_type=F32, precision=lax.Precision.HIGHEST)
            carry_ref[...] = g * decay

    rev = lambda b: nb - 1 - b
    return pl.pallas_call(
        body, name=_nm("gla_bwd"), grid=(heads, nb),
        out_shape=(jax.ShapeDtypeStruct((S, heads * GLA_DK), hm.dtype), jax.ShapeDtypeStruct((S, heads * GLA_DK), hm.dtype),
                   jax.ShapeDtypeStruct((S, heads * GLA_DV), hm.dtype), jax.ShapeDtypeStruct((S, heads * GLA_DK), F32)),
        in_specs=_gla_specs(heads, rev) + [
            pl.BlockSpec((1, cpb, GLA_DV, GLA_DK), lambda h, b: (h, rev(b), 0, 0)),
            pl.BlockSpec((GLA_ROWS, GLA_DV), lambda h, b: (rev(b), h))],
        out_specs=(pl.BlockSpec((GLA_ROWS, GLA_DK), lambda h, b: (rev(b), h)),
                   pl.BlockSpec((GLA_ROWS, GLA_DK), lambda h, b: (rev(b), h)),
                   pl.BlockSpec((GLA_ROWS, GLA_DV), lambda h, b: (rev(b), h)),
                   pl.BlockSpec((GLA_ROWS, GLA_DK), lambda h, b: (rev(b), h))),
        scratch_shapes=[pltpu.VMEM((GLA_DV, GLA_DK), F32)],
        compiler_params=_cp(("parallel", "arbitrary")),
    )(hm, hm, hm, la, sprev, do)


@functools.partial(jax.custom_vjp, nondiff_argnums=(3,))
def gla_core(hm, la, o_norm, heads):
    return _gla_core_fwd(hm, la, o_norm, heads)[0]


def _gla_core_fwd(hm, la, o_norm, heads):
    o, sprev = _gla_fwd_call(hm, la, heads)
    vd = heads * GLA_DV
    rows = [(o, vd, 0), (hm, vd, 2 * heads * GLA_DK // vd + 1)]
    (y,) = _rw_fwd(_gla_out_fn(heads), rows, [o_norm], [BF16], "gla_out_fwd")
    return y, (hm, la, o_norm, o, sprev)


def _gla_core_bwd(heads, res, dy):
    hm, la, o_norm, o, sprev = res
    vd = heads * GLA_DV
    rows = [(o, vd, 0), (hm, vd, 2 * heads * GLA_DK // vd + 1)]
    (do, dr), (dg,) = _rw_bwd(_gla_out_fn(heads), rows, [o_norm], [dy], [F32, hm.dtype], "gla_out_bwd")
    dq, dk, dv, dla = _gla_bwd_call(hm, la, sprev, do, heads)
    return jnp.concatenate([dq, dk, dv, dr], axis=1), dla, dg


gla_core.defvjp(_gla_core_fwd, _gla_core_bwd)


CONV_COLS = 256
HALO = 16


def _conv_rows(S):
    return min(512, S)


def _conv_taps(main_ref, halo_ref, i):
    prev = jnp.where(i > 0, halo_ref[...].astype(F32), 0.0)
    full = jnp.concatenate([prev, main_ref[...].astype(F32)], axis=0)
    return full[HALO:], pltpu.roll(full, 1, 0)[HALO:], pltpu.roll(full, 2, 0)[HALO:]


def _conv_apply(taps, w_ref, b_ref):
    x0, x1, x2 = taps
    return x2 * w_ref[0:1, :] + x1 * w_ref[1:2, :] + x0 * w_ref[2:3, :] + b_ref[...]


def _gelu_gate(uc, gc):
    return uc * jax.nn.gelu(gc)


def _conv_cols(dff, pref):
    return max(c for c in range(LANES, pref + 1, LANES) if dff % c == 0)


def _conv_in_specs(R, C, nj):
    hpr = R // HALO
    main = lambda off: pl.BlockSpec((R, C), lambda j, i: (i, j + off))
    halo = lambda off: pl.BlockSpec((HALO, C), lambda j, i: (jnp.maximum(i * hpr - 1, 0), j + off))
    par = lambda rows, off: pl.BlockSpec((rows, C), lambda j, i: (0, j + off))
    return [main(0), halo(0), main(nj), halo(nj), par(CONV_W, 0), par(CONV_W, nj), par(1, 0), par(1, nj)]


def _conv_fwd_call(h, cw, cb, comm=None):
    S, dff = h.shape[0], h.shape[1] // 2
    R, C = _conv_rows(S), _conv_cols(dff, 768)
    nj = dff // C

    def body(u_ref, uh_ref, g_ref, gh_ref, wu_ref, wg_ref, bu_ref, bg_ref, a_ref):
        i = pl.program_id(1)
        uc = _conv_apply(_conv_taps(u_ref, uh_ref, i), wu_ref, bu_ref)
        gc = _conv_apply(_conv_taps(g_ref, gh_ref, i), wg_ref, bg_ref)
        a_ref[...] = _gelu_gate(uc, gc).astype(a_ref.dtype)

    return _pcall(
        body, "conv_fwd", (nj, S // R), 0, in_specs=_conv_in_specs(R, C, nj),
        out_specs=[pl.BlockSpec((R, C), lambda j, i: (i, j))], out_shape=[jax.ShapeDtypeStruct((S, dff), BF16)],
        scratch=[], sem=("parallel", "parallel"), args=[h, h, h, h, cw, cw, cb, cb], comm=comm)


def _conv_bwd_gate_call(h, cw, cb, da, comm=None):
    S, dff = h.shape[0], h.shape[1] // 2
    R, C = _conv_rows(S), _conv_cols(dff, 512)
    nj = dff // C

    def body(u_ref, uh_ref, g_ref, gh_ref, wu_ref, wg_ref, bu_ref, bg_ref, da_ref,
             du_ref, dg_ref, dwu_ref, dwg_ref, dbu_ref, dbg_ref):
        i = pl.program_id(1)
        ut, gt = _conv_taps(u_ref, uh_ref, i), _conv_taps(g_ref, gh_ref, i)
        uc, gc = _conv_apply(ut, wu_ref, bu_ref), _conv_apply(gt, wg_ref, bg_ref)
        _, vjp_fn = jax.vjp(_gelu_gate, uc, gc)
        du, dg = vjp_fn(da_ref[...].astype(F32))
        du_ref[...] = du.astype(du_ref.dtype)
        dg_ref[...] = dg.astype(dg_ref.dtype)

        @pl.when(i == 0)
        def _():
            for r in (dwu_ref, dwg_ref, dbu_ref, dbg_ref):
                r[...] = jnp.zeros(r.shape, F32)

        for d, taps, dw_ref, db_ref in ((du, ut, dwu_ref, dbu_ref), (dg, gt, dwg_ref, dbg_ref)):
            x0, x1, x2 = taps
            dw_ref[0:1, :] += jnp.sum(d * x2, axis=0, keepdims=True)
            dw_ref[1:2, :] += jnp.sum(d * x1, axis=0, keepdims=True)
            dw_ref[2:3, :] += jnp.sum(d * x0, axis=0, keepdims=True)
            db_ref[...] += jnp.sum(d, axis=0, keepdims=True)

    tile = pl.BlockSpec((R, C), lambda j, i: (i, j))
    par = lambda rows: pl.BlockSpec((rows, C), lambda j, i: (0, j))
    return _pcall(
        body, "conv_bwd_gate", (nj, S // R), 0, in_specs=_conv_in_specs(R, C, nj) + [tile],
        out_specs=[tile, tile, par(CONV_W), par(CONV_W), par(1), par(1)],
        out_shape=[jax.ShapeDtypeStruct((S, dff), BF16), jax.ShapeDtypeStruct((S, dff), BF16),
                   jax.ShapeDtypeStruct((CONV_W, dff), F32), jax.ShapeDtypeStruct((CONV_W, dff), F32),
                   jax.ShapeDtypeStruct((1, dff), F32), jax.ShapeDtypeStruct((1, dff), F32)],
        scratch=[], sem=("parallel", "arbitrary"), args=[h, h, h, h, cw, cw, cb, cb, da], comm=comm)


def _conv_bwd_shift_call(dc, cw, into=None):
    S, dff = dc.shape
    R, C = min(1024, S), _conv_cols(dff, 1024)
    nj = dff // C
    col_off = 0 if into is None else nj
    hpr, last = R // HALO, S // HALO - 1
    ni = S // R

    def body(d_ref, nx_ref, w_ref, *rest):
        o_ref = rest[-1]
        i = pl.program_id(1)
        nxt = jnp.where(i < ni - 1, nx_ref[...].astype(F32), 0.0)
        full = jnp.concatenate([d_ref[...].astype(F32), nxt], axis=0)
        n = R + HALO
        y1, y2 = pltpu.roll(full, n - 1, 0)[:R], pltpu.roll(full, n - 2, 0)[:R]
        o_ref[...] = (full[:R] * w_ref[2:3, :] + y1 * w_ref[1:2, :] + y2 * w_ref[0:1, :]).astype(o_ref.dtype)

    in_specs = [pl.BlockSpec((R, C), lambda j, i: (i, j)),
                pl.BlockSpec((HALO, C), lambda j, i: (jnp.minimum((i + 1) * hpr, last), j)),
                pl.BlockSpec((CONV_W, C), lambda j, i: (0, j + col_off))]
    args = [dc, dc, cw]
    if into is not None:
        in_specs.append(pl.BlockSpec(memory_space=pl.ANY))
        args.append(into)
    return pl.pallas_call(
        body, name=_nm("conv_bwd_shift"), grid=(nj, ni), out_shape=jax.ShapeDtypeStruct((S, 2 * dff), BF16),
        in_specs=in_specs, out_specs=pl.BlockSpec((R, C), lambda j, i: (i, j + col_off)),
        input_output_aliases={} if into is None else {3: 0},
        compiler_params=_cp(("parallel", "parallel")),
    )(*args)


@jax.custom_vjp
def ffn_hidden(x1, w3, cw, cb):
    return _ffn_hidden_fwd(x1, w3, cw, cb)[0]


def _ffn_hidden_fwd(x1, w3, cw, cb, comm=None):
    h = _mm(x1, w3, "nn", BF16, "up", _all_blocks(w3))
    a, *carried = _conv_fwd_call(h, cw, cb, comm)
    return (a, (x1, w3, cw, cb, h)) + ((carried,) if comm is not None else ())


def _ffn_hidden_bwd(res, da, comm=None):
    x1, w3, cw, cb, h = res
    du, dg, dwu, dwg, dbu, dbg, *carried = _conv_bwd_gate_call(h, cw, cb, da, comm)
    dh = _conv_bwd_shift_call(dg, cw, into=_conv_bwd_shift_call(du, cw))
    dx = _mm(dh, w3, "nt", F32, "up_dx", _all_blocks(w3))
    dw3 = _mm(x1, dh, "tn", w3.dtype, "up_dw", _all_blocks(w3))
    grads = (dx, dw3, jnp.concatenate([dwu, dwg], axis=1), jnp.concatenate([dbu, dbg], axis=1))
    return (grads, carried) if comm is not None else grads


ffn_hidden.defvjp(_ffn_hidden_fwd, _ffn_hidden_bwd)


def _loss_call(y, target):
    S, D = y.shape
    tr = _row_tile(S, D)

    def body(y_ref, t_ref, sq_ref, dy_ref):
        diff = y_ref[...] - t_ref[...]
        dy_ref[...] = diff * (1.0 / D)
        part = jnp.sum(diff * diff, axis=0, keepdims=True)
        i = pl.program_id(0)

        @pl.when(i == 0)
        def _():
            sq_ref[...] = part

        @pl.when(i > 0)
        def _():
            sq_ref[...] += part

    row = pl.BlockSpec((tr, D), lambda i: (i, 0))
    return pl.pallas_call(
        body, name=_nm("loss"), grid=(S // tr,),
        out_shape=(jax.ShapeDtypeStruct((1, D), F32), jax.ShapeDtypeStruct((S, D), F32)),
        in_specs=[row, row], out_specs=(pl.BlockSpec((1, D), lambda i: (0, 0)), row),
        compiler_params=_cp(("arbitrary",)),
    )(y, target)


def _row_divisor(rows):
    for cand in range(min(rows, 512), 15, -1):
        if rows % cand == 0 and cand % 16 == 0:
            return cand
    return rows


def _adamw_call(w, g, m, v):
    shape = w.shape
    w2, g2, m2, v2 = (a.reshape(-1, shape[-1]) for a in (w, g, m, v))
    rows, cols = w2.shape
    tr = _row_divisor(rows)

    def body(w_ref, g_ref, m_ref, v_ref, d_ref, nm_ref, nv_ref):
        g_ = g_ref[...]
        m_ = ADAM_B1 * m_ref[...] + (1.0 - ADAM_B1) * g_
        v_ = ADAM_B2 * v_ref[...] + (1.0 - ADAM_B2) * (g_ * g_)
        m_hat = m_ / (1.0 - ADAM_B1 ** ADAM_STEP)
        v_hat = v_ / (1.0 - ADAM_B2 ** ADAM_STEP)
        d_ref[...] = -ADAM_LR * (m_hat / (jnp.sqrt(v_hat) + ADAM_EPS) + ADAM_WD * w_ref[...])
        nm_ref[...] = m_
        nv_ref[...] = v_

    blk = pl.BlockSpec((tr, cols), lambda i: (i, 0))
    outs = pl.pallas_call(
        body, name=_nm("adamw"), grid=(rows // tr,),
        out_shape=tuple(jax.ShapeDtypeStruct((rows, cols), F32) for _ in range(3)),
        in_specs=[blk] * 4, out_specs=(blk,) * 3, compiler_params=_cp(("parallel",)),
    )(w2, g2, m2, v2)
    return tuple(o.reshape(shape) for o in outs)


ANY = pl.BlockSpec(memory_space=pl.ANY)


def _place():
    return lax.axis_index("x"), lax.axis_index("y"), lax.axis_index("c")


def all_gather(shards):
    n = len(shards)

    def body(*refs):
        x_refs, out_refs = refs[:n], refs[n:2 * n]
        send_sems, recv_sems, local_sems = refs[2 * n:]
        x, y, c = _place()
        me, sibling = (x, y, c), (x, y, 1 - c)
        chips = [(1 - x, y), (x, 1 - y), (1 - x, 1 - y)]

        def slot(a, px, py, pc):
            return out_refs[a].at[:, 4 * px + 2 * py + pc]

        def copy(a, k, block, to, own=False):
            return pltpu.make_async_remote_copy(
                src_ref=x_refs[a] if own else slot(a, *block), dst_ref=slot(a, *block),
                send_sem=send_sems.at[7 * a + k], recv_sem=recv_sems.at[7 * a + k], device_id=to, device_id_type=MESH)

        mine = [pltpu.make_async_copy(x_refs[a], slot(a, *me), local_sems.at[a]) for a in range(n)]
        first = []
        for a in range(n):
            mine[a].start()
            first.append(copy(a, 0, me, sibling, own=True))
            first += [copy(a, 1 + j, me, (*chip, c), own=True) for j, chip in enumerate(chips)]
        for cp in first:
            cp.start()
        passed = []
        for j, chip in enumerate(chips):
            for a in range(n):
                copy(a, 1 + j, (*chip, c), me).wait_recv()
                passed.append(copy(a, 4 + j, (*chip, c), sibling))
                passed[-1].start()
        for a in range(n):
            copy(a, 0, sibling, me).wait_recv()
        for j, chip in enumerate(chips):
            for a in range(n):
                copy(a, 4 + j, (*chip, 1 - c), me).wait_recv()
        for cp in first + passed:
            cp.wait_send()
        for cp in mine:
            cp.wait()

    return pl.pallas_call(
        body, name=_nm("all_gather"),
        out_shape=tuple(jax.ShapeDtypeStruct((s.shape[0], N_DEV) + s.shape[1:], s.dtype) for s in shards),
        in_specs=[ANY] * n, out_specs=(ANY,) * n,
        scratch_shapes=[pltpu.SemaphoreType.DMA((7 * n,)), pltpu.SemaphoreType.DMA((7 * n,)), pltpu.SemaphoreType.DMA((n,))],
    )(*shards)


def _rs_pair_exchange(gs):
    n = len(gs)

    def body(*refs):
        g_refs, recv_refs = refs[:n], refs[n:2 * n]
        send_sems, recv_sems = refs[2 * n:]
        x, y, c = _place()
        copies = [pltpu.make_async_remote_copy(
            src_ref=g_refs[a].at[:, 2 * j + (1 - c)], dst_ref=recv_refs[a].at[j], send_sem=send_sems.at[4 * a + j],
            recv_sem=recv_sems.at[4 * a + j], device_id=(x, y, 1 - c), device_id_type=MESH)
            for a in range(n) for j in range(4)]
        for cp in copies:
            cp.start()
        for cp in copies:
            cp.wait_recv()
        for cp in copies:
            cp.wait_send()

    return pl.pallas_call(
        body, name=_nm("rs_pair"),
        out_shape=tuple(jax.ShapeDtypeStruct((4, g.shape[0]) + g.shape[2:], g.dtype) for g in gs),
        in_specs=[ANY] * n, out_specs=(ANY,) * n,
        scratch_shapes=[pltpu.SemaphoreType.DMA((4 * n,)), pltpu.SemaphoreType.DMA((4 * n,))],
    )(*gs)


def _rs_chip_exchange(ps):
    n = len(ps)

    def body(*refs):
        p_refs, recv_refs = refs[:n], refs[n:2 * n]
        send_sems, recv_sems = refs[2 * n:]
        x, y, c = _place()
        chips = [(1 - x, y), (x, 1 - y), (1 - x, 1 - y)]
        copies = [pltpu.make_async_remote_copy(
            src_ref=p_refs[a].at[2 * cx + cy], dst_ref=recv_refs[a].at[k], send_sem=send_sems.at[3 * a + k],
            recv_sem=recv_sems.at[3 * a + k], device_id=(cx, cy, c), device_id_type=MESH)
            for a in range(n) for k, (cx, cy) in enumerate(chips)]
        for cp in copies:
            cp.start()
        for cp in copies:
            cp.wait_recv()
        for cp in copies:
            cp.wait_send()

    return pl.pallas_call(
        body, name=_nm("rs_chip"),
        out_shape=tuple(jax.ShapeDtypeStruct((3,) + p.shape[1:], p.dtype) for p in ps),
        in_specs=[ANY] * n, out_specs=(ANY,) * n,
        scratch_shapes=[pltpu.SemaphoreType.DMA((3 * n,)), pltpu.SemaphoreType.DMA((3 * n,))],
    )(*ps)


def _rs_pair_add(g, recv, c_idx):
    L, _, a, b = g.shape
    ta = _row_divisor(a)

    def body(c_ref, g_ref, r_ref, o_ref):
        o_ref[...] = (g_ref[...].astype(F32) + r_ref[...].astype(F32)).astype(o_ref.dtype)

    grid_spec = pltpu.PrefetchScalarGridSpec(
        num_scalar_prefetch=1, grid=(4, L, a // ta),
        in_specs=[pl.BlockSpec((None, None, ta, b), lambda j, l, i, c_ref: (l, 2 * j + c_ref[0], i, 0)),
                  pl.BlockSpec((None, None, ta, b), lambda j, l, i, c_ref: (j, l, i, 0))],
        out_specs=pl.BlockSpec((None, None, ta, b), lambda j, l, i, c_ref: (j, l, i, 0)))
    return pl.pallas_call(
        body, name=_nm("rs_pair_add"), grid_spec=grid_spec, out_shape=jax.ShapeDtypeStruct((4, L, a, b), g.dtype),
        compiler_params=_cp(("parallel", "parallel", "parallel")),
    )(c_idx, g, recv)


def _rs_final_add(p1, recv, chip_idx):
    _, L, a, b = p1.shape
    ta = _row_divisor(a)

    def body(chip_ref, p_ref, r_ref, o_ref):
        acc = p_ref[...].astype(F32)
        for k in range(3):
            acc = acc + r_ref[k].astype(F32)
        o_ref[...] = acc

    grid_spec = pltpu.PrefetchScalarGridSpec(
        num_scalar_prefetch=1, grid=(L, a // ta),
        in_specs=[pl.BlockSpec((None, None, ta, b), lambda l, i, chip_ref: (chip_ref[0], l, i, 0)),
                  pl.BlockSpec((3, None, ta, b), lambda l, i, chip_ref: (0, l, i, 0))],
        out_specs=pl.BlockSpec((None, ta, b), lambda l, i, chip_ref: (l, i, 0)))
    return pl.pallas_call(
        body, name=_nm("rs_final_add"), grid_spec=grid_spec, out_shape=jax.ShapeDtypeStruct((L, a, b), F32),
        compiler_params=_cp(("parallel", "parallel")),
    )(chip_idx, p1, recv)


def reduce_scatter(gs):
    x, y, c = _place()
    c_idx = jnp.reshape(c, (1,)).astype(jnp.int32)
    chip_idx = jnp.reshape(2 * x + y, (1,)).astype(jnp.int32)
    recv1 = _rs_pair_exchange(gs)
    p1 = [_rs_pair_add(g, r, c_idx) for g, r in zip(gs, recv1)]
    recv2 = _rs_chip_exchange(p1)
    return [_rs_final_add(p, r, chip_idx) for p, r in zip(p1, recv2)]


def all_reduce_small(v):
    r, C = v.shape

    def body(v_ref, out_ref, buf_ref, send_sems, recv_sems):
        x, y, c = _place()
        my_id = 4 * x + 2 * y + c
        buf_ref[my_id] = v_ref[...]
        copies = []
        for k in range(1, N_DEV):
            fx, fy, fc = (k >> 2) & 1, (k >> 1) & 1, k & 1
            peer = (x ^ fx, y ^ fy, c ^ fc)
            copies.append(pltpu.make_async_remote_copy(
                src_ref=v_ref, dst_ref=buf_ref.at[my_id], send_sem=send_sems.at[k - 1], recv_sem=recv_sems.at[k - 1],
                device_id=peer, device_id_type=MESH))
        for cp in copies:
            cp.start()
        for cp in copies:
            cp.wait_recv()
        for cp in copies:
            cp.wait_send()
        acc = buf_ref[0]
        for d in range(1, N_DEV):
            acc = acc + buf_ref[d]
        out_ref[...] = acc

    vm = pl.BlockSpec(memory_space=pltpu.VMEM)
    return pl.pallas_call(
        body, name=_nm("all_reduce_small"), out_shape=jax.ShapeDtypeStruct((r, C), F32),
        in_specs=[vm], out_specs=vm,
        scratch_shapes=[pltpu.VMEM((N_DEV, r, C), F32), pltpu.SemaphoreType.DMA((7,)), pltpu.SemaphoreType.DMA((7,))],
    )(v)


def _slot(ref, px, py, pc):
    return ref.at[:, 4 * px + 2 * py + pc]


def comm_gather_own(shards):
    n = len(shards)

    def build(cin, cout, send_sems, recv_sems, local_sems):
        x, y, c = _place()
        me = (x, y, c)
        peers = [(x, y, 1 - c), (1 - x, y, c), (x, 1 - y, c), (1 - x, 1 - y, c)]
        starts, waits = [], []
        for a in range(n):
            local = pltpu.make_async_copy(cin[a], _slot(cout[a], *me), local_sems.at[a])
            starts.append(local)
            waits.append(local.wait)
            for k, peer in enumerate(peers):
                send = pltpu.make_async_remote_copy(
                    src_ref=cin[a], dst_ref=_slot(cout[a], *me), send_sem=send_sems.at[4 * a + k],
                    recv_sem=recv_sems.at[4 * a + k], device_id=peer, device_id_type=MESH)
                arrive = pltpu.make_async_remote_copy(
                    src_ref=cin[a], dst_ref=_slot(cout[a], *peer), send_sem=send_sems.at[4 * a + k],
                    recv_sem=recv_sems.at[4 * a + k], device_id=peer, device_id_type=MESH)
                starts.append(send)
                waits += [arrive.wait_recv, send.wait_send]
        return starts, waits

    out_shapes = [jax.ShapeDtypeStruct((s.shape[0], N_DEV) + s.shape[1:], s.dtype) for s in shards]
    return Comm(shards, out_shapes, {}, 4 * n, 4 * n, n, build)


def comm_gather_pass(partial):
    n = len(partial)

    def build(cin, cout, send_sems, recv_sems, local_sems):
        x, y, c = _place()
        chips = [(1 - x, y), (x, 1 - y), (1 - x, 1 - y)]
        starts, waits = [], []
        for a in range(n):
            for j, chip in enumerate(chips):
                send = pltpu.make_async_remote_copy(
                    src_ref=_slot(cout[a], *chip, c), dst_ref=_slot(cout[a], *chip, c), send_sem=send_sems.at[3 * a + j],
                    recv_sem=recv_sems.at[3 * a + j], device_id=(x, y, 1 - c), device_id_type=MESH)
                arrive = pltpu.make_async_remote_copy(
                    src_ref=_slot(cout[a], *chip, c), dst_ref=_slot(cout[a], *chip, 1 - c),
                    send_sem=send_sems.at[3 * a + j], recv_sem=recv_sems.at[3 * a + j],
                    device_id=(x, y, 1 - c), device_id_type=MESH)
                starts.append(send)
                waits += [arrive.wait_recv, send.wait_send]
        return starts, waits

    out_shapes = [jax.ShapeDtypeStruct(p.shape, p.dtype) for p in partial]
    return Comm(partial, out_shapes, {a: a for a in range(n)}, 3 * n, 3 * n, 1, build)


def comm_rs_pair(gs):
    n = len(gs)

    def build(cin, cout, send_sems, recv_sems, local_sems):
        x, y, c = _place()
        starts, waits = [], []
        for a in range(n):
            for j in range(4):
                cp = pltpu.make_async_remote_copy(
                    src_ref=cin[a].at[:, 2 * j + (1 - c)], dst_ref=cout[a].at[j], send_sem=send_sems.at[4 * a + j],
                    recv_sem=recv_sems.at[4 * a + j], device_id=(x, y, 1 - c), device_id_type=MESH)
                starts.append(cp)
                waits += [cp.wait_recv, cp.wait_send]
        return starts, waits

    out_shapes = [jax.ShapeDtypeStruct((4, g.shape[0]) + g.shape[2:], g.dtype) for g in gs]
    return Comm(gs, out_shapes, {}, 4 * n, 4 * n, 1, build)


def comm_rs_chip(ps):
    n = len(ps)

    def build(cin, cout, send_sems, recv_sems, local_sems):
        x, y, c = _place()
        chips = [(1 - x, y), (x, 1 - y), (1 - x, 1 - y)]
        starts, waits = [], []
        for a in range(n):
            for k, (cx, cy) in enumerate(chips):
                cp = pltpu.make_async_remote_copy(
                    src_ref=cin[a].at[2 * cx + cy], dst_ref=cout[a].at[k], send_sem=send_sems.at[3 * a + k],
                    recv_sem=recv_sems.at[3 * a + k], device_id=(cx, cy, c), device_id_type=MESH)
                starts.append(cp)
                waits += [cp.wait_recv, cp.wait_send]
        return starts, waits

    out_shapes = [jax.ShapeDtypeStruct((3,) + p.shape[1:], p.dtype) for p in ps]
    return Comm(ps, out_shapes, {}, 3 * n, 3 * n, 1, build)


def _pack(arrays, dtype, row_align):
    lead = arrays[0].shape[:-1]
    quantum = row_align * PACK_COLS
    parts, sizes = [], []
    for a in arrays:
        n = a.shape[-1]
        padded = _round_up(n, quantum)
        a = a.astype(dtype)
        if padded != n:
            a = jnp.pad(a, [(0, 0)] * len(lead) + [(0, padded - n)])
        parts.append(a.reshape(*lead, padded // PACK_COLS, PACK_COLS))
        sizes.append((n, padded // PACK_COLS))
    return jnp.concatenate(parts, axis=len(lead)), sizes


def _unpack(packed, sizes):
    lead = packed.shape[:-2]
    out, row = [], 0
    for n, rows in sizes:
        part = lax.slice_in_dim(packed, row, row + rows, axis=len(lead))
        out.append(part.reshape(*lead, rows * PACK_COLS)[..., :n])
        row += rows
    return out


def _unshard(gathered, axis):
    _, L, a, b = gathered.shape
    if axis == 1:
        return [gathered[:, l].reshape(N_DEV * a, b) for l in range(L)]
    return [jnp.transpose(gathered[:, l], (1, 0, 2)).reshape(a, N_DEV * b) for l in range(L)]


def _reshard(fulls, axis):
    blocks = []
    for f in fulls:
        A, B = f.shape
        if axis == 1:
            blocks.append(f.reshape(N_DEV, A // N_DEV, B))
        else:
            blocks.append(jnp.transpose(f.reshape(A, N_DEV, B // N_DEV), (1, 0, 2)))
    return jnp.stack(blocks, axis=1)


def _as3(a):
    return a if a.ndim == 3 else a[:, None, :]


def _prep_big(name, w, dims):
    w = w.astype(BF16)
    L, a, b = w.shape
    if name == 'mla_w_in':
        return jnp.pad(w, ((0, 0), (0, 0), (0, dims['h_width'] - b)))
    if name == 'mla_w_uq':
        hd = MLA_NOPE + MLA_ROPE
        w = jnp.pad(w.reshape(L, a, b // hd, hd), ((0, 0), (0, 0), (0, 0), (0, MLA_QK_PAD - hd)))
        return w.reshape(L, a, b // hd * MLA_QK_PAD)
    if name == 'ffn_w_up':
        return jnp.pad(w, ((0, 0), (0, 0), (0, _round_up(b, CONV_COLS) - b)))
    return w


def _unprep_big(name, g, shape):
    L, a, b = shape
    if name == 'mla_w_uq':
        hd = MLA_NOPE + MLA_ROPE
        return g.reshape(L, a, b // hd, MLA_QK_PAD)[..., :hd].reshape(L, a, b)
    return g[:, :, :b]


def _rope_tables(positions):
    inv = 1.0 / (ROPE_THETA ** (jnp.arange(0, MLA_ROPE, 2, dtype=F32) / MLA_ROPE))
    ang = positions.astype(F32)[:, None] * inv
    cos, sin = jnp.cos(ang), jnp.sin(ang)
    one, zero = jnp.ones_like(cos), jnp.zeros_like(cos)
    a = jnp.concatenate([cos, cos, one, one], axis=1)
    up = jnp.concatenate([sin, zero, zero, zero], axis=1)
    down = jnp.concatenate([zero, sin, zero, zero], axis=1)
    return (a, -up, down), (a, up, -down)


def _rows_full(w):
    return w.reshape(w.shape[0] * w.shape[1], w.shape[2])


def _ops(depth):
    alpha = (2 * depth) ** 0.25
    return {'ln_res': rw_op(_ln_res_fn(alpha), "ln_res", 2, [F32]), 'ple': rw_op(_ple_fn, "ple", 3, [F32]),
            'gla_gate': rw_op(_gla_gate_fn, "gla_gate", 1, [F32])}


MLA_PRE_W = ['mla_w_in', 'mla_q_norm', 'mla_kv_norm', 'mla_w_uq', 'mla_w_uk', 'mla_w_uv']
FFN_IN_W = ['mla_w_o', 'ln1_g', 'ln1_b', 'ffn_conv_w', 'ffn_conv_b']
FFN_OUT_W = ['ffn_w_down', 'ln2_g', 'ln2_b', 'ple_w_gate', 'ple_w_proj', 'ple_b_gate']


def _mla_heads(wl, j):
    return N_DEV * wl['mla_w_uk'][j].shape[2] // MLA_NOPE


def _mla_pre(x, wl, j):
    w_uq, w_uk, w_uv = wl['mla_w_uq'][j], wl['mla_w_uk'][j], wl['mla_w_uv'][j]
    h = linear(x, _rows_full(wl['mla_w_in'][j]), F32)
    mla_norm = rw_op(_mla_norm_fn(w_uq.shape[1], w_uk.shape[1]), "mla_norm", 1, [BF16, BF16, F32])
    cq, ckv, kr_raw = mla_norm(h, wl['mla_q_norm'][j], wl['mla_kv_norm'][j])
    return linear(cq, w_uq, BF16), linear(ckv, w_uk, BF16), linear(ckv, w_uv, BF16), kr_raw


def _gla_mixer(x, wl, j, ops):
    w_in3, w_a2 = wl['gla_w_in'][j], wl['gla_w_a2'][j]
    w_o = _rows_full(wl['gla_w_o'][j])
    w_in = jnp.transpose(w_in3, (1, 0, 2)).reshape(w_in3.shape[1], N_DEV * w_in3.shape[2])
    heads = w_o.shape[0] // GLA_DV
    n_main = 2 * heads * GLA_DK + 2 * heads * GLA_DV
    w_a = jnp.pad(w_in[:, n_main:], ((0, 0), (0, LANES - GLA_RANK)))
    w_a2_p = jnp.pad(w_a2, ((0, LANES - GLA_RANK), (0, 0))).astype(BF16)
    hm = linear(x, w_in[:, :n_main], BF16)
    ha = linear(x, w_a, BF16)
    (la,) = ops['gla_gate'](linear(ha, w_a2_p, F32), wl['gla_b_a'][j])
    return linear(gla_core(hm, la, wl['gla_o_norm'][j], heads), w_o, F32)


def _ffn_in(x, m, wl, i, ops, bp):
    (x1,) = ops['ln_res'](x, m, wl['ln1_g'][i], wl['ln1_b'][i])
    cw, cb = wl['ffn_conv_w'][i], wl['ffn_conv_b'][i]
    bu = cw.shape[1] // N_DEV
    cwp = jnp.pad(cw.reshape(CONV_W, N_DEV, bu), ((0, 0), (0, 0), (0, bp - bu))).reshape(CONV_W, N_DEV * bp)
    cbp = jnp.pad(cb.reshape(1, N_DEV, bu), ((0, 0), (0, 0), (0, bp - bu))).reshape(1, N_DEV * bp)
    return x1, cwp, cbp


def _ffn_out(x1, a, wl, p_i, i, ops):
    w_down3 = wl['ffn_w_down'][i]
    half, bu, d_model = N_DEV // 2, 2 * w_down3.shape[1], w_down3.shape[2]
    bp = a.shape[1] // half
    w_down = jnp.pad(w_down3.reshape(half, bu, d_model), ((0, 0), (0, bp - bu), (0, 0))).reshape(half * bp, d_model)
    f = linear(a, w_down, F32)
    (x2,) = ops['ln_res'](x1, f, wl['ln2_g'][i], wl['ln2_b'][i])
    glog = linear(x2, _rows_full(wl['ple_w_gate'][i]), F32)
    pp = linear(p_i, wl['ple_w_proj'][i], F32)
    (x,) = ops['ple'](x2, glog, pp, wl['ple_b_gate'][i])
    return x


def _layer(x, wl, p_i, aux, i, ops):
    j = i // 2
    if i % 2 == 0:
        q_raw, kn, v, kr_raw = _mla_pre(x, wl, j)
        o = make_attention(aux, _mla_heads(wl, j))[0](q_raw, kn, v, kr_raw)
        m = linear(o, _rows_full(wl['mla_w_o'][j]), F32)
    else:
        m = _gla_mixer(x, wl, j, ops)
    x1, cwp, cbp = _ffn_in(x, m, wl, i, ops, wl['ffn_w_up'][i].shape[2])
    return _ffn_out(x1, ffn_hidden(x1, wl['ffn_w_up'][i], cwp, cbp), wl, p_i, i, ops)


def kernel(x, p, positions, mla_w_in, mla_q_norm, mla_kv_norm, mla_w_uq, mla_w_uk, mla_w_uv, mla_w_o, gla_w_in, gla_w_a2, gla_b_a, gla_o_norm, gla_w_o, ln1_g, ln1_b, ln2_g, ln2_b, ffn_w_up, ffn_conv_w, ffn_conv_b, ffn_w_down, ple_w_proj, ple_w_gate, ple_b_gate, loss_target, m_mla_w_in, m_mla_q_norm, m_mla_kv_norm, m_mla_w_uq, m_mla_w_uk, m_mla_w_uv, m_mla_w_o, m_gla_w_in, m_gla_w_a2, m_gla_b_a, m_gla_o_norm, m_gla_w_o, m_ln1_g, m_ln1_b, m_ln2_g, m_ln2_b, m_ffn_w_up, m_ffn_conv_w, m_ffn_conv_b, m_ffn_w_down, m_ple_w_proj, m_ple_w_gate, m_ple_b_gate, v_mla_w_in, v_mla_q_norm, v_mla_kv_norm, v_mla_w_uq, v_mla_w_uk, v_mla_w_uv, v_mla_w_o, v_gla_w_in, v_gla_w_a2, v_gla_b_a, v_gla_o_norm, v_gla_w_o, v_ln1_g, v_ln1_b, v_ln2_g, v_ln2_b, v_ffn_w_up, v_ffn_conv_w, v_ffn_conv_b, v_ffn_w_down, v_ple_w_proj, v_ple_w_gate, v_ple_b_gate):
    w = dict(zip(WEIGHTS, (mla_w_in, mla_q_norm, mla_kv_norm, mla_w_uq, mla_w_uk, mla_w_uv, mla_w_o, gla_w_in, gla_w_a2,
                           gla_b_a, gla_o_norm, gla_w_o, ln1_g, ln1_b, ln2_g, ln2_b, ffn_w_up, ffn_conv_w, ffn_conv_b,
                           ffn_w_down, ple_w_proj, ple_w_gate, ple_b_gate)))
    m_in = dict(zip(WEIGHTS, (m_mla_w_in, m_mla_q_norm, m_mla_kv_norm, m_mla_w_uq, m_mla_w_uk, m_mla_w_uv, m_mla_w_o,
                              m_gla_w_in, m_gla_w_a2, m_gla_b_a, m_gla_o_norm, m_gla_w_o, m_ln1_g, m_ln1_b, m_ln2_g,
                              m_ln2_b, m_ffn_w_up, m_ffn_conv_w, m_ffn_conv_b, m_ffn_w_down, m_ple_w_proj, m_ple_w_gate,
                              m_ple_b_gate)))
    v_in = dict(zip(WEIGHTS, (v_mla_w_in, v_mla_q_norm, v_mla_kv_norm, v_mla_w_uq, v_mla_w_uk, v_mla_w_uv, v_mla_w_o,
                              v_gla_w_in, v_gla_w_a2, v_gla_b_a, v_gla_o_norm, v_gla_w_o, v_ln1_g, v_ln1_b, v_ln2_g,
                              v_ln2_b, v_ffn_w_up, v_ffn_conv_w, v_ffn_conv_b, v_ffn_w_down, v_ple_w_proj, v_ple_w_gate,
                              v_ple_b_gate)))
    _uid[0] = itertools.count()
    x2d, target, pos = x[0], loss_target[0], positions[0]
    p3 = p[:, 0]
    dims = {'h_width': mla_w_uq.shape[1] + mla_w_uk.shape[1] + LANES}

    depth = ln1_g.shape[0]
    ops = _ops(depth)
    cid = pos // CHUNK
    aux = {'rope': _rope_tables(pos), 'cidq': cid[:, None], 'cidk': cid[None, :],
           'need': _mask_table(cid, _attn_tile(pos.shape[0]))}

    in_layer0 = set(MLA_PRE_W + FFN_IN_W + FFN_OUT_W + ['ffn_w_up'])
    prepped = {n: _prep_big(n, w[n], dims) for n in BIG}
    first_names = [n for n in BIG if n in MLA_PRE_W + ['mla_w_o']]
    mid_names = [n for n in BIG if n in in_layer0 and n not in first_names]
    rest_names = [n for n in BIG if prepped[n].shape[0] > (1 if n in in_layer0 else 0)]
    rest_from = {n: (1 if n in in_layer0 else 0) for n in rest_names}
    small3 = [_as3(w[n]) for n in SMALL]
    small_packed, small_sizes = _pack([s.reshape(1, -1) for s in small3], F32, 8)
    first = all_gather([prepped[n][:1] for n in first_names] + [small_packed])
    wl = {n: [None] * prepped[n].shape[0] for n in BIG}
    for n, g in zip(first_names, first):
        wl[n][0] = g[0]
    for n, s3, flat in zip(SMALL, small3, _unpack(first[-1][0], small_sizes)):
        wl[n] = _unshard(flat.reshape(N_DEV, *s3.shape), SHARD_AXIS[n] if w[n].ndim == 3 else 2)
    for n in REPL:
        wl[n] = [w[n][l][None, :] for l in range(w[n].shape[0])]

    def pick(names, layer0):
        return {n: [wl[n][l] if (l == 0 and n in in_layer0) == layer0 else None for l in range(len(wl[n]))] for n in names}

    heads0 = _mla_heads(wl, 0)
    _, attn_run, attn_bwd = make_attention(aux, heads0)
    pre, vjp_pre = jax.vjp(lambda x_, wl_: _mla_pre(x_, wl_, 0), x2d, pick(MLA_PRE_W, True))
    o, attn_res, partial = attn_run(*pre, comm_gather_own([prepped[n][:1] for n in mid_names]
                                                          + [prepped[n][rest_from[n]:] for n in rest_names]))
    for n, g in zip(mid_names, _run_comm(comm_gather_pass(partial[:len(mid_names)]), "gather_pass")):
        wl[n][0] = g[0]
    partial = partial[len(mid_names):]
    w_up0 = wl['ffn_w_up'][0]

    def ffn_in(x_, o_, wl_):
        return _ffn_in(x_, linear(o_, _rows_full(wl_['mla_w_o'][0]), F32), wl_, 0, ops, w_up0.shape[2])

    (x1, cwp, cbp), vjp_in = jax.vjp(ffn_in, x2d, o, pick(FFN_IN_W, True))
    a, ffn_res, rest = _ffn_hidden_fwd(x1, w_up0, cwp, cbp, comm_gather_pass(partial))
    for n, g in zip(rest_names, rest):
        for l in range(g.shape[0]):
            wl[n][rest_from[n] + l] = g[l]
    x_l0, vjp_out = jax.vjp(lambda x1_, a_, wl_: _ffn_out(x1_, a_, wl_, p3[0], 0, ops), x1, a, pick(FFN_OUT_W, True))

    def tail(x_, wl_):
        for i in range(1, depth):
            x_ = _layer(x_, wl_, p3[i], aux, i, ops)
        return x_

    y, vjp_tail = jax.vjp(tail, x_l0, pick(WEIGHTS, False))
    sq, dy = _loss_call(y, target)

    dwl = {n: [None] * len(wl[n]) for n in WEIGHTS}

    def keep(part):
        for n, per_layer in part.items():
            for l, g in enumerate(per_layer):
                if g is not None:
                    dwl[n][l] = g

    x_c, y_c, c_place = _place()
    c_idx = jnp.reshape(c_place, (1,)).astype(jnp.int32)
    chip_idx = jnp.reshape(2 * x_c + y_c, (1,)).astype(jnp.int32)
    dx_l0, d_tail = vjp_tail(dy)
    keep(d_tail)
    g_rest = [jnp.stack(dwl[n][rest_from[n]:], axis=0) for n in rest_names]
    dx1_out, da, d_out = vjp_out(dx_l0)
    keep(d_out)
    (dx1_ffn, dw_up0, dcwp, dcbp), recv1 = _ffn_hidden_bwd(ffn_res, da, comm_rs_pair(g_rest))
    dwl['ffn_w_up'][0] = dw_up0
    g_mid = [dwl[n][0][None] for n in mid_names]
    g_rest, recv1 = g_rest + g_mid, list(recv1) + list(_rs_pair_exchange(g_mid))
    p1 = [_rs_pair_add(g, r, c_idx) for g, r in zip(g_rest, recv1)]
    dx_in, do, d_in = vjp_in((dx1_out + dx1_ffn, dcwp, dcbp))
    keep(d_in)
    d_pre_in, recv2 = attn_bwd(attn_res, do, comm_rs_chip(p1))
    red_rest = [_rs_final_add(p_, r, chip_idx) for p_, r in zip(p1, recv2)]
    dx_pre, d_pre = vjp_pre(d_pre_in)
    keep(d_pre)
    dx = dx_pre + dx_in

    small_blocks = [_reshard(dwl[n], SHARD_AXIS[n] if w[n].ndim == 3 else 2).reshape(N_DEV, -1) for n in SMALL]
    small_grad_packed, _ = _pack(small_blocks, F32, 8)
    red_first = reduce_scatter([dwl[n][0][None] for n in first_names] + [small_grad_packed[None]])
    by_layer = {n: [] for n in BIG}
    for n, g in zip(first_names, red_first):
        by_layer[n].append(g)
    for n, g in zip(mid_names, red_rest[len(rest_names):]):
        by_layer[n].append(g)
    for n, g in zip(rest_names, red_rest):
        by_layer[n].append(g)
    grads = {n: _unprep_big(n, jnp.concatenate(by_layer[n], axis=0), w[n].shape) for n in BIG}
    for n, f in zip(SMALL, _unpack(red_first[-1][0], small_sizes)):
        grads[n] = f.reshape(w[n].shape)

    repl_flat = [jnp.concatenate([g.reshape(-1) for g in dwl[n]]).reshape(1, -1) for n in REPL]
    loss_part = 0.5 * jnp.sum(sq) / sq.shape[1]
    packed, repl_sizes = _pack(repl_flat + [loss_part.reshape(1, 1)], F32, 8)
    summed = _unpack(all_reduce_small(packed[0])[None], repl_sizes)
    for n, f in zip(REPL, summed[:-1]):
        grads[n] = f.reshape(w[n].shape)
    loss = summed[-1].reshape(())

    delta, new_m, new_v = {}, {}, {}
    for n in WEIGHTS:
        delta[n], new_m[n], new_v[n] = _adamw_call(w[n], grads[n], m_in[n], v_in[n])
    return (loss, dx[None], *[grads[n] for n in WEIGHTS], *[delta[n] for n in WEIGHTS],
            *[new_m[n] for n in WEIGHTS], *[new_v[n] for n in WEIGHTS])
```

```python
import functools
import itertools

import jax
import jax.numpy as jnp
from jax import lax
from jax.experimental import pallas as pl
from jax.experimental.pallas import tpu as pltpu

F32 = jnp.float32
BF16 = jnp.bfloat16
MESH = pl.DeviceIdType.MESH
N_DEV = 8

EPS = 1e-5
NEG_INF = -1e30
CHUNK = 64
Q_BLOCK = 128
MLA_NOPE = 128
MLA_ROPE = 64
MLA_V = 128
MLA_QK_PAD = 256
ROPE_THETA = 10000.0
GLA_DK = 128
GLA_DV = 256
GLA_RANK = 16
GLA_TAU = 16.0
CONV_W = 3
ADAM_LR = 0.001
ADAM_B1 = 0.9
ADAM_B2 = 0.999
ADAM_EPS = 1e-08
ADAM_WD = 0.01
ADAM_STEP = 10
LOG2E = 1.4426950408889634

LANES = 128
PACK_COLS = 1024
VMEM_LIMIT = 56 * 1024 * 1024
MM_VMEM_BUDGET = 38 * 1024 * 1024

WEIGHTS = ['mla_w_in', 'mla_q_norm', 'mla_kv_norm', 'mla_w_uq', 'mla_w_uk', 'mla_w_uv', 'mla_w_o', 'gla_w_in',
           'gla_w_a2', 'gla_b_a', 'gla_o_norm', 'gla_w_o', 'ln1_g', 'ln1_b', 'ln2_g', 'ln2_b', 'ffn_w_up',
           'ffn_conv_w', 'ffn_conv_b', 'ffn_w_down', 'ple_w_proj', 'ple_w_gate', 'ple_b_gate']
SHARD_AXIS = {'mla_w_in': 1, 'mla_q_norm': None, 'mla_kv_norm': None, 'mla_w_uq': 2, 'mla_w_uk': 2, 'mla_w_uv': 2,
              'mla_w_o': 1, 'gla_w_in': 2, 'gla_w_a2': 2, 'gla_b_a': 1, 'gla_o_norm': 1, 'gla_w_o': 1,
              'ln1_g': None, 'ln1_b': None, 'ln2_g': None, 'ln2_b': None, 'ffn_w_up': 2, 'ffn_conv_w': 2,
              'ffn_conv_b': None, 'ffn_w_down': 1, 'ple_w_proj': 2, 'ple_w_gate': 1, 'ple_b_gate': None}
BIG = ['mla_w_in', 'mla_w_uq', 'mla_w_uk', 'mla_w_uv', 'mla_w_o', 'gla_w_in', 'gla_w_o', 'ffn_w_up', 'ffn_w_down',
       'ple_w_proj', 'ple_w_gate']
SMALL = ['gla_w_a2', 'gla_b_a', 'gla_o_norm', 'ffn_conv_w']
REPL = [n for n in WEIGHTS if SHARD_AXIS[n] is None]

_uid = [itertools.count()]


def _nm(base):
    return f"{base}_{next(_uid[0])}"


def _cp(sem=None):
    return pltpu.CompilerParams(dimension_semantics=sem, vmem_limit_bytes=VMEM_LIMIT)


def _round_up(n, m):
    return -(-n // m) * m


class Comm:
    def __init__(self, inputs, out_shapes, aliases, n_send, n_recv, n_local, build):
        self.inputs, self.out_shapes, self.aliases = list(inputs), list(out_shapes), dict(aliases)
        self.n_send, self.n_recv, self.n_local, self.build = n_send, n_recv, n_local, build


def _pcall(body, name, grid, n_prefetch, in_specs, out_specs, out_shape, scratch, sem, args, comm=None):
    in_specs, out_specs, out_shape, scratch, args = list(in_specs), list(out_specs), list(out_shape), list(scratch), list(args)
    n_in, n_out, n_scr = len(in_specs), len(out_specs), len(scratch)
    aliases = {}
    kernel_body = body
    if comm is not None:
        ci, co = len(comm.inputs), len(comm.out_shapes)
        any_spec = pl.BlockSpec(memory_space=pl.ANY)
        in_specs += [any_spec] * ci
        out_specs += [any_spec] * co
        out_shape += comm.out_shapes
        scratch += [pltpu.SemaphoreType.DMA((comm.n_send,)), pltpu.SemaphoreType.DMA((comm.n_recv,)),
                    pltpu.SemaphoreType.DMA((comm.n_local,))]
        aliases = {n_prefetch + n_in + k: n_out + v for k, v in comm.aliases.items()}
        args += comm.inputs
        sem = ("arbitrary",) * len(grid)

        def kernel_body(*refs):
            pre, r = refs[:n_prefetch], refs[n_prefetch:]
            ins, cin = r[:n_in], r[n_in:n_in + ci]
            outs, cout = r[n_in + ci:n_in + ci + n_out], r[n_in + ci + n_out:n_in + ci + n_out + co]
            scr = r[n_in + ci + n_out + co:n_in + ci + n_out + co + n_scr]
            send_sems, recv_sems, local_sems = r[-3:]
            first = functools.reduce(lambda a, b: a & b, [pl.program_id(d) == 0 for d in range(len(grid))])
            last = functools.reduce(lambda a, b: a & b, [pl.program_id(d) == grid[d] - 1 for d in range(len(grid))])
            starts, waits = comm.build(cin, cout, send_sems, recv_sems, local_sems)

            @pl.when(first)
            def _():
                for cp in starts:
                    cp.start()

            body(*pre, *ins, *outs, *scr)

            @pl.when(last)
            def _():
                for wait in waits:
                    wait()

    grid_spec = pltpu.PrefetchScalarGridSpec(num_scalar_prefetch=n_prefetch, grid=grid, in_specs=in_specs,
                                             out_specs=out_specs, scratch_shapes=scratch)
    return pl.pallas_call(kernel_body, name=_nm(name), grid_spec=grid_spec, out_shape=tuple(out_shape),
                          input_output_aliases=aliases, compiler_params=_cp(sem))(*args)


def _run_comm(comm, name):
    return _pcall(lambda: None, name, (1,), 0, [], [], [], [], ("arbitrary",), [], comm)


def _divisor_tiles(n, cap):
    if n % LANES:
        return [n]
    out = [t for t in range(LANES, min(n, cap) + 1, LANES) if n % t == 0]
    return out or [n]


def _mm_tiles(M, N, K, abytes, bbytes, obytes, tn_fixed=None, tk_fixed=None):
    best = None
    for tm in _divisor_tiles(M, 1024):
        for tn in ([tn_fixed] if tn_fixed else _divisor_tiles(N, 1536)):
            for tk in ([tk_fixed] if tk_fixed else _divisor_tiles(K, 4096)):
                vmem = 2 * (tm * tk * abytes + tk * tn * bbytes + tm * tn * obytes) + tm * tn * 4
                if vmem > MM_VMEM_BUDGET:
                    continue
                key = (tm * tn * tk, tk)
                if best is None or key > best[0]:
                    best = (key, (tm, tn, tk))
    assert best is not None, (M, N, K)
    return best[1]


def _mm(a, b, mode, out_dtype, base="mm", blocks=None):
    blk0, nblk = blocks if blocks else (0, 1)
    tn_fixed = tk_fixed = None
    if mode == "nn":
        M, K = a.shape
        N = nblk * b.shape[2] if blocks else b.shape[1]
        tn_fixed = b.shape[2] if blocks else None
    elif mode == "nt":
        M, K = a.shape
        N = b.shape[1] if blocks else b.shape[0]
        tk_fixed = b.shape[2] if blocks else None
        assert not blocks or K == nblk * b.shape[2]
    else:
        (K, M), N = a.shape, b.shape[1]
        tn_fixed = N // nblk if blocks else None
    tm, tn, tk = _mm_tiles(M, N, K, a.dtype.itemsize, b.dtype.itemsize, jnp.dtype(out_dtype).itemsize, tn_fixed, tk_fixed)
    nk = K // tk
    out_shape = jax.ShapeDtypeStruct((M, N), out_dtype)
    out_spec = pl.BlockSpec((tm, tn), lambda i, j, k: (i, j))
    if mode == "nn":
        a_spec = pl.BlockSpec((tm, tk), lambda i, j, k: (i, k))
        b_spec = (pl.BlockSpec((None, tk, tn), lambda i, j, k: (blk0 + j, k, 0)) if blocks
                  else pl.BlockSpec((tk, tn), lambda i, j, k: (k, j)))
        dims = (((1,), (0,)), ((), ()))
    elif mode == "nt":
        a_spec = pl.BlockSpec((tm, tk), lambda i, j, k: (i, k))
        b_spec = (pl.BlockSpec((None, tn, tk), lambda i, j, k: (blk0 + k, j, 0)) if blocks
                  else pl.BlockSpec((tn, tk), lambda i, j, k: (j, k)))
        dims = (((1,), (1,)), ((), ()))
    else:
        a_spec = pl.BlockSpec((tk, tm), lambda i, j, k: (k, i))
        b_spec = pl.BlockSpec((tk, tn), lambda i, j, k: (k, j))
        dims = (((0,), (0,)), ((), ()))
        if blocks:
            out_shape = jax.ShapeDtypeStruct((nblk, M, tn), out_dtype)
            out_spec = pl.BlockSpec((None, tm, tn), lambda i, j, k: (j, i, 0))

    def body(a_ref, b_ref, o_ref, acc_ref):
        part = lax.dot_general(a_ref[...].astype(BF16), b_ref[...].astype(BF16), dims, preferred_element_type=F32)
        if nk == 1:
            o_ref[...] = part.astype(o_ref.dtype)
        else:
            k = pl.program_id(2)

            @pl.when(k == 0)
            def _():
                acc_ref[...] = part

            @pl.when(k > 0)
            def _():
                acc_ref[...] += part

            @pl.when(k == nk - 1)
            def _():
                o_ref[...] = acc_ref[...].astype(o_ref.dtype)

    return pl.pallas_call(
        body, name=_nm(base), grid=(M // tm, N // tn, nk), out_shape=out_shape,
        in_specs=[a_spec, b_spec], out_specs=out_spec,
        scratch_shapes=[pltpu.VMEM((tm, tn) if nk > 1 else (8, LANES), F32)],
        compiler_params=_cp(("parallel", "parallel", "arbitrary")),
    )(a, b)


def _all_blocks(w):
    return (0, w.shape[0]) if w.ndim == 3 else None


@functools.partial(jax.custom_vjp, nondiff_argnums=(2,))
def linear(a, w, out_dtype):
    return _mm(a, w, "nn", out_dtype, "lin_fwd", _all_blocks(w))


def _linear_fwd(a, w, out_dtype):
    return _mm(a, w, "nn", out_dtype, "lin_fwd", _all_blocks(w)), (a, w)


def _linear_bwd(out_dtype, res, dy):
    a, w = res
    return (_mm(dy, w, "nt", a.dtype, "lin_dx", _all_blocks(w)), _mm(a, dy, "tn", w.dtype, "lin_dw", _all_blocks(w)))


linear.defvjp(_linear_fwd, _linear_bwd)


@functools.partial(jax.custom_vjp, nondiff_argnums=(3,))
def linear_sh(a, a_bf16, w, out_dtype):
    return _mm(a_bf16, w, "nn", out_dtype, "lin_fwd", _all_blocks(w))


def _linear_sh_fwd(a, a_bf16, w, out_dtype):
    return _mm(a_bf16, w, "nn", out_dtype, "lin_fwd", _all_blocks(w)), (a_bf16, w, jnp.zeros((), a.dtype))


def _linear_sh_bwd(out_dtype, res, dy):
    a_bf16, w, tok = res
    return (_mm(dy, w, "nt", tok.dtype, "lin_dx", _all_blocks(w)), jnp.zeros_like(a_bf16),
            _mm(a_bf16, dy, "tn", w.dtype, "lin_dw", _all_blocks(w)))


linear_sh.defvjp(_linear_sh_fwd, _linear_sh_bwd)


def _row_tile(S, width):
    tr = 512 if width <= 1024 else 256
    return min(tr, S)


def _rw_fwd(f, rows, params, out_dtypes, base):
    S = rows[0][0].shape[0]
    tr = _row_tile(S, max(w for _, w, _ in rows))
    n_in = len(rows) + len(params)
    avals = [jax.ShapeDtypeStruct((tr, w), F32) for _, w, _ in rows] + [jax.ShapeDtypeStruct(p.shape, F32) for p in params]
    outs = jax.eval_shape(f, *avals)

    def body(*refs):
        vals = [r[...].astype(F32) for r in refs[:n_in]]
        for o_ref, r in zip(refs[n_in:], f(*vals)):
            o_ref[...] = r.astype(o_ref.dtype)

    in_specs = [pl.BlockSpec((tr, w), functools.partial(lambda i, cb: (i, cb), cb=cb)) for _, w, cb in rows]
    in_specs += [pl.BlockSpec(p.shape, lambda i: (0, 0)) for p in params]
    return pl.pallas_call(
        body, name=_nm(base), grid=(S // tr,),
        out_shape=tuple(jax.ShapeDtypeStruct((S, o.shape[1]), dt) for o, dt in zip(outs, out_dtypes)),
        in_specs=in_specs, out_specs=tuple(pl.BlockSpec((tr, o.shape[1]), lambda i: (i, 0)) for o in outs),
        compiler_params=_cp(("parallel",)),
    )(*[a for a, _, _ in rows], *params)


def _rw_bwd(f, rows, params, cts, row_grad_dtypes, base):
    S = rows[0][0].shape[0]
    tr = _row_tile(S, max(w for _, w, _ in rows))
    n_rows, n_par, n_ct = len(rows), len(params), len(cts)
    want = [k for k, dt in enumerate(row_grad_dtypes) if dt is not None]

    def body(*refs):
        in_refs = refs[:n_rows + n_par]
        ct_refs = refs[n_rows + n_par:n_rows + n_par + n_ct]
        out_refs = refs[n_rows + n_par + n_ct:]
        vals = [r[...].astype(F32) for r in in_refs]
        _, vjp_fn = jax.vjp(f, *vals)
        grads = vjp_fn(tuple(c[...].astype(F32) for c in ct_refs))
        for o_ref, k in zip(out_refs[:len(want)], want):
            o_ref[...] = grads[k].astype(o_ref.dtype)
        i = pl.program_id(0)
        for o_ref, g in zip(out_refs[len(want):], grads[n_rows:]):
            @pl.when(i == 0)
            def _(o_ref=o_ref, g=g):
                o_ref[...] = g

            @pl.when(i > 0)
            def _(o_ref=o_ref, g=g):
                o_ref[...] += g

    in_specs = [pl.BlockSpec((tr, w), functools.partial(lambda i, cb: (i, cb), cb=cb)) for _, w, cb in rows]
    in_specs += [pl.BlockSpec(p.shape, lambda i: (0, 0)) for p in params]
    in_specs += [pl.BlockSpec((tr, c.shape[1]), lambda i: (i, 0)) for c in cts]
    out_shape = [jax.ShapeDtypeStruct((S, rows[k][1]), row_grad_dtypes[k]) for k in want]
    out_specs = [pl.BlockSpec((tr, rows[k][1]), lambda i: (i, 0)) for k in want]
    out_shape += [jax.ShapeDtypeStruct(p.shape, F32) for p in params]
    out_specs += [pl.BlockSpec(p.shape, lambda i: (0, 0)) for p in params]
    res = pl.pallas_call(
        body, name=_nm(base), grid=(S // tr,), out_shape=tuple(out_shape),
        in_specs=in_specs, out_specs=tuple(out_specs), compiler_params=_cp(("arbitrary",)),
    )(*[a for a, _, _ in rows], *params, *cts)
    row_grads = [None] * n_rows
    for k, g in zip(want, res[:len(want)]):
        row_grads[k] = g
    return row_grads, list(res[len(want):])


def rw_op(f, base, n_rows, out_dtypes, shadow=False):
    f_fwd = (lambda *a: (lambda r: tuple(r) + (r[0],))(f(*a))) if shadow else f
    fwd_dtypes = list(out_dtypes) + ([BF16] if shadow else [])

    @jax.custom_vjp
    def op(*args):
        return fwd(*args)[0]

    def split(args):
        rows = [(a, a.shape[1], 0) for a in args[:n_rows]]
        return rows, list(args[n_rows:])

    def fwd(*args):
        rows, params = split(args)
        return tuple(_rw_fwd(f_fwd, rows, params, fwd_dtypes, base + "_fwd")), args

    def bwd(args, cts):
        rows, params = split(args)
        cts = list(cts)[:len(out_dtypes)]
        rg, pg = _rw_bwd(f, rows, params, cts, [a.dtype for a, _, _ in rows], base + "_bwd")
        return tuple(rg) + tuple(g.astype(p.dtype) for g, p in zip(pg, params))

    op.defvjp(fwd, bwd)
    return op


def _ln_res_fn(alpha):
    def f(x, m, g, b):
        z = alpha * x + m
        mu = jnp.mean(z, -1, keepdims=True)
        zc = z - mu
        var = jnp.mean(zc * zc, -1, keepdims=True)
        return (zc * lax.rsqrt(var + EPS) * g + b,)
    return f


def _rms(x, g):
    return x * lax.rsqrt(jnp.mean(x * x, -1, keepdims=True) + EPS) * g


def _mla_norm_fn(q_lora, kv_lora):
    def f(h, qn, kvn):
        return (_rms(h[:, :q_lora], qn), _rms(h[:, q_lora:q_lora + kv_lora], kvn),
                h[:, q_lora + kv_lora:q_lora + kv_lora + LANES])
    return f


def _log_sigmoid(z):
    return jnp.minimum(z, 0.0) - jnp.log(1.0 + jnp.exp(-jnp.abs(z)))


def _gla_gate_fn(z, b):
    return (_log_sigmoid(z + b) / GLA_TAU,)


def _ple_fn(x, glog, pp, b):
    return (x + jax.nn.sigmoid(glog + b) * pp,)


def _gla_out_fn(heads):
    def f(o, r, g):
        parts = []
        for h in range(heads):
            oh = o[:, h * GLA_DV:(h + 1) * GLA_DV]
            mu = jnp.mean(oh, -1, keepdims=True)
            oc = oh - mu
            var = jnp.mean(oc * oc, -1, keepdims=True)
            parts.append(oc * lax.rsqrt(var + EPS) * g[:, h * GLA_DV:(h + 1) * GLA_DV])
        return (jnp.concatenate(parts, axis=1) * (r * jax.nn.sigmoid(r)),)
    return f


def _rope_call(x, tabs, roped, out_dtype, base, fold=False):
    S, C = x.shape
    nb = C // LANES
    tr = min(512 if C <= 1024 else 256, S)
    out_c = LANES if fold else C

    def rot(v, a, b1, b2):
        return v * a + pltpu.roll(v, 96, 1) * b1 + pltpu.roll(v, 32, 1) * b2

    def body(x_ref, a_ref, b1_ref, b2_ref, o_ref):
        a, b1, b2 = a_ref[...], b1_ref[...], b2_ref[...]
        if fold:
            v = x_ref[:, 0:LANES].astype(F32)
            for blk in range(1, nb):
                v = v + x_ref[:, blk * LANES:(blk + 1) * LANES].astype(F32)
            o_ref[...] = rot(v, a, b1, b2).astype(o_ref.dtype)
            return
        for blk in range(nb):
            v = x_ref[:, blk * LANES:(blk + 1) * LANES].astype(F32)
            if roped(blk):
                v = rot(v, a, b1, b2)
            o_ref[:, blk * LANES:(blk + 1) * LANES] = v.astype(o_ref.dtype)

    row = lambda w: pl.BlockSpec((tr, w), lambda i: (i, 0))
    return pl.pallas_call(
        body, name=_nm(base), grid=(S // tr,), out_shape=jax.ShapeDtypeStruct((S, out_c), out_dtype),
        in_specs=[row(C), row(LANES), row(LANES), row(LANES)], out_specs=row(out_c),
        compiler_params=_cp(("parallel",)),
    )(x, *tabs)


def _attn_tile(S):
    return min(512, S)


def _tri_schedule(n, by_key):
    pairs = ([(i, j) for j in range(n) for i in range(j, n)] if by_key
             else [(i, j) for i in range(n) for j in range(i + 1)])
    return (jnp.asarray([p[0] for p in pairs], jnp.int32), jnp.asarray([p[1] for p in pairs], jnp.int32))


def _mask_table(cid, t):
    n = cid.shape[0] // t
    blocks = cid.reshape(n, t)
    cmin_q, cmax_k = jnp.min(blocks, axis=1), jnp.max(blocks, axis=1)
    need = (cmax_k[None, :] > cmin_q[:, None]) | jnp.eye(n, dtype=bool)
    return need.astype(jnp.int32).reshape(n * n)


def _attn_mask(cidq_ref, cidk_ref, i, j, t):
    qrow = i * t + lax.broadcasted_iota(jnp.int32, (t, 1), 0)
    kcol = j * t + lax.broadcasted_iota(jnp.int32, (1, t), 1)
    qlim = (qrow // Q_BLOCK + 1) * Q_BLOCK
    return (cidk_ref[...] <= cidq_ref[...]) & (kcol < qlim)


ATTN_FWD_HEADS = 4
ATTN_BWD_HEADS = 2


def _attn_fwd_call(q, kn, v, kr, aux, heads, comm=None):
    S = q.shape[0]
    t = _attn_tile(S)
    n = S // t
    hp = ATTN_FWD_HEADS if heads % ATTN_FWD_HEADS == 0 else 1
    qi_tab, kj_tab = _tri_schedule(n, False)
    scale2 = (MLA_NOPE + MLA_ROPE) ** -0.5 * LOG2E

    def body(qi_ref, kj_ref, need_ref, q_ref, kn_ref, v_ref, kr_ref, cidq_ref, cidk_ref, o_ref, lse_ref,
             m_ref, l_ref, acc_ref):
        st = pl.program_id(1)
        i, j = qi_ref[st], kj_ref[st]

        @pl.when(j == 0)
        def _():
            m_ref[...] = jnp.full(m_ref.shape, NEG_INF, F32)
            l_ref[...] = jnp.zeros(l_ref.shape, F32)
            acc_ref[...] = jnp.zeros(acc_ref.shape, F32)

        def update(masked):
            mask = _attn_mask(cidq_ref, cidk_ref, i, j, t) if masked else None
            for hh in range(hp):
                lanes = slice(hh * LANES, (hh + 1) * LANES)
                k = jnp.concatenate([kn_ref[:, lanes], kr_ref[...]], axis=1)
                qh = q_ref[:, hh * MLA_QK_PAD:(hh + 1) * MLA_QK_PAD]
                s = lax.dot_general(qh, k, (((1,), (1,)), ((), ())), preferred_element_type=F32) * scale2
                if masked:
                    s = jnp.where(mask, s, NEG_INF)
                m_prev = m_ref[:, lanes]
                m_new = jnp.maximum(m_prev, jnp.max(s, axis=1, keepdims=True))
                alpha = jnp.exp2(m_prev - m_new)
                p = jnp.exp2(s - m_new[:, :1])
                l_ref[:, lanes] = alpha * l_ref[:, lanes] + jnp.sum(p, axis=1, keepdims=True)
                acc_ref[:, lanes] = alpha * acc_ref[:, lanes] + jnp.dot(p.astype(BF16), v_ref[:, lanes],
                                                                        preferred_element_type=F32)
                m_ref[:, lanes] = m_new

        need = need_ref[i * n + j]

        @pl.when(need != 0)
        def _():
            update(True)

        @pl.when(need == 0)
        def _():
            update(False)

        @pl.when(j == i)
        def _():
            o_ref[...] = (acc_ref[...] / l_ref[...]).astype(o_ref.dtype)
            lse_ref[...] = m_ref[...] + jnp.log(l_ref[...]) * LOG2E

    qmap = lambda h, s, qi, kj, need: (qi[s], h)
    kmap = lambda h, s, qi, kj, need: (kj[s], h)
    return _pcall(
        body, "attn_fwd", (heads // hp, qi_tab.shape[0]), 3,
        in_specs=[pl.BlockSpec((t, hp * MLA_QK_PAD), qmap), pl.BlockSpec((t, hp * MLA_NOPE), kmap),
                  pl.BlockSpec((t, hp * MLA_V), kmap),
                  pl.BlockSpec((t, LANES), lambda h, s, qi, kj, need: (kj[s], 0)),
                  pl.BlockSpec((t, 1), lambda h, s, qi, kj, need: (qi[s], 0)),
                  pl.BlockSpec((1, t), lambda h, s, qi, kj, need: (0, kj[s]))],
        out_specs=[pl.BlockSpec((t, hp * MLA_V), qmap), pl.BlockSpec((t, hp * LANES), qmap)],
        out_shape=[jax.ShapeDtypeStruct((S, heads * MLA_V), BF16), jax.ShapeDtypeStruct((S, heads * LANES), F32)],
        scratch=[pltpu.VMEM((t, hp * LANES), F32), pltpu.VMEM((t, hp * LANES), F32), pltpu.VMEM((t, hp * MLA_V), F32)],
        sem=("parallel", "arbitrary"),
        args=[qi_tab, kj_tab, aux['need'], q, kn, v, kr, aux['cidq'], aux['cidk']], comm=comm)


def _attn_bwd_call(q, kn, v, kr, o, lse, do, aux, heads, comm=None):
    S = q.shape[0]
    t = _attn_tile(S)
    n = S // t
    qi_tab, kj_tab = _tri_schedule(n, True)
    scale = (MLA_NOPE + MLA_ROPE) ** -0.5
    scale2 = scale * LOG2E
    nt_dims = (((1,), (1,)), ((), ()))
    tn_dims = (((0,), (0,)), ((), ()))

    n_steps = n * (n + 1) // 2
    hp = ATTN_BWD_HEADS if heads % ATTN_BWD_HEADS == 0 else 1

    def body(qi_ref, kj_ref, need_ref, q_ref, kn_ref, v_ref, kr_ref, cidq_ref, cidk_ref, o_ref, lse_ref, do_ref,
             ta_ref, tb1_ref, tb2_ref, dq_ref, dkn_ref, dv_ref, dkr_ref, dk_acc, dv_acc, dq_acc):
        st = pl.program_id(1)
        i, j = qi_ref[st], kj_ref[st]

        @pl.when(st == 0)
        def _():
            dq_acc[...] = jnp.zeros(dq_acc.shape, F32)

        @pl.when(i == j)
        def _():
            dk_acc[...] = jnp.zeros(dk_acc.shape, F32)
            dv_acc[...] = jnp.zeros(dv_acc.shape, F32)

        rows = pl.ds(pl.multiple_of(i * t, t), t)

        def grads(masked):
            mask = _attn_mask(cidq_ref, cidk_ref, i, j, t) if masked else None
            for hh in range(hp):
                lanes = slice(hh * LANES, (hh + 1) * LANES)
                wide = slice(hh * MLA_QK_PAD, (hh + 1) * MLA_QK_PAD)
                k = jnp.concatenate([kn_ref[:, lanes], kr_ref[...]], axis=1)
                qt, do = q_ref[:, wide], do_ref[:, lanes]
                s = lax.dot_general(qt, k, nt_dims, preferred_element_type=F32) * scale2
                if masked:
                    s = jnp.where(mask, s, NEG_INF)
                dp = lax.dot_general(do, v_ref[:, lanes], nt_dims, preferred_element_type=F32)
                dsum = jnp.sum(do.astype(F32) * o_ref[:, lanes].astype(F32), axis=1, keepdims=True)
                p = jnp.exp2(s - lse_ref[:, hh * LANES:hh * LANES + 1])
                ds = (p * (dp - dsum) * scale).astype(BF16)
                dv_acc[:, lanes] += lax.dot_general(p.astype(BF16), do, tn_dims, preferred_element_type=F32)
                dk_acc[:, wide] += lax.dot_general(ds, qt, tn_dims, preferred_element_type=F32)
                dq_acc[rows, wide] += jnp.dot(ds, k, preferred_element_type=F32)

        need = need_ref[i * n + j]

        @pl.when(need != 0)
        def _():
            grads(True)

        @pl.when(need == 0)
        def _():
            grads(False)

        @pl.when(i == n - 1)
        def _():
            for hh in range(hp):
                lanes = slice(hh * LANES, (hh + 1) * LANES)
                off = hh * MLA_QK_PAD
                dkn_ref[:, lanes] = dk_acc[:, off:off + MLA_NOPE].astype(dkn_ref.dtype)
                dkr_ref[:, lanes] = dk_acc[:, off + MLA_NOPE:off + MLA_QK_PAD]
            dv_ref[...] = dv_acc[...].astype(dv_ref.dtype)

        @pl.when(st == n_steps - 1)
        def _():
            for r in range(n):
                rs = slice(r * t, (r + 1) * t)
                for hh in range(hp):
                    off = hh * MLA_QK_PAD
                    dq_ref[rs, off:off + MLA_NOPE] = dq_acc[rs, off:off + MLA_NOPE].astype(dq_ref.dtype)
                    g = dq_acc[rs, off + MLA_NOPE:off + MLA_QK_PAD]
                    g = g * ta_ref[rs, :] + pltpu.roll(g, 96, 1) * tb1_ref[rs, :] + pltpu.roll(g, 32, 1) * tb2_ref[rs, :]
                    dq_ref[rs, off + MLA_NOPE:off + MLA_QK_PAD] = g.astype(dq_ref.dtype)

    qmap = lambda h, s, qi, kj, need: (qi[s], h)
    kmap = lambda h, s, qi, kj, need: (kj[s], h)
    whole = pl.BlockSpec((S, LANES), lambda h, s, qi, kj, need: (0, 0))
    return _pcall(
        body, "attn_bwd", (heads // hp, qi_tab.shape[0]), 3,
        in_specs=[pl.BlockSpec((t, hp * MLA_QK_PAD), qmap), pl.BlockSpec((t, hp * MLA_NOPE), kmap),
                  pl.BlockSpec((t, hp * MLA_V), kmap),
                  pl.BlockSpec((t, LANES), lambda h, s, qi, kj, need: (kj[s], 0)),
                  pl.BlockSpec((t, 1), lambda h, s, qi, kj, need: (qi[s], 0)),
                  pl.BlockSpec((1, t), lambda h, s, qi, kj, need: (0, kj[s])),
                  pl.BlockSpec((t, hp * MLA_V), qmap), pl.BlockSpec((t, hp * LANES), qmap),
                  pl.BlockSpec((t, hp * MLA_V), qmap), whole, whole, whole],
        out_specs=[pl.BlockSpec((S, hp * MLA_QK_PAD), lambda h, s, qi, kj, need: (0, h)),
                   pl.BlockSpec((t, hp * MLA_NOPE), kmap), pl.BlockSpec((t, hp * MLA_V), kmap),
                   pl.BlockSpec((t, hp * LANES), kmap)],
        out_shape=[jax.ShapeDtypeStruct((S, heads * MLA_QK_PAD), BF16), jax.ShapeDtypeStruct((S, heads * MLA_NOPE), BF16),
                   jax.ShapeDtypeStruct((S, heads * MLA_V), BF16), jax.ShapeDtypeStruct((S, heads * LANES), F32)],
        scratch=[pltpu.VMEM((t, hp * MLA_QK_PAD), F32), pltpu.VMEM((t, hp * MLA_V), F32),
                 pltpu.VMEM((S, hp * MLA_QK_PAD), F32)],
        sem=("parallel", "arbitrary"),
        args=[qi_tab, kj_tab, aux['need'], q, kn, v, kr, aux['cidq'], aux['cidk'], o, lse, do, *aux['rope'][1]],
        comm=comm)


def make_attention(aux, heads):
    tabs_f, tabs_b = aux['rope']
    odd, every = (lambda blk: blk % 2 == 1), (lambda blk: True)

    def run_host(q_raw, kn, v, kr_raw, comm=None):
        q = _rope_call(q_raw, tabs_f, odd, BF16, "rope_q")
        kr = _rope_call(kr_raw, tabs_f, every, BF16, "rope_k")
        o, lse, *carried = _attn_fwd_call(q, kn, v, kr, aux, heads, comm)
        return o, (q, kn, v, kr, o, lse), carried

    def bwd_host(res, do, comm=None):
        q, kn, v, kr, o, lse = res
        dq_raw, dkn, dv, dkr, *carried = _attn_bwd_call(q, kn, v, kr, o, lse, do, aux, heads, comm)
        return (dq_raw, dkn, dv, _rope_call(dkr, tabs_b, every, F32, "rope_dk", fold=True)), carried

    @jax.custom_vjp
    def attn(q_raw, kn, v, kr_raw):
        return run_host(q_raw, kn, v, kr_raw)[0]

    attn.defvjp(lambda *a: run_host(*a)[:2], lambda res, do: bwd_host(res, do)[0])
    return attn, run_host, bwd_host


GLA_ROWS = 256
GLA_HEADS_PER_STEP = 2


def _tri(lower):
    r = lax.broadcasted_iota(jnp.int32, (CHUNK, CHUNK), 0)
    c = lax.broadcasted_iota(jnp.int32, (CHUNK, CHUNK), 1)
    return jnp.where((c <= r) if lower else (c >= r), 1.0, 0.0).astype(F32)


def _gla_chunk(q_ref, k_ref, v_ref, la_ref, sl, hh):
    lk, lv = slice(hh * GLA_DK, (hh + 1) * GLA_DK), slice(hh * GLA_DV, (hh + 1) * GLA_DV)
    la = la_ref[sl, lk]
    cum = jnp.dot(_tri(True), la, preferred_element_type=F32, precision=lax.Precision.HIGHEST)
    tot = cum[CHUNK - 1:CHUNK, :]
    e = jnp.exp(tot - cum)
    k = k_ref[sl, lk].astype(F32)
    kdec = k * e
    v = v_ref[sl, lv]
    upd_t = lax.dot_general(v.astype(BF16), kdec.astype(BF16), (((0,), (0,)), ((), ())), preferred_element_type=F32)
    qs = (q_ref[sl, lk].astype(F32) * (GLA_DK ** -0.5)).astype(BF16)
    return e, k, kdec, v, upd_t, jnp.exp(tot), qs


def _gla_group(heads):
    return GLA_HEADS_PER_STEP if heads % GLA_HEADS_PER_STEP == 0 else 1


def _gla_specs(heads, hp, rows_map):
    groups = heads // hp
    return [pl.BlockSpec((GLA_ROWS, hp * GLA_DK), lambda h, b: (rows_map(b), h)),
            pl.BlockSpec((GLA_ROWS, hp * GLA_DK), lambda h, b: (rows_map(b), groups + h)),
            pl.BlockSpec((GLA_ROWS, hp * GLA_DV), lambda h, b: (rows_map(b), groups + h)),
            pl.BlockSpec((GLA_ROWS, hp * GLA_DK), lambda h, b: (rows_map(b), h))]


def _gla_fwd_call(hm, la, heads):
    S = hm.shape[0]
    assert S % GLA_ROWS == 0
    nb, cpb, hp = S // GLA_ROWS, GLA_ROWS // CHUNK, _gla_group(heads)

    def body(q_ref, k_ref, v_ref, la_ref, o_ref, sp_ref, st_ref):
        @pl.when(pl.program_id(1) == 0)
        def _():
            st_ref[...] = jnp.zeros(st_ref.shape, F32)

        for c in range(cpb):
            sl = slice(c * CHUNK, (c + 1) * CHUNK)
            for hh in range(hp):
                _, _, _, _, upd_t, decay, qs = _gla_chunk(q_ref, k_ref, v_ref, la_ref, sl, hh)
                state = st_ref[hh]
                sp_ref[hh, c] = state
                state = state * decay + upd_t
                st_ref[hh] = state
                o_ref[sl, hh * GLA_DV:(hh + 1) * GLA_DV] = lax.dot_general(
                    qs, state.astype(BF16), (((1,), (1,)), ((), ())), preferred_element_type=F32)

    return pl.pallas_call(
        body, name=_nm("gla_fwd"), grid=(heads // hp, nb),
        out_shape=(jax.ShapeDtypeStruct((S, heads * GLA_DV), F32),
                   jax.ShapeDtypeStruct((heads, S // CHUNK, GLA_DV, GLA_DK), F32)),
        in_specs=_gla_specs(heads, hp, lambda b: b),
        out_specs=(pl.BlockSpec((GLA_ROWS, hp * GLA_DV), lambda h, b: (b, h)),
                   pl.BlockSpec((hp, cpb, GLA_DV, GLA_DK), lambda h, b: (h, b, 0, 0))),
        scratch_shapes=[pltpu.VMEM((hp, GLA_DV, GLA_DK), F32)],
        compiler_params=_cp(("parallel", "arbitrary")),
    )(hm, hm, hm, la)


def _gla_bwd_call(hm, la, sprev, do, heads):
    S = hm.shape[0]
    nb, cpb, hp = S // GLA_ROWS, GLA_ROWS // CHUNK, _gla_group(heads)
    scale = GLA_DK ** -0.5

    def body(q_ref, k_ref, v_ref, la_ref, sp_ref, do_ref, dq_ref, dk_ref, dv_ref, dla_ref, carry_ref):
        @pl.when(pl.program_id(1) == 0)
        def _():
            carry_ref[...] = jnp.zeros(carry_ref.shape, F32)

        for c in reversed(range(cpb)):
            sl = slice(c * CHUNK, (c + 1) * CHUNK)
            for hh in range(hp):
                lk, lv = slice(hh * GLA_DK, (hh + 1) * GLA_DK), slice(hh * GLA_DV, (hh + 1) * GLA_DV)
                e, k, kdec, v, upd_t, decay, qs = _gla_chunk(q_ref, k_ref, v_ref, la_ref, sl, hh)
                sp = sp_ref[hh, c]
                s_n = sp * decay + upd_t
                dob = do_ref[sl, lv].astype(BF16)
                g = carry_ref[hh] + lax.dot_general(dob, qs, (((0,), (0,)), ((), ())), preferred_element_type=F32)
                gb = g.astype(BF16)
                dq_ref[sl, lk] = (jnp.dot(dob, s_n.astype(BF16), preferred_element_type=F32) * scale).astype(dq_ref.dtype)
                ddecay = jnp.sum(g * sp, axis=0, keepdims=True)
                dkdec = jnp.dot(v.astype(BF16), gb, preferred_element_type=F32)
                dv_ref[sl, lv] = lax.dot_general(kdec.astype(BF16), gb, (((1,), (1,)), ((), ())),
                                                 preferred_element_type=F32).astype(dv_ref.dtype)
                dk_ref[sl, lk] = (dkdec * e).astype(dk_ref.dtype)
                w = dkdec * k * e
                dtot = jnp.sum(w, axis=0, keepdims=True) + ddecay * decay
                last = lax.broadcasted_iota(jnp.int32, (CHUNK, 1), 0) == CHUNK - 1
                dcum = jnp.where(last, dtot - w, -w)
                dla_ref[sl, lk] = jnp.dot(_tri(False), dcum, preferred_element_type=F32, precision=lax.Precision.HIGHEST)
                carry_ref[hh] = g * decay

    rev = lambda b: nb - 1 - b
    narrow = pl.BlockSpec((GLA_ROWS, hp * GLA_DK), lambda h, b: (rev(b), h))
    wide = pl.BlockSpec((GLA_ROWS, hp * GLA_DV), lambda h, b: (rev(b), h))
    return pl.pallas_call(
        body, name=_nm("gla_bwd"), grid=(heads // hp, nb),
        out_shape=(jax.ShapeDtypeStruct((S, heads * GLA_DK), hm.dtype), jax.ShapeDtypeStruct((S, heads * GLA_DK), hm.dtype),
                   jax.ShapeDtypeStruct((S, heads * GLA_DV), hm.dtype), jax.ShapeDtypeStruct((S, heads * GLA_DK), F32)),
        in_specs=_gla_specs(heads, hp, rev) + [
            pl.BlockSpec((hp, cpb, GLA_DV, GLA_DK), lambda h, b: (h, rev(b), 0, 0)), wide],
        out_specs=(narrow, narrow, wide, narrow),
        scratch_shapes=[pltpu.VMEM((hp, GLA_DV, GLA_DK), F32)],
        compiler_params=_cp(("parallel", "arbitrary")),
    )(hm, hm, hm, la, sprev, do)


@functools.partial(jax.custom_vjp, nondiff_argnums=(3,))
def gla_core(hm, la, o_norm, heads):
    return _gla_core_fwd(hm, la, o_norm, heads)[0]


def _gla_core_fwd(hm, la, o_norm, heads):
    o, sprev = _gla_fwd_call(hm, la, heads)
    vd = heads * GLA_DV
    rows = [(o, vd, 0), (hm, vd, 2 * heads * GLA_DK // vd + 1)]
    (y,) = _rw_fwd(_gla_out_fn(heads), rows, [o_norm], [BF16], "gla_out_fwd")
    return y, (hm, la, o_norm, o, sprev)


def _gla_core_bwd(heads, res, dy):
    hm, la, o_norm, o, sprev = res
    vd = heads * GLA_DV
    rows = [(o, vd, 0), (hm, vd, 2 * heads * GLA_DK // vd + 1)]
    (do, dr), (dg,) = _rw_bwd(_gla_out_fn(heads), rows, [o_norm], [dy], [F32, hm.dtype], "gla_out_bwd")
    dq, dk, dv, dla = _gla_bwd_call(hm, la, sprev, do, heads)
    return jnp.concatenate([dq, dk, dv, dr], axis=1), dla, dg


gla_core.defvjp(_gla_core_fwd, _gla_core_bwd)


CONV_COLS = 256
HALO = 16


def _conv_rows(S):
    return min(512, S)


def _conv_taps(main_ref, halo_ref, i):
    prev = jnp.where(i > 0, halo_ref[...].astype(F32), 0.0)
    full = jnp.concatenate([prev, main_ref[...].astype(F32)], axis=0)
    return full[HALO:], pltpu.roll(full, 1, 0)[HALO:], pltpu.roll(full, 2, 0)[HALO:]


def _conv_apply(taps, w_ref, b_ref):
    x0, x1, x2 = taps
    return x2 * w_ref[0:1, :] + x1 * w_ref[1:2, :] + x0 * w_ref[2:3, :] + b_ref[...]


def _gelu_gate(uc, gc):
    return uc * jax.nn.gelu(gc)


def _conv_cols(dff, pref):
    return max(c for c in range(LANES, pref + 1, LANES) if dff % c == 0)


def _conv_in_specs(R, C, nj):
    hpr = R // HALO
    main = lambda off: pl.BlockSpec((R, C), lambda j, i: (i, j + off))
    halo = lambda off: pl.BlockSpec((HALO, C), lambda j, i: (jnp.maximum(i * hpr - 1, 0), j + off))
    par = lambda rows, off: pl.BlockSpec((rows, C), lambda j, i: (0, j + off))
    return [main(0), halo(0), main(nj), halo(nj), par(CONV_W, 0), par(CONV_W, nj), par(1, 0), par(1, nj)]


def _conv_fwd_call(h, cw, cb, comm=None):
    S, dff = h.shape[0], h.shape[1] // 2
    R, C = _conv_rows(S), _conv_cols(dff, 768)
    nj = dff // C

    def body(u_ref, uh_ref, g_ref, gh_ref, wu_ref, wg_ref, bu_ref, bg_ref, a_ref):
        i = pl.program_id(1)
        uc = _conv_apply(_conv_taps(u_ref, uh_ref, i), wu_ref, bu_ref)
        gc = _conv_apply(_conv_taps(g_ref, gh_ref, i), wg_ref, bg_ref)
        a_ref[...] = _gelu_gate(uc, gc).astype(a_ref.dtype)

    return _pcall(
        body, "conv_fwd", (nj, S // R), 0, in_specs=_conv_in_specs(R, C, nj),
        out_specs=[pl.BlockSpec((R, C), lambda j, i: (i, j))], out_shape=[jax.ShapeDtypeStruct((S, dff), BF16)],
        scratch=[], sem=("parallel", "parallel"), args=[h, h, h, h, cw, cw, cb, cb], comm=comm)


def _conv_bwd_gate_call(h, cw, cb, da, comm=None):
    S, dff = h.shape[0], h.shape[1] // 2
    R, C = _conv_rows(S), _conv_cols(dff, 512)
    nj = dff // C

    def body(u_ref, uh_ref, g_ref, gh_ref, wu_ref, wg_ref, bu_ref, bg_ref, da_ref,
             du_ref, dg_ref, dwu_ref, dwg_ref, dbu_ref, dbg_ref):
        i = pl.program_id(1)
        ut, gt = _conv_taps(u_ref, uh_ref, i), _conv_taps(g_ref, gh_ref, i)
        uc, gc = _conv_apply(ut, wu_ref, bu_ref), _conv_apply(gt, wg_ref, bg_ref)
        _, vjp_fn = jax.vjp(_gelu_gate, uc, gc)
        du, dg = vjp_fn(da_ref[...].astype(F32))
        du_ref[...] = du.astype(du_ref.dtype)
        dg_ref[...] = dg.astype(dg_ref.dtype)

        @pl.when(i == 0)
        def _():
            for r in (dwu_ref, dwg_ref, dbu_ref, dbg_ref):
                r[...] = jnp.zeros(r.shape, F32)

        for d, taps, dw_ref, db_ref in ((du, ut, dwu_ref, dbu_ref), (dg, gt, dwg_ref, dbg_ref)):
            x0, x1, x2 = taps
            dw_ref[0:1, :] += jnp.sum(d * x2, axis=0, keepdims=True)
            dw_ref[1:2, :] += jnp.sum(d * x1, axis=0, keepdims=True)
            dw_ref[2:3, :] += jnp.sum(d * x0, axis=0, keepdims=True)
            db_ref[...] += jnp.sum(d, axis=0, keepdims=True)

    tile = pl.BlockSpec((R, C), lambda j, i: (i, j))
    par = lambda rows: pl.BlockSpec((rows, C), lambda j, i: (0, j))
    return _pcall(
        body, "conv_bwd_gate", (nj, S // R), 0, in_specs=_conv_in_specs(R, C, nj) + [tile],
        out_specs=[tile, tile, par(CONV_W), par(CONV_W), par(1), par(1)],
        out_shape=[jax.ShapeDtypeStruct((S, dff), BF16), jax.ShapeDtypeStruct((S, dff), BF16),
                   jax.ShapeDtypeStruct((CONV_W, dff), F32), jax.ShapeDtypeStruct((CONV_W, dff), F32),
                   jax.ShapeDtypeStruct((1, dff), F32), jax.ShapeDtypeStruct((1, dff), F32)],
        scratch=[], sem=("parallel", "arbitrary"), args=[h, h, h, h, cw, cw, cb, cb, da], comm=comm)


def _conv_bwd_shift_call(dc, cw, into=None):
    S, dff = dc.shape
    R, C = min(1024, S), _conv_cols(dff, 1024)
    nj = dff // C
    col_off = 0 if into is None else nj
    hpr, last = R // HALO, S // HALO - 1
    ni = S // R

    def body(d_ref, nx_ref, w_ref, *rest):
        o_ref = rest[-1]
        i = pl.program_id(1)
        nxt = jnp.where(i < ni - 1, nx_ref[...].astype(F32), 0.0)
        full = jnp.concatenate([d_ref[...].astype(F32), nxt], axis=0)
        n = R + HALO
        y1, y2 = pltpu.roll(full, n - 1, 0)[:R], pltpu.roll(full, n - 2, 0)[:R]
        o_ref[...] = (full[:R] * w_ref[2:3, :] + y1 * w_ref[1:2, :] + y2 * w_ref[0:1, :]).astype(o_ref.dtype)

    in_specs = [pl.BlockSpec((R, C), lambda j, i: (i, j)),
                pl.BlockSpec((HALO, C), lambda j, i: (jnp.minimum((i + 1) * hpr, last), j)),
                pl.BlockSpec((CONV_W, C), lambda j, i: (0, j + col_off))]
    args = [dc, dc, cw]
    if into is not None:
        in_specs.append(pl.BlockSpec(memory_space=pl.ANY))
        args.append(into)
    return pl.pallas_call(
        body, name=_nm("conv_bwd_shift"), grid=(nj, ni), out_shape=jax.ShapeDtypeStruct((S, 2 * dff), BF16),
        in_specs=in_specs, out_specs=pl.BlockSpec((R, C), lambda j, i: (i, j + col_off)),
        input_output_aliases={} if into is None else {3: 0},
        compiler_params=_cp(("parallel", "parallel")),
    )(*args)


@jax.custom_vjp
def ffn_hidden(x1, x1_bf16, w3, cw, cb):
    return _ffn_hidden_fwd(x1, x1_bf16, w3, cw, cb)[0]


def _ffn_hidden_fwd(x1, x1_bf16, w3, cw, cb, comm=None):
    h = _mm(x1_bf16, w3, "nn", BF16, "up", _all_blocks(w3))
    a, *carried = _conv_fwd_call(h, cw, cb, comm)
    return (a, (x1_bf16, w3, cw, cb, h)) + ((carried,) if comm is not None else ())


def _ffn_hidden_bwd(res, da, comm=None):
    x1, w3, cw, cb, h = res
    du, dg, dwu, dwg, dbu, dbg, *carried = _conv_bwd_gate_call(h, cw, cb, da, comm)
    dh = _conv_bwd_shift_call(dg, cw, into=_conv_bwd_shift_call(du, cw))
    dx = _mm(dh, w3, "nt", F32, "up_dx", _all_blocks(w3))
    dw3 = _mm(x1, dh, "tn", w3.dtype, "up_dw", _all_blocks(w3))
    grads = (dx, jnp.zeros_like(x1), dw3, jnp.concatenate([dwu, dwg], axis=1), jnp.concatenate([dbu, dbg], axis=1))
    return (grads, carried) if comm is not None else grads


ffn_hidden.defvjp(_ffn_hidden_fwd, _ffn_hidden_bwd)


def _loss_call(y, target):
    S, D = y.shape
    tr = _row_tile(S, D)

    def body(y_ref, t_ref, sq_ref, dy_ref):
        diff = y_ref[...] - t_ref[...]
        dy_ref[...] = diff * (1.0 / D)
        part = jnp.sum(diff * diff, axis=0, keepdims=True)
        i = pl.program_id(0)

        @pl.when(i == 0)
        def _():
            sq_ref[...] = part

        @pl.when(i > 0)
        def _():
            sq_ref[...] += part

    row = pl.BlockSpec((tr, D), lambda i: (i, 0))
    return pl.pallas_call(
        body, name=_nm("loss"), grid=(S // tr,),
        out_shape=(jax.ShapeDtypeStruct((1, D), F32), jax.ShapeDtypeStruct((S, D), F32)),
        in_specs=[row, row], out_specs=(pl.BlockSpec((1, D), lambda i: (0, 0)), row),
        compiler_params=_cp(("arbitrary",)),
    )(y, target)


def _row_divisor(rows):
    for cand in range(min(rows, 512), 15, -1):
        if rows % cand == 0 and cand % 16 == 0:
            return cand
    return rows


def _adamw_call(w, g, m, v):
    shape = w.shape
    w2, g2, m2, v2 = (a.reshape(-1, shape[-1]) for a in (w, g, m, v))
    rows, cols = w2.shape
    tr = _row_divisor(rows)

    def body(w_ref, g_ref, m_ref, v_ref, d_ref, nm_ref, nv_ref):
        g_ = g_ref[...]
        m_ = ADAM_B1 * m_ref[...] + (1.0 - ADAM_B1) * g_
        v_ = ADAM_B2 * v_ref[...] + (1.0 - ADAM_B2) * (g_ * g_)
        m_hat = m_ / (1.0 - ADAM_B1 ** ADAM_STEP)
        v_hat = v_ / (1.0 - ADAM_B2 ** ADAM_STEP)
        d_ref[...] = -ADAM_LR * (m_hat / (jnp.sqrt(v_hat) + ADAM_EPS) + ADAM_WD * w_ref[...])
        nm_ref[...] = m_
        nv_ref[...] = v_

    blk = pl.BlockSpec((tr, cols), lambda i: (i, 0))
    outs = pl.pallas_call(
        body, name=_nm("adamw"), grid=(rows // tr,),
        out_shape=tuple(jax.ShapeDtypeStruct((rows, cols), F32) for _ in range(3)),
        in_specs=[blk] * 4, out_specs=(blk,) * 3, compiler_params=_cp(("parallel",)),
    )(w2, g2, m2, v2)
    return tuple(o.reshape(shape) for o in outs)


ANY = pl.BlockSpec(memory_space=pl.ANY)


def _place():
    return lax.axis_index("x"), lax.axis_index("y"), lax.axis_index("c")


def all_gather(shards):
    n = len(shards)

    def body(*refs):
        x_refs, out_refs = refs[:n], refs[n:2 * n]
        send_sems, recv_sems, local_sems = refs[2 * n:]
        x, y, c = _place()
        me, sibling = (x, y, c), (x, y, 1 - c)
        chips = [(1 - x, y), (x, 1 - y), (1 - x, 1 - y)]

        def slot(a, px, py, pc):
            return out_refs[a].at[:, 4 * px + 2 * py + pc]

        def copy(a, k, block, to, own=False):
            return pltpu.make_async_remote_copy(
                src_ref=x_refs[a] if own else slot(a, *block), dst_ref=slot(a, *block),
                send_sem=send_sems.at[7 * a + k], recv_sem=recv_sems.at[7 * a + k], device_id=to, device_id_type=MESH)

        mine = [pltpu.make_async_copy(x_refs[a], slot(a, *me), local_sems.at[a]) for a in range(n)]
        first = []
        for a in range(n):
            mine[a].start()
            first.append(copy(a, 0, me, sibling, own=True))
            first += [copy(a, 1 + j, me, (*chip, c), own=True) for j, chip in enumerate(chips)]
        for cp in first:
            cp.start()
        passed = []
        for j, chip in enumerate(chips):
            for a in range(n):
                copy(a, 1 + j, (*chip, c), me).wait_recv()
                passed.append(copy(a, 4 + j, (*chip, c), sibling))
                passed[-1].start()
        for a in range(n):
            copy(a, 0, sibling, me).wait_recv()
        for j, chip in enumerate(chips):
            for a in range(n):
                copy(a, 4 + j, (*chip, 1 - c), me).wait_recv()
        for cp in first + passed:
            cp.wait_send()
        for cp in mine:
            cp.wait()

    return pl.pallas_call(
        body, name=_nm("all_gather"),
        out_shape=tuple(jax.ShapeDtypeStruct((s.shape[0], N_DEV) + s.shape[1:], s.dtype) for s in shards),
        in_specs=[ANY] * n, out_specs=(ANY,) * n,
        scratch_shapes=[pltpu.SemaphoreType.DMA((7 * n,)), pltpu.SemaphoreType.DMA((7 * n,)), pltpu.SemaphoreType.DMA((n,))],
    )(*shards)


def _rs_pair_exchange(gs):
    n = len(gs)

    def body(*refs):
        g_refs, recv_refs = refs[:n], refs[n:2 * n]
        send_sems, recv_sems = refs[2 * n:]
        x, y, c = _place()
        copies = [pltpu.make_async_remote_copy(
            src_ref=g_refs[a].at[:, 2 * j + (1 - c)], dst_ref=recv_refs[a].at[j], send_sem=send_sems.at[4 * a + j],
            recv_sem=recv_sems.at[4 * a + j], device_id=(x, y, 1 - c), device_id_type=MESH)
            for a in range(n) for j in range(4)]
        for cp in copies:
            cp.start()
        for cp in copies:
            cp.wait_recv()
        for cp in copies:
            cp.wait_send()

    return pl.pallas_call(
        body, name=_nm("rs_pair"),
        out_shape=tuple(jax.ShapeDtypeStruct((4, g.shape[0]) + g.shape[2:], g.dtype) for g in gs),
        in_specs=[ANY] * n, out_specs=(ANY,) * n,
        scratch_shapes=[pltpu.SemaphoreType.DMA((4 * n,)), pltpu.SemaphoreType.DMA((4 * n,))],
    )(*gs)


def _rs_chip_exchange(ps):
    n = len(ps)

    def body(*refs):
        p_refs, recv_refs = refs[:n], refs[n:2 * n]
        send_sems, recv_sems = refs[2 * n:]
        x, y, c = _place()
        chips = [(1 - x, y), (x, 1 - y), (1 - x, 1 - y)]
        copies = [pltpu.make_async_remote_copy(
            src_ref=p_refs[a].at[2 * cx + cy], dst_ref=recv_refs[a].at[k], send_sem=send_sems.at[3 * a + k],
            recv_sem=recv_sems.at[3 * a + k], device_id=(cx, cy, c), device_id_type=MESH)
            for a in range(n) for k, (cx, cy) in enumerate(chips)]
        for cp in copies:
            cp.start()
        for cp in copies:
            cp.wait_recv()
        for cp in copies:
            cp.wait_send()

    return pl.pallas_call(
        body, name=_nm("rs_chip"),
        out_shape=tuple(jax.ShapeDtypeStruct((3,) + p.shape[1:], p.dtype) for p in ps),
        in_specs=[ANY] * n, out_specs=(ANY,) * n,
        scratch_shapes=[pltpu.SemaphoreType.DMA((3 * n,)), pltpu.SemaphoreType.DMA((3 * n,))],
    )(*ps)


def _rs_pair_add(g, recv, c_idx):
    L, _, a, b = g.shape
    ta = _row_divisor(a)

    def body(c_ref, g_ref, r_ref, o_ref):
        o_ref[...] = (g_ref[...].astype(F32) + r_ref[...].astype(F32)).astype(o_ref.dtype)

    grid_spec = pltpu.PrefetchScalarGridSpec(
        num_scalar_prefetch=1, grid=(4, L, a // ta),
        in_specs=[pl.BlockSpec((None, None, ta, b), lambda j, l, i, c_ref: (l, 2 * j + c_ref[0], i, 0)),
                  pl.BlockSpec((None, None, ta, b), lambda j, l, i, c_ref: (j, l, i, 0))],
        out_specs=pl.BlockSpec((None, None, ta, b), lambda j, l, i, c_ref: (j, l, i, 0)))
    return pl.pallas_call(
        body, name=_nm("rs_pair_add"), grid_spec=grid_spec, out_shape=jax.ShapeDtypeStruct((4, L, a, b), g.dtype),
        compiler_params=_cp(("parallel", "parallel", "parallel")),
    )(c_idx, g, recv)


def _rs_final_add(p1, recv, chip_idx):
    _, L, a, b = p1.shape
    ta = _row_divisor(a)

    def body(chip_ref, p_ref, r_ref, o_ref):
        acc = p_ref[...].astype(F32)
        for k in range(3):
            acc = acc + r_ref[k].astype(F32)
        o_ref[...] = acc

    grid_spec = pltpu.PrefetchScalarGridSpec(
        num_scalar_prefetch=1, grid=(L, a // ta),
        in_specs=[pl.BlockSpec((None, None, ta, b), lambda l, i, chip_ref: (chip_ref[0], l, i, 0)),
                  pl.BlockSpec((3, None, ta, b), lambda l, i, chip_ref: (0, l, i, 0))],
        out_specs=pl.BlockSpec((None, ta, b), lambda l, i, chip_ref: (l, i, 0)))
    return pl.pallas_call(
        body, name=_nm("rs_final_add"), grid_spec=grid_spec, out_shape=jax.ShapeDtypeStruct((L, a, b), F32),
        compiler_params=_cp(("parallel", "parallel")),
    )(chip_idx, p1, recv)


def reduce_scatter(gs):
    x, y, c = _place()
    c_idx = jnp.reshape(c, (1,)).astype(jnp.int32)
    chip_idx = jnp.reshape(2 * x + y, (1,)).astype(jnp.int32)
    recv1 = _rs_pair_exchange(gs)
    p1 = [_rs_pair_add(g, r, c_idx) for g, r in zip(gs, recv1)]
    recv2 = _rs_chip_exchange(p1)
    return [_rs_final_add(p, r, chip_idx) for p, r in zip(p1, recv2)]


def all_reduce_small(v):
    r, C = v.shape

    def body(v_ref, out_ref, buf_ref, send_sems, recv_sems):
        x, y, c = _place()
        my_id = 4 * x + 2 * y + c
        buf_ref[my_id] = v_ref[...]
        copies = []
        for k in range(1, N_DEV):
            fx, fy, fc = (k >> 2) & 1, (k >> 1) & 1, k & 1
            peer = (x ^ fx, y ^ fy, c ^ fc)
            copies.append(pltpu.make_async_remote_copy(
                src_ref=v_ref, dst_ref=buf_ref.at[my_id], send_sem=send_sems.at[k - 1], recv_sem=recv_sems.at[k - 1],
                device_id=peer, device_id_type=MESH))
        for cp in copies:
            cp.start()
        for cp in copies:
            cp.wait_recv()
        for cp in copies:
            cp.wait_send()
        acc = buf_ref[0]
        for d in range(1, N_DEV):
            acc = acc + buf_ref[d]
        out_ref[...] = acc

    vm = pl.BlockSpec(memory_space=pltpu.VMEM)
    return pl.pallas_call(
        body, name=_nm("all_reduce_small"), out_shape=jax.ShapeDtypeStruct((r, C), F32),
        in_specs=[vm], out_specs=vm,
        scratch_shapes=[pltpu.VMEM((N_DEV, r, C), F32), pltpu.SemaphoreType.DMA((7,)), pltpu.SemaphoreType.DMA((7,))],
    )(v)


def _slot(ref, px, py, pc):
    return ref.at[:, 4 * px + 2 * py + pc]


def comm_gather_own(shards):
    n = len(shards)

    def build(cin, cout, send_sems, recv_sems, local_sems):
        x, y, c = _place()
        me = (x, y, c)
        peers = [(x, y, 1 - c), (1 - x, y, c), (x, 1 - y, c), (1 - x, 1 - y, c)]
        starts, waits = [], []
        for a in range(n):
            local = pltpu.make_async_copy(cin[a], _slot(cout[a], *me), local_sems.at[a])
            starts.append(local)
            waits.append(local.wait)
            for k, peer in enumerate(peers):
                send = pltpu.make_async_remote_copy(
                    src_ref=cin[a], dst_ref=_slot(cout[a], *me), send_sem=send_sems.at[4 * a + k],
                    recv_sem=recv_sems.at[4 * a + k], device_id=peer, device_id_type=MESH)
                arrive = pltpu.make_async_remote_copy(
                    src_ref=cin[a], dst_ref=_slot(cout[a], *peer), send_sem=send_sems.at[4 * a + k],
                    recv_sem=recv_sems.at[4 * a + k], device_id=peer, device_id_type=MESH)
                starts.append(send)
                waits += [arrive.wait_recv, send.wait_send]
        return starts, waits

    out_shapes = [jax.ShapeDtypeStruct((s.shape[0], N_DEV) + s.shape[1:], s.dtype) for s in shards]
    return Comm(shards, out_shapes, {}, 4 * n, 4 * n, n, build)


def comm_gather_pass(partial):
    n = len(partial)

    def build(cin, cout, send_sems, recv_sems, local_sems):
        x, y, c = _place()
        chips = [(1 - x, y), (x, 1 - y), (1 - x, 1 - y)]
        starts, waits = [], []
        for a in range(n):
            for j, chip in enumerate(chips):
                send = pltpu.make_async_remote_copy(
                    src_ref=_slot(cout[a], *chip, c), dst_ref=_slot(cout[a], *chip, c), send_sem=send_sems.at[3 * a + j],
                    recv_sem=recv_sems.at[3 * a + j], device_id=(x, y, 1 - c), device_id_type=MESH)
                arrive = pltpu.make_async_remote_copy(
                    src_ref=_slot(cout[a], *chip, c), dst_ref=_slot(cout[a], *chip, 1 - c),
                    send_sem=send_sems.at[3 * a + j], recv_sem=recv_sems.at[3 * a + j],
                    device_id=(x, y, 1 - c), device_id_type=MESH)
                starts.append(send)
                waits += [arrive.wait_recv, send.wait_send]
        return starts, waits

    out_shapes = [jax.ShapeDtypeStruct(p.shape, p.dtype) for p in partial]
    return Comm(partial, out_shapes, {a: a for a in range(n)}, 3 * n, 3 * n, 1, build)


def comm_rs_pair(gs):
    n = len(gs)

    def build(cin, cout, send_sems, recv_sems, local_sems):
        x, y, c = _place()
        starts, waits = [], []
        for a in range(n):
            for j in range(4):
                cp = pltpu.make_async_remote_copy(
                    src_ref=cin[a].at[:, 2 * j + (1 - c)], dst_ref=cout[a].at[j], send_sem=send_sems.at[4 * a + j],
                    recv_sem=recv_sems.at[4 * a + j], device_id=(x, y, 1 - c), device_id_type=MESH)
                starts.append(cp)
                waits += [cp.wait_recv, cp.wait_send]
        return starts, waits

    out_shapes = [jax.ShapeDtypeStruct((4, g.shape[0]) + g.shape[2:], g.dtype) for g in gs]
    return Comm(gs, out_shapes, {}, 4 * n, 4 * n, 1, build)


def comm_rs_chip(ps):
    n = len(ps)

    def build(cin, cout, send_sems, recv_sems, local_sems):
        x, y, c = _place()
        chips = [(1 - x, y), (x, 1 - y), (1 - x, 1 - y)]
        starts, waits = [], []
        for a in range(n):
            for k, (cx, cy) in enumerate(chips):
                cp = pltpu.make_async_remote_copy(
                    src_ref=cin[a].at[2 * cx + cy], dst_ref=cout[a].at[k], send_sem=send_sems.at[3 * a + k],
                    recv_sem=recv_sems.at[3 * a + k], device_id=(cx, cy, c), device_id_type=MESH)
                starts.append(cp)
                waits += [cp.wait_recv, cp.wait_send]
        return starts, waits

    out_shapes = [jax.ShapeDtypeStruct((3,) + p.shape[1:], p.dtype) for p in ps]
    return Comm(ps, out_shapes, {}, 3 * n, 3 * n, 1, build)


def _pack(arrays, dtype, row_align):
    lead = arrays[0].shape[:-1]
    quantum = row_align * PACK_COLS
    parts, sizes = [], []
    for a in arrays:
        n = a.shape[-1]
        padded = _round_up(n, quantum)
        a = a.astype(dtype)
        if padded != n:
            a = jnp.pad(a, [(0, 0)] * len(lead) + [(0, padded - n)])
        parts.append(a.reshape(*lead, padded // PACK_COLS, PACK_COLS))
        sizes.append((n, padded // PACK_COLS))
    return jnp.concatenate(parts, axis=len(lead)), sizes


def _unpack(packed, sizes):
    lead = packed.shape[:-2]
    out, row = [], 0
    for n, rows in sizes:
        part = lax.slice_in_dim(packed, row, row + rows, axis=len(lead))
        out.append(part.reshape(*lead, rows * PACK_COLS)[..., :n])
        row += rows
    return out


def _unshard(gathered, axis):
    _, L, a, b = gathered.shape
    if axis == 1:
        return [gathered[:, l].reshape(N_DEV * a, b) for l in range(L)]
    return [jnp.transpose(gathered[:, l], (1, 0, 2)).reshape(a, N_DEV * b) for l in range(L)]


def _reshard(fulls, axis):
    blocks = []
    for f in fulls:
        A, B = f.shape
        if axis == 1:
            blocks.append(f.reshape(N_DEV, A // N_DEV, B))
        else:
            blocks.append(jnp.transpose(f.reshape(A, N_DEV, B // N_DEV), (1, 0, 2)))
    return jnp.stack(blocks, axis=1)


def _as3(a):
    return a if a.ndim == 3 else a[:, None, :]


def _prep_big(name, w, dims):
    w = w.astype(BF16)
    L, a, b = w.shape
    if name == 'mla_w_in':
        return jnp.pad(w, ((0, 0), (0, 0), (0, dims['h_width'] - b)))
    if name == 'mla_w_uq':
        hd = MLA_NOPE + MLA_ROPE
        w = jnp.pad(w.reshape(L, a, b // hd, hd), ((0, 0), (0, 0), (0, 0), (0, MLA_QK_PAD - hd)))
        return w.reshape(L, a, b // hd * MLA_QK_PAD)
    if name == 'ffn_w_up':
        return jnp.pad(w, ((0, 0), (0, 0), (0, _round_up(b, CONV_COLS) - b)))
    return w


def _unprep_big(name, g, shape):
    L, a, b = shape
    if name == 'mla_w_uq':
        hd = MLA_NOPE + MLA_ROPE
        return g.reshape(L, a, b // hd, MLA_QK_PAD)[..., :hd].reshape(L, a, b)
    return g[:, :, :b]


def _rope_tables(positions):
    inv = 1.0 / (ROPE_THETA ** (jnp.arange(0, MLA_ROPE, 2, dtype=F32) / MLA_ROPE))
    ang = positions.astype(F32)[:, None] * inv
    cos, sin = jnp.cos(ang), jnp.sin(ang)
    one, zero = jnp.ones_like(cos), jnp.zeros_like(cos)
    a = jnp.concatenate([cos, cos, one, one], axis=1)
    up = jnp.concatenate([sin, zero, zero, zero], axis=1)
    down = jnp.concatenate([zero, sin, zero, zero], axis=1)
    return (a, -up, down), (a, up, -down)


def _rows_full(w):
    return w.reshape(w.shape[0] * w.shape[1], w.shape[2])


def _ops(depth):
    alpha = (2 * depth) ** 0.25
    return {'ln_res': rw_op(_ln_res_fn(alpha), "ln_res", 2, [F32], shadow=True),
            'ple': rw_op(_ple_fn, "ple", 3, [F32], shadow=True), 'gla_gate': rw_op(_gla_gate_fn, "gla_gate", 1, [F32])}


MLA_PRE_W = ['mla_w_in', 'mla_q_norm', 'mla_kv_norm', 'mla_w_uq', 'mla_w_uk', 'mla_w_uv']
FFN_IN_W = ['mla_w_o', 'ln1_g', 'ln1_b', 'ffn_conv_w', 'ffn_conv_b']
FFN_OUT_W = ['ffn_w_down', 'ln2_g', 'ln2_b', 'ple_w_gate', 'ple_w_proj', 'ple_b_gate']


def _mla_heads(wl, j):
    return N_DEV * wl['mla_w_uk'][j].shape[2] // MLA_NOPE


def _mla_pre(x, xb, wl, j):
    w_uq, w_uk, w_uv = wl['mla_w_uq'][j], wl['mla_w_uk'][j], wl['mla_w_uv'][j]
    h = linear_sh(x, xb, _rows_full(wl['mla_w_in'][j]), F32)
    mla_norm = rw_op(_mla_norm_fn(w_uq.shape[1], w_uk.shape[1]), "mla_norm", 1, [BF16, BF16, F32])
    cq, ckv, kr_raw = mla_norm(h, wl['mla_q_norm'][j], wl['mla_kv_norm'][j])
    return linear(cq, w_uq, BF16), linear(ckv, w_uk, BF16), linear(ckv, w_uv, BF16), kr_raw


def _gla_mixer(x, xb, wl, j, ops):
    w_in3, w_a2 = wl['gla_w_in'][j], wl['gla_w_a2'][j]
    w_o = _rows_full(wl['gla_w_o'][j])
    w_in = jnp.transpose(w_in3, (1, 0, 2)).reshape(w_in3.shape[1], N_DEV * w_in3.shape[2])
    heads = w_o.shape[0] // GLA_DV
    n_main = 2 * heads * GLA_DK + 2 * heads * GLA_DV
    w_a = jnp.pad(w_in[:, n_main:], ((0, 0), (0, LANES - GLA_RANK)))
    w_a2_p = jnp.pad(w_a2, ((0, LANES - GLA_RANK), (0, 0))).astype(BF16)
    hm = linear_sh(x, xb, w_in[:, :n_main], BF16)
    ha = linear_sh(x, xb, w_a, BF16)
    (la,) = ops['gla_gate'](linear(ha, w_a2_p, F32), wl['gla_b_a'][j])
    return linear(gla_core(hm, la, wl['gla_o_norm'][j], heads), w_o, F32)


def _ffn_in(x, m, wl, i, ops, bp):
    x1, x1b = ops['ln_res'](x, m, wl['ln1_g'][i], wl['ln1_b'][i])
    cw, cb = wl['ffn_conv_w'][i], wl['ffn_conv_b'][i]
    bu = cw.shape[1] // N_DEV
    cwp = jnp.pad(cw.reshape(CONV_W, N_DEV, bu), ((0, 0), (0, 0), (0, bp - bu))).reshape(CONV_W, N_DEV * bp)
    cbp = jnp.pad(cb.reshape(1, N_DEV, bu), ((0, 0), (0, 0), (0, bp - bu))).reshape(1, N_DEV * bp)
    return x1, lax.stop_gradient(x1b), cwp, cbp


def _ffn_out(x1, a, wl, p_i, i, ops):
    w_down3 = wl['ffn_w_down'][i]
    half, bu, d_model = N_DEV // 2, 2 * w_down3.shape[1], w_down3.shape[2]
    bp = a.shape[1] // half
    w_down = jnp.pad(w_down3.reshape(half, bu, d_model), ((0, 0), (0, bp - bu), (0, 0))).reshape(half * bp, d_model)
    f = linear(a, w_down, F32)
    x2, x2b = ops['ln_res'](x1, f, wl['ln2_g'][i], wl['ln2_b'][i])
    glog = linear_sh(x2, lax.stop_gradient(x2b), _rows_full(wl['ple_w_gate'][i]), F32)
    pp = linear(p_i, wl['ple_w_proj'][i], F32)
    x, xb = ops['ple'](x2, glog, pp, wl['ple_b_gate'][i])
    return x, lax.stop_gradient(xb)


def _layer(x, xb, wl, p_i, aux, i, ops):
    j = i // 2
    if i % 2 == 0:
        q_raw, kn, v, kr_raw = _mla_pre(x, xb, wl, j)
        o = make_attention(aux, _mla_heads(wl, j))[0](q_raw, kn, v, kr_raw)
        m = linear(o, _rows_full(wl['mla_w_o'][j]), F32)
    else:
        m = _gla_mixer(x, xb, wl, j, ops)
    x1, x1b, cwp, cbp = _ffn_in(x, m, wl, i, ops, wl['ffn_w_up'][i].shape[2])
    return _ffn_out(x1, ffn_hidden(x1, x1b, wl['ffn_w_up'][i], cwp, cbp), wl, p_i, i, ops)


def kernel(x, p, positions, mla_w_in, mla_q_norm, mla_kv_norm, mla_w_uq, mla_w_uk, mla_w_uv, mla_w_o, gla_w_in, gla_w_a2, gla_b_a, gla_o_norm, gla_w_o, ln1_g, ln1_b, ln2_g, ln2_b, ffn_w_up, ffn_conv_w, ffn_conv_b, ffn_w_down, ple_w_proj, ple_w_gate, ple_b_gate, loss_target, m_mla_w_in, m_mla_q_norm, m_mla_kv_norm, m_mla_w_uq, m_mla_w_uk, m_mla_w_uv, m_mla_w_o, m_gla_w_in, m_gla_w_a2, m_gla_b_a, m_gla_o_norm, m_gla_w_o, m_ln1_g, m_ln1_b, m_ln2_g, m_ln2_b, m_ffn_w_up, m_ffn_conv_w, m_ffn_conv_b, m_ffn_w_down, m_ple_w_proj, m_ple_w_gate, m_ple_b_gate, v_mla_w_in, v_mla_q_norm, v_mla_kv_norm, v_mla_w_uq, v_mla_w_uk, v_mla_w_uv, v_mla_w_o, v_gla_w_in, v_gla_w_a2, v_gla_b_a, v_gla_o_norm, v_gla_w_o, v_ln1_g, v_ln1_b, v_ln2_g, v_ln2_b, v_ffn_w_up, v_ffn_conv_w, v_ffn_conv_b, v_ffn_w_down, v_ple_w_proj, v_ple_w_gate, v_ple_b_gate):
    w = dict(zip(WEIGHTS, (mla_w_in, mla_q_norm, mla_kv_norm, mla_w_uq, mla_w_uk, mla_w_uv, mla_w_o, gla_w_in, gla_w_a2,
                           gla_b_a, gla_o_norm, gla_w_o, ln1_g, ln1_b, ln2_g, ln2_b, ffn_w_up, ffn_conv_w, ffn_conv_b,
                           ffn_w_down, ple_w_proj, ple_w_gate, ple_b_gate)))
    m_in = dict(zip(WEIGHTS, (m_mla_w_in, m_mla_q_norm, m_mla_kv_norm, m_mla_w_uq, m_mla_w_uk, m_mla_w_uv, m_mla_w_o,
                              m_gla_w_in, m_gla_w_a2, m_gla_b_a, m_gla_o_norm, m_gla_w_o, m_ln1_g, m_ln1_b, m_ln2_g,
                              m_ln2_b, m_ffn_w_up, m_ffn_conv_w, m_ffn_conv_b, m_ffn_w_down, m_ple_w_proj, m_ple_w_gate,
                              m_ple_b_gate)))
    v_in = dict(zip(WEIGHTS, (v_mla_w_in, v_mla_q_norm, v_mla_kv_norm, v_mla_w_uq, v_mla_w_uk, v_mla_w_uv, v_mla_w_o,
                              v_gla_w_in, v_gla_w_a2, v_gla_b_a, v_gla_o_norm, v_gla_w_o, v_ln1_g, v_ln1_b, v_ln2_g,
                              v_ln2_b, v_ffn_w_up, v_ffn_conv_w, v_ffn_conv_b, v_ffn_w_down, v_ple_w_proj, v_ple_w_gate,
                              v_ple_b_gate)))
    _uid[0] = itertools.count()
    x2d, target, pos = x[0], loss_target[0], positions[0]
    p3 = p[:, 0]
    dims = {'h_width': mla_w_uq.shape[1] + mla_w_uk.shape[1] + LANES}

    depth = ln1_g.shape[0]
    ops = _ops(depth)
    cid = pos // CHUNK
    aux = {'rope': _rope_tables(pos), 'cidq': cid[:, None], 'cidk': cid[None, :],
           'need': _mask_table(cid, _attn_tile(pos.shape[0]))}

    in_layer0 = set(MLA_PRE_W + FFN_IN_W + FFN_OUT_W + ['ffn_w_up'])
    prepped = {n: _prep_big(n, w[n], dims) for n in BIG}
    first_names = [n for n in BIG if n in MLA_PRE_W + ['mla_w_o']]
    mid_names = [n for n in BIG if n in in_layer0 and n not in first_names]
    rest_names = [n for n in BIG if prepped[n].shape[0] > (1 if n in in_layer0 else 0)]
    rest_from = {n: (1 if n in in_layer0 else 0) for n in rest_names}
    small3 = [_as3(w[n]) for n in SMALL]
    small_packed, small_sizes = _pack([s.reshape(1, -1) for s in small3], F32, 8)
    first = all_gather([prepped[n][:1] for n in first_names] + [small_packed])
    wl = {n: [None] * prepped[n].shape[0] for n in BIG}
    for n, g in zip(first_names, first):
        wl[n][0] = g[0]
    for n, s3, flat in zip(SMALL, small3, _unpack(first[-1][0], small_sizes)):
        wl[n] = _unshard(flat.reshape(N_DEV, *s3.shape), SHARD_AXIS[n] if w[n].ndim == 3 else 2)
    for n in REPL:
        wl[n] = [w[n][l][None, :] for l in range(w[n].shape[0])]

    def pick(names, layer0):
        return {n: [wl[n][l] if (l == 0 and n in in_layer0) == layer0 else None for l in range(len(wl[n]))] for n in names}

    heads0 = _mla_heads(wl, 0)
    _, attn_run, attn_bwd = make_attention(aux, heads0)
    x2db = x2d.astype(BF16)
    pre, vjp_pre = jax.vjp(lambda x_, wl_: _mla_pre(x_, x2db, wl_, 0), x2d, pick(MLA_PRE_W, True))
    o, attn_res, partial = attn_run(*pre, comm_gather_own([prepped[n][:1] for n in mid_names]
                                                          + [prepped[n][rest_from[n]:] for n in rest_names]))
    for n, g in zip(mid_names, _run_comm(comm_gather_pass(partial[:len(mid_names)]), "gather_pass")):
        wl[n][0] = g[0]
    partial = partial[len(mid_names):]
    w_up0 = wl['ffn_w_up'][0]

    def ffn_in(x_, o_, wl_):
        x1_, x1b_, cwp_, cbp_ = _ffn_in(x_, linear(o_, _rows_full(wl_['mla_w_o'][0]), F32), wl_, 0, ops, w_up0.shape[2])
        return (x1_, cwp_, cbp_), x1b_

    (x1, cwp, cbp), vjp_in, x1b = jax.vjp(ffn_in, x2d, o, pick(FFN_IN_W, True), has_aux=True)
    a, ffn_res, rest = _ffn_hidden_fwd(x1, x1b, w_up0, cwp, cbp, comm_gather_pass(partial))
    for n, g in zip(rest_names, rest):
        for l in range(g.shape[0]):
            wl[n][rest_from[n] + l] = g[l]
    x_l0, vjp_out, x_l0b = jax.vjp(lambda x1_, a_, wl_: _ffn_out(x1_, a_, wl_, p3[0], 0, ops), x1, a,
                                   pick(FFN_OUT_W, True), has_aux=True)

    def tail(x_, wl_):
        xb_ = x_l0b
        for i in range(1, depth):
            x_, xb_ = _layer(x_, xb_, wl_, p3[i], aux, i, ops)
        return x_

    y, vjp_tail = jax.vjp(tail, x_l0, pick(WEIGHTS, False))
    sq, dy = _loss_call(y, target)

    dwl = {n: [None] * len(wl[n]) for n in WEIGHTS}

    def keep(part):
        for n, per_layer in part.items():
            for l, g in enumerate(per_layer):
                if g is not None:
                    dwl[n][l] = g

    x_c, y_c, c_place = _place()
    c_idx = jnp.reshape(c_place, (1,)).astype(jnp.int32)
    chip_idx = jnp.reshape(2 * x_c + y_c, (1,)).astype(jnp.int32)
    dx_l0, d_tail = vjp_tail(dy)
    keep(d_tail)
    g_rest = [jnp.stack(dwl[n][rest_from[n]:], axis=0) for n in rest_names]
    dx1_out, da, d_out = vjp_out(dx_l0)
    keep(d_out)
    (dx1_ffn, _, dw_up0, dcwp, dcbp), recv1 = _ffn_hidden_bwd(ffn_res, da, comm_rs_pair(g_rest))
    dwl['ffn_w_up'][0] = dw_up0
    g_mid = [dwl[n][0][None] for n in mid_names]
    g_rest, recv1 = g_rest + g_mid, list(recv1) + list(_rs_pair_exchange(g_mid))
    p1 = [_rs_pair_add(g, r, c_idx) for g, r in zip(g_rest, recv1)]
    dx_in, do, d_in = vjp_in((dx1_out + dx1_ffn, dcwp, dcbp))
    keep(d_in)
    d_pre_in, recv2 = attn_bwd(attn_res, do, comm_rs_chip(p1))
    red_rest = [_rs_final_add(p_, r, chip_idx) for p_, r in zip(p1, recv2)]
    dx_pre, d_pre = vjp_pre(d_pre_in)
    keep(d_pre)
    dx = dx_pre + dx_in

    small_blocks = [_reshard(dwl[n], SHARD_AXIS[n] if w[n].ndim == 3 else 2).reshape(N_DEV, -1) for n in SMALL]
    small_grad_packed, _ = _pack(small_blocks, F32, 8)
    red_first = reduce_scatter([dwl[n][0][None] for n in first_names] + [small_grad_packed[None]])
    by_layer = {n: [] for n in BIG}
    for n, g in zip(first_names, red_first):
        by_layer[n].append(g)
    for n, g in zip(mid_names, red_rest[len(rest_names):]):
        by_layer[n].append(g)
    for n, g in zip(rest_names, red_rest):
        by_layer[n].append(g)
    grads = {n: _unprep_big(n, jnp.concatenate(by_layer[n], axis=0), w[n].shape) for n in BIG}
    for n, f in zip(SMALL, _unpack(red_first[-1][0], small_sizes)):
        grads[n] = f.reshape(w[n].shape)

    repl_flat = [jnp.concatenate([g.reshape(-1) for g in dwl[n]]).reshape(1, -1) for n in REPL]
    loss_part = 0.5 * jnp.sum(sq) / sq.shape[1]
    packed, repl_sizes = _pack(repl_flat + [loss_part.reshape(1, 1)], F32, 8)
    summed = _unpack(all_reduce_small(packed[0])[None], repl_sizes)
    for n, f in zip(REPL, summed[:-1]):
        grads[n] = f.reshape(w[n].shape)
    loss = summed[-1].reshape(())

    delta, new_m, new_v = {}, {}, {}
    for n in WEIGHTS:
        delta[n], new_m[n], new_v[n] = _adamw_call(w[n], grads[n], m_in[n], v_in[n])
    return (loss, dx[None], *[grads[n] for n in WEIGHTS], *[delta[n] for n in WEIGHTS],
            *[new_m[n] for n in WEIGHTS], *[new_v[n] for n in WEIGHTS])
```

```python
import functools
import itertools

import jax
import jax.numpy as jnp
from jax import lax
from jax.experimental import pallas as pl
from jax.experimental.pallas import tpu as pltpu

F32 = jnp.float32
BF16 = jnp.bfloat16
MESH = pl.DeviceIdType.MESH
N_DEV = 8

EPS = 1e-5
NEG_INF = -1e30
CHUNK = 64
Q_BLOCK = 128
MLA_NOPE = 128
MLA_ROPE = 64
MLA_V = 128
MLA_QK_PAD = 256
ROPE_THETA = 10000.0
GLA_DK = 128
GLA_DV = 256
GLA_RANK = 16
GLA_TAU = 16.0
CONV_W = 3
ADAM_LR = 0.001
ADAM_B1 = 0.9
ADAM_B2 = 0.999
ADAM_EPS = 1e-08
ADAM_WD = 0.01
ADAM_STEP = 10
LOG2E = 1.4426950408889634

LANES = 128
PACK_COLS = 1024
VMEM_LIMIT = 56 * 1024 * 1024
MM_VMEM_BUDGET = 38 * 1024 * 1024

WEIGHTS = ['mla_w_in', 'mla_q_norm', 'mla_kv_norm', 'mla_w_uq', 'mla_w_uk', 'mla_w_uv', 'mla_w_o', 'gla_w_in',
           'gla_w_a2', 'gla_b_a', 'gla_o_norm', 'gla_w_o', 'ln1_g', 'ln1_b', 'ln2_g', 'ln2_b', 'ffn_w_up',
           'ffn_conv_w', 'ffn_conv_b', 'ffn_w_down', 'ple_w_proj', 'ple_w_gate', 'ple_b_gate']
SHARD_AXIS = {'mla_w_in': 1, 'mla_q_norm': None, 'mla_kv_norm': None, 'mla_w_uq': 2, 'mla_w_uk': 2, 'mla_w_uv': 2,
              'mla_w_o': 1, 'gla_w_in': 2, 'gla_w_a2': 2, 'gla_b_a': 1, 'gla_o_norm': 1, 'gla_w_o': 1,
              'ln1_g': None, 'ln1_b': None, 'ln2_g': None, 'ln2_b': None, 'ffn_w_up': 2, 'ffn_conv_w': 2,
              'ffn_conv_b': None, 'ffn_w_down': 1, 'ple_w_proj': 2, 'ple_w_gate': 1, 'ple_b_gate': None}
BIG = ['mla_w_in', 'mla_w_uq', 'mla_w_uk', 'mla_w_uv', 'mla_w_o', 'gla_w_in', 'gla_w_o', 'ffn_w_up', 'ffn_w_down',
       'ple_w_proj', 'ple_w_gate']
SMALL = ['gla_w_a2', 'gla_b_a', 'gla_o_norm', 'ffn_conv_w']
REPL = [n for n in WEIGHTS if SHARD_AXIS[n] is None]

_uid = [itertools.count()]


def _nm(base):
    return f"{base}_{next(_uid[0])}"


def _cp(sem=None):
    return pltpu.CompilerParams(dimension_semantics=sem, vmem_limit_bytes=VMEM_LIMIT)


def _round_up(n, m):
    return -(-n // m) * m


class Comm:
    MID_AT = 0.6

    def __init__(self, inputs, out_shapes, aliases, n_send, n_recv, n_local, build):
        self.inputs, self.out_shapes, self.aliases = list(inputs), list(out_shapes), dict(aliases)
        self.n_send, self.n_recv, self.n_local, self.build = n_send, n_recv, n_local, build


def _pcall(body, name, grid, n_prefetch, in_specs, out_specs, out_shape, scratch, sem, args, comm=None):
    in_specs, out_specs, out_shape, scratch, args = list(in_specs), list(out_specs), list(out_shape), list(scratch), list(args)
    n_in, n_out, n_scr = len(in_specs), len(out_specs), len(scratch)
    aliases = {}
    kernel_body = body
    if comm is not None:
        ci, co = len(comm.inputs), len(comm.out_shapes)
        any_spec = pl.BlockSpec(memory_space=pl.ANY)
        in_specs += [any_spec] * ci
        out_specs += [any_spec] * co
        out_shape += comm.out_shapes
        scratch += [pltpu.SemaphoreType.DMA((comm.n_send,)), pltpu.SemaphoreType.DMA((comm.n_recv,)),
                    pltpu.SemaphoreType.DMA((comm.n_local,))]
        aliases = {n_prefetch + n_in + k: n_out + v for k, v in comm.aliases.items()}
        args += comm.inputs
        sem = ("arbitrary",) * len(grid)

        def kernel_body(*refs):
            pre, r = refs[:n_prefetch], refs[n_prefetch:]
            ins, cin = r[:n_in], r[n_in:n_in + ci]
            outs, cout = r[n_in + ci:n_in + ci + n_out], r[n_in + ci + n_out:n_in + ci + n_out + co]
            scr = r[n_in + ci + n_out + co:n_in + ci + n_out + co + n_scr]
            send_sems, recv_sems, local_sems = r[-3:]
            first = functools.reduce(lambda a, b: a & b, [pl.program_id(d) == 0 for d in range(len(grid))])
            last = functools.reduce(lambda a, b: a & b, [pl.program_id(d) == grid[d] - 1 for d in range(len(grid))])
            built = comm.build(cin, cout, send_sems, recv_sems, local_sems)
            starts, waits = built[0], built[-1]
            mid_waits, mid_starts = (built[1], built[2]) if len(built) == 4 else ([], [])

            @pl.when(first)
            def _():
                for cp in starts:
                    cp.start()

            if mid_starts:
                step = pl.program_id(0)
                for d in range(1, len(grid)):
                    step = step * grid[d] + pl.program_id(d)
                n_steps = functools.reduce(lambda a, b: a * b, grid)

                @pl.when(step == int(n_steps * comm.MID_AT))
                def _():
                    for wait in mid_waits:
                        wait()
                    for cp in mid_starts:
                        cp.start()

            body(*pre, *ins, *outs, *scr)

            @pl.when(last)
            def _():
                for wait in waits:
                    wait()

    grid_spec = pltpu.PrefetchScalarGridSpec(num_scalar_prefetch=n_prefetch, grid=grid, in_specs=in_specs,
                                             out_specs=out_specs, scratch_shapes=scratch)
    return pl.pallas_call(kernel_body, name=_nm(name), grid_spec=grid_spec, out_shape=tuple(out_shape),
                          input_output_aliases=aliases, compiler_params=_cp(sem))(*args)


def _run_comm(comm, name):
    return _pcall(lambda: None, name, (1,), 0, [], [], [], [], ("arbitrary",), [], comm)


def _divisor_tiles(n, cap):
    if n % LANES:
        return [n]
    out = [t for t in range(LANES, min(n, cap) + 1, LANES) if n % t == 0]
    return out or [n]


def _mm_tiles(M, N, K, abytes, bbytes, obytes, tn_fixed=None, tk_fixed=None):
    best = None
    for tm in _divisor_tiles(M, 1024):
        for tn in ([tn_fixed] if tn_fixed else _divisor_tiles(N, 1536)):
            for tk in ([tk_fixed] if tk_fixed else _divisor_tiles(K, 4096)):
                vmem = 2 * (tm * tk * abytes + tk * tn * bbytes + tm * tn * obytes) + tm * tn * 4
                if vmem > MM_VMEM_BUDGET:
                    continue
                key = (tm * tn * tk, tk)
                if best is None or key > best[0]:
                    best = (key, (tm, tn, tk))
    assert best is not None, (M, N, K)
    return best[1]


def _mm(a, b, mode, out_dtype, base="mm", blocks=None):
    blk0, nblk = blocks if blocks else (0, 1)
    tn_fixed = tk_fixed = None
    if mode == "nn":
        M, K = a.shape
        N = nblk * b.shape[2] if blocks else b.shape[1]
        tn_fixed = b.shape[2] if blocks else None
    elif mode == "nt":
        M, K = a.shape
        N = b.shape[1] if blocks else b.shape[0]
        tk_fixed = b.shape[2] if blocks else None
        assert not blocks or K == nblk * b.shape[2]
    else:
        (K, M), N = a.shape, b.shape[1]
        tn_fixed = N // nblk if blocks else None
    tm, tn, tk = _mm_tiles(M, N, K, a.dtype.itemsize, b.dtype.itemsize, jnp.dtype(out_dtype).itemsize, tn_fixed, tk_fixed)
    nk = K // tk
    out_shape = jax.ShapeDtypeStruct((M, N), out_dtype)
    out_spec = pl.BlockSpec((tm, tn), lambda i, j, k: (i, j))
    if mode == "nn":
        a_spec = pl.BlockSpec((tm, tk), lambda i, j, k: (i, k))
        b_spec = (pl.BlockSpec((None, tk, tn), lambda i, j, k: (blk0 + j, k, 0)) if blocks
                  else pl.BlockSpec((tk, tn), lambda i, j, k: (k, j)))
        dims = (((1,), (0,)), ((), ()))
    elif mode == "nt":
        a_spec = pl.BlockSpec((tm, tk), lambda i, j, k: (i, k))
        b_spec = (pl.BlockSpec((None, tn, tk), lambda i, j, k: (blk0 + k, j, 0)) if blocks
                  else pl.BlockSpec((tn, tk), lambda i, j, k: (j, k)))
        dims = (((1,), (1,)), ((), ()))
    else:
        a_spec = pl.BlockSpec((tk, tm), lambda i, j, k: (k, i))
        b_spec = pl.BlockSpec((tk, tn), lambda i, j, k: (k, j))
        dims = (((0,), (0,)), ((), ()))
        if blocks:
            out_shape = jax.ShapeDtypeStruct((nblk, M, tn), out_dtype)
            out_spec = pl.BlockSpec((None, tm, tn), lambda i, j, k: (j, i, 0))

    def body(a_ref, b_ref, o_ref, acc_ref):
        part = lax.dot_general(a_ref[...].astype(BF16), b_ref[...].astype(BF16), dims, preferred_element_type=F32)
        if nk == 1:
            o_ref[...] = part.astype(o_ref.dtype)
        else:
            k = pl.program_id(2)

            @pl.when(k == 0)
            def _():
                acc_ref[...] = part

            @pl.when(k > 0)
            def _():
                acc_ref[...] += part

            @pl.when(k == nk - 1)
            def _():
                o_ref[...] = acc_ref[...].astype(o_ref.dtype)

    return pl.pallas_call(
        body, name=_nm(base), grid=(M // tm, N // tn, nk), out_shape=out_shape,
        in_specs=[a_spec, b_spec], out_specs=out_spec,
        scratch_shapes=[pltpu.VMEM((tm, tn) if nk > 1 else (8, LANES), F32)],
        compiler_params=_cp(("parallel", "parallel", "arbitrary")),
    )(a, b)


def _all_blocks(w):
    return (0, w.shape[0]) if w.ndim == 3 else None


@functools.partial(jax.custom_vjp, nondiff_argnums=(2,))
def linear(a, w, out_dtype):
    return _mm(a, w, "nn", out_dtype, "lin_fwd", _all_blocks(w))


def _linear_fwd(a, w, out_dtype):
    return _mm(a, w, "nn", out_dtype, "lin_fwd", _all_blocks(w)), (a, w)


def _linear_bwd(out_dtype, res, dy):
    a, w = res
    return (_mm(dy, w, "nt", a.dtype, "lin_dx", _all_blocks(w)), _mm(a, dy, "tn", w.dtype, "lin_dw", _all_blocks(w)))


linear.defvjp(_linear_fwd, _linear_bwd)


@functools.partial(jax.custom_vjp, nondiff_argnums=(3,))
def linear_sh(a, a_bf16, w, out_dtype):
    return _mm(a_bf16, w, "nn", out_dtype, "lin_fwd", _all_blocks(w))


def _linear_sh_fwd(a, a_bf16, w, out_dtype):
    return _mm(a_bf16, w, "nn", out_dtype, "lin_fwd", _all_blocks(w)), (a_bf16, w, jnp.zeros((), a.dtype))


def _linear_sh_bwd(out_dtype, res, dy):
    a_bf16, w, tok = res
    return (_mm(dy, w, "nt", tok.dtype, "lin_dx", _all_blocks(w)), jnp.zeros_like(a_bf16),
            _mm(a_bf16, dy, "tn", w.dtype, "lin_dw", _all_blocks(w)))


linear_sh.defvjp(_linear_sh_fwd, _linear_sh_bwd)


def _row_tile(S, width):
    tr = 512 if width <= 1024 else 256
    return min(tr, S)


def _rw_fwd(f, rows, params, out_dtypes, base):
    S = rows[0][0].shape[0]
    tr = _row_tile(S, max(w for _, w, _ in rows))
    n_in = len(rows) + len(params)
    avals = [jax.ShapeDtypeStruct((tr, w), F32) for _, w, _ in rows] + [jax.ShapeDtypeStruct(p.shape, F32) for p in params]
    outs = jax.eval_shape(f, *avals)

    def body(*refs):
        vals = [r[...].astype(F32) for r in refs[:n_in]]
        for o_ref, r in zip(refs[n_in:], f(*vals)):
            o_ref[...] = r.astype(o_ref.dtype)

    in_specs = [pl.BlockSpec((tr, w), functools.partial(lambda i, cb: (i, cb), cb=cb)) for _, w, cb in rows]
    in_specs += [pl.BlockSpec(p.shape, lambda i: (0, 0)) for p in params]
    return pl.pallas_call(
        body, name=_nm(base), grid=(S // tr,),
        out_shape=tuple(jax.ShapeDtypeStruct((S, o.shape[1]), dt) for o, dt in zip(outs, out_dtypes)),
        in_specs=in_specs, out_specs=tuple(pl.BlockSpec((tr, o.shape[1]), lambda i: (i, 0)) for o in outs),
        compiler_params=_cp(("parallel",)),
    )(*[a for a, _, _ in rows], *params)


def _rw_bwd(f, rows, params, cts, row_grad_dtypes, base):
    S = rows[0][0].shape[0]
    tr = _row_tile(S, max(w for _, w, _ in rows))
    n_rows, n_par, n_ct = len(rows), len(params), len(cts)
    want = [k for k, dt in enumerate(row_grad_dtypes) if dt is not None]

    def body(*refs):
        in_refs = refs[:n_rows + n_par]
        ct_refs = refs[n_rows + n_par:n_rows + n_par + n_ct]
        out_refs = refs[n_rows + n_par + n_ct:]
        vals = [r[...].astype(F32) for r in in_refs]
        _, vjp_fn = jax.vjp(f, *vals)
        grads = vjp_fn(tuple(c[...].astype(F32) for c in ct_refs))
        for o_ref, k in zip(out_refs[:len(want)], want):
            o_ref[...] = grads[k].astype(o_ref.dtype)
        i = pl.program_id(0)
        for o_ref, g in zip(out_refs[len(want):], grads[n_rows:]):
            @pl.when(i == 0)
            def _(o_ref=o_ref, g=g):
                o_ref[...] = g

            @pl.when(i > 0)
            def _(o_ref=o_ref, g=g):
                o_ref[...] += g

    in_specs = [pl.BlockSpec((tr, w), functools.partial(lambda i, cb: (i, cb), cb=cb)) for _, w, cb in rows]
    in_specs += [pl.BlockSpec(p.shape, lambda i: (0, 0)) for p in params]
    in_specs += [pl.BlockSpec((tr, c.shape[1]), lambda i: (i, 0)) for c in cts]
    out_shape = [jax.ShapeDtypeStruct((S, rows[k][1]), row_grad_dtypes[k]) for k in want]
    out_specs = [pl.BlockSpec((tr, rows[k][1]), lambda i: (i, 0)) for k in want]
    out_shape += [jax.ShapeDtypeStruct(p.shape, F32) for p in params]
    out_specs += [pl.BlockSpec(p.shape, lambda i: (0, 0)) for p in params]
    res = pl.pallas_call(
        body, name=_nm(base), grid=(S // tr,), out_shape=tuple(out_shape),
        in_specs=in_specs, out_specs=tuple(out_specs), compiler_params=_cp(("arbitrary",)),
    )(*[a for a, _, _ in rows], *params, *cts)
    row_grads = [None] * n_rows
    for k, g in zip(want, res[:len(want)]):
        row_grads[k] = g
    return row_grads, list(res[len(want):])


def rw_op(f, base, n_rows, out_dtypes, shadow=False):
    f_fwd = (lambda *a: (lambda r: tuple(r) + (r[0],))(f(*a))) if shadow else f
    fwd_dtypes = list(out_dtypes) + ([BF16] if shadow else [])

    @jax.custom_vjp
    def op(*args):
        return fwd(*args)[0]

    def split(args):
        rows = [(a, a.shape[1], 0) for a in args[:n_rows]]
        return rows, list(args[n_rows:])

    def fwd(*args):
        rows, params = split(args)
        return tuple(_rw_fwd(f_fwd, rows, params, fwd_dtypes, base + "_fwd")), args

    def bwd(args, cts):
        rows, params = split(args)
        cts = list(cts)[:len(out_dtypes)]
        rg, pg = _rw_bwd(f, rows, params, cts, [a.dtype for a, _, _ in rows], base + "_bwd")
        return tuple(rg) + tuple(g.astype(p.dtype) for g, p in zip(pg, params))

    op.defvjp(fwd, bwd)
    return op


def _ln_res_fn(alpha):
    def f(x, m, g, b):
        z = alpha * x + m
        mu = jnp.mean(z, -1, keepdims=True)
        zc = z - mu
        var = jnp.mean(zc * zc, -1, keepdims=True)
        return (zc * lax.rsqrt(var + EPS) * g + b,)
    return f


def _rms(x, g):
    return x * lax.rsqrt(jnp.mean(x * x, -1, keepdims=True) + EPS) * g


def _mla_norm_fn(q_lora, kv_lora):
    def f(h, qn, kvn):
        return (_rms(h[:, :q_lora], qn), _rms(h[:, q_lora:q_lora + kv_lora], kvn),
                h[:, q_lora + kv_lora:q_lora + kv_lora + LANES])
    return f


def _log_sigmoid(z):
    return jnp.minimum(z, 0.0) - jnp.log(1.0 + jnp.exp(-jnp.abs(z)))


def _gla_gate_fn(z, b):
    return (_log_sigmoid(z + b) / GLA_TAU,)


def _ple_fn(x, glog, pp, b):
    return (x + jax.nn.sigmoid(glog + b) * pp,)


def _gla_out_fn(heads):
    def f(o, r, g):
        parts = []
        for h in range(heads):
            oh = o[:, h * GLA_DV:(h + 1) * GLA_DV]
            mu = jnp.mean(oh, -1, keepdims=True)
            oc = oh - mu
            var = jnp.mean(oc * oc, -1, keepdims=True)
            parts.append(oc * lax.rsqrt(var + EPS) * g[:, h * GLA_DV:(h + 1) * GLA_DV])
        return (jnp.concatenate(parts, axis=1) * (r * jax.nn.sigmoid(r)),)
    return f


def _rope_call(x, tabs, roped, out_dtype, base, fold=False):
    S, C = x.shape
    nb = C // LANES
    tr = min(512 if C <= 1024 else 256, S)
    out_c = LANES if fold else C

    def rot(v, a, b1, b2):
        return v * a + pltpu.roll(v, 96, 1) * b1 + pltpu.roll(v, 32, 1) * b2

    def body(x_ref, a_ref, b1_ref, b2_ref, o_ref):
        a, b1, b2 = a_ref[...], b1_ref[...], b2_ref[...]
        if fold:
            v = x_ref[:, 0:LANES].astype(F32)
            for blk in range(1, nb):
                v = v + x_ref[:, blk * LANES:(blk + 1) * LANES].astype(F32)
            o_ref[...] = rot(v, a, b1, b2).astype(o_ref.dtype)
            return
        for blk in range(nb):
            v = x_ref[:, blk * LANES:(blk + 1) * LANES].astype(F32)
            if roped(blk):
                v = rot(v, a, b1, b2)
            o_ref[:, blk * LANES:(blk + 1) * LANES] = v.astype(o_ref.dtype)

    row = lambda w: pl.BlockSpec((tr, w), lambda i: (i, 0))
    return pl.pallas_call(
        body, name=_nm(base), grid=(S // tr,), out_shape=jax.ShapeDtypeStruct((S, out_c), out_dtype),
        in_specs=[row(C), row(LANES), row(LANES), row(LANES)], out_specs=row(out_c),
        compiler_params=_cp(("parallel",)),
    )(x, *tabs)


def _attn_tile(S):
    return min(512, S)


def _tri_schedule(n, by_key):
    pairs = ([(i, j) for j in range(n) for i in range(j, n)] if by_key
             else [(i, j) for i in range(n) for j in range(i + 1)])
    return (jnp.asarray([p[0] for p in pairs], jnp.int32), jnp.asarray([p[1] for p in pairs], jnp.int32))


def _mask_table(cid, t):
    n = cid.shape[0] // t
    blocks = cid.reshape(n, t)
    cmin_q, cmax_k = jnp.min(blocks, axis=1), jnp.max(blocks, axis=1)
    need = (cmax_k[None, :] > cmin_q[:, None]) | jnp.eye(n, dtype=bool)
    return need.astype(jnp.int32).reshape(n * n)


def _attn_mask(cidq_ref, cidk_ref, i, j, t):
    qrow = i * t + lax.broadcasted_iota(jnp.int32, (t, 1), 0)
    kcol = j * t + lax.broadcasted_iota(jnp.int32, (1, t), 1)
    qlim = (qrow // Q_BLOCK + 1) * Q_BLOCK
    return (cidk_ref[...] <= cidq_ref[...]) & (kcol < qlim)


ATTN_FWD_HEADS = 4
ATTN_BWD_HEADS = 2


def _attn_fwd_call(q, kn, v, kr, aux, heads, comm=None):
    S = q.shape[0]
    t = _attn_tile(S)
    n = S // t
    hp = ATTN_FWD_HEADS if heads % ATTN_FWD_HEADS == 0 else 1
    qi_tab, kj_tab = _tri_schedule(n, False)
    scale2 = (MLA_NOPE + MLA_ROPE) ** -0.5 * LOG2E

    def body(qi_ref, kj_ref, need_ref, q_ref, kn_ref, v_ref, kr_ref, cidq_ref, cidk_ref, o_ref, lse_ref,
             m_ref, l_ref, acc_ref):
        st = pl.program_id(1)
        i, j = qi_ref[st], kj_ref[st]

        @pl.when(j == 0)
        def _():
            m_ref[...] = jnp.full(m_ref.shape, NEG_INF, F32)
            l_ref[...] = jnp.zeros(l_ref.shape, F32)
            acc_ref[...] = jnp.zeros(acc_ref.shape, F32)

        def update(masked):
            mask = _attn_mask(cidq_ref, cidk_ref, i, j, t) if masked else None
            for hh in range(hp):
                lanes = slice(hh * LANES, (hh + 1) * LANES)
                k = jnp.concatenate([kn_ref[:, lanes], kr_ref[...]], axis=1)
                qh = q_ref[:, hh * MLA_QK_PAD:(hh + 1) * MLA_QK_PAD]
                s = lax.dot_general(qh, k, (((1,), (1,)), ((), ())), preferred_element_type=F32) * scale2
                if masked:
                    s = jnp.where(mask, s, NEG_INF)
                m_prev = m_ref[:, lanes]
                m_new = jnp.maximum(m_prev, jnp.max(s, axis=1, keepdims=True))
                alpha = jnp.exp2(m_prev - m_new)
                p = jnp.exp2(s - m_new[:, :1])
                l_ref[:, lanes] = alpha * l_ref[:, lanes] + jnp.sum(p, axis=1, keepdims=True)
                acc_ref[:, lanes] = alpha * acc_ref[:, lanes] + jnp.dot(p.astype(BF16), v_ref[:, lanes],
                                                                        preferred_element_type=F32)
                m_ref[:, lanes] = m_new

        need = need_ref[i * n + j]

        @pl.when(need != 0)
        def _():
            update(True)

        @pl.when(need == 0)
        def _():
            update(False)

        @pl.when(j == i)
        def _():
            o_ref[...] = (acc_ref[...] / l_ref[...]).astype(o_ref.dtype)
            lse_ref[...] = m_ref[...] + jnp.log(l_ref[...]) * LOG2E

    qmap = lambda h, s, qi, kj, need: (qi[s], h)
    kmap = lambda h, s, qi, kj, need: (kj[s], h)
    return _pcall(
        body, "attn_fwd", (heads // hp, qi_tab.shape[0]), 3,
        in_specs=[pl.BlockSpec((t, hp * MLA_QK_PAD), qmap), pl.BlockSpec((t, hp * MLA_NOPE), kmap),
                  pl.BlockSpec((t, hp * MLA_V), kmap),
                  pl.BlockSpec((t, LANES), lambda h, s, qi, kj, need: (kj[s], 0)),
                  pl.BlockSpec((t, 1), lambda h, s, qi, kj, need: (qi[s], 0)),
                  pl.BlockSpec((1, t), lambda h, s, qi, kj, need: (0, kj[s]))],
        out_specs=[pl.BlockSpec((t, hp * MLA_V), qmap), pl.BlockSpec((t, hp * LANES), qmap)],
        out_shape=[jax.ShapeDtypeStruct((S, heads * MLA_V), BF16), jax.ShapeDtypeStruct((S, heads * LANES), F32)],
        scratch=[pltpu.VMEM((t, hp * LANES), F32), pltpu.VMEM((t, hp * LANES), F32), pltpu.VMEM((t, hp * MLA_V), F32)],
        sem=("parallel", "arbitrary"),
        args=[qi_tab, kj_tab, aux['need'], q, kn, v, kr, aux['cidq'], aux['cidk']], comm=comm)


def _attn_bwd_call(q, kn, v, kr, o, lse, do, aux, heads, comm=None):
    S = q.shape[0]
    t = _attn_tile(S)
    n = S // t
    qi_tab, kj_tab = _tri_schedule(n, True)
    scale = (MLA_NOPE + MLA_ROPE) ** -0.5
    scale2 = scale * LOG2E
    nt_dims = (((1,), (1,)), ((), ()))
    tn_dims = (((0,), (0,)), ((), ()))

    n_steps = n * (n + 1) // 2
    hp = ATTN_BWD_HEADS if heads % ATTN_BWD_HEADS == 0 else 1

    def body(qi_ref, kj_ref, need_ref, q_ref, kn_ref, v_ref, kr_ref, cidq_ref, cidk_ref, o_ref, lse_ref, do_ref,
             ta_ref, tb1_ref, tb2_ref, dq_ref, dkn_ref, dv_ref, dkr_ref, dk_acc, dv_acc, dq_acc):
        st = pl.program_id(1)
        i, j = qi_ref[st], kj_ref[st]

        @pl.when(st == 0)
        def _():
            dq_acc[...] = jnp.zeros(dq_acc.shape, F32)

        @pl.when(i == j)
        def _():
            dk_acc[...] = jnp.zeros(dk_acc.shape, F32)
            dv_acc[...] = jnp.zeros(dv_acc.shape, F32)

        rows = pl.ds(pl.multiple_of(i * t, t), t)

        def grads(masked):
            mask = _attn_mask(cidq_ref, cidk_ref, i, j, t) if masked else None
            for hh in range(hp):
                lanes = slice(hh * LANES, (hh + 1) * LANES)
                wide = slice(hh * MLA_QK_PAD, (hh + 1) * MLA_QK_PAD)
                k = jnp.concatenate([kn_ref[:, lanes], kr_ref[...]], axis=1)
                qt, do = q_ref[:, wide], do_ref[:, lanes]
                s = lax.dot_general(qt, k, nt_dims, preferred_element_type=F32) * scale2
                if masked:
                    s = jnp.where(mask, s, NEG_INF)
                dp = lax.dot_general(do, v_ref[:, lanes], nt_dims, preferred_element_type=F32)
                dsum = jnp.sum(do.astype(F32) * o_ref[:, lanes].astype(F32), axis=1, keepdims=True)
                p = jnp.exp2(s - lse_ref[:, hh * LANES:hh * LANES + 1])
                ds = (p * (dp - dsum) * scale).astype(BF16)
                dv_acc[:, lanes] += lax.dot_general(p.astype(BF16), do, tn_dims, preferred_element_type=F32)
                dk_acc[:, wide] += lax.dot_general(ds, qt, tn_dims, preferred_element_type=F32)
                dq_acc[rows, wide] += jnp.dot(ds, k, preferred_element_type=F32)

        need = need_ref[i * n + j]

        @pl.when(need != 0)
        def _():
            grads(True)

        @pl.when(need == 0)
        def _():
            grads(False)

        @pl.when(i == n - 1)
        def _():
            for hh in range(hp):
                lanes = slice(hh * LANES, (hh + 1) * LANES)
                off = hh * MLA_QK_PAD
                dkn_ref[:, lanes] = dk_acc[:, off:off + MLA_NOPE].astype(dkn_ref.dtype)
                dkr_ref[:, lanes] = dk_acc[:, off + MLA_NOPE:off + MLA_QK_PAD]
            dv_ref[...] = dv_acc[...].astype(dv_ref.dtype)

        @pl.when(st == n_steps - 1)
        def _():
            for r in range(n):
                rs = slice(r * t, (r + 1) * t)
                for hh in range(hp):
                    off = hh * MLA_QK_PAD
                    dq_ref[rs, off:off + MLA_NOPE] = dq_acc[rs, off:off + MLA_NOPE].astype(dq_ref.dtype)
                    g = dq_acc[rs, off + MLA_NOPE:off + MLA_QK_PAD]
                    g = g * ta_ref[rs, :] + pltpu.roll(g, 96, 1) * tb1_ref[rs, :] + pltpu.roll(g, 32, 1) * tb2_ref[rs, :]
                    dq_ref[rs, off + MLA_NOPE:off + MLA_QK_PAD] = g.astype(dq_ref.dtype)

    qmap = lambda h, s, qi, kj, need: (qi[s], h)
    kmap = lambda h, s, qi, kj, need: (kj[s], h)
    whole = pl.BlockSpec((S, LANES), lambda h, s, qi, kj, need: (0, 0))
    return _pcall(
        body, "attn_bwd", (heads // hp, qi_tab.shape[0]), 3,
        in_specs=[pl.BlockSpec((t, hp * MLA_QK_PAD), qmap), pl.BlockSpec((t, hp * MLA_NOPE), kmap),
                  pl.BlockSpec((t, hp * MLA_V), kmap),
                  pl.BlockSpec((t, LANES), lambda h, s, qi, kj, need: (kj[s], 0)),
                  pl.BlockSpec((t, 1), lambda h, s, qi, kj, need: (qi[s], 0)),
                  pl.BlockSpec((1, t), lambda h, s, qi, kj, need: (0, kj[s])),
                  pl.BlockSpec((t, hp * MLA_V), qmap), pl.BlockSpec((t, hp * LANES), qmap),
                  pl.BlockSpec((t, hp * MLA_V), qmap), whole, whole, whole],
        out_specs=[pl.BlockSpec((S, hp * MLA_QK_PAD), lambda h, s, qi, kj, need: (0, h)),
                   pl.BlockSpec((t, hp * MLA_NOPE), kmap), pl.BlockSpec((t, hp * MLA_V), kmap),
                   pl.BlockSpec((t, hp * LANES), kmap)],
        out_shape=[jax.ShapeDtypeStruct((S, heads * MLA_QK_PAD), BF16), jax.ShapeDtypeStruct((S, heads * MLA_NOPE), BF16),
                   jax.ShapeDtypeStruct((S, heads * MLA_V), BF16), jax.ShapeDtypeStruct((S, heads * LANES), F32)],
        scratch=[pltpu.VMEM((t, hp * MLA_QK_PAD), F32), pltpu.VMEM((t, hp * MLA_V), F32),
                 pltpu.VMEM((S, hp * MLA_QK_PAD), F32)],
        sem=("parallel", "arbitrary"),
        args=[qi_tab, kj_tab, aux['need'], q, kn, v, kr, aux['cidq'], aux['cidk'], o, lse, do, *aux['rope'][1]],
        comm=comm)


def make_attention(aux, heads):
    tabs_f, tabs_b = aux['rope']
    odd, every = (lambda blk: blk % 2 == 1), (lambda blk: True)

    def run_host(q_raw, kn, v, kr_raw, comm=None):
        q = _rope_call(q_raw, tabs_f, odd, BF16, "rope_q")
        kr = _rope_call(kr_raw, tabs_f, every, BF16, "rope_k")
        o, lse, *carried = _attn_fwd_call(q, kn, v, kr, aux, heads, comm)
        return o, (q, kn, v, kr, o, lse), carried

    def bwd_host(res, do, comm=None):
        q, kn, v, kr, o, lse = res
        dq_raw, dkn, dv, dkr, *carried = _attn_bwd_call(q, kn, v, kr, o, lse, do, aux, heads, comm)
        return (dq_raw, dkn, dv, _rope_call(dkr, tabs_b, every, F32, "rope_dk", fold=True)), carried

    @jax.custom_vjp
    def attn(q_raw, kn, v, kr_raw):
        return run_host(q_raw, kn, v, kr_raw)[0]

    attn.defvjp(lambda *a: run_host(*a)[:2], lambda res, do: bwd_host(res, do)[0])
    return attn, run_host, bwd_host


GLA_ROWS = 256
GLA_HEADS_PER_STEP = 2


def _tri(lower):
    r = lax.broadcasted_iota(jnp.int32, (CHUNK, CHUNK), 0)
    c = lax.broadcasted_iota(jnp.int32, (CHUNK, CHUNK), 1)
    return jnp.where((c <= r) if lower else (c >= r), 1.0, 0.0).astype(F32)


def _gla_chunk(q_ref, k_ref, v_ref, la_ref, sl, hh):
    lk, lv = slice(hh * GLA_DK, (hh + 1) * GLA_DK), slice(hh * GLA_DV, (hh + 1) * GLA_DV)
    la = la_ref[sl, lk]
    cum = jnp.dot(_tri(True), la, preferred_element_type=F32, precision=lax.Precision.HIGHEST)
    tot = cum[CHUNK - 1:CHUNK, :]
    e = jnp.exp(tot - cum)
    k = k_ref[sl, lk].astype(F32)
    kdec = k * e
    v = v_ref[sl, lv]
    upd_t = lax.dot_general(v.astype(BF16), kdec.astype(BF16), (((0,), (0,)), ((), ())), preferred_element_type=F32)
    qs = (q_ref[sl, lk].astype(F32) * (GLA_DK ** -0.5)).astype(BF16)
    return e, k, kdec, v, upd_t, jnp.exp(tot), qs


def _gla_group(heads):
    return GLA_HEADS_PER_STEP if heads % GLA_HEADS_PER_STEP == 0 else 1


def _gla_specs(heads, hp, rows_map):
    groups = heads // hp
    return [pl.BlockSpec((GLA_ROWS, hp * GLA_DK), lambda h, b: (rows_map(b), h)),
            pl.BlockSpec((GLA_ROWS, hp * GLA_DK), lambda h, b: (rows_map(b), groups + h)),
            pl.BlockSpec((GLA_ROWS, hp * GLA_DV), lambda h, b: (rows_map(b), groups + h)),
            pl.BlockSpec((GLA_ROWS, hp * GLA_DK), lambda h, b: (rows_map(b), h))]


def _gla_fwd_call(hm, la, heads):
    S = hm.shape[0]
    assert S % GLA_ROWS == 0
    nb, cpb, hp = S // GLA_ROWS, GLA_ROWS // CHUNK, _gla_group(heads)

    def body(q_ref, k_ref, v_ref, la_ref, o_ref, sp_ref, st_ref):
        @pl.when(pl.program_id(1) == 0)
        def _():
            st_ref[...] = jnp.zeros(st_ref.shape, F32)

        for c in range(cpb):
            sl = slice(c * CHUNK, (c + 1) * CHUNK)
            for hh in range(hp):
                _, _, _, _, upd_t, decay, qs = _gla_chunk(q_ref, k_ref, v_ref, la_ref, sl, hh)
                state = st_ref[hh]
                sp_ref[hh, c] = state
                state = state * decay + upd_t
                st_ref[hh] = state
                o_ref[sl, hh * GLA_DV:(hh + 1) * GLA_DV] = lax.dot_general(
                    qs, state.astype(BF16), (((1,), (1,)), ((), ())), preferred_element_type=F32)

    return pl.pallas_call(
        body, name=_nm("gla_fwd"), grid=(heads // hp, nb),
        out_shape=(jax.ShapeDtypeStruct((S, heads * GLA_DV), F32),
                   jax.ShapeDtypeStruct((heads, S // CHUNK, GLA_DV, GLA_DK), F32)),
        in_specs=_gla_specs(heads, hp, lambda b: b),
        out_specs=(pl.BlockSpec((GLA_ROWS, hp * GLA_DV), lambda h, b: (b, h)),
                   pl.BlockSpec((hp, cpb, GLA_DV, GLA_DK), lambda h, b: (h, b, 0, 0))),
        scratch_shapes=[pltpu.VMEM((hp, GLA_DV, GLA_DK), F32)],
        compiler_params=_cp(("parallel", "arbitrary")),
    )(hm, hm, hm, la)


def _gla_bwd_call(hm, la, sprev, do, heads):
    S = hm.shape[0]
    nb, cpb, hp = S // GLA_ROWS, GLA_ROWS // CHUNK, _gla_group(heads)
    scale = GLA_DK ** -0.5

    def body(q_ref, k_ref, v_ref, la_ref, sp_ref, do_ref, dq_ref, dk_ref, dv_ref, dla_ref, carry_ref):
        @pl.when(pl.program_id(1) == 0)
        def _():
            carry_ref[...] = jnp.zeros(carry_ref.shape, F32)

        for c in reversed(range(cpb)):
            sl = slice(c * CHUNK, (c + 1) * CHUNK)
            for hh in range(hp):
                lk, lv = slice(hh * GLA_DK, (hh + 1) * GLA_DK), slice(hh * GLA_DV, (hh + 1) * GLA_DV)
                e, k, kdec, v, upd_t, decay, qs = _gla_chunk(q_ref, k_ref, v_ref, la_ref, sl, hh)
                sp = sp_ref[hh, c]
                s_n = sp * decay + upd_t
                dob = do_ref[sl, lv].astype(BF16)
                g = carry_ref[hh] + lax.dot_general(dob, qs, (((0,), (0,)), ((), ())), preferred_element_type=F32)
                gb = g.astype(BF16)
                dq_ref[sl, lk] = (jnp.dot(dob, s_n.astype(BF16), preferred_element_type=F32) * scale).astype(dq_ref.dtype)
                ddecay = jnp.sum(g * sp, axis=0, keepdims=True)
                dkdec = jnp.dot(v.astype(BF16), gb, preferred_element_type=F32)
                dv_ref[sl, lv] = lax.dot_general(kdec.astype(BF16), gb, (((1,), (1,)), ((), ())),
                                                 preferred_element_type=F32).astype(dv_ref.dtype)
                dk_ref[sl, lk] = (dkdec * e).astype(dk_ref.dtype)
                w = dkdec * k * e
                dtot = jnp.sum(w, axis=0, keepdims=True) + ddecay * decay
                last = lax.broadcasted_iota(jnp.int32, (CHUNK, 1), 0) == CHUNK - 1
                dcum = jnp.where(last, dtot - w, -w)
                dla_ref[sl, lk] = jnp.dot(_tri(False), dcum, preferred_element_type=F32, precision=lax.Precision.HIGHEST)
                carry_ref[hh] = g * decay

    rev = lambda b: nb - 1 - b
    narrow = pl.BlockSpec((GLA_ROWS, hp * GLA_DK), lambda h, b: (rev(b), h))
    wide = pl.BlockSpec((GLA_ROWS, hp * GLA_DV), lambda h, b: (rev(b), h))
    return pl.pallas_call(
        body, name=_nm("gla_bwd"), grid=(heads // hp, nb),
        out_shape=(jax.ShapeDtypeStruct((S, heads * GLA_DK), hm.dtype), jax.ShapeDtypeStruct((S, heads * GLA_DK), hm.dtype),
                   jax.ShapeDtypeStruct((S, heads * GLA_DV), hm.dtype), jax.ShapeDtypeStruct((S, heads * GLA_DK), F32)),
        in_specs=_gla_specs(heads, hp, rev) + [
            pl.BlockSpec((hp, cpb, GLA_DV, GLA_DK), lambda h, b: (h, rev(b), 0, 0)), wide],
        out_specs=(narrow, narrow, wide, narrow),
        scratch_shapes=[pltpu.VMEM((hp, GLA_DV, GLA_DK), F32)],
        compiler_params=_cp(("parallel", "arbitrary")),
    )(hm, hm, hm, la, sprev, do)


@functools.partial(jax.custom_vjp, nondiff_argnums=(3,))
def gla_core(hm, la, o_norm, heads):
    return _gla_core_fwd(hm, la, o_norm, heads)[0]


def _gla_core_fwd(hm, la, o_norm, heads):
    o, sprev = _gla_fwd_call(hm, la, heads)
    vd = heads * GLA_DV
    rows = [(o, vd, 0), (hm, vd, 2 * heads * GLA_DK // vd + 1)]
    (y,) = _rw_fwd(_gla_out_fn(heads), rows, [o_norm], [BF16], "gla_out_fwd")
    return y, (hm, la, o_norm, o, sprev)


def _gla_core_bwd(heads, res, dy):
    hm, la, o_norm, o, sprev = res
    vd = heads * GLA_DV
    rows = [(o, vd, 0), (hm, vd, 2 * heads * GLA_DK // vd + 1)]
    (do, dr), (dg,) = _rw_bwd(_gla_out_fn(heads), rows, [o_norm], [dy], [F32, hm.dtype], "gla_out_bwd")
    dq, dk, dv, dla = _gla_bwd_call(hm, la, sprev, do, heads)
    return jnp.concatenate([dq, dk, dv, dr], axis=1), dla, dg


gla_core.defvjp(_gla_core_fwd, _gla_core_bwd)


CONV_COLS = 256
HALO = 16


def _conv_rows(S):
    return min(512, S)


def _conv_taps(main_ref, halo_ref, i):
    prev = jnp.where(i > 0, halo_ref[...].astype(F32), 0.0)
    full = jnp.concatenate([prev, main_ref[...].astype(F32)], axis=0)
    return full[HALO:], pltpu.roll(full, 1, 0)[HALO:], pltpu.roll(full, 2, 0)[HALO:]


def _conv_apply(taps, w_ref, b_ref):
    x0, x1, x2 = taps
    return x2 * w_ref[0:1, :] + x1 * w_ref[1:2, :] + x0 * w_ref[2:3, :] + b_ref[...]


def _gelu_gate(uc, gc):
    return uc * jax.nn.gelu(gc)


def _conv_cols(dff, pref):
    return max(c for c in range(LANES, pref + 1, LANES) if dff % c == 0)


def _conv_in_specs(R, C, nj):
    hpr = R // HALO
    main = lambda off: pl.BlockSpec((R, C), lambda j, i: (i, j + off))
    halo = lambda off: pl.BlockSpec((HALO, C), lambda j, i: (jnp.maximum(i * hpr - 1, 0), j + off))
    par = lambda rows, off: pl.BlockSpec((rows, C), lambda j, i: (0, j + off))
    return [main(0), halo(0), main(nj), halo(nj), par(CONV_W, 0), par(CONV_W, nj), par(1, 0), par(1, nj)]


def _conv_fwd_call(h, cw, cb, comm=None):
    S, dff = h.shape[0], h.shape[1] // 2
    R, C = _conv_rows(S), _conv_cols(dff, 768)
    nj = dff // C

    def body(u_ref, uh_ref, g_ref, gh_ref, wu_ref, wg_ref, bu_ref, bg_ref, a_ref):
        i = pl.program_id(1)
        uc = _conv_apply(_conv_taps(u_ref, uh_ref, i), wu_ref, bu_ref)
        gc = _conv_apply(_conv_taps(g_ref, gh_ref, i), wg_ref, bg_ref)
        a_ref[...] = _gelu_gate(uc, gc).astype(a_ref.dtype)

    return _pcall(
        body, "conv_fwd", (nj, S // R), 0, in_specs=_conv_in_specs(R, C, nj),
        out_specs=[pl.BlockSpec((R, C), lambda j, i: (i, j))], out_shape=[jax.ShapeDtypeStruct((S, dff), BF16)],
        scratch=[], sem=("parallel", "parallel"), args=[h, h, h, h, cw, cw, cb, cb], comm=comm)


def _conv_bwd_gate_call(h, cw, cb, da, comm=None):
    S, dff = h.shape[0], h.shape[1] // 2
    R, C = _conv_rows(S), _conv_cols(dff, 512)
    nj = dff // C

    def body(u_ref, uh_ref, g_ref, gh_ref, wu_ref, wg_ref, bu_ref, bg_ref, da_ref,
             du_ref, dg_ref, dwu_ref, dwg_ref, dbu_ref, dbg_ref):
        i = pl.program_id(1)
        ut, gt = _conv_taps(u_ref, uh_ref, i), _conv_taps(g_ref, gh_ref, i)
        uc, gc = _conv_apply(ut, wu_ref, bu_ref), _conv_apply(gt, wg_ref, bg_ref)
        _, vjp_fn = jax.vjp(_gelu_gate, uc, gc)
        du, dg = vjp_fn(da_ref[...].astype(F32))
        du_ref[...] = du.astype(du_ref.dtype)
        dg_ref[...] = dg.astype(dg_ref.dtype)

        @pl.when(i == 0)
        def _():
            for r in (dwu_ref, dwg_ref, dbu_ref, dbg_ref):
                r[...] = jnp.zeros(r.shape, F32)

        for d, taps, dw_ref, db_ref in ((du, ut, dwu_ref, dbu_ref), (dg, gt, dwg_ref, dbg_ref)):
            x0, x1, x2 = taps
            dw_ref[0:1, :] += jnp.sum(d * x2, axis=0, keepdims=True)
            dw_ref[1:2, :] += jnp.sum(d * x1, axis=0, keepdims=True)
            dw_ref[2:3, :] += jnp.sum(d * x0, axis=0, keepdims=True)
            db_ref[...] += jnp.sum(d, axis=0, keepdims=True)

    tile = pl.BlockSpec((R, C), lambda j, i: (i, j))
    par = lambda rows: pl.BlockSpec((rows, C), lambda j, i: (0, j))
    return _pcall(
        body, "conv_bwd_gate", (nj, S // R), 0, in_specs=_conv_in_specs(R, C, nj) + [tile],
        out_specs=[tile, tile, par(CONV_W), par(CONV_W), par(1), par(1)],
        out_shape=[jax.ShapeDtypeStruct((S, dff), BF16), jax.ShapeDtypeStruct((S, dff), BF16),
                   jax.ShapeDtypeStruct((CONV_W, dff), F32), jax.ShapeDtypeStruct((CONV_W, dff), F32),
                   jax.ShapeDtypeStruct((1, dff), F32), jax.ShapeDtypeStruct((1, dff), F32)],
        scratch=[], sem=("parallel", "arbitrary"), args=[h, h, h, h, cw, cw, cb, cb, da], comm=comm)


def _conv_bwd_shift_call(dc, cw, into=None):
    S, dff = dc.shape
    R, C = min(1024, S), _conv_cols(dff, 1024)
    nj = dff // C
    col_off = 0 if into is None else nj
    hpr, last = R // HALO, S // HALO - 1
    ni = S // R

    def body(d_ref, nx_ref, w_ref, *rest):
        o_ref = rest[-1]
        i = pl.program_id(1)
        nxt = jnp.where(i < ni - 1, nx_ref[...].astype(F32), 0.0)
        full = jnp.concatenate([d_ref[...].astype(F32), nxt], axis=0)
        n = R + HALO
        y1, y2 = pltpu.roll(full, n - 1, 0)[:R], pltpu.roll(full, n - 2, 0)[:R]
        o_ref[...] = (full[:R] * w_ref[2:3, :] + y1 * w_ref[1:2, :] + y2 * w_ref[0:1, :]).astype(o_ref.dtype)

    in_specs = [pl.BlockSpec((R, C), lambda j, i: (i, j)),
                pl.BlockSpec((HALO, C), lambda j, i: (jnp.minimum((i + 1) * hpr, last), j)),
                pl.BlockSpec((CONV_W, C), lambda j, i: (0, j + col_off))]
    args = [dc, dc, cw]
    if into is not None:
        in_specs.append(pl.BlockSpec(memory_space=pl.ANY))
        args.append(into)
    return pl.pallas_call(
        body, name=_nm("conv_bwd_shift"), grid=(nj, ni), out_shape=jax.ShapeDtypeStruct((S, 2 * dff), BF16),
        in_specs=in_specs, out_specs=pl.BlockSpec((R, C), lambda j, i: (i, j + col_off)),
        input_output_aliases={} if into is None else {3: 0},
        compiler_params=_cp(("parallel", "parallel")),
    )(*args)


@jax.custom_vjp
def ffn_hidden(x1, x1_bf16, w3, cw, cb):
    return _ffn_hidden_fwd(x1, x1_bf16, w3, cw, cb)[0]


def _ffn_hidden_fwd(x1, x1_bf16, w3, cw, cb, comm=None):
    h = _mm(x1_bf16, w3, "nn", BF16, "up", _all_blocks(w3))
    a, *carried = _conv_fwd_call(h, cw, cb, comm)
    return (a, (x1_bf16, w3, cw, cb, h)) + ((carried,) if comm is not None else ())


def _ffn_hidden_bwd(res, da, comm=None):
    x1, w3, cw, cb, h = res
    du, dg, dwu, dwg, dbu, dbg, *carried = _conv_bwd_gate_call(h, cw, cb, da, comm)
    dh = _conv_bwd_shift_call(dg, cw, into=_conv_bwd_shift_call(du, cw))
    dx = _mm(dh, w3, "nt", F32, "up_dx", _all_blocks(w3))
    dw3 = _mm(x1, dh, "tn", w3.dtype, "up_dw", _all_blocks(w3))
    grads = (dx, jnp.zeros_like(x1), dw3, jnp.concatenate([dwu, dwg], axis=1), jnp.concatenate([dbu, dbg], axis=1))
    return (grads, carried) if comm is not None else grads


ffn_hidden.defvjp(_ffn_hidden_fwd, _ffn_hidden_bwd)


def _loss_call(y, target):
    S, D = y.shape
    tr = _row_tile(S, D)

    def body(y_ref, t_ref, sq_ref, dy_ref):
        diff = y_ref[...] - t_ref[...]
        dy_ref[...] = diff * (1.0 / D)
        part = jnp.sum(diff * diff, axis=0, keepdims=True)
        i = pl.program_id(0)

        @pl.when(i == 0)
        def _():
            sq_ref[...] = part

        @pl.when(i > 0)
        def _():
            sq_ref[...] += part

    row = pl.BlockSpec((tr, D), lambda i: (i, 0))
    return pl.pallas_call(
        body, name=_nm("loss"), grid=(S // tr,),
        out_shape=(jax.ShapeDtypeStruct((1, D), F32), jax.ShapeDtypeStruct((S, D), F32)),
        in_specs=[row, row], out_specs=(pl.BlockSpec((1, D), lambda i: (0, 0)), row),
        compiler_params=_cp(("arbitrary",)),
    )(y, target)


def _row_divisor(rows):
    for cand in range(min(rows, 512), 15, -1):
        if rows % cand == 0 and cand % 16 == 0:
            return cand
    return rows


def _adamw_call(w, g, m, v):
    shape = w.shape
    w2, g2, m2, v2 = (a.reshape(-1, shape[-1]) for a in (w, g, m, v))
    rows, cols = w2.shape
    tr = _row_divisor(rows)

    def body(w_ref, g_ref, m_ref, v_ref, d_ref, nm_ref, nv_ref):
        g_ = g_ref[...]
        m_ = ADAM_B1 * m_ref[...] + (1.0 - ADAM_B1) * g_
        v_ = ADAM_B2 * v_ref[...] + (1.0 - ADAM_B2) * (g_ * g_)
        m_hat = m_ / (1.0 - ADAM_B1 ** ADAM_STEP)
        v_hat = v_ / (1.0 - ADAM_B2 ** ADAM_STEP)
        d_ref[...] = -ADAM_LR * (m_hat / (jnp.sqrt(v_hat) + ADAM_EPS) + ADAM_WD * w_ref[...])
        nm_ref[...] = m_
        nv_ref[...] = v_

    blk = pl.BlockSpec((tr, cols), lambda i: (i, 0))
    outs = pl.pallas_call(
        body, name=_nm("adamw"), grid=(rows // tr,),
        out_shape=tuple(jax.ShapeDtypeStruct((rows, cols), F32) for _ in range(3)),
        in_specs=[blk] * 4, out_specs=(blk,) * 3, compiler_params=_cp(("parallel",)),
    )(w2, g2, m2, v2)
    return tuple(o.reshape(shape) for o in outs)


ANY = pl.BlockSpec(memory_space=pl.ANY)


def _place():
    return lax.axis_index("x"), lax.axis_index("y"), lax.axis_index("c")


def all_gather(shards):
    n = len(shards)

    def body(*refs):
        x_refs, out_refs = refs[:n], refs[n:2 * n]
        send_sems, recv_sems, local_sems = refs[2 * n:]
        x, y, c = _place()
        me, sibling = (x, y, c), (x, y, 1 - c)
        chips = [(1 - x, y), (x, 1 - y), (1 - x, 1 - y)]

        def slot(a, px, py, pc):
            return out_refs[a].at[:, 4 * px + 2 * py + pc]

        def copy(a, k, block, to, own=False):
            return pltpu.make_async_remote_copy(
                src_ref=x_refs[a] if own else slot(a, *block), dst_ref=slot(a, *block),
                send_sem=send_sems.at[7 * a + k], recv_sem=recv_sems.at[7 * a + k], device_id=to, device_id_type=MESH)

        mine = [pltpu.make_async_copy(x_refs[a], slot(a, *me), local_sems.at[a]) for a in range(n)]
        first = []
        for a in range(n):
            mine[a].start()
            first.append(copy(a, 0, me, sibling, own=True))
            first += [copy(a, 1 + j, me, (*chip, c), own=True) for j, chip in enumerate(chips)]
        for cp in first:
            cp.start()
        passed = []
        for j, chip in enumerate(chips):
            for a in range(n):
                copy(a, 1 + j, (*chip, c), me).wait_recv()
                passed.append(copy(a, 4 + j, (*chip, c), sibling))
                passed[-1].start()
        for a in range(n):
            copy(a, 0, sibling, me).wait_recv()
        for j, chip in enumerate(chips):
            for a in range(n):
                copy(a, 4 + j, (*chip, 1 - c), me).wait_recv()
        for cp in first + passed:
            cp.wait_send()
        for cp in mine:
            cp.wait()

    return pl.pallas_call(
        body, name=_nm("all_gather"),
        out_shape=tuple(jax.ShapeDtypeStruct((s.shape[0], N_DEV) + s.shape[1:], s.dtype) for s in shards),
        in_specs=[ANY] * n, out_specs=(ANY,) * n,
        scratch_shapes=[pltpu.SemaphoreType.DMA((7 * n,)), pltpu.SemaphoreType.DMA((7 * n,)), pltpu.SemaphoreType.DMA((n,))],
    )(*shards)


def _rs_pair_exchange(gs):
    n = len(gs)

    def body(*refs):
        g_refs, recv_refs = refs[:n], refs[n:2 * n]
        send_sems, recv_sems = refs[2 * n:]
        x, y, c = _place()
        copies = [pltpu.make_async_remote_copy(
            src_ref=g_refs[a].at[:, 2 * j + (1 - c)], dst_ref=recv_refs[a].at[j], send_sem=send_sems.at[4 * a + j],
            recv_sem=recv_sems.at[4 * a + j], device_id=(x, y, 1 - c), device_id_type=MESH)
            for a in range(n) for j in range(4)]
        for cp in copies:
            cp.start()
        for cp in copies:
            cp.wait_recv()
        for cp in copies:
            cp.wait_send()

    return pl.pallas_call(
        body, name=_nm("rs_pair"),
        out_shape=tuple(jax.ShapeDtypeStruct((4, g.shape[0]) + g.shape[2:], g.dtype) for g in gs),
        in_specs=[ANY] * n, out_specs=(ANY,) * n,
        scratch_shapes=[pltpu.SemaphoreType.DMA((4 * n,)), pltpu.SemaphoreType.DMA((4 * n,))],
    )(*gs)


def _rs_chip_exchange(ps):
    n = len(ps)

    def body(*refs):
        p_refs, recv_refs = refs[:n], refs[n:2 * n]
        send_sems, recv_sems = refs[2 * n:]
        x, y, c = _place()
        chips = [(1 - x, y), (x, 1 - y), (1 - x, 1 - y)]
        copies = [pltpu.make_async_remote_copy(
            src_ref=p_refs[a].at[2 * cx + cy], dst_ref=recv_refs[a].at[k], send_sem=send_sems.at[3 * a + k],
            recv_sem=recv_sems.at[3 * a + k], device_id=(cx, cy, c), device_id_type=MESH)
            for a in range(n) for k, (cx, cy) in enumerate(chips)]
        for cp in copies:
            cp.start()
        for cp in copies:
            cp.wait_recv()
        for cp in copies:
            cp.wait_send()

    return pl.pallas_call(
        body, name=_nm("rs_chip"),
        out_shape=tuple(jax.ShapeDtypeStruct((3,) + p.shape[1:], p.dtype) for p in ps),
        in_specs=[ANY] * n, out_specs=(ANY,) * n,
        scratch_shapes=[pltpu.SemaphoreType.DMA((3 * n,)), pltpu.SemaphoreType.DMA((3 * n,))],
    )(*ps)


def _rs_pair_add(g, recv, c_idx):
    L, _, a, b = g.shape
    ta = _row_divisor(a)

    def body(c_ref, g_ref, r_ref, o_ref):
        o_ref[...] = (g_ref[...].astype(F32) + r_ref[...].astype(F32)).astype(o_ref.dtype)

    grid_spec = pltpu.PrefetchScalarGridSpec(
        num_scalar_prefetch=1, grid=(4, L, a // ta),
        in_specs=[pl.BlockSpec((None, None, ta, b), lambda j, l, i, c_ref: (l, 2 * j + c_ref[0], i, 0)),
                  pl.BlockSpec((None, None, ta, b), lambda j, l, i, c_ref: (j, l, i, 0))],
        out_specs=pl.BlockSpec((None, None, ta, b), lambda j, l, i, c_ref: (j, l, i, 0)))
    return pl.pallas_call(
        body, name=_nm("rs_pair_add"), grid_spec=grid_spec, out_shape=jax.ShapeDtypeStruct((4, L, a, b), g.dtype),
        compiler_params=_cp(("parallel", "parallel", "parallel")),
    )(c_idx, g, recv)


def _rs_final_add(p1, recv, chip_idx):
    _, L, a, b = p1.shape
    ta = _row_divisor(a)

    def body(chip_ref, p_ref, r_ref, o_ref):
        acc = p_ref[...].astype(F32)
        for k in range(3):
            acc = acc + r_ref[k].astype(F32)
        o_ref[...] = acc

    grid_spec = pltpu.PrefetchScalarGridSpec(
        num_scalar_prefetch=1, grid=(L, a // ta),
        in_specs=[pl.BlockSpec((None, None, ta, b), lambda l, i, chip_ref: (chip_ref[0], l, i, 0)),
                  pl.BlockSpec((3, None, ta, b), lambda l, i, chip_ref: (0, l, i, 0))],
        out_specs=pl.BlockSpec((None, ta, b), lambda l, i, chip_ref: (l, i, 0)))
    return pl.pallas_call(
        body, name=_nm("rs_final_add"), grid_spec=grid_spec, out_shape=jax.ShapeDtypeStruct((L, a, b), F32),
        compiler_params=_cp(("parallel", "parallel")),
    )(chip_idx, p1, recv)


def reduce_scatter(gs):
    x, y, c = _place()
    c_idx = jnp.reshape(c, (1,)).astype(jnp.int32)
    chip_idx = jnp.reshape(2 * x + y, (1,)).astype(jnp.int32)
    recv1 = _rs_pair_exchange(gs)
    p1 = [_rs_pair_add(g, r, c_idx) for g, r in zip(gs, recv1)]
    recv2 = _rs_chip_exchange(p1)
    return [_rs_final_add(p, r, chip_idx) for p, r in zip(p1, recv2)]


def all_reduce_small(v):
    r, C = v.shape

    def body(v_ref, out_ref, buf_ref, send_sems, recv_sems):
        x, y, c = _place()
        my_id = 4 * x + 2 * y + c
        buf_ref[my_id] = v_ref[...]
        copies = []
        for k in range(1, N_DEV):
            fx, fy, fc = (k >> 2) & 1, (k >> 1) & 1, k & 1
            peer = (x ^ fx, y ^ fy, c ^ fc)
            copies.append(pltpu.make_async_remote_copy(
                src_ref=v_ref, dst_ref=buf_ref.at[my_id], send_sem=send_sems.at[k - 1], recv_sem=recv_sems.at[k - 1],
                device_id=peer, device_id_type=MESH))
        for cp in copies:
            cp.start()
        for cp in copies:
            cp.wait_recv()
        for cp in copies:
            cp.wait_send()
        acc = buf_ref[0]
        for d in range(1, N_DEV):
            acc = acc + buf_ref[d]
        out_ref[...] = acc

    vm = pl.BlockSpec(memory_space=pltpu.VMEM)
    return pl.pallas_call(
        body, name=_nm("all_reduce_small"), out_shape=jax.ShapeDtypeStruct((r, C), F32),
        in_specs=[vm], out_specs=vm,
        scratch_shapes=[pltpu.VMEM((N_DEV, r, C), F32), pltpu.SemaphoreType.DMA((7,)), pltpu.SemaphoreType.DMA((7,))],
    )(v)


def _slot(ref, px, py, pc):
    return ref.at[:, 4 * px + 2 * py + pc]


def comm_gather_own(shards):
    n = len(shards)

    def build(cin, cout, send_sems, recv_sems, local_sems):
        x, y, c = _place()
        me = (x, y, c)
        direct = [(x, y, 1 - c), (1 - x, y, c), (x, 1 - y, c)]
        src_nb, dst_nb, diag = (x ^ (1 - c), y ^ c, c), (x ^ c, y ^ (1 - c), c), (1 - x, 1 - y, c)
        starts, mid_waits, mid_starts, waits = [], [], [], []
        for a in range(n):
            local = pltpu.make_async_copy(cin[a], _slot(cout[a], *me), local_sems.at[a])
            starts.append(local)
            waits.append(local.wait)
            for k, peer in enumerate(direct):
                send = pltpu.make_async_remote_copy(
                    src_ref=cin[a], dst_ref=_slot(cout[a], *me), send_sem=send_sems.at[4 * a + k],
                    recv_sem=recv_sems.at[4 * a + k], device_id=peer, device_id_type=MESH)
                arrive = pltpu.make_async_remote_copy(
                    src_ref=cin[a], dst_ref=_slot(cout[a], *peer), send_sem=send_sems.at[4 * a + k],
                    recv_sem=recv_sems.at[4 * a + k], device_id=peer, device_id_type=MESH)
                starts.append(send)
                (waits if k == 0 else mid_waits).append(arrive.wait_recv)
                waits.append(send.wait_send)
            onward = pltpu.make_async_remote_copy(
                src_ref=_slot(cout[a], *src_nb), dst_ref=_slot(cout[a], *src_nb), send_sem=send_sems.at[4 * a + 3],
                recv_sem=recv_sems.at[4 * a + 3], device_id=dst_nb, device_id_type=MESH)
            arrive = pltpu.make_async_remote_copy(
                src_ref=_slot(cout[a], *src_nb), dst_ref=_slot(cout[a], *diag), send_sem=send_sems.at[4 * a + 3],
                recv_sem=recv_sems.at[4 * a + 3], device_id=dst_nb, device_id_type=MESH)
            mid_starts.append(onward)
            waits += [arrive.wait_recv, onward.wait_send]
        return starts, mid_waits, mid_starts, waits

    out_shapes = [jax.ShapeDtypeStruct((s.shape[0], N_DEV) + s.shape[1:], s.dtype) for s in shards]
    return Comm(shards, out_shapes, {}, 4 * n, 4 * n, n, build)


def comm_gather_pass(partial):
    n = len(partial)

    def build(cin, cout, send_sems, recv_sems, local_sems):
        x, y, c = _place()
        chips = [(1 - x, y), (x, 1 - y), (1 - x, 1 - y)]
        starts, waits = [], []
        for a in range(n):
            for j, chip in enumerate(chips):
                send = pltpu.make_async_remote_copy(
                    src_ref=_slot(cout[a], *chip, c), dst_ref=_slot(cout[a], *chip, c), send_sem=send_sems.at[3 * a + j],
                    recv_sem=recv_sems.at[3 * a + j], device_id=(x, y, 1 - c), device_id_type=MESH)
                arrive = pltpu.make_async_remote_copy(
                    src_ref=_slot(cout[a], *chip, c), dst_ref=_slot(cout[a], *chip, 1 - c),
                    send_sem=send_sems.at[3 * a + j], recv_sem=recv_sems.at[3 * a + j],
                    device_id=(x, y, 1 - c), device_id_type=MESH)
                starts.append(send)
                waits += [arrive.wait_recv, send.wait_send]
        return starts, waits

    out_shapes = [jax.ShapeDtypeStruct(p.shape, p.dtype) for p in partial]
    return Comm(partial, out_shapes, {a: a for a in range(n)}, 3 * n, 3 * n, 1, build)


def comm_rs_pair(gs):
    n = len(gs)

    def build(cin, cout, send_sems, recv_sems, local_sems):
        x, y, c = _place()
        starts, waits = [], []
        for a in range(n):
            for j in range(4):
                cp = pltpu.make_async_remote_copy(
                    src_ref=cin[a].at[:, 2 * j + (1 - c)], dst_ref=cout[a].at[j], send_sem=send_sems.at[4 * a + j],
                    recv_sem=recv_sems.at[4 * a + j], device_id=(x, y, 1 - c), device_id_type=MESH)
                starts.append(cp)
                waits += [cp.wait_recv, cp.wait_send]
        return starts, waits

    out_shapes = [jax.ShapeDtypeStruct((4, g.shape[0]) + g.shape[2:], g.dtype) for g in gs]
    return Comm(gs, out_shapes, {}, 4 * n, 4 * n, 1, build)


def comm_rs_chip(ps):
    n = len(ps)

    def build(cin, cout, send_sems, recv_sems, local_sems):
        x, y, c = _place()
        chips = [(1 - x, y), (x, 1 - y), (1 - x, 1 - y)]
        starts, waits = [], []
        for a in range(n):
            for k, (cx, cy) in enumerate(chips):
                cp = pltpu.make_async_remote_copy(
                    src_ref=cin[a].at[2 * cx + cy], dst_ref=cout[a].at[k], send_sem=send_sems.at[3 * a + k],
                    recv_sem=recv_sems.at[3 * a + k], device_id=(cx, cy, c), device_id_type=MESH)
                starts.append(cp)
                waits += [cp.wait_recv, cp.wait_send]
        return starts, waits

    out_shapes = [jax.ShapeDtypeStruct((3,) + p.shape[1:], p.dtype) for p in ps]
    return Comm(ps, out_shapes, {}, 3 * n, 3 * n, 1, build)


def _pack(arrays, dtype, row_align):
    lead = arrays[0].shape[:-1]
    quantum = row_align * PACK_COLS
    parts, sizes = [], []
    for a in arrays:
        n = a.shape[-1]
        padded = _round_up(n, quantum)
        a = a.astype(dtype)
        if padded != n:
            a = jnp.pad(a, [(0, 0)] * len(lead) + [(0, padded - n)])
        parts.append(a.reshape(*lead, padded // PACK_COLS, PACK_COLS))
        sizes.append((n, padded // PACK_COLS))
    return jnp.concatenate(parts, axis=len(lead)), sizes


def _unpack(packed, sizes):
    lead = packed.shape[:-2]
    out, row = [], 0
    for n, rows in sizes:
        part = lax.slice_in_dim(packed, row, row + rows, axis=len(lead))
        out.append(part.reshape(*lead, rows * PACK_COLS)[..., :n])
        row += rows
    return out


def _unshard(gathered, axis):
    _, L, a, b = gathered.shape
    if axis == 1:
        return [gathered[:, l].reshape(N_DEV * a, b) for l in range(L)]
    return [jnp.transpose(gathered[:, l], (1, 0, 2)).reshape(a, N_DEV * b) for l in range(L)]


def _reshard(fulls, axis):
    blocks = []
    for f in fulls:
        A, B = f.shape
        if axis == 1:
            blocks.append(f.reshape(N_DEV, A // N_DEV, B))
        else:
            blocks.append(jnp.transpose(f.reshape(A, N_DEV, B // N_DEV), (1, 0, 2)))
    return jnp.stack(blocks, axis=1)


def _as3(a):
    return a if a.ndim == 3 else a[:, None, :]


def _prep_big(name, w, dims):
    w = w.astype(BF16)
    L, a, b = w.shape
    if name == 'mla_w_in':
        return jnp.pad(w, ((0, 0), (0, 0), (0, dims['h_width'] - b)))
    if name == 'mla_w_uq':
        hd = MLA_NOPE + MLA_ROPE
        w = jnp.pad(w.reshape(L, a, b // hd, hd), ((0, 0), (0, 0), (0, 0), (0, MLA_QK_PAD - hd)))
        return w.reshape(L, a, b // hd * MLA_QK_PAD)
    if name == 'ffn_w_up':
        return jnp.pad(w, ((0, 0), (0, 0), (0, _round_up(b, CONV_COLS) - b)))
    return w


def _unprep_big(name, g, shape):
    L, a, b = shape
    if name == 'mla_w_uq':
        hd = MLA_NOPE + MLA_ROPE
        return g.reshape(L, a, b // hd, MLA_QK_PAD)[..., :hd].reshape(L, a, b)
    return g[:, :, :b]


def _rope_tables(positions):
    inv = 1.0 / (ROPE_THETA ** (jnp.arange(0, MLA_ROPE, 2, dtype=F32) / MLA_ROPE))
    ang = positions.astype(F32)[:, None] * inv
    cos, sin = jnp.cos(ang), jnp.sin(ang)
    one, zero = jnp.ones_like(cos), jnp.zeros_like(cos)
    a = jnp.concatenate([cos, cos, one, one], axis=1)
    up = jnp.concatenate([sin, zero, zero, zero], axis=1)
    down = jnp.concatenate([zero, sin, zero, zero], axis=1)
    return (a, -up, down), (a, up, -down)


def _rows_full(w):
    return w.reshape(w.shape[0] * w.shape[1], w.shape[2])


def _ops(depth):
    alpha = (2 * depth) ** 0.25
    return {'ln_res': rw_op(_ln_res_fn(alpha), "ln_res", 2, [F32], shadow=True),
            'ple': rw_op(_ple_fn, "ple", 3, [F32], shadow=True), 'gla_gate': rw_op(_gla_gate_fn, "gla_gate", 1, [F32])}


MLA_PRE_W = ['mla_w_in', 'mla_q_norm', 'mla_kv_norm', 'mla_w_uq', 'mla_w_uk', 'mla_w_uv']
FFN_IN_W = ['mla_w_o', 'ln1_g', 'ln1_b', 'ffn_conv_w', 'ffn_conv_b']
FFN_OUT_W = ['ffn_w_down', 'ln2_g', 'ln2_b', 'ple_w_gate', 'ple_w_proj', 'ple_b_gate']


def _mla_heads(wl, j):
    return N_DEV * wl['mla_w_uk'][j].shape[2] // MLA_NOPE


def _mla_pre(x, xb, wl, j):
    w_uq, w_uk, w_uv = wl['mla_w_uq'][j], wl['mla_w_uk'][j], wl['mla_w_uv'][j]
    h = linear_sh(x, xb, _rows_full(wl['mla_w_in'][j]), F32)
    mla_norm = rw_op(_mla_norm_fn(w_uq.shape[1], w_uk.shape[1]), "mla_norm", 1, [BF16, BF16, F32])
    cq, ckv, kr_raw = mla_norm(h, wl['mla_q_norm'][j], wl['mla_kv_norm'][j])
    return linear(cq, w_uq, BF16), linear(ckv, w_uk, BF16), linear(ckv, w_uv, BF16), kr_raw


def _gla_mixer(x, xb, wl, j, ops):
    w_in3, w_a2 = wl['gla_w_in'][j], wl['gla_w_a2'][j]
    w_o = _rows_full(wl['gla_w_o'][j])
    w_in = jnp.transpose(w_in3, (1, 0, 2)).reshape(w_in3.shape[1], N_DEV * w_in3.shape[2])
    heads = w_o.shape[0] // GLA_DV
    n_main = 2 * heads * GLA_DK + 2 * heads * GLA_DV
    w_a = jnp.pad(w_in[:, n_main:], ((0, 0), (0, LANES - GLA_RANK)))
    w_a2_p = jnp.pad(w_a2, ((0, LANES - GLA_RANK), (0, 0))).astype(BF16)
    hm = linear_sh(x, xb, w_in[:, :n_main], BF16)
    ha = linear_sh(x, xb, w_a, BF16)
    (la,) = ops['gla_gate'](linear(ha, w_a2_p, F32), wl['gla_b_a'][j])
    return linear(gla_core(hm, la, wl['gla_o_norm'][j], heads), w_o, F32)


def _ffn_in(x, m, wl, i, ops, bp):
    x1, x1b = ops['ln_res'](x, m, wl['ln1_g'][i], wl['ln1_b'][i])
    cw, cb = wl['ffn_conv_w'][i], wl['ffn_conv_b'][i]
    bu = cw.shape[1] // N_DEV
    cwp = jnp.pad(cw.reshape(CONV_W, N_DEV, bu), ((0, 0), (0, 0), (0, bp - bu))).reshape(CONV_W, N_DEV * bp)
    cbp = jnp.pad(cb.reshape(1, N_DEV, bu), ((0, 0), (0, 0), (0, bp - bu))).reshape(1, N_DEV * bp)
    return x1, lax.stop_gradient(x1b), cwp, cbp


def _ffn_out(x1, a, wl, p_i, i, ops):
    w_down3 = wl['ffn_w_down'][i]
    half, bu, d_model = N_DEV // 2, 2 * w_down3.shape[1], w_down3.shape[2]
    bp = a.shape[1] // half
    w_down = jnp.pad(w_down3.reshape(half, bu, d_model), ((0, 0), (0, bp - bu), (0, 0))).reshape(half * bp, d_model)
    f = linear(a, w_down, F32)
    x2, x2b = ops['ln_res'](x1, f, wl['ln2_g'][i], wl['ln2_b'][i])
    glog = linear_sh(x2, lax.stop_gradient(x2b), _rows_full(wl['ple_w_gate'][i]), F32)
    pp = linear(p_i, wl['ple_w_proj'][i], F32)
    x, xb = ops['ple'](x2, glog, pp, wl['ple_b_gate'][i])
    return x, lax.stop_gradient(xb)


def _layer(x, xb, wl, p_i, aux, i, ops):
    j = i // 2
    if i % 2 == 0:
        q_raw, kn, v, kr_raw = _mla_pre(x, xb, wl, j)
        o = make_attention(aux, _mla_heads(wl, j))[0](q_raw, kn, v, kr_raw)
        m = linear(o, _rows_full(wl['mla_w_o'][j]), F32)
    else:
        m = _gla_mixer(x, xb, wl, j, ops)
    x1, x1b, cwp, cbp = _ffn_in(x, m, wl, i, ops, wl['ffn_w_up'][i].shape[2])
    return _ffn_out(x1, ffn_hidden(x1, x1b, wl['ffn_w_up'][i], cwp, cbp), wl, p_i, i, ops)


def kernel(x, p, positions, mla_w_in, mla_q_norm, mla_kv_norm, mla_w_uq, mla_w_uk, mla_w_uv, mla_w_o, gla_w_in, gla_w_a2, gla_b_a, gla_o_norm, gla_w_o, ln1_g, ln1_b, ln2_g, ln2_b, ffn_w_up, ffn_conv_w, ffn_conv_b, ffn_w_down, ple_w_proj, ple_w_gate, ple_b_gate, loss_target, m_mla_w_in, m_mla_q_norm, m_mla_kv_norm, m_mla_w_uq, m_mla_w_uk, m_mla_w_uv, m_mla_w_o, m_gla_w_in, m_gla_w_a2, m_gla_b_a, m_gla_o_norm, m_gla_w_o, m_ln1_g, m_ln1_b, m_ln2_g, m_ln2_b, m_ffn_w_up, m_ffn_conv_w, m_ffn_conv_b, m_ffn_w_down, m_ple_w_proj, m_ple_w_gate, m_ple_b_gate, v_mla_w_in, v_mla_q_norm, v_mla_kv_norm, v_mla_w_uq, v_mla_w_uk, v_mla_w_uv, v_mla_w_o, v_gla_w_in, v_gla_w_a2, v_gla_b_a, v_gla_o_norm, v_gla_w_o, v_ln1_g, v_ln1_b, v_ln2_g, v_ln2_b, v_ffn_w_up, v_ffn_conv_w, v_ffn_conv_b, v_ffn_w_down, v_ple_w_proj, v_ple_w_gate, v_ple_b_gate):
    w = dict(zip(WEIGHTS, (mla_w_in, mla_q_norm, mla_kv_norm, mla_w_uq, mla_w_uk, mla_w_uv, mla_w_o, gla_w_in, gla_w_a2,
                           gla_b_a, gla_o_norm, gla_w_o, ln1_g, ln1_b, ln2_g, ln2_b, ffn_w_up, ffn_conv_w, ffn_conv_b,
                           ffn_w_down, ple_w_proj, ple_w_gate, ple_b_gate)))
    m_in = dict(zip(WEIGHTS, (m_mla_w_in, m_mla_q_norm, m_mla_kv_norm, m_mla_w_uq, m_mla_w_uk, m_mla_w_uv, m_mla_w_o,
                              m_gla_w_in, m_gla_w_a2, m_gla_b_a, m_gla_o_norm, m_gla_w_o, m_ln1_g, m_ln1_b, m_ln2_g,
                              m_ln2_b, m_ffn_w_up, m_ffn_conv_w, m_ffn_conv_b, m_ffn_w_down, m_ple_w_proj, m_ple_w_gate,
                              m_ple_b_gate)))
    v_in = dict(zip(WEIGHTS, (v_mla_w_in, v_mla_q_norm, v_mla_kv_norm, v_mla_w_uq, v_mla_w_uk, v_mla_w_uv, v_mla_w_o,
                              v_gla_w_in, v_gla_w_a2, v_gla_b_a, v_gla_o_norm, v_gla_w_o, v_ln1_g, v_ln1_b, v_ln2_g,
                              v_ln2_b, v_ffn_w_up, v_ffn_conv_w, v_ffn_conv_b, v_ffn_w_down, v_ple_w_proj, v_ple_w_gate,
                              v_ple_b_gate)))
    _uid[0] = itertools.count()
    x2d, target, pos = x[0], loss_target[0], positions[0]
    p3 = p[:, 0]
    dims = {'h_width': mla_w_uq.shape[1] + mla_w_uk.shape[1] + LANES}

    depth = ln1_g.shape[0]
    ops = _ops(depth)
    cid = pos // CHUNK
    aux = {'rope': _rope_tables(pos), 'cidq': cid[:, None], 'cidk': cid[None, :],
           'need': _mask_table(cid, _attn_tile(pos.shape[0]))}

    in_layer0 = set(MLA_PRE_W + FFN_IN_W + FFN_OUT_W + ['ffn_w_up'])
    prepped = {n: _prep_big(n, w[n], dims) for n in BIG}
    first_names = [n for n in BIG if n in MLA_PRE_W + ['mla_w_o']]
    mid_names = [n for n in BIG if n in in_layer0 and n not in first_names]
    rest_names = [n for n in BIG if prepped[n].shape[0] > (1 if n in in_layer0 else 0)]
    rest_from = {n: (1 if n in in_layer0 else 0) for n in rest_names}
    small3 = [_as3(w[n]) for n in SMALL]
    small_packed, small_sizes = _pack([s.reshape(1, -1) for s in small3], F32, 8)
    first = all_gather([prepped[n][:1] for n in first_names] + [small_packed])
    wl = {n: [None] * prepped[n].shape[0] for n in BIG}
    for n, g in zip(first_names, first):
        wl[n][0] = g[0]
    for n, s3, flat in zip(SMALL, small3, _unpack(first[-1][0], small_sizes)):
        wl[n] = _unshard(flat.reshape(N_DEV, *s3.shape), SHARD_AXIS[n] if w[n].ndim == 3 else 2)
    for n in REPL:
        wl[n] = [w[n][l][None, :] for l in range(w[n].shape[0])]

    def pick(names, layer0):
        return {n: [wl[n][l] if (l == 0 and n in in_layer0) == layer0 else None for l in range(len(wl[n]))] for n in names}

    heads0 = _mla_heads(wl, 0)
    _, attn_run, attn_bwd = make_attention(aux, heads0)
    x2db = x2d.astype(BF16)
    pre, vjp_pre = jax.vjp(lambda x_, wl_: _mla_pre(x_, x2db, wl_, 0), x2d, pick(MLA_PRE_W, True))
    o, attn_res, partial = attn_run(*pre, comm_gather_own([prepped[n][:1] for n in mid_names]
                                                          + [prepped[n][rest_from[n]:] for n in rest_names]))
    for n, g in zip(mid_names, _run_comm(comm_gather_pass(partial[:len(mid_names)]), "gather_pass")):
        wl[n][0] = g[0]
    partial = partial[len(mid_names):]
    w_up0 = wl['ffn_w_up'][0]

    def ffn_in(x_, o_, wl_):
        x1_, x1b_, cwp_, cbp_ = _ffn_in(x_, linear(o_, _rows_full(wl_['mla_w_o'][0]), F32), wl_, 0, ops, w_up0.shape[2])
        return (x1_, cwp_, cbp_), x1b_

    (x1, cwp, cbp), vjp_in, x1b = jax.vjp(ffn_in, x2d, o, pick(FFN_IN_W, True), has_aux=True)
    a, ffn_res, rest = _ffn_hidden_fwd(x1, x1b, w_up0, cwp, cbp, comm_gather_pass(partial))
    for n, g in zip(rest_names, rest):
        for l in range(g.shape[0]):
            wl[n][rest_from[n] + l] = g[l]
    x_l0, vjp_out, x_l0b = jax.vjp(lambda x1_, a_, wl_: _ffn_out(x1_, a_, wl_, p3[0], 0, ops), x1, a,
                                   pick(FFN_OUT_W, True), has_aux=True)

    def tail(x_, wl_):
        xb_ = x_l0b
        for i in range(1, depth):
            x_, xb_ = _layer(x_, xb_, wl_, p3[i], aux, i, ops)
        return x_

    y, vjp_tail = jax.vjp(tail, x_l0, pick(WEIGHTS, False))
    sq, dy = _loss_call(y, target)

    dwl = {n: [None] * len(wl[n]) for n in WEIGHTS}

    def keep(part):
        for n, per_layer in part.items():
            for l, g in enumerate(per_layer):
                if g is not None:
                    dwl[n][l] = g

    x_c, y_c, c_place = _place()
    c_idx = jnp.reshape(c_place, (1,)).astype(jnp.int32)
    chip_idx = jnp.reshape(2 * x_c + y_c, (1,)).astype(jnp.int32)
    dx_l0, d_tail = vjp_tail(dy)
    keep(d_tail)
    g_rest = [jnp.stack(dwl[n][rest_from[n]:], axis=0) for n in rest_names]
    dx1_out, da, d_out = vjp_out(dx_l0)
    keep(d_out)
    (dx1_ffn, _, dw_up0, dcwp, dcbp), recv1 = _ffn_hidden_bwd(ffn_res, da, comm_rs_pair(g_rest))
    dwl['ffn_w_up'][0] = dw_up0
    g_mid = [dwl[n][0][None] for n in mid_names]
    g_rest, recv1 = g_rest + g_mid, list(recv1) + list(_rs_pair_exchange(g_mid))
    p1 = [_rs_pair_add(g, r, c_idx) for g, r in zip(g_rest, recv1)]
    dx_in, do, d_in = vjp_in((dx1_out + dx1_ffn, dcwp, dcbp))
    keep(d_in)
    d_pre_in, recv2 = attn_bwd(attn_res, do, comm_rs_chip(p1))
    red_rest = [_rs_final_add(p_, r, chip_idx) for p_, r in zip(p1, recv2)]
    dx_pre, d_pre = vjp_pre(d_pre_in)
    keep(d_pre)
    dx = dx_pre + dx_in

    small_blocks = [_reshard(dwl[n], SHARD_AXIS[n] if w[n].ndim == 3 else 2).reshape(N_DEV, -1) for n in SMALL]
    small_grad_packed, _ = _pack(small_blocks, F32, 8)
    red_first = reduce_scatter([dwl[n][0][None] for n in first_names] + [small_grad_packed[None]])
    by_layer = {n: [] for n in BIG}
    for n, g in zip(first_names, red_first):
        by_layer[n].append(g)
    for n, g in zip(mid_names, red_rest[len(rest_names):]):
        by_layer[n].append(g)
    for n, g in zip(rest_names, red_rest):
        by_layer[n].append(g)
    grads = {n: _unprep_big(n, jnp.concatenate(by_layer[n], axis=0), w[n].shape) for n in BIG}
    for n, f in zip(SMALL, _unpack(red_first[-1][0], small_sizes)):
        grads[n] = f.reshape(w[n].shape)

    repl_flat = [jnp.concatenate([g.reshape(-1) for g in dwl[n]]).reshape(1, -1) for n in REPL]
    loss_part = 0.5 * jnp.sum(sq) / sq.shape[1]
    packed, repl_sizes = _pack(repl_flat + [loss_part.reshape(1, 1)], F32, 8)
    summed = _unpack(all_reduce_small(packed[0])[None], repl_sizes)
    for n, f in zip(REPL, summed[:-1]):
        grads[n] = f.reshape(w[n].shape)
    loss = summed[-1].reshape(())

    delta, new_m, new_v = {}, {}, {}
    for n in WEIGHTS:
        delta[n], new_m[n], new_v[n] = _adamw_call(w[n], grads[n], m_in[n], v_in[n])
    return (loss, dx[None], *[grads[n] for n in WEIGHTS], *[delta[n] for n in WEIGHTS],
            *[new_m[n] for n in WEIGHTS], *[new_v[n] for n in WEIGHTS])
```

```python
import functools
import itertools

import jax
import jax.numpy as jnp
from jax import lax
from jax.experimental import pallas as pl
from jax.experimental.pallas import tpu as pltpu

F32 = jnp.float32
BF16 = jnp.bfloat16
MESH = pl.DeviceIdType.MESH
N_DEV = 8

EPS = 1e-5
NEG_INF = -1e30
CHUNK = 64
Q_BLOCK = 128
MLA_NOPE = 128
MLA_ROPE = 64
MLA_V = 128
MLA_QK_PAD = 256
ROPE_THETA = 10000.0
GLA_DK = 128
GLA_DV = 256
GLA_RANK = 16
GLA_TAU = 16.0
CONV_W = 3
ADAM_LR = 0.001
ADAM_B1 = 0.9
ADAM_B2 = 0.999
ADAM_EPS = 1e-08
ADAM_WD = 0.01
ADAM_STEP = 10
LOG2E = 1.4426950408889634

LANES = 128
PACK_COLS = 1024
VMEM_LIMIT = 56 * 1024 * 1024
MM_VMEM_BUDGET = 38 * 1024 * 1024

WEIGHTS = ['mla_w_in', 'mla_q_norm', 'mla_kv_norm', 'mla_w_uq', 'mla_w_uk', 'mla_w_uv', 'mla_w_o', 'gla_w_in',
           'gla_w_a2', 'gla_b_a', 'gla_o_norm', 'gla_w_o', 'ln1_g', 'ln1_b', 'ln2_g', 'ln2_b', 'ffn_w_up',
           'ffn_conv_w', 'ffn_conv_b', 'ffn_w_down', 'ple_w_proj', 'ple_w_gate', 'ple_b_gate']
SHARD_AXIS = {'mla_w_in': 1, 'mla_q_norm': None, 'mla_kv_norm': None, 'mla_w_uq': 2, 'mla_w_uk': 2, 'mla_w_uv': 2,
              'mla_w_o': 1, 'gla_w_in': 2, 'gla_w_a2': 2, 'gla_b_a': 1, 'gla_o_norm': 1, 'gla_w_o': 1,
              'ln1_g': None, 'ln1_b': None, 'ln2_g': None, 'ln2_b': None, 'ffn_w_up': 2, 'ffn_conv_w': 2,
              'ffn_conv_b': None, 'ffn_w_down': 1, 'ple_w_proj': 2, 'ple_w_gate': 1, 'ple_b_gate': None}
BIG = ['mla_w_in', 'mla_w_uq', 'mla_w_uk', 'mla_w_uv', 'mla_w_o', 'gla_w_in', 'gla_w_o', 'ffn_w_up', 'ffn_w_down',
       'ple_w_proj', 'ple_w_gate']
SMALL = ['gla_w_a2', 'gla_b_a', 'gla_o_norm', 'ffn_conv_w']
REPL = [n for n in WEIGHTS if SHARD_AXIS[n] is None]

_uid = [itertools.count()]


def _nm(base):
    return f"{base}_{next(_uid[0])}"


def _cp(sem=None):
    return pltpu.CompilerParams(dimension_semantics=sem, vmem_limit_bytes=VMEM_LIMIT)


def _round_up(n, m):
    return -(-n // m) * m


class Comm:
    MID_AT = 0.6

    def __init__(self, inputs, out_shapes, aliases, n_send, n_recv, n_local, build):
        self.inputs, self.out_shapes, self.aliases = list(inputs), list(out_shapes), dict(aliases)
        self.n_send, self.n_recv, self.n_local, self.build = n_send, n_recv, n_local, build


def _pcall(body, name, grid, n_prefetch, in_specs, out_specs, out_shape, scratch, sem, args, comm=None):
    in_specs, out_specs, out_shape, scratch, args = list(in_specs), list(out_specs), list(out_shape), list(scratch), list(args)
    n_in, n_out, n_scr = len(in_specs), len(out_specs), len(scratch)
    aliases = {}
    kernel_body = body
    if comm is not None:
        ci, co = len(comm.inputs), len(comm.out_shapes)
        any_spec = pl.BlockSpec(memory_space=pl.ANY)
        in_specs += [any_spec] * ci
        out_specs += [any_spec] * co
        out_shape += comm.out_shapes
        scratch += [pltpu.SemaphoreType.DMA((comm.n_send,)), pltpu.SemaphoreType.DMA((comm.n_recv,)),
                    pltpu.SemaphoreType.DMA((comm.n_local,))]
        aliases = {n_prefetch + n_in + k: n_out + v for k, v in comm.aliases.items()}
        args += comm.inputs
        sem = ("arbitrary",) * len(grid)

        def kernel_body(*refs):
            pre, r = refs[:n_prefetch], refs[n_prefetch:]
            ins, cin = r[:n_in], r[n_in:n_in + ci]
            outs, cout = r[n_in + ci:n_in + ci + n_out], r[n_in + ci + n_out:n_in + ci + n_out + co]
            scr = r[n_in + ci + n_out + co:n_in + ci + n_out + co + n_scr]
            send_sems, recv_sems, local_sems = r[-3:]
            first = functools.reduce(lambda a, b: a & b, [pl.program_id(d) == 0 for d in range(len(grid))])
            last = functools.reduce(lambda a, b: a & b, [pl.program_id(d) == grid[d] - 1 for d in range(len(grid))])
            built = comm.build(cin, cout, send_sems, recv_sems, local_sems)
            starts, waits = built[0], built[-1]
            mid_waits, mid_starts = (built[1], built[2]) if len(built) == 4 else ([], [])

            @pl.when(first)
            def _():
                for cp in starts:
                    cp.start()

            if mid_starts:
                step = pl.program_id(0)
                for d in range(1, len(grid)):
                    step = step * grid[d] + pl.program_id(d)
                n_steps = functools.reduce(lambda a, b: a * b, grid)

                @pl.when(step == int(n_steps * comm.MID_AT))
                def _():
                    for wait in mid_waits:
                        wait()
                    for cp in mid_starts:
                        cp.start()

            body(*pre, *ins, *outs, *scr)

            @pl.when(last)
            def _():
                for wait in waits:
                    wait()

    grid_spec = pltpu.PrefetchScalarGridSpec(num_scalar_prefetch=n_prefetch, grid=grid, in_specs=in_specs,
                                             out_specs=out_specs, scratch_shapes=scratch)
    return pl.pallas_call(kernel_body, name=_nm(name), grid_spec=grid_spec, out_shape=tuple(out_shape),
                          input_output_aliases=aliases, compiler_params=_cp(sem))(*args)


def _run_comm(comm, name):
    return _pcall(lambda: None, name, (1,), 0, [], [], [], [], ("arbitrary",), [], comm)


def _divisor_tiles(n, cap):
    if n % LANES:
        return [n]
    out = [t for t in range(LANES, min(n, cap) + 1, LANES) if n % t == 0]
    return out or [n]


def _mm_tiles(M, N, K, abytes, bbytes, obytes, tn_fixed=None, tk_fixed=None):
    best = None
    for tm in _divisor_tiles(M, 1024):
        for tn in ([tn_fixed] if tn_fixed else _divisor_tiles(N, 1536)):
            for tk in ([tk_fixed] if tk_fixed else _divisor_tiles(K, 4096)):
                vmem = 2 * (tm * tk * abytes + tk * tn * bbytes + tm * tn * obytes) + tm * tn * 4
                if vmem > MM_VMEM_BUDGET:
                    continue
                key = (tm * tn * tk, tk)
                if best is None or key > best[0]:
                    best = (key, (tm, tn, tk))
    assert best is not None, (M, N, K)
    return best[1]


def _mm(a, b, mode, out_dtype, base="mm", blocks=None):
    blk0, nblk = blocks if blocks else (0, 1)
    tn_fixed = tk_fixed = None
    if mode == "nn":
        M, K = a.shape
        N = nblk * b.shape[2] if blocks else b.shape[1]
        tn_fixed = b.shape[2] if blocks else None
    elif mode == "nt":
        M, K = a.shape
        N = b.shape[1] if blocks else b.shape[0]
        tk_fixed = b.shape[2] if blocks else None
        assert not blocks or K == nblk * b.shape[2]
    else:
        (K, M), N = a.shape, b.shape[1]
        tn_fixed = N // nblk if blocks else None
    tm, tn, tk = _mm_tiles(M, N, K, a.dtype.itemsize, b.dtype.itemsize, jnp.dtype(out_dtype).itemsize, tn_fixed, tk_fixed)
    nk = K // tk
    out_shape = jax.ShapeDtypeStruct((M, N), out_dtype)
    out_spec = pl.BlockSpec((tm, tn), lambda i, j, k: (i, j))
    if mode == "nn":
        a_spec = pl.BlockSpec((tm, tk), lambda i, j, k: (i, k))
        b_spec = (pl.BlockSpec((None, tk, tn), lambda i, j, k: (blk0 + j, k, 0)) if blocks
                  else pl.BlockSpec((tk, tn), lambda i, j, k: (k, j)))
        dims = (((1,), (0,)), ((), ()))
    elif mode == "nt":
        a_spec = pl.BlockSpec((tm, tk), lambda i, j, k: (i, k))
        b_spec = (pl.BlockSpec((None, tn, tk), lambda i, j, k: (blk0 + k, j, 0)) if blocks
                  else pl.BlockSpec((tn, tk), lambda i, j, k: (j, k)))
        dims = (((1,), (1,)), ((), ()))
    else:
        a_spec = pl.BlockSpec((tk, tm), lambda i, j, k: (k, i))
        b_spec = pl.BlockSpec((tk, tn), lambda i, j, k: (k, j))
        dims = (((0,), (0,)), ((), ()))
        if blocks:
            out_shape = jax.ShapeDtypeStruct((nblk, M, tn), out_dtype)
            out_spec = pl.BlockSpec((None, tm, tn), lambda i, j, k: (j, i, 0))

    def body(a_ref, b_ref, o_ref, acc_ref):
        part = lax.dot_general(a_ref[...].astype(BF16), b_ref[...].astype(BF16), dims, preferred_element_type=F32)
        if nk == 1:
            o_ref[...] = part.astype(o_ref.dtype)
        else:
            k = pl.program_id(2)

            @pl.when(k == 0)
            def _():
                acc_ref[...] = part

            @pl.when(k > 0)
            def _():
                acc_ref[...] += part

            @pl.when(k == nk - 1)
            def _():
                o_ref[...] = acc_ref[...].astype(o_ref.dtype)

    return pl.pallas_call(
        body, name=_nm(base), grid=(M // tm, N // tn, nk), out_shape=out_shape,
        in_specs=[a_spec, b_spec], out_specs=out_spec,
        scratch_shapes=[pltpu.VMEM((tm, tn) if nk > 1 else (8, LANES), F32)],
        compiler_params=_cp(("parallel", "parallel", "arbitrary")),
    )(a, b)


def _all_blocks(w):
    return (0, w.shape[0]) if w.ndim == 3 else None


@functools.partial(jax.custom_vjp, nondiff_argnums=(2,))
def linear(a, w, out_dtype):
    return _mm(a, w, "nn", out_dtype, "lin_fwd", _all_blocks(w))


def _linear_fwd(a, w, out_dtype):
    return _mm(a, w, "nn", out_dtype, "lin_fwd", _all_blocks(w)), (a, w)


def _linear_bwd(out_dtype, res, dy):
    a, w = res
    return (_mm(dy, w, "nt", a.dtype, "lin_dx", _all_blocks(w)), _mm(a, dy, "tn", w.dtype, "lin_dw", _all_blocks(w)))


linear.defvjp(_linear_fwd, _linear_bwd)


@functools.partial(jax.custom_vjp, nondiff_argnums=(3,))
def linear_sh(a, a_bf16, w, out_dtype):
    return _mm(a_bf16, w, "nn", out_dtype, "lin_fwd", _all_blocks(w))


def _linear_sh_fwd(a, a_bf16, w, out_dtype):
    return _mm(a_bf16, w, "nn", out_dtype, "lin_fwd", _all_blocks(w)), (a_bf16, w, jnp.zeros((), a.dtype))


def _linear_sh_bwd(out_dtype, res, dy):
    a_bf16, w, tok = res
    return (_mm(dy, w, "nt", tok.dtype, "lin_dx", _all_blocks(w)), jnp.zeros_like(a_bf16),
            _mm(a_bf16, dy, "tn", w.dtype, "lin_dw", _all_blocks(w)))


linear_sh.defvjp(_linear_sh_fwd, _linear_sh_bwd)


def _row_tile(S, width):
    tr = 512 if width <= 1024 else 256
    return min(tr, S)


def _rw_fwd(f, rows, params, out_dtypes, base):
    S = rows[0][0].shape[0]
    tr = _row_tile(S, max(w for _, w, _ in rows))
    n_in = len(rows) + len(params)
    avals = [jax.ShapeDtypeStruct((tr, w), F32) for _, w, _ in rows] + [jax.ShapeDtypeStruct(p.shape, F32) for p in params]
    outs = jax.eval_shape(f, *avals)

    def body(*refs):
        vals = [r[...].astype(F32) for r in refs[:n_in]]
        for o_ref, r in zip(refs[n_in:], f(*vals)):
            o_ref[...] = r.astype(o_ref.dtype)

    in_specs = [pl.BlockSpec((tr, w), functools.partial(lambda i, cb: (i, cb), cb=cb)) for _, w, cb in rows]
    in_specs += [pl.BlockSpec(p.shape, lambda i: (0, 0)) for p in params]
    return pl.pallas_call(
        body, name=_nm(base), grid=(S // tr,),
        out_shape=tuple(jax.ShapeDtypeStruct((S, o.shape[1]), dt) for o, dt in zip(outs, out_dtypes)),
        in_specs=in_specs, out_specs=tuple(pl.BlockSpec((tr, o.shape[1]), lambda i: (i, 0)) for o in outs),
        compiler_params=_cp(("parallel",)),
    )(*[a for a, _, _ in rows], *params)


def _rw_bwd(f, rows, params, cts, row_grad_dtypes, base):
    S = rows[0][0].shape[0]
    tr = _row_tile(S, max(w for _, w, _ in rows))
    n_rows, n_par, n_ct = len(rows), len(params), len(cts)
    want = [k for k, dt in enumerate(row_grad_dtypes) if dt is not None]

    def body(*refs):
        in_refs = refs[:n_rows + n_par]
        ct_refs = refs[n_rows + n_par:n_rows + n_par + n_ct]
        out_refs = refs[n_rows + n_par + n_ct:]
        vals = [r[...].astype(F32) for r in in_refs]
        _, vjp_fn = jax.vjp(f, *vals)
        grads = vjp_fn(tuple(c[...].astype(F32) for c in ct_refs))
        for o_ref, k in zip(out_refs[:len(want)], want):
            o_ref[...] = grads[k].astype(o_ref.dtype)
        i = pl.program_id(0)
        for o_ref, g in zip(out_refs[len(want):], grads[n_rows:]):
            @pl.when(i == 0)
            def _(o_ref=o_ref, g=g):
                o_ref[...] = g

            @pl.when(i > 0)
            def _(o_ref=o_ref, g=g):
                o_ref[...] += g

    in_specs = [pl.BlockSpec((tr, w), functools.partial(lambda i, cb: (i, cb), cb=cb)) for _, w, cb in rows]
    in_specs += [pl.BlockSpec(p.shape, lambda i: (0, 0)) for p in params]
    in_specs += [pl.BlockSpec((tr, c.shape[1]), lambda i: (i, 0)) for c in cts]
    out_shape = [jax.ShapeDtypeStruct((S, rows[k][1]), row_grad_dtypes[k]) for k in want]
    out_specs = [pl.BlockSpec((tr, rows[k][1]), lambda i: (i, 0)) for k in want]
    out_shape += [jax.ShapeDtypeStruct(p.shape, F32) for p in params]
    out_specs += [pl.BlockSpec(p.shape, lambda i: (0, 0)) for p in params]
    res = pl.pallas_call(
        body, name=_nm(base), grid=(S // tr,), out_shape=tuple(out_shape),
        in_specs=in_specs, out_specs=tuple(out_specs), compiler_params=_cp(("arbitrary",)),
    )(*[a for a, _, _ in rows], *params, *cts)
    row_grads = [None] * n_rows
    for k, g in zip(want, res[:len(want)]):
        row_grads[k] = g
    return row_grads, list(res[len(want):])


def rw_op(f, base, n_rows, out_dtypes, shadow=False):
    f_fwd = (lambda *a: (lambda r: tuple(r) + (r[0],))(f(*a))) if shadow else f
    fwd_dtypes = list(out_dtypes) + ([BF16] if shadow else [])

    @jax.custom_vjp
    def op(*args):
        return fwd(*args)[0]

    def split(args):
        rows = [(a, a.shape[1], 0) for a in args[:n_rows]]
        return rows, list(args[n_rows:])

    def fwd(*args):
        rows, params = split(args)
        return tuple(_rw_fwd(f_fwd, rows, params, fwd_dtypes, base + "_fwd")), args

    def bwd(args, cts):
        rows, params = split(args)
        cts = list(cts)[:len(out_dtypes)]
        rg, pg = _rw_bwd(f, rows, params, cts, [a.dtype for a, _, _ in rows], base + "_bwd")
        return tuple(rg) + tuple(g.astype(p.dtype) for g, p in zip(pg, params))

    op.defvjp(fwd, bwd)
    return op


def _ln_res_fn(alpha):
    def f(x, m, g, b):
        z = alpha * x + m
        mu = jnp.mean(z, -1, keepdims=True)
        zc = z - mu
        var = jnp.mean(zc * zc, -1, keepdims=True)
        return (zc * lax.rsqrt(var + EPS) * g + b,)
    return f


def _rms(x, g):
    return x * lax.rsqrt(jnp.mean(x * x, -1, keepdims=True) + EPS) * g


def _mla_norm_fn(q_lora, kv_lora):
    def f(h, qn, kvn):
        return (_rms(h[:, :q_lora], qn), _rms(h[:, q_lora:q_lora + kv_lora], kvn),
                h[:, q_lora + kv_lora:q_lora + kv_lora + LANES])
    return f


def _log_sigmoid(z):
    return jnp.minimum(z, 0.0) - jnp.log(1.0 + jnp.exp(-jnp.abs(z)))


def _gla_gate_fn(z, b):
    return (_log_sigmoid(z + b) / GLA_TAU,)


def _ple_fn(x, glog, pp, b):
    return (x + jax.nn.sigmoid(glog + b) * pp,)


def _gla_out_fn(heads):
    def f(o, r, g):
        parts = []
        for h in range(heads):
            oh = o[:, h * GLA_DV:(h + 1) * GLA_DV]
            mu = jnp.mean(oh, -1, keepdims=True)
            oc = oh - mu
            var = jnp.mean(oc * oc, -1, keepdims=True)
            parts.append(oc * lax.rsqrt(var + EPS) * g[:, h * GLA_DV:(h + 1) * GLA_DV])
        return (jnp.concatenate(parts, axis=1) * (r * jax.nn.sigmoid(r)),)
    return f


def _rope_call(x, tabs, roped, out_dtype, base, fold=False, scale=None):
    S, C = x.shape
    nb = C // LANES
    tr = min(512 if C <= 1024 else 256, S)
    out_c = LANES if fold else C

    def rot(v, a, b1, b2):
        return v * a + pltpu.roll(v, 96, 1) * b1 + pltpu.roll(v, 32, 1) * b2

    def body(x_ref, a_ref, b1_ref, b2_ref, o_ref):
        a, b1, b2 = a_ref[...], b1_ref[...], b2_ref[...]
        if fold:
            v = x_ref[:, 0:LANES].astype(F32)
            for blk in range(1, nb):
                v = v + x_ref[:, blk * LANES:(blk + 1) * LANES].astype(F32)
            o_ref[...] = rot(v, a, b1, b2).astype(o_ref.dtype)
            return
        for blk in range(nb):
            v = x_ref[:, blk * LANES:(blk + 1) * LANES].astype(F32)
            if roped(blk):
                v = rot(v, a, b1, b2)
            if scale is not None:
                v = v * scale
            o_ref[:, blk * LANES:(blk + 1) * LANES] = v.astype(o_ref.dtype)

    row = lambda w: pl.BlockSpec((tr, w), lambda i: (i, 0))
    return pl.pallas_call(
        body, name=_nm(base), grid=(S // tr,), out_shape=jax.ShapeDtypeStruct((S, out_c), out_dtype),
        in_specs=[row(C), row(LANES), row(LANES), row(LANES)], out_specs=row(out_c),
        compiler_params=_cp(("parallel",)),
    )(x, *tabs)


def _attn_tile(S):
    return min(512, S)


def _tri_schedule(n, by_key):
    pairs = ([(i, j) for j in range(n) for i in range(j, n)] if by_key
             else [(i, j) for i in range(n) for j in range(i + 1)])
    return (jnp.asarray([p[0] for p in pairs], jnp.int32), jnp.asarray([p[1] for p in pairs], jnp.int32))


def _mask_table(cid, t):
    n = cid.shape[0] // t
    blocks = cid.reshape(n, t)
    cmin_q, cmax_k = jnp.min(blocks, axis=1), jnp.max(blocks, axis=1)
    need = (cmax_k[None, :] > cmin_q[:, None]) | jnp.eye(n, dtype=bool)
    return need.astype(jnp.int32).reshape(n * n)


def _attn_mask(cidq_ref, cidk_ref, i, j, t):
    qrow = i * t + lax.broadcasted_iota(jnp.int32, (t, 1), 0)
    kcol = j * t + lax.broadcasted_iota(jnp.int32, (1, t), 1)
    qlim = (qrow // Q_BLOCK + 1) * Q_BLOCK
    return (cidk_ref[...] <= cidq_ref[...]) & (kcol < qlim)


ATTN_FWD_HEADS = 8
ATTN_BWD_HEADS = 2
ATTN_SCALE = (MLA_NOPE + MLA_ROPE) ** -0.5
ATTN_SCALE2 = ATTN_SCALE * LOG2E


def _attn_fwd_call(q, kn, v, kr, aux, heads, comm=None):
    S = q.shape[0]
    t = _attn_tile(S)
    n = S // t
    hp = ATTN_FWD_HEADS if heads % ATTN_FWD_HEADS == 0 else 1
    qi_tab, kj_tab = _tri_schedule(n, False)

    def body(qi_ref, kj_ref, need_ref, q_ref, kn_ref, v_ref, kr_ref, cidq_ref, cidk_ref, o_ref, lse_ref,
             m_ref, l_ref, acc_ref):
        st = pl.program_id(1)
        i, j = qi_ref[st], kj_ref[st]

        @pl.when(j == 0)
        def _():
            m_ref[...] = jnp.full(m_ref.shape, NEG_INF, F32)
            l_ref[...] = jnp.zeros(l_ref.shape, F32)
            acc_ref[...] = jnp.zeros(acc_ref.shape, F32)

        def update(masked):
            mask = _attn_mask(cidq_ref, cidk_ref, i, j, t) if masked else None
            for hh in range(hp):
                lanes = slice(hh * LANES, (hh + 1) * LANES)
                k = jnp.concatenate([kn_ref[:, lanes], kr_ref[...]], axis=1)
                qh = q_ref[:, hh * MLA_QK_PAD:(hh + 1) * MLA_QK_PAD]
                s = lax.dot_general(qh, k, (((1,), (1,)), ((), ())), preferred_element_type=F32)
                if masked:
                    s = jnp.where(mask, s, NEG_INF)
                m_prev = m_ref[:, lanes]
                m_new = jnp.maximum(m_prev, jnp.max(s, axis=1, keepdims=True))
                alpha = jnp.exp2(m_prev - m_new)
                p = jnp.exp2(s - m_new[:, :1])
                l_ref[:, lanes] = alpha * l_ref[:, lanes] + jnp.sum(p, axis=1, keepdims=True)
                acc_ref[:, lanes] = alpha * acc_ref[:, lanes] + jnp.dot(p.astype(BF16), v_ref[:, lanes],
                                                                        preferred_element_type=F32)
                m_ref[:, lanes] = m_new

        need = need_ref[i * n + j]

        @pl.when(need != 0)
        def _():
            update(True)

        @pl.when(need == 0)
        def _():
            update(False)

        @pl.when(j == i)
        def _():
            o_ref[...] = (acc_ref[...] / l_ref[...]).astype(o_ref.dtype)
            lse_ref[...] = m_ref[...] + jnp.log(l_ref[...]) * LOG2E

    qmap = lambda h, s, qi, kj, need: (qi[s], h)
    kmap = lambda h, s, qi, kj, need: (kj[s], h)
    return _pcall(
        body, "attn_fwd", (heads // hp, qi_tab.shape[0]), 3,
        in_specs=[pl.BlockSpec((t, hp * MLA_QK_PAD), qmap), pl.BlockSpec((t, hp * MLA_NOPE), kmap),
                  pl.BlockSpec((t, hp * MLA_V), kmap),
                  pl.BlockSpec((t, LANES), lambda h, s, qi, kj, need: (kj[s], 0)),
                  pl.BlockSpec((t, 1), lambda h, s, qi, kj, need: (qi[s], 0)),
                  pl.BlockSpec((1, t), lambda h, s, qi, kj, need: (0, kj[s]))],
        out_specs=[pl.BlockSpec((t, hp * MLA_V), qmap), pl.BlockSpec((t, hp * LANES), qmap)],
        out_shape=[jax.ShapeDtypeStruct((S, heads * MLA_V), BF16), jax.ShapeDtypeStruct((S, heads * LANES), F32)],
        scratch=[pltpu.VMEM((t, hp * LANES), F32), pltpu.VMEM((t, hp * LANES), F32), pltpu.VMEM((t, hp * MLA_V), F32)],
        sem=("parallel", "arbitrary"),
        args=[qi_tab, kj_tab, aux['need'], q, kn, v, kr, aux['cidq'], aux['cidk']], comm=comm)


def _attn_bwd_call(q, kn, v, kr, o, lse, do, aux, heads, comm=None):
    S = q.shape[0]
    t = _attn_tile(S)
    n = S // t
    qi_tab, kj_tab = _tri_schedule(n, True)
    scale = ATTN_SCALE
    nt_dims = (((1,), (1,)), ((), ()))
    tn_dims = (((0,), (0,)), ((), ()))

    n_steps = n * (n + 1) // 2
    hp = ATTN_BWD_HEADS if heads % ATTN_BWD_HEADS == 0 else 1

    def body(qi_ref, kj_ref, need_ref, q_ref, kn_ref, v_ref, kr_ref, cidq_ref, cidk_ref, o_ref, lse_ref, do_ref,
             ta_ref, tb1_ref, tb2_ref, dq_ref, dkn_ref, dv_ref, dkr_ref, dk_acc, dv_acc, dq_acc):
        st = pl.program_id(1)
        i, j = qi_ref[st], kj_ref[st]

        @pl.when(st == 0)
        def _():
            dq_acc[...] = jnp.zeros(dq_acc.shape, F32)

        @pl.when(i == j)
        def _():
            dk_acc[...] = jnp.zeros(dk_acc.shape, F32)
            dv_acc[...] = jnp.zeros(dv_acc.shape, F32)

        rows = pl.ds(pl.multiple_of(i * t, t), t)

        def grads(masked):
            mask = _attn_mask(cidq_ref, cidk_ref, i, j, t) if masked else None
            for hh in range(hp):
                lanes = slice(hh * LANES, (hh + 1) * LANES)
                wide = slice(hh * MLA_QK_PAD, (hh + 1) * MLA_QK_PAD)
                k = jnp.concatenate([kn_ref[:, lanes], kr_ref[...]], axis=1)
                qt, do = q_ref[:, wide], do_ref[:, lanes]
                s = lax.dot_general(qt, k, nt_dims, preferred_element_type=F32)
                if masked:
                    s = jnp.where(mask, s, NEG_INF)
                dp = lax.dot_general(do, v_ref[:, lanes], nt_dims, preferred_element_type=F32)
                dsum = jnp.sum(do.astype(F32) * o_ref[:, lanes].astype(F32), axis=1, keepdims=True)
                p = jnp.exp2(s - lse_ref[:, hh * LANES:hh * LANES + 1])
                ds = (p * (dp - dsum) * scale).astype(BF16)
                dv_acc[:, lanes] += lax.dot_general(p.astype(BF16), do, tn_dims, preferred_element_type=F32)
                dk_acc[:, wide] += lax.dot_general(ds, qt, tn_dims, preferred_element_type=F32)
                dq_acc[rows, wide] += jnp.dot(ds, k, preferred_element_type=F32)

        need = need_ref[i * n + j]

        @pl.when(need != 0)
        def _():
            grads(True)

        @pl.when(need == 0)
        def _():
            grads(False)

        @pl.when(i == n - 1)
        def _():
            for hh in range(hp):
                lanes = slice(hh * LANES, (hh + 1) * LANES)
                off = hh * MLA_QK_PAD
                dkn_ref[:, lanes] = (dk_acc[:, off:off + MLA_NOPE] * (1.0 / ATTN_SCALE2)).astype(dkn_ref.dtype)
                dkr_ref[:, lanes] = dk_acc[:, off + MLA_NOPE:off + MLA_QK_PAD] * (1.0 / ATTN_SCALE2)
            dv_ref[...] = dv_acc[...].astype(dv_ref.dtype)

        @pl.when(st == n_steps - 1)
        def _():
            for r in range(n):
                rs = slice(r * t, (r + 1) * t)
                for hh in range(hp):
                    off = hh * MLA_QK_PAD
                    dq_ref[rs, off:off + MLA_NOPE] = dq_acc[rs, off:off + MLA_NOPE].astype(dq_ref.dtype)
                    g = dq_acc[rs, off + MLA_NOPE:off + MLA_QK_PAD]
                    g = g * ta_ref[rs, :] + pltpu.roll(g, 96, 1) * tb1_ref[rs, :] + pltpu.roll(g, 32, 1) * tb2_ref[rs, :]
                    dq_ref[rs, off + MLA_NOPE:off + MLA_QK_PAD] = g.astype(dq_ref.dtype)

    qmap = lambda h, s, qi, kj, need: (qi[s], h)
    kmap = lambda h, s, qi, kj, need: (kj[s], h)
    whole = pl.BlockSpec((S, LANES), lambda h, s, qi, kj, need: (0, 0))
    return _pcall(
        body, "attn_bwd", (heads // hp, qi_tab.shape[0]), 3,
        in_specs=[pl.BlockSpec((t, hp * MLA_QK_PAD), qmap), pl.BlockSpec((t, hp * MLA_NOPE), kmap),
                  pl.BlockSpec((t, hp * MLA_V), kmap),
                  pl.BlockSpec((t, LANES), lambda h, s, qi, kj, need: (kj[s], 0)),
                  pl.BlockSpec((t, 1), lambda h, s, qi, kj, need: (qi[s], 0)),
                  pl.BlockSpec((1, t), lambda h, s, qi, kj, need: (0, kj[s])),
                  pl.BlockSpec((t, hp * MLA_V), qmap), pl.BlockSpec((t, hp * LANES), qmap),
                  pl.BlockSpec((t, hp * MLA_V), qmap), whole, whole, whole],
        out_specs=[pl.BlockSpec((S, hp * MLA_QK_PAD), lambda h, s, qi, kj, need: (0, h)),
                   pl.BlockSpec((t, hp * MLA_NOPE), kmap), pl.BlockSpec((t, hp * MLA_V), kmap),
                   pl.BlockSpec((t, hp * LANES), kmap)],
        out_shape=[jax.ShapeDtypeStruct((S, heads * MLA_QK_PAD), BF16), jax.ShapeDtypeStruct((S, heads * MLA_NOPE), BF16),
                   jax.ShapeDtypeStruct((S, heads * MLA_V), BF16), jax.ShapeDtypeStruct((S, heads * LANES), F32)],
        scratch=[pltpu.VMEM((t, hp * MLA_QK_PAD), F32), pltpu.VMEM((t, hp * MLA_V), F32),
                 pltpu.VMEM((S, hp * MLA_QK_PAD), F32)],
        sem=("parallel", "arbitrary"),
        args=[qi_tab, kj_tab, aux['need'], q, kn, v, kr, aux['cidq'], aux['cidk'], o, lse, do, *aux['rope'][1]],
        comm=comm)


def make_attention(aux, heads):
    tabs_f, tabs_b = aux['rope']
    odd, every = (lambda blk: blk % 2 == 1), (lambda blk: True)

    def run_host(q_raw, kn, v, kr_raw, comm=None):
        q = _rope_call(q_raw, tabs_f, odd, BF16, "rope_q", scale=ATTN_SCALE2)
        kr = _rope_call(kr_raw, tabs_f, every, BF16, "rope_k")
        o, lse, *carried = _attn_fwd_call(q, kn, v, kr, aux, heads, comm)
        return o, (q, kn, v, kr, o, lse), carried

    def bwd_host(res, do, comm=None):
        q, kn, v, kr, o, lse = res
        dq_raw, dkn, dv, dkr, *carried = _attn_bwd_call(q, kn, v, kr, o, lse, do, aux, heads, comm)
        return (dq_raw, dkn, dv, _rope_call(dkr, tabs_b, every, F32, "rope_dk", fold=True)), carried

    @jax.custom_vjp
    def attn(q_raw, kn, v, kr_raw):
        return run_host(q_raw, kn, v, kr_raw)[0]

    attn.defvjp(lambda *a: run_host(*a)[:2], lambda res, do: bwd_host(res, do)[0])
    return attn, run_host, bwd_host


GLA_ROWS = 256
GLA_HEADS_PER_STEP = 2


def _tri(lower):
    r = lax.broadcasted_iota(jnp.int32, (CHUNK, CHUNK), 0)
    c = lax.broadcasted_iota(jnp.int32, (CHUNK, CHUNK), 1)
    return jnp.where((c <= r) if lower else (c >= r), 1.0, 0.0).astype(F32)


def _gla_chunk(q_ref, k_ref, v_ref, la_ref, sl, hh):
    lk, lv = slice(hh * GLA_DK, (hh + 1) * GLA_DK), slice(hh * GLA_DV, (hh + 1) * GLA_DV)
    la = la_ref[sl, lk]
    cum = jnp.dot(_tri(True), la, preferred_element_type=F32, precision=lax.Precision.HIGHEST)
    tot = cum[CHUNK - 1:CHUNK, :]
    e = jnp.exp(tot - cum)
    k = k_ref[sl, lk].astype(F32)
    kdec = k * e
    v = v_ref[sl, lv]
    upd_t = lax.dot_general(v.astype(BF16), kdec.astype(BF16), (((0,), (0,)), ((), ())), preferred_element_type=F32)
    qs = (q_ref[sl, lk].astype(F32) * (GLA_DK ** -0.5)).astype(BF16)
    return e, k, kdec, v, upd_t, jnp.exp(tot), qs


def _gla_group(heads):
    return GLA_HEADS_PER_STEP if heads % GLA_HEADS_PER_STEP == 0 else 1


def _gla_specs(heads, hp, rows_map):
    groups = heads // hp
    return [pl.BlockSpec((GLA_ROWS, hp * GLA_DK), lambda h, b: (rows_map(b), h)),
            pl.BlockSpec((GLA_ROWS, hp * GLA_DK), lambda h, b: (rows_map(b), groups + h)),
            pl.BlockSpec((GLA_ROWS, hp * GLA_DV), lambda h, b: (rows_map(b), groups + h)),
            pl.BlockSpec((GLA_ROWS, hp * GLA_DK), lambda h, b: (rows_map(b), h))]


def _gla_fwd_call(hm, la, heads):
    S = hm.shape[0]
    assert S % GLA_ROWS == 0
    nb, cpb, hp = S // GLA_ROWS, GLA_ROWS // CHUNK, _gla_group(heads)

    def body(q_ref, k_ref, v_ref, la_ref, o_ref, sp_ref, st_ref):
        @pl.when(pl.program_id(1) == 0)
        def _():
            st_ref[...] = jnp.zeros(st_ref.shape, F32)

        for c in range(cpb):
            sl = slice(c * CHUNK, (c + 1) * CHUNK)
            for hh in range(hp):
                _, _, _, _, upd_t, decay, qs = _gla_chunk(q_ref, k_ref, v_ref, la_ref, sl, hh)
                state = st_ref[hh]
                sp_ref[hh, c] = state
                state = state * decay + upd_t
                st_ref[hh] = state
                o_ref[sl, hh * GLA_DV:(hh + 1) * GLA_DV] = lax.dot_general(
                    qs, state.astype(BF16), (((1,), (1,)), ((), ())), preferred_element_type=F32)

    return pl.pallas_call(
        body, name=_nm("gla_fwd"), grid=(heads // hp, nb),
        out_shape=(jax.ShapeDtypeStruct((S, heads * GLA_DV), F32),
                   jax.ShapeDtypeStruct((heads, S // CHUNK, GLA_DV, GLA_DK), F32)),
        in_specs=_gla_specs(heads, hp, lambda b: b),
        out_specs=(pl.BlockSpec((GLA_ROWS, hp * GLA_DV), lambda h, b: (b, h)),
                   pl.BlockSpec((hp, cpb, GLA_DV, GLA_DK), lambda h, b: (h, b, 0, 0))),
        scratch_shapes=[pltpu.VMEM((hp, GLA_DV, GLA_DK), F32)],
        compiler_params=_cp(("parallel", "arbitrary")),
    )(hm, hm, hm, la)


def _gla_bwd_call(hm, la, sprev, do, heads):
    S = hm.shape[0]
    nb, cpb, hp = S // GLA_ROWS, GLA_ROWS // CHUNK, _gla_group(heads)
    scale = GLA_DK ** -0.5

    def body(q_ref, k_ref, v_ref, la_ref, sp_ref, do_ref, dq_ref, dk_ref, dv_ref, dla_ref, carry_ref):
        @pl.when(pl.program_id(1) == 0)
        def _():
            carry_ref[...] = jnp.zeros(carry_ref.shape, F32)

        for c in reversed(range(cpb)):
            sl = slice(c * CHUNK, (c + 1) * CHUNK)
            for hh in range(hp):
                lk, lv = slice(hh * GLA_DK, (hh + 1) * GLA_DK), slice(hh * GLA_DV, (hh + 1) * GLA_DV)
                e, k, kdec, v, upd_t, decay, qs = _gla_chunk(q_ref, k_ref, v_ref, la_ref, sl, hh)
                sp = sp_ref[hh, c]
                s_n = sp * decay + upd_t
                dob = do_ref[sl, lv].astype(BF16)
                g = carry_ref[hh] + lax.dot_general(dob, qs, (((0,), (0,)), ((), ())), preferred_element_type=F32)
                gb = g.astype(BF16)
                dq_ref[sl, lk] = (jnp.dot(dob, s_n.astype(BF16), preferred_element_type=F32) * scale).astype(dq_ref.dtype)
                ddecay = jnp.sum(g * sp, axis=0, keepdims=True)
                dkdec = jnp.dot(v.astype(BF16), gb, preferred_element_type=F32)
                dv_ref[sl, lv] = lax.dot_general(kdec.astype(BF16), gb, (((1,), (1,)), ((), ())),
                                                 preferred_element_type=F32).astype(dv_ref.dtype)
                dk_ref[sl, lk] = (dkdec * e).astype(dk_ref.dtype)
                w = dkdec * k * e
                dtot = jnp.sum(w, axis=0, keepdims=True) + ddecay * decay
                last = lax.broadcasted_iota(jnp.int32, (CHUNK, 1), 0) == CHUNK - 1
                dcum = jnp.where(last, dtot - w, -w)
                dla_ref[sl, lk] = jnp.dot(_tri(False), dcum, preferred_element_type=F32, precision=lax.Precision.HIGHEST)
                carry_ref[hh] = g * decay

    rev = lambda b: nb - 1 - b
    narrow = pl.BlockSpec((GLA_ROWS, hp * GLA_DK), lambda h, b: (rev(b), h))
    wide = pl.BlockSpec((GLA_ROWS, hp * GLA_DV), lambda h, b: (rev(b), h))
    return pl.pallas_call(
        body, name=_nm("gla_bwd"), grid=(heads // hp, nb),
        out_shape=(jax.ShapeDtypeStruct((S, heads * GLA_DK), hm.dtype), jax.ShapeDtypeStruct((S, heads * GLA_DK), hm.dtype),
                   jax.ShapeDtypeStruct((S, heads * GLA_DV), hm.dtype), jax.ShapeDtypeStruct((S, heads * GLA_DK), F32)),
        in_specs=_gla_specs(heads, hp, rev) + [
            pl.BlockSpec((hp, cpb, GLA_DV, GLA_DK), lambda h, b: (h, rev(b), 0, 0)), wide],
        out_specs=(narrow, narrow, wide, narrow),
        scratch_shapes=[pltpu.VMEM((hp, GLA_DV, GLA_DK), F32)],
        compiler_params=_cp(("parallel", "arbitrary")),
    )(hm, hm, hm, la, sprev, do)


@functools.partial(jax.custom_vjp, nondiff_argnums=(3,))
def gla_core(hm, la, o_norm, heads):
    return _gla_core_fwd(hm, la, o_norm, heads)[0]


def _gla_core_fwd(hm, la, o_norm, heads):
    o, sprev = _gla_fwd_call(hm, la, heads)
    vd = heads * GLA_DV
    rows = [(o, vd, 0), (hm, vd, 2 * heads * GLA_DK // vd + 1)]
    (y,) = _rw_fwd(_gla_out_fn(heads), rows, [o_norm], [BF16], "gla_out_fwd")
    return y, (hm, la, o_norm, o, sprev)


def _gla_core_bwd(heads, res, dy):
    hm, la, o_norm, o, sprev = res
    vd = heads * GLA_DV
    rows = [(o, vd, 0), (hm, vd, 2 * heads * GLA_DK // vd + 1)]
    (do, dr), (dg,) = _rw_bwd(_gla_out_fn(heads), rows, [o_norm], [dy], [F32, hm.dtype], "gla_out_bwd")
    dq, dk, dv, dla = _gla_bwd_call(hm, la, sprev, do, heads)
    return jnp.concatenate([dq, dk, dv, dr], axis=1), dla, dg


gla_core.defvjp(_gla_core_fwd, _gla_core_bwd)


CONV_COLS = 256
HALO = 16


def _conv_rows(S):
    return min(512, S)


def _conv_taps(main_ref, halo_ref, i):
    prev = jnp.where(i > 0, halo_ref[...].astype(F32), 0.0)
    full = jnp.concatenate([prev, main_ref[...].astype(F32)], axis=0)
    return full[HALO:], pltpu.roll(full, 1, 0)[HALO:], pltpu.roll(full, 2, 0)[HALO:]


def _conv_apply(taps, w_ref, b_ref):
    x0, x1, x2 = taps
    return x2 * w_ref[0:1, :] + x1 * w_ref[1:2, :] + x0 * w_ref[2:3, :] + b_ref[...]


def _gelu_gate(uc, gc):
    return uc * jax.nn.gelu(gc)


def _conv_cols(dff, pref):
    return max(c for c in range(LANES, pref + 1, LANES) if dff % c == 0)


def _conv_in_specs(R, C, nj):
    hpr = R // HALO
    main = lambda off: pl.BlockSpec((R, C), lambda j, i: (i, j + off))
    halo = lambda off: pl.BlockSpec((HALO, C), lambda j, i: (jnp.maximum(i * hpr - 1, 0), j + off))
    par = lambda rows, off: pl.BlockSpec((rows, C), lambda j, i: (0, j + off))
    return [main(0), halo(0), main(nj), halo(nj), par(CONV_W, 0), par(CONV_W, nj), par(1, 0), par(1, nj)]


def _conv_fwd_call(h, cw, cb, comm=None):
    S, dff = h.shape[0], h.shape[1] // 2
    R, C = _conv_rows(S), _conv_cols(dff, 768)
    nj = dff // C

    def body(u_ref, uh_ref, g_ref, gh_ref, wu_ref, wg_ref, bu_ref, bg_ref, a_ref):
        i = pl.program_id(1)
        uc = _conv_apply(_conv_taps(u_ref, uh_ref, i), wu_ref, bu_ref)
        gc = _conv_apply(_conv_taps(g_ref, gh_ref, i), wg_ref, bg_ref)
        a_ref[...] = _gelu_gate(uc, gc).astype(a_ref.dtype)

    return _pcall(
        body, "conv_fwd", (nj, S // R), 0, in_specs=_conv_in_specs(R, C, nj),
        out_specs=[pl.BlockSpec((R, C), lambda j, i: (i, j))], out_shape=[jax.ShapeDtypeStruct((S, dff), BF16)],
        scratch=[], sem=("parallel", "parallel"), args=[h, h, h, h, cw, cw, cb, cb], comm=comm)


def _conv_bwd_gate_call(h, cw, cb, da, comm=None):
    S, dff = h.shape[0], h.shape[1] // 2
    R, C = _conv_rows(S), _conv_cols(dff, 512)
    nj = dff // C

    def body(u_ref, uh_ref, g_ref, gh_ref, wu_ref, wg_ref, bu_ref, bg_ref, da_ref,
             du_ref, dg_ref, dwu_ref, dwg_ref, dbu_ref, dbg_ref):
        i = pl.program_id(1)
        ut, gt = _conv_taps(u_ref, uh_ref, i), _conv_taps(g_ref, gh_ref, i)
        uc, gc = _conv_apply(ut, wu_ref, bu_ref), _conv_apply(gt, wg_ref, bg_ref)
        _, vjp_fn = jax.vjp(_gelu_gate, uc, gc)
        du, dg = vjp_fn(da_ref[...].astype(F32))
        du_ref[...] = du.astype(du_ref.dtype)
        dg_ref[...] = dg.astype(dg_ref.dtype)

        @pl.when(i == 0)
        def _():
            for r in (dwu_ref, dwg_ref, dbu_ref, dbg_ref):
                r[...] = jnp.zeros(r.shape, F32)

        for d, taps, dw_ref, db_ref in ((du, ut, dwu_ref, dbu_ref), (dg, gt, dwg_ref, dbg_ref)):
            x0, x1, x2 = taps
            dw_ref[0:1, :] += jnp.sum(d * x2, axis=0, keepdims=True)
            dw_ref[1:2, :] += jnp.sum(d * x1, axis=0, keepdims=True)
            dw_ref[2:3, :] += jnp.sum(d * x0, axis=0, keepdims=True)
            db_ref[...] += jnp.sum(d, axis=0, keepdims=True)

    tile = pl.BlockSpec((R, C), lambda j, i: (i, j))
    par = lambda rows: pl.BlockSpec((rows, C), lambda j, i: (0, j))
    return _pcall(
        body, "conv_bwd_gate", (nj, S // R), 0, in_specs=_conv_in_specs(R, C, nj) + [tile],
        out_specs=[tile, tile, par(CONV_W), par(CONV_W), par(1), par(1)],
        out_shape=[jax.ShapeDtypeStruct((S, dff), BF16), jax.ShapeDtypeStruct((S, dff), BF16),
                   jax.ShapeDtypeStruct((CONV_W, dff), F32), jax.ShapeDtypeStruct((CONV_W, dff), F32),
                   jax.ShapeDtypeStruct((1, dff), F32), jax.ShapeDtypeStruct((1, dff), F32)],
        scratch=[], sem=("parallel", "arbitrary"), args=[h, h, h, h, cw, cw, cb, cb, da], comm=comm)


def _conv_bwd_shift_call(dc, cw, into=None):
    S, dff = dc.shape
    R, C = min(1024, S), _conv_cols(dff, 1024)
    nj = dff // C
    col_off = 0 if into is None else nj
    hpr, last = R // HALO, S // HALO - 1
    ni = S // R

    def body(d_ref, nx_ref, w_ref, *rest):
        o_ref = rest[-1]
        i = pl.program_id(1)
        nxt = jnp.where(i < ni - 1, nx_ref[...].astype(F32), 0.0)
        full = jnp.concatenate([d_ref[...].astype(F32), nxt], axis=0)
        n = R + HALO
        y1, y2 = pltpu.roll(full, n - 1, 0)[:R], pltpu.roll(full, n - 2, 0)[:R]
        o_ref[...] = (full[:R] * w_ref[2:3, :] + y1 * w_ref[1:2, :] + y2 * w_ref[0:1, :]).astype(o_ref.dtype)

    in_specs = [pl.BlockSpec((R, C), lambda j, i: (i, j)),
                pl.BlockSpec((HALO, C), lambda j, i: (jnp.minimum((i + 1) * hpr, last), j)),
                pl.BlockSpec((CONV_W, C), lambda j, i: (0, j + col_off))]
    args = [dc, dc, cw]
    if into is not None:
        in_specs.append(pl.BlockSpec(memory_space=pl.ANY))
        args.append(into)
    return pl.pallas_call(
        body, name=_nm("conv_bwd_shift"), grid=(nj, ni), out_shape=jax.ShapeDtypeStruct((S, 2 * dff), BF16),
        in_specs=in_specs, out_specs=pl.BlockSpec((R, C), lambda j, i: (i, j + col_off)),
        input_output_aliases={} if into is None else {3: 0},
        compiler_params=_cp(("parallel", "parallel")),
    )(*args)


@jax.custom_vjp
def ffn_hidden(x1, x1_bf16, w3, cw, cb):
    return _ffn_hidden_fwd(x1, x1_bf16, w3, cw, cb)[0]


def _ffn_hidden_fwd(x1, x1_bf16, w3, cw, cb, comm=None):
    h = _mm(x1_bf16, w3, "nn", BF16, "up", _all_blocks(w3))
    a, *carried = _conv_fwd_call(h, cw, cb, comm)
    return (a, (x1_bf16, w3, cw, cb, h)) + ((carried,) if comm is not None else ())


def _ffn_hidden_bwd(res, da, comm=None):
    x1, w3, cw, cb, h = res
    du, dg, dwu, dwg, dbu, dbg, *carried = _conv_bwd_gate_call(h, cw, cb, da, comm)
    dh = _conv_bwd_shift_call(dg, cw, into=_conv_bwd_shift_call(du, cw))
    dx = _mm(dh, w3, "nt", F32, "up_dx", _all_blocks(w3))
    dw3 = _mm(x1, dh, "tn", w3.dtype, "up_dw", _all_blocks(w3))
    grads = (dx, jnp.zeros_like(x1), dw3, jnp.concatenate([dwu, dwg], axis=1), jnp.concatenate([dbu, dbg], axis=1))
    return (grads, carried) if comm is not None else grads


ffn_hidden.defvjp(_ffn_hidden_fwd, _ffn_hidden_bwd)


def _loss_call(y, target):
    S, D = y.shape
    tr = _row_tile(S, D)

    def body(y_ref, t_ref, sq_ref, dy_ref):
        diff = y_ref[...] - t_ref[...]
        dy_ref[...] = diff * (1.0 / D)
        part = jnp.sum(diff * diff, axis=0, keepdims=True)
        i = pl.program_id(0)

        @pl.when(i == 0)
        def _():
            sq_ref[...] = part

        @pl.when(i > 0)
        def _():
            sq_ref[...] += part

    row = pl.BlockSpec((tr, D), lambda i: (i, 0))
    return pl.pallas_call(
        body, name=_nm("loss"), grid=(S // tr,),
        out_shape=(jax.ShapeDtypeStruct((1, D), F32), jax.ShapeDtypeStruct((S, D), F32)),
        in_specs=[row, row], out_specs=(pl.BlockSpec((1, D), lambda i: (0, 0)), row),
        compiler_params=_cp(("arbitrary",)),
    )(y, target)


def _row_divisor(rows):
    for cand in range(min(rows, 512), 15, -1):
        if rows % cand == 0 and cand % 16 == 0:
            return cand
    return rows


def _adamw_call(w, g, m, v):
    shape = w.shape
    w2, g2, m2, v2 = (a.reshape(-1, shape[-1]) for a in (w, g, m, v))
    rows, cols = w2.shape
    tr = _row_divisor(rows)

    def body(w_ref, g_ref, m_ref, v_ref, d_ref, nm_ref, nv_ref):
        g_ = g_ref[...]
        m_ = ADAM_B1 * m_ref[...] + (1.0 - ADAM_B1) * g_
        v_ = ADAM_B2 * v_ref[...] + (1.0 - ADAM_B2) * (g_ * g_)
        m_hat = m_ / (1.0 - ADAM_B1 ** ADAM_STEP)
        v_hat = v_ / (1.0 - ADAM_B2 ** ADAM_STEP)
        d_ref[...] = -ADAM_LR * (m_hat / (jnp.sqrt(v_hat) + ADAM_EPS) + ADAM_WD * w_ref[...])
        nm_ref[...] = m_
        nv_ref[...] = v_

    blk = pl.BlockSpec((tr, cols), lambda i: (i, 0))
    outs = pl.pallas_call(
        body, name=_nm("adamw"), grid=(rows // tr,),
        out_shape=tuple(jax.ShapeDtypeStruct((rows, cols), F32) for _ in range(3)),
        in_specs=[blk] * 4, out_specs=(blk,) * 3, compiler_params=_cp(("parallel",)),
    )(w2, g2, m2, v2)
    return tuple(o.reshape(shape) for o in outs)


ANY = pl.BlockSpec(memory_space=pl.ANY)


def _place():
    return lax.axis_index("x"), lax.axis_index("y"), lax.axis_index("c")


def all_gather(shards):
    n = len(shards)

    def body(*refs):
        x_refs, out_refs = refs[:n], refs[n:2 * n]
        send_sems, recv_sems, local_sems = refs[2 * n:]
        x, y, c = _place()
        me, sibling = (x, y, c), (x, y, 1 - c)
        chips = [(1 - x, y), (x, 1 - y), (1 - x, 1 - y)]

        def slot(a, px, py, pc):
            return out_refs[a].at[:, 4 * px + 2 * py + pc]

        def copy(a, k, block, to, own=False):
            return pltpu.make_async_remote_copy(
                src_ref=x_refs[a] if own else slot(a, *block), dst_ref=slot(a, *block),
                send_sem=send_sems.at[7 * a + k], recv_sem=recv_sems.at[7 * a + k], device_id=to, device_id_type=MESH)

        mine = [pltpu.make_async_copy(x_refs[a], slot(a, *me), local_sems.at[a]) for a in range(n)]
        first = []
        for a in range(n):
            mine[a].start()
            first.append(copy(a, 0, me, sibling, own=True))
            first += [copy(a, 1 + j, me, (*chip, c), own=True) for j, chip in enumerate(chips)]
        for cp in first:
            cp.start()
        passed = []
        for j, chip in enumerate(chips):
            for a in range(n):
                copy(a, 1 + j, (*chip, c), me).wait_recv()
                passed.append(copy(a, 4 + j, (*chip, c), sibling))
                passed[-1].start()
        for a in range(n):
            copy(a, 0, sibling, me).wait_recv()
        for j, chip in enumerate(chips):
            for a in range(n):
                copy(a, 4 + j, (*chip, 1 - c), me).wait_recv()
        for cp in first + passed:
            cp.wait_send()
        for cp in mine:
            cp.wait()

    return pl.pallas_call(
        body, name=_nm("all_gather"),
        out_shape=tuple(jax.ShapeDtypeStruct((s.shape[0], N_DEV) + s.shape[1:], s.dtype) for s in shards),
        in_specs=[ANY] * n, out_specs=(ANY,) * n,
        scratch_shapes=[pltpu.SemaphoreType.DMA((7 * n,)), pltpu.SemaphoreType.DMA((7 * n,)), pltpu.SemaphoreType.DMA((n,))],
    )(*shards)


def _rs_pair_exchange(gs):
    n = len(gs)

    def body(*refs):
        g_refs, recv_refs = refs[:n], refs[n:2 * n]
        send_sems, recv_sems = refs[2 * n:]
        x, y, c = _place()
        copies = [pltpu.make_async_remote_copy(
            src_ref=g_refs[a].at[:, 2 * j + (1 - c)], dst_ref=recv_refs[a].at[j], send_sem=send_sems.at[4 * a + j],
            recv_sem=recv_sems.at[4 * a + j], device_id=(x, y, 1 - c), device_id_type=MESH)
            for a in range(n) for j in range(4)]
        for cp in copies:
            cp.start()
        for cp in copies:
            cp.wait_recv()
        for cp in copies:
            cp.wait_send()

    return pl.pallas_call(
        body, name=_nm("rs_pair"),
        out_shape=tuple(jax.ShapeDtypeStruct((4, g.shape[0]) + g.shape[2:], g.dtype) for g in gs),
        in_specs=[ANY] * n, out_specs=(ANY,) * n,
        scratch_shapes=[pltpu.SemaphoreType.DMA((4 * n,)), pltpu.SemaphoreType.DMA((4 * n,))],
    )(*gs)


def _rs_chip_exchange(ps):
    n = len(ps)

    def body(*refs):
        p_refs, recv_refs = refs[:n], refs[n:2 * n]
        send_sems, recv_sems = refs[2 * n:]
        x, y, c = _place()
        chips = [(1 - x, y), (x, 1 - y), (1 - x, 1 - y)]
        copies = [pltpu.make_async_remote_copy(
            src_ref=p_refs[a].at[2 * cx + cy], dst_ref=recv_refs[a].at[k], send_sem=send_sems.at[3 * a + k],
            recv_sem=recv_sems.at[3 * a + k], device_id=(cx, cy, c), device_id_type=MESH)
            for a in range(n) for k, (cx, cy) in enumerate(chips)]
        for cp in copies:
            cp.start()
        for cp in copies:
            cp.wait_recv()
        for cp in copies:
            cp.wait_send()

    return pl.pallas_call(
        body, name=_nm("rs_chip"),
        out_shape=tuple(jax.ShapeDtypeStruct((3,) + p.shape[1:], p.dtype) for p in ps),
        in_specs=[ANY] * n, out_specs=(ANY,) * n,
        scratch_shapes=[pltpu.SemaphoreType.DMA((3 * n,)), pltpu.SemaphoreType.DMA((3 * n,))],
    )(*ps)


def _rs_pair_add(g, recv, c_idx):
    L, _, a, b = g.shape
    ta = _row_divisor(a)

    def body(c_ref, g_ref, r_ref, o_ref):
        o_ref[...] = (g_ref[...].astype(F32) + r_ref[...].astype(F32)).astype(o_ref.dtype)

    grid_spec = pltpu.PrefetchScalarGridSpec(
        num_scalar_prefetch=1, grid=(4, L, a // ta),
        in_specs=[pl.BlockSpec((None, None, ta, b), lambda j, l, i, c_ref: (l, 2 * j + c_ref[0], i, 0)),
                  pl.BlockSpec((None, None, ta, b), lambda j, l, i, c_ref: (j, l, i, 0))],
        out_specs=pl.BlockSpec((None, None, ta, b), lambda j, l, i, c_ref: (j, l, i, 0)))
    return pl.pallas_call(
        body, name=_nm("rs_pair_add"), grid_spec=grid_spec, out_shape=jax.ShapeDtypeStruct((4, L, a, b), g.dtype),
        compiler_params=_cp(("parallel", "parallel", "parallel")),
    )(c_idx, g, recv)


def _rs_final_add(p1, recv, chip_idx):
    _, L, a, b = p1.shape
    ta = _row_divisor(a)

    def body(chip_ref, p_ref, r_ref, o_ref):
        acc = p_ref[...].astype(F32)
        for k in range(3):
            acc = acc + r_ref[k].astype(F32)
        o_ref[...] = acc

    grid_spec = pltpu.PrefetchScalarGridSpec(
        num_scalar_prefetch=1, grid=(L, a // ta),
        in_specs=[pl.BlockSpec((None, None, ta, b), lambda l, i, chip_ref: (chip_ref[0], l, i, 0)),
                  pl.BlockSpec((3, None, ta, b), lambda l, i, chip_ref: (0, l, i, 0))],
        out_specs=pl.BlockSpec((None, ta, b), lambda l, i, chip_ref: (l, i, 0)))
    return pl.pallas_call(
        body, name=_nm("rs_final_add"), grid_spec=grid_spec, out_shape=jax.ShapeDtypeStruct((L, a, b), F32),
        compiler_params=_cp(("parallel", "parallel")),
    )(chip_idx, p1, recv)


def reduce_scatter(gs):
    x, y, c = _place()
    c_idx = jnp.reshape(c, (1,)).astype(jnp.int32)
    chip_idx = jnp.reshape(2 * x + y, (1,)).astype(jnp.int32)
    recv1 = _rs_pair_exchange(gs)
    p1 = [_rs_pair_add(g, r, c_idx) for g, r in zip(gs, recv1)]
    recv2 = _rs_chip_exchange(p1)
    return [_rs_final_add(p, r, chip_idx) for p, r in zip(p1, recv2)]


def all_reduce_small(v):
    r, C = v.shape

    def body(v_ref, out_ref, buf_ref, send_sems, recv_sems):
        x, y, c = _place()
        my_id = 4 * x + 2 * y + c
        buf_ref[my_id] = v_ref[...]
        copies = []
        for k in range(1, N_DEV):
            fx, fy, fc = (k >> 2) & 1, (k >> 1) & 1, k & 1
            peer = (x ^ fx, y ^ fy, c ^ fc)
            copies.append(pltpu.make_async_remote_copy(
                src_ref=v_ref, dst_ref=buf_ref.at[my_id], send_sem=send_sems.at[k - 1], recv_sem=recv_sems.at[k - 1],
                device_id=peer, device_id_type=MESH))
        for cp in copies:
            cp.start()
        for cp in copies:
            cp.wait_recv()
        for cp in copies:
            cp.wait_send()
        acc = buf_ref[0]
        for d in range(1, N_DEV):
            acc = acc + buf_ref[d]
        out_ref[...] = acc

    vm = pl.BlockSpec(memory_space=pltpu.VMEM)
    return pl.pallas_call(
        body, name=_nm("all_reduce_small"), out_shape=jax.ShapeDtypeStruct((r, C), F32),
        in_specs=[vm], out_specs=vm,
        scratch_shapes=[pltpu.VMEM((N_DEV, r, C), F32), pltpu.SemaphoreType.DMA((7,)), pltpu.SemaphoreType.DMA((7,))],
    )(v)


def _slot(ref, px, py, pc):
    return ref.at[:, 4 * px + 2 * py + pc]


def comm_gather_own(shards):
    n = len(shards)

    def build(cin, cout, send_sems, recv_sems, local_sems):
        x, y, c = _place()
        me = (x, y, c)
        direct = [(x, y, 1 - c), (1 - x, y, c), (x, 1 - y, c)]
        src_nb, dst_nb, diag = (x ^ (1 - c), y ^ c, c), (x ^ c, y ^ (1 - c), c), (1 - x, 1 - y, c)
        starts, mid_waits, mid_starts, waits = [], [], [], []
        for a in range(n):
            local = pltpu.make_async_copy(cin[a], _slot(cout[a], *me), local_sems.at[a])
            starts.append(local)
            waits.append(local.wait)
            for k, peer in enumerate(direct):
                send = pltpu.make_async_remote_copy(
                    src_ref=cin[a], dst_ref=_slot(cout[a], *me), send_sem=send_sems.at[4 * a + k],
                    recv_sem=recv_sems.at[4 * a + k], device_id=peer, device_id_type=MESH)
                arrive = pltpu.make_async_remote_copy(
                    src_ref=cin[a], dst_ref=_slot(cout[a], *peer), send_sem=send_sems.at[4 * a + k],
                    recv_sem=recv_sems.at[4 * a + k], device_id=peer, device_id_type=MESH)
                starts.append(send)
                (waits if k == 0 else mid_waits).append(arrive.wait_recv)
                waits.append(send.wait_send)
            onward = pltpu.make_async_remote_copy(
                src_ref=_slot(cout[a], *src_nb), dst_ref=_slot(cout[a], *src_nb), send_sem=send_sems.at[4 * a + 3],
                recv_sem=recv_sems.at[4 * a + 3], device_id=dst_nb, device_id_type=MESH)
            arrive = pltpu.make_async_remote_copy(
                src_ref=_slot(cout[a], *src_nb), dst_ref=_slot(cout[a], *diag), send_sem=send_sems.at[4 * a + 3],
                recv_sem=recv_sems.at[4 * a + 3], device_id=dst_nb, device_id_type=MESH)
            mid_starts.append(onward)
            waits += [arrive.wait_recv, onward.wait_send]
        return starts, mid_waits, mid_starts, waits

    out_shapes = [jax.ShapeDtypeStruct((s.shape[0], N_DEV) + s.shape[1:], s.dtype) for s in shards]
    return Comm(shards, out_shapes, {}, 4 * n, 4 * n, n, build)


def comm_gather_pass(partial):
    n = len(partial)

    def build(cin, cout, send_sems, recv_sems, local_sems):
        x, y, c = _place()
        chips = [(1 - x, y), (x, 1 - y), (1 - x, 1 - y)]
        starts, waits = [], []
        for a in range(n):
            for j, chip in enumerate(chips):
                send = pltpu.make_async_remote_copy(
                    src_ref=_slot(cout[a], *chip, c), dst_ref=_slot(cout[a], *chip, c), send_sem=send_sems.at[3 * a + j],
                    recv_sem=recv_sems.at[3 * a + j], device_id=(x, y, 1 - c), device_id_type=MESH)
                arrive = pltpu.make_async_remote_copy(
                    src_ref=_slot(cout[a], *chip, c), dst_ref=_slot(cout[a], *chip, 1 - c),
                    send_sem=send_sems.at[3 * a + j], recv_sem=recv_sems.at[3 * a + j],
                    device_id=(x, y, 1 - c), device_id_type=MESH)
                starts.append(send)
                waits += [arrive.wait_recv, send.wait_send]
        return starts, waits

    out_shapes = [jax.ShapeDtypeStruct(p.shape, p.dtype) for p in partial]
    return Comm(partial, out_shapes, {a: a for a in range(n)}, 3 * n, 3 * n, 1, build)


def comm_rs_pair(gs):
    n = len(gs)

    def build(cin, cout, send_sems, recv_sems, local_sems):
        x, y, c = _place()
        starts, waits = [], []
        for a in range(n):
            for j in range(4):
                cp = pltpu.make_async_remote_copy(
                    src_ref=cin[a].at[:, 2 * j + (1 - c)], dst_ref=cout[a].at[j], send_sem=send_sems.at[4 * a + j],
                    recv_sem=recv_sems.at[4 * a + j], device_id=(x, y, 1 - c), device_id_type=MESH)
                starts.append(cp)
                waits += [cp.wait_recv, cp.wait_send]
        return starts, waits

    out_shapes = [jax.ShapeDtypeStruct((4, g.shape[0]) + g.shape[2:], g.dtype) for g in gs]
    return Comm(gs, out_shapes, {}, 4 * n, 4 * n, 1, build)


def comm_rs_chip(ps):
    n = len(ps)

    def build(cin, cout, send_sems, recv_sems, local_sems):
        x, y, c = _place()
        chips = [(1 - x, y), (x, 1 - y), (1 - x, 1 - y)]
        starts, waits = [], []
        for a in range(n):
            for k, (cx, cy) in enumerate(chips):
                cp = pltpu.make_async_remote_copy(
                    src_ref=cin[a].at[2 * cx + cy], dst_ref=cout[a].at[k], send_sem=send_sems.at[3 * a + k],
                    recv_sem=recv_sems.at[3 * a + k], device_id=(cx, cy, c), device_id_type=MESH)
                starts.append(cp)
                waits += [cp.wait_recv, cp.wait_send]
        return starts, waits

    out_shapes = [jax.ShapeDtypeStruct((3,) + p.shape[1:], p.dtype) for p in ps]
    return Comm(ps, out_shapes, {}, 3 * n, 3 * n, 1, build)


def _pack(arrays, dtype, row_align):
    lead = arrays[0].shape[:-1]
    quantum = row_align * PACK_COLS
    parts, sizes = [], []
    for a in arrays:
        n = a.shape[-1]
        padded = _round_up(n, quantum)
        a = a.astype(dtype)
        if padded != n:
            a = jnp.pad(a, [(0, 0)] * len(lead) + [(0, padded - n)])
        parts.append(a.reshape(*lead, padded // PACK_COLS, PACK_COLS))
        sizes.append((n, padded // PACK_COLS))
    return jnp.concatenate(parts, axis=len(lead)), sizes


def _unpack(packed, sizes):
    lead = packed.shape[:-2]
    out, row = [], 0
    for n, rows in sizes:
        part = lax.slice_in_dim(packed, row, row + rows, axis=len(lead))
        out.append(part.reshape(*lead, rows * PACK_COLS)[..., :n])
        row += rows
    return out


def _unshard(gathered, axis):
    _, L, a, b = gathered.shape
    if axis == 1:
        return [gathered[:, l].reshape(N_DEV * a, b) for l in range(L)]
    return [jnp.transpose(gathered[:, l], (1, 0, 2)).reshape(a, N_DEV * b) for l in range(L)]


def _reshard(fulls, axis):
    blocks = []
    for f in fulls:
        A, B = f.shape
        if axis == 1:
            blocks.append(f.reshape(N_DEV, A // N_DEV, B))
        else:
            blocks.append(jnp.transpose(f.reshape(A, N_DEV, B // N_DEV), (1, 0, 2)))
    return jnp.stack(blocks, axis=1)


def _as3(a):
    return a if a.ndim == 3 else a[:, None, :]


def _prep_big(name, w, dims):
    w = w.astype(BF16)
    L, a, b = w.shape
    if name == 'mla_w_in':
        return jnp.pad(w, ((0, 0), (0, 0), (0, dims['h_width'] - b)))
    if name == 'mla_w_uq':
        hd = MLA_NOPE + MLA_ROPE
        w = jnp.pad(w.reshape(L, a, b // hd, hd), ((0, 0), (0, 0), (0, 0), (0, MLA_QK_PAD - hd)))
        return w.reshape(L, a, b // hd * MLA_QK_PAD)
    if name == 'ffn_w_up':
        return jnp.pad(w, ((0, 0), (0, 0), (0, _round_up(b, CONV_COLS) - b)))
    return w


def _unprep_big(name, g, shape):
    L, a, b = shape
    if name == 'mla_w_uq':
        hd = MLA_NOPE + MLA_ROPE
        return g.reshape(L, a, b // hd, MLA_QK_PAD)[..., :hd].reshape(L, a, b)
    return g[:, :, :b]


def _rope_tables(positions):
    inv = 1.0 / (ROPE_THETA ** (jnp.arange(0, MLA_ROPE, 2, dtype=F32) / MLA_ROPE))
    ang = positions.astype(F32)[:, None] * inv
    cos, sin = jnp.cos(ang), jnp.sin(ang)
    one, zero = jnp.ones_like(cos), jnp.zeros_like(cos)
    a = jnp.concatenate([cos, cos, one, one], axis=1)
    up = jnp.concatenate([sin, zero, zero, zero], axis=1)
    down = jnp.concatenate([zero, sin, zero, zero], axis=1)
    return (a, -up, down), (a, up, -down)


def _rows_full(w):
    return w.reshape(w.shape[0] * w.shape[1], w.shape[2])


def _ops(depth):
    alpha = (2 * depth) ** 0.25
    return {'ln_res': rw_op(_ln_res_fn(alpha), "ln_res", 2, [F32], shadow=True),
            'ple': rw_op(_ple_fn, "ple", 3, [F32], shadow=True), 'gla_gate': rw_op(_gla_gate_fn, "gla_gate", 1, [F32])}


MLA_PRE_W = ['mla_w_in', 'mla_q_norm', 'mla_kv_norm', 'mla_w_uq', 'mla_w_uk', 'mla_w_uv']
FFN_IN_W = ['mla_w_o', 'ln1_g', 'ln1_b', 'ffn_conv_w', 'ffn_conv_b']
FFN_OUT_W = ['ffn_w_down', 'ln2_g', 'ln2_b', 'ple_w_gate', 'ple_w_proj', 'ple_b_gate']


def _mla_heads(wl, j):
    return N_DEV * wl['mla_w_uk'][j].shape[2] // MLA_NOPE


def _mla_pre(x, xb, wl, j):
    w_uq, w_uk, w_uv = wl['mla_w_uq'][j], wl['mla_w_uk'][j], wl['mla_w_uv'][j]
    h = linear_sh(x, xb, _rows_full(wl['mla_w_in'][j]), F32)
    mla_norm = rw_op(_mla_norm_fn(w_uq.shape[1], w_uk.shape[1]), "mla_norm", 1, [BF16, BF16, F32])
    cq, ckv, kr_raw = mla_norm(h, wl['mla_q_norm'][j], wl['mla_kv_norm'][j])
    return linear(cq, w_uq, BF16), linear(ckv, w_uk, BF16), linear(ckv, w_uv, BF16), kr_raw


def _gla_mixer(x, xb, wl, j, ops):
    w_in3, w_a2 = wl['gla_w_in'][j], wl['gla_w_a2'][j]
    w_o = _rows_full(wl['gla_w_o'][j])
    w_in = jnp.transpose(w_in3, (1, 0, 2)).reshape(w_in3.shape[1], N_DEV * w_in3.shape[2])
    heads = w_o.shape[0] // GLA_DV
    n_main = 2 * heads * GLA_DK + 2 * heads * GLA_DV
    w_a = jnp.pad(w_in[:, n_main:], ((0, 0), (0, LANES - GLA_RANK)))
    w_a2_p = jnp.pad(w_a2, ((0, LANES - GLA_RANK), (0, 0))).astype(BF16)
    hm = linear_sh(x, xb, w_in[:, :n_main], BF16)
    ha = linear_sh(x, xb, w_a, BF16)
    (la,) = ops['gla_gate'](linear(ha, w_a2_p, F32), wl['gla_b_a'][j])
    return linear(gla_core(hm, la, wl['gla_o_norm'][j], heads), w_o, F32)


def _ffn_in(x, m, wl, i, ops, bp):
    x1, x1b = ops['ln_res'](x, m, wl['ln1_g'][i], wl['ln1_b'][i])
    cw, cb = wl['ffn_conv_w'][i], wl['ffn_conv_b'][i]
    bu = cw.shape[1] // N_DEV
    cwp = jnp.pad(cw.reshape(CONV_W, N_DEV, bu), ((0, 0), (0, 0), (0, bp - bu))).reshape(CONV_W, N_DEV * bp)
    cbp = jnp.pad(cb.reshape(1, N_DEV, bu), ((0, 0), (0, 0), (0, bp - bu))).reshape(1, N_DEV * bp)
    return x1, lax.stop_gradient(x1b), cwp, cbp


def _ffn_out(x1, a, wl, p_i, i, ops):
    w_down3 = wl['ffn_w_down'][i]
    half, bu, d_model = N_DEV // 2, 2 * w_down3.shape[1], w_down3.shape[2]
    bp = a.shape[1] // half
    w_down = jnp.pad(w_down3.reshape(half, bu, d_model), ((0, 0), (0, bp - bu), (0, 0))).reshape(half * bp, d_model)
    f = linear(a, w_down, F32)
    x2, x2b = ops['ln_res'](x1, f, wl['ln2_g'][i], wl['ln2_b'][i])
    glog = linear_sh(x2, lax.stop_gradient(x2b), _rows_full(wl['ple_w_gate'][i]), F32)
    pp = linear(p_i, wl['ple_w_proj'][i], F32)
    x, xb = ops['ple'](x2, glog, pp, wl['ple_b_gate'][i])
    return x, lax.stop_gradient(xb)


def _layer(x, xb, wl, p_i, aux, i, ops):
    j = i // 2
    if i % 2 == 0:
        q_raw, kn, v, kr_raw = _mla_pre(x, xb, wl, j)
        o = make_attention(aux, _mla_heads(wl, j))[0](q_raw, kn, v, kr_raw)
        m = linear(o, _rows_full(wl['mla_w_o'][j]), F32)
    else:
        m = _gla_mixer(x, xb, wl, j, ops)
    x1, x1b, cwp, cbp = _ffn_in(x, m, wl, i, ops, wl['ffn_w_up'][i].shape[2])
    return _ffn_out(x1, ffn_hidden(x1, x1b, wl['ffn_w_up'][i], cwp, cbp), wl, p_i, i, ops)


def kernel(x, p, positions, mla_w_in, mla_q_norm, mla_kv_norm, mla_w_uq, mla_w_uk, mla_w_uv, mla_w_o, gla_w_in, gla_w_a2, gla_b_a, gla_o_norm, gla_w_o, ln1_g, ln1_b, ln2_g, ln2_b, ffn_w_up, ffn_conv_w, ffn_conv_b, ffn_w_down, ple_w_proj, ple_w_gate, ple_b_gate, loss_target, m_mla_w_in, m_mla_q_norm, m_mla_kv_norm, m_mla_w_uq, m_mla_w_uk, m_mla_w_uv, m_mla_w_o, m_gla_w_in, m_gla_w_a2, m_gla_b_a, m_gla_o_norm, m_gla_w_o, m_ln1_g, m_ln1_b, m_ln2_g, m_ln2_b, m_ffn_w_up, m_ffn_conv_w, m_ffn_conv_b, m_ffn_w_down, m_ple_w_proj, m_ple_w_gate, m_ple_b_gate, v_mla_w_in, v_mla_q_norm, v_mla_kv_norm, v_mla_w_uq, v_mla_w_uk, v_mla_w_uv, v_mla_w_o, v_gla_w_in, v_gla_w_a2, v_gla_b_a, v_gla_o_norm, v_gla_w_o, v_ln1_g, v_ln1_b, v_ln2_g, v_ln2_b, v_ffn_w_up, v_ffn_conv_w, v_ffn_conv_b, v_ffn_w_down, v_ple_w_proj, v_ple_w_gate, v_ple_b_gate):
    w = dict(zip(WEIGHTS, (mla_w_in, mla_q_norm, mla_kv_norm, mla_w_uq, mla_w_uk, mla_w_uv, mla_w_o, gla_w_in, gla_w_a2,
                           gla_b_a, gla_o_norm, gla_w_o, ln1_g, ln1_b, ln2_g, ln2_b, ffn_w_up, ffn_conv_w, ffn_conv_b,
                           ffn_w_down, ple_w_proj, ple_w_gate, ple_b_gate)))
    m_in = dict(zip(WEIGHTS, (m_mla_w_in, m_mla_q_norm, m_mla_kv_norm, m_mla_w_uq, m_mla_w_uk, m_mla_w_uv, m_mla_w_o,
                              m_gla_w_in, m_gla_w_a2, m_gla_b_a, m_gla_o_norm, m_gla_w_o, m_ln1_g, m_ln1_b, m_ln2_g,
                              m_ln2_b, m_ffn_w_up, m_ffn_conv_w, m_ffn_conv_b, m_ffn_w_down, m_ple_w_proj, m_ple_w_gate,
                              m_ple_b_gate)))
    v_in = dict(zip(WEIGHTS, (v_mla_w_in, v_mla_q_norm, v_mla_kv_norm, v_mla_w_uq, v_mla_w_uk, v_mla_w_uv, v_mla_w_o,
                              v_gla_w_in, v_gla_w_a2, v_gla_b_a, v_gla_o_norm, v_gla_w_o, v_ln1_g, v_ln1_b, v_ln2_g,
                              v_ln2_b, v_ffn_w_up, v_ffn_conv_w, v_ffn_conv_b, v_ffn_w_down, v_ple_w_proj, v_ple_w_gate,
                              v_ple_b_gate)))
    _uid[0] = itertools.count()
    x2d, target, pos = x[0], loss_target[0], positions[0]
    p3 = p[:, 0]
    dims = {'h_width': mla_w_uq.shape[1] + mla_w_uk.shape[1] + LANES}

    depth = ln1_g.shape[0]
    ops = _ops(depth)
    cid = pos // CHUNK
    aux = {'rope': _rope_tables(pos), 'cidq': cid[:, None], 'cidk': cid[None, :],
           'need': _mask_table(cid, _attn_tile(pos.shape[0]))}

    in_layer0 = set(MLA_PRE_W + FFN_IN_W + FFN_OUT_W + ['ffn_w_up'])
    prepped = {n: _prep_big(n, w[n], dims) for n in BIG}
    first_names = [n for n in BIG if n in MLA_PRE_W + ['mla_w_o']]
    mid_names = [n for n in BIG if n in in_layer0 and n not in first_names]
    rest_names = [n for n in BIG if prepped[n].shape[0] > (1 if n in in_layer0 else 0)]
    rest_from = {n: (1 if n in in_layer0 else 0) for n in rest_names}
    small3 = [_as3(w[n]) for n in SMALL]
    small_packed, small_sizes = _pack([s.reshape(1, -1) for s in small3], F32, 8)
    first = all_gather([prepped[n][:1] for n in first_names] + [small_packed])
    wl = {n: [None] * prepped[n].shape[0] for n in BIG}
    for n, g in zip(first_names, first):
        wl[n][0] = g[0]
    for n, s3, flat in zip(SMALL, small3, _unpack(first[-1][0], small_sizes)):
        wl[n] = _unshard(flat.reshape(N_DEV, *s3.shape), SHARD_AXIS[n] if w[n].ndim == 3 else 2)
    for n in REPL:
        wl[n] = [w[n][l][None, :] for l in range(w[n].shape[0])]

    def pick(names, layer0):
        return {n: [wl[n][l] if (l == 0 and n in in_layer0) == layer0 else None for l in range(len(wl[n]))] for n in names}

    heads0 = _mla_heads(wl, 0)
    _, attn_run, attn_bwd = make_attention(aux, heads0)
    x2db = x2d.astype(BF16)
    pre, vjp_pre = jax.vjp(lambda x_, wl_: _mla_pre(x_, x2db, wl_, 0), x2d, pick(MLA_PRE_W, True))
    o, attn_res, partial = attn_run(*pre, comm_gather_own([prepped[n][:1] for n in mid_names]
                                                          + [prepped[n][rest_from[n]:] for n in rest_names]))
    for n, g in zip(mid_names, _run_comm(comm_gather_pass(partial[:len(mid_names)]), "gather_pass")):
        wl[n][0] = g[0]
    partial = partial[len(mid_names):]
    w_up0 = wl['ffn_w_up'][0]

    def ffn_in(x_, o_, wl_):
        x1_, x1b_, cwp_, cbp_ = _ffn_in(x_, linear(o_, _rows_full(wl_['mla_w_o'][0]), F32), wl_, 0, ops, w_up0.shape[2])
        return (x1_, cwp_, cbp_), x1b_

    (x1, cwp, cbp), vjp_in, x1b = jax.vjp(ffn_in, x2d, o, pick(FFN_IN_W, True), has_aux=True)
    a, ffn_res, rest = _ffn_hidden_fwd(x1, x1b, w_up0, cwp, cbp, comm_gather_pass(partial))
    for n, g in zip(rest_names, rest):
        for l in range(g.shape[0]):
            wl[n][rest_from[n] + l] = g[l]
    x_l0, vjp_out, x_l0b = jax.vjp(lambda x1_, a_, wl_: _ffn_out(x1_, a_, wl_, p3[0], 0, ops), x1, a,
                                   pick(FFN_OUT_W, True), has_aux=True)

    def tail(x_, wl_):
        xb_ = x_l0b
        for i in range(1, depth):
            x_, xb_ = _layer(x_, xb_, wl_, p3[i], aux, i, ops)
        return x_

    y, vjp_tail = jax.vjp(tail, x_l0, pick(WEIGHTS, False))
    sq, dy = _loss_call(y, target)

    dwl = {n: [None] * len(wl[n]) for n in WEIGHTS}

    def keep(part):
        for n, per_layer in part.items():
            for l, g in enumerate(per_layer):
                if g is not None:
                    dwl[n][l] = g

    x_c, y_c, c_place = _place()
    c_idx = jnp.reshape(c_place, (1,)).astype(jnp.int32)
    chip_idx = jnp.reshape(2 * x_c + y_c, (1,)).astype(jnp.int32)
    dx_l0, d_tail = vjp_tail(dy)
    keep(d_tail)
    g_rest = [jnp.stack(dwl[n][rest_from[n]:], axis=0) for n in rest_names]
    dx1_out, da, d_out = vjp_out(dx_l0)
    keep(d_out)
    (dx1_ffn, _, dw_up0, dcwp, dcbp), recv1 = _ffn_hidden_bwd(ffn_res, da, comm_rs_pair(g_rest))
    dwl['ffn_w_up'][0] = dw_up0
    g_mid = [dwl[n][0][None] for n in mid_names]
    g_rest, recv1 = g_rest + g_mid, list(recv1) + list(_rs_pair_exchange(g_mid))
    p1 = [_rs_pair_add(g, r, c_idx) for g, r in zip(g_rest, recv1)]
    dx_in, do, d_in = vjp_in((dx1_out + dx1_ffn, dcwp, dcbp))
    keep(d_in)
    d_pre_in, recv2 = attn_bwd(attn_res, do, comm_rs_chip(p1))
    red_rest = [_rs_final_add(p_, r, chip_idx) for p_, r in zip(p1, recv2)]
    dx_pre, d_pre = vjp_pre(d_pre_in)
    keep(d_pre)
    dx = dx_pre + dx_in

    small_blocks = [_reshard(dwl[n], SHARD_AXIS[n] if w[n].ndim == 3 else 2).reshape(N_DEV, -1) for n in SMALL]
    small_grad_packed, _ = _pack(small_blocks, F32, 8)
    red_first = reduce_scatter([dwl[n][0][None] for n in first_names] + [small_grad_packed[None]])
    by_layer = {n: [] for n in BIG}
    for n, g in zip(first_names, red_first):
        by_layer[n].append(g)
    for n, g in zip(mid_names, red_rest[len(rest_names):]):
        by_layer[n].append(g)
    for n, g in zip(rest_names, red_rest):
        by_layer[n].append(g)
    grads = {n: _unprep_big(n, jnp.concatenate(by_layer[n], axis=0), w[n].shape) for n in BIG}
    for n, f in zip(SMALL, _unpack(red_first[-1][0], small_sizes)):
        grads[n] = f.reshape(w[n].shape)

    repl_flat = [jnp.concatenate([g.reshape(-1) for g in dwl[n]]).reshape(1, -1) for n in REPL]
    loss_part = 0.5 * jnp.sum(sq) / sq.shape[1]
    packed, repl_sizes = _pack(repl_flat + [loss_part.reshape(1, 1)], F32, 8)
    summed = _unpack(all_reduce_small(packed[0])[None], repl_sizes)
    for n, f in zip(REPL, summed[:-1]):
        grads[n] = f.reshape(w[n].shape)
    loss = summed[-1].reshape(())

    delta, new_m, new_v = {}, {}, {}
    for n in WEIGHTS:
        delta[n], new_m[n], new_v[n] = _adamw_call(w[n], grads[n], m_in[n], v_in[n])
    return (loss, dx[None], *[grads[n] for n in WEIGHTS], *[delta[n] for n in WEIGHTS],
            *[new_m[n] for n in WEIGHTS], *[new_v[n] for n in WEIGHTS])
```

```python
import functools
import itertools

import jax
import jax.numpy as jnp
from jax import lax
from jax.experimental import pallas as pl
from jax.experimental.pallas import tpu as pltpu

F32 = jnp.float32
BF16 = jnp.bfloat16
MESH = pl.DeviceIdType.MESH
N_DEV = 8

EPS = 1e-5
NEG_INF = -1e30
CHUNK = 64
Q_BLOCK = 128
MLA_NOPE = 128
MLA_ROPE = 64
MLA_V = 128
MLA_QK_PAD = 256
ROPE_THETA = 10000.0
GLA_DK = 128
GLA_DV = 256
GLA_RANK = 16
GLA_TAU = 16.0
CONV_W = 3
ADAM_LR = 0.001
ADAM_B1 = 0.9
ADAM_B2 = 0.999
ADAM_EPS = 1e-08
ADAM_WD = 0.01
ADAM_STEP = 10
LOG2E = 1.4426950408889634

LANES = 128
PACK_COLS = 1024
VMEM_LIMIT = 56 * 1024 * 1024
MM_VMEM_BUDGET = 38 * 1024 * 1024

WEIGHTS = ['mla_w_in', 'mla_q_norm', 'mla_kv_norm', 'mla_w_uq', 'mla_w_uk', 'mla_w_uv', 'mla_w_o', 'gla_w_in',
           'gla_w_a2', 'gla_b_a', 'gla_o_norm', 'gla_w_o', 'ln1_g', 'ln1_b', 'ln2_g', 'ln2_b', 'ffn_w_up',
           'ffn_conv_w', 'ffn_conv_b', 'ffn_w_down', 'ple_w_proj', 'ple_w_gate', 'ple_b_gate']
SHARD_AXIS = {'mla_w_in': 1, 'mla_q_norm': None, 'mla_kv_norm': None, 'mla_w_uq': 2, 'mla_w_uk': 2, 'mla_w_uv': 2,
              'mla_w_o': 1, 'gla_w_in': 2, 'gla_w_a2': 2, 'gla_b_a': 1, 'gla_o_norm': 1, 'gla_w_o': 1,
              'ln1_g': None, 'ln1_b': None, 'ln2_g': None, 'ln2_b': None, 'ffn_w_up': 2, 'ffn_conv_w': 2,
              'ffn_conv_b': None, 'ffn_w_down': 1, 'ple_w_proj': 2, 'ple_w_gate': 1, 'ple_b_gate': None}
BIG = ['mla_w_in', 'mla_w_uq', 'mla_w_uk', 'mla_w_uv', 'mla_w_o', 'gla_w_in', 'gla_w_o', 'ffn_w_up', 'ffn_w_down',
       'ple_w_proj', 'ple_w_gate']
SMALL = ['gla_w_a2', 'gla_b_a', 'gla_o_norm', 'ffn_conv_w']
REPL = [n for n in WEIGHTS if SHARD_AXIS[n] is None]

_uid = [itertools.count()]


def _nm(base):
    return f"{base}_{next(_uid[0])}"


def _cp(sem=None):
    return pltpu.CompilerParams(dimension_semantics=sem, vmem_limit_bytes=VMEM_LIMIT)


def _round_up(n, m):
    return -(-n // m) * m


class Comm:
    MID_AT = 0.6

    def __init__(self, inputs, out_shapes, aliases, n_send, n_recv, n_local, build):
        self.inputs, self.out_shapes, self.aliases = list(inputs), list(out_shapes), dict(aliases)
        self.n_send, self.n_recv, self.n_local, self.build = n_send, n_recv, n_local, build


def _pcall(body, name, grid, n_prefetch, in_specs, out_specs, out_shape, scratch, sem, args, comm=None):
    in_specs, out_specs, out_shape, scratch, args = list(in_specs), list(out_specs), list(out_shape), list(scratch), list(args)
    n_in, n_out, n_scr = len(in_specs), len(out_specs), len(scratch)
    aliases = {}
    kernel_body = body
    if comm is not None:
        ci, co = len(comm.inputs), len(comm.out_shapes)
        any_spec = pl.BlockSpec(memory_space=pl.ANY)
        in_specs += [any_spec] * ci
        out_specs += [any_spec] * co
        out_shape += comm.out_shapes
        scratch += [pltpu.SemaphoreType.DMA((comm.n_send,)), pltpu.SemaphoreType.DMA((comm.n_recv,)),
                    pltpu.SemaphoreType.DMA((comm.n_local,))]
        aliases = {n_prefetch + n_in + k: n_out + v for k, v in comm.aliases.items()}
        args += comm.inputs
        sem = ("arbitrary",) * len(grid)

        def kernel_body(*refs):
            pre, r = refs[:n_prefetch], refs[n_prefetch:]
            ins, cin = r[:n_in], r[n_in:n_in + ci]
            outs, cout = r[n_in + ci:n_in + ci + n_out], r[n_in + ci + n_out:n_in + ci + n_out + co]
            scr = r[n_in + ci + n_out + co:n_in + ci + n_out + co + n_scr]
            send_sems, recv_sems, local_sems = r[-3:]
            first = functools.reduce(lambda a, b: a & b, [pl.program_id(d) == 0 for d in range(len(grid))])
            last = functools.reduce(lambda a, b: a & b, [pl.program_id(d) == grid[d] - 1 for d in range(len(grid))])
            built = comm.build(cin, cout, send_sems, recv_sems, local_sems)
            starts, waits = built[0], built[-1]
            mid_waits, mid_starts = (built[1], built[2]) if len(built) == 4 else ([], [])

            @pl.when(first)
            def _():
                for cp in starts:
                    cp.start()

            if mid_starts:
                step = pl.program_id(0)
                for d in range(1, len(grid)):
                    step = step * grid[d] + pl.program_id(d)
                n_steps = functools.reduce(lambda a, b: a * b, grid)

                @pl.when(step == int(n_steps * comm.MID_AT))
                def _():
                    for wait in mid_waits:
                        wait()
                    for cp in mid_starts:
                        cp.start()

            body(*pre, *ins, *outs, *scr)

            @pl.when(last)
            def _():
                for wait in waits:
                    wait()

    grid_spec = pltpu.PrefetchScalarGridSpec(num_scalar_prefetch=n_prefetch, grid=grid, in_specs=in_specs,
                                             out_specs=out_specs, scratch_shapes=scratch)
    return pl.pallas_call(kernel_body, name=_nm(name), grid_spec=grid_spec, out_shape=tuple(out_shape),
                          input_output_aliases=aliases, compiler_params=_cp(sem))(*args)


def _run_comm(comm, name):
    return _pcall(lambda: None, name, (1,), 0, [], [], [], [], ("arbitrary",), [], comm)


def _divisor_tiles(n, cap):
    if n % LANES:
        return [n]
    out = [t for t in range(LANES, min(n, cap) + 1, LANES) if n % t == 0]
    return out or [n]


def _mm_tiles(M, N, K, abytes, bbytes, obytes, tn_fixed=None, tk_fixed=None):
    best = None
    for tm in _divisor_tiles(M, 1024):
        for tn in ([tn_fixed] if tn_fixed else _divisor_tiles(N, 1536)):
            for tk in ([tk_fixed] if tk_fixed else _divisor_tiles(K, 4096)):
                vmem = 2 * (tm * tk * abytes + tk * tn * bbytes + tm * tn * obytes) + tm * tn * 4
                if vmem > MM_VMEM_BUDGET:
                    continue
                key = (tm * tn * tk, tk)
                if best is None or key > best[0]:
                    best = (key, (tm, tn, tk))
    assert best is not None, (M, N, K)
    return best[1]


def _mm(a, b, mode, out_dtype, base="mm", blocks=None):
    blk0, nblk = blocks if blocks else (0, 1)
    tn_fixed = tk_fixed = None
    if mode == "nn":
        M, K = a.shape
        N = nblk * b.shape[2] if blocks else b.shape[1]
        tn_fixed = b.shape[2] if blocks else None
    elif mode == "nt":
        M, K = a.shape
        N = b.shape[1] if blocks else b.shape[0]
        tk_fixed = b.shape[2] if blocks else None
        assert not blocks or K == nblk * b.shape[2]
    else:
        (K, M), N = a.shape, b.shape[1]
        tn_fixed = N // nblk if blocks else None
    tm, tn, tk = _mm_tiles(M, N, K, a.dtype.itemsize, b.dtype.itemsize, jnp.dtype(out_dtype).itemsize, tn_fixed, tk_fixed)
    nk = K // tk
    out_shape = jax.ShapeDtypeStruct((M, N), out_dtype)
    out_spec = pl.BlockSpec((tm, tn), lambda i, j, k: (i, j))
    if mode == "nn":
        a_spec = pl.BlockSpec((tm, tk), lambda i, j, k: (i, k))
        b_spec = (pl.BlockSpec((None, tk, tn), lambda i, j, k: (blk0 + j, k, 0)) if blocks
                  else pl.BlockSpec((tk, tn), lambda i, j, k: (k, j)))
        dims = (((1,), (0,)), ((), ()))
    elif mode == "nt":
        a_spec = pl.BlockSpec((tm, tk), lambda i, j, k: (i, k))
        b_spec = (pl.BlockSpec((None, tn, tk), lambda i, j, k: (blk0 + k, j, 0)) if blocks
                  else pl.BlockSpec((tn, tk), lambda i, j, k: (j, k)))
        dims = (((1,), (1,)), ((), ()))
    else:
        a_spec = pl.BlockSpec((tk, tm), lambda i, j, k: (k, i))
        b_spec = pl.BlockSpec((tk, tn), lambda i, j, k: (k, j))
        dims = (((0,), (0,)), ((), ()))
        if blocks:
            out_shape = jax.ShapeDtypeStruct((nblk, M, tn), out_dtype)
            out_spec = pl.BlockSpec((None, tm, tn), lambda i, j, k: (j, i, 0))

    def body(a_ref, b_ref, o_ref, acc_ref):
        part = lax.dot_general(a_ref[...].astype(BF16), b_ref[...].astype(BF16), dims, preferred_element_type=F32)
        if nk == 1:
            o_ref[...] = part.astype(o_ref.dtype)
        else:
            k = pl.program_id(2)

            @pl.when(k == 0)
            def _():
                acc_ref[...] = part

            @pl.when(k > 0)
            def _():
                acc_ref[...] += part

            @pl.when(k == nk - 1)
            def _():
                o_ref[...] = acc_ref[...].astype(o_ref.dtype)

    return pl.pallas_call(
        body, name=_nm(base), grid=(M // tm, N // tn, nk), out_shape=out_shape,
        in_specs=[a_spec, b_spec], out_specs=out_spec,
        scratch_shapes=[pltpu.VMEM((tm, tn) if nk > 1 else (8, LANES), F32)],
        compiler_params=_cp(("parallel", "parallel", "arbitrary")),
    )(a, b)


def _all_blocks(w):
    return (0, w.shape[0]) if w.ndim == 3 else None


@functools.partial(jax.custom_vjp, nondiff_argnums=(2,))
def linear(a, w, out_dtype):
    return _mm(a, w, "nn", out_dtype, "lin_fwd", _all_blocks(w))


def _linear_fwd(a, w, out_dtype):
    return _mm(a, w, "nn", out_dtype, "lin_fwd", _all_blocks(w)), (a, w)


def _linear_bwd(out_dtype, res, dy):
    a, w = res
    return (_mm(dy, w, "nt", a.dtype, "lin_dx", _all_blocks(w)), _mm(a, dy, "tn", w.dtype, "lin_dw", _all_blocks(w)))


linear.defvjp(_linear_fwd, _linear_bwd)


@functools.partial(jax.custom_vjp, nondiff_argnums=(3,))
def linear_sh(a, a_bf16, w, out_dtype):
    return _mm(a_bf16, w, "nn", out_dtype, "lin_fwd", _all_blocks(w))


def _linear_sh_fwd(a, a_bf16, w, out_dtype):
    return _mm(a_bf16, w, "nn", out_dtype, "lin_fwd", _all_blocks(w)), (a_bf16, w, jnp.zeros((), a.dtype))


def _linear_sh_bwd(out_dtype, res, dy):
    a_bf16, w, tok = res
    return (_mm(dy, w, "nt", tok.dtype, "lin_dx", _all_blocks(w)), jnp.zeros_like(a_bf16),
            _mm(a_bf16, dy, "tn", w.dtype, "lin_dw", _all_blocks(w)))


linear_sh.defvjp(_linear_sh_fwd, _linear_sh_bwd)


def _row_tile(S, width):
    tr = 512 if width <= 1024 else 256
    return min(tr, S)


def _rw_fwd(f, rows, params, out_dtypes, base):
    S = rows[0][0].shape[0]
    tr = _row_tile(S, max(w for _, w, _ in rows))
    n_in = len(rows) + len(params)
    avals = [jax.ShapeDtypeStruct((tr, w), F32) for _, w, _ in rows] + [jax.ShapeDtypeStruct(p.shape, F32) for p in params]
    outs = jax.eval_shape(f, *avals)

    def body(*refs):
        vals = [r[...].astype(F32) for r in refs[:n_in]]
        for o_ref, r in zip(refs[n_in:], f(*vals)):
            o_ref[...] = r.astype(o_ref.dtype)

    in_specs = [pl.BlockSpec((tr, w), functools.partial(lambda i, cb: (i, cb), cb=cb)) for _, w, cb in rows]
    in_specs += [pl.BlockSpec(p.shape, lambda i: (0, 0)) for p in params]
    return pl.pallas_call(
        body, name=_nm(base), grid=(S // tr,),
        out_shape=tuple(jax.ShapeDtypeStruct((S, o.shape[1]), dt) for o, dt in zip(outs, out_dtypes)),
        in_specs=in_specs, out_specs=tuple(pl.BlockSpec((tr, o.shape[1]), lambda i: (i, 0)) for o in outs),
        compiler_params=_cp(("parallel",)),
    )(*[a for a, _, _ in rows], *params)


def _rw_bwd(f, rows, params, cts, row_grad_dtypes, base):
    S = rows[0][0].shape[0]
    tr = _row_tile(S, max(w for _, w, _ in rows))
    n_rows, n_par, n_ct = len(rows), len(params), len(cts)
    want = [k for k, dt in enumerate(row_grad_dtypes) if dt is not None]

    def body(*refs):
        in_refs = refs[:n_rows + n_par]
        ct_refs = refs[n_rows + n_par:n_rows + n_par + n_ct]
        out_refs = refs[n_rows + n_par + n_ct:]
        vals = [r[...].astype(F32) for r in in_refs]
        _, vjp_fn = jax.vjp(f, *vals)
        grads = vjp_fn(tuple(c[...].astype(F32) for c in ct_refs))
        for o_ref, k in zip(out_refs[:len(want)], want):
            o_ref[...] = grads[k].astype(o_ref.dtype)
        i = pl.program_id(0)
        for o_ref, g in zip(out_refs[len(want):], grads[n_rows:]):
            @pl.when(i == 0)
            def _(o_ref=o_ref, g=g):
                o_ref[...] = g

            @pl.when(i > 0)
            def _(o_ref=o_ref, g=g):
                o_ref[...] += g

    in_specs = [pl.BlockSpec((tr, w), functools.partial(lambda i, cb: (i, cb), cb=cb)) for _, w, cb in rows]
    in_specs += [pl.BlockSpec(p.shape, lambda i: (0, 0)) for p in params]
    in_specs += [pl.BlockSpec((tr, c.shape[1]), lambda i: (i, 0)) for c in cts]
    out_shape = [jax.ShapeDtypeStruct((S, rows[k][1]), row_grad_dtypes[k]) for k in want]
    out_specs = [pl.BlockSpec((tr, rows[k][1]), lambda i: (i, 0)) for k in want]
    out_shape += [jax.ShapeDtypeStruct(p.shape, F32) for p in params]
    out_specs += [pl.BlockSpec(p.shape, lambda i: (0, 0)) for p in params]
    res = pl.pallas_call(
        body, name=_nm(base), grid=(S // tr,), out_shape=tuple(out_shape),
        in_specs=in_specs, out_specs=tuple(out_specs), compiler_params=_cp(("arbitrary",)),
    )(*[a for a, _, _ in rows], *params, *cts)
    row_grads = [None] * n_rows
    for k, g in zip(want, res[:len(want)]):
        row_grads[k] = g
    return row_grads, list(res[len(want):])


def rw_op(f, base, n_rows, out_dtypes, shadow=False):
    f_fwd = (lambda *a: (lambda r: tuple(r) + (r[0],))(f(*a))) if shadow else f
    fwd_dtypes = list(out_dtypes) + ([BF16] if shadow else [])

    @jax.custom_vjp
    def op(*args):
        return fwd(*args)[0]

    def split(args):
        rows = [(a, a.shape[1], 0) for a in args[:n_rows]]
        return rows, list(args[n_rows:])

    def fwd(*args):
        rows, params = split(args)
        return tuple(_rw_fwd(f_fwd, rows, params, fwd_dtypes, base + "_fwd")), args

    def bwd(args, cts):
        rows, params = split(args)
        cts = list(cts)[:len(out_dtypes)]
        rg, pg = _rw_bwd(f, rows, params, cts, [a.dtype for a, _, _ in rows], base + "_bwd")
        return tuple(rg) + tuple(g.astype(p.dtype) for g, p in zip(pg, params))

    op.defvjp(fwd, bwd)
    return op


def _ln_res_fn(alpha):
    def f(x, m, g, b):
        z = alpha * x + m
        mu = jnp.mean(z, -1, keepdims=True)
        zc = z - mu
        var = jnp.mean(zc * zc, -1, keepdims=True)
        return (zc * lax.rsqrt(var + EPS) * g + b,)
    return f


def _rms(x, g):
    return x * lax.rsqrt(jnp.mean(x * x, -1, keepdims=True) + EPS) * g


def _mla_norm_fn(q_lora, kv_lora):
    def f(h, qn, kvn):
        return (_rms(h[:, :q_lora], qn), _rms(h[:, q_lora:q_lora + kv_lora], kvn),
                h[:, q_lora + kv_lora:q_lora + kv_lora + LANES])
    return f


def _log_sigmoid(z):
    return jnp.minimum(z, 0.0) - jnp.log(1.0 + jnp.exp(-jnp.abs(z)))


def _gla_gate_fn(z, b):
    return (_log_sigmoid(z + b) / GLA_TAU,)


def _ple_fn(x, glog, pp, b):
    return (x + jax.nn.sigmoid(glog + b) * pp,)


def _gla_out_fn(heads):
    def f(o, r, g):
        parts = []
        for h in range(heads):
            oh = o[:, h * GLA_DV:(h + 1) * GLA_DV]
            mu = jnp.mean(oh, -1, keepdims=True)
            oc = oh - mu
            var = jnp.mean(oc * oc, -1, keepdims=True)
            parts.append(oc * lax.rsqrt(var + EPS) * g[:, h * GLA_DV:(h + 1) * GLA_DV])
        return (jnp.concatenate(parts, axis=1) * (r * jax.nn.sigmoid(r)),)
    return f


def _rope_call(x, tabs, roped, out_dtype, base, fold=False, scale=None):
    S, C = x.shape
    nb = C // LANES
    tr = min(512 if C <= 1024 else 256, S)
    out_c = LANES if fold else C

    def rot(v, a, b1, b2):
        return v * a + pltpu.roll(v, 96, 1) * b1 + pltpu.roll(v, 32, 1) * b2

    def body(x_ref, a_ref, b1_ref, b2_ref, o_ref):
        a, b1, b2 = a_ref[...], b1_ref[...], b2_ref[...]
        if fold:
            v = x_ref[:, 0:LANES].astype(F32)
            for blk in range(1, nb):
                v = v + x_ref[:, blk * LANES:(blk + 1) * LANES].astype(F32)
            o_ref[...] = rot(v, a, b1, b2).astype(o_ref.dtype)
            return
        for blk in range(nb):
            v = x_ref[:, blk * LANES:(blk + 1) * LANES].astype(F32)
            if roped(blk):
                v = rot(v, a, b1, b2)
            if scale is not None:
                v = v * scale
            o_ref[:, blk * LANES:(blk + 1) * LANES] = v.astype(o_ref.dtype)

    row = lambda w: pl.BlockSpec((tr, w), lambda i: (i, 0))
    return pl.pallas_call(
        body, name=_nm(base), grid=(S // tr,), out_shape=jax.ShapeDtypeStruct((S, out_c), out_dtype),
        in_specs=[row(C), row(LANES), row(LANES), row(LANES)], out_specs=row(out_c),
        compiler_params=_cp(("parallel",)),
    )(x, *tabs)


def _attn_tile(S):
    return min(512, S)


def _tri_schedule(n, by_key):
    pairs = ([(i, j) for j in range(n) for i in range(j, n)] if by_key
             else [(i, j) for i in range(n) for j in range(i + 1)])
    return (jnp.asarray([p[0] for p in pairs], jnp.int32), jnp.asarray([p[1] for p in pairs], jnp.int32))


def _mask_table(cid, t):
    n = cid.shape[0] // t
    blocks = cid.reshape(n, t)
    cmin_q, cmax_k = jnp.min(blocks, axis=1), jnp.max(blocks, axis=1)
    need = (cmax_k[None, :] > cmin_q[:, None]) | jnp.eye(n, dtype=bool)
    return need.astype(jnp.int32).reshape(n * n)


def _attn_mask(cidq_ref, cidk_ref, i, j, t):
    qrow = i * t + lax.broadcasted_iota(jnp.int32, (t, 1), 0)
    kcol = j * t + lax.broadcasted_iota(jnp.int32, (1, t), 1)
    qlim = (qrow // Q_BLOCK + 1) * Q_BLOCK
    return (cidk_ref[...] <= cidq_ref[...]) & (kcol < qlim)


ATTN_FWD_HEADS = 8
ATTN_BWD_HEADS = 4
ATTN_SCALE = (MLA_NOPE + MLA_ROPE) ** -0.5
ATTN_SCALE2 = ATTN_SCALE * LOG2E


def _attn_fwd_call(q, kn, v, kr, aux, heads, comm=None):
    S = q.shape[0]
    t = _attn_tile(S)
    n = S // t
    hp = ATTN_FWD_HEADS if heads % ATTN_FWD_HEADS == 0 else 1
    qi_tab, kj_tab = _tri_schedule(n, False)

    def body(qi_ref, kj_ref, need_ref, q_ref, kn_ref, v_ref, kr_ref, cidq_ref, cidk_ref, o_ref, lse_ref,
             m_ref, l_ref, acc_ref):
        st = pl.program_id(1)
        i, j = qi_ref[st], kj_ref[st]

        @pl.when(j == 0)
        def _():
            m_ref[...] = jnp.full(m_ref.shape, NEG_INF, F32)
            l_ref[...] = jnp.zeros(l_ref.shape, F32)
            acc_ref[...] = jnp.zeros(acc_ref.shape, F32)

        def update(masked):
            mask = _attn_mask(cidq_ref, cidk_ref, i, j, t) if masked else None
            for hh in range(hp):
                lanes = slice(hh * LANES, (hh + 1) * LANES)
                k = jnp.concatenate([kn_ref[:, lanes], kr_ref[...]], axis=1)
                qh = q_ref[:, hh * MLA_QK_PAD:(hh + 1) * MLA_QK_PAD]
                s = lax.dot_general(qh, k, (((1,), (1,)), ((), ())), preferred_element_type=F32)
                if masked:
                    s = jnp.where(mask, s, NEG_INF)
                m_prev = m_ref[:, lanes]
                m_new = jnp.maximum(m_prev, jnp.max(s, axis=1, keepdims=True))
                alpha = jnp.exp2(m_prev - m_new)
                p = jnp.exp2(s - m_new[:, :1])
                l_ref[:, lanes] = alpha * l_ref[:, lanes] + jnp.sum(p, axis=1, keepdims=True)
                acc_ref[:, lanes] = alpha * acc_ref[:, lanes] + jnp.dot(p.astype(BF16), v_ref[:, lanes],
                                                                        preferred_element_type=F32)
                m_ref[:, lanes] = m_new

        need = need_ref[i * n + j]

        @pl.when(need != 0)
        def _():
            update(True)

        @pl.when(need == 0)
        def _():
            update(False)

        @pl.when(j == i)
        def _():
            o_ref[...] = (acc_ref[...] / l_ref[...]).astype(o_ref.dtype)
            lse_ref[...] = m_ref[...] + jnp.log(l_ref[...]) * LOG2E

    qmap = lambda h, s, qi, kj, need: (qi[s], h)
    kmap = lambda h, s, qi, kj, need: (kj[s], h)
    return _pcall(
        body, "attn_fwd", (heads // hp, qi_tab.shape[0]), 3,
        in_specs=[pl.BlockSpec((t, hp * MLA_QK_PAD), qmap), pl.BlockSpec((t, hp * MLA_NOPE), kmap),
                  pl.BlockSpec((t, hp * MLA_V), kmap),
                  pl.BlockSpec((t, LANES), lambda h, s, qi, kj, need: (kj[s], 0)),
                  pl.BlockSpec((t, 1), lambda h, s, qi, kj, need: (qi[s], 0)),
                  pl.BlockSpec((1, t), lambda h, s, qi, kj, need: (0, kj[s]))],
        out_specs=[pl.BlockSpec((t, hp * MLA_V), qmap), pl.BlockSpec((t, hp * LANES), qmap)],
        out_shape=[jax.ShapeDtypeStruct((S, heads * MLA_V), BF16), jax.ShapeDtypeStruct((S, heads * LANES), F32)],
        scratch=[pltpu.VMEM((t, hp * LANES), F32), pltpu.VMEM((t, hp * LANES), F32), pltpu.VMEM((t, hp * MLA_V), F32)],
        sem=("parallel", "arbitrary"),
        args=[qi_tab, kj_tab, aux['need'], q, kn, v, kr, aux['cidq'], aux['cidk']], comm=comm)


def _attn_bwd_call(q, kn, v, kr, o, lse, do, aux, heads, comm=None):
    S = q.shape[0]
    t = _attn_tile(S)
    n = S // t
    qi_tab, kj_tab = _tri_schedule(n, True)
    scale = ATTN_SCALE
    nt_dims = (((1,), (1,)), ((), ()))
    tn_dims = (((0,), (0,)), ((), ()))

    hp = ATTN_BWD_HEADS if heads % ATTN_BWD_HEADS == 0 else 1

    def body(qi_ref, kj_ref, need_ref, q_ref, kn_ref, v_ref, kr_ref, cidq_ref, cidk_ref, o_ref, lse_ref, do_ref,
             ta_ref, tb1_ref, tb2_ref, dq_ref, dkn_ref, dv_ref, dkr_ref, dk_acc, dv_acc, dq_acc):
        st = pl.program_id(1)
        i, j = qi_ref[st], kj_ref[st]

        @pl.when(st == 0)
        def _():
            dq_acc[...] = jnp.zeros(dq_acc.shape, F32)

        @pl.when(i == j)
        def _():
            dk_acc[...] = jnp.zeros(dk_acc.shape, F32)
            dv_acc[...] = jnp.zeros(dv_acc.shape, F32)

        rows = pl.ds(pl.multiple_of(i * t, t), t)

        def grads(masked):
            mask = _attn_mask(cidq_ref, cidk_ref, i, j, t) if masked else None
            for hh in range(hp):
                lanes = slice(hh * LANES, (hh + 1) * LANES)
                wide = slice(hh * MLA_QK_PAD, (hh + 1) * MLA_QK_PAD)
                k = jnp.concatenate([kn_ref[:, lanes], kr_ref[...]], axis=1)
                qt, do = q_ref[:, wide], do_ref[:, lanes]
                s = lax.dot_general(qt, k, nt_dims, preferred_element_type=F32)
                if masked:
                    s = jnp.where(mask, s, NEG_INF)
                dp = lax.dot_general(do, v_ref[:, lanes], nt_dims, preferred_element_type=F32)
                dsum = jnp.sum(do.astype(F32) * o_ref[:, lanes].astype(F32), axis=1, keepdims=True)
                p = jnp.exp2(s - lse_ref[:, hh * LANES:hh * LANES + 1])
                ds = (p * (dp - dsum) * scale).astype(BF16)
                dv_acc[:, lanes] += lax.dot_general(p.astype(BF16), do, tn_dims, preferred_element_type=F32)
                dk_acc[:, wide] += lax.dot_general(ds, qt, tn_dims, preferred_element_type=F32)
                dq_acc[rows, wide] += jnp.dot(ds, k, preferred_element_type=F32)

        need = need_ref[i * n + j]

        @pl.when(need != 0)
        def _():
            grads(True)

        @pl.when(need == 0)
        def _():
            grads(False)

        @pl.when(i == n - 1)
        def _():
            for hh in range(hp):
                lanes = slice(hh * LANES, (hh + 1) * LANES)
                off = hh * MLA_QK_PAD
                dkn_ref[:, lanes] = (dk_acc[:, off:off + MLA_NOPE] * (1.0 / ATTN_SCALE2)).astype(dkn_ref.dtype)
                dkr_ref[:, lanes] = dk_acc[:, off + MLA_NOPE:off + MLA_QK_PAD] * (1.0 / ATTN_SCALE2)
            dv_ref[...] = dv_acc[...].astype(dv_ref.dtype)

        @pl.when(i == j)
        def _():
            for hh in range(hp):
                off = hh * MLA_QK_PAD
                dq_ref[rows, off:off + MLA_NOPE] = dq_acc[rows, off:off + MLA_NOPE].astype(dq_ref.dtype)
                g = dq_acc[rows, off + MLA_NOPE:off + MLA_QK_PAD]
                g = g * ta_ref[...] + pltpu.roll(g, 96, 1) * tb1_ref[...] + pltpu.roll(g, 32, 1) * tb2_ref[...]
                dq_ref[rows, off + MLA_NOPE:off + MLA_QK_PAD] = g.astype(dq_ref.dtype)

    qmap = lambda h, s, qi, kj, need: (qi[s], h)
    kmap = lambda h, s, qi, kj, need: (kj[s], h)
    whole = pl.BlockSpec((t, LANES), lambda h, s, qi, kj, need: (qi[s], 0))
    return _pcall(
        body, "attn_bwd", (heads // hp, qi_tab.shape[0]), 3,
        in_specs=[pl.BlockSpec((t, hp * MLA_QK_PAD), qmap), pl.BlockSpec((t, hp * MLA_NOPE), kmap),
                  pl.BlockSpec((t, hp * MLA_V), kmap),
                  pl.BlockSpec((t, LANES), lambda h, s, qi, kj, need: (kj[s], 0)),
                  pl.BlockSpec((t, 1), lambda h, s, qi, kj, need: (qi[s], 0)),
                  pl.BlockSpec((1, t), lambda h, s, qi, kj, need: (0, kj[s])),
                  pl.BlockSpec((t, hp * MLA_V), qmap), pl.BlockSpec((t, hp * LANES), qmap),
                  pl.BlockSpec((t, hp * MLA_V), qmap), whole, whole, whole],
        out_specs=[pl.BlockSpec((S, hp * MLA_QK_PAD), lambda h, s, qi, kj, need: (0, h)),
                   pl.BlockSpec((t, hp * MLA_NOPE), kmap), pl.BlockSpec((t, hp * MLA_V), kmap),
                   pl.BlockSpec((t, hp * LANES), kmap)],
        out_shape=[jax.ShapeDtypeStruct((S, heads * MLA_QK_PAD), BF16), jax.ShapeDtypeStruct((S, heads * MLA_NOPE), BF16),
                   jax.ShapeDtypeStruct((S, heads * MLA_V), BF16), jax.ShapeDtypeStruct((S, heads * LANES), F32)],
        scratch=[pltpu.VMEM((t, hp * MLA_QK_PAD), F32), pltpu.VMEM((t, hp * MLA_V), F32),
                 pltpu.VMEM((S, hp * MLA_QK_PAD), F32)],
        sem=("parallel", "arbitrary"),
        args=[qi_tab, kj_tab, aux['need'], q, kn, v, kr, aux['cidq'], aux['cidk'], o, lse, do, *aux['rope'][1]],
        comm=comm)


def make_attention(aux, heads):
    tabs_f, tabs_b = aux['rope']
    odd, every = (lambda blk: blk % 2 == 1), (lambda blk: True)

    def run_host(q_raw, kn, v, kr_raw, comm=None):
        q = _rope_call(q_raw, tabs_f, odd, BF16, "rope_q", scale=ATTN_SCALE2)
        kr = _rope_call(kr_raw, tabs_f, every, BF16, "rope_k")
        o, lse, *carried = _attn_fwd_call(q, kn, v, kr, aux, heads, comm)
        return o, (q, kn, v, kr, o, lse), carried

    def bwd_host(res, do, comm=None):
        q, kn, v, kr, o, lse = res
        dq_raw, dkn, dv, dkr, *carried = _attn_bwd_call(q, kn, v, kr, o, lse, do, aux, heads, comm)
        return (dq_raw, dkn, dv, _rope_call(dkr, tabs_b, every, F32, "rope_dk", fold=True)), carried

    @jax.custom_vjp
    def attn(q_raw, kn, v, kr_raw):
        return run_host(q_raw, kn, v, kr_raw)[0]

    attn.defvjp(lambda *a: run_host(*a)[:2], lambda res, do: bwd_host(res, do)[0])
    return attn, run_host, bwd_host


GLA_ROWS = 256
GLA_HEADS_PER_STEP = 2


def _tri(lower):
    r = lax.broadcasted_iota(jnp.int32, (CHUNK, CHUNK), 0)
    c = lax.broadcasted_iota(jnp.int32, (CHUNK, CHUNK), 1)
    return jnp.where((c <= r) if lower else (c >= r), 1.0, 0.0).astype(F32)


def _gla_chunk(q_ref, k_ref, v_ref, la_ref, sl, hh):
    lk, lv = slice(hh * GLA_DK, (hh + 1) * GLA_DK), slice(hh * GLA_DV, (hh + 1) * GLA_DV)
    la = la_ref[sl, lk]
    cum = jnp.dot(_tri(True), la, preferred_element_type=F32, precision=lax.Precision.HIGHEST)
    tot = cum[CHUNK - 1:CHUNK, :]
    e = jnp.exp(tot - cum)
    k = k_ref[sl, lk].astype(F32)
    kdec = k * e
    v = v_ref[sl, lv]
    upd_t = lax.dot_general(v.astype(BF16), kdec.astype(BF16), (((0,), (0,)), ((), ())), preferred_element_type=F32)
    qs = (q_ref[sl, lk].astype(F32) * (GLA_DK ** -0.5)).astype(BF16)
    return e, k, kdec, v, upd_t, jnp.exp(tot), qs


def _gla_group(heads):
    return GLA_HEADS_PER_STEP if heads % GLA_HEADS_PER_STEP == 0 else 1


def _gla_specs(heads, hp, rows_map):
    groups = heads // hp
    return [pl.BlockSpec((GLA_ROWS, hp * GLA_DK), lambda h, b: (rows_map(b), h)),
            pl.BlockSpec((GLA_ROWS, hp * GLA_DK), lambda h, b: (rows_map(b), groups + h)),
            pl.BlockSpec((GLA_ROWS, hp * GLA_DV), lambda h, b: (rows_map(b), groups + h)),
            pl.BlockSpec((GLA_ROWS, hp * GLA_DK), lambda h, b: (rows_map(b), h))]


def _gla_fwd_call(hm, la, heads):
    S = hm.shape[0]
    assert S % GLA_ROWS == 0
    nb, cpb, hp = S // GLA_ROWS, GLA_ROWS // CHUNK, _gla_group(heads)

    def body(q_ref, k_ref, v_ref, la_ref, o_ref, sp_ref, st_ref):
        @pl.when(pl.program_id(1) == 0)
        def _():
            st_ref[...] = jnp.zeros(st_ref.shape, F32)

        for c in range(cpb):
            sl = slice(c * CHUNK, (c + 1) * CHUNK)
            for hh in range(hp):
                _, _, _, _, upd_t, decay, qs = _gla_chunk(q_ref, k_ref, v_ref, la_ref, sl, hh)
                state = st_ref[hh]
                sp_ref[hh, c] = state
                state = state * decay + upd_t
                st_ref[hh] = state
                o_ref[sl, hh * GLA_DV:(hh + 1) * GLA_DV] = lax.dot_general(
                    qs, state.astype(BF16), (((1,), (1,)), ((), ())), preferred_element_type=F32)

    return pl.pallas_call(
        body, name=_nm("gla_fwd"), grid=(heads // hp, nb),
        out_shape=(jax.ShapeDtypeStruct((S, heads * GLA_DV), F32),
                   jax.ShapeDtypeStruct((heads, S // CHUNK, GLA_DV, GLA_DK), F32)),
        in_specs=_gla_specs(heads, hp, lambda b: b),
        out_specs=(pl.BlockSpec((GLA_ROWS, hp * GLA_DV), lambda h, b: (b, h)),
                   pl.BlockSpec((hp, cpb, GLA_DV, GLA_DK), lambda h, b: (h, b, 0, 0))),
        scratch_shapes=[pltpu.VMEM((hp, GLA_DV, GLA_DK), F32)],
        compiler_params=_cp(("parallel", "arbitrary")),
    )(hm, hm, hm, la)


def _gla_bwd_call(hm, la, sprev, do, heads):
    S = hm.shape[0]
    nb, cpb, hp = S // GLA_ROWS, GLA_ROWS // CHUNK, _gla_group(heads)
    scale = GLA_DK ** -0.5

    def body(q_ref, k_ref, v_ref, la_ref, sp_ref, do_ref, dq_ref, dk_ref, dv_ref, dla_ref, carry_ref):
        @pl.when(pl.program_id(1) == 0)
        def _():
            carry_ref[...] = jnp.zeros(carry_ref.shape, F32)

        for c in reversed(range(cpb)):
            sl = slice(c * CHUNK, (c + 1) * CHUNK)
            for hh in range(hp):
                lk, lv = slice(hh * GLA_DK, (hh + 1) * GLA_DK), slice(hh * GLA_DV, (hh + 1) * GLA_DV)
                e, k, kdec, v, upd_t, decay, qs = _gla_chunk(q_ref, k_ref, v_ref, la_ref, sl, hh)
                sp = sp_ref[hh, c]
                s_n = sp * decay + upd_t
                dob = do_ref[sl, lv].astype(BF16)
                g = carry_ref[hh] + lax.dot_general(dob, qs, (((0,), (0,)), ((), ())), preferred_element_type=F32)
                gb = g.astype(BF16)
                dq_ref[sl, lk] = (jnp.dot(dob, s_n.astype(BF16), preferred_element_type=F32) * scale).astype(dq_ref.dtype)
                ddecay = jnp.sum(g * sp, axis=0, keepdims=True)
                dkdec = jnp.dot(v.astype(BF16), gb, preferred_element_type=F32)
                dv_ref[sl, lv] = lax.dot_general(kdec.astype(BF16), gb, (((1,), (1,)), ((), ())),
                                                 preferred_element_type=F32).astype(dv_ref.dtype)
                dk_ref[sl, lk] = (dkdec * e).astype(dk_ref.dtype)
                w = dkdec * k * e
                dtot = jnp.sum(w, axis=0, keepdims=True) + ddecay * decay
                last = lax.broadcasted_iota(jnp.int32, (CHUNK, 1), 0) == CHUNK - 1
                dcum = jnp.where(last, dtot - w, -w)
                dla_ref[sl, lk] = jnp.dot(_tri(False), dcum, preferred_element_type=F32, precision=lax.Precision.HIGHEST)
                carry_ref[hh] = g * decay

    rev = lambda b: nb - 1 - b
    narrow = pl.BlockSpec((GLA_ROWS, hp * GLA_DK), lambda h, b: (rev(b), h))
    wide = pl.BlockSpec((GLA_ROWS, hp * GLA_DV), lambda h, b: (rev(b), h))
    return pl.pallas_call(
        body, name=_nm("gla_bwd"), grid=(heads // hp, nb),
        out_shape=(jax.ShapeDtypeStruct((S, heads * GLA_DK), hm.dtype), jax.ShapeDtypeStruct((S, heads * GLA_DK), hm.dtype),
                   jax.ShapeDtypeStruct((S, heads * GLA_DV), hm.dtype), jax.ShapeDtypeStruct((S, heads * GLA_DK), F32)),
        in_specs=_gla_specs(heads, hp, rev) + [
            pl.BlockSpec((hp, cpb, GLA_DV, GLA_DK), lambda h, b: (h, rev(b), 0, 0)), wide],
        out_specs=(narrow, narrow, wide, narrow),
        scratch_shapes=[pltpu.VMEM((hp, GLA_DV, GLA_DK), F32)],
        compiler_params=_cp(("parallel", "arbitrary")),
    )(hm, hm, hm, la, sprev, do)


@functools.partial(jax.custom_vjp, nondiff_argnums=(3,))
def gla_core(hm, la, o_norm, heads):
    return _gla_core_fwd(hm, la, o_norm, heads)[0]


def _gla_core_fwd(hm, la, o_norm, heads):
    o, sprev = _gla_fwd_call(hm, la, heads)
    vd = heads * GLA_DV
    rows = [(o, vd, 0), (hm, vd, 2 * heads * GLA_DK // vd + 1)]
    (y,) = _rw_fwd(_gla_out_fn(heads), rows, [o_norm], [BF16], "gla_out_fwd")
    return y, (hm, la, o_norm, o, sprev)


def _gla_core_bwd(heads, res, dy):
    hm, la, o_norm, o, sprev = res
    vd = heads * GLA_DV
    rows = [(o, vd, 0), (hm, vd, 2 * heads * GLA_DK // vd + 1)]
    (do, dr), (dg,) = _rw_bwd(_gla_out_fn(heads), rows, [o_norm], [dy], [F32, hm.dtype], "gla_out_bwd")
    dq, dk, dv, dla = _gla_bwd_call(hm, la, sprev, do, heads)
    return jnp.concatenate([dq, dk, dv, dr], axis=1), dla, dg


gla_core.defvjp(_gla_core_fwd, _gla_core_bwd)


CONV_COLS = 256
HALO = 16


def _conv_rows(S):
    return min(512, S)


def _conv_taps(main_ref, halo_ref, i):
    prev = jnp.where(i > 0, halo_ref[...].astype(F32), 0.0)
    full = jnp.concatenate([prev, main_ref[...].astype(F32)], axis=0)
    return full[HALO:], pltpu.roll(full, 1, 0)[HALO:], pltpu.roll(full, 2, 0)[HALO:]


def _conv_apply(taps, w_ref, b_ref):
    x0, x1, x2 = taps
    return x2 * w_ref[0:1, :] + x1 * w_ref[1:2, :] + x0 * w_ref[2:3, :] + b_ref[...]


def _gelu_gate(uc, gc):
    return uc * jax.nn.gelu(gc)


def _conv_cols(dff, pref):
    return max(c for c in range(LANES, pref + 1, LANES) if dff % c == 0)


def _conv_in_specs(R, C, nj):
    hpr = R // HALO
    main = lambda off: pl.BlockSpec((R, C), lambda j, i: (i, j + off))
    halo = lambda off: pl.BlockSpec((HALO, C), lambda j, i: (jnp.maximum(i * hpr - 1, 0), j + off))
    par = lambda rows, off: pl.BlockSpec((rows, C), lambda j, i: (0, j + off))
    return [main(0), halo(0), main(nj), halo(nj), par(CONV_W, 0), par(CONV_W, nj), par(1, 0), par(1, nj)]


def _conv_fwd_call(h, cw, cb, comm=None):
    S, dff = h.shape[0], h.shape[1] // 2
    R, C = _conv_rows(S), _conv_cols(dff, 768)
    nj = dff // C

    def body(u_ref, uh_ref, g_ref, gh_ref, wu_ref, wg_ref, bu_ref, bg_ref, a_ref):
        i = pl.program_id(1)
        uc = _conv_apply(_conv_taps(u_ref, uh_ref, i), wu_ref, bu_ref)
        gc = _conv_apply(_conv_taps(g_ref, gh_ref, i), wg_ref, bg_ref)
        a_ref[...] = _gelu_gate(uc, gc).astype(a_ref.dtype)

    return _pcall(
        body, "conv_fwd", (nj, S // R), 0, in_specs=_conv_in_specs(R, C, nj),
        out_specs=[pl.BlockSpec((R, C), lambda j, i: (i, j))], out_shape=[jax.ShapeDtypeStruct((S, dff), BF16)],
        scratch=[], sem=("parallel", "parallel"), args=[h, h, h, h, cw, cw, cb, cb], comm=comm)


def _conv_bwd_gate_call(h, cw, cb, da, comm=None):
    S, dff = h.shape[0], h.shape[1] // 2
    R, C = _conv_rows(S), _conv_cols(dff, 512)
    nj = dff // C

    def body(u_ref, uh_ref, g_ref, gh_ref, wu_ref, wg_ref, bu_ref, bg_ref, da_ref,
             du_ref, dg_ref, dwu_ref, dwg_ref, dbu_ref, dbg_ref):
        i = pl.program_id(1)
        ut, gt = _conv_taps(u_ref, uh_ref, i), _conv_taps(g_ref, gh_ref, i)
        uc, gc = _conv_apply(ut, wu_ref, bu_ref), _conv_apply(gt, wg_ref, bg_ref)
        _, vjp_fn = jax.vjp(_gelu_gate, uc, gc)
        du, dg = vjp_fn(da_ref[...].astype(F32))
        du_ref[...] = du.astype(du_ref.dtype)
        dg_ref[...] = dg.astype(dg_ref.dtype)

        @pl.when(i == 0)
        def _():
            for r in (dwu_ref, dwg_ref, dbu_ref, dbg_ref):
                r[...] = jnp.zeros(r.shape, F32)

        for d, taps, dw_ref, db_ref in ((du, ut, dwu_ref, dbu_ref), (dg, gt, dwg_ref, dbg_ref)):
            x0, x1, x2 = taps
            dw_ref[0:1, :] += jnp.sum(d * x2, axis=0, keepdims=True)
            dw_ref[1:2, :] += jnp.sum(d * x1, axis=0, keepdims=True)
            dw_ref[2:3, :] += jnp.sum(d * x0, axis=0, keepdims=True)
            db_ref[...] += jnp.sum(d, axis=0, keepdims=True)

    tile = pl.BlockSpec((R, C), lambda j, i: (i, j))
    par = lambda rows: pl.BlockSpec((rows, C), lambda j, i: (0, j))
    return _pcall(
        body, "conv_bwd_gate", (nj, S // R), 0, in_specs=_conv_in_specs(R, C, nj) + [tile],
        out_specs=[tile, tile, par(CONV_W), par(CONV_W), par(1), par(1)],
        out_shape=[jax.ShapeDtypeStruct((S, dff), BF16), jax.ShapeDtypeStruct((S, dff), BF16),
                   jax.ShapeDtypeStruct((CONV_W, dff), F32), jax.ShapeDtypeStruct((CONV_W, dff), F32),
                   jax.ShapeDtypeStruct((1, dff), F32), jax.ShapeDtypeStruct((1, dff), F32)],
        scratch=[], sem=("parallel", "arbitrary"), args=[h, h, h, h, cw, cw, cb, cb, da], comm=comm)


def _conv_bwd_shift_call(dc, cw, into=None):
    S, dff = dc.shape
    R, C = min(1024, S), _conv_cols(dff, 1024)
    nj = dff // C
    col_off = 0 if into is None else nj
    hpr, last = R // HALO, S // HALO - 1
    ni = S // R

    def body(d_ref, nx_ref, w_ref, *rest):
        o_ref = rest[-1]
        i = pl.program_id(1)
        nxt = jnp.where(i < ni - 1, nx_ref[...].astype(F32), 0.0)
        full = jnp.concatenate([d_ref[...].astype(F32), nxt], axis=0)
        n = R + HALO
        y1, y2 = pltpu.roll(full, n - 1, 0)[:R], pltpu.roll(full, n - 2, 0)[:R]
        o_ref[...] = (full[:R] * w_ref[2:3, :] + y1 * w_ref[1:2, :] + y2 * w_ref[0:1, :]).astype(o_ref.dtype)

    in_specs = [pl.BlockSpec((R, C), lambda j, i: (i, j)),
                pl.BlockSpec((HALO, C), lambda j, i: (jnp.minimum((i + 1) * hpr, last), j)),
                pl.BlockSpec((CONV_W, C), lambda j, i: (0, j + col_off))]
    args = [dc, dc, cw]
    if into is not None:
        in_specs.append(pl.BlockSpec(memory_space=pl.ANY))
        args.append(into)
    return pl.pallas_call(
        body, name=_nm("conv_bwd_shift"), grid=(nj, ni), out_shape=jax.ShapeDtypeStruct((S, 2 * dff), BF16),
        in_specs=in_specs, out_specs=pl.BlockSpec((R, C), lambda j, i: (i, j + col_off)),
        input_output_aliases={} if into is None else {3: 0},
        compiler_params=_cp(("parallel", "parallel")),
    )(*args)


@jax.custom_vjp
def ffn_hidden(x1, x1_bf16, w3, cw, cb):
    return _ffn_hidden_fwd(x1, x1_bf16, w3, cw, cb)[0]


def _ffn_hidden_fwd(x1, x1_bf16, w3, cw, cb, comm=None):
    h = _mm(x1_bf16, w3, "nn", BF16, "up", _all_blocks(w3))
    a, *carried = _conv_fwd_call(h, cw, cb, comm)
    return (a, (x1_bf16, w3, cw, cb, h)) + ((carried,) if comm is not None else ())


def _ffn_hidden_bwd(res, da, comm=None):
    x1, w3, cw, cb, h = res
    du, dg, dwu, dwg, dbu, dbg, *carried = _conv_bwd_gate_call(h, cw, cb, da, comm)
    dh = _conv_bwd_shift_call(dg, cw, into=_conv_bwd_shift_call(du, cw))
    dx = _mm(dh, w3, "nt", F32, "up_dx", _all_blocks(w3))
    dw3 = _mm(x1, dh, "tn", w3.dtype, "up_dw", _all_blocks(w3))
    grads = (dx, jnp.zeros_like(x1), dw3, jnp.concatenate([dwu, dwg], axis=1), jnp.concatenate([dbu, dbg], axis=1))
    return (grads, carried) if comm is not None else grads


ffn_hidden.defvjp(_ffn_hidden_fwd, _ffn_hidden_bwd)


def _loss_call(y, target):
    S, D = y.shape
    tr = _row_tile(S, D)

    def body(y_ref, t_ref, sq_ref, dy_ref):
        diff = y_ref[...] - t_ref[...]
        dy_ref[...] = diff * (1.0 / D)
        part = jnp.sum(diff * diff, axis=0, keepdims=True)
        i = pl.program_id(0)

        @pl.when(i == 0)
        def _():
            sq_ref[...] = part

        @pl.when(i > 0)
        def _():
            sq_ref[...] += part

    row = pl.BlockSpec((tr, D), lambda i: (i, 0))
    return pl.pallas_call(
        body, name=_nm("loss"), grid=(S // tr,),
        out_shape=(jax.ShapeDtypeStruct((1, D), F32), jax.ShapeDtypeStruct((S, D), F32)),
        in_specs=[row, row], out_specs=(pl.BlockSpec((1, D), lambda i: (0, 0)), row),
        compiler_params=_cp(("arbitrary",)),
    )(y, target)


def _row_divisor(rows):
    for cand in range(min(rows, 512), 15, -1):
        if rows % cand == 0 and cand % 16 == 0:
            return cand
    return rows


def _adamw_call(w, g, m, v):
    shape = w.shape
    w2, g2, m2, v2 = (a.reshape(-1, shape[-1]) for a in (w, g, m, v))
    rows, cols = w2.shape
    tr = _row_divisor(rows)

    def body(w_ref, g_ref, m_ref, v_ref, d_ref, nm_ref, nv_ref):
        g_ = g_ref[...]
        m_ = ADAM_B1 * m_ref[...] + (1.0 - ADAM_B1) * g_
        v_ = ADAM_B2 * v_ref[...] + (1.0 - ADAM_B2) * (g_ * g_)
        m_hat = m_ / (1.0 - ADAM_B1 ** ADAM_STEP)
        v_hat = v_ / (1.0 - ADAM_B2 ** ADAM_STEP)
        d_ref[...] = -ADAM_LR * (m_hat / (jnp.sqrt(v_hat) + ADAM_EPS) + ADAM_WD * w_ref[...])
        nm_ref[...] = m_
        nv_ref[...] = v_

    blk = pl.BlockSpec((tr, cols), lambda i: (i, 0))
    outs = pl.pallas_call(
        body, name=_nm("adamw"), grid=(rows // tr,),
        out_shape=tuple(jax.ShapeDtypeStruct((rows, cols), F32) for _ in range(3)),
        in_specs=[blk] * 4, out_specs=(blk,) * 3, compiler_params=_cp(("parallel",)),
    )(w2, g2, m2, v2)
    return tuple(o.reshape(shape) for o in outs)


ANY = pl.BlockSpec(memory_space=pl.ANY)


def _place():
    return lax.axis_index("x"), lax.axis_index("y"), lax.axis_index("c")


def all_gather(shards):
    n = len(shards)

    def body(*refs):
        x_refs, out_refs = refs[:n], refs[n:2 * n]
        send_sems, recv_sems, local_sems = refs[2 * n:]
        x, y, c = _place()
        me, sibling = (x, y, c), (x, y, 1 - c)
        chips = [(1 - x, y), (x, 1 - y), (1 - x, 1 - y)]

        def slot(a, px, py, pc):
            return out_refs[a].at[:, 4 * px + 2 * py + pc]

        def copy(a, k, block, to, own=False):
            return pltpu.make_async_remote_copy(
                src_ref=x_refs[a] if own else slot(a, *block), dst_ref=slot(a, *block),
                send_sem=send_sems.at[7 * a + k], recv_sem=recv_sems.at[7 * a + k], device_id=to, device_id_type=MESH)

        mine = [pltpu.make_async_copy(x_refs[a], slot(a, *me), local_sems.at[a]) for a in range(n)]
        first = []
        for a in range(n):
            mine[a].start()
            first.append(copy(a, 0, me, sibling, own=True))
            first += [copy(a, 1 + j, me, (*chip, c), own=True) for j, chip in enumerate(chips)]
        for cp in first:
            cp.start()
        passed = []
        for j, chip in enumerate(chips):
            for a in range(n):
                copy(a, 1 + j, (*chip, c), me).wait_recv()
                passed.append(copy(a, 4 + j, (*chip, c), sibling))
                passed[-1].start()
        for a in range(n):
            copy(a, 0, sibling, me).wait_recv()
        for j, chip in enumerate(chips):
            for a in range(n):
                copy(a, 4 + j, (*chip, 1 - c), me).wait_recv()
        for cp in first + passed:
            cp.wait_send()
        for cp in mine:
            cp.wait()

    return pl.pallas_call(
        body, name=_nm("all_gather"),
        out_shape=tuple(jax.ShapeDtypeStruct((s.shape[0], N_DEV) + s.shape[1:], s.dtype) for s in shards),
        in_specs=[ANY] * n, out_specs=(ANY,) * n,
        scratch_shapes=[pltpu.SemaphoreType.DMA((7 * n,)), pltpu.SemaphoreType.DMA((7 * n,)), pltpu.SemaphoreType.DMA((n,))],
    )(*shards)


def _rs_pair_exchange(gs):
    n = len(gs)

    def body(*refs):
        g_refs, recv_refs = refs[:n], refs[n:2 * n]
        send_sems, recv_sems = refs[2 * n:]
        x, y, c = _place()
        copies = [pltpu.make_async_remote_copy(
            src_ref=g_refs[a].at[:, 2 * j + (1 - c)], dst_ref=recv_refs[a].at[j], send_sem=send_sems.at[4 * a + j],
            recv_sem=recv_sems.at[4 * a + j], device_id=(x, y, 1 - c), device_id_type=MESH)
            for a in range(n) for j in range(4)]
        for cp in copies:
            cp.start()
        for cp in copies:
            cp.wait_recv()
        for cp in copies:
            cp.wait_send()

    return pl.pallas_call(
        body, name=_nm("rs_pair"),
        out_shape=tuple(jax.ShapeDtypeStruct((4, g.shape[0]) + g.shape[2:], g.dtype) for g in gs),
        in_specs=[ANY] * n, out_specs=(ANY,) * n,
        scratch_shapes=[pltpu.SemaphoreType.DMA((4 * n,)), pltpu.SemaphoreType.DMA((4 * n,))],
    )(*gs)


def _rs_chip_exchange(ps):
    n = len(ps)

    def body(*refs):
        p_refs, recv_refs = refs[:n], refs[n:2 * n]
        send_sems, recv_sems = refs[2 * n:]
        x, y, c = _place()
        chips = [(1 - x, y), (x, 1 - y), (1 - x, 1 - y)]
        copies = [pltpu.make_async_remote_copy(
            src_ref=p_refs[a].at[2 * cx + cy], dst_ref=recv_refs[a].at[k], send_sem=send_sems.at[3 * a + k],
            recv_sem=recv_sems.at[3 * a + k], device_id=(cx, cy, c), device_id_type=MESH)
            for a in range(n) for k, (cx, cy) in enumerate(chips)]
        for cp in copies:
            cp.start()
        for cp in copies:
            cp.wait_recv()
        for cp in copies:
            cp.wait_send()

    return pl.pallas_call(
        body, name=_nm("rs_chip"),
        out_shape=tuple(jax.ShapeDtypeStruct((3,) + p.shape[1:], p.dtype) for p in ps),
        in_specs=[ANY] * n, out_specs=(ANY,) * n,
        scratch_shapes=[pltpu.SemaphoreType.DMA((3 * n,)), pltpu.SemaphoreType.DMA((3 * n,))],
    )(*ps)


def _rs_pair_add(g, recv, c_idx):
    L, _, a, b = g.shape
    ta = _row_divisor(a)

    def body(c_ref, g_ref, r_ref, o_ref):
        o_ref[...] = (g_ref[...].astype(F32) + r_ref[...].astype(F32)).astype(o_ref.dtype)

    grid_spec = pltpu.PrefetchScalarGridSpec(
        num_scalar_prefetch=1, grid=(4, L, a // ta),
        in_specs=[pl.BlockSpec((None, None, ta, b), lambda j, l, i, c_ref: (l, 2 * j + c_ref[0], i, 0)),
                  pl.BlockSpec((None, None, ta, b), lambda j, l, i, c_ref: (j, l, i, 0))],
        out_specs=pl.BlockSpec((None, None, ta, b), lambda j, l, i, c_ref: (j, l, i, 0)))
    return pl.pallas_call(
        body, name=_nm("rs_pair_add"), grid_spec=grid_spec, out_shape=jax.ShapeDtypeStruct((4, L, a, b), g.dtype),
        compiler_params=_cp(("parallel", "parallel", "parallel")),
    )(c_idx, g, recv)


def _rs_final_add(p1, recv, chip_idx):
    _, L, a, b = p1.shape
    ta = _row_divisor(a)

    def body(chip_ref, p_ref, r_ref, o_ref):
        acc = p_ref[...].astype(F32)
        for k in range(3):
            acc = acc + r_ref[k].astype(F32)
        o_ref[...] = acc

    grid_spec = pltpu.PrefetchScalarGridSpec(
        num_scalar_prefetch=1, grid=(L, a // ta),
        in_specs=[pl.BlockSpec((None, None, ta, b), lambda l, i, chip_ref: (chip_ref[0], l, i, 0)),
                  pl.BlockSpec((3, None, ta, b), lambda l, i, chip_ref: (0, l, i, 0))],
        out_specs=pl.BlockSpec((None, ta, b), lambda l, i, chip_ref: (l, i, 0)))
    return pl.pallas_call(
        body, name=_nm("rs_final_add"), grid_spec=grid_spec, out_shape=jax.ShapeDtypeStruct((L, a, b), F32),
        compiler_params=_cp(("parallel", "parallel")),
    )(chip_idx, p1, recv)


def reduce_scatter(gs):
    x, y, c = _place()
    c_idx = jnp.reshape(c, (1,)).astype(jnp.int32)
    chip_idx = jnp.reshape(2 * x + y, (1,)).astype(jnp.int32)
    recv1 = _rs_pair_exchange(gs)
    p1 = [_rs_pair_add(g, r, c_idx) for g, r in zip(gs, recv1)]
    recv2 = _rs_chip_exchange(p1)
    return [_rs_final_add(p, r, chip_idx) for p, r in zip(p1, recv2)]


def all_reduce_small(v):
    r, C = v.shape

    def body(v_ref, out_ref, buf_ref, send_sems, recv_sems):
        x, y, c = _place()
        my_id = 4 * x + 2 * y + c
        buf_ref[my_id] = v_ref[...]
        copies = []
        for k in range(1, N_DEV):
            fx, fy, fc = (k >> 2) & 1, (k >> 1) & 1, k & 1
            peer = (x ^ fx, y ^ fy, c ^ fc)
            copies.append(pltpu.make_async_remote_copy(
                src_ref=v_ref, dst_ref=buf_ref.at[my_id], send_sem=send_sems.at[k - 1], recv_sem=recv_sems.at[k - 1],
                device_id=peer, device_id_type=MESH))
        for cp in copies:
            cp.start()
        for cp in copies:
            cp.wait_recv()
        for cp in copies:
            cp.wait_send()
        acc = buf_ref[0]
        for d in range(1, N_DEV):
            acc = acc + buf_ref[d]
        out_ref[...] = acc

    vm = pl.BlockSpec(memory_space=pltpu.VMEM)
    return pl.pallas_call(
        body, name=_nm("all_reduce_small"), out_shape=jax.ShapeDtypeStruct((r, C), F32),
        in_specs=[vm], out_specs=vm,
        scratch_shapes=[pltpu.VMEM((N_DEV, r, C), F32), pltpu.SemaphoreType.DMA((7,)), pltpu.SemaphoreType.DMA((7,))],
    )(v)


def _slot(ref, px, py, pc):
    return ref.at[:, 4 * px + 2 * py + pc]


def comm_gather_own(shards):
    n = len(shards)

    def build(cin, cout, send_sems, recv_sems, local_sems):
        x, y, c = _place()
        me = (x, y, c)
        direct = [(x, y, 1 - c), (1 - x, y, c), (x, 1 - y, c)]
        src_nb, dst_nb, diag = (x ^ (1 - c), y ^ c, c), (x ^ c, y ^ (1 - c), c), (1 - x, 1 - y, c)
        starts, mid_waits, mid_starts, waits = [], [], [], []
        for a in range(n):
            local = pltpu.make_async_copy(cin[a], _slot(cout[a], *me), local_sems.at[a])
            starts.append(local)
            waits.append(local.wait)
            for k, peer in enumerate(direct):
                send = pltpu.make_async_remote_copy(
                    src_ref=cin[a], dst_ref=_slot(cout[a], *me), send_sem=send_sems.at[4 * a + k],
                    recv_sem=recv_sems.at[4 * a + k], device_id=peer, device_id_type=MESH)
                arrive = pltpu.make_async_remote_copy(
                    src_ref=cin[a], dst_ref=_slot(cout[a], *peer), send_sem=send_sems.at[4 * a + k],
                    recv_sem=recv_sems.at[4 * a + k], device_id=peer, device_id_type=MESH)
                starts.append(send)
                (waits if k == 0 else mid_waits).append(arrive.wait_recv)
                waits.append(send.wait_send)
            onward = pltpu.make_async_remote_copy(
                src_ref=_slot(cout[a], *src_nb), dst_ref=_slot(cout[a], *src_nb), send_sem=send_sems.at[4 * a + 3],
                recv_sem=recv_sems.at[4 * a + 3], device_id=dst_nb, device_id_type=MESH)
            arrive = pltpu.make_async_remote_copy(
                src_ref=_slot(cout[a], *src_nb), dst_ref=_slot(cout[a], *diag), send_sem=send_sems.at[4 * a + 3],
                recv_sem=recv_sems.at[4 * a + 3], device_id=dst_nb, device_id_type=MESH)
            mid_starts.append(onward)
            waits += [arrive.wait_recv, onward.wait_send]
        return starts, mid_waits, mid_starts, waits

    out_shapes = [jax.ShapeDtypeStruct((s.shape[0], N_DEV) + s.shape[1:], s.dtype) for s in shards]
    return Comm(shards, out_shapes, {}, 4 * n, 4 * n, n, build)


def comm_gather_pass(partial):
    n = len(partial)

    def build(cin, cout, send_sems, recv_sems, local_sems):
        x, y, c = _place()
        chips = [(1 - x, y), (x, 1 - y), (1 - x, 1 - y)]
        starts, waits = [], []
        for a in range(n):
            for j, chip in enumerate(chips):
                send = pltpu.make_async_remote_copy(
                    src_ref=_slot(cout[a], *chip, c), dst_ref=_slot(cout[a], *chip, c), send_sem=send_sems.at[3 * a + j],
                    recv_sem=recv_sems.at[3 * a + j], device_id=(x, y, 1 - c), device_id_type=MESH)
                arrive = pltpu.make_async_remote_copy(
                    src_ref=_slot(cout[a], *chip, c), dst_ref=_slot(cout[a], *chip, 1 - c),
                    send_sem=send_sems.at[3 * a + j], recv_sem=recv_sems.at[3 * a + j],
                    device_id=(x, y, 1 - c), device_id_type=MESH)
                starts.append(send)
                waits += [arrive.wait_recv, send.wait_send]
        return starts, waits

    out_shapes = [jax.ShapeDtypeStruct(p.shape, p.dtype) for p in partial]
    return Comm(partial, out_shapes, {a: a for a in range(n)}, 3 * n, 3 * n, 1, build)


def comm_rs_pair(gs):
    n = len(gs)

    def build(cin, cout, send_sems, recv_sems, local_sems):
        x, y, c = _place()
        starts, waits = [], []
        for a in range(n):
            for j in range(4):
                cp = pltpu.make_async_remote_copy(
                    src_ref=cin[a].at[:, 2 * j + (1 - c)], dst_ref=cout[a].at[j], send_sem=send_sems.at[4 * a + j],
                    recv_sem=recv_sems.at[4 * a + j], device_id=(x, y, 1 - c), device_id_type=MESH)
                starts.append(cp)
                waits += [cp.wait_recv, cp.wait_send]
        return starts, waits

    out_shapes = [jax.ShapeDtypeStruct((4, g.shape[0]) + g.shape[2:], g.dtype) for g in gs]
    return Comm(gs, out_shapes, {}, 4 * n, 4 * n, 1, build)


def comm_rs_chip(ps):
    n = len(ps)

    def build(cin, cout, send_sems, recv_sems, local_sems):
        x, y, c = _place()
        chips = [(1 - x, y), (x, 1 - y), (1 - x, 1 - y)]
        starts, waits = [], []
        for a in range(n):
            for k, (cx, cy) in enumerate(chips):
                cp = pltpu.make_async_remote_copy(
                    src_ref=cin[a].at[2 * cx + cy], dst_ref=cout[a].at[k], send_sem=send_sems.at[3 * a + k],
                    recv_sem=recv_sems.at[3 * a + k], device_id=(cx, cy, c), device_id_type=MESH)
                starts.append(cp)
                waits += [cp.wait_recv, cp.wait_send]
        return starts, waits

    out_shapes = [jax.ShapeDtypeStruct((3,) + p.shape[1:], p.dtype) for p in ps]
    return Comm(ps, out_shapes, {}, 3 * n, 3 * n, 1, build)


def _pack(arrays, dtype, row_align):
    lead = arrays[0].shape[:-1]
    quantum = row_align * PACK_COLS
    parts, sizes = [], []
    for a in arrays:
        n = a.shape[-1]
        padded = _round_up(n, quantum)
        a = a.astype(dtype)
        if padded != n:
            a = jnp.pad(a, [(0, 0)] * len(lead) + [(0, padded - n)])
        parts.append(a.reshape(*lead, padded // PACK_COLS, PACK_COLS))
        sizes.append((n, padded // PACK_COLS))
    return jnp.concatenate(parts, axis=len(lead)), sizes


def _unpack(packed, sizes):
    lead = packed.shape[:-2]
    out, row = [], 0
    for n, rows in sizes:
        part = lax.slice_in_dim(packed, row, row + rows, axis=len(lead))
        out.append(part.reshape(*lead, rows * PACK_COLS)[..., :n])
        row += rows
    return out


def _unshard(gathered, axis):
    _, L, a, b = gathered.shape
    if axis == 1:
        return [gathered[:, l].reshape(N_DEV * a, b) for l in range(L)]
    return [jnp.transpose(gathered[:, l], (1, 0, 2)).reshape(a, N_DEV * b) for l in range(L)]


def _reshard(fulls, axis):
    blocks = []
    for f in fulls:
        A, B = f.shape
        if axis == 1:
            blocks.append(f.reshape(N_DEV, A // N_DEV, B))
        else:
            blocks.append(jnp.transpose(f.reshape(A, N_DEV, B // N_DEV), (1, 0, 2)))
    return jnp.stack(blocks, axis=1)


def _as3(a):
    return a if a.ndim == 3 else a[:, None, :]


def _prep_big(name, w, dims):
    w = w.astype(BF16)
    L, a, b = w.shape
    if name == 'mla_w_in':
        return jnp.pad(w, ((0, 0), (0, 0), (0, dims['h_width'] - b)))
    if name == 'mla_w_uq':
        hd = MLA_NOPE + MLA_ROPE
        w = jnp.pad(w.reshape(L, a, b // hd, hd), ((0, 0), (0, 0), (0, 0), (0, MLA_QK_PAD - hd)))
        return w.reshape(L, a, b // hd * MLA_QK_PAD)
    if name == 'ffn_w_up':
        return jnp.pad(w, ((0, 0), (0, 0), (0, _round_up(b, CONV_COLS) - b)))
    return w


def _unprep_big(name, g, shape):
    L, a, b = shape
    if name == 'mla_w_uq':
        hd = MLA_NOPE + MLA_ROPE
        return g.reshape(L, a, b // hd, MLA_QK_PAD)[..., :hd].reshape(L, a, b)
    return g[:, :, :b]


def _rope_tables(positions):
    inv = 1.0 / (ROPE_THETA ** (jnp.arange(0, MLA_ROPE, 2, dtype=F32) / MLA_ROPE))
    ang = positions.astype(F32)[:, None] * inv
    cos, sin = jnp.cos(ang), jnp.sin(ang)
    one, zero = jnp.ones_like(cos), jnp.zeros_like(cos)
    a = jnp.concatenate([cos, cos, one, one], axis=1)
    up = jnp.concatenate([sin, zero, zero, zero], axis=1)
    down = jnp.concatenate([zero, sin, zero, zero], axis=1)
    return (a, -up, down), (a, up, -down)


def _rows_full(w):
    return w.reshape(w.shape[0] * w.shape[1], w.shape[2])


def _ops(depth):
    alpha = (2 * depth) ** 0.25
    return {'ln_res': rw_op(_ln_res_fn(alpha), "ln_res", 2, [F32], shadow=True),
            'ple': rw_op(_ple_fn, "ple", 3, [F32], shadow=True), 'gla_gate': rw_op(_gla_gate_fn, "gla_gate", 1, [F32])}


MLA_PRE_W = ['mla_w_in', 'mla_q_norm', 'mla_kv_norm', 'mla_w_uq', 'mla_w_uk', 'mla_w_uv']
FFN_IN_W = ['mla_w_o', 'ln1_g', 'ln1_b', 'ffn_conv_w', 'ffn_conv_b']
FFN_OUT_W = ['ffn_w_down', 'ln2_g', 'ln2_b', 'ple_w_gate', 'ple_w_proj', 'ple_b_gate']


def _mla_heads(wl, j):
    return N_DEV * wl['mla_w_uk'][j].shape[2] // MLA_NOPE


def _mla_pre(x, xb, wl, j):
    w_uq, w_uk, w_uv = wl['mla_w_uq'][j], wl['mla_w_uk'][j], wl['mla_w_uv'][j]
    h = linear_sh(x, xb, _rows_full(wl['mla_w_in'][j]), F32)
    mla_norm = rw_op(_mla_norm_fn(w_uq.shape[1], w_uk.shape[1]), "mla_norm", 1, [BF16, BF16, F32])
    cq, ckv, kr_raw = mla_norm(h, wl['mla_q_norm'][j], wl['mla_kv_norm'][j])
    return linear(cq, w_uq, BF16), linear(ckv, w_uk, BF16), linear(ckv, w_uv, BF16), kr_raw


def _gla_mixer(x, xb, wl, j, ops):
    w_in3, w_a2 = wl['gla_w_in'][j], wl['gla_w_a2'][j]
    w_o = _rows_full(wl['gla_w_o'][j])
    w_in = jnp.transpose(w_in3, (1, 0, 2)).reshape(w_in3.shape[1], N_DEV * w_in3.shape[2])
    heads = w_o.shape[0] // GLA_DV
    n_main = 2 * heads * GLA_DK + 2 * heads * GLA_DV
    w_a = jnp.pad(w_in[:, n_main:], ((0, 0), (0, LANES - GLA_RANK)))
    w_a2_p = jnp.pad(w_a2, ((0, LANES - GLA_RANK), (0, 0))).astype(BF16)
    hm = linear_sh(x, xb, w_in[:, :n_main], BF16)
    ha = linear_sh(x, xb, w_a, BF16)
    (la,) = ops['gla_gate'](linear(ha, w_a2_p, F32), wl['gla_b_a'][j])
    return linear(gla_core(hm, la, wl['gla_o_norm'][j], heads), w_o, F32)


def _ffn_in(x, m, wl, i, ops, bp):
    x1, x1b = ops['ln_res'](x, m, wl['ln1_g'][i], wl['ln1_b'][i])
    cw, cb = wl['ffn_conv_w'][i], wl['ffn_conv_b'][i]
    bu = cw.shape[1] // N_DEV
    cwp = jnp.pad(cw.reshape(CONV_W, N_DEV, bu), ((0, 0), (0, 0), (0, bp - bu))).reshape(CONV_W, N_DEV * bp)
    cbp = jnp.pad(cb.reshape(1, N_DEV, bu), ((0, 0), (0, 0), (0, bp - bu))).reshape(1, N_DEV * bp)
    return x1, lax.stop_gradient(x1b), cwp, cbp


def _ffn_out(x1, a, wl, p_i, i, ops):
    w_down3 = wl['ffn_w_down'][i]
    half, bu, d_model = N_DEV // 2, 2 * w_down3.shape[1], w_down3.shape[2]
    bp = a.shape[1] // half
    w_down = jnp.pad(w_down3.reshape(half, bu, d_model), ((0, 0), (0, bp - bu), (0, 0))).reshape(half * bp, d_model)
    f = linear(a, w_down, F32)
    x2, x2b = ops['ln_res'](x1, f, wl['ln2_g'][i], wl['ln2_b'][i])
    glog = linear_sh(x2, lax.stop_gradient(x2b), _rows_full(wl['ple_w_gate'][i]), F32)
    pp = linear(p_i, wl['ple_w_proj'][i], F32)
    x, xb = ops['ple'](x2, glog, pp, wl['ple_b_gate'][i])
    return x, lax.stop_gradient(xb)


def _layer(x, xb, wl, p_i, aux, i, ops):
    j = i // 2
    if i % 2 == 0:
        q_raw, kn, v, kr_raw = _mla_pre(x, xb, wl, j)
        o = make_attention(aux, _mla_heads(wl, j))[0](q_raw, kn, v, kr_raw)
        m = linear(o, _rows_full(wl['mla_w_o'][j]), F32)
    else:
        m = _gla_mixer(x, xb, wl, j, ops)
    x1, x1b, cwp, cbp = _ffn_in(x, m, wl, i, ops, wl['ffn_w_up'][i].shape[2])
    return _ffn_out(x1, ffn_hidden(x1, x1b, wl['ffn_w_up'][i], cwp, cbp), wl, p_i, i, ops)


def kernel(x, p, positions, mla_w_in, mla_q_norm, mla_kv_norm, mla_w_uq, mla_w_uk, mla_w_uv, mla_w_o, gla_w_in, gla_w_a2, gla_b_a, gla_o_norm, gla_w_o, ln1_g, ln1_b, ln2_g, ln2_b, ffn_w_up, ffn_conv_w, ffn_conv_b, ffn_w_down, ple_w_proj, ple_w_gate, ple_b_gate, loss_target, m_mla_w_in, m_mla_q_norm, m_mla_kv_norm, m_mla_w_uq, m_mla_w_uk, m_mla_w_uv, m_mla_w_o, m_gla_w_in, m_gla_w_a2, m_gla_b_a, m_gla_o_norm, m_gla_w_o, m_ln1_g, m_ln1_b, m_ln2_g, m_ln2_b, m_ffn_w_up, m_ffn_conv_w, m_ffn_conv_b, m_ffn_w_down, m_ple_w_proj, m_ple_w_gate, m_ple_b_gate, v_mla_w_in, v_mla_q_norm, v_mla_kv_norm, v_mla_w_uq, v_mla_w_uk, v_mla_w_uv, v_mla_w_o, v_gla_w_in, v_gla_w_a2, v_gla_b_a, v_gla_o_norm, v_gla_w_o, v_ln1_g, v_ln1_b, v_ln2_g, v_ln2_b, v_ffn_w_up, v_ffn_conv_w, v_ffn_conv_b, v_ffn_w_down, v_ple_w_proj, v_ple_w_gate, v_ple_b_gate):
    w = dict(zip(WEIGHTS, (mla_w_in, mla_q_norm, mla_kv_norm, mla_w_uq, mla_w_uk, mla_w_uv, mla_w_o, gla_w_in, gla_w_a2,
                           gla_b_a, gla_o_norm, gla_w_o, ln1_g, ln1_b, ln2_g, ln2_b, ffn_w_up, ffn_conv_w, ffn_conv_b,
                           ffn_w_down, ple_w_proj, ple_w_gate, ple_b_gate)))
    m_in = dict(zip(WEIGHTS, (m_mla_w_in, m_mla_q_norm, m_mla_kv_norm, m_mla_w_uq, m_mla_w_uk, m_mla_w_uv, m_mla_w_o,
                              m_gla_w_in, m_gla_w_a2, m_gla_b_a, m_gla_o_norm, m_gla_w_o, m_ln1_g, m_ln1_b, m_ln2_g,
                              m_ln2_b, m_ffn_w_up, m_ffn_conv_w, m_ffn_conv_b, m_ffn_w_down, m_ple_w_proj, m_ple_w_gate,
                              m_ple_b_gate)))
    v_in = dict(zip(WEIGHTS, (v_mla_w_in, v_mla_q_norm, v_mla_kv_norm, v_mla_w_uq, v_mla_w_uk, v_mla_w_uv, v_mla_w_o,
                              v_gla_w_in, v_gla_w_a2, v_gla_b_a, v_gla_o_norm, v_gla_w_o, v_ln1_g, v_ln1_b, v_ln2_g,
                              v_ln2_b, v_ffn_w_up, v_ffn_conv_w, v_ffn_conv_b, v_ffn_w_down, v_ple_w_proj, v_ple_w_gate,
                              v_ple_b_gate)))
    _uid[0] = itertools.count()
    x2d, target, pos = x[0], loss_target[0], positions[0]
    p3 = p[:, 0]
    dims = {'h_width': mla_w_uq.shape[1] + mla_w_uk.shape[1] + LANES}

    depth = ln1_g.shape[0]
    ops = _ops(depth)
    cid = pos // CHUNK
    aux = {'rope': _rope_tables(pos), 'cidq': cid[:, None], 'cidk': cid[None, :],
           'need': _mask_table(cid, _attn_tile(pos.shape[0]))}

    in_layer0 = set(MLA_PRE_W + FFN_IN_W + FFN_OUT_W + ['ffn_w_up'])
    prepped = {n: _prep_big(n, w[n], dims) for n in BIG}
    first_names = [n for n in BIG if n in MLA_PRE_W + ['mla_w_o']]
    mid_names = [n for n in BIG if n in in_layer0 and n not in first_names]
    rest_names = [n for n in BIG if prepped[n].shape[0] > (1 if n in in_layer0 else 0)]
    rest_from = {n: (1 if n in in_layer0 else 0) for n in rest_names}
    small3 = [_as3(w[n]) for n in SMALL]
    small_packed, small_sizes = _pack([s.reshape(1, -1) for s in small3], F32, 8)
    first = all_gather([prepped[n][:1] for n in first_names] + [small_packed])
    wl = {n: [None] * prepped[n].shape[0] for n in BIG}
    for n, g in zip(first_names, first):
        wl[n][0] = g[0]
    for n, s3, flat in zip(SMALL, small3, _unpack(first[-1][0], small_sizes)):
        wl[n] = _unshard(flat.reshape(N_DEV, *s3.shape), SHARD_AXIS[n] if w[n].ndim == 3 else 2)
    for n in REPL:
        wl[n] = [w[n][l][None, :] for l in range(w[n].shape[0])]

    def pick(names, layer0):
        return {n: [wl[n][l] if (l == 0 and n in in_layer0) == layer0 else None for l in range(len(wl[n]))] for n in names}

    heads0 = _mla_heads(wl, 0)
    _, attn_run, attn_bwd = make_attention(aux, heads0)
    x2db = x2d.astype(BF16)
    pre, vjp_pre = jax.vjp(lambda x_, wl_: _mla_pre(x_, x2db, wl_, 0), x2d, pick(MLA_PRE_W, True))
    o, attn_res, partial = attn_run(*pre, comm_gather_own([prepped[n][:1] for n in mid_names]
                                                          + [prepped[n][rest_from[n]:] for n in rest_names]))
    for n, g in zip(mid_names, _run_comm(comm_gather_pass(partial[:len(mid_names)]), "gather_pass")):
        wl[n][0] = g[0]
    partial = partial[len(mid_names):]
    w_up0 = wl['ffn_w_up'][0]

    def ffn_in(x_, o_, wl_):
        x1_, x1b_, cwp_, cbp_ = _ffn_in(x_, linear(o_, _rows_full(wl_['mla_w_o'][0]), F32), wl_, 0, ops, w_up0.shape[2])
        return (x1_, cwp_, cbp_), x1b_

    (x1, cwp, cbp), vjp_in, x1b = jax.vjp(ffn_in, x2d, o, pick(FFN_IN_W, True), has_aux=True)
    a, ffn_res, rest = _ffn_hidden_fwd(x1, x1b, w_up0, cwp, cbp, comm_gather_pass(partial))
    for n, g in zip(rest_names, rest):
        for l in range(g.shape[0]):
            wl[n][rest_from[n] + l] = g[l]
    x_l0, vjp_out, x_l0b = jax.vjp(lambda x1_, a_, wl_: _ffn_out(x1_, a_, wl_, p3[0], 0, ops), x1, a,
                                   pick(FFN_OUT_W, True), has_aux=True)

    def tail(x_, wl_):
        xb_ = x_l0b
        for i in range(1, depth):
            x_, xb_ = _layer(x_, xb_, wl_, p3[i], aux, i, ops)
        return x_

    y, vjp_tail = jax.vjp(tail, x_l0, pick(WEIGHTS, False))
    sq, dy = _loss_call(y, target)

    dwl = {n: [None] * len(wl[n]) for n in WEIGHTS}

    def keep(part):
        for n, per_layer in part.items():
            for l, g in enumerate(per_layer):
                if g is not None:
                    dwl[n][l] = g

    x_c, y_c, c_place = _place()
    c_idx = jnp.reshape(c_place, (1,)).astype(jnp.int32)
    chip_idx = jnp.reshape(2 * x_c + y_c, (1,)).astype(jnp.int32)
    dx_l0, d_tail = vjp_tail(dy)
    keep(d_tail)
    g_rest = [jnp.stack(dwl[n][rest_from[n]:], axis=0) for n in rest_names]
    dx1_out, da, d_out = vjp_out(dx_l0)
    keep(d_out)
    (dx1_ffn, _, dw_up0, dcwp, dcbp), recv1 = _ffn_hidden_bwd(ffn_res, da, comm_rs_pair(g_rest))
    dwl['ffn_w_up'][0] = dw_up0
    g_mid = [dwl[n][0][None] for n in mid_names]
    g_rest, recv1 = g_rest + g_mid, list(recv1) + list(_rs_pair_exchange(g_mid))
    p1 = [_rs_pair_add(g, r, c_idx) for g, r in zip(g_rest, recv1)]
    dx_in, do, d_in = vjp_in((dx1_out + dx1_ffn, dcwp, dcbp))
    keep(d_in)
    d_pre_in, recv2 = attn_bwd(attn_res, do, comm_rs_chip(p1))
    red_rest = [_rs_final_add(p_, r, chip_idx) for p_, r in zip(p1, recv2)]
    dx_pre, d_pre = vjp_pre(d_pre_in)
    keep(d_pre)
    dx = dx_pre + dx_in

    small_blocks = [_reshard(dwl[n], SHARD_AXIS[n] if w[n].ndim == 3 else 2).reshape(N_DEV, -1) for n in SMALL]
    small_grad_packed, _ = _pack(small_blocks, F32, 8)
    red_first = reduce_scatter([dwl[n][0][None] for n in first_names] + [small_grad_packed[None]])
    by_layer = {n: [] for n in BIG}
    for n, g in zip(first_names, red_first):
        by_layer[n].append(g)
    for n, g in zip(mid_names, red_rest[len(rest_names):]):
        by_layer[n].append(g)
    for n, g in zip(rest_names, red_rest):
        by_layer[n].append(g)
    grads = {n: _unprep_big(n, jnp.concatenate(by_layer[n], axis=0), w[n].shape) for n in BIG}
    for n, f in zip(SMALL, _unpack(red_first[-1][0], small_sizes)):
        grads[n] = f.reshape(w[n].shape)

    repl_flat = [jnp.concatenate([g.reshape(-1) for g in dwl[n]]).reshape(1, -1) for n in REPL]
    loss_part = 0.5 * jnp.sum(sq) / sq.shape[1]
    packed, repl_sizes = _pack(repl_flat + [loss_part.reshape(1, 1)], F32, 8)
    summed = _unpack(all_reduce_small(packed[0])[None], repl_sizes)
    for n, f in zip(REPL, summed[:-1]):
        grads[n] = f.reshape(w[n].shape)
    loss = summed[-1].reshape(())

    delta, new_m, new_v = {}, {}, {}
    for n in WEIGHTS:
        delta[n], new_m[n], new_v[n] = _adamw_call(w[n], grads[n], m_in[n], v_in[n])
    return (loss, dx[None], *[grads[n] for n in WEIGHTS], *[delta[n] for n in WEIGHTS],
            *[new_m[n] for n in WEIGHTS], *[new_v[n] for n in WEIGHTS])
```

```python
import functools
import itertools

import jax
import jax.numpy as jnp
from jax import lax
from jax.experimental import pallas as pl
from jax.experimental.pallas import tpu as pltpu

F32 = jnp.float32
BF16 = jnp.bfloat16
MESH = pl.DeviceIdType.MESH
N_DEV = 8

EPS = 1e-5
NEG_INF = -1e30
CHUNK = 64
Q_BLOCK = 128
MLA_NOPE = 128
MLA_ROPE = 64
MLA_V = 128
MLA_QK_PAD = 256
ROPE_THETA = 10000.0
GLA_DK = 128
GLA_DV = 256
GLA_RANK = 16
GLA_TAU = 16.0
CONV_W = 3
ADAM_LR = 0.001
ADAM_B1 = 0.9
ADAM_B2 = 0.999
ADAM_EPS = 1e-08
ADAM_WD = 0.01
ADAM_STEP = 10
LOG2E = 1.4426950408889634

LANES = 128
PACK_COLS = 1024
VMEM_LIMIT = 56 * 1024 * 1024
MM_VMEM_BUDGET = 38 * 1024 * 1024

WEIGHTS = ['mla_w_in', 'mla_q_norm', 'mla_kv_norm', 'mla_w_uq', 'mla_w_uk', 'mla_w_uv', 'mla_w_o', 'gla_w_in',
           'gla_w_a2', 'gla_b_a', 'gla_o_norm', 'gla_w_o', 'ln1_g', 'ln1_b', 'ln2_g', 'ln2_b', 'ffn_w_up',
           'ffn_conv_w', 'ffn_conv_b', 'ffn_w_down', 'ple_w_proj', 'ple_w_gate', 'ple_b_gate']
SHARD_AXIS = {'mla_w_in': 1, 'mla_q_norm': None, 'mla_kv_norm': None, 'mla_w_uq': 2, 'mla_w_uk': 2, 'mla_w_uv': 2,
              'mla_w_o': 1, 'gla_w_in': 2, 'gla_w_a2': 2, 'gla_b_a': 1, 'gla_o_norm': 1, 'gla_w_o': 1,
              'ln1_g': None, 'ln1_b': None, 'ln2_g': None, 'ln2_b': None, 'ffn_w_up': 2, 'ffn_conv_w': 2,
              'ffn_conv_b': None, 'ffn_w_down': 1, 'ple_w_proj': 2, 'ple_w_gate': 1, 'ple_b_gate': None}
BIG = ['mla_w_in', 'mla_w_uq', 'mla_w_uk', 'mla_w_uv', 'mla_w_o', 'gla_w_in', 'gla_w_o', 'ffn_w_up', 'ffn_w_down',
       'ple_w_proj', 'ple_w_gate']
SMALL = ['gla_w_a2', 'gla_b_a', 'gla_o_norm', 'ffn_conv_w']
REPL = [n for n in WEIGHTS if SHARD_AXIS[n] is None]

_uid = [itertools.count()]


def _nm(base):
    return f"{base}_{next(_uid[0])}"


def _cp(sem=None):
    return pltpu.CompilerParams(dimension_semantics=sem, vmem_limit_bytes=VMEM_LIMIT)


def _round_up(n, m):
    return -(-n // m) * m


class Comm:
    MID_AT = 0.6

    def __init__(self, inputs, out_shapes, aliases, n_send, n_recv, n_local, build):
        self.inputs, self.out_shapes, self.aliases = list(inputs), list(out_shapes), dict(aliases)
        self.n_send, self.n_recv, self.n_local, self.build = n_send, n_recv, n_local, build


def _pcall(body, name, grid, n_prefetch, in_specs, out_specs, out_shape, scratch, sem, args, comm=None):
    in_specs, out_specs, out_shape, scratch, args = list(in_specs), list(out_specs), list(out_shape), list(scratch), list(args)
    n_in, n_out, n_scr = len(in_specs), len(out_specs), len(scratch)
    aliases = {}
    kernel_body = body
    if comm is not None:
        ci, co = len(comm.inputs), len(comm.out_shapes)
        any_spec = pl.BlockSpec(memory_space=pl.ANY)
        in_specs += [any_spec] * ci
        out_specs += [any_spec] * co
        out_shape += comm.out_shapes
        scratch += [pltpu.SemaphoreType.DMA((comm.n_send,)), pltpu.SemaphoreType.DMA((comm.n_recv,)),
                    pltpu.SemaphoreType.DMA((comm.n_local,))]
        aliases = {n_prefetch + n_in + k: n_out + v for k, v in comm.aliases.items()}
        args += comm.inputs
        sem = ("arbitrary",) * len(grid)

        def kernel_body(*refs):
            pre, r = refs[:n_prefetch], refs[n_prefetch:]
            ins, cin = r[:n_in], r[n_in:n_in + ci]
            outs, cout = r[n_in + ci:n_in + ci + n_out], r[n_in + ci + n_out:n_in + ci + n_out + co]
            scr = r[n_in + ci + n_out + co:n_in + ci + n_out + co + n_scr]
            send_sems, recv_sems, local_sems = r[-3:]
            first = functools.reduce(lambda a, b: a & b, [pl.program_id(d) == 0 for d in range(len(grid))])
            last = functools.reduce(lambda a, b: a & b, [pl.program_id(d) == grid[d] - 1 for d in range(len(grid))])
            built = comm.build(cin, cout, send_sems, recv_sems, local_sems)
            starts, waits = built[0], built[-1]
            mid_waits, mid_starts = (built[1], built[2]) if len(built) == 4 else ([], [])

            @pl.when(first)
            def _():
                for cp in starts:
                    cp.start()

            if mid_starts:
                step = pl.program_id(0)
                for d in range(1, len(grid)):
                    step = step * grid[d] + pl.program_id(d)
                n_steps = functools.reduce(lambda a, b: a * b, grid)

                @pl.when(step == int(n_steps * comm.MID_AT))
                def _():
                    for wait in mid_waits:
                        wait()
                    for cp in mid_starts:
                        cp.start()

            body(*pre, *ins, *outs, *scr)

            @pl.when(last)
            def _():
                for wait in waits:
                    wait()

    grid_spec = pltpu.PrefetchScalarGridSpec(num_scalar_prefetch=n_prefetch, grid=grid, in_specs=in_specs,
                                             out_specs=out_specs, scratch_shapes=scratch)
    return pl.pallas_call(kernel_body, name=_nm(name), grid_spec=grid_spec, out_shape=tuple(out_shape),
                          input_output_aliases=aliases, compiler_params=_cp(sem))(*args)


def _run_comm(comm, name):
    return _pcall(lambda: None, name, (1,), 0, [], [], [], [], ("arbitrary",), [], comm)


def _divisor_tiles(n, cap):
    if n % LANES:
        return [n]
    out = [t for t in range(LANES, min(n, cap) + 1, LANES) if n % t == 0]
    return out or [n]


def _mm_tiles(M, N, K, abytes, bbytes, obytes, tn_fixed=None, tk_fixed=None):
    best = None
    for tm in _divisor_tiles(M, 2048):
        for tn in ([tn_fixed] if tn_fixed else _divisor_tiles(N, 1536)):
            for tk in ([tk_fixed] if tk_fixed else _divisor_tiles(K, 4096)):
                vmem = 2 * (tm * tk * abytes + tk * tn * bbytes + tm * tn * obytes) + tm * tn * 4
                if vmem > MM_VMEM_BUDGET:
                    continue
                key = (tm * tn * tk, tk)
                if best is None or key > best[0]:
                    best = (key, (tm, tn, tk))
    assert best is not None, (M, N, K)
    return best[1]


def _mm(a, b, mode, out_dtype, base="mm", blocks=None):
    blk0, nblk = blocks if blocks else (0, 1)
    tn_fixed = tk_fixed = None
    if mode == "nn":
        M, K = a.shape
        N = nblk * b.shape[2] if blocks else b.shape[1]
        tn_fixed = b.shape[2] if blocks else None
    elif mode == "nt":
        M, K = a.shape
        N = b.shape[1] if blocks else b.shape[0]
        tk_fixed = b.shape[2] if blocks else None
        assert not blocks or K == nblk * b.shape[2]
    else:
        (K, M), N = a.shape, b.shape[1]
        tn_fixed = N // nblk if blocks else None
    tm, tn, tk = _mm_tiles(M, N, K, a.dtype.itemsize, b.dtype.itemsize, jnp.dtype(out_dtype).itemsize, tn_fixed, tk_fixed)
    nk = K // tk
    out_shape = jax.ShapeDtypeStruct((M, N), out_dtype)
    out_spec = pl.BlockSpec((tm, tn), lambda i, j, k: (i, j))
    if mode == "nn":
        a_spec = pl.BlockSpec((tm, tk), lambda i, j, k: (i, k))
        b_spec = (pl.BlockSpec((None, tk, tn), lambda i, j, k: (blk0 + j, k, 0)) if blocks
                  else pl.BlockSpec((tk, tn), lambda i, j, k: (k, j)))
        dims = (((1,), (0,)), ((), ()))
    elif mode == "nt":
        a_spec = pl.BlockSpec((tm, tk), lambda i, j, k: (i, k))
        b_spec = (pl.BlockSpec((None, tn, tk), lambda i, j, k: (blk0 + k, j, 0)) if blocks
                  else pl.BlockSpec((tn, tk), lambda i, j, k: (j, k)))
        dims = (((1,), (1,)), ((), ()))
    else:
        a_spec = pl.BlockSpec((tk, tm), lambda i, j, k: (k, i))
        b_spec = pl.BlockSpec((tk, tn), lambda i, j, k: (k, j))
        dims = (((0,), (0,)), ((), ()))
        if blocks:
            out_shape = jax.ShapeDtypeStruct((nblk, M, tn), out_dtype)
            out_spec = pl.BlockSpec((None, tm, tn), lambda i, j, k: (j, i, 0))

    def body(a_ref, b_ref, o_ref, acc_ref):
        part = lax.dot_general(a_ref[...].astype(BF16), b_ref[...].astype(BF16), dims, preferred_element_type=F32)
        if nk == 1:
            o_ref[...] = part.astype(o_ref.dtype)
        else:
            k = pl.program_id(2)

            @pl.when(k == 0)
            def _():
                acc_ref[...] = part

            @pl.when(k > 0)
            def _():
                acc_ref[...] += part

            @pl.when(k == nk - 1)
            def _():
                o_ref[...] = acc_ref[...].astype(o_ref.dtype)

    return pl.pallas_call(
        body, name=_nm(base), grid=(M // tm, N // tn, nk), out_shape=out_shape,
        in_specs=[a_spec, b_spec], out_specs=out_spec,
        scratch_shapes=[pltpu.VMEM((tm, tn) if nk > 1 else (8, LANES), F32)],
        compiler_params=_cp(("parallel", "parallel", "arbitrary")),
    )(a, b)


def _all_blocks(w):
    return (0, w.shape[0]) if w.ndim == 3 else None


@functools.partial(jax.custom_vjp, nondiff_argnums=(2,))
def linear(a, w, out_dtype):
    return _mm(a, w, "nn", out_dtype, "lin_fwd", _all_blocks(w))


def _linear_fwd(a, w, out_dtype):
    return _mm(a, w, "nn", out_dtype, "lin_fwd", _all_blocks(w)), (a, w)


def _linear_bwd(out_dtype, res, dy):
    a, w = res
    return (_mm(dy, w, "nt", a.dtype, "lin_dx", _all_blocks(w)), _mm(a, dy, "tn", w.dtype, "lin_dw", _all_blocks(w)))


linear.defvjp(_linear_fwd, _linear_bwd)


@functools.partial(jax.custom_vjp, nondiff_argnums=(3,))
def linear_sh(a, a_bf16, w, out_dtype):
    return _mm(a_bf16, w, "nn", out_dtype, "lin_fwd", _all_blocks(w))


def _linear_sh_fwd(a, a_bf16, w, out_dtype):
    return _mm(a_bf16, w, "nn", out_dtype, "lin_fwd", _all_blocks(w)), (a_bf16, w, jnp.zeros((), a.dtype))


def _linear_sh_bwd(out_dtype, res, dy):
    a_bf16, w, tok = res
    return (_mm(dy, w, "nt", tok.dtype, "lin_dx", _all_blocks(w)), jnp.zeros_like(a_bf16),
            _mm(a_bf16, dy, "tn", w.dtype, "lin_dw", _all_blocks(w)))


linear_sh.defvjp(_linear_sh_fwd, _linear_sh_bwd)


def _row_tile(S, width):
    tr = 512 if width <= 1024 else 256
    return min(tr, S)


def _rw_fwd(f, rows, params, out_dtypes, base):
    S = rows[0][0].shape[0]
    tr = _row_tile(S, max(w for _, w, _ in rows))
    n_in = len(rows) + len(params)
    avals = [jax.ShapeDtypeStruct((tr, w), F32) for _, w, _ in rows] + [jax.ShapeDtypeStruct(p.shape, F32) for p in params]
    outs = jax.eval_shape(f, *avals)

    def body(*refs):
        vals = [r[...].astype(F32) for r in refs[:n_in]]
        for o_ref, r in zip(refs[n_in:], f(*vals)):
            o_ref[...] = r.astype(o_ref.dtype)

    in_specs = [pl.BlockSpec((tr, w), functools.partial(lambda i, cb: (i, cb), cb=cb)) for _, w, cb in rows]
    in_specs += [pl.BlockSpec(p.shape, lambda i: (0, 0)) for p in params]
    return pl.pallas_call(
        body, name=_nm(base), grid=(S // tr,),
        out_shape=tuple(jax.ShapeDtypeStruct((S, o.shape[1]), dt) for o, dt in zip(outs, out_dtypes)),
        in_specs=in_specs, out_specs=tuple(pl.BlockSpec((tr, o.shape[1]), lambda i: (i, 0)) for o in outs),
        compiler_params=_cp(("parallel",)),
    )(*[a for a, _, _ in rows], *params)


def _rw_bwd(f, rows, params, cts, row_grad_dtypes, base):
    S = rows[0][0].shape[0]
    tr = _row_tile(S, max(w for _, w, _ in rows))
    n_rows, n_par, n_ct = len(rows), len(params), len(cts)
    want = [k for k, dt in enumerate(row_grad_dtypes) if dt is not None]

    def body(*refs):
        in_refs = refs[:n_rows + n_par]
        ct_refs = refs[n_rows + n_par:n_rows + n_par + n_ct]
        out_refs = refs[n_rows + n_par + n_ct:]
        vals = [r[...].astype(F32) for r in in_refs]
        _, vjp_fn = jax.vjp(f, *vals)
        grads = vjp_fn(tuple(c[...].astype(F32) for c in ct_refs))
        for o_ref, k in zip(out_refs[:len(want)], want):
            o_ref[...] = grads[k].astype(o_ref.dtype)
        i = pl.program_id(0)
        for o_ref, g in zip(out_refs[len(want):], grads[n_rows:]):
            @pl.when(i == 0)
            def _(o_ref=o_ref, g=g):
                o_ref[...] = g

            @pl.when(i > 0)
            def _(o_ref=o_ref, g=g):
                o_ref[...] += g

    in_specs = [pl.BlockSpec((tr, w), functools.partial(lambda i, cb: (i, cb), cb=cb)) for _, w, cb in rows]
    in_specs += [pl.BlockSpec(p.shape, lambda i: (0, 0)) for p in params]
    in_specs += [pl.BlockSpec((tr, c.shape[1]), lambda i: (i, 0)) for c in cts]
    out_shape = [jax.ShapeDtypeStruct((S, rows[k][1]), row_grad_dtypes[k]) for k in want]
    out_specs = [pl.BlockSpec((tr, rows[k][1]), lambda i: (i, 0)) for k in want]
    out_shape += [jax.ShapeDtypeStruct(p.shape, F32) for p in params]
    out_specs += [pl.BlockSpec(p.shape, lambda i: (0, 0)) for p in params]
    res = pl.pallas_call(
        body, name=_nm(base), grid=(S // tr,), out_shape=tuple(out_shape),
        in_specs=in_specs, out_specs=tuple(out_specs), compiler_params=_cp(("arbitrary",)),
    )(*[a for a, _, _ in rows], *params, *cts)
    row_grads = [None] * n_rows
    for k, g in zip(want, res[:len(want)]):
        row_grads[k] = g
    return row_grads, list(res[len(want):])


def rw_op(f, base, n_rows, out_dtypes, shadow=False):
    f_fwd = (lambda *a: (lambda r: tuple(r) + (r[0],))(f(*a))) if shadow else f
    fwd_dtypes = list(out_dtypes) + ([BF16] if shadow else [])

    @jax.custom_vjp
    def op(*args):
        return fwd(*args)[0]

    def split(args):
        rows = [(a, a.shape[1], 0) for a in args[:n_rows]]
        return rows, list(args[n_rows:])

    def fwd(*args):
        rows, params = split(args)
        return tuple(_rw_fwd(f_fwd, rows, params, fwd_dtypes, base + "_fwd")), args

    def bwd(args, cts):
        rows, params = split(args)
        cts = list(cts)[:len(out_dtypes)]
        rg, pg = _rw_bwd(f, rows, params, cts, [a.dtype for a, _, _ in rows], base + "_bwd")
        return tuple(rg) + tuple(g.astype(p.dtype) for g, p in zip(pg, params))

    op.defvjp(fwd, bwd)
    return op


def _ln_res_fn(alpha):
    def f(x, m, g, b):
        z = alpha * x + m
        mu = jnp.mean(z, -1, keepdims=True)
        zc = z - mu
        var = jnp.mean(zc * zc, -1, keepdims=True)
        return (zc * lax.rsqrt(var + EPS) * g + b,)
    return f


def _rms(x, g):
    return x * lax.rsqrt(jnp.mean(x * x, -1, keepdims=True) + EPS) * g


def _mla_norm_fn(q_lora, kv_lora):
    def f(h, qn, kvn):
        return (_rms(h[:, :q_lora], qn), _rms(h[:, q_lora:q_lora + kv_lora], kvn),
                h[:, q_lora + kv_lora:q_lora + kv_lora + LANES])
    return f


def _log_sigmoid(z):
    return jnp.minimum(z, 0.0) - jnp.log(1.0 + jnp.exp(-jnp.abs(z)))


def _gla_gate_fn(z, b):
    return (_log_sigmoid(z + b) / GLA_TAU,)


def _ple_fn(x, glog, pp, b):
    return (x + jax.nn.sigmoid(glog + b) * pp,)


def _gla_out_fn(heads):
    def f(o, r, g):
        parts = []
        for h in range(heads):
            oh = o[:, h * GLA_DV:(h + 1) * GLA_DV]
            mu = jnp.mean(oh, -1, keepdims=True)
            oc = oh - mu
            var = jnp.mean(oc * oc, -1, keepdims=True)
            parts.append(oc * lax.rsqrt(var + EPS) * g[:, h * GLA_DV:(h + 1) * GLA_DV])
        return (jnp.concatenate(parts, axis=1) * (r * jax.nn.sigmoid(r)),)
    return f


def _rope_call(x, tabs, roped, out_dtype, base, fold=False, scale=None):
    S, C = x.shape
    nb = C // LANES
    tr = min(512 if C <= 1024 else 256, S)
    out_c = LANES if fold else C

    def rot(v, a, b1, b2):
        return v * a + pltpu.roll(v, 96, 1) * b1 + pltpu.roll(v, 32, 1) * b2

    def body(x_ref, a_ref, b1_ref, b2_ref, o_ref):
        a, b1, b2 = a_ref[...], b1_ref[...], b2_ref[...]
        if fold:
            v = x_ref[:, 0:LANES].astype(F32)
            for blk in range(1, nb):
                v = v + x_ref[:, blk * LANES:(blk + 1) * LANES].astype(F32)
            o_ref[...] = rot(v, a, b1, b2).astype(o_ref.dtype)
            return
        for blk in range(nb):
            v = x_ref[:, blk * LANES:(blk + 1) * LANES].astype(F32)
            if roped(blk):
                v = rot(v, a, b1, b2)
            if scale is not None:
                v = v * scale
            o_ref[:, blk * LANES:(blk + 1) * LANES] = v.astype(o_ref.dtype)

    row = lambda w: pl.BlockSpec((tr, w), lambda i: (i, 0))
    return pl.pallas_call(
        body, name=_nm(base), grid=(S // tr,), out_shape=jax.ShapeDtypeStruct((S, out_c), out_dtype),
        in_specs=[row(C), row(LANES), row(LANES), row(LANES)], out_specs=row(out_c),
        compiler_params=_cp(("parallel",)),
    )(x, *tabs)


def _attn_tile(S):
    return min(512, S)


def _tri_schedule(n, by_key):
    pairs = ([(i, j) for j in range(n) for i in range(j, n)] if by_key
             else [(i, j) for i in range(n) for j in range(i + 1)])
    return (jnp.asarray([p[0] for p in pairs], jnp.int32), jnp.asarray([p[1] for p in pairs], jnp.int32))


def _mask_table(cid, t):
    n = cid.shape[0] // t
    blocks = cid.reshape(n, t)
    cmin_q, cmax_k = jnp.min(blocks, axis=1), jnp.max(blocks, axis=1)
    need = (cmax_k[None, :] > cmin_q[:, None]) | jnp.eye(n, dtype=bool)
    return need.astype(jnp.int32).reshape(n * n)


def _attn_mask(cidq_ref, cidk_ref, i, j, t):
    qrow = i * t + lax.broadcasted_iota(jnp.int32, (t, 1), 0)
    kcol = j * t + lax.broadcasted_iota(jnp.int32, (1, t), 1)
    qlim = (qrow // Q_BLOCK + 1) * Q_BLOCK
    return (cidk_ref[...] <= cidq_ref[...]) & (kcol < qlim)


ATTN_FWD_HEADS = 8
ATTN_BWD_HEADS = 4
ATTN_SCALE = (MLA_NOPE + MLA_ROPE) ** -0.5
ATTN_SCALE2 = ATTN_SCALE * LOG2E


def _attn_fwd_call(q, kn, v, kr, aux, heads, comm=None):
    S = q.shape[0]
    t = _attn_tile(S)
    n = S // t
    hp = ATTN_FWD_HEADS if heads % ATTN_FWD_HEADS == 0 else 1
    qi_tab, kj_tab = _tri_schedule(n, False)

    def body(qi_ref, kj_ref, need_ref, q_ref, kn_ref, v_ref, kr_ref, cidq_ref, cidk_ref, o_ref, lse_ref,
             m_ref, l_ref, acc_ref):
        st = pl.program_id(1)
        i, j = qi_ref[st], kj_ref[st]

        @pl.when(j == 0)
        def _():
            m_ref[...] = jnp.full(m_ref.shape, NEG_INF, F32)
            l_ref[...] = jnp.zeros(l_ref.shape, F32)
            acc_ref[...] = jnp.zeros(acc_ref.shape, F32)

        def update(masked):
            mask = _attn_mask(cidq_ref, cidk_ref, i, j, t) if masked else None
            for hh in range(hp):
                lanes = slice(hh * LANES, (hh + 1) * LANES)
                k = jnp.concatenate([kn_ref[:, lanes], kr_ref[...]], axis=1)
                qh = q_ref[:, hh * MLA_QK_PAD:(hh + 1) * MLA_QK_PAD]
                s = lax.dot_general(qh, k, (((1,), (1,)), ((), ())), preferred_element_type=F32)
                if masked:
                    s = jnp.where(mask, s, NEG_INF)
                m_prev = m_ref[:, lanes]
                m_new = jnp.maximum(m_prev, jnp.max(s, axis=1, keepdims=True))
                alpha = jnp.exp2(m_prev - m_new)
                p = jnp.exp2(s - m_new[:, :1])
                l_ref[:, lanes] = alpha * l_ref[:, lanes] + jnp.sum(p, axis=1, keepdims=True)
                acc_ref[:, lanes] = alpha * acc_ref[:, lanes] + jnp.dot(p.astype(BF16), v_ref[:, lanes],
                                                                        preferred_element_type=F32)
                m_ref[:, lanes] = m_new

        need = need_ref[i * n + j]

        @pl.when(need != 0)
        def _():
            update(True)

        @pl.when(need == 0)
        def _():
            update(False)

        @pl.when(j == i)
        def _():
            o_ref[...] = (acc_ref[...] / l_ref[...]).astype(o_ref.dtype)
            lse_ref[...] = m_ref[...] + jnp.log(l_ref[...]) * LOG2E

    qmap = lambda h, s, qi, kj, need: (qi[s], h)
    kmap = lambda h, s, qi, kj, need: (kj[s], h)
    return _pcall(
        body, "attn_fwd", (heads // hp, qi_tab.shape[0]), 3,
        in_specs=[pl.BlockSpec((t, hp * MLA_QK_PAD), qmap), pl.BlockSpec((t, hp * MLA_NOPE), kmap),
                  pl.BlockSpec((t, hp * MLA_V), kmap),
                  pl.BlockSpec((t, LANES), lambda h, s, qi, kj, need: (kj[s], 0)),
                  pl.BlockSpec((t, 1), lambda h, s, qi, kj, need: (qi[s], 0)),
                  pl.BlockSpec((1, t), lambda h, s, qi, kj, need: (0, kj[s]))],
        out_specs=[pl.BlockSpec((t, hp * MLA_V), qmap), pl.BlockSpec((t, hp * LANES), qmap)],
        out_shape=[jax.ShapeDtypeStruct((S, heads * MLA_V), BF16), jax.ShapeDtypeStruct((S, heads * LANES), F32)],
        scratch=[pltpu.VMEM((t, hp * LANES), F32), pltpu.VMEM((t, hp * LANES), F32), pltpu.VMEM((t, hp * MLA_V), F32)],
        sem=("parallel", "arbitrary"),
        args=[qi_tab, kj_tab, aux['need'], q, kn, v, kr, aux['cidq'], aux['cidk']], comm=comm)


def _attn_bwd_call(q, kn, v, kr, o, lse, do, aux, heads, comm=None):
    S = q.shape[0]
    t = _attn_tile(S)
    n = S // t
    qi_tab, kj_tab = _tri_schedule(n, True)
    scale = ATTN_SCALE
    nt_dims = (((1,), (1,)), ((), ()))
    tn_dims = (((0,), (0,)), ((), ()))

    hp = ATTN_BWD_HEADS if heads % ATTN_BWD_HEADS == 0 else 1

    def body(qi_ref, kj_ref, need_ref, q_ref, kn_ref, v_ref, kr_ref, cidq_ref, cidk_ref, o_ref, lse_ref, do_ref,
             ta_ref, tb1_ref, tb2_ref, dq_ref, dkn_ref, dv_ref, dkr_ref, dk_acc, dv_acc, dq_acc):
        st = pl.program_id(1)
        i, j = qi_ref[st], kj_ref[st]

        @pl.when(st == 0)
        def _():
            dq_acc[...] = jnp.zeros(dq_acc.shape, F32)

        @pl.when(i == j)
        def _():
            dk_acc[...] = jnp.zeros(dk_acc.shape, F32)
            dv_acc[...] = jnp.zeros(dv_acc.shape, F32)

        rows = pl.ds(pl.multiple_of(i * t, t), t)

        def grads(masked):
            mask = _attn_mask(cidq_ref, cidk_ref, i, j, t) if masked else None
            for hh in range(hp):
                lanes = slice(hh * LANES, (hh + 1) * LANES)
                wide = slice(hh * MLA_QK_PAD, (hh + 1) * MLA_QK_PAD)
                k = jnp.concatenate([kn_ref[:, lanes], kr_ref[...]], axis=1)
                qt, do = q_ref[:, wide], do_ref[:, lanes]
                s = lax.dot_general(qt, k, nt_dims, preferred_element_type=F32)
                if masked:
                    s = jnp.where(mask, s, NEG_INF)
                dp = lax.dot_general(do, v_ref[:, lanes], nt_dims, preferred_element_type=F32)
                dsum = jnp.sum(do.astype(F32) * o_ref[:, lanes].astype(F32), axis=1, keepdims=True)
                p = jnp.exp2(s - lse_ref[:, hh * LANES:hh * LANES + 1])
                ds = (p * (dp - dsum) * scale).astype(BF16)
                dv_acc[:, lanes] += lax.dot_general(p.astype(BF16), do, tn_dims, preferred_element_type=F32)
                dk_acc[:, wide] += lax.dot_general(ds, qt, tn_dims, preferred_element_type=F32)
                dq_acc[rows, wide] += jnp.dot(ds, k, preferred_element_type=F32)

        need = need_ref[i * n + j]

        @pl.when(need != 0)
        def _():
            grads(True)

        @pl.when(need == 0)
        def _():
            grads(False)

        @pl.when(i == n - 1)
        def _():
            for hh in range(hp):
                lanes = slice(hh * LANES, (hh + 1) * LANES)
                off = hh * MLA_QK_PAD
                dkn_ref[:, lanes] = (dk_acc[:, off:off + MLA_NOPE] * (1.0 / ATTN_SCALE2)).astype(dkn_ref.dtype)
                dkr_ref[:, lanes] = dk_acc[:, off + MLA_NOPE:off + MLA_QK_PAD] * (1.0 / ATTN_SCALE2)
            dv_ref[...] = dv_acc[...].astype(dv_ref.dtype)

        @pl.when(i == j)
        def _():
            for hh in range(hp):
                off = hh * MLA_QK_PAD
                dq_ref[rows, off:off + MLA_NOPE] = dq_acc[rows, off:off + MLA_NOPE].astype(dq_ref.dtype)
                g = dq_acc[rows, off + MLA_NOPE:off + MLA_QK_PAD]
                g = g * ta_ref[...] + pltpu.roll(g, 96, 1) * tb1_ref[...] + pltpu.roll(g, 32, 1) * tb2_ref[...]
                dq_ref[rows, off + MLA_NOPE:off + MLA_QK_PAD] = g.astype(dq_ref.dtype)

    qmap = lambda h, s, qi, kj, need: (qi[s], h)
    kmap = lambda h, s, qi, kj, need: (kj[s], h)
    whole = pl.BlockSpec((t, LANES), lambda h, s, qi, kj, need: (qi[s], 0))
    return _pcall(
        body, "attn_bwd", (heads // hp, qi_tab.shape[0]), 3,
        in_specs=[pl.BlockSpec((t, hp * MLA_QK_PAD), qmap), pl.BlockSpec((t, hp * MLA_NOPE), kmap),
                  pl.BlockSpec((t, hp * MLA_V), kmap),
                  pl.BlockSpec((t, LANES), lambda h, s, qi, kj, need: (kj[s], 0)),
                  pl.BlockSpec((t, 1), lambda h, s, qi, kj, need: (qi[s], 0)),
                  pl.BlockSpec((1, t), lambda h, s, qi, kj, need: (0, kj[s])),
                  pl.BlockSpec((t, hp * MLA_V), qmap), pl.BlockSpec((t, hp * LANES), qmap),
                  pl.BlockSpec((t, hp * MLA_V), qmap), whole, whole, whole],
        out_specs=[pl.BlockSpec((S, hp * MLA_QK_PAD), lambda h, s, qi, kj, need: (0, h)),
                   pl.BlockSpec((t, hp * MLA_NOPE), kmap), pl.BlockSpec((t, hp * MLA_V), kmap),
                   pl.BlockSpec((t, hp * LANES), kmap)],
        out_shape=[jax.ShapeDtypeStruct((S, heads * MLA_QK_PAD), BF16), jax.ShapeDtypeStruct((S, heads * MLA_NOPE), BF16),
                   jax.ShapeDtypeStruct((S, heads * MLA_V), BF16), jax.ShapeDtypeStruct((S, heads * LANES), F32)],
        scratch=[pltpu.VMEM((t, hp * MLA_QK_PAD), F32), pltpu.VMEM((t, hp * MLA_V), F32),
                 pltpu.VMEM((S, hp * MLA_QK_PAD), F32)],
        sem=("parallel", "arbitrary"),
        args=[qi_tab, kj_tab, aux['need'], q, kn, v, kr, aux['cidq'], aux['cidk'], o, lse, do, *aux['rope'][1]],
        comm=comm)


def make_attention(aux, heads):
    tabs_f, tabs_b = aux['rope']
    odd, every = (lambda blk: blk % 2 == 1), (lambda blk: True)

    def run_host(q_raw, kn, v, kr_raw, comm=None):
        q = _rope_call(q_raw, tabs_f, odd, BF16, "rope_q", scale=ATTN_SCALE2)
        kr = _rope_call(kr_raw, tabs_f, every, BF16, "rope_k")
        o, lse, *carried = _attn_fwd_call(q, kn, v, kr, aux, heads, comm)
        return o, (q, kn, v, kr, o, lse), carried

    def bwd_host(res, do, comm=None):
        q, kn, v, kr, o, lse = res
        dq_raw, dkn, dv, dkr, *carried = _attn_bwd_call(q, kn, v, kr, o, lse, do, aux, heads, comm)
        return (dq_raw, dkn, dv, _rope_call(dkr, tabs_b, every, F32, "rope_dk", fold=True)), carried

    @jax.custom_vjp
    def attn(q_raw, kn, v, kr_raw):
        return run_host(q_raw, kn, v, kr_raw)[0]

    attn.defvjp(lambda *a: run_host(*a)[:2], lambda res, do: bwd_host(res, do)[0])
    return attn, run_host, bwd_host


GLA_ROWS = 256
GLA_HEADS_PER_STEP = 4


def _tri(lower):
    r = lax.broadcasted_iota(jnp.int32, (CHUNK, CHUNK), 0)
    c = lax.broadcasted_iota(jnp.int32, (CHUNK, CHUNK), 1)
    return jnp.where((c <= r) if lower else (c >= r), 1.0, 0.0).astype(F32)


def _gla_chunk(q_ref, k_ref, v_ref, la_ref, sl, hh):
    lk, lv = slice(hh * GLA_DK, (hh + 1) * GLA_DK), slice(hh * GLA_DV, (hh + 1) * GLA_DV)
    la = la_ref[sl, lk]
    cum = jnp.dot(_tri(True), la, preferred_element_type=F32, precision=lax.Precision.HIGHEST)
    tot = cum[CHUNK - 1:CHUNK, :]
    e = jnp.exp(tot - cum)
    k = k_ref[sl, lk].astype(F32)
    kdec = k * e
    v = v_ref[sl, lv]
    upd_t = lax.dot_general(v.astype(BF16), kdec.astype(BF16), (((0,), (0,)), ((), ())), preferred_element_type=F32)
    qs = (q_ref[sl, lk].astype(F32) * (GLA_DK ** -0.5)).astype(BF16)
    return e, k, kdec, v, upd_t, jnp.exp(tot), qs


def _gla_group(heads):
    return GLA_HEADS_PER_STEP if heads % GLA_HEADS_PER_STEP == 0 else 1


def _gla_specs(heads, hp, rows_map):
    groups = heads // hp
    return [pl.BlockSpec((GLA_ROWS, hp * GLA_DK), lambda h, b: (rows_map(b), h)),
            pl.BlockSpec((GLA_ROWS, hp * GLA_DK), lambda h, b: (rows_map(b), groups + h)),
            pl.BlockSpec((GLA_ROWS, hp * GLA_DV), lambda h, b: (rows_map(b), groups + h)),
            pl.BlockSpec((GLA_ROWS, hp * GLA_DK), lambda h, b: (rows_map(b), h))]


def _gla_fwd_call(hm, la, heads):
    S = hm.shape[0]
    assert S % GLA_ROWS == 0
    nb, cpb, hp = S // GLA_ROWS, GLA_ROWS // CHUNK, _gla_group(heads)

    def body(q_ref, k_ref, v_ref, la_ref, o_ref, sp_ref, st_ref):
        @pl.when(pl.program_id(1) == 0)
        def _():
            st_ref[...] = jnp.zeros(st_ref.shape, F32)

        for c in range(cpb):
            sl = slice(c * CHUNK, (c + 1) * CHUNK)
            for hh in range(hp):
                _, _, _, _, upd_t, decay, qs = _gla_chunk(q_ref, k_ref, v_ref, la_ref, sl, hh)
                state = st_ref[hh]
                sp_ref[hh, c] = state
                state = state * decay + upd_t
                st_ref[hh] = state
                o_ref[sl, hh * GLA_DV:(hh + 1) * GLA_DV] = lax.dot_general(
                    qs, state.astype(BF16), (((1,), (1,)), ((), ())), preferred_element_type=F32)

    return pl.pallas_call(
        body, name=_nm("gla_fwd"), grid=(heads // hp, nb),
        out_shape=(jax.ShapeDtypeStruct((S, heads * GLA_DV), F32),
                   jax.ShapeDtypeStruct((heads, S // CHUNK, GLA_DV, GLA_DK), F32)),
        in_specs=_gla_specs(heads, hp, lambda b: b),
        out_specs=(pl.BlockSpec((GLA_ROWS, hp * GLA_DV), lambda h, b: (b, h)),
                   pl.BlockSpec((hp, cpb, GLA_DV, GLA_DK), lambda h, b: (h, b, 0, 0))),
        scratch_shapes=[pltpu.VMEM((hp, GLA_DV, GLA_DK), F32)],
        compiler_params=_cp(("parallel", "arbitrary")),
    )(hm, hm, hm, la)


def _gla_bwd_call(hm, la, sprev, do, heads):
    S = hm.shape[0]
    nb, cpb, hp = S // GLA_ROWS, GLA_ROWS // CHUNK, _gla_group(heads)
    scale = GLA_DK ** -0.5

    def body(q_ref, k_ref, v_ref, la_ref, sp_ref, do_ref, dq_ref, dk_ref, dv_ref, dla_ref, carry_ref):
        @pl.when(pl.program_id(1) == 0)
        def _():
            carry_ref[...] = jnp.zeros(carry_ref.shape, F32)

        for c in reversed(range(cpb)):
            sl = slice(c * CHUNK, (c + 1) * CHUNK)
            for hh in range(hp):
                lk, lv = slice(hh * GLA_DK, (hh + 1) * GLA_DK), slice(hh * GLA_DV, (hh + 1) * GLA_DV)
                e, k, kdec, v, upd_t, decay, qs = _gla_chunk(q_ref, k_ref, v_ref, la_ref, sl, hh)
                sp = sp_ref[hh, c]
                s_n = sp * decay + upd_t
                dob = do_ref[sl, lv].astype(BF16)
                g = carry_ref[hh] + lax.dot_general(dob, qs, (((0,), (0,)), ((), ())), preferred_element_type=F32)
                gb = g.astype(BF16)
                dq_ref[sl, lk] = (jnp.dot(dob, s_n.astype(BF16), preferred_element_type=F32) * scale).astype(dq_ref.dtype)
                ddecay = jnp.sum(g * sp, axis=0, keepdims=True)
                dkdec = jnp.dot(v.astype(BF16), gb, preferred_element_type=F32)
                dv_ref[sl, lv] = lax.dot_general(kdec.astype(BF16), gb, (((1,), (1,)), ((), ())),
                                                 preferred_element_type=F32).astype(dv_ref.dtype)
                dk_ref[sl, lk] = (dkdec * e).astype(dk_ref.dtype)
                w = dkdec * k * e
                dtot = jnp.sum(w, axis=0, keepdims=True) + ddecay * decay
                last = lax.broadcasted_iota(jnp.int32, (CHUNK, 1), 0) == CHUNK - 1
                dcum = jnp.where(last, dtot - w, -w)
                dla_ref[sl, lk] = jnp.dot(_tri(False), dcum, preferred_element_type=F32, precision=lax.Precision.HIGHEST)
                carry_ref[hh] = g * decay

    rev = lambda b: nb - 1 - b
    narrow = pl.BlockSpec((GLA_ROWS, hp * GLA_DK), lambda h, b: (rev(b), h))
    wide = pl.BlockSpec((GLA_ROWS, hp * GLA_DV), lambda h, b: (rev(b), h))
    return pl.pallas_call(
        body, name=_nm("gla_bwd"), grid=(heads // hp, nb),
        out_shape=(jax.ShapeDtypeStruct((S, heads * GLA_DK), hm.dtype), jax.ShapeDtypeStruct((S, heads * GLA_DK), hm.dtype),
                   jax.ShapeDtypeStruct((S, heads * GLA_DV), hm.dtype), jax.ShapeDtypeStruct((S, heads * GLA_DK), F32)),
        in_specs=_gla_specs(heads, hp, rev) + [
            pl.BlockSpec((hp, cpb, GLA_DV, GLA_DK), lambda h, b: (h, rev(b), 0, 0)), wide],
        out_specs=(narrow, narrow, wide, narrow),
        scratch_shapes=[pltpu.VMEM((hp, GLA_DV, GLA_DK), F32)],
        compiler_params=_cp(("parallel", "arbitrary")),
    )(hm, hm, hm, la, sprev, do)


@functools.partial(jax.custom_vjp, nondiff_argnums=(3,))
def gla_core(hm, la, o_norm, heads):
    return _gla_core_fwd(hm, la, o_norm, heads)[0]


def _gla_core_fwd(hm, la, o_norm, heads):
    o, sprev = _gla_fwd_call(hm, la, heads)
    vd = heads * GLA_DV
    rows = [(o, vd, 0), (hm, vd, 2 * heads * GLA_DK // vd + 1)]
    (y,) = _rw_fwd(_gla_out_fn(heads), rows, [o_norm], [BF16], "gla_out_fwd")
    return y, (hm, la, o_norm, o, sprev)


def _gla_core_bwd(heads, res, dy):
    hm, la, o_norm, o, sprev = res
    vd = heads * GLA_DV
    rows = [(o, vd, 0), (hm, vd, 2 * heads * GLA_DK // vd + 1)]
    (do, dr), (dg,) = _rw_bwd(_gla_out_fn(heads), rows, [o_norm], [dy], [F32, hm.dtype], "gla_out_bwd")
    dq, dk, dv, dla = _gla_bwd_call(hm, la, sprev, do, heads)
    return jnp.concatenate([dq, dk, dv, dr], axis=1), dla, dg


gla_core.defvjp(_gla_core_fwd, _gla_core_bwd)


CONV_COLS = 256
HALO = 16


def _conv_rows(S):
    return min(512, S)


def _conv_taps(main_ref, halo_ref, i):
    prev = jnp.where(i > 0, halo_ref[...].astype(F32), 0.0)
    full = jnp.concatenate([prev, main_ref[...].astype(F32)], axis=0)
    return full[HALO:], pltpu.roll(full, 1, 0)[HALO:], pltpu.roll(full, 2, 0)[HALO:]


def _conv_apply(taps, w_ref, b_ref):
    x0, x1, x2 = taps
    return x2 * w_ref[0:1, :] + x1 * w_ref[1:2, :] + x0 * w_ref[2:3, :] + b_ref[...]


def _gelu_gate(uc, gc):
    return uc * jax.nn.gelu(gc)


def _conv_cols(dff, pref):
    return max(c for c in range(LANES, pref + 1, LANES) if dff % c == 0)


def _conv_in_specs(R, C, nj):
    hpr = R // HALO
    main = lambda off: pl.BlockSpec((R, C), lambda j, i: (i, j + off))
    halo = lambda off: pl.BlockSpec((HALO, C), lambda j, i: (jnp.maximum(i * hpr - 1, 0), j + off))
    par = lambda rows, off: pl.BlockSpec((rows, C), lambda j, i: (0, j + off))
    return [main(0), halo(0), main(nj), halo(nj), par(CONV_W, 0), par(CONV_W, nj), par(1, 0), par(1, nj)]


def _conv_fwd_call(h, cw, cb, comm=None):
    S, dff = h.shape[0], h.shape[1] // 2
    R, C = _conv_rows(S), _conv_cols(dff, 768)
    nj = dff // C

    def body(u_ref, uh_ref, g_ref, gh_ref, wu_ref, wg_ref, bu_ref, bg_ref, a_ref):
        i = pl.program_id(1)
        uc = _conv_apply(_conv_taps(u_ref, uh_ref, i), wu_ref, bu_ref)
        gc = _conv_apply(_conv_taps(g_ref, gh_ref, i), wg_ref, bg_ref)
        a_ref[...] = _gelu_gate(uc, gc).astype(a_ref.dtype)

    return _pcall(
        body, "conv_fwd", (nj, S // R), 0, in_specs=_conv_in_specs(R, C, nj),
        out_specs=[pl.BlockSpec((R, C), lambda j, i: (i, j))], out_shape=[jax.ShapeDtypeStruct((S, dff), BF16)],
        scratch=[], sem=("parallel", "parallel"), args=[h, h, h, h, cw, cw, cb, cb], comm=comm)


def _conv_bwd_gate_call(h, cw, cb, da, comm=None):
    S, dff = h.shape[0], h.shape[1] // 2
    R, C = _conv_rows(S), _conv_cols(dff, 512)
    nj = dff // C

    def body(u_ref, uh_ref, g_ref, gh_ref, wu_ref, wg_ref, bu_ref, bg_ref, da_ref,
             du_ref, dg_ref, dwu_ref, dwg_ref, dbu_ref, dbg_ref):
        i = pl.program_id(1)
        ut, gt = _conv_taps(u_ref, uh_ref, i), _conv_taps(g_ref, gh_ref, i)
        uc, gc = _conv_apply(ut, wu_ref, bu_ref), _conv_apply(gt, wg_ref, bg_ref)
        _, vjp_fn = jax.vjp(_gelu_gate, uc, gc)
        du, dg = vjp_fn(da_ref[...].astype(F32))
        du_ref[...] = du.astype(du_ref.dtype)
        dg_ref[...] = dg.astype(dg_ref.dtype)

        @pl.when(i == 0)
        def _():
            for r in (dwu_ref, dwg_ref, dbu_ref, dbg_ref):
                r[...] = jnp.zeros(r.shape, F32)

        for d, taps, dw_ref, db_ref in ((du, ut, dwu_ref, dbu_ref), (dg, gt, dwg_ref, dbg_ref)):
            x0, x1, x2 = taps
            dw_ref[0:1, :] += jnp.sum(d * x2, axis=0, keepdims=True)
            dw_ref[1:2, :] += jnp.sum(d * x1, axis=0, keepdims=True)
            dw_ref[2:3, :] += jnp.sum(d * x0, axis=0, keepdims=True)
            db_ref[...] += jnp.sum(d, axis=0, keepdims=True)

    tile = pl.BlockSpec((R, C), lambda j, i: (i, j))
    par = lambda rows: pl.BlockSpec((rows, C), lambda j, i: (0, j))
    return _pcall(
        body, "conv_bwd_gate", (nj, S // R), 0, in_specs=_conv_in_specs(R, C, nj) + [tile],
        out_specs=[tile, tile, par(CONV_W), par(CONV_W), par(1), par(1)],
        out_shape=[jax.ShapeDtypeStruct((S, dff), BF16), jax.ShapeDtypeStruct((S, dff), BF16),
                   jax.ShapeDtypeStruct((CONV_W, dff), F32), jax.ShapeDtypeStruct((CONV_W, dff), F32),
                   jax.ShapeDtypeStruct((1, dff), F32), jax.ShapeDtypeStruct((1, dff), F32)],
        scratch=[], sem=("parallel", "arbitrary"), args=[h, h, h, h, cw, cw, cb, cb, da], comm=comm)


def _conv_bwd_shift_call(dc, cw, into=None):
    S, dff = dc.shape
    R, C = min(1024, S), _conv_cols(dff, 1024)
    nj = dff // C
    col_off = 0 if into is None else nj
    hpr, last = R // HALO, S // HALO - 1
    ni = S // R

    def body(d_ref, nx_ref, w_ref, *rest):
        o_ref = rest[-1]
        i = pl.program_id(1)
        nxt = jnp.where(i < ni - 1, nx_ref[...].astype(F32), 0.0)
        full = jnp.concatenate([d_ref[...].astype(F32), nxt], axis=0)
        n = R + HALO
        y1, y2 = pltpu.roll(full, n - 1, 0)[:R], pltpu.roll(full, n - 2, 0)[:R]
        o_ref[...] = (full[:R] * w_ref[2:3, :] + y1 * w_ref[1:2, :] + y2 * w_ref[0:1, :]).astype(o_ref.dtype)

    in_specs = [pl.BlockSpec((R, C), lambda j, i: (i, j)),
                pl.BlockSpec((HALO, C), lambda j, i: (jnp.minimum((i + 1) * hpr, last), j)),
                pl.BlockSpec((CONV_W, C), lambda j, i: (0, j + col_off))]
    args = [dc, dc, cw]
    if into is not None:
        in_specs.append(pl.BlockSpec(memory_space=pl.ANY))
        args.append(into)
    return pl.pallas_call(
        body, name=_nm("conv_bwd_shift"), grid=(nj, ni), out_shape=jax.ShapeDtypeStruct((S, 2 * dff), BF16),
        in_specs=in_specs, out_specs=pl.BlockSpec((R, C), lambda j, i: (i, j + col_off)),
        input_output_aliases={} if into is None else {3: 0},
        compiler_params=_cp(("parallel", "parallel")),
    )(*args)


@jax.custom_vjp
def ffn_hidden(x1, x1_bf16, w3, cw, cb):
    return _ffn_hidden_fwd(x1, x1_bf16, w3, cw, cb)[0]


def _ffn_hidden_fwd(x1, x1_bf16, w3, cw, cb, comm=None):
    h = _mm(x1_bf16, w3, "nn", BF16, "up", _all_blocks(w3))
    a, *carried = _conv_fwd_call(h, cw, cb, comm)
    return (a, (x1_bf16, w3, cw, cb, h)) + ((carried,) if comm is not None else ())


def _ffn_hidden_bwd(res, da, comm=None):
    x1, w3, cw, cb, h = res
    du, dg, dwu, dwg, dbu, dbg, *carried = _conv_bwd_gate_call(h, cw, cb, da, comm)
    dh = _conv_bwd_shift_call(dg, cw, into=_conv_bwd_shift_call(du, cw))
    dx = _mm(dh, w3, "nt", F32, "up_dx", _all_blocks(w3))
    dw3 = _mm(x1, dh, "tn", w3.dtype, "up_dw", _all_blocks(w3))
    grads = (dx, jnp.zeros_like(x1), dw3, jnp.concatenate([dwu, dwg], axis=1), jnp.concatenate([dbu, dbg], axis=1))
    return (grads, carried) if comm is not None else grads


ffn_hidden.defvjp(_ffn_hidden_fwd, _ffn_hidden_bwd)


def _loss_call(y, target):
    S, D = y.shape
    tr = _row_tile(S, D)

    def body(y_ref, t_ref, sq_ref, dy_ref):
        diff = y_ref[...] - t_ref[...]
        dy_ref[...] = diff * (1.0 / D)
        part = jnp.sum(diff * diff, axis=0, keepdims=True)
        i = pl.program_id(0)

        @pl.when(i == 0)
        def _():
            sq_ref[...] = part

        @pl.when(i > 0)
        def _():
            sq_ref[...] += part

    row = pl.BlockSpec((tr, D), lambda i: (i, 0))
    return pl.pallas_call(
        body, name=_nm("loss"), grid=(S // tr,),
        out_shape=(jax.ShapeDtypeStruct((1, D), F32), jax.ShapeDtypeStruct((S, D), F32)),
        in_specs=[row, row], out_specs=(pl.BlockSpec((1, D), lambda i: (0, 0)), row),
        compiler_params=_cp(("arbitrary",)),
    )(y, target)


def _row_divisor(rows):
    for cand in range(min(rows, 512), 15, -1):
        if rows % cand == 0 and cand % 16 == 0:
            return cand
    return rows


def _adamw_call(w, g, m, v):
    shape = w.shape
    w2, g2, m2, v2 = (a.reshape(-1, shape[-1]) for a in (w, g, m, v))
    rows, cols = w2.shape
    tr = _row_divisor(rows)

    def body(w_ref, g_ref, m_ref, v_ref, d_ref, nm_ref, nv_ref):
        g_ = g_ref[...]
        m_ = ADAM_B1 * m_ref[...] + (1.0 - ADAM_B1) * g_
        v_ = ADAM_B2 * v_ref[...] + (1.0 - ADAM_B2) * (g_ * g_)
        m_hat = m_ / (1.0 - ADAM_B1 ** ADAM_STEP)
        v_hat = v_ / (1.0 - ADAM_B2 ** ADAM_STEP)
        d_ref[...] = -ADAM_LR * (m_hat / (jnp.sqrt(v_hat) + ADAM_EPS) + ADAM_WD * w_ref[...])
        nm_ref[...] = m_
        nv_ref[...] = v_

    blk = pl.BlockSpec((tr, cols), lambda i: (i, 0))
    outs = pl.pallas_call(
        body, name=_nm("adamw"), grid=(rows // tr,),
        out_shape=tuple(jax.ShapeDtypeStruct((rows, cols), F32) for _ in range(3)),
        in_specs=[blk] * 4, out_specs=(blk,) * 3, compiler_params=_cp(("parallel",)),
    )(w2, g2, m2, v2)
    return tuple(o.reshape(shape) for o in outs)


ANY = pl.BlockSpec(memory_space=pl.ANY)


def _place():
    return lax.axis_index("x"), lax.axis_index("y"), lax.axis_index("c")


def all_gather(shards):
    n = len(shards)

    def body(*refs):
        x_refs, out_refs = refs[:n], refs[n:2 * n]
        send_sems, recv_sems, local_sems = refs[2 * n:]
        x, y, c = _place()
        me, sibling = (x, y, c), (x, y, 1 - c)
        chips = [(1 - x, y), (x, 1 - y), (1 - x, 1 - y)]

        def slot(a, px, py, pc):
            return out_refs[a].at[:, 4 * px + 2 * py + pc]

        def copy(a, k, block, to, own=False):
            return pltpu.make_async_remote_copy(
                src_ref=x_refs[a] if own else slot(a, *block), dst_ref=slot(a, *block),
                send_sem=send_sems.at[7 * a + k], recv_sem=recv_sems.at[7 * a + k], device_id=to, device_id_type=MESH)

        mine = [pltpu.make_async_copy(x_refs[a], slot(a, *me), local_sems.at[a]) for a in range(n)]
        first = []
        for a in range(n):
            mine[a].start()
            first.append(copy(a, 0, me, sibling, own=True))
            first += [copy(a, 1 + j, me, (*chip, c), own=True) for j, chip in enumerate(chips)]
        for cp in first:
            cp.start()
        passed = []
        for j, chip in enumerate(chips):
            for a in range(n):
                copy(a, 1 + j, (*chip, c), me).wait_recv()
                passed.append(copy(a, 4 + j, (*chip, c), sibling))
                passed[-1].start()
        for a in range(n):
            copy(a, 0, sibling, me).wait_recv()
        for j, chip in enumerate(chips):
            for a in range(n):
                copy(a, 4 + j, (*chip, 1 - c), me).wait_recv()
        for cp in first + passed:
            cp.wait_send()
        for cp in mine:
            cp.wait()

    return pl.pallas_call(
        body, name=_nm("all_gather"),
        out_shape=tuple(jax.ShapeDtypeStruct((s.shape[0], N_DEV) + s.shape[1:], s.dtype) for s in shards),
        in_specs=[ANY] * n, out_specs=(ANY,) * n,
        scratch_shapes=[pltpu.SemaphoreType.DMA((7 * n,)), pltpu.SemaphoreType.DMA((7 * n,)), pltpu.SemaphoreType.DMA((n,))],
    )(*shards)


def _rs_pair_exchange(gs):
    n = len(gs)

    def body(*refs):
        g_refs, recv_refs = refs[:n], refs[n:2 * n]
        send_sems, recv_sems = refs[2 * n:]
        x, y, c = _place()
        copies = [pltpu.make_async_remote_copy(
            src_ref=g_refs[a].at[:, 2 * j + (1 - c)], dst_ref=recv_refs[a].at[j], send_sem=send_sems.at[4 * a + j],
            recv_sem=recv_sems.at[4 * a + j], device_id=(x, y, 1 - c), device_id_type=MESH)
            for a in range(n) for j in range(4)]
        for cp in copies:
            cp.start()
        for cp in copies:
            cp.wait_recv()
        for cp in copies:
            cp.wait_send()

    return pl.pallas_call(
        body, name=_nm("rs_pair"),
        out_shape=tuple(jax.ShapeDtypeStruct((4, g.shape[0]) + g.shape[2:], g.dtype) for g in gs),
        in_specs=[ANY] * n, out_specs=(ANY,) * n,
        scratch_shapes=[pltpu.SemaphoreType.DMA((4 * n,)), pltpu.SemaphoreType.DMA((4 * n,))],
    )(*gs)


def _rs_chip_exchange(ps):
    n = len(ps)

    def body(*refs):
        p_refs, recv_refs = refs[:n], refs[n:2 * n]
        send_sems, recv_sems = refs[2 * n:]
        x, y, c = _place()
        chips = [(1 - x, y), (x, 1 - y), (1 - x, 1 - y)]
        copies = [pltpu.make_async_remote_copy(
            src_ref=p_refs[a].at[2 * cx + cy], dst_ref=recv_refs[a].at[k], send_sem=send_sems.at[3 * a + k],
            recv_sem=recv_sems.at[3 * a + k], device_id=(cx, cy, c), device_id_type=MESH)
            for a in range(n) for k, (cx, cy) in enumerate(chips)]
        for cp in copies:
            cp.start()
        for cp in copies:
            cp.wait_recv()
        for cp in copies:
            cp.wait_send()

    return pl.pallas_call(
        body, name=_nm("rs_chip"),
        out_shape=tuple(jax.ShapeDtypeStruct((3,) + p.shape[1:], p.dtype) for p in ps),
        in_specs=[ANY] * n, out_specs=(ANY,) * n,
        scratch_shapes=[pltpu.SemaphoreType.DMA((3 * n,)), pltpu.SemaphoreType.DMA((3 * n,))],
    )(*ps)


def _rs_pair_add(g, recv, c_idx):
    L, _, a, b = g.shape
    ta = _row_divisor(a)

    def body(c_ref, g_ref, r_ref, o_ref):
        o_ref[...] = (g_ref[...].astype(F32) + r_ref[...].astype(F32)).astype(o_ref.dtype)

    grid_spec = pltpu.PrefetchScalarGridSpec(
        num_scalar_prefetch=1, grid=(4, L, a // ta),
        in_specs=[pl.BlockSpec((None, None, ta, b), lambda j, l, i, c_ref: (l, 2 * j + c_ref[0], i, 0)),
                  pl.BlockSpec((None, None, ta, b), lambda j, l, i, c_ref: (j, l, i, 0))],
        out_specs=pl.BlockSpec((None, None, ta, b), lambda j, l, i, c_ref: (j, l, i, 0)))
    return pl.pallas_call(
        body, name=_nm("rs_pair_add"), grid_spec=grid_spec, out_shape=jax.ShapeDtypeStruct((4, L, a, b), g.dtype),
        compiler_params=_cp(("parallel", "parallel", "parallel")),
    )(c_idx, g, recv)


def _rs_final_add(p1, recv, chip_idx):
    _, L, a, b = p1.shape
    ta = _row_divisor(a)

    def body(chip_ref, p_ref, r_ref, o_ref):
        acc = p_ref[...].astype(F32)
        for k in range(3):
            acc = acc + r_ref[k].astype(F32)
        o_ref[...] = acc

    grid_spec = pltpu.PrefetchScalarGridSpec(
        num_scalar_prefetch=1, grid=(L, a // ta),
        in_specs=[pl.BlockSpec((None, None, ta, b), lambda l, i, chip_ref: (chip_ref[0], l, i, 0)),
                  pl.BlockSpec((3, None, ta, b), lambda l, i, chip_ref: (0, l, i, 0))],
        out_specs=pl.BlockSpec((None, ta, b), lambda l, i, chip_ref: (l, i, 0)))
    return pl.pallas_call(
        body, name=_nm("rs_final_add"), grid_spec=grid_spec, out_shape=jax.ShapeDtypeStruct((L, a, b), F32),
        compiler_params=_cp(("parallel", "parallel")),
    )(chip_idx, p1, recv)


def reduce_scatter(gs):
    x, y, c = _place()
    c_idx = jnp.reshape(c, (1,)).astype(jnp.int32)
    chip_idx = jnp.reshape(2 * x + y, (1,)).astype(jnp.int32)
    recv1 = _rs_pair_exchange(gs)
    p1 = [_rs_pair_add(g, r, c_idx) for g, r in zip(gs, recv1)]
    recv2 = _rs_chip_exchange(p1)
    return [_rs_final_add(p, r, chip_idx) for p, r in zip(p1, recv2)]


def all_reduce_small(v):
    r, C = v.shape

    def body(v_ref, out_ref, buf_ref, send_sems, recv_sems):
        x, y, c = _place()
        my_id = 4 * x + 2 * y + c
        buf_ref[my_id] = v_ref[...]
        copies = []
        for k in range(1, N_DEV):
            fx, fy, fc = (k >> 2) & 1, (k >> 1) & 1, k & 1
            peer = (x ^ fx, y ^ fy, c ^ fc)
            copies.append(pltpu.make_async_remote_copy(
                src_ref=v_ref, dst_ref=buf_ref.at[my_id], send_sem=send_sems.at[k - 1], recv_sem=recv_sems.at[k - 1],
                device_id=peer, device_id_type=MESH))
        for cp in copies:
            cp.start()
        for cp in copies:
            cp.wait_recv()
        for cp in copies:
            cp.wait_send()
        acc = buf_ref[0]
        for d in range(1, N_DEV):
            acc = acc + buf_ref[d]
        out_ref[...] = acc

    vm = pl.BlockSpec(memory_space=pltpu.VMEM)
    return pl.pallas_call(
        body, name=_nm("all_reduce_small"), out_shape=jax.ShapeDtypeStruct((r, C), F32),
        in_specs=[vm], out_specs=vm,
        scratch_shapes=[pltpu.VMEM((N_DEV, r, C), F32), pltpu.SemaphoreType.DMA((7,)), pltpu.SemaphoreType.DMA((7,))],
    )(v)


def _slot(ref, px, py, pc):
    return ref.at[:, 4 * px + 2 * py + pc]


def comm_gather_own(shards):
    n = len(shards)

    def build(cin, cout, send_sems, recv_sems, local_sems):
        x, y, c = _place()
        me = (x, y, c)
        direct = [(x, y, 1 - c), (1 - x, y, c), (x, 1 - y, c)]
        src_nb, dst_nb, diag = (x ^ (1 - c), y ^ c, c), (x ^ c, y ^ (1 - c), c), (1 - x, 1 - y, c)
        starts, mid_waits, mid_starts, waits = [], [], [], []
        for a in range(n):
            local = pltpu.make_async_copy(cin[a], _slot(cout[a], *me), local_sems.at[a])
            starts.append(local)
            waits.append(local.wait)
            for k, peer in enumerate(direct):
                send = pltpu.make_async_remote_copy(
                    src_ref=cin[a], dst_ref=_slot(cout[a], *me), send_sem=send_sems.at[4 * a + k],
                    recv_sem=recv_sems.at[4 * a + k], device_id=peer, device_id_type=MESH)
                arrive = pltpu.make_async_remote_copy(
                    src_ref=cin[a], dst_ref=_slot(cout[a], *peer), send_sem=send_sems.at[4 * a + k],
                    recv_sem=recv_sems.at[4 * a + k], device_id=peer, device_id_type=MESH)
                starts.append(send)
                (waits if k == 0 else mid_waits).append(arrive.wait_recv)
                waits.append(send.wait_send)
            onward = pltpu.make_async_remote_copy(
                src_ref=_slot(cout[a], *src_nb), dst_ref=_slot(cout[a], *src_nb), send_sem=send_sems.at[4 * a + 3],
                recv_sem=recv_sems.at[4 * a + 3], device_id=dst_nb, device_id_type=MESH)
            arrive = pltpu.make_async_remote_copy(
                src_ref=_slot(cout[a], *src_nb), dst_ref=_slot(cout[a], *diag), send_sem=send_sems.at[4 * a + 3],
                recv_sem=recv_sems.at[4 * a + 3], device_id=dst_nb, device_id_type=MESH)
            mid_starts.append(onward)
            waits += [arrive.wait_recv, onward.wait_send]
        return starts, mid_waits, mid_starts, waits

    out_shapes = [jax.ShapeDtypeStruct((s.shape[0], N_DEV) + s.shape[1:], s.dtype) for s in shards]
    return Comm(shards, out_shapes, {}, 4 * n, 4 * n, n, build)


def comm_gather_pass(partial):
    n = len(partial)

    def build(cin, cout, send_sems, recv_sems, local_sems):
        x, y, c = _place()
        chips = [(1 - x, y), (x, 1 - y), (1 - x, 1 - y)]
        starts, waits = [], []
        for a in range(n):
            for j, chip in enumerate(chips):
                send = pltpu.make_async_remote_copy(
                    src_ref=_slot(cout[a], *chip, c), dst_ref=_slot(cout[a], *chip, c), send_sem=send_sems.at[3 * a + j],
                    recv_sem=recv_sems.at[3 * a + j], device_id=(x, y, 1 - c), device_id_type=MESH)
                arrive = pltpu.make_async_remote_copy(
                    src_ref=_slot(cout[a], *chip, c), dst_ref=_slot(cout[a], *chip, 1 - c),
                    send_sem=send_sems.at[3 * a + j], recv_sem=recv_sems.at[3 * a + j],
                    device_id=(x, y, 1 - c), device_id_type=MESH)
                starts.append(send)
                waits += [arrive.wait_recv, send.wait_send]
        return starts, waits

    out_shapes = [jax.ShapeDtypeStruct(p.shape, p.dtype) for p in partial]
    return Comm(partial, out_shapes, {a: a for a in range(n)}, 3 * n, 3 * n, 1, build)


def comm_rs_pair(gs):
    n = len(gs)

    def build(cin, cout, send_sems, recv_sems, local_sems):
        x, y, c = _place()
        starts, waits = [], []
        for a in range(n):
            for j in range(4):
                cp = pltpu.make_async_remote_copy(
                    src_ref=cin[a].at[:, 2 * j + (1 - c)], dst_ref=cout[a].at[j], send_sem=send_sems.at[4 * a + j],
                    recv_sem=recv_sems.at[4 * a + j], device_id=(x, y, 1 - c), device_id_type=MESH)
                starts.append(cp)
                waits += [cp.wait_recv, cp.wait_send]
        return starts, waits

    out_shapes = [jax.ShapeDtypeStruct((4, g.shape[0]) + g.shape[2:], g.dtype) for g in gs]
    return Comm(gs, out_shapes, {}, 4 * n, 4 * n, 1, build)


def comm_rs_chip(ps):
    n = len(ps)

    def build(cin, cout, send_sems, recv_sems, local_sems):
        x, y, c = _place()
        chips = [(1 - x, y), (x, 1 - y), (1 - x, 1 - y)]
        starts, waits = [], []
        for a in range(n):
            for k, (cx, cy) in enumerate(chips):
                cp = pltpu.make_async_remote_copy(
                    src_ref=cin[a].at[2 * cx + cy], dst_ref=cout[a].at[k], send_sem=send_sems.at[3 * a + k],
                    recv_sem=recv_sems.at[3 * a + k], device_id=(cx, cy, c), device_id_type=MESH)
                starts.append(cp)
                waits += [cp.wait_recv, cp.wait_send]
        return starts, waits

    out_shapes = [jax.ShapeDtypeStruct((3,) + p.shape[1:], p.dtype) for p in ps]
    return Comm(ps, out_shapes, {}, 3 * n, 3 * n, 1, build)


def _pack(arrays, dtype, row_align):
    lead = arrays[0].shape[:-1]
    quantum = row_align * PACK_COLS
    parts, sizes = [], []
    for a in arrays:
        n = a.shape[-1]
        padded = _round_up(n, quantum)
        a = a.astype(dtype)
        if padded != n:
            a = jnp.pad(a, [(0, 0)] * len(lead) + [(0, padded - n)])
        parts.append(a.reshape(*lead, padded // PACK_COLS, PACK_COLS))
        sizes.append((n, padded // PACK_COLS))
    return jnp.concatenate(parts, axis=len(lead)), sizes


def _unpack(packed, sizes):
    lead = packed.shape[:-2]
    out, row = [], 0
    for n, rows in sizes:
        part = lax.slice_in_dim(packed, row, row + rows, axis=len(lead))
        out.append(part.reshape(*lead, rows * PACK_COLS)[..., :n])
        row += rows
    return out


def _unshard(gathered, axis):
    _, L, a, b = gathered.shape
    if axis == 1:
        return [gathered[:, l].reshape(N_DEV * a, b) for l in range(L)]
    return [jnp.transpose(gathered[:, l], (1, 0, 2)).reshape(a, N_DEV * b) for l in range(L)]


def _reshard(fulls, axis):
    blocks = []
    for f in fulls:
        A, B = f.shape
        if axis == 1:
            blocks.append(f.reshape(N_DEV, A // N_DEV, B))
        else:
            blocks.append(jnp.transpose(f.reshape(A, N_DEV, B // N_DEV), (1, 0, 2)))
    return jnp.stack(blocks, axis=1)


def _as3(a):
    return a if a.ndim == 3 else a[:, None, :]


def _prep_big(name, w, dims):
    w = w.astype(BF16)
    L, a, b = w.shape
    if name == 'mla_w_in':
        return jnp.pad(w, ((0, 0), (0, 0), (0, dims['h_width'] - b)))
    if name == 'mla_w_uq':
        hd = MLA_NOPE + MLA_ROPE
        w = jnp.pad(w.reshape(L, a, b // hd, hd), ((0, 0), (0, 0), (0, 0), (0, MLA_QK_PAD - hd)))
        return w.reshape(L, a, b // hd * MLA_QK_PAD)
    if name == 'ffn_w_up':
        return jnp.pad(w, ((0, 0), (0, 0), (0, _round_up(b, CONV_COLS) - b)))
    return w


def _unprep_big(name, g, shape):
    L, a, b = shape
    if name == 'mla_w_uq':
        hd = MLA_NOPE + MLA_ROPE
        return g.reshape(L, a, b // hd, MLA_QK_PAD)[..., :hd].reshape(L, a, b)
    return g[:, :, :b]


def _rope_tables(positions):
    inv = 1.0 / (ROPE_THETA ** (jnp.arange(0, MLA_ROPE, 2, dtype=F32) / MLA_ROPE))
    ang = positions.astype(F32)[:, None] * inv
    cos, sin = jnp.cos(ang), jnp.sin(ang)
    one, zero = jnp.ones_like(cos), jnp.zeros_like(cos)
    a = jnp.concatenate([cos, cos, one, one], axis=1)
    up = jnp.concatenate([sin, zero, zero, zero], axis=1)
    down = jnp.concatenate([zero, sin, zero, zero], axis=1)
    return (a, -up, down), (a, up, -down)


def _rows_full(w):
    return w.reshape(w.shape[0] * w.shape[1], w.shape[2])


def _ops(depth):
    alpha = (2 * depth) ** 0.25
    return {'ln_res': rw_op(_ln_res_fn(alpha), "ln_res", 2, [F32], shadow=True),
            'ple': rw_op(_ple_fn, "ple", 3, [F32], shadow=True), 'gla_gate': rw_op(_gla_gate_fn, "gla_gate", 1, [F32])}


MLA_PRE_W = ['mla_w_in', 'mla_q_norm', 'mla_kv_norm', 'mla_w_uq', 'mla_w_uk', 'mla_w_uv']
FFN_IN_W = ['mla_w_o', 'ln1_g', 'ln1_b', 'ffn_conv_w', 'ffn_conv_b']
FFN_OUT_W = ['ffn_w_down', 'ln2_g', 'ln2_b', 'ple_w_gate', 'ple_w_proj', 'ple_b_gate']


def _mla_heads(wl, j):
    return N_DEV * wl['mla_w_uk'][j].shape[2] // MLA_NOPE


def _mla_pre(x, xb, wl, j):
    w_uq, w_uk, w_uv = wl['mla_w_uq'][j], wl['mla_w_uk'][j], wl['mla_w_uv'][j]
    h = linear_sh(x, xb, _rows_full(wl['mla_w_in'][j]), F32)
    mla_norm = rw_op(_mla_norm_fn(w_uq.shape[1], w_uk.shape[1]), "mla_norm", 1, [BF16, BF16, F32])
    cq, ckv, kr_raw = mla_norm(h, wl['mla_q_norm'][j], wl['mla_kv_norm'][j])
    return linear(cq, w_uq, BF16), linear(ckv, w_uk, BF16), linear(ckv, w_uv, BF16), kr_raw


def _gla_mixer(x, xb, wl, j, ops):
    w_in3, w_a2 = wl['gla_w_in'][j], wl['gla_w_a2'][j]
    w_o = _rows_full(wl['gla_w_o'][j])
    w_in = jnp.transpose(w_in3, (1, 0, 2)).reshape(w_in3.shape[1], N_DEV * w_in3.shape[2])
    heads = w_o.shape[0] // GLA_DV
    n_main = 2 * heads * GLA_DK + 2 * heads * GLA_DV
    w_a = jnp.pad(w_in[:, n_main:], ((0, 0), (0, LANES - GLA_RANK)))
    w_a2_p = jnp.pad(w_a2, ((0, LANES - GLA_RANK), (0, 0))).astype(BF16)
    hm = linear_sh(x, xb, w_in[:, :n_main], BF16)
    ha = linear_sh(x, xb, w_a, BF16)
    (la,) = ops['gla_gate'](linear(ha, w_a2_p, F32), wl['gla_b_a'][j])
    return linear(gla_core(hm, la, wl['gla_o_norm'][j], heads), w_o, F32)


def _ffn_in(x, m, wl, i, ops, bp):
    x1, x1b = ops['ln_res'](x, m, wl['ln1_g'][i], wl['ln1_b'][i])
    cw, cb = wl['ffn_conv_w'][i], wl['ffn_conv_b'][i]
    bu = cw.shape[1] // N_DEV
    cwp = jnp.pad(cw.reshape(CONV_W, N_DEV, bu), ((0, 0), (0, 0), (0, bp - bu))).reshape(CONV_W, N_DEV * bp)
    cbp = jnp.pad(cb.reshape(1, N_DEV, bu), ((0, 0), (0, 0), (0, bp - bu))).reshape(1, N_DEV * bp)
    return x1, lax.stop_gradient(x1b), cwp, cbp


def _ffn_out(x1, a, wl, p_i, i, ops):
    w_down3 = wl['ffn_w_down'][i]
    half, bu, d_model = N_DEV // 2, 2 * w_down3.shape[1], w_down3.shape[2]
    bp = a.shape[1] // half
    w_down = jnp.pad(w_down3.reshape(half, bu, d_model), ((0, 0), (0, bp - bu), (0, 0))).reshape(half * bp, d_model)
    f = linear(a, w_down, F32)
    x2, x2b = ops['ln_res'](x1, f, wl['ln2_g'][i], wl['ln2_b'][i])
    glog = linear_sh(x2, lax.stop_gradient(x2b), _rows_full(wl['ple_w_gate'][i]), F32)
    pp = linear(p_i, wl['ple_w_proj'][i], F32)
    x, xb = ops['ple'](x2, glog, pp, wl['ple_b_gate'][i])
    return x, lax.stop_gradient(xb)


def _layer(x, xb, wl, p_i, aux, i, ops):
    j = i // 2
    if i % 2 == 0:
        q_raw, kn, v, kr_raw = _mla_pre(x, xb, wl, j)
        o = make_attention(aux, _mla_heads(wl, j))[0](q_raw, kn, v, kr_raw)
        m = linear(o, _rows_full(wl['mla_w_o'][j]), F32)
    else:
        m = _gla_mixer(x, xb, wl, j, ops)
    x1, x1b, cwp, cbp = _ffn_in(x, m, wl, i, ops, wl['ffn_w_up'][i].shape[2])
    return _ffn_out(x1, ffn_hidden(x1, x1b, wl['ffn_w_up'][i], cwp, cbp), wl, p_i, i, ops)


def kernel(x, p, positions, mla_w_in, mla_q_norm, mla_kv_norm, mla_w_uq, mla_w_uk, mla_w_uv, mla_w_o, gla_w_in, gla_w_a2, gla_b_a, gla_o_norm, gla_w_o, ln1_g, ln1_b, ln2_g, ln2_b, ffn_w_up, ffn_conv_w, ffn_conv_b, ffn_w_down, ple_w_proj, ple_w_gate, ple_b_gate, loss_target, m_mla_w_in, m_mla_q_norm, m_mla_kv_norm, m_mla_w_uq, m_mla_w_uk, m_mla_w_uv, m_mla_w_o, m_gla_w_in, m_gla_w_a2, m_gla_b_a, m_gla_o_norm, m_gla_w_o, m_ln1_g, m_ln1_b, m_ln2_g, m_ln2_b, m_ffn_w_up, m_ffn_conv_w, m_ffn_conv_b, m_ffn_w_down, m_ple_w_proj, m_ple_w_gate, m_ple_b_gate, v_mla_w_in, v_mla_q_norm, v_mla_kv_norm, v_mla_w_uq, v_mla_w_uk, v_mla_w_uv, v_mla_w_o, v_gla_w_in, v_gla_w_a2, v_gla_b_a, v_gla_o_norm, v_gla_w_o, v_ln1_g, v_ln1_b, v_ln2_g, v_ln2_b, v_ffn_w_up, v_ffn_conv_w, v_ffn_conv_b, v_ffn_w_down, v_ple_w_proj, v_ple_w_gate, v_ple_b_gate):
    w = dict(zip(WEIGHTS, (mla_w_in, mla_q_norm, mla_kv_norm, mla_w_uq, mla_w_uk, mla_w_uv, mla_w_o, gla_w_in, gla_w_a2,
                           gla_b_a, gla_o_norm, gla_w_o, ln1_g, ln1_b, ln2_g, ln2_b, ffn_w_up, ffn_conv_w, ffn_conv_b,
                           ffn_w_down, ple_w_proj, ple_w_gate, ple_b_gate)))
    m_in = dict(zip(WEIGHTS, (m_mla_w_in, m_mla_q_norm, m_mla_kv_norm, m_mla_w_uq, m_mla_w_uk, m_mla_w_uv, m_mla_w_o,
                              m_gla_w_in, m_gla_w_a2, m_gla_b_a, m_gla_o_norm, m_gla_w_o, m_ln1_g, m_ln1_b, m_ln2_g,
                              m_ln2_b, m_ffn_w_up, m_ffn_conv_w, m_ffn_conv_b, m_ffn_w_down, m_ple_w_proj, m_ple_w_gate,
                              m_ple_b_gate)))
    v_in = dict(zip(WEIGHTS, (v_mla_w_in, v_mla_q_norm, v_mla_kv_norm, v_mla_w_uq, v_mla_w_uk, v_mla_w_uv, v_mla_w_o,
                              v_gla_w_in, v_gla_w_a2, v_gla_b_a, v_gla_o_norm, v_gla_w_o, v_ln1_g, v_ln1_b, v_ln2_g,
                              v_ln2_b, v_ffn_w_up, v_ffn_conv_w, v_ffn_conv_b, v_ffn_w_down, v_ple_w_proj, v_ple_w_gate,
                              v_ple_b_gate)))
    _uid[0] = itertools.count()
    x2d, target, pos = x[0], loss_target[0], positions[0]
    p3 = p[:, 0]
    dims = {'h_width': mla_w_uq.shape[1] + mla_w_uk.shape[1] + LANES}

    depth = ln1_g.shape[0]
    ops = _ops(depth)
    cid = pos // CHUNK
    aux = {'rope': _rope_tables(pos), 'cidq': cid[:, None], 'cidk': cid[None, :],
           'need': _mask_table(cid, _attn_tile(pos.shape[0]))}

    in_layer0 = set(MLA_PRE_W + FFN_IN_W + FFN_OUT_W + ['ffn_w_up'])
    prepped = {n: _prep_big(n, w[n], dims) for n in BIG}
    first_names = [n for n in BIG if n in MLA_PRE_W + ['mla_w_o']]
    mid_names = [n for n in BIG if n in in_layer0 and n not in first_names]
    rest_names = [n for n in BIG if prepped[n].shape[0] > (1 if n in in_layer0 else 0)]
    rest_from = {n: (1 if n in in_layer0 else 0) for n in rest_names}
    small3 = [_as3(w[n]) for n in SMALL]
    small_packed, small_sizes = _pack([s.reshape(1, -1) for s in small3], F32, 8)
    first = all_gather([prepped[n][:1] for n in first_names] + [small_packed])
    wl = {n: [None] * prepped[n].shape[0] for n in BIG}
    for n, g in zip(first_names, first):
        wl[n][0] = g[0]
    for n, s3, flat in zip(SMALL, small3, _unpack(first[-1][0], small_sizes)):
        wl[n] = _unshard(flat.reshape(N_DEV, *s3.shape), SHARD_AXIS[n] if w[n].ndim == 3 else 2)
    for n in REPL:
        wl[n] = [w[n][l][None, :] for l in range(w[n].shape[0])]

    def pick(names, layer0):
        return {n: [wl[n][l] if (l == 0 and n in in_layer0) == layer0 else None for l in range(len(wl[n]))] for n in names}

    heads0 = _mla_heads(wl, 0)
    _, attn_run, attn_bwd = make_attention(aux, heads0)
    x2db = x2d.astype(BF16)
    pre, vjp_pre = jax.vjp(lambda x_, wl_: _mla_pre(x_, x2db, wl_, 0), x2d, pick(MLA_PRE_W, True))
    o, attn_res, partial = attn_run(*pre, comm_gather_own([prepped[n][:1] for n in mid_names]
                                                          + [prepped[n][rest_from[n]:] for n in rest_names]))
    for n, g in zip(mid_names, _run_comm(comm_gather_pass(partial[:len(mid_names)]), "gather_pass")):
        wl[n][0] = g[0]
    partial = partial[len(mid_names):]
    w_up0 = wl['ffn_w_up'][0]

    def ffn_in(x_, o_, wl_):
        x1_, x1b_, cwp_, cbp_ = _ffn_in(x_, linear(o_, _rows_full(wl_['mla_w_o'][0]), F32), wl_, 0, ops, w_up0.shape[2])
        return (x1_, cwp_, cbp_), x1b_

    (x1, cwp, cbp), vjp_in, x1b = jax.vjp(ffn_in, x2d, o, pick(FFN_IN_W, True), has_aux=True)
    a, ffn_res, rest = _ffn_hidden_fwd(x1, x1b, w_up0, cwp, cbp, comm_gather_pass(partial))
    for n, g in zip(rest_names, rest):
        for l in range(g.shape[0]):
            wl[n][rest_from[n] + l] = g[l]
    x_l0, vjp_out, x_l0b = jax.vjp(lambda x1_, a_, wl_: _ffn_out(x1_, a_, wl_, p3[0], 0, ops), x1, a,
                                   pick(FFN_OUT_W, True), has_aux=True)

    def tail(x_, wl_):
        xb_ = x_l0b
        for i in range(1, depth):
            x_, xb_ = _layer(x_, xb_, wl_, p3[i], aux, i, ops)
        return x_

    y, vjp_tail = jax.vjp(tail, x_l0, pick(WEIGHTS, False))
    sq, dy = _loss_call(y, target)

    dwl = {n: [None] * len(wl[n]) for n in WEIGHTS}

    def keep(part):
        for n, per_layer in part.items():
            for l, g in enumerate(per_layer):
                if g is not None:
                    dwl[n][l] = g

    x_c, y_c, c_place = _place()
    c_idx = jnp.reshape(c_place, (1,)).astype(jnp.int32)
    chip_idx = jnp.reshape(2 * x_c + y_c, (1,)).astype(jnp.int32)
    dx_l0, d_tail = vjp_tail(dy)
    keep(d_tail)
    g_rest = [jnp.stack(dwl[n][rest_from[n]:], axis=0) for n in rest_names]
    dx1_out, da, d_out = vjp_out(dx_l0)
    keep(d_out)
    (dx1_ffn, _, dw_up0, dcwp, dcbp), recv1 = _ffn_hidden_bwd(ffn_res, da, comm_rs_pair(g_rest))
    dwl['ffn_w_up'][0] = dw_up0
    g_mid = [dwl[n][0][None] for n in mid_names]
    g_rest, recv1 = g_rest + g_mid, list(recv1) + list(_rs_pair_exchange(g_mid))
    p1 = [_rs_pair_add(g, r, c_idx) for g, r in zip(g_rest, recv1)]
    dx_in, do, d_in = vjp_in((dx1_out + dx1_ffn, dcwp, dcbp))
    keep(d_in)
    d_pre_in, recv2 = attn_bwd(attn_res, do, comm_rs_chip(p1))
    red_rest = [_rs_final_add(p_, r, chip_idx) for p_, r in zip(p1, recv2)]
    dx_pre, d_pre = vjp_pre(d_pre_in)
    keep(d_pre)
    dx = dx_pre + dx_in

    small_blocks = [_reshard(dwl[n], SHARD_AXIS[n] if w[n].ndim == 3 else 2).reshape(N_DEV, -1) for n in SMALL]
    small_grad_packed, _ = _pack(small_blocks, F32, 8)
    red_first = reduce_scatter([dwl[n][0][None] for n in first_names] + [small_grad_packed[None]])
    by_layer = {n: [] for n in BIG}
    for n, g in zip(first_names, red_first):
        by_layer[n].append(g)
    for n, g in zip(mid_names, red_rest[len(rest_names):]):
        by_layer[n].append(g)
    for n, g in zip(rest_names, red_rest):
        by_layer[n].append(g)
    grads = {n: _unprep_big(n, jnp.concatenate(by_layer[n], axis=0), w[n].shape) for n in BIG}
    for n, f in zip(SMALL, _unpack(red_first[-1][0], small_sizes)):
        grads[n] = f.reshape(w[n].shape)

    repl_flat = [jnp.concatenate([g.reshape(-1) for g in dwl[n]]).reshape(1, -1) for n in REPL]
    loss_part = 0.5 * jnp.sum(sq) / sq.shape[1]
    packed, repl_sizes = _pack(repl_flat + [loss_part.reshape(1, 1)], F32, 8)
    summed = _unpack(all_reduce_small(packed[0])[None], repl_sizes)
    for n, f in zip(REPL, summed[:-1]):
        grads[n] = f.reshape(w[n].shape)
    loss = summed[-1].reshape(())

    delta, new_m, new_v = {}, {}, {}
    for n in WEIGHTS:
        delta[n], new_m[n], new_v[n] = _adamw_call(w[n], grads[n], m_in[n], v_in[n])
    return (loss, dx[None], *[grads[n] for n in WEIGHTS], *[delta[n] for n in WEIGHTS],
            *[new_m[n] for n in WEIGHTS], *[new_v[n] for n in WEIGHTS])
```

```python
import functools
import itertools

import jax
import jax.numpy as jnp
from jax import lax
from jax.experimental import pallas as pl
from jax.experimental.pallas import tpu as pltpu

F32 = jnp.float32
BF16 = jnp.bfloat16
MESH = pl.DeviceIdType.MESH
N_DEV = 8

EPS = 1e-5
NEG_INF = -1e30
CHUNK = 64
Q_BLOCK = 128
MLA_NOPE = 128
MLA_ROPE = 64
MLA_V = 128
MLA_QK_PAD = 256
ROPE_THETA = 10000.0
GLA_DK = 128
GLA_DV = 256
GLA_RANK = 16
GLA_TAU = 16.0
CONV_W = 3
ADAM_LR = 0.001
ADAM_B1 = 0.9
ADAM_B2 = 0.999
ADAM_EPS = 1e-08
ADAM_WD = 0.01
ADAM_STEP = 10
LOG2E = 1.4426950408889634

LANES = 128
PACK_COLS = 1024
VMEM_LIMIT = 56 * 1024 * 1024
MM_VMEM_BUDGET = 38 * 1024 * 1024

WEIGHTS = ['mla_w_in', 'mla_q_norm', 'mla_kv_norm', 'mla_w_uq', 'mla_w_uk', 'mla_w_uv', 'mla_w_o', 'gla_w_in',
           'gla_w_a2', 'gla_b_a', 'gla_o_norm', 'gla_w_o', 'ln1_g', 'ln1_b', 'ln2_g', 'ln2_b', 'ffn_w_up',
           'ffn_conv_w', 'ffn_conv_b', 'ffn_w_down', 'ple_w_proj', 'ple_w_gate', 'ple_b_gate']
SHARD_AXIS = {'mla_w_in': 1, 'mla_q_norm': None, 'mla_kv_norm': None, 'mla_w_uq': 2, 'mla_w_uk': 2, 'mla_w_uv': 2,
              'mla_w_o': 1, 'gla_w_in': 2, 'gla_w_a2': 2, 'gla_b_a': 1, 'gla_o_norm': 1, 'gla_w_o': 1,
              'ln1_g': None, 'ln1_b': None, 'ln2_g': None, 'ln2_b': None, 'ffn_w_up': 2, 'ffn_conv_w': 2,
              'ffn_conv_b': None, 'ffn_w_down': 1, 'ple_w_proj': 2, 'ple_w_gate': 1, 'ple_b_gate': None}
BIG = ['mla_w_in', 'mla_w_uq', 'mla_w_uk', 'mla_w_uv', 'mla_w_o', 'gla_w_in', 'gla_w_o', 'ffn_w_up', 'ffn_w_down',
       'ple_w_proj', 'ple_w_gate']
SMALL = ['gla_w_a2', 'gla_b_a', 'gla_o_norm', 'ffn_conv_w']
REPL = [n for n in WEIGHTS if SHARD_AXIS[n] is None]

_uid = [itertools.count()]


def _nm(base):
    return f"{base}_{next(_uid[0])}"


def _cp(sem=None):
    return pltpu.CompilerParams(dimension_semantics=sem, vmem_limit_bytes=VMEM_LIMIT)


def _round_up(n, m):
    return -(-n // m) * m


class Comm:
    MID_AT = 0.6

    def __init__(self, inputs, out_shapes, aliases, n_send, n_recv, n_local, build):
        self.inputs, self.out_shapes, self.aliases = list(inputs), list(out_shapes), dict(aliases)
        self.n_send, self.n_recv, self.n_local, self.build = n_send, n_recv, n_local, build


def _pcall(body, name, grid, n_prefetch, in_specs, out_specs, out_shape, scratch, sem, args, comm=None):
    in_specs, out_specs, out_shape, scratch, args = list(in_specs), list(out_specs), list(out_shape), list(scratch), list(args)
    n_in, n_out, n_scr = len(in_specs), len(out_specs), len(scratch)
    aliases = {}
    kernel_body = body
    if comm is not None:
        ci, co = len(comm.inputs), len(comm.out_shapes)
        any_spec = pl.BlockSpec(memory_space=pl.ANY)
        in_specs += [any_spec] * ci
        out_specs += [any_spec] * co
        out_shape += comm.out_shapes
        scratch += [pltpu.SemaphoreType.DMA((comm.n_send,)), pltpu.SemaphoreType.DMA((comm.n_recv,)),
                    pltpu.SemaphoreType.DMA((comm.n_local,))]
        aliases = {n_prefetch + n_in + k: n_out + v for k, v in comm.aliases.items()}
        args += comm.inputs
        sem = ("arbitrary",) * len(grid)

        def kernel_body(*refs):
            pre, r = refs[:n_prefetch], refs[n_prefetch:]
            ins, cin = r[:n_in], r[n_in:n_in + ci]
            outs, cout = r[n_in + ci:n_in + ci + n_out], r[n_in + ci + n_out:n_in + ci + n_out + co]
            scr = r[n_in + ci + n_out + co:n_in + ci + n_out + co + n_scr]
            send_sems, recv_sems, local_sems = r[-3:]
            first = functools.reduce(lambda a, b: a & b, [pl.program_id(d) == 0 for d in range(len(grid))])
            last = functools.reduce(lambda a, b: a & b, [pl.program_id(d) == grid[d] - 1 for d in range(len(grid))])
            built = comm.build(cin, cout, send_sems, recv_sems, local_sems)
            starts, waits = built[0], built[-1]
            mid_waits, mid_starts = (built[1], built[2]) if len(built) == 4 else ([], [])

            @pl.when(first)
            def _():
                for cp in starts:
                    cp.start()

            if mid_starts:
                step = pl.program_id(0)
                for d in range(1, len(grid)):
                    step = step * grid[d] + pl.program_id(d)
                n_steps = functools.reduce(lambda a, b: a * b, grid)

                @pl.when(step == int(n_steps * comm.MID_AT))
                def _():
                    for wait in mid_waits:
                        wait()
                    for cp in mid_starts:
                        cp.start()

            body(*pre, *ins, *outs, *scr)

            @pl.when(last)
            def _():
                for wait in waits:
                    wait()

    grid_spec = pltpu.PrefetchScalarGridSpec(num_scalar_prefetch=n_prefetch, grid=grid, in_specs=in_specs,
                                             out_specs=out_specs, scratch_shapes=scratch)
    return pl.pallas_call(kernel_body, name=_nm(name), grid_spec=grid_spec, out_shape=tuple(out_shape),
                          input_output_aliases=aliases, compiler_params=_cp(sem))(*args)


def _run_comm(comm, name):
    return _pcall(lambda: None, name, (1,), 0, [], [], [], [], ("arbitrary",), [], comm)


def _divisor_tiles(n, cap):
    if n % LANES:
        return [n]
    out = [t for t in range(LANES, min(n, cap) + 1, LANES) if n % t == 0]
    return out or [n]


def _mm_tiles(M, N, K, abytes, bbytes, obytes, tn_fixed=None, tk_fixed=None):
    best = None
    for tm in _divisor_tiles(M, 2048):
        for tn in ([tn_fixed] if tn_fixed else _divisor_tiles(N, 1536)):
            for tk in ([tk_fixed] if tk_fixed else _divisor_tiles(K, 4096)):
                vmem = 2 * (tm * tk * abytes + tk * tn * bbytes + tm * tn * obytes) + tm * tn * 4
                if vmem > MM_VMEM_BUDGET:
                    continue
                key = (tm * tn * tk, tk)
                if best is None or key > best[0]:
                    best = (key, (tm, tn, tk))
    assert best is not None, (M, N, K)
    return best[1]


def _mm(a, b, mode, out_dtype, base="mm", blocks=None):
    blk0, nblk = blocks if blocks else (0, 1)
    tn_fixed = tk_fixed = None
    if mode == "nn":
        M, K = a.shape
        N = nblk * b.shape[2] if blocks else b.shape[1]
        tn_fixed = b.shape[2] if blocks else None
    elif mode == "nt":
        M, K = a.shape
        N = b.shape[1] if blocks else b.shape[0]
        kb = 2 if blocks and nblk % 2 == 0 and blk0 % 2 == 0 else 1
        tk_fixed = kb * b.shape[2] if blocks else None
        assert not blocks or K == nblk * b.shape[2]
    else:
        (K, M), N = a.shape, b.shape[1]
        tn_fixed = N // nblk if blocks else None
    tm, tn, tk = _mm_tiles(M, N, K, a.dtype.itemsize, b.dtype.itemsize, jnp.dtype(out_dtype).itemsize, tn_fixed, tk_fixed)
    nk = K // tk
    out_shape = jax.ShapeDtypeStruct((M, N), out_dtype)
    out_spec = pl.BlockSpec((tm, tn), lambda i, j, k: (i, j))
    if mode == "nn":
        a_spec = pl.BlockSpec((tm, tk), lambda i, j, k: (i, k))
        b_spec = (pl.BlockSpec((None, tk, tn), lambda i, j, k: (blk0 + j, k, 0)) if blocks
                  else pl.BlockSpec((tk, tn), lambda i, j, k: (k, j)))
        dims = (((1,), (0,)), ((), ()))
    elif mode == "nt":
        a_spec = pl.BlockSpec((tm, tk), lambda i, j, k: (i, k))
        b_spec = (pl.BlockSpec((kb, tn, tk // kb), lambda i, j, k: (blk0 // kb + k, j, 0)) if blocks
                  else pl.BlockSpec((tn, tk), lambda i, j, k: (j, k)))
        dims = (((1,), (1,)), ((), ()))
    else:
        a_spec = pl.BlockSpec((tk, tm), lambda i, j, k: (k, i))
        b_spec = pl.BlockSpec((tk, tn), lambda i, j, k: (k, j))
        dims = (((0,), (0,)), ((), ()))
        if blocks:
            out_shape = jax.ShapeDtypeStruct((nblk, M, tn), out_dtype)
            out_spec = pl.BlockSpec((None, tm, tn), lambda i, j, k: (j, i, 0))

    def body(a_ref, b_ref, o_ref, acc_ref):
        if mode == "nt" and blocks:
            bn = tk // kb
            part = sum(lax.dot_general(a_ref[:, q * bn:(q + 1) * bn].astype(BF16), b_ref[q].astype(BF16), dims,
                                       preferred_element_type=F32) for q in range(kb))
        else:
            part = lax.dot_general(a_ref[...].astype(BF16), b_ref[...].astype(BF16), dims, preferred_element_type=F32)
        if nk == 1:
            o_ref[...] = part.astype(o_ref.dtype)
        else:
            k = pl.program_id(2)

            @pl.when(k == 0)
            def _():
                acc_ref[...] = part

            @pl.when(k > 0)
            def _():
                acc_ref[...] += part

            @pl.when(k == nk - 1)
            def _():
                o_ref[...] = acc_ref[...].astype(o_ref.dtype)

    return pl.pallas_call(
        body, name=_nm(base), grid=(M // tm, N // tn, nk), out_shape=out_shape,
        in_specs=[a_spec, b_spec], out_specs=out_spec,
        scratch_shapes=[pltpu.VMEM((tm, tn) if nk > 1 else (8, LANES), F32)],
        compiler_params=_cp(("parallel", "parallel", "arbitrary")),
    )(a, b)


def _all_blocks(w):
    return (0, w.shape[0]) if w.ndim == 3 else None


@functools.partial(jax.custom_vjp, nondiff_argnums=(2,))
def linear(a, w, out_dtype):
    return _mm(a, w, "nn", out_dtype, "lin_fwd", _all_blocks(w))


def _linear_fwd(a, w, out_dtype):
    return _mm(a, w, "nn", out_dtype, "lin_fwd", _all_blocks(w)), (a, w)


def _linear_bwd(out_dtype, res, dy):
    a, w = res
    return (_mm(dy, w, "nt", a.dtype, "lin_dx", _all_blocks(w)), _mm(a, dy, "tn", w.dtype, "lin_dw", _all_blocks(w)))


linear.defvjp(_linear_fwd, _linear_bwd)


@functools.partial(jax.custom_vjp, nondiff_argnums=(3,))
def linear_sh(a, a_bf16, w, out_dtype):
    return _mm(a_bf16, w, "nn", out_dtype, "lin_fwd", _all_blocks(w))


def _linear_sh_fwd(a, a_bf16, w, out_dtype):
    return _mm(a_bf16, w, "nn", out_dtype, "lin_fwd", _all_blocks(w)), (a_bf16, w, jnp.zeros((), a.dtype))


def _linear_sh_bwd(out_dtype, res, dy):
    a_bf16, w, tok = res
    return (_mm(dy, w, "nt", tok.dtype, "lin_dx", _all_blocks(w)), jnp.zeros_like(a_bf16),
            _mm(a_bf16, dy, "tn", w.dtype, "lin_dw", _all_blocks(w)))


linear_sh.defvjp(_linear_sh_fwd, _linear_sh_bwd)


def _row_tile(S, width):
    tr = 512 if width <= 1024 else 256
    return min(tr, S)


def _rw_fwd(f, rows, params, out_dtypes, base):
    S = rows[0][0].shape[0]
    tr = _row_tile(S, max(w for _, w, _ in rows))
    n_in = len(rows) + len(params)
    avals = [jax.ShapeDtypeStruct((tr, w), F32) for _, w, _ in rows] + [jax.ShapeDtypeStruct(p.shape, F32) for p in params]
    outs = jax.eval_shape(f, *avals)

    def body(*refs):
        vals = [r[...].astype(F32) for r in refs[:n_in]]
        for o_ref, r in zip(refs[n_in:], f(*vals)):
            o_ref[...] = r.astype(o_ref.dtype)

    in_specs = [pl.BlockSpec((tr, w), functools.partial(lambda i, cb: (i, cb), cb=cb)) for _, w, cb in rows]
    in_specs += [pl.BlockSpec(p.shape, lambda i: (0, 0)) for p in params]
    return pl.pallas_call(
        body, name=_nm(base), grid=(S // tr,),
        out_shape=tuple(jax.ShapeDtypeStruct((S, o.shape[1]), dt) for o, dt in zip(outs, out_dtypes)),
        in_specs=in_specs, out_specs=tuple(pl.BlockSpec((tr, o.shape[1]), lambda i: (i, 0)) for o in outs),
        compiler_params=_cp(("parallel",)),
    )(*[a for a, _, _ in rows], *params)


def _rw_bwd(f, rows, params, cts, row_grad_dtypes, base):
    S = rows[0][0].shape[0]
    tr = _row_tile(S, max(w for _, w, _ in rows))
    n_rows, n_par, n_ct = len(rows), len(params), len(cts)
    want = [k for k, dt in enumerate(row_grad_dtypes) if dt is not None]

    def body(*refs):
        in_refs = refs[:n_rows + n_par]
        ct_refs = refs[n_rows + n_par:n_rows + n_par + n_ct]
        out_refs = refs[n_rows + n_par + n_ct:]
        vals = [r[...].astype(F32) for r in in_refs]
        _, vjp_fn = jax.vjp(f, *vals)
        grads = vjp_fn(tuple(c[...].astype(F32) for c in ct_refs))
        for o_ref, k in zip(out_refs[:len(want)], want):
            o_ref[...] = grads[k].astype(o_ref.dtype)
        i = pl.program_id(0)
        for o_ref, g in zip(out_refs[len(want):], grads[n_rows:]):
            @pl.when(i == 0)
            def _(o_ref=o_ref, g=g):
                o_ref[...] = g

            @pl.when(i > 0)
            def _(o_ref=o_ref, g=g):
                o_ref[...] += g

    in_specs = [pl.BlockSpec((tr, w), functools.partial(lambda i, cb: (i, cb), cb=cb)) for _, w, cb in rows]
    in_specs += [pl.BlockSpec(p.shape, lambda i: (0, 0)) for p in params]
    in_specs += [pl.BlockSpec((tr, c.shape[1]), lambda i: (i, 0)) for c in cts]
    out_shape = [jax.ShapeDtypeStruct((S, rows[k][1]), row_grad_dtypes[k]) for k in want]
    out_specs = [pl.BlockSpec((tr, rows[k][1]), lambda i: (i, 0)) for k in want]
    out_shape += [jax.ShapeDtypeStruct(p.shape, F32) for p in params]
    out_specs += [pl.BlockSpec(p.shape, lambda i: (0, 0)) for p in params]
    res = pl.pallas_call(
        body, name=_nm(base), grid=(S // tr,), out_shape=tuple(out_shape),
        in_specs=in_specs, out_specs=tuple(out_specs), compiler_params=_cp(("arbitrary",)),
    )(*[a for a, _, _ in rows], *params, *cts)
    row_grads = [None] * n_rows
    for k, g in zip(want, res[:len(want)]):
        row_grads[k] = g
    return row_grads, list(res[len(want):])


def rw_op(f, base, n_rows, out_dtypes, shadow=False):
    f_fwd = (lambda *a: (lambda r: tuple(r) + (r[0],))(f(*a))) if shadow else f
    fwd_dtypes = list(out_dtypes) + ([BF16] if shadow else [])

    @jax.custom_vjp
    def op(*args):
        return fwd(*args)[0]

    def split(args):
        rows = [(a, a.shape[1], 0) for a in args[:n_rows]]
        return rows, list(args[n_rows:])

    def fwd(*args):
        rows, params = split(args)
        return tuple(_rw_fwd(f_fwd, rows, params, fwd_dtypes, base + "_fwd")), args

    def bwd(args, cts):
        rows, params = split(args)
        cts = list(cts)[:len(out_dtypes)]
        rg, pg = _rw_bwd(f, rows, params, cts, [a.dtype for a, _, _ in rows], base + "_bwd")
        return tuple(rg) + tuple(g.astype(p.dtype) for g, p in zip(pg, params))

    op.defvjp(fwd, bwd)
    return op


def _ln_res_fn(alpha):
    def f(x, m, g, b):
        z = alpha * x + m
        mu = jnp.mean(z, -1, keepdims=True)
        zc = z - mu
        var = jnp.mean(zc * zc, -1, keepdims=True)
        return (zc * lax.rsqrt(var + EPS) * g + b,)
    return f


def _rms(x, g):
    return x * lax.rsqrt(jnp.mean(x * x, -1, keepdims=True) + EPS) * g


def _mla_norm_fn(q_lora, kv_lora):
    def f(h, qn, kvn):
        return (_rms(h[:, :q_lora], qn), _rms(h[:, q_lora:q_lora + kv_lora], kvn),
                h[:, q_lora + kv_lora:q_lora + kv_lora + LANES])
    return f


def _log_sigmoid(z):
    return jnp.minimum(z, 0.0) - jnp.log(1.0 + jnp.exp(-jnp.abs(z)))


def _gla_gate_fn(z, b):
    return (_log_sigmoid(z + b) / GLA_TAU,)


def _ple_fn(x, glog, pp, b):
    return (x + jax.nn.sigmoid(glog + b) * pp,)


def _gla_out_fn(heads):
    def f(o, r, g):
        parts = []
        for h in range(heads):
            oh = o[:, h * GLA_DV:(h + 1) * GLA_DV]
            mu = jnp.mean(oh, -1, keepdims=True)
            oc = oh - mu
            var = jnp.mean(oc * oc, -1, keepdims=True)
            parts.append(oc * lax.rsqrt(var + EPS) * g[:, h * GLA_DV:(h + 1) * GLA_DV])
        return (jnp.concatenate(parts, axis=1) * (r * jax.nn.sigmoid(r)),)
    return f


def _rope_call(x, tabs, roped, out_dtype, base, fold=False, scale=None):
    S, C = x.shape
    nb = C // LANES
    tr = min(512 if C <= 1024 else 256, S)
    out_c = LANES if fold else C

    def rot(v, a, b1, b2):
        return v * a + pltpu.roll(v, 96, 1) * b1 + pltpu.roll(v, 32, 1) * b2

    def body(x_ref, a_ref, b1_ref, b2_ref, o_ref):
        a, b1, b2 = a_ref[...], b1_ref[...], b2_ref[...]
        if fold:
            v = x_ref[:, 0:LANES].astype(F32)
            for blk in range(1, nb):
                v = v + x_ref[:, blk * LANES:(blk + 1) * LANES].astype(F32)
            o_ref[...] = rot(v, a, b1, b2).astype(o_ref.dtype)
            return
        for blk in range(nb):
            v = x_ref[:, blk * LANES:(blk + 1) * LANES].astype(F32)
            if roped(blk):
                v = rot(v, a, b1, b2)
            if scale is not None:
                v = v * scale
            o_ref[:, blk * LANES:(blk + 1) * LANES] = v.astype(o_ref.dtype)

    row = lambda w: pl.BlockSpec((tr, w), lambda i: (i, 0))
    return pl.pallas_call(
        body, name=_nm(base), grid=(S // tr,), out_shape=jax.ShapeDtypeStruct((S, out_c), out_dtype),
        in_specs=[row(C), row(LANES), row(LANES), row(LANES)], out_specs=row(out_c),
        compiler_params=_cp(("parallel",)),
    )(x, *tabs)


def _attn_tile(S):
    return min(512, S)


def _tri_schedule(n, by_key):
    pairs = ([(i, j) for j in range(n) for i in range(j, n)] if by_key
             else [(i, j) for i in range(n) for j in range(i + 1)])
    return (jnp.asarray([p[0] for p in pairs], jnp.int32), jnp.asarray([p[1] for p in pairs], jnp.int32))


def _mask_table(cid, t):
    n = cid.shape[0] // t
    blocks = cid.reshape(n, t)
    cmin_q, cmax_k = jnp.min(blocks, axis=1), jnp.max(blocks, axis=1)
    need = (cmax_k[None, :] > cmin_q[:, None]) | jnp.eye(n, dtype=bool)
    return need.astype(jnp.int32).reshape(n * n)


def _attn_mask(cidq_ref, cidk_ref, i, j, t):
    qrow = i * t + lax.broadcasted_iota(jnp.int32, (t, 1), 0)
    kcol = j * t + lax.broadcasted_iota(jnp.int32, (1, t), 1)
    qlim = (qrow // Q_BLOCK + 1) * Q_BLOCK
    return (cidk_ref[...] <= cidq_ref[...]) & (kcol < qlim)


ATTN_FWD_HEADS = 8
ATTN_BWD_HEADS = 4
ATTN_SCALE = (MLA_NOPE + MLA_ROPE) ** -0.5
ATTN_SCALE2 = ATTN_SCALE * LOG2E


def _attn_fwd_call(q, kn, v, kr, aux, heads, comm=None):
    S = q.shape[0]
    t = _attn_tile(S)
    n = S // t
    hp = ATTN_FWD_HEADS if heads % ATTN_FWD_HEADS == 0 else 1
    qi_tab, kj_tab = _tri_schedule(n, False)

    def body(qi_ref, kj_ref, need_ref, q_ref, kn_ref, v_ref, kr_ref, cidq_ref, cidk_ref, o_ref, lse_ref,
             m_ref, l_ref, acc_ref):
        st = pl.program_id(1)
        i, j = qi_ref[st], kj_ref[st]

        @pl.when(j == 0)
        def _():
            m_ref[...] = jnp.full(m_ref.shape, NEG_INF, F32)
            l_ref[...] = jnp.zeros(l_ref.shape, F32)
            acc_ref[...] = jnp.zeros(acc_ref.shape, F32)

        def update(masked):
            mask = _attn_mask(cidq_ref, cidk_ref, i, j, t) if masked else None
            for hh in range(hp):
                lanes = slice(hh * LANES, (hh + 1) * LANES)
                k = jnp.concatenate([kn_ref[:, lanes], kr_ref[...]], axis=1)
                qh = q_ref[:, hh * MLA_QK_PAD:(hh + 1) * MLA_QK_PAD]
                s = lax.dot_general(qh, k, (((1,), (1,)), ((), ())), preferred_element_type=F32)
                if masked:
                    s = jnp.where(mask, s, NEG_INF)
                m_prev = m_ref[:, lanes]
                m_new = jnp.maximum(m_prev, jnp.max(s, axis=1, keepdims=True))
                alpha = jnp.exp2(m_prev - m_new)
                p = jnp.exp2(s - m_new[:, :1])
                l_ref[:, lanes] = alpha * l_ref[:, lanes] + jnp.sum(p, axis=1, keepdims=True)
                acc_ref[:, lanes] = alpha * acc_ref[:, lanes] + jnp.dot(p.astype(BF16), v_ref[:, lanes],
                                                                        preferred_element_type=F32)
                m_ref[:, lanes] = m_new

        need = need_ref[i * n + j]

        @pl.when(need != 0)
        def _():
            update(True)

        @pl.when(need == 0)
        def _():
            update(False)

        @pl.when(j == i)
        def _():
            o_ref[...] = (acc_ref[...] / l_ref[...]).astype(o_ref.dtype)
            lse_ref[...] = m_ref[...] + jnp.log(l_ref[...]) * LOG2E

    qmap = lambda h, s, qi, kj, need: (qi[s], h)
    kmap = lambda h, s, qi, kj, need: (kj[s], h)
    return _pcall(
        body, "attn_fwd", (heads // hp, qi_tab.shape[0]), 3,
        in_specs=[pl.BlockSpec((t, hp * MLA_QK_PAD), qmap), pl.BlockSpec((t, hp * MLA_NOPE), kmap),
                  pl.BlockSpec((t, hp * MLA_V), kmap),
                  pl.BlockSpec((t, LANES), lambda h, s, qi, kj, need: (kj[s], 0)),
                  pl.BlockSpec((t, 1), lambda h, s, qi, kj, need: (qi[s], 0)),
                  pl.BlockSpec((1, t), lambda h, s, qi, kj, need: (0, kj[s]))],
        out_specs=[pl.BlockSpec((t, hp * MLA_V), qmap), pl.BlockSpec((t, hp * LANES), qmap)],
        out_shape=[jax.ShapeDtypeStruct((S, heads * MLA_V), BF16), jax.ShapeDtypeStruct((S, heads * LANES), F32)],
        scratch=[pltpu.VMEM((t, hp * LANES), F32), pltpu.VMEM((t, hp * LANES), F32), pltpu.VMEM((t, hp * MLA_V), F32)],
        sem=("parallel", "arbitrary"),
        args=[qi_tab, kj_tab, aux['need'], q, kn, v, kr, aux['cidq'], aux['cidk']], comm=comm)


def _attn_bwd_call(q, kn, v, kr, o, lse, do, aux, heads, comm=None):
    S = q.shape[0]
    t = _attn_tile(S)
    n = S // t
    qi_tab, kj_tab = _tri_schedule(n, True)
    scale = ATTN_SCALE
    nt_dims = (((1,), (1,)), ((), ()))
    tn_dims = (((0,), (0,)), ((), ()))

    hp = ATTN_BWD_HEADS if heads % ATTN_BWD_HEADS == 0 else 1

    def body(qi_ref, kj_ref, need_ref, q_ref, kn_ref, v_ref, kr_ref, cidq_ref, cidk_ref, o_ref, lse_ref, do_ref,
             ta_ref, tb1_ref, tb2_ref, dq_ref, dkn_ref, dv_ref, dkr_ref, dk_acc, dv_acc, dq_acc):
        st = pl.program_id(1)
        i, j = qi_ref[st], kj_ref[st]

        @pl.when(st == 0)
        def _():
            dq_acc[...] = jnp.zeros(dq_acc.shape, F32)

        @pl.when(i == j)
        def _():
            dk_acc[...] = jnp.zeros(dk_acc.shape, F32)
            dv_acc[...] = jnp.zeros(dv_acc.shape, F32)

        rows = pl.ds(pl.multiple_of(i * t, t), t)

        def grads(masked):
            mask = _attn_mask(cidq_ref, cidk_ref, i, j, t) if masked else None
            for hh in range(hp):
                lanes = slice(hh * LANES, (hh + 1) * LANES)
                wide = slice(hh * MLA_QK_PAD, (hh + 1) * MLA_QK_PAD)
                k = jnp.concatenate([kn_ref[:, lanes], kr_ref[...]], axis=1)
                qt, do = q_ref[:, wide], do_ref[:, lanes]
                s = lax.dot_general(qt, k, nt_dims, preferred_element_type=F32)
                if masked:
                    s = jnp.where(mask, s, NEG_INF)
                dp = lax.dot_general(do, v_ref[:, lanes], nt_dims, preferred_element_type=F32)
                dsum = jnp.sum(do.astype(F32) * o_ref[:, lanes].astype(F32), axis=1, keepdims=True)
                p = jnp.exp2(s - lse_ref[:, hh * LANES:hh * LANES + 1])
                ds = (p * (dp - dsum) * scale).astype(BF16)
                dv_acc[:, lanes] += lax.dot_general(p.astype(BF16), do, tn_dims, preferred_element_type=F32)
                dk_acc[:, wide] += lax.dot_general(ds, qt, tn_dims, preferred_element_type=F32)
                dq_acc[rows, wide] += jnp.dot(ds, k, preferred_element_type=F32)

        need = need_ref[i * n + j]

        @pl.when(need != 0)
        def _():
            grads(True)

        @pl.when(need == 0)
        def _():
            grads(False)

        @pl.when(i == n - 1)
        def _():
            for hh in range(hp):
                lanes = slice(hh * LANES, (hh + 1) * LANES)
                off = hh * MLA_QK_PAD
                dkn_ref[:, lanes] = (dk_acc[:, off:off + MLA_NOPE] * (1.0 / ATTN_SCALE2)).astype(dkn_ref.dtype)
                dkr_ref[:, lanes] = dk_acc[:, off + MLA_NOPE:off + MLA_QK_PAD] * (1.0 / ATTN_SCALE2)
            dv_ref[...] = dv_acc[...].astype(dv_ref.dtype)

        @pl.when(i == j)
        def _():
            for hh in range(hp):
                off = hh * MLA_QK_PAD
                dq_ref[rows, off:off + MLA_NOPE] = dq_acc[rows, off:off + MLA_NOPE].astype(dq_ref.dtype)
                g = dq_acc[rows, off + MLA_NOPE:off + MLA_QK_PAD]
                g = g * ta_ref[...] + pltpu.roll(g, 96, 1) * tb1_ref[...] + pltpu.roll(g, 32, 1) * tb2_ref[...]
                dq_ref[rows, off + MLA_NOPE:off + MLA_QK_PAD] = g.astype(dq_ref.dtype)

    qmap = lambda h, s, qi, kj, need: (qi[s], h)
    kmap = lambda h, s, qi, kj, need: (kj[s], h)
    whole = pl.BlockSpec((t, LANES), lambda h, s, qi, kj, need: (qi[s], 0))
    return _pcall(
        body, "attn_bwd", (heads // hp, qi_tab.shape[0]), 3,
        in_specs=[pl.BlockSpec((t, hp * MLA_QK_PAD), qmap), pl.BlockSpec((t, hp * MLA_NOPE), kmap),
                  pl.BlockSpec((t, hp * MLA_V), kmap),
                  pl.BlockSpec((t, LANES), lambda h, s, qi, kj, need: (kj[s], 0)),
                  pl.BlockSpec((t, 1), lambda h, s, qi, kj, need: (qi[s], 0)),
                  pl.BlockSpec((1, t), lambda h, s, qi, kj, need: (0, kj[s])),
                  pl.BlockSpec((t, hp * MLA_V), qmap), pl.BlockSpec((t, hp * LANES), qmap),
                  pl.BlockSpec((t, hp * MLA_V), qmap), whole, whole, whole],
        out_specs=[pl.BlockSpec((S, hp * MLA_QK_PAD), lambda h, s, qi, kj, need: (0, h)),
                   pl.BlockSpec((t, hp * MLA_NOPE), kmap), pl.BlockSpec((t, hp * MLA_V), kmap),
                   pl.BlockSpec((t, hp * LANES), kmap)],
        out_shape=[jax.ShapeDtypeStruct((S, heads * MLA_QK_PAD), BF16), jax.ShapeDtypeStruct((S, heads * MLA_NOPE), BF16),
                   jax.ShapeDtypeStruct((S, heads * MLA_V), BF16), jax.ShapeDtypeStruct((S, heads * LANES), F32)],
        scratch=[pltpu.VMEM((t, hp * MLA_QK_PAD), F32), pltpu.VMEM((t, hp * MLA_V), F32),
                 pltpu.VMEM((S, hp * MLA_QK_PAD), F32)],
        sem=("parallel", "arbitrary"),
        args=[qi_tab, kj_tab, aux['need'], q, kn, v, kr, aux['cidq'], aux['cidk'], o, lse, do, *aux['rope'][1]],
        comm=comm)


def make_attention(aux, heads):
    tabs_f, tabs_b = aux['rope']
    odd, every = (lambda blk: blk % 2 == 1), (lambda blk: True)

    def run_host(q_raw, kn, v, kr_raw, comm=None):
        q = _rope_call(q_raw, tabs_f, odd, BF16, "rope_q", scale=ATTN_SCALE2)
        kr = _rope_call(kr_raw, tabs_f, every, BF16, "rope_k")
        o, lse, *carried = _attn_fwd_call(q, kn, v, kr, aux, heads, comm)
        return o, (q, kn, v, kr, o, lse), carried

    def bwd_host(res, do, comm=None):
        q, kn, v, kr, o, lse = res
        dq_raw, dkn, dv, dkr, *carried = _attn_bwd_call(q, kn, v, kr, o, lse, do, aux, heads, comm)
        return (dq_raw, dkn, dv, _rope_call(dkr, tabs_b, every, F32, "rope_dk", fold=True)), carried

    @jax.custom_vjp
    def attn(q_raw, kn, v, kr_raw):
        return run_host(q_raw, kn, v, kr_raw)[0]

    attn.defvjp(lambda *a: run_host(*a)[:2], lambda res, do: bwd_host(res, do)[0])
    return attn, run_host, bwd_host


GLA_ROWS = 256
GLA_HEADS_PER_STEP = 4


def _tri(lower):
    r = lax.broadcasted_iota(jnp.int32, (CHUNK, CHUNK), 0)
    c = lax.broadcasted_iota(jnp.int32, (CHUNK, CHUNK), 1)
    return jnp.where((c <= r) if lower else (c >= r), 1.0, 0.0).astype(F32)


def _gla_chunk(q_ref, k_ref, v_ref, la_ref, sl, hh):
    lk, lv = slice(hh * GLA_DK, (hh + 1) * GLA_DK), slice(hh * GLA_DV, (hh + 1) * GLA_DV)
    la = la_ref[sl, lk]
    cum = jnp.dot(_tri(True), la, preferred_element_type=F32, precision=lax.Precision.HIGHEST)
    tot = cum[CHUNK - 1:CHUNK, :]
    e = jnp.exp(tot - cum)
    k = k_ref[sl, lk].astype(F32)
    kdec = k * e
    v = v_ref[sl, lv]
    upd_t = lax.dot_general(v.astype(BF16), kdec.astype(BF16), (((0,), (0,)), ((), ())), preferred_element_type=F32)
    qs = (q_ref[sl, lk].astype(F32) * (GLA_DK ** -0.5)).astype(BF16)
    return e, k, kdec, v, upd_t, jnp.exp(tot), qs


def _gla_group(heads):
    return GLA_HEADS_PER_STEP if heads % GLA_HEADS_PER_STEP == 0 else 1


def _gla_specs(heads, hp, rows_map):
    groups = heads // hp
    return [pl.BlockSpec((GLA_ROWS, hp * GLA_DK), lambda h, b: (rows_map(b), h)),
            pl.BlockSpec((GLA_ROWS, hp * GLA_DK), lambda h, b: (rows_map(b), groups + h)),
            pl.BlockSpec((GLA_ROWS, hp * GLA_DV), lambda h, b: (rows_map(b), groups + h)),
            pl.BlockSpec((GLA_ROWS, hp * GLA_DK), lambda h, b: (rows_map(b), h))]


def _gla_fwd_call(hm, la, heads):
    S = hm.shape[0]
    assert S % GLA_ROWS == 0
    nb, cpb, hp = S // GLA_ROWS, GLA_ROWS // CHUNK, _gla_group(heads)

    def body(q_ref, k_ref, v_ref, la_ref, o_ref, sp_ref, st_ref):
        @pl.when(pl.program_id(1) == 0)
        def _():
            st_ref[...] = jnp.zeros(st_ref.shape, F32)

        for c in range(cpb):
            sl = slice(c * CHUNK, (c + 1) * CHUNK)
            for hh in range(hp):
                _, _, _, _, upd_t, decay, qs = _gla_chunk(q_ref, k_ref, v_ref, la_ref, sl, hh)
                state = st_ref[hh]
                sp_ref[hh, c] = state
                state = state * decay + upd_t
                st_ref[hh] = state
                o_ref[sl, hh * GLA_DV:(hh + 1) * GLA_DV] = lax.dot_general(
                    qs, state.astype(BF16), (((1,), (1,)), ((), ())), preferred_element_type=F32)

    return pl.pallas_call(
        body, name=_nm("gla_fwd"), grid=(heads // hp, nb),
        out_shape=(jax.ShapeDtypeStruct((S, heads * GLA_DV), F32),
                   jax.ShapeDtypeStruct((heads, S // CHUNK, GLA_DV, GLA_DK), F32)),
        in_specs=_gla_specs(heads, hp, lambda b: b),
        out_specs=(pl.BlockSpec((GLA_ROWS, hp * GLA_DV), lambda h, b: (b, h)),
                   pl.BlockSpec((hp, cpb, GLA_DV, GLA_DK), lambda h, b: (h, b, 0, 0))),
        scratch_shapes=[pltpu.VMEM((hp, GLA_DV, GLA_DK), F32)],
        compiler_params=_cp(("parallel", "arbitrary")),
    )(hm, hm, hm, la)


def _gla_bwd_call(hm, la, sprev, do, heads):
    S = hm.shape[0]
    nb, cpb, hp = S // GLA_ROWS, GLA_ROWS // CHUNK, _gla_group(heads)
    scale = GLA_DK ** -0.5

    def body(q_ref, k_ref, v_ref, la_ref, sp_ref, do_ref, dq_ref, dk_ref, dv_ref, dla_ref, carry_ref):
        @pl.when(pl.program_id(1) == 0)
        def _():
            carry_ref[...] = jnp.zeros(carry_ref.shape, F32)

        for c in reversed(range(cpb)):
            sl = slice(c * CHUNK, (c + 1) * CHUNK)
            for hh in range(hp):
                lk, lv = slice(hh * GLA_DK, (hh + 1) * GLA_DK), slice(hh * GLA_DV, (hh + 1) * GLA_DV)
                e, k, kdec, v, upd_t, decay, qs = _gla_chunk(q_ref, k_ref, v_ref, la_ref, sl, hh)
                sp = sp_ref[hh, c]
                s_n = sp * decay + upd_t
                dob = do_ref[sl, lv].astype(BF16)
                g = carry_ref[hh] + lax.dot_general(dob, qs, (((0,), (0,)), ((), ())), preferred_element_type=F32)
                gb = g.astype(BF16)
                dq_ref[sl, lk] = (jnp.dot(dob, s_n.astype(BF16), preferred_element_type=F32) * scale).astype(dq_ref.dtype)
                ddecay = jnp.sum(g * sp, axis=0, keepdims=True)
                dkdec = jnp.dot(v.astype(BF16), gb, preferred_element_type=F32)
                dv_ref[sl, lv] = lax.dot_general(kdec.astype(BF16), gb, (((1,), (1,)), ((), ())),
                                                 preferred_element_type=F32).astype(dv_ref.dtype)
                dk_ref[sl, lk] = (dkdec * e).astype(dk_ref.dtype)
                w = dkdec * k * e
                dtot = jnp.sum(w, axis=0, keepdims=True) + ddecay * decay
                last = lax.broadcasted_iota(jnp.int32, (CHUNK, 1), 0) == CHUNK - 1
                dcum = jnp.where(last, dtot - w, -w)
                dla_ref[sl, lk] = jnp.dot(_tri(False), dcum, preferred_element_type=F32, precision=lax.Precision.HIGHEST)
                carry_ref[hh] = g * decay

    rev = lambda b: nb - 1 - b
    narrow = pl.BlockSpec((GLA_ROWS, hp * GLA_DK), lambda h, b: (rev(b), h))
    wide = pl.BlockSpec((GLA_ROWS, hp * GLA_DV), lambda h, b: (rev(b), h))
    return pl.pallas_call(
        body, name=_nm("gla_bwd"), grid=(heads // hp, nb),
        out_shape=(jax.ShapeDtypeStruct((S, heads * GLA_DK), hm.dtype), jax.ShapeDtypeStruct((S, heads * GLA_DK), hm.dtype),
                   jax.ShapeDtypeStruct((S, heads * GLA_DV), hm.dtype), jax.ShapeDtypeStruct((S, heads * GLA_DK), F32)),
        in_specs=_gla_specs(heads, hp, rev) + [
            pl.BlockSpec((hp, cpb, GLA_DV, GLA_DK), lambda h, b: (h, rev(b), 0, 0)), wide],
        out_specs=(narrow, narrow, wide, narrow),
        scratch_shapes=[pltpu.VMEM((hp, GLA_DV, GLA_DK), F32)],
        compiler_params=_cp(("parallel", "arbitrary")),
    )(hm, hm, hm, la, sprev, do)


@functools.partial(jax.custom_vjp, nondiff_argnums=(3,))
def gla_core(hm, la, o_norm, heads):
    return _gla_core_fwd(hm, la, o_norm, heads)[0]


def _gla_core_fwd(hm, la, o_norm, heads):
    o, sprev = _gla_fwd_call(hm, la, heads)
    vd = heads * GLA_DV
    rows = [(o, vd, 0), (hm, vd, 2 * heads * GLA_DK // vd + 1)]
    (y,) = _rw_fwd(_gla_out_fn(heads), rows, [o_norm], [BF16], "gla_out_fwd")
    return y, (hm, la, o_norm, o, sprev)


def _gla_core_bwd(heads, res, dy):
    hm, la, o_norm, o, sprev = res
    vd = heads * GLA_DV
    rows = [(o, vd, 0), (hm, vd, 2 * heads * GLA_DK // vd + 1)]
    (do, dr), (dg,) = _rw_bwd(_gla_out_fn(heads), rows, [o_norm], [dy], [F32, hm.dtype], "gla_out_bwd")
    dq, dk, dv, dla = _gla_bwd_call(hm, la, sprev, do, heads)
    return jnp.concatenate([dq, dk, dv, dr], axis=1), dla, dg


gla_core.defvjp(_gla_core_fwd, _gla_core_bwd)


CONV_COLS = 256
HALO = 16


def _conv_rows(S):
    return min(512, S)


def _conv_taps(main_ref, halo_ref, i):
    prev = jnp.where(i > 0, halo_ref[...].astype(F32), 0.0)
    full = jnp.concatenate([prev, main_ref[...].astype(F32)], axis=0)
    return full[HALO:], pltpu.roll(full, 1, 0)[HALO:], pltpu.roll(full, 2, 0)[HALO:]


def _conv_apply(taps, w_ref, b_ref):
    x0, x1, x2 = taps
    return x2 * w_ref[0:1, :] + x1 * w_ref[1:2, :] + x0 * w_ref[2:3, :] + b_ref[...]


def _gelu_gate(uc, gc):
    return uc * jax.nn.gelu(gc)


def _conv_cols(dff, pref):
    return max(c for c in range(LANES, pref + 1, LANES) if dff % c == 0)


def _conv_in_specs(R, C, nj):
    hpr = R // HALO
    main = lambda off: pl.BlockSpec((R, C), lambda j, i: (i, j + off))
    halo = lambda off: pl.BlockSpec((HALO, C), lambda j, i: (jnp.maximum(i * hpr - 1, 0), j + off))
    par = lambda rows, off: pl.BlockSpec((rows, C), lambda j, i: (0, j + off))
    return [main(0), halo(0), main(nj), halo(nj), par(CONV_W, 0), par(CONV_W, nj), par(1, 0), par(1, nj)]


def _conv_fwd_call(h, cw, cb, comm=None):
    S, dff = h.shape[0], h.shape[1] // 2
    R, C = _conv_rows(S), _conv_cols(dff, 768)
    nj = dff // C

    def body(u_ref, uh_ref, g_ref, gh_ref, wu_ref, wg_ref, bu_ref, bg_ref, a_ref):
        i = pl.program_id(1)
        uc = _conv_apply(_conv_taps(u_ref, uh_ref, i), wu_ref, bu_ref)
        gc = _conv_apply(_conv_taps(g_ref, gh_ref, i), wg_ref, bg_ref)
        a_ref[...] = _gelu_gate(uc, gc).astype(a_ref.dtype)

    return _pcall(
        body, "conv_fwd", (nj, S // R), 0, in_specs=_conv_in_specs(R, C, nj),
        out_specs=[pl.BlockSpec((R, C), lambda j, i: (i, j))], out_shape=[jax.ShapeDtypeStruct((S, dff), BF16)],
        scratch=[], sem=("parallel", "parallel"), args=[h, h, h, h, cw, cw, cb, cb], comm=comm)


def _conv_bwd_gate_call(h, cw, cb, da, comm=None):
    S, dff = h.shape[0], h.shape[1] // 2
    R, C = _conv_rows(S), _conv_cols(dff, 512)
    nj = dff // C

    def body(u_ref, uh_ref, g_ref, gh_ref, wu_ref, wg_ref, bu_ref, bg_ref, da_ref,
             du_ref, dg_ref, dwu_ref, dwg_ref, dbu_ref, dbg_ref):
        i = pl.program_id(1)
        ut, gt = _conv_taps(u_ref, uh_ref, i), _conv_taps(g_ref, gh_ref, i)
        uc, gc = _conv_apply(ut, wu_ref, bu_ref), _conv_apply(gt, wg_ref, bg_ref)
        _, vjp_fn = jax.vjp(_gelu_gate, uc, gc)
        du, dg = vjp_fn(da_ref[...].astype(F32))
        du_ref[...] = du.astype(du_ref.dtype)
        dg_ref[...] = dg.astype(dg_ref.dtype)

        @pl.when(i == 0)
        def _():
            for r in (dwu_ref, dwg_ref, dbu_ref, dbg_ref):
                r[...] = jnp.zeros(r.shape, F32)

        for d, taps, dw_ref, db_ref in ((du, ut, dwu_ref, dbu_ref), (dg, gt, dwg_ref, dbg_ref)):
            x0, x1, x2 = taps
            dw_ref[0:1, :] += jnp.sum(d * x2, axis=0, keepdims=True)
            dw_ref[1:2, :] += jnp.sum(d * x1, axis=0, keepdims=True)
            dw_ref[2:3, :] += jnp.sum(d * x0, axis=0, keepdims=True)
            db_ref[...] += jnp.sum(d, axis=0, keepdims=True)

    tile = pl.BlockSpec((R, C), lambda j, i: (i, j))
    par = lambda rows: pl.BlockSpec((rows, C), lambda j, i: (0, j))
    return _pcall(
        body, "conv_bwd_gate", (nj, S // R), 0, in_specs=_conv_in_specs(R, C, nj) + [tile],
        out_specs=[tile, tile, par(CONV_W), par(CONV_W), par(1), par(1)],
        out_shape=[jax.ShapeDtypeStruct((S, dff), BF16), jax.ShapeDtypeStruct((S, dff), BF16),
                   jax.ShapeDtypeStruct((CONV_W, dff), F32), jax.ShapeDtypeStruct((CONV_W, dff), F32),
                   jax.ShapeDtypeStruct((1, dff), F32), jax.ShapeDtypeStruct((1, dff), F32)],
        scratch=[], sem=("parallel", "arbitrary"), args=[h, h, h, h, cw, cw, cb, cb, da], comm=comm)


def _conv_bwd_shift_call(dc, cw, into=None):
    S, dff = dc.shape
    R, C = min(1024, S), _conv_cols(dff, 1024)
    nj = dff // C
    col_off = 0 if into is None else nj
    hpr, last = R // HALO, S // HALO - 1
    ni = S // R

    def body(d_ref, nx_ref, w_ref, *rest):
        o_ref = rest[-1]
        i = pl.program_id(1)
        nxt = jnp.where(i < ni - 1, nx_ref[...].astype(F32), 0.0)
        full = jnp.concatenate([d_ref[...].astype(F32), nxt], axis=0)
        n = R + HALO
        y1, y2 = pltpu.roll(full, n - 1, 0)[:R], pltpu.roll(full, n - 2, 0)[:R]
        o_ref[...] = (full[:R] * w_ref[2:3, :] + y1 * w_ref[1:2, :] + y2 * w_ref[0:1, :]).astype(o_ref.dtype)

    in_specs = [pl.BlockSpec((R, C), lambda j, i: (i, j)),
                pl.BlockSpec((HALO, C), lambda j, i: (jnp.minimum((i + 1) * hpr, last), j)),
                pl.BlockSpec((CONV_W, C), lambda j, i: (0, j + col_off))]
    args = [dc, dc, cw]
    if into is not None:
        in_specs.append(pl.BlockSpec(memory_space=pl.ANY))
        args.append(into)
    return pl.pallas_call(
        body, name=_nm("conv_bwd_shift"), grid=(nj, ni), out_shape=jax.ShapeDtypeStruct((S, 2 * dff), BF16),
        in_specs=in_specs, out_specs=pl.BlockSpec((R, C), lambda j, i: (i, j + col_off)),
        input_output_aliases={} if into is None else {3: 0},
        compiler_params=_cp(("parallel", "parallel")),
    )(*args)


@jax.custom_vjp
def ffn_hidden(x1, x1_bf16, w3, cw, cb):
    return _ffn_hidden_fwd(x1, x1_bf16, w3, cw, cb)[0]


def _ffn_hidden_fwd(x1, x1_bf16, w3, cw, cb, comm=None):
    h = _mm(x1_bf16, w3, "nn", BF16, "up", _all_blocks(w3))
    a, *carried = _conv_fwd_call(h, cw, cb, comm)
    return (a, (x1_bf16, w3, cw, cb, h)) + ((carried,) if comm is not None else ())


def _ffn_hidden_bwd(res, da, comm=None):
    x1, w3, cw, cb, h = res
    du, dg, dwu, dwg, dbu, dbg, *carried = _conv_bwd_gate_call(h, cw, cb, da, comm)
    dh = _conv_bwd_shift_call(dg, cw, into=_conv_bwd_shift_call(du, cw))
    dx = _mm(dh, w3, "nt", F32, "up_dx", _all_blocks(w3))
    dw3 = _mm(x1, dh, "tn", w3.dtype, "up_dw", _all_blocks(w3))
    grads = (dx, jnp.zeros_like(x1), dw3, jnp.concatenate([dwu, dwg], axis=1), jnp.concatenate([dbu, dbg], axis=1))
    return (grads, carried) if comm is not None else grads


ffn_hidden.defvjp(_ffn_hidden_fwd, _ffn_hidden_bwd)


def _loss_call(y, target):
    S, D = y.shape
    tr = _row_tile(S, D)

    def body(y_ref, t_ref, sq_ref, dy_ref):
        diff = y_ref[...] - t_ref[...]
        dy_ref[...] = diff * (1.0 / D)
        part = jnp.sum(diff * diff, axis=0, keepdims=True)
        i = pl.program_id(0)

        @pl.when(i == 0)
        def _():
            sq_ref[...] = part

        @pl.when(i > 0)
        def _():
            sq_ref[...] += part

    row = pl.BlockSpec((tr, D), lambda i: (i, 0))
    return pl.pallas_call(
        body, name=_nm("loss"), grid=(S // tr,),
        out_shape=(jax.ShapeDtypeStruct((1, D), F32), jax.ShapeDtypeStruct((S, D), F32)),
        in_specs=[row, row], out_specs=(pl.BlockSpec((1, D), lambda i: (0, 0)), row),
        compiler_params=_cp(("arbitrary",)),
    )(y, target)


def _row_divisor(rows, cap=512):
    for cand in range(min(rows, cap), 15, -1):
        if rows % cand == 0 and cand % 16 == 0:
            return cand
    return rows


def _adamw_call(w, g, m, v):
    shape = w.shape
    w2, g2, m2, v2 = (a.reshape(-1, shape[-1]) for a in (w, g, m, v))
    rows, cols = w2.shape
    tr = _row_divisor(rows)

    def body(w_ref, g_ref, m_ref, v_ref, d_ref, nm_ref, nv_ref):
        g_ = g_ref[...]
        m_ = ADAM_B1 * m_ref[...] + (1.0 - ADAM_B1) * g_
        v_ = ADAM_B2 * v_ref[...] + (1.0 - ADAM_B2) * (g_ * g_)
        m_hat = m_ / (1.0 - ADAM_B1 ** ADAM_STEP)
        v_hat = v_ / (1.0 - ADAM_B2 ** ADAM_STEP)
        d_ref[...] = -ADAM_LR * (m_hat / (jnp.sqrt(v_hat) + ADAM_EPS) + ADAM_WD * w_ref[...])
        nm_ref[...] = m_
        nv_ref[...] = v_

    blk = pl.BlockSpec((tr, cols), lambda i: (i, 0))
    outs = pl.pallas_call(
        body, name=_nm("adamw"), grid=(rows // tr,),
        out_shape=tuple(jax.ShapeDtypeStruct((rows, cols), F32) for _ in range(3)),
        in_specs=[blk] * 4, out_specs=(blk,) * 3, compiler_params=_cp(("parallel",)),
    )(w2, g2, m2, v2)
    return tuple(o.reshape(shape) for o in outs)


ANY = pl.BlockSpec(memory_space=pl.ANY)


def _place():
    return lax.axis_index("x"), lax.axis_index("y"), lax.axis_index("c")


def all_gather(shards):
    n = len(shards)

    def body(*refs):
        x_refs, out_refs = refs[:n], refs[n:2 * n]
        send_sems, recv_sems, local_sems = refs[2 * n:]
        x, y, c = _place()
        me, sibling = (x, y, c), (x, y, 1 - c)
        chips = [(1 - x, y), (x, 1 - y), (1 - x, 1 - y)]

        def slot(a, px, py, pc):
            return out_refs[a].at[:, 4 * px + 2 * py + pc]

        def copy(a, k, block, to, own=False):
            return pltpu.make_async_remote_copy(
                src_ref=x_refs[a] if own else slot(a, *block), dst_ref=slot(a, *block),
                send_sem=send_sems.at[7 * a + k], recv_sem=recv_sems.at[7 * a + k], device_id=to, device_id_type=MESH)

        mine = [pltpu.make_async_copy(x_refs[a], slot(a, *me), local_sems.at[a]) for a in range(n)]
        first = []
        for a in range(n):
            mine[a].start()
            first.append(copy(a, 0, me, sibling, own=True))
            first += [copy(a, 1 + j, me, (*chip, c), own=True) for j, chip in enumerate(chips)]
        for cp in first:
            cp.start()
        passed = []
        for j, chip in enumerate(chips):
            for a in range(n):
                copy(a, 1 + j, (*chip, c), me).wait_recv()
                passed.append(copy(a, 4 + j, (*chip, c), sibling))
                passed[-1].start()
        for a in range(n):
            copy(a, 0, sibling, me).wait_recv()
        for j, chip in enumerate(chips):
            for a in range(n):
                copy(a, 4 + j, (*chip, 1 - c), me).wait_recv()
        for cp in first + passed:
            cp.wait_send()
        for cp in mine:
            cp.wait()

    return pl.pallas_call(
        body, name=_nm("all_gather"),
        out_shape=tuple(jax.ShapeDtypeStruct((s.shape[0], N_DEV) + s.shape[1:], s.dtype) for s in shards),
        in_specs=[ANY] * n, out_specs=(ANY,) * n,
        scratch_shapes=[pltpu.SemaphoreType.DMA((7 * n,)), pltpu.SemaphoreType.DMA((7 * n,)), pltpu.SemaphoreType.DMA((n,))],
    )(*shards)


def _rs_pair_exchange(gs):
    n = len(gs)

    def body(*refs):
        g_refs, recv_refs = refs[:n], refs[n:2 * n]
        send_sems, recv_sems = refs[2 * n:]
        x, y, c = _place()
        copies = [pltpu.make_async_remote_copy(
            src_ref=g_refs[a].at[:, 2 * j + (1 - c)], dst_ref=recv_refs[a].at[j], send_sem=send_sems.at[4 * a + j],
            recv_sem=recv_sems.at[4 * a + j], device_id=(x, y, 1 - c), device_id_type=MESH)
            for a in range(n) for j in range(4)]
        for cp in copies:
            cp.start()
        for cp in copies:
            cp.wait_recv()
        for cp in copies:
            cp.wait_send()

    return pl.pallas_call(
        body, name=_nm("rs_pair"),
        out_shape=tuple(jax.ShapeDtypeStruct((4, g.shape[0]) + g.shape[2:], g.dtype) for g in gs),
        in_specs=[ANY] * n, out_specs=(ANY,) * n,
        scratch_shapes=[pltpu.SemaphoreType.DMA((4 * n,)), pltpu.SemaphoreType.DMA((4 * n,))],
    )(*gs)


def _rs_chip_exchange(ps):
    n = len(ps)

    def body(*refs):
        p_refs, recv_refs = refs[:n], refs[n:2 * n]
        send_sems, recv_sems = refs[2 * n:]
        x, y, c = _place()
        chips = [(1 - x, y), (x, 1 - y), (1 - x, 1 - y)]
        copies = [pltpu.make_async_remote_copy(
            src_ref=p_refs[a].at[2 * cx + cy], dst_ref=recv_refs[a].at[k], send_sem=send_sems.at[3 * a + k],
            recv_sem=recv_sems.at[3 * a + k], device_id=(cx, cy, c), device_id_type=MESH)
            for a in range(n) for k, (cx, cy) in enumerate(chips)]
        for cp in copies:
            cp.start()
        for cp in copies:
            cp.wait_recv()
        for cp in copies:
            cp.wait_send()

    return pl.pallas_call(
        body, name=_nm("rs_chip"),
        out_shape=tuple(jax.ShapeDtypeStruct((3,) + p.shape[1:], p.dtype) for p in ps),
        in_specs=[ANY] * n, out_specs=(ANY,) * n,
        scratch_shapes=[pltpu.SemaphoreType.DMA((3 * n,)), pltpu.SemaphoreType.DMA((3 * n,))],
    )(*ps)


def _rs_pair_add(g, recv, c_idx):
    L, _, a, b = g.shape
    ta = _row_divisor(a, 1024)

    def body(c_ref, g_ref, r_ref, o_ref):
        o_ref[...] = (g_ref[...].astype(F32) + r_ref[...].astype(F32)).astype(o_ref.dtype)

    grid_spec = pltpu.PrefetchScalarGridSpec(
        num_scalar_prefetch=1, grid=(4, L, a // ta),
        in_specs=[pl.BlockSpec((None, None, ta, b), lambda j, l, i, c_ref: (l, 2 * j + c_ref[0], i, 0)),
                  pl.BlockSpec((None, None, ta, b), lambda j, l, i, c_ref: (j, l, i, 0))],
        out_specs=pl.BlockSpec((None, None, ta, b), lambda j, l, i, c_ref: (j, l, i, 0)))
    return pl.pallas_call(
        body, name=_nm("rs_pair_add"), grid_spec=grid_spec, out_shape=jax.ShapeDtypeStruct((4, L, a, b), g.dtype),
        compiler_params=_cp(("parallel", "parallel", "parallel")),
    )(c_idx, g, recv)


def _rs_final_add(p1, recv, chip_idx):
    _, L, a, b = p1.shape
    ta = _row_divisor(a, 1024)

    def body(chip_ref, p_ref, r_ref, o_ref):
        acc = p_ref[...].astype(F32)
        for k in range(3):
            acc = acc + r_ref[k].astype(F32)
        o_ref[...] = acc

    grid_spec = pltpu.PrefetchScalarGridSpec(
        num_scalar_prefetch=1, grid=(L, a // ta),
        in_specs=[pl.BlockSpec((None, None, ta, b), lambda l, i, chip_ref: (chip_ref[0], l, i, 0)),
                  pl.BlockSpec((3, None, ta, b), lambda l, i, chip_ref: (0, l, i, 0))],
        out_specs=pl.BlockSpec((None, ta, b), lambda l, i, chip_ref: (l, i, 0)))
    return pl.pallas_call(
        body, name=_nm("rs_final_add"), grid_spec=grid_spec, out_shape=jax.ShapeDtypeStruct((L, a, b), F32),
        compiler_params=_cp(("parallel", "parallel")),
    )(chip_idx, p1, recv)


def reduce_scatter(gs):
    x, y, c = _place()
    c_idx = jnp.reshape(c, (1,)).astype(jnp.int32)
    chip_idx = jnp.reshape(2 * x + y, (1,)).astype(jnp.int32)
    recv1 = _rs_pair_exchange(gs)
    p1 = [_rs_pair_add(g, r, c_idx) for g, r in zip(gs, recv1)]
    recv2 = _rs_chip_exchange(p1)
    return [_rs_final_add(p, r, chip_idx) for p, r in zip(p1, recv2)]


def all_reduce_small(v):
    r, C = v.shape

    def body(v_ref, out_ref, buf_ref, send_sems, recv_sems):
        x, y, c = _place()
        my_id = 4 * x + 2 * y + c
        buf_ref[my_id] = v_ref[...]
        copies = []
        for k in range(1, N_DEV):
            fx, fy, fc = (k >> 2) & 1, (k >> 1) & 1, k & 1
            peer = (x ^ fx, y ^ fy, c ^ fc)
            copies.append(pltpu.make_async_remote_copy(
                src_ref=v_ref, dst_ref=buf_ref.at[my_id], send_sem=send_sems.at[k - 1], recv_sem=recv_sems.at[k - 1],
                device_id=peer, device_id_type=MESH))
        for cp in copies:
            cp.start()
        for cp in copies:
            cp.wait_recv()
        for cp in copies:
            cp.wait_send()
        acc = buf_ref[0]
        for d in range(1, N_DEV):
            acc = acc + buf_ref[d]
        out_ref[...] = acc

    vm = pl.BlockSpec(memory_space=pltpu.VMEM)
    return pl.pallas_call(
        body, name=_nm("all_reduce_small"), out_shape=jax.ShapeDtypeStruct((r, C), F32),
        in_specs=[vm], out_specs=vm,
        scratch_shapes=[pltpu.VMEM((N_DEV, r, C), F32), pltpu.SemaphoreType.DMA((7,)), pltpu.SemaphoreType.DMA((7,))],
    )(v)


def _slot(ref, px, py, pc):
    return ref.at[:, 4 * px + 2 * py + pc]


def comm_gather_own(shards):
    n = len(shards)

    def build(cin, cout, send_sems, recv_sems, local_sems):
        x, y, c = _place()
        me = (x, y, c)
        direct = [(x, y, 1 - c), (1 - x, y, c), (x, 1 - y, c)]
        src_nb, dst_nb, diag = (x ^ (1 - c), y ^ c, c), (x ^ c, y ^ (1 - c), c), (1 - x, 1 - y, c)
        starts, mid_waits, mid_starts, waits = [], [], [], []
        for a in range(n):
            local = pltpu.make_async_copy(cin[a], _slot(cout[a], *me), local_sems.at[a])
            starts.append(local)
            waits.append(local.wait)
            for k, peer in enumerate(direct):
                send = pltpu.make_async_remote_copy(
                    src_ref=cin[a], dst_ref=_slot(cout[a], *me), send_sem=send_sems.at[4 * a + k],
                    recv_sem=recv_sems.at[4 * a + k], device_id=peer, device_id_type=MESH)
                arrive = pltpu.make_async_remote_copy(
                    src_ref=cin[a], dst_ref=_slot(cout[a], *peer), send_sem=send_sems.at[4 * a + k],
                    recv_sem=recv_sems.at[4 * a + k], device_id=peer, device_id_type=MESH)
                starts.append(send)
                (waits if k == 0 else mid_waits).append(arrive.wait_recv)
                waits.append(send.wait_send)
            onward = pltpu.make_async_remote_copy(
                src_ref=_slot(cout[a], *src_nb), dst_ref=_slot(cout[a], *src_nb), send_sem=send_sems.at[4 * a + 3],
                recv_sem=recv_sems.at[4 * a + 3], device_id=dst_nb, device_id_type=MESH)
            arrive = pltpu.make_async_remote_copy(
                src_ref=_slot(cout[a], *src_nb), dst_ref=_slot(cout[a], *diag), send_sem=send_sems.at[4 * a + 3],
                recv_sem=recv_sems.at[4 * a + 3], device_id=dst_nb, device_id_type=MESH)
            mid_starts.append(onward)
            waits += [arrive.wait_recv, onward.wait_send]
        return starts, mid_waits, mid_starts, waits

    out_shapes = [jax.ShapeDtypeStruct((s.shape[0], N_DEV) + s.shape[1:], s.dtype) for s in shards]
    return Comm(shards, out_shapes, {}, 4 * n, 4 * n, n, build)


def comm_gather_pass(partial):
    n = len(partial)

    def build(cin, cout, send_sems, recv_sems, local_sems):
        x, y, c = _place()
        chips = [(1 - x, y), (x, 1 - y), (1 - x, 1 - y)]
        starts, waits = [], []
        for a in range(n):
            for j, chip in enumerate(chips):
                send = pltpu.make_async_remote_copy(
                    src_ref=_slot(cout[a], *chip, c), dst_ref=_slot(cout[a], *chip, c), send_sem=send_sems.at[3 * a + j],
                    recv_sem=recv_sems.at[3 * a + j], device_id=(x, y, 1 - c), device_id_type=MESH)
                arrive = pltpu.make_async_remote_copy(
                    src_ref=_slot(cout[a], *chip, c), dst_ref=_slot(cout[a], *chip, 1 - c),
                    send_sem=send_sems.at[3 * a + j], recv_sem=recv_sems.at[3 * a + j],
                    device_id=(x, y, 1 - c), device_id_type=MESH)
                starts.append(send)
                waits += [arrive.wait_recv, send.wait_send]
        return starts, waits

    out_shapes = [jax.ShapeDtypeStruct(p.shape, p.dtype) for p in partial]
    return Comm(partial, out_shapes, {a: a for a in range(n)}, 3 * n, 3 * n, 1, build)


def comm_rs_pair(gs):
    n = len(gs)

    def build(cin, cout, send_sems, recv_sems, local_sems):
        x, y, c = _place()
        starts, waits = [], []
        for a in range(n):
            for j in range(4):
                cp = pltpu.make_async_remote_copy(
                    src_ref=cin[a].at[:, 2 * j + (1 - c)], dst_ref=cout[a].at[j], send_sem=send_sems.at[4 * a + j],
                    recv_sem=recv_sems.at[4 * a + j], device_id=(x, y, 1 - c), device_id_type=MESH)
                starts.append(cp)
                waits += [cp.wait_recv, cp.wait_send]
        return starts, waits

    out_shapes = [jax.ShapeDtypeStruct((4, g.shape[0]) + g.shape[2:], g.dtype) for g in gs]
    return Comm(gs, out_shapes, {}, 4 * n, 4 * n, 1, build)


def comm_rs_chip(ps):
    n = len(ps)

    def build(cin, cout, send_sems, recv_sems, local_sems):
        x, y, c = _place()
        chips = [(1 - x, y), (x, 1 - y), (1 - x, 1 - y)]
        starts, waits = [], []
        for a in range(n):
            for k, (cx, cy) in enumerate(chips):
                cp = pltpu.make_async_remote_copy(
                    src_ref=cin[a].at[2 * cx + cy], dst_ref=cout[a].at[k], send_sem=send_sems.at[3 * a + k],
                    recv_sem=recv_sems.at[3 * a + k], device_id=(cx, cy, c), device_id_type=MESH)
                starts.append(cp)
                waits += [cp.wait_recv, cp.wait_send]
        return starts, waits

    out_shapes = [jax.ShapeDtypeStruct((3,) + p.shape[1:], p.dtype) for p in ps]
    return Comm(ps, out_shapes, {}, 3 * n, 3 * n, 1, build)


def _pack(arrays, dtype, row_align):
    lead = arrays[0].shape[:-1]
    quantum = row_align * PACK_COLS
    parts, sizes = [], []
    for a in arrays:
        n = a.shape[-1]
        padded = _round_up(n, quantum)
        a = a.astype(dtype)
        if padded != n:
            a = jnp.pad(a, [(0, 0)] * len(lead) + [(0, padded - n)])
        parts.append(a.reshape(*lead, padded // PACK_COLS, PACK_COLS))
        sizes.append((n, padded // PACK_COLS))
    return jnp.concatenate(parts, axis=len(lead)), sizes


def _unpack(packed, sizes):
    lead = packed.shape[:-2]
    out, row = [], 0
    for n, rows in sizes:
        part = lax.slice_in_dim(packed, row, row + rows, axis=len(lead))
        out.append(part.reshape(*lead, rows * PACK_COLS)[..., :n])
        row += rows
    return out


def _unshard(gathered, axis):
    _, L, a, b = gathered.shape
    if axis == 1:
        return [gathered[:, l].reshape(N_DEV * a, b) for l in range(L)]
    return [jnp.transpose(gathered[:, l], (1, 0, 2)).reshape(a, N_DEV * b) for l in range(L)]


def _reshard(fulls, axis):
    blocks = []
    for f in fulls:
        A, B = f.shape
        if axis == 1:
            blocks.append(f.reshape(N_DEV, A // N_DEV, B))
        else:
            blocks.append(jnp.transpose(f.reshape(A, N_DEV, B // N_DEV), (1, 0, 2)))
    return jnp.stack(blocks, axis=1)


def _as3(a):
    return a if a.ndim == 3 else a[:, None, :]


def _prep_big(name, w, dims):
    w = w.astype(BF16)
    L, a, b = w.shape
    if name == 'mla_w_in':
        return jnp.pad(w, ((0, 0), (0, 0), (0, dims['h_width'] - b)))
    if name == 'mla_w_uq':
        hd = MLA_NOPE + MLA_ROPE
        w = jnp.pad(w.reshape(L, a, b // hd, hd), ((0, 0), (0, 0), (0, 0), (0, MLA_QK_PAD - hd)))
        return w.reshape(L, a, b // hd * MLA_QK_PAD)
    if name == 'ffn_w_up':
        return jnp.pad(w, ((0, 0), (0, 0), (0, _round_up(b, CONV_COLS) - b)))
    return w


def _unprep_big(name, g, shape):
    L, a, b = shape
    if name == 'mla_w_uq':
        hd = MLA_NOPE + MLA_ROPE
        return g.reshape(L, a, b // hd, MLA_QK_PAD)[..., :hd].reshape(L, a, b)
    return g[:, :, :b]


def _rope_tables(positions):
    inv = 1.0 / (ROPE_THETA ** (jnp.arange(0, MLA_ROPE, 2, dtype=F32) / MLA_ROPE))
    ang = positions.astype(F32)[:, None] * inv
    cos, sin = jnp.cos(ang), jnp.sin(ang)
    one, zero = jnp.ones_like(cos), jnp.zeros_like(cos)
    a = jnp.concatenate([cos, cos, one, one], axis=1)
    up = jnp.concatenate([sin, zero, zero, zero], axis=1)
    down = jnp.concatenate([zero, sin, zero, zero], axis=1)
    return (a, -up, down), (a, up, -down)


def _rows_full(w):
    return w.reshape(w.shape[0] * w.shape[1], w.shape[2])


def _ops(depth):
    alpha = (2 * depth) ** 0.25
    return {'ln_res': rw_op(_ln_res_fn(alpha), "ln_res", 2, [F32], shadow=True),
            'ple': rw_op(_ple_fn, "ple", 3, [F32], shadow=True), 'gla_gate': rw_op(_gla_gate_fn, "gla_gate", 1, [F32])}


MLA_PRE_W = ['mla_w_in', 'mla_q_norm', 'mla_kv_norm', 'mla_w_uq', 'mla_w_uk', 'mla_w_uv']
FFN_IN_W = ['mla_w_o', 'ln1_g', 'ln1_b', 'ffn_conv_w', 'ffn_conv_b']
FFN_OUT_W = ['ffn_w_down', 'ln2_g', 'ln2_b', 'ple_w_gate', 'ple_w_proj', 'ple_b_gate']


def _mla_heads(wl, j):
    return N_DEV * wl['mla_w_uk'][j].shape[2] // MLA_NOPE


def _mla_pre(x, xb, wl, j):
    w_uq, w_uk, w_uv = wl['mla_w_uq'][j], wl['mla_w_uk'][j], wl['mla_w_uv'][j]
    h = linear_sh(x, xb, _rows_full(wl['mla_w_in'][j]), F32)
    mla_norm = rw_op(_mla_norm_fn(w_uq.shape[1], w_uk.shape[1]), "mla_norm", 1, [BF16, BF16, F32])
    cq, ckv, kr_raw = mla_norm(h, wl['mla_q_norm'][j], wl['mla_kv_norm'][j])
    return linear(cq, w_uq, BF16), linear(ckv, w_uk, BF16), linear(ckv, w_uv, BF16), kr_raw


def _gla_mixer(x, xb, wl, j, ops):
    w_in3, w_a2 = wl['gla_w_in'][j], wl['gla_w_a2'][j]
    w_o = _rows_full(wl['gla_w_o'][j])
    w_in = jnp.transpose(w_in3, (1, 0, 2)).reshape(w_in3.shape[1], N_DEV * w_in3.shape[2])
    heads = w_o.shape[0] // GLA_DV
    n_main = 2 * heads * GLA_DK + 2 * heads * GLA_DV
    w_a = jnp.pad(w_in[:, n_main:], ((0, 0), (0, LANES - GLA_RANK)))
    w_a2_p = jnp.pad(w_a2, ((0, LANES - GLA_RANK), (0, 0))).astype(BF16)
    hm = linear_sh(x, xb, w_in[:, :n_main], BF16)
    ha = linear_sh(x, xb, w_a, BF16)
    (la,) = ops['gla_gate'](linear(ha, w_a2_p, F32), wl['gla_b_a'][j])
    return linear(gla_core(hm, la, wl['gla_o_norm'][j], heads), w_o, F32)


def _ffn_in(x, m, wl, i, ops, bp):
    x1, x1b = ops['ln_res'](x, m, wl['ln1_g'][i], wl['ln1_b'][i])
    cw, cb = wl['ffn_conv_w'][i], wl['ffn_conv_b'][i]
    bu = cw.shape[1] // N_DEV
    cwp = jnp.pad(cw.reshape(CONV_W, N_DEV, bu), ((0, 0), (0, 0), (0, bp - bu))).reshape(CONV_W, N_DEV * bp)
    cbp = jnp.pad(cb.reshape(1, N_DEV, bu), ((0, 0), (0, 0), (0, bp - bu))).reshape(1, N_DEV * bp)
    return x1, lax.stop_gradient(x1b), cwp, cbp


def _ffn_out(x1, a, wl, p_i, i, ops):
    w_down3 = wl['ffn_w_down'][i]
    half, bu, d_model = N_DEV // 2, 2 * w_down3.shape[1], w_down3.shape[2]
    bp = a.shape[1] // half
    w_down = jnp.pad(w_down3.reshape(half, bu, d_model), ((0, 0), (0, bp - bu), (0, 0))).reshape(half * bp, d_model)
    f = linear(a, w_down, F32)
    x2, x2b = ops['ln_res'](x1, f, wl['ln2_g'][i], wl['ln2_b'][i])
    glog = linear_sh(x2, lax.stop_gradient(x2b), _rows_full(wl['ple_w_gate'][i]), F32)
    pp = linear(p_i, wl['ple_w_proj'][i], F32)
    x, xb = ops['ple'](x2, glog, pp, wl['ple_b_gate'][i])
    return x, lax.stop_gradient(xb)


def _layer(x, xb, wl, p_i, aux, i, ops):
    j = i // 2
    if i % 2 == 0:
        q_raw, kn, v, kr_raw = _mla_pre(x, xb, wl, j)
        o = make_attention(aux, _mla_heads(wl, j))[0](q_raw, kn, v, kr_raw)
        m = linear(o, _rows_full(wl['mla_w_o'][j]), F32)
    else:
        m = _gla_mixer(x, xb, wl, j, ops)
    x1, x1b, cwp, cbp = _ffn_in(x, m, wl, i, ops, wl['ffn_w_up'][i].shape[2])
    return _ffn_out(x1, ffn_hidden(x1, x1b, wl['ffn_w_up'][i], cwp, cbp), wl, p_i, i, ops)


def kernel(x, p, positions, mla_w_in, mla_q_norm, mla_kv_norm, mla_w_uq, mla_w_uk, mla_w_uv, mla_w_o, gla_w_in, gla_w_a2, gla_b_a, gla_o_norm, gla_w_o, ln1_g, ln1_b, ln2_g, ln2_b, ffn_w_up, ffn_conv_w, ffn_conv_b, ffn_w_down, ple_w_proj, ple_w_gate, ple_b_gate, loss_target, m_mla_w_in, m_mla_q_norm, m_mla_kv_norm, m_mla_w_uq, m_mla_w_uk, m_mla_w_uv, m_mla_w_o, m_gla_w_in, m_gla_w_a2, m_gla_b_a, m_gla_o_norm, m_gla_w_o, m_ln1_g, m_ln1_b, m_ln2_g, m_ln2_b, m_ffn_w_up, m_ffn_conv_w, m_ffn_conv_b, m_ffn_w_down, m_ple_w_proj, m_ple_w_gate, m_ple_b_gate, v_mla_w_in, v_mla_q_norm, v_mla_kv_norm, v_mla_w_uq, v_mla_w_uk, v_mla_w_uv, v_mla_w_o, v_gla_w_in, v_gla_w_a2, v_gla_b_a, v_gla_o_norm, v_gla_w_o, v_ln1_g, v_ln1_b, v_ln2_g, v_ln2_b, v_ffn_w_up, v_ffn_conv_w, v_ffn_conv_b, v_ffn_w_down, v_ple_w_proj, v_ple_w_gate, v_ple_b_gate):
    w = dict(zip(WEIGHTS, (mla_w_in, mla_q_norm, mla_kv_norm, mla_w_uq, mla_w_uk, mla_w_uv, mla_w_o, gla_w_in, gla_w_a2,
                           gla_b_a, gla_o_norm, gla_w_o, ln1_g, ln1_b, ln2_g, ln2_b, ffn_w_up, ffn_conv_w, ffn_conv_b,
                           ffn_w_down, ple_w_proj, ple_w_gate, ple_b_gate)))
    m_in = dict(zip(WEIGHTS, (m_mla_w_in, m_mla_q_norm, m_mla_kv_norm, m_mla_w_uq, m_mla_w_uk, m_mla_w_uv, m_mla_w_o,
                              m_gla_w_in, m_gla_w_a2, m_gla_b_a, m_gla_o_norm, m_gla_w_o, m_ln1_g, m_ln1_b, m_ln2_g,
                              m_ln2_b, m_ffn_w_up, m_ffn_conv_w, m_ffn_conv_b, m_ffn_w_down, m_ple_w_proj, m_ple_w_gate,
                              m_ple_b_gate)))
    v_in = dict(zip(WEIGHTS, (v_mla_w_in, v_mla_q_norm, v_mla_kv_norm, v_mla_w_uq, v_mla_w_uk, v_mla_w_uv, v_mla_w_o,
                              v_gla_w_in, v_gla_w_a2, v_gla_b_a, v_gla_o_norm, v_gla_w_o, v_ln1_g, v_ln1_b, v_ln2_g,
                              v_ln2_b, v_ffn_w_up, v_ffn_conv_w, v_ffn_conv_b, v_ffn_w_down, v_ple_w_proj, v_ple_w_gate,
                              v_ple_b_gate)))
    _uid[0] = itertools.count()
    x2d, target, pos = x[0], loss_target[0], positions[0]
    p3 = p[:, 0]
    dims = {'h_width': mla_w_uq.shape[1] + mla_w_uk.shape[1] + LANES}

    depth = ln1_g.shape[0]
    ops = _ops(depth)
    cid = pos // CHUNK
    aux = {'rope': _rope_tables(pos), 'cidq': cid[:, None], 'cidk': cid[None, :],
           'need': _mask_table(cid, _attn_tile(pos.shape[0]))}

    in_layer0 = set(MLA_PRE_W + FFN_IN_W + FFN_OUT_W + ['ffn_w_up'])
    prepped = {n: _prep_big(n, w[n], dims) for n in BIG}
    first_names = [n for n in BIG if n in MLA_PRE_W + ['mla_w_o']]
    mid_names = [n for n in BIG if n in in_layer0 and n not in first_names]
    rest_names = [n for n in BIG if prepped[n].shape[0] > (1 if n in in_layer0 else 0)]
    rest_from = {n: (1 if n in in_layer0 else 0) for n in rest_names}
    small3 = [_as3(w[n]) for n in SMALL]
    small_packed, small_sizes = _pack([s.reshape(1, -1) for s in small3], F32, 8)
    first = all_gather([prepped[n][:1] for n in first_names] + [small_packed])
    wl = {n: [None] * prepped[n].shape[0] for n in BIG}
    for n, g in zip(first_names, first):
        wl[n][0] = g[0]
    for n, s3, flat in zip(SMALL, small3, _unpack(first[-1][0], small_sizes)):
        wl[n] = _unshard(flat.reshape(N_DEV, *s3.shape), SHARD_AXIS[n] if w[n].ndim == 3 else 2)
    for n in REPL:
        wl[n] = [w[n][l][None, :] for l in range(w[n].shape[0])]

    def pick(names, layer0):
        return {n: [wl[n][l] if (l == 0 and n in in_layer0) == layer0 else None for l in range(len(wl[n]))] for n in names}

    heads0 = _mla_heads(wl, 0)
    _, attn_run, attn_bwd = make_attention(aux, heads0)
    x2db = x2d.astype(BF16)
    pre, vjp_pre = jax.vjp(lambda x_, wl_: _mla_pre(x_, x2db, wl_, 0), x2d, pick(MLA_PRE_W, True))
    o, attn_res, partial = attn_run(*pre, comm_gather_own([prepped[n][:1] for n in mid_names]
                                                          + [prepped[n][rest_from[n]:] for n in rest_names]))
    for n, g in zip(mid_names, _run_comm(comm_gather_pass(partial[:len(mid_names)]), "gather_pass")):
        wl[n][0] = g[0]
    partial = partial[len(mid_names):]
    w_up0 = wl['ffn_w_up'][0]

    def ffn_in(x_, o_, wl_):
        x1_, x1b_, cwp_, cbp_ = _ffn_in(x_, linear(o_, _rows_full(wl_['mla_w_o'][0]), F32), wl_, 0, ops, w_up0.shape[2])
        return (x1_, cwp_, cbp_), x1b_

    (x1, cwp, cbp), vjp_in, x1b = jax.vjp(ffn_in, x2d, o, pick(FFN_IN_W, True), has_aux=True)
    a, ffn_res, rest = _ffn_hidden_fwd(x1, x1b, w_up0, cwp, cbp, comm_gather_pass(partial))
    for n, g in zip(rest_names, rest):
        for l in range(g.shape[0]):
            wl[n][rest_from[n] + l] = g[l]
    x_l0, vjp_out, x_l0b = jax.vjp(lambda x1_, a_, wl_: _ffn_out(x1_, a_, wl_, p3[0], 0, ops), x1, a,
                                   pick(FFN_OUT_W, True), has_aux=True)

    def tail(x_, wl_):
        xb_ = x_l0b
        for i in range(1, depth):
            x_, xb_ = _layer(x_, xb_, wl_, p3[i], aux, i, ops)
        return x_

    y, vjp_tail = jax.vjp(tail, x_l0, pick(WEIGHTS, False))
    sq, dy = _loss_call(y, target)

    dwl = {n: [None] * len(wl[n]) for n in WEIGHTS}

    def keep(part):
        for n, per_layer in part.items():
            for l, g in enumerate(per_layer):
                if g is not None:
                    dwl[n][l] = g

    x_c, y_c, c_place = _place()
    c_idx = jnp.reshape(c_place, (1,)).astype(jnp.int32)
    chip_idx = jnp.reshape(2 * x_c + y_c, (1,)).astype(jnp.int32)
    dx_l0, d_tail = vjp_tail(dy)
    keep(d_tail)
    g_rest = [jnp.stack(dwl[n][rest_from[n]:], axis=0) for n in rest_names]
    dx1_out, da, d_out = vjp_out(dx_l0)
    keep(d_out)
    (dx1_ffn, _, dw_up0, dcwp, dcbp), recv1 = _ffn_hidden_bwd(ffn_res, da, comm_rs_pair(g_rest))
    dwl['ffn_w_up'][0] = dw_up0
    g_mid = [dwl[n][0][None] for n in mid_names]
    g_rest, recv1 = g_rest + g_mid, list(recv1) + list(_rs_pair_exchange(g_mid))
    p1 = [_rs_pair_add(g, r, c_idx) for g, r in zip(g_rest, recv1)]
    dx_in, do, d_in = vjp_in((dx1_out + dx1_ffn, dcwp, dcbp))
    keep(d_in)
    d_pre_in, recv2 = attn_bwd(attn_res, do, comm_rs_chip(p1))
    red_rest = [_rs_final_add(p_, r, chip_idx) for p_, r in zip(p1, recv2)]
    dx_pre, d_pre = vjp_pre(d_pre_in)
    keep(d_pre)
    dx = dx_pre + dx_in

    small_blocks = [_reshard(dwl[n], SHARD_AXIS[n] if w[n].ndim == 3 else 2).reshape(N_DEV, -1) for n in SMALL]
    small_grad_packed, _ = _pack(small_blocks, F32, 8)
    red_first = reduce_scatter([dwl[n][0][None] for n in first_names] + [small_grad_packed[None]])
    by_layer = {n: [] for n in BIG}
    for n, g in zip(first_names, red_first):
        by_layer[n].append(g)
    for n, g in zip(mid_names, red_rest[len(rest_names):]):
        by_layer[n].append(g)
    for n, g in zip(rest_names, red_rest):
        by_layer[n].append(g)
    grads = {n: _unprep_big(n, jnp.concatenate(by_layer[n], axis=0), w[n].shape) for n in BIG}
    for n, f in zip(SMALL, _unpack(red_first[-1][0], small_sizes)):
        grads[n] = f.reshape(w[n].shape)

    repl_flat = [jnp.concatenate([g.reshape(-1) for g in dwl[n]]).reshape(1, -1) for n in REPL]
    loss_part = 0.5 * jnp.sum(sq) / sq.shape[1]
    packed, repl_sizes = _pack(repl_flat + [loss_part.reshape(1, 1)], F32, 8)
    summed = _unpack(all_reduce_small(packed[0])[None], repl_sizes)
    for n, f in zip(REPL, summed[:-1]):
        grads[n] = f.reshape(w[n].shape)
    loss = summed[-1].reshape(())

    delta, new_m, new_v = {}, {}, {}
    for n in WEIGHTS:
        delta[n], new_m[n], new_v[n] = _adamw_call(w[n], grads[n], m_in[n], v_in[n])
    return (loss, dx[None], *[grads[n] for n in WEIGHTS], *[delta[n] for n in WEIGHTS],
            *[new_m[n] for n in WEIGHTS], *[new_v[n] for n in WEIGHTS])
```

```python
import functools
import itertools

import jax
import jax.numpy as jnp
from jax import lax
from jax.experimental import pallas as pl
from jax.experimental.pallas import tpu as pltpu

F32 = jnp.float32
BF16 = jnp.bfloat16
MESH = pl.DeviceIdType.MESH
N_DEV = 8

EPS = 1e-5
NEG_INF = -1e30
CHUNK = 64
Q_BLOCK = 128
MLA_NOPE = 128
MLA_ROPE = 64
MLA_V = 128
MLA_QK_PAD = 256
ROPE_THETA = 10000.0
GLA_DK = 128
GLA_DV = 256
GLA_RANK = 16
GLA_TAU = 16.0
CONV_W = 3
ADAM_LR = 0.001
ADAM_B1 = 0.9
ADAM_B2 = 0.999
ADAM_EPS = 1e-08
ADAM_WD = 0.01
ADAM_STEP = 10
LOG2E = 1.4426950408889634

LANES = 128
PACK_COLS = 1024
VMEM_LIMIT = 56 * 1024 * 1024
MM_VMEM_BUDGET = 38 * 1024 * 1024

WEIGHTS = ['mla_w_in', 'mla_q_norm', 'mla_kv_norm', 'mla_w_uq', 'mla_w_uk', 'mla_w_uv', 'mla_w_o', 'gla_w_in',
           'gla_w_a2', 'gla_b_a', 'gla_o_norm', 'gla_w_o', 'ln1_g', 'ln1_b', 'ln2_g', 'ln2_b', 'ffn_w_up',
           'ffn_conv_w', 'ffn_conv_b', 'ffn_w_down', 'ple_w_proj', 'ple_w_gate', 'ple_b_gate']
SHARD_AXIS = {'mla_w_in': 1, 'mla_q_norm': None, 'mla_kv_norm': None, 'mla_w_uq': 2, 'mla_w_uk': 2, 'mla_w_uv': 2,
              'mla_w_o': 1, 'gla_w_in': 2, 'gla_w_a2': 2, 'gla_b_a': 1, 'gla_o_norm': 1, 'gla_w_o': 1,
              'ln1_g': None, 'ln1_b': None, 'ln2_g': None, 'ln2_b': None, 'ffn_w_up': 2, 'ffn_conv_w': 2,
              'ffn_conv_b': None, 'ffn_w_down': 1, 'ple_w_proj': 2, 'ple_w_gate': 1, 'ple_b_gate': None}
BIG = ['mla_w_in', 'mla_w_uq', 'mla_w_uk', 'mla_w_uv', 'mla_w_o', 'gla_w_in', 'gla_w_o', 'ffn_w_up', 'ffn_w_down',
       'ple_w_proj', 'ple_w_gate']
SMALL = ['gla_w_a2', 'gla_b_a', 'gla_o_norm', 'ffn_conv_w']
REPL = [n for n in WEIGHTS if SHARD_AXIS[n] is None]

_uid = [itertools.count()]


def _nm(base):
    return f"{base}_{next(_uid[0])}"


def _cp(sem=None):
    return pltpu.CompilerParams(dimension_semantics=sem, vmem_limit_bytes=VMEM_LIMIT)


def _round_up(n, m):
    return -(-n // m) * m


class Comm:
    MID_AT = 0.6

    def __init__(self, inputs, out_shapes, aliases, n_send, n_recv, n_local, build):
        self.inputs, self.out_shapes, self.aliases = list(inputs), list(out_shapes), dict(aliases)
        self.n_send, self.n_recv, self.n_local, self.build = n_send, n_recv, n_local, build


def _pcall(body, name, grid, n_prefetch, in_specs, out_specs, out_shape, scratch, sem, args, comm=None):
    in_specs, out_specs, out_shape, scratch, args = list(in_specs), list(out_specs), list(out_shape), list(scratch), list(args)
    n_in, n_out, n_scr = len(in_specs), len(out_specs), len(scratch)
    aliases = {}
    kernel_body = body
    if comm is not None:
        ci, co = len(comm.inputs), len(comm.out_shapes)
        any_spec = pl.BlockSpec(memory_space=pl.ANY)
        in_specs += [any_spec] * ci
        out_specs += [any_spec] * co
        out_shape += comm.out_shapes
        scratch += [pltpu.SemaphoreType.DMA((comm.n_send,)), pltpu.SemaphoreType.DMA((comm.n_recv,)),
                    pltpu.SemaphoreType.DMA((comm.n_local,))]
        aliases = {n_prefetch + n_in + k: n_out + v for k, v in comm.aliases.items()}
        args += comm.inputs
        sem = ("arbitrary",) * len(grid)

        def kernel_body(*refs):
            pre, r = refs[:n_prefetch], refs[n_prefetch:]
            ins, cin = r[:n_in], r[n_in:n_in + ci]
            outs, cout = r[n_in + ci:n_in + ci + n_out], r[n_in + ci + n_out:n_in + ci + n_out + co]
            scr = r[n_in + ci + n_out + co:n_in + ci + n_out + co + n_scr]
            send_sems, recv_sems, local_sems = r[-3:]
            first = functools.reduce(lambda a, b: a & b, [pl.program_id(d) == 0 for d in range(len(grid))])
            last = functools.reduce(lambda a, b: a & b, [pl.program_id(d) == grid[d] - 1 for d in range(len(grid))])
            built = comm.build(cin, cout, send_sems, recv_sems, local_sems)
            starts, waits = built[0], built[-1]
            mid_waits, mid_starts = (built[1], built[2]) if len(built) == 4 else ([], [])

            @pl.when(first)
            def _():
                for cp in starts:
                    cp.start()

            if mid_starts:
                step = pl.program_id(0)
                for d in range(1, len(grid)):
                    step = step * grid[d] + pl.program_id(d)
                n_steps = functools.reduce(lambda a, b: a * b, grid)

                @pl.when(step == int(n_steps * comm.MID_AT))
                def _():
                    for wait in mid_waits:
                        wait()
                    for cp in mid_starts:
                        cp.start()

            body(*pre, *ins, *outs, *scr)

            @pl.when(last)
            def _():
                for wait in waits:
                    wait()

    grid_spec = pltpu.PrefetchScalarGridSpec(num_scalar_prefetch=n_prefetch, grid=grid, in_specs=in_specs,
                                             out_specs=out_specs, scratch_shapes=scratch)
    return pl.pallas_call(kernel_body, name=_nm(name), grid_spec=grid_spec, out_shape=tuple(out_shape),
                          input_output_aliases=aliases, compiler_params=_cp(sem))(*args)


def _run_comm(comm, name):
    return _pcall(lambda: None, name, (1,), 0, [], [], [], [], ("arbitrary",), [], comm)


def _divisor_tiles(n, cap):
    if n % LANES:
        return [n]
    out = [t for t in range(LANES, min(n, cap) + 1, LANES) if n % t == 0]
    return out or [n]


def _mm_tiles(M, N, K, abytes, bbytes, obytes, tn_fixed=None, tk_fixed=None):
    best = None
    for tm in _divisor_tiles(M, 2048):
        for tn in ([tn_fixed] if tn_fixed else _divisor_tiles(N, 1536)):
            for tk in ([tk_fixed] if tk_fixed else _divisor_tiles(K, 4096)):
                vmem = 2 * (tm * tk * abytes + tk * tn * bbytes + tm * tn * obytes) + tm * tn * 4
                if vmem > MM_VMEM_BUDGET:
                    continue
                key = (tm * tn * tk, tk)
                if best is None or key > best[0]:
                    best = (key, (tm, tn, tk))
    assert best is not None, (M, N, K)
    return best[1]


def _mm(a, b, mode, out_dtype, base="mm", blocks=None):
    blk0, nblk = blocks if blocks else (0, 1)
    tn_fixed = tk_fixed = None
    if mode == "nn":
        M, K = a.shape
        N = nblk * b.shape[2] if blocks else b.shape[1]
        tn_fixed = b.shape[2] if blocks else None
    elif mode == "nt":
        M, K = a.shape
        N = b.shape[1] if blocks else b.shape[0]
        kb = 2 if blocks and nblk % 2 == 0 and blk0 % 2 == 0 else 1
        tk_fixed = kb * b.shape[2] if blocks else None
        assert not blocks or K == nblk * b.shape[2]
    else:
        (K, M), N = a.shape, b.shape[1]
        tn_fixed = N // nblk if blocks else None
    tm, tn, tk = _mm_tiles(M, N, K, a.dtype.itemsize, b.dtype.itemsize, jnp.dtype(out_dtype).itemsize, tn_fixed, tk_fixed)
    nk = K // tk
    out_shape = jax.ShapeDtypeStruct((M, N), out_dtype)
    out_spec = pl.BlockSpec((tm, tn), lambda i, j, k: (i, j))
    if mode == "nn":
        a_spec = pl.BlockSpec((tm, tk), lambda i, j, k: (i, k))
        b_spec = (pl.BlockSpec((None, tk, tn), lambda i, j, k: (blk0 + j, k, 0)) if blocks
                  else pl.BlockSpec((tk, tn), lambda i, j, k: (k, j)))
        dims = (((1,), (0,)), ((), ()))
    elif mode == "nt":
        a_spec = pl.BlockSpec((tm, tk), lambda i, j, k: (i, k))
        b_spec = (pl.BlockSpec((kb, tn, tk // kb), lambda i, j, k: (blk0 // kb + k, j, 0)) if blocks
                  else pl.BlockSpec((tn, tk), lambda i, j, k: (j, k)))
        dims = (((1,), (1,)), ((), ()))
    else:
        a_spec = pl.BlockSpec((tk, tm), lambda i, j, k: (k, i))
        b_spec = pl.BlockSpec((tk, tn), lambda i, j, k: (k, j))
        dims = (((0,), (0,)), ((), ()))
        if blocks:
            out_shape = jax.ShapeDtypeStruct((nblk, M, tn), out_dtype)
            out_spec = pl.BlockSpec((None, tm, tn), lambda i, j, k: (j, i, 0))

    def body(a_ref, b_ref, o_ref, acc_ref):
        if mode == "nt" and blocks:
            bn = tk // kb
            part = sum(lax.dot_general(a_ref[:, q * bn:(q + 1) * bn].astype(BF16), b_ref[q].astype(BF16), dims,
                                       preferred_element_type=F32) for q in range(kb))
        else:
            part = lax.dot_general(a_ref[...].astype(BF16), b_ref[...].astype(BF16), dims, preferred_element_type=F32)
        if nk == 1:
            o_ref[...] = part.astype(o_ref.dtype)
        else:
            k = pl.program_id(2)

            @pl.when(k == 0)
            def _():
                acc_ref[...] = part

            @pl.when(k > 0)
            def _():
                acc_ref[...] += part

            @pl.when(k == nk - 1)
            def _():
                o_ref[...] = acc_ref[...].astype(o_ref.dtype)

    return pl.pallas_call(
        body, name=_nm(base), grid=(M // tm, N // tn, nk), out_shape=out_shape,
        in_specs=[a_spec, b_spec], out_specs=out_spec,
        scratch_shapes=[pltpu.VMEM((tm, tn) if nk > 1 else (8, LANES), F32)],
        compiler_params=_cp(("parallel", "parallel", "arbitrary")),
    )(a, b)


def _all_blocks(w):
    return (0, w.shape[0]) if w.ndim == 3 else None


@functools.partial(jax.custom_vjp, nondiff_argnums=(2,))
def linear(a, w, out_dtype):
    return _mm(a, w, "nn", out_dtype, "lin_fwd", _all_blocks(w))


def _linear_fwd(a, w, out_dtype):
    return _mm(a, w, "nn", out_dtype, "lin_fwd", _all_blocks(w)), (a, w)


def _linear_bwd(out_dtype, res, dy):
    a, w = res
    return (_mm(dy, w, "nt", a.dtype, "lin_dx", _all_blocks(w)), _mm(a, dy, "tn", w.dtype, "lin_dw", _all_blocks(w)))


linear.defvjp(_linear_fwd, _linear_bwd)


@functools.partial(jax.custom_vjp, nondiff_argnums=(3,))
def linear_sh(a, a_bf16, w, out_dtype):
    return _mm(a_bf16, w, "nn", out_dtype, "lin_fwd", _all_blocks(w))


def _linear_sh_fwd(a, a_bf16, w, out_dtype):
    return _mm(a_bf16, w, "nn", out_dtype, "lin_fwd", _all_blocks(w)), (a_bf16, w, jnp.zeros((), a.dtype))


def _linear_sh_bwd(out_dtype, res, dy):
    a_bf16, w, tok = res
    return (_mm(dy, w, "nt", tok.dtype, "lin_dx", _all_blocks(w)), jnp.zeros_like(a_bf16),
            _mm(a_bf16, dy, "tn", w.dtype, "lin_dw", _all_blocks(w)))


linear_sh.defvjp(_linear_sh_fwd, _linear_sh_bwd)


def _row_tile(S, width, forward=False):
    tr = 512 if width <= 1024 else 256
    return min(2 * tr if forward else tr, S)


def _rw_fwd(f, rows, params, out_dtypes, base):
    S = rows[0][0].shape[0]
    tr = _row_tile(S, max(w for _, w, _ in rows), forward=True)
    n_in = len(rows) + len(params)
    avals = [jax.ShapeDtypeStruct((tr, w), F32) for _, w, _ in rows] + [jax.ShapeDtypeStruct(p.shape, F32) for p in params]
    outs = jax.eval_shape(f, *avals)

    def body(*refs):
        vals = [r[...].astype(F32) for r in refs[:n_in]]
        for o_ref, r in zip(refs[n_in:], f(*vals)):
            o_ref[...] = r.astype(o_ref.dtype)

    in_specs = [pl.BlockSpec((tr, w), functools.partial(lambda i, cb: (i, cb), cb=cb)) for _, w, cb in rows]
    in_specs += [pl.BlockSpec(p.shape, lambda i: (0, 0)) for p in params]
    return pl.pallas_call(
        body, name=_nm(base), grid=(S // tr,),
        out_shape=tuple(jax.ShapeDtypeStruct((S, o.shape[1]), dt) for o, dt in zip(outs, out_dtypes)),
        in_specs=in_specs, out_specs=tuple(pl.BlockSpec((tr, o.shape[1]), lambda i: (i, 0)) for o in outs),
        compiler_params=_cp(("parallel",)),
    )(*[a for a, _, _ in rows], *params)


def _rw_bwd(f, rows, params, cts, row_grad_dtypes, base):
    S = rows[0][0].shape[0]
    tr = _row_tile(S, max(w for _, w, _ in rows))
    n_rows, n_par, n_ct = len(rows), len(params), len(cts)
    want = [k for k, dt in enumerate(row_grad_dtypes) if dt is not None]

    def body(*refs):
        in_refs = refs[:n_rows + n_par]
        ct_refs = refs[n_rows + n_par:n_rows + n_par + n_ct]
        out_refs = refs[n_rows + n_par + n_ct:]
        vals = [r[...].astype(F32) for r in in_refs]
        _, vjp_fn = jax.vjp(f, *vals)
        grads = vjp_fn(tuple(c[...].astype(F32) for c in ct_refs))
        for o_ref, k in zip(out_refs[:len(want)], want):
            o_ref[...] = grads[k].astype(o_ref.dtype)
        i = pl.program_id(0)
        for o_ref, g in zip(out_refs[len(want):], grads[n_rows:]):
            @pl.when(i == 0)
            def _(o_ref=o_ref, g=g):
                o_ref[...] = g

            @pl.when(i > 0)
            def _(o_ref=o_ref, g=g):
                o_ref[...] += g

    in_specs = [pl.BlockSpec((tr, w), functools.partial(lambda i, cb: (i, cb), cb=cb)) for _, w, cb in rows]
    in_specs += [pl.BlockSpec(p.shape, lambda i: (0, 0)) for p in params]
    in_specs += [pl.BlockSpec((tr, c.shape[1]), lambda i: (i, 0)) for c in cts]
    out_shape = [jax.ShapeDtypeStruct((S, rows[k][1]), row_grad_dtypes[k]) for k in want]
    out_specs = [pl.BlockSpec((tr, rows[k][1]), lambda i: (i, 0)) for k in want]
    out_shape += [jax.ShapeDtypeStruct(p.shape, F32) for p in params]
    out_specs += [pl.BlockSpec(p.shape, lambda i: (0, 0)) for p in params]
    res = pl.pallas_call(
        body, name=_nm(base), grid=(S // tr,), out_shape=tuple(out_shape),
        in_specs=in_specs, out_specs=tuple(out_specs), compiler_params=_cp(("arbitrary",)),
    )(*[a for a, _, _ in rows], *params, *cts)
    row_grads = [None] * n_rows
    for k, g in zip(want, res[:len(want)]):
        row_grads[k] = g
    return row_grads, list(res[len(want):])


def rw_op(f, base, n_rows, out_dtypes, shadow=False):
    f_fwd = (lambda *a: (lambda r: tuple(r) + (r[0],))(f(*a))) if shadow else f
    fwd_dtypes = list(out_dtypes) + ([BF16] if shadow else [])

    @jax.custom_vjp
    def op(*args):
        return fwd(*args)[0]

    def split(args):
        rows = [(a, a.shape[1], 0) for a in args[:n_rows]]
        return rows, list(args[n_rows:])

    def fwd(*args):
        rows, params = split(args)
        return tuple(_rw_fwd(f_fwd, rows, params, fwd_dtypes, base + "_fwd")), args

    def bwd(args, cts):
        rows, params = split(args)
        cts = list(cts)[:len(out_dtypes)]
        rg, pg = _rw_bwd(f, rows, params, cts, [a.dtype for a, _, _ in rows], base + "_bwd")
        return tuple(rg) + tuple(g.astype(p.dtype) for g, p in zip(pg, params))

    op.defvjp(fwd, bwd)
    return op


def _ln_res_fn(alpha):
    def f(x, m, g, b):
        z = alpha * x + m
        mu = jnp.mean(z, -1, keepdims=True)
        zc = z - mu
        var = jnp.mean(zc * zc, -1, keepdims=True)
        return (zc * lax.rsqrt(var + EPS) * g + b,)
    return f


def _rms(x, g):
    return x * lax.rsqrt(jnp.mean(x * x, -1, keepdims=True) + EPS) * g


def _mla_norm_fn(q_lora, kv_lora):
    def f(h, qn, kvn):
        return (_rms(h[:, :q_lora], qn), _rms(h[:, q_lora:q_lora + kv_lora], kvn),
                h[:, q_lora + kv_lora:q_lora + kv_lora + LANES])
    return f


def _log_sigmoid(z):
    return jnp.minimum(z, 0.0) - jnp.log(1.0 + jnp.exp(-jnp.abs(z)))


def _gla_gate_fn(z, b):
    return (_log_sigmoid(z + b) / GLA_TAU,)


def _ple_fn(x, glog, pp, b):
    return (x + jax.nn.sigmoid(glog + b) * pp,)


def _gla_out_fn(heads):
    def f(o, r, g):
        parts = []
        for h in range(heads):
            oh = o[:, h * GLA_DV:(h + 1) * GLA_DV]
            mu = jnp.mean(oh, -1, keepdims=True)
            oc = oh - mu
            var = jnp.mean(oc * oc, -1, keepdims=True)
            parts.append(oc * lax.rsqrt(var + EPS) * g[:, h * GLA_DV:(h + 1) * GLA_DV])
        return (jnp.concatenate(parts, axis=1) * (r * jax.nn.sigmoid(r)),)
    return f


def _rope_call(x, tabs, roped, out_dtype, base, fold=False, scale=None):
    S, C = x.shape
    nb = C // LANES
    tr = min(512 if C <= 1024 else 256, S)
    out_c = LANES if fold else C

    def rot(v, a, b1, b2):
        return v * a + pltpu.roll(v, 96, 1) * b1 + pltpu.roll(v, 32, 1) * b2

    def body(x_ref, a_ref, b1_ref, b2_ref, o_ref):
        a, b1, b2 = a_ref[...], b1_ref[...], b2_ref[...]
        if fold:
            v = x_ref[:, 0:LANES].astype(F32)
            for blk in range(1, nb):
                v = v + x_ref[:, blk * LANES:(blk + 1) * LANES].astype(F32)
            o_ref[...] = rot(v, a, b1, b2).astype(o_ref.dtype)
            return
        for blk in range(nb):
            v = x_ref[:, blk * LANES:(blk + 1) * LANES].astype(F32)
            if roped(blk):
                v = rot(v, a, b1, b2)
            if scale is not None:
                v = v * scale
            o_ref[:, blk * LANES:(blk + 1) * LANES] = v.astype(o_ref.dtype)

    row = lambda w: pl.BlockSpec((tr, w), lambda i: (i, 0))
    return pl.pallas_call(
        body, name=_nm(base), grid=(S // tr,), out_shape=jax.ShapeDtypeStruct((S, out_c), out_dtype),
        in_specs=[row(C), row(LANES), row(LANES), row(LANES)], out_specs=row(out_c),
        compiler_params=_cp(("parallel",)),
    )(x, *tabs)


def _attn_tile(S):
    return min(512, S)


def _tri_schedule(n, by_key):
    pairs = ([(i, j) for j in range(n) for i in range(j, n)] if by_key
             else [(i, j) for i in range(n) for j in range(i + 1)])
    return (jnp.asarray([p[0] for p in pairs], jnp.int32), jnp.asarray([p[1] for p in pairs], jnp.int32))


def _mask_table(cid, t):
    n = cid.shape[0] // t
    blocks = cid.reshape(n, t)
    cmin_q, cmax_k = jnp.min(blocks, axis=1), jnp.max(blocks, axis=1)
    need = (cmax_k[None, :] > cmin_q[:, None]) | jnp.eye(n, dtype=bool)
    return need.astype(jnp.int32).reshape(n * n)


def _attn_mask(cidq_ref, cidk_ref, i, j, t):
    qrow = i * t + lax.broadcasted_iota(jnp.int32, (t, 1), 0)
    kcol = j * t + lax.broadcasted_iota(jnp.int32, (1, t), 1)
    qlim = (qrow // Q_BLOCK + 1) * Q_BLOCK
    return (cidk_ref[...] <= cidq_ref[...]) & (kcol < qlim)


ATTN_FWD_HEADS = 8
ATTN_BWD_HEADS = 4
ATTN_SCALE = (MLA_NOPE + MLA_ROPE) ** -0.5
ATTN_SCALE2 = ATTN_SCALE * LOG2E


def _attn_fwd_call(q, kn, v, kr, aux, heads, comm=None):
    S = q.shape[0]
    t = _attn_tile(S)
    n = S // t
    hp = ATTN_FWD_HEADS if heads % ATTN_FWD_HEADS == 0 else 1
    qi_tab, kj_tab = _tri_schedule(n, False)

    def body(qi_ref, kj_ref, need_ref, q_ref, kn_ref, v_ref, kr_ref, cidq_ref, cidk_ref, o_ref, lse_ref,
             m_ref, l_ref, acc_ref):
        st = pl.program_id(1)
        i, j = qi_ref[st], kj_ref[st]

        @pl.when(j == 0)
        def _():
            m_ref[...] = jnp.full(m_ref.shape, NEG_INF, F32)
            l_ref[...] = jnp.zeros(l_ref.shape, F32)
            acc_ref[...] = jnp.zeros(acc_ref.shape, F32)

        def update(masked):
            mask = _attn_mask(cidq_ref, cidk_ref, i, j, t) if masked else None
            for hh in range(hp):
                lanes = slice(hh * LANES, (hh + 1) * LANES)
                k = jnp.concatenate([kn_ref[:, lanes], kr_ref[...]], axis=1)
                qh = q_ref[:, hh * MLA_QK_PAD:(hh + 1) * MLA_QK_PAD]
                s = lax.dot_general(qh, k, (((1,), (1,)), ((), ())), preferred_element_type=F32)
                if masked:
                    s = jnp.where(mask, s, NEG_INF)
                m_prev = m_ref[:, lanes]
                m_new = jnp.maximum(m_prev, jnp.max(s, axis=1, keepdims=True))
                alpha = jnp.exp2(m_prev - m_new)
                p = jnp.exp2(s - m_new[:, :1])
                l_ref[:, lanes] = alpha * l_ref[:, lanes] + jnp.sum(p, axis=1, keepdims=True)
                acc_ref[:, lanes] = alpha * acc_ref[:, lanes] + jnp.dot(p.astype(BF16), v_ref[:, lanes],
                                                                        preferred_element_type=F32)
                m_ref[:, lanes] = m_new

        need = need_ref[i * n + j]

        @pl.when(need != 0)
        def _():
            update(True)

        @pl.when(need == 0)
        def _():
            update(False)

        @pl.when(j == i)
        def _():
            o_ref[...] = (acc_ref[...] / l_ref[...]).astype(o_ref.dtype)
            lse_ref[...] = m_ref[...] + jnp.log(l_ref[...]) * LOG2E

    qmap = lambda h, s, qi, kj, need: (qi[s], h)
    kmap = lambda h, s, qi, kj, need: (kj[s], h)
    return _pcall(
        body, "attn_fwd", (heads // hp, qi_tab.shape[0]), 3,
        in_specs=[pl.BlockSpec((t, hp * MLA_QK_PAD), qmap), pl.BlockSpec((t, hp * MLA_NOPE), kmap),
                  pl.BlockSpec((t, hp * MLA_V), kmap),
                  pl.BlockSpec((t, LANES), lambda h, s, qi, kj, need: (kj[s], 0)),
                  pl.BlockSpec((t, 1), lambda h, s, qi, kj, need: (qi[s], 0)),
                  pl.BlockSpec((1, t), lambda h, s, qi, kj, need: (0, kj[s]))],
        out_specs=[pl.BlockSpec((t, hp * MLA_V), qmap), pl.BlockSpec((t, hp * LANES), qmap)],
        out_shape=[jax.ShapeDtypeStruct((S, heads * MLA_V), BF16), jax.ShapeDtypeStruct((S, heads * LANES), F32)],
        scratch=[pltpu.VMEM((t, hp * LANES), F32), pltpu.VMEM((t, hp * LANES), F32), pltpu.VMEM((t, hp * MLA_V), F32)],
        sem=("parallel", "arbitrary"),
        args=[qi_tab, kj_tab, aux['need'], q, kn, v, kr, aux['cidq'], aux['cidk']], comm=comm)


def _attn_bwd_call(q, kn, v, kr, o, lse, do, aux, heads, comm=None):
    S = q.shape[0]
    t = _attn_tile(S)
    n = S // t
    qi_tab, kj_tab = _tri_schedule(n, True)
    scale = ATTN_SCALE
    nt_dims = (((1,), (1,)), ((), ()))
    tn_dims = (((0,), (0,)), ((), ()))

    hp = ATTN_BWD_HEADS if heads % ATTN_BWD_HEADS == 0 else 1

    def body(qi_ref, kj_ref, need_ref, q_ref, kn_ref, v_ref, kr_ref, cidq_ref, cidk_ref, o_ref, lse_ref, do_ref,
             ta_ref, tb1_ref, tb2_ref, dq_ref, dkn_ref, dv_ref, dkr_ref, dk_acc, dv_acc, dq_acc):
        st = pl.program_id(1)
        i, j = qi_ref[st], kj_ref[st]

        @pl.when(st == 0)
        def _():
            dq_acc[...] = jnp.zeros(dq_acc.shape, F32)

        @pl.when(i == j)
        def _():
            dk_acc[...] = jnp.zeros(dk_acc.shape, F32)
            dv_acc[...] = jnp.zeros(dv_acc.shape, F32)

        rows = pl.ds(pl.multiple_of(i * t, t), t)

        def grads(masked):
            mask = _attn_mask(cidq_ref, cidk_ref, i, j, t) if masked else None
            for hh in range(hp):
                lanes = slice(hh * LANES, (hh + 1) * LANES)
                wide = slice(hh * MLA_QK_PAD, (hh + 1) * MLA_QK_PAD)
                k = jnp.concatenate([kn_ref[:, lanes], kr_ref[...]], axis=1)
                qt, do = q_ref[:, wide], do_ref[:, lanes]
                s = lax.dot_general(qt, k, nt_dims, preferred_element_type=F32)
                if masked:
                    s = jnp.where(mask, s, NEG_INF)
                dp = lax.dot_general(do, v_ref[:, lanes], nt_dims, preferred_element_type=F32)
                dsum = jnp.sum(do.astype(F32) * o_ref[:, lanes].astype(F32), axis=1, keepdims=True)
                p = jnp.exp2(s - lse_ref[:, hh * LANES:hh * LANES + 1])
                ds = (p * (dp - dsum) * scale).astype(BF16)
                dv_acc[:, lanes] += lax.dot_general(p.astype(BF16), do, tn_dims, preferred_element_type=F32)
                dk_acc[:, wide] += lax.dot_general(ds, qt, tn_dims, preferred_element_type=F32)
                dq_acc[rows, wide] += jnp.dot(ds, k, preferred_element_type=F32)

        need = need_ref[i * n + j]

        @pl.when(need != 0)
        def _():
            grads(True)

        @pl.when(need == 0)
        def _():
            grads(False)

        @pl.when(i == n - 1)
        def _():
            for hh in range(hp):
                lanes = slice(hh * LANES, (hh + 1) * LANES)
                off = hh * MLA_QK_PAD
                dkn_ref[:, lanes] = (dk_acc[:, off:off + MLA_NOPE] * (1.0 / ATTN_SCALE2)).astype(dkn_ref.dtype)
                dkr_ref[:, lanes] = dk_acc[:, off + MLA_NOPE:off + MLA_QK_PAD] * (1.0 / ATTN_SCALE2)
            dv_ref[...] = dv_acc[...].astype(dv_ref.dtype)

        @pl.when(i == j)
        def _():
            for hh in range(hp):
                off = hh * MLA_QK_PAD
                dq_ref[rows, off:off + MLA_NOPE] = dq_acc[rows, off:off + MLA_NOPE].astype(dq_ref.dtype)
                g = dq_acc[rows, off + MLA_NOPE:off + MLA_QK_PAD]
                g = g * ta_ref[...] + pltpu.roll(g, 96, 1) * tb1_ref[...] + pltpu.roll(g, 32, 1) * tb2_ref[...]
                dq_ref[rows, off + MLA_NOPE:off + MLA_QK_PAD] = g.astype(dq_ref.dtype)

    qmap = lambda h, s, qi, kj, need: (qi[s], h)
    kmap = lambda h, s, qi, kj, need: (kj[s], h)
    whole = pl.BlockSpec((t, LANES), lambda h, s, qi, kj, need: (qi[s], 0))
    return _pcall(
        body, "attn_bwd", (heads // hp, qi_tab.shape[0]), 3,
        in_specs=[pl.BlockSpec((t, hp * MLA_QK_PAD), qmap), pl.BlockSpec((t, hp * MLA_NOPE), kmap),
                  pl.BlockSpec((t, hp * MLA_V), kmap),
                  pl.BlockSpec((t, LANES), lambda h, s, qi, kj, need: (kj[s], 0)),
                  pl.BlockSpec((t, 1), lambda h, s, qi, kj, need: (qi[s], 0)),
                  pl.BlockSpec((1, t), lambda h, s, qi, kj, need: (0, kj[s])),
                  pl.BlockSpec((t, hp * MLA_V), qmap), pl.BlockSpec((t, hp * LANES), qmap),
                  pl.BlockSpec((t, hp * MLA_V), qmap), whole, whole, whole],
        out_specs=[pl.BlockSpec((S, hp * MLA_QK_PAD), lambda h, s, qi, kj, need: (0, h)),
                   pl.BlockSpec((t, hp * MLA_NOPE), kmap), pl.BlockSpec((t, hp * MLA_V), kmap),
                   pl.BlockSpec((t, hp * LANES), kmap)],
        out_shape=[jax.ShapeDtypeStruct((S, heads * MLA_QK_PAD), BF16), jax.ShapeDtypeStruct((S, heads * MLA_NOPE), BF16),
                   jax.ShapeDtypeStruct((S, heads * MLA_V), BF16), jax.ShapeDtypeStruct((S, heads * LANES), F32)],
        scratch=[pltpu.VMEM((t, hp * MLA_QK_PAD), F32), pltpu.VMEM((t, hp * MLA_V), F32),
                 pltpu.VMEM((S, hp * MLA_QK_PAD), F32)],
        sem=("parallel", "arbitrary"),
        args=[qi_tab, kj_tab, aux['need'], q, kn, v, kr, aux['cidq'], aux['cidk'], o, lse, do, *aux['rope'][1]],
        comm=comm)


def make_attention(aux, heads):
    tabs_f, tabs_b = aux['rope']
    odd, every = (lambda blk: blk % 2 == 1), (lambda blk: True)

    def run_host(q_raw, kn, v, kr_raw, comm=None):
        q = _rope_call(q_raw, tabs_f, odd, BF16, "rope_q", scale=ATTN_SCALE2)
        kr = _rope_call(kr_raw, tabs_f, every, BF16, "rope_k")
        o, lse, *carried = _attn_fwd_call(q, kn, v, kr, aux, heads, comm)
        return o, (q, kn, v, kr, o, lse), carried

    def bwd_host(res, do, comm=None):
        q, kn, v, kr, o, lse = res
        dq_raw, dkn, dv, dkr, *carried = _attn_bwd_call(q, kn, v, kr, o, lse, do, aux, heads, comm)
        return (dq_raw, dkn, dv, _rope_call(dkr, tabs_b, every, F32, "rope_dk", fold=True)), carried

    @jax.custom_vjp
    def attn(q_raw, kn, v, kr_raw):
        return run_host(q_raw, kn, v, kr_raw)[0]

    attn.defvjp(lambda *a: run_host(*a)[:2], lambda res, do: bwd_host(res, do)[0])
    return attn, run_host, bwd_host


GLA_ROWS = 256
GLA_HEADS_PER_STEP = 4


def _tri(lower):
    r = lax.broadcasted_iota(jnp.int32, (CHUNK, CHUNK), 0)
    c = lax.broadcasted_iota(jnp.int32, (CHUNK, CHUNK), 1)
    return jnp.where((c <= r) if lower else (c >= r), 1.0, 0.0).astype(F32)


def _gla_chunk(q_ref, k_ref, v_ref, la_ref, sl, hh):
    lk, lv = slice(hh * GLA_DK, (hh + 1) * GLA_DK), slice(hh * GLA_DV, (hh + 1) * GLA_DV)
    la = la_ref[sl, lk]
    cum = jnp.dot(_tri(True), la, preferred_element_type=F32, precision=lax.Precision.HIGHEST)
    tot = cum[CHUNK - 1:CHUNK, :]
    e = jnp.exp(tot - cum)
    k = k_ref[sl, lk].astype(F32)
    kdec = k * e
    v = v_ref[sl, lv]
    upd_t = lax.dot_general(v.astype(BF16), kdec.astype(BF16), (((0,), (0,)), ((), ())), preferred_element_type=F32)
    qs = (q_ref[sl, lk].astype(F32) * (GLA_DK ** -0.5)).astype(BF16)
    return e, k, kdec, v, upd_t, jnp.exp(tot), qs


def _gla_group(heads):
    return GLA_HEADS_PER_STEP if heads % GLA_HEADS_PER_STEP == 0 else 1


def _gla_specs(heads, hp, rows_map):
    groups = heads // hp
    return [pl.BlockSpec((GLA_ROWS, hp * GLA_DK), lambda h, b: (rows_map(b), h)),
            pl.BlockSpec((GLA_ROWS, hp * GLA_DK), lambda h, b: (rows_map(b), groups + h)),
            pl.BlockSpec((GLA_ROWS, hp * GLA_DV), lambda h, b: (rows_map(b), groups + h)),
            pl.BlockSpec((GLA_ROWS, hp * GLA_DK), lambda h, b: (rows_map(b), h))]


def _gla_fwd_call(hm, la, heads):
    S = hm.shape[0]
    assert S % GLA_ROWS == 0
    nb, cpb, hp = S // GLA_ROWS, GLA_ROWS // CHUNK, _gla_group(heads)

    def body(q_ref, k_ref, v_ref, la_ref, o_ref, sp_ref, st_ref):
        @pl.when(pl.program_id(1) == 0)
        def _():
            st_ref[...] = jnp.zeros(st_ref.shape, F32)

        for c in range(cpb):
            sl = slice(c * CHUNK, (c + 1) * CHUNK)
            for hh in range(hp):
                _, _, _, _, upd_t, decay, qs = _gla_chunk(q_ref, k_ref, v_ref, la_ref, sl, hh)
                state = st_ref[hh]
                sp_ref[hh, c] = state
                state = state * decay + upd_t
                st_ref[hh] = state
                o_ref[sl, hh * GLA_DV:(hh + 1) * GLA_DV] = lax.dot_general(
                    qs, state.astype(BF16), (((1,), (1,)), ((), ())), preferred_element_type=F32)

    return pl.pallas_call(
        body, name=_nm("gla_fwd"), grid=(heads // hp, nb),
        out_shape=(jax.ShapeDtypeStruct((S, heads * GLA_DV), F32),
                   jax.ShapeDtypeStruct((heads, S // CHUNK, GLA_DV, GLA_DK), F32)),
        in_specs=_gla_specs(heads, hp, lambda b: b),
        out_specs=(pl.BlockSpec((GLA_ROWS, hp * GLA_DV), lambda h, b: (b, h)),
                   pl.BlockSpec((hp, cpb, GLA_DV, GLA_DK), lambda h, b: (h, b, 0, 0))),
        scratch_shapes=[pltpu.VMEM((hp, GLA_DV, GLA_DK), F32)],
        compiler_params=_cp(("parallel", "arbitrary")),
    )(hm, hm, hm, la)


def _gla_bwd_call(hm, la, sprev, do, heads):
    S = hm.shape[0]
    nb, cpb, hp = S // GLA_ROWS, GLA_ROWS // CHUNK, _gla_group(heads)
    scale = GLA_DK ** -0.5

    def body(q_ref, k_ref, v_ref, la_ref, sp_ref, do_ref, dq_ref, dk_ref, dv_ref, dla_ref, carry_ref):
        @pl.when(pl.program_id(1) == 0)
        def _():
            carry_ref[...] = jnp.zeros(carry_ref.shape, F32)

        for c in reversed(range(cpb)):
            sl = slice(c * CHUNK, (c + 1) * CHUNK)
            for hh in range(hp):
                lk, lv = slice(hh * GLA_DK, (hh + 1) * GLA_DK), slice(hh * GLA_DV, (hh + 1) * GLA_DV)
                e, k, kdec, v, upd_t, decay, qs = _gla_chunk(q_ref, k_ref, v_ref, la_ref, sl, hh)
                sp = sp_ref[hh, c]
                s_n = sp * decay + upd_t
                dob = do_ref[sl, lv].astype(BF16)
                g = carry_ref[hh] + lax.dot_general(dob, qs, (((0,), (0,)), ((), ())), preferred_element_type=F32)
                gb = g.astype(BF16)
                dq_ref[sl, lk] = (jnp.dot(dob, s_n.astype(BF16), preferred_element_type=F32) * scale).astype(dq_ref.dtype)
                ddecay = jnp.sum(g * sp, axis=0, keepdims=True)
                dkdec = jnp.dot(v.astype(BF16), gb, preferred_element_type=F32)
                dv_ref[sl, lv] = lax.dot_general(kdec.astype(BF16), gb, (((1,), (1,)), ((), ())),
                                                 preferred_element_type=F32).astype(dv_ref.dtype)
                dk_ref[sl, lk] = (dkdec * e).astype(dk_ref.dtype)
                w = dkdec * k * e
                dtot = jnp.sum(w, axis=0, keepdims=True) + ddecay * decay
                last = lax.broadcasted_iota(jnp.int32, (CHUNK, 1), 0) == CHUNK - 1
                dcum = jnp.where(last, dtot - w, -w)
                dla_ref[sl, lk] = jnp.dot(_tri(False), dcum, preferred_element_type=F32, precision=lax.Precision.HIGHEST)
                carry_ref[hh] = g * decay

    rev = lambda b: nb - 1 - b
    narrow = pl.BlockSpec((GLA_ROWS, hp * GLA_DK), lambda h, b: (rev(b), h))
    wide = pl.BlockSpec((GLA_ROWS, hp * GLA_DV), lambda h, b: (rev(b), h))
    return pl.pallas_call(
        body, name=_nm("gla_bwd"), grid=(heads // hp, nb),
        out_shape=(jax.ShapeDtypeStruct((S, heads * GLA_DK), hm.dtype), jax.ShapeDtypeStruct((S, heads * GLA_DK), hm.dtype),
                   jax.ShapeDtypeStruct((S, heads * GLA_DV), hm.dtype), jax.ShapeDtypeStruct((S, heads * GLA_DK), F32)),
        in_specs=_gla_specs(heads, hp, rev) + [
            pl.BlockSpec((hp, cpb, GLA_DV, GLA_DK), lambda h, b: (h, rev(b), 0, 0)), wide],
        out_specs=(narrow, narrow, wide, narrow),
        scratch_shapes=[pltpu.VMEM((hp, GLA_DV, GLA_DK), F32)],
        compiler_params=_cp(("parallel", "arbitrary")),
    )(hm, hm, hm, la, sprev, do)


@functools.partial(jax.custom_vjp, nondiff_argnums=(3,))
def gla_core(hm, la, o_norm, heads):
    return _gla_core_fwd(hm, la, o_norm, heads)[0]


def _gla_core_fwd(hm, la, o_norm, heads):
    o, sprev = _gla_fwd_call(hm, la, heads)
    vd = heads * GLA_DV
    rows = [(o, vd, 0), (hm, vd, 2 * heads * GLA_DK // vd + 1)]
    (y,) = _rw_fwd(_gla_out_fn(heads), rows, [o_norm], [BF16], "gla_out_fwd")
    return y, (hm, la, o_norm, o, sprev)


def _gla_core_bwd(heads, res, dy):
    hm, la, o_norm, o, sprev = res
    vd = heads * GLA_DV
    rows = [(o, vd, 0), (hm, vd, 2 * heads * GLA_DK // vd + 1)]
    (do, dr), (dg,) = _rw_bwd(_gla_out_fn(heads), rows, [o_norm], [dy], [F32, hm.dtype], "gla_out_bwd")
    dq, dk, dv, dla = _gla_bwd_call(hm, la, sprev, do, heads)
    return jnp.concatenate([dq, dk, dv, dr], axis=1), dla, dg


gla_core.defvjp(_gla_core_fwd, _gla_core_bwd)


CONV_COLS = 256
HALO = 16


def _conv_rows(S):
    return min(512, S)


def _conv_taps(main_ref, halo_ref, i):
    prev = jnp.where(i > 0, halo_ref[...].astype(F32), 0.0)
    full = jnp.concatenate([prev, main_ref[...].astype(F32)], axis=0)
    return full[HALO:], pltpu.roll(full, 1, 0)[HALO:], pltpu.roll(full, 2, 0)[HALO:]


def _conv_apply(taps, w_ref, b_ref):
    x0, x1, x2 = taps
    return x2 * w_ref[0:1, :] + x1 * w_ref[1:2, :] + x0 * w_ref[2:3, :] + b_ref[...]


def _gelu_gate(uc, gc):
    return uc * jax.nn.gelu(gc)


def _conv_cols(dff, pref):
    return max(c for c in range(LANES, pref + 1, LANES) if dff % c == 0)


def _conv_in_specs(R, C, nj):
    hpr = R // HALO
    main = lambda off: pl.BlockSpec((R, C), lambda j, i: (i, j + off))
    halo = lambda off: pl.BlockSpec((HALO, C), lambda j, i: (jnp.maximum(i * hpr - 1, 0), j + off))
    par = lambda rows, off: pl.BlockSpec((rows, C), lambda j, i: (0, j + off))
    return [main(0), halo(0), main(nj), halo(nj), par(CONV_W, 0), par(CONV_W, nj), par(1, 0), par(1, nj)]


def _conv_fwd_call(h, cw, cb, comm=None):
    S, dff = h.shape[0], h.shape[1] // 2
    R, C = _conv_rows(S), _conv_cols(dff, 768)
    nj = dff // C

    def body(u_ref, uh_ref, g_ref, gh_ref, wu_ref, wg_ref, bu_ref, bg_ref, a_ref):
        i = pl.program_id(1)
        uc = _conv_apply(_conv_taps(u_ref, uh_ref, i), wu_ref, bu_ref)
        gc = _conv_apply(_conv_taps(g_ref, gh_ref, i), wg_ref, bg_ref)
        a_ref[...] = _gelu_gate(uc, gc).astype(a_ref.dtype)

    return _pcall(
        body, "conv_fwd", (nj, S // R), 0, in_specs=_conv_in_specs(R, C, nj),
        out_specs=[pl.BlockSpec((R, C), lambda j, i: (i, j))], out_shape=[jax.ShapeDtypeStruct((S, dff), BF16)],
        scratch=[], sem=("parallel", "parallel"), args=[h, h, h, h, cw, cw, cb, cb], comm=comm)


def _conv_bwd_gate_call(h, cw, cb, da, comm=None):
    S, dff = h.shape[0], h.shape[1] // 2
    R, C = _conv_rows(S), _conv_cols(dff, 768)
    nj = dff // C

    def body(u_ref, uh_ref, g_ref, gh_ref, wu_ref, wg_ref, bu_ref, bg_ref, da_ref,
             du_ref, dg_ref, dwu_ref, dwg_ref, dbu_ref, dbg_ref):
        i = pl.program_id(1)
        ut, gt = _conv_taps(u_ref, uh_ref, i), _conv_taps(g_ref, gh_ref, i)
        uc, gc = _conv_apply(ut, wu_ref, bu_ref), _conv_apply(gt, wg_ref, bg_ref)
        _, vjp_fn = jax.vjp(_gelu_gate, uc, gc)
        du, dg = vjp_fn(da_ref[...].astype(F32))
        du_ref[...] = du.astype(du_ref.dtype)
        dg_ref[...] = dg.astype(dg_ref.dtype)

        @pl.when(i == 0)
        def _():
            for r in (dwu_ref, dwg_ref, dbu_ref, dbg_ref):
                r[...] = jnp.zeros(r.shape, F32)

        for d, taps, dw_ref, db_ref in ((du, ut, dwu_ref, dbu_ref), (dg, gt, dwg_ref, dbg_ref)):
            x0, x1, x2 = taps
            dw_ref[0:1, :] += jnp.sum(d * x2, axis=0, keepdims=True)
            dw_ref[1:2, :] += jnp.sum(d * x1, axis=0, keepdims=True)
            dw_ref[2:3, :] += jnp.sum(d * x0, axis=0, keepdims=True)
            db_ref[...] += jnp.sum(d, axis=0, keepdims=True)

    tile = pl.BlockSpec((R, C), lambda j, i: (i, j))
    par = lambda rows: pl.BlockSpec((rows, C), lambda j, i: (0, j))
    return _pcall(
        body, "conv_bwd_gate", (nj, S // R), 0, in_specs=_conv_in_specs(R, C, nj) + [tile],
        out_specs=[tile, tile, par(CONV_W), par(CONV_W), par(1), par(1)],
        out_shape=[jax.ShapeDtypeStruct((S, dff), BF16), jax.ShapeDtypeStruct((S, dff), BF16),
                   jax.ShapeDtypeStruct((CONV_W, dff), F32), jax.ShapeDtypeStruct((CONV_W, dff), F32),
                   jax.ShapeDtypeStruct((1, dff), F32), jax.ShapeDtypeStruct((1, dff), F32)],
        scratch=[], sem=("parallel", "arbitrary"), args=[h, h, h, h, cw, cw, cb, cb, da], comm=comm)


def _conv_bwd_shift_call(dc, cw, into=None):
    S, dff = dc.shape
    R, C = min(1024, S), _conv_cols(dff, 1024)
    nj = dff // C
    col_off = 0 if into is None else nj
    hpr, last = R // HALO, S // HALO - 1
    ni = S // R

    def body(d_ref, nx_ref, w_ref, *rest):
        o_ref = rest[-1]
        i = pl.program_id(1)
        nxt = jnp.where(i < ni - 1, nx_ref[...].astype(F32), 0.0)
        full = jnp.concatenate([d_ref[...].astype(F32), nxt], axis=0)
        n = R + HALO
        y1, y2 = pltpu.roll(full, n - 1, 0)[:R], pltpu.roll(full, n - 2, 0)[:R]
        o_ref[...] = (full[:R] * w_ref[2:3, :] + y1 * w_ref[1:2, :] + y2 * w_ref[0:1, :]).astype(o_ref.dtype)

    in_specs = [pl.BlockSpec((R, C), lambda j, i: (i, j)),
                pl.BlockSpec((HALO, C), lambda j, i: (jnp.minimum((i + 1) * hpr, last), j)),
                pl.BlockSpec((CONV_W, C), lambda j, i: (0, j + col_off))]
    args = [dc, dc, cw]
    if into is not None:
        in_specs.append(pl.BlockSpec(memory_space=pl.ANY))
        args.append(into)
    return pl.pallas_call(
        body, name=_nm("conv_bwd_shift"), grid=(nj, ni), out_shape=jax.ShapeDtypeStruct((S, 2 * dff), BF16),
        in_specs=in_specs, out_specs=pl.BlockSpec((R, C), lambda j, i: (i, j + col_off)),
        input_output_aliases={} if into is None else {3: 0},
        compiler_params=_cp(("parallel", "parallel")),
    )(*args)


@jax.custom_vjp
def ffn_hidden(x1, x1_bf16, w3, cw, cb):
    return _ffn_hidden_fwd(x1, x1_bf16, w3, cw, cb)[0]


def _ffn_hidden_fwd(x1, x1_bf16, w3, cw, cb, comm=None):
    h = _mm(x1_bf16, w3, "nn", BF16, "up", _all_blocks(w3))
    a, *carried = _conv_fwd_call(h, cw, cb, comm)
    return (a, (x1_bf16, w3, cw, cb, h)) + ((carried,) if comm is not None else ())


def _ffn_hidden_bwd(res, da, comm=None):
    x1, w3, cw, cb, h = res
    du, dg, dwu, dwg, dbu, dbg, *carried = _conv_bwd_gate_call(h, cw, cb, da, comm)
    dh = _conv_bwd_shift_call(dg, cw, into=_conv_bwd_shift_call(du, cw))
    dx = _mm(dh, w3, "nt", F32, "up_dx", _all_blocks(w3))
    dw3 = _mm(x1, dh, "tn", w3.dtype, "up_dw", _all_blocks(w3))
    grads = (dx, jnp.zeros_like(x1), dw3, jnp.concatenate([dwu, dwg], axis=1), jnp.concatenate([dbu, dbg], axis=1))
    return (grads, carried) if comm is not None else grads


ffn_hidden.defvjp(_ffn_hidden_fwd, _ffn_hidden_bwd)


def _loss_call(y, target):
    S, D = y.shape
    tr = _row_tile(S, D)

    def body(y_ref, t_ref, sq_ref, dy_ref):
        diff = y_ref[...] - t_ref[...]
        dy_ref[...] = diff * (1.0 / D)
        part = jnp.sum(diff * diff, axis=0, keepdims=True)
        i = pl.program_id(0)

        @pl.when(i == 0)
        def _():
            sq_ref[...] = part

        @pl.when(i > 0)
        def _():
            sq_ref[...] += part

    row = pl.BlockSpec((tr, D), lambda i: (i, 0))
    return pl.pallas_call(
        body, name=_nm("loss"), grid=(S // tr,),
        out_shape=(jax.ShapeDtypeStruct((1, D), F32), jax.ShapeDtypeStruct((S, D), F32)),
        in_specs=[row, row], out_specs=(pl.BlockSpec((1, D), lambda i: (0, 0)), row),
        compiler_params=_cp(("arbitrary",)),
    )(y, target)


def _row_divisor(rows, cap=512):
    for cand in range(min(rows, cap), 15, -1):
        if rows % cand == 0 and cand % 16 == 0:
            return cand
    return rows


def _adamw_call(w, g, m, v):
    shape = w.shape
    w2, g2, m2, v2 = (a.reshape(-1, shape[-1]) for a in (w, g, m, v))
    rows, cols = w2.shape
    tr = _row_divisor(rows)

    def body(w_ref, g_ref, m_ref, v_ref, d_ref, nm_ref, nv_ref):
        g_ = g_ref[...]
        m_ = ADAM_B1 * m_ref[...] + (1.0 - ADAM_B1) * g_
        v_ = ADAM_B2 * v_ref[...] + (1.0 - ADAM_B2) * (g_ * g_)
        m_hat = m_ / (1.0 - ADAM_B1 ** ADAM_STEP)
        v_hat = v_ / (1.0 - ADAM_B2 ** ADAM_STEP)
        d_ref[...] = -ADAM_LR * (m_hat / (jnp.sqrt(v_hat) + ADAM_EPS) + ADAM_WD * w_ref[...])
        nm_ref[...] = m_
        nv_ref[...] = v_

    blk = pl.BlockSpec((tr, cols), lambda i: (i, 0))
    outs = pl.pallas_call(
        body, name=_nm("adamw"), grid=(rows // tr,),
        out_shape=tuple(jax.ShapeDtypeStruct((rows, cols), F32) for _ in range(3)),
        in_specs=[blk] * 4, out_specs=(blk,) * 3, compiler_params=_cp(("parallel",)),
    )(w2, g2, m2, v2)
    return tuple(o.reshape(shape) for o in outs)


ANY = pl.BlockSpec(memory_space=pl.ANY)


def _place():
    return lax.axis_index("x"), lax.axis_index("y"), lax.axis_index("c")


def all_gather(shards):
    n = len(shards)

    def body(*refs):
        x_refs, out_refs = refs[:n], refs[n:2 * n]
        send_sems, recv_sems, local_sems = refs[2 * n:]
        x, y, c = _place()
        me, sibling = (x, y, c), (x, y, 1 - c)
        chips = [(1 - x, y), (x, 1 - y), (1 - x, 1 - y)]

        def slot(a, px, py, pc):
            return out_refs[a].at[:, 4 * px + 2 * py + pc]

        def copy(a, k, block, to, own=False):
            return pltpu.make_async_remote_copy(
                src_ref=x_refs[a] if own else slot(a, *block), dst_ref=slot(a, *block),
                send_sem=send_sems.at[7 * a + k], recv_sem=recv_sems.at[7 * a + k], device_id=to, device_id_type=MESH)

        mine = [pltpu.make_async_copy(x_refs[a], slot(a, *me), local_sems.at[a]) for a in range(n)]
        first = []
        for a in range(n):
            mine[a].start()
            first.append(copy(a, 0, me, sibling, own=True))
            first += [copy(a, 1 + j, me, (*chip, c), own=True) for j, chip in enumerate(chips)]
        for cp in first:
            cp.start()
        passed = []
        for j, chip in enumerate(chips):
            for a in range(n):
                copy(a, 1 + j, (*chip, c), me).wait_recv()
                passed.append(copy(a, 4 + j, (*chip, c), sibling))
                passed[-1].start()
        for a in range(n):
            copy(a, 0, sibling, me).wait_recv()
        for j, chip in enumerate(chips):
            for a in range(n):
                copy(a, 4 + j, (*chip, 1 - c), me).wait_recv()
        for cp in first + passed:
            cp.wait_send()
        for cp in mine:
            cp.wait()

    return pl.pallas_call(
        body, name=_nm("all_gather"),
        out_shape=tuple(jax.ShapeDtypeStruct((s.shape[0], N_DEV) + s.shape[1:], s.dtype) for s in shards),
        in_specs=[ANY] * n, out_specs=(ANY,) * n,
        scratch_shapes=[pltpu.SemaphoreType.DMA((7 * n,)), pltpu.SemaphoreType.DMA((7 * n,)), pltpu.SemaphoreType.DMA((n,))],
    )(*shards)


def _rs_pair_exchange(gs):
    n = len(gs)

    def body(*refs):
        g_refs, recv_refs = refs[:n], refs[n:2 * n]
        send_sems, recv_sems = refs[2 * n:]
        x, y, c = _place()
        copies = [pltpu.make_async_remote_copy(
            src_ref=g_refs[a].at[:, 2 * j + (1 - c)], dst_ref=recv_refs[a].at[j], send_sem=send_sems.at[4 * a + j],
            recv_sem=recv_sems.at[4 * a + j], device_id=(x, y, 1 - c), device_id_type=MESH)
            for a in range(n) for j in range(4)]
        for cp in copies:
            cp.start()
        for cp in copies:
            cp.wait_recv()
        for cp in copies:
            cp.wait_send()

    return pl.pallas_call(
        body, name=_nm("rs_pair"),
        out_shape=tuple(jax.ShapeDtypeStruct((4, g.shape[0]) + g.shape[2:], g.dtype) for g in gs),
        in_specs=[ANY] * n, out_specs=(ANY,) * n,
        scratch_shapes=[pltpu.SemaphoreType.DMA((4 * n,)), pltpu.SemaphoreType.DMA((4 * n,))],
    )(*gs)


def _rs_chip_exchange(ps):
    n = len(ps)

    def body(*refs):
        p_refs, recv_refs = refs[:n], refs[n:2 * n]
        send_sems, recv_sems = refs[2 * n:]
        x, y, c = _place()
        chips = [(1 - x, y), (x, 1 - y), (1 - x, 1 - y)]
        copies = [pltpu.make_async_remote_copy(
            src_ref=p_refs[a].at[2 * cx + cy], dst_ref=recv_refs[a].at[k], send_sem=send_sems.at[3 * a + k],
            recv_sem=recv_sems.at[3 * a + k], device_id=(cx, cy, c), device_id_type=MESH)
            for a in range(n) for k, (cx, cy) in enumerate(chips)]
        for cp in copies:
            cp.start()
        for cp in copies:
            cp.wait_recv()
        for cp in copies:
            cp.wait_send()

    return pl.pallas_call(
        body, name=_nm("rs_chip"),
        out_shape=tuple(jax.ShapeDtypeStruct((3,) + p.shape[1:], p.dtype) for p in ps),
        in_specs=[ANY] * n, out_specs=(ANY,) * n,
        scratch_shapes=[pltpu.SemaphoreType.DMA((3 * n,)), pltpu.SemaphoreType.DMA((3 * n,))],
    )(*ps)


def _rs_pair_add(g, recv, c_idx):
    L, _, a, b = g.shape
    ta = _row_divisor(a, 1024)

    def body(c_ref, g_ref, r_ref, o_ref):
        o_ref[...] = (g_ref[...].astype(F32) + r_ref[...].astype(F32)).astype(o_ref.dtype)

    grid_spec = pltpu.PrefetchScalarGridSpec(
        num_scalar_prefetch=1, grid=(4, L, a // ta),
        in_specs=[pl.BlockSpec((None, None, ta, b), lambda j, l, i, c_ref: (l, 2 * j + c_ref[0], i, 0)),
                  pl.BlockSpec((None, None, ta, b), lambda j, l, i, c_ref: (j, l, i, 0))],
        out_specs=pl.BlockSpec((None, None, ta, b), lambda j, l, i, c_ref: (j, l, i, 0)))
    return pl.pallas_call(
        body, name=_nm("rs_pair_add"), grid_spec=grid_spec, out_shape=jax.ShapeDtypeStruct((4, L, a, b), g.dtype),
        compiler_params=_cp(("parallel", "parallel", "parallel")),
    )(c_idx, g, recv)


def _rs_final_add(p1, recv, chip_idx):
    _, L, a, b = p1.shape
    ta = _row_divisor(a, 1024)

    def body(chip_ref, p_ref, r_ref, o_ref):
        acc = p_ref[...].astype(F32)
        for k in range(3):
            acc = acc + r_ref[k].astype(F32)
        o_ref[...] = acc

    grid_spec = pltpu.PrefetchScalarGridSpec(
        num_scalar_prefetch=1, grid=(L, a // ta),
        in_specs=[pl.BlockSpec((None, None, ta, b), lambda l, i, chip_ref: (chip_ref[0], l, i, 0)),
                  pl.BlockSpec((3, None, ta, b), lambda l, i, chip_ref: (0, l, i, 0))],
        out_specs=pl.BlockSpec((None, ta, b), lambda l, i, chip_ref: (l, i, 0)))
    return pl.pallas_call(
        body, name=_nm("rs_final_add"), grid_spec=grid_spec, out_shape=jax.ShapeDtypeStruct((L, a, b), F32),
        compiler_params=_cp(("parallel", "parallel")),
    )(chip_idx, p1, recv)


def reduce_scatter(gs):
    x, y, c = _place()
    c_idx = jnp.reshape(c, (1,)).astype(jnp.int32)
    chip_idx = jnp.reshape(2 * x + y, (1,)).astype(jnp.int32)
    recv1 = _rs_pair_exchange(gs)
    p1 = [_rs_pair_add(g, r, c_idx) for g, r in zip(gs, recv1)]
    recv2 = _rs_chip_exchange(p1)
    return [_rs_final_add(p, r, chip_idx) for p, r in zip(p1, recv2)]


def all_reduce_small(v):
    r, C = v.shape

    def body(v_ref, out_ref, buf_ref, send_sems, recv_sems):
        x, y, c = _place()
        my_id = 4 * x + 2 * y + c
        buf_ref[my_id] = v_ref[...]
        copies = []
        for k in range(1, N_DEV):
            fx, fy, fc = (k >> 2) & 1, (k >> 1) & 1, k & 1
            peer = (x ^ fx, y ^ fy, c ^ fc)
            copies.append(pltpu.make_async_remote_copy(
                src_ref=v_ref, dst_ref=buf_ref.at[my_id], send_sem=send_sems.at[k - 1], recv_sem=recv_sems.at[k - 1],
                device_id=peer, device_id_type=MESH))
        for cp in copies:
            cp.start()
        for cp in copies:
            cp.wait_recv()
        for cp in copies:
            cp.wait_send()
        acc = buf_ref[0]
        for d in range(1, N_DEV):
            acc = acc + buf_ref[d]
        out_ref[...] = acc

    vm = pl.BlockSpec(memory_space=pltpu.VMEM)
    return pl.pallas_call(
        body, name=_nm("all_reduce_small"), out_shape=jax.ShapeDtypeStruct((r, C), F32),
        in_specs=[vm], out_specs=vm,
        scratch_shapes=[pltpu.VMEM((N_DEV, r, C), F32), pltpu.SemaphoreType.DMA((7,)), pltpu.SemaphoreType.DMA((7,))],
    )(v)


def _slot(ref, px, py, pc):
    return ref.at[:, 4 * px + 2 * py + pc]


def comm_gather_own(shards):
    n = len(shards)

    def build(cin, cout, send_sems, recv_sems, local_sems):
        x, y, c = _place()
        me = (x, y, c)
        direct = [(x, y, 1 - c), (1 - x, y, c), (x, 1 - y, c)]
        src_nb, dst_nb, diag = (x ^ (1 - c), y ^ c, c), (x ^ c, y ^ (1 - c), c), (1 - x, 1 - y, c)
        starts, mid_waits, mid_starts, waits = [], [], [], []
        for a in range(n):
            local = pltpu.make_async_copy(cin[a], _slot(cout[a], *me), local_sems.at[a])
            starts.append(local)
            waits.append(local.wait)
            for k, peer in enumerate(direct):
                send = pltpu.make_async_remote_copy(
                    src_ref=cin[a], dst_ref=_slot(cout[a], *me), send_sem=send_sems.at[4 * a + k],
                    recv_sem=recv_sems.at[4 * a + k], device_id=peer, device_id_type=MESH)
                arrive = pltpu.make_async_remote_copy(
                    src_ref=cin[a], dst_ref=_slot(cout[a], *peer), send_sem=send_sems.at[4 * a + k],
                    recv_sem=recv_sems.at[4 * a + k], device_id=peer, device_id_type=MESH)
                starts.append(send)
                (waits if k == 0 else mid_waits).append(arrive.wait_recv)
                waits.append(send.wait_send)
            onward = pltpu.make_async_remote_copy(
                src_ref=_slot(cout[a], *src_nb), dst_ref=_slot(cout[a], *src_nb), send_sem=send_sems.at[4 * a + 3],
                recv_sem=recv_sems.at[4 * a + 3], device_id=dst_nb, device_id_type=MESH)
            arrive = pltpu.make_async_remote_copy(
                src_ref=_slot(cout[a], *src_nb), dst_ref=_slot(cout[a], *diag), send_sem=send_sems.at[4 * a + 3],
                recv_sem=recv_sems.at[4 * a + 3], device_id=dst_nb, device_id_type=MESH)
            mid_starts.append(onward)
            waits += [arrive.wait_recv, onward.wait_send]
        return starts, mid_waits, mid_starts, waits

    out_shapes = [jax.ShapeDtypeStruct((s.shape[0], N_DEV) + s.shape[1:], s.dtype) for s in shards]
    return Comm(shards, out_shapes, {}, 4 * n, 4 * n, n, build)


def comm_gather_pass(partial):
    n = len(partial)

    def build(cin, cout, send_sems, recv_sems, local_sems):
        x, y, c = _place()
        chips = [(1 - x, y), (x, 1 - y), (1 - x, 1 - y)]
        starts, waits = [], []
        for a in range(n):
            for j, chip in enumerate(chips):
                send = pltpu.make_async_remote_copy(
                    src_ref=_slot(cout[a], *chip, c), dst_ref=_slot(cout[a], *chip, c), send_sem=send_sems.at[3 * a + j],
                    recv_sem=recv_sems.at[3 * a + j], device_id=(x, y, 1 - c), device_id_type=MESH)
                arrive = pltpu.make_async_remote_copy(
                    src_ref=_slot(cout[a], *chip, c), dst_ref=_slot(cout[a], *chip, 1 - c),
                    send_sem=send_sems.at[3 * a + j], recv_sem=recv_sems.at[3 * a + j],
                    device_id=(x, y, 1 - c), device_id_type=MESH)
                starts.append(send)
                waits += [arrive.wait_recv, send.wait_send]
        return starts, waits

    out_shapes = [jax.ShapeDtypeStruct(p.shape, p.dtype) for p in partial]
    return Comm(partial, out_shapes, {a: a for a in range(n)}, 3 * n, 3 * n, 1, build)


def comm_rs_pair(gs):
    n = len(gs)

    def build(cin, cout, send_sems, recv_sems, local_sems):
        x, y, c = _place()
        starts, waits = [], []
        for a in range(n):
            for j in range(4):
                cp = pltpu.make_async_remote_copy(
                    src_ref=cin[a].at[:, 2 * j + (1 - c)], dst_ref=cout[a].at[j], send_sem=send_sems.at[4 * a + j],
                    recv_sem=recv_sems.at[4 * a + j], device_id=(x, y, 1 - c), device_id_type=MESH)
                starts.append(cp)
                waits += [cp.wait_recv, cp.wait_send]
        return starts, waits

    out_shapes = [jax.ShapeDtypeStruct((4, g.shape[0]) + g.shape[2:], g.dtype) for g in gs]
    return Comm(gs, out_shapes, {}, 4 * n, 4 * n, 1, build)


def comm_rs_chip(ps):
    n = len(ps)

    def build(cin, cout, send_sems, recv_sems, local_sems):
        x, y, c = _place()
        chips = [(1 - x, y), (x, 1 - y), (1 - x, 1 - y)]
        starts, waits = [], []
        for a in range(n):
            for k, (cx, cy) in enumerate(chips):
                cp = pltpu.make_async_remote_copy(
                    src_ref=cin[a].at[2 * cx + cy], dst_ref=cout[a].at[k], send_sem=send_sems.at[3 * a + k],
                    recv_sem=recv_sems.at[3 * a + k], device_id=(cx, cy, c), device_id_type=MESH)
                starts.append(cp)
                waits += [cp.wait_recv, cp.wait_send]
        return starts, waits

    out_shapes = [jax.ShapeDtypeStruct((3,) + p.shape[1:], p.dtype) for p in ps]
    return Comm(ps, out_shapes, {}, 3 * n, 3 * n, 1, build)


def _pack(arrays, dtype, row_align):
    lead = arrays[0].shape[:-1]
    quantum = row_align * PACK_COLS
    parts, sizes = [], []
    for a in arrays:
        n = a.shape[-1]
        padded = _round_up(n, quantum)
        a = a.astype(dtype)
        if padded != n:
            a = jnp.pad(a, [(0, 0)] * len(lead) + [(0, padded - n)])
        parts.append(a.reshape(*lead, padded // PACK_COLS, PACK_COLS))
        sizes.append((n, padded // PACK_COLS))
    return jnp.concatenate(parts, axis=len(lead)), sizes


def _unpack(packed, sizes):
    lead = packed.shape[:-2]
    out, row = [], 0
    for n, rows in sizes:
        part = lax.slice_in_dim(packed, row, row + rows, axis=len(lead))
        out.append(part.reshape(*lead, rows * PACK_COLS)[..., :n])
        row += rows
    return out


def _unshard(gathered, axis):
    _, L, a, b = gathered.shape
    if axis == 1:
        return [gathered[:, l].reshape(N_DEV * a, b) for l in range(L)]
    return [jnp.transpose(gathered[:, l], (1, 0, 2)).reshape(a, N_DEV * b) for l in range(L)]


def _reshard(fulls, axis):
    blocks = []
    for f in fulls:
        A, B = f.shape
        if axis == 1:
            blocks.append(f.reshape(N_DEV, A // N_DEV, B))
        else:
            blocks.append(jnp.transpose(f.reshape(A, N_DEV, B // N_DEV), (1, 0, 2)))
    return jnp.stack(blocks, axis=1)


def _as3(a):
    return a if a.ndim == 3 else a[:, None, :]


def _prep_big(name, w, dims):
    w = w.astype(BF16)
    L, a, b = w.shape
    if name == 'mla_w_in':
        return jnp.pad(w, ((0, 0), (0, 0), (0, dims['h_width'] - b)))
    if name == 'mla_w_uq':
        hd = MLA_NOPE + MLA_ROPE
        w = jnp.pad(w.reshape(L, a, b // hd, hd), ((0, 0), (0, 0), (0, 0), (0, MLA_QK_PAD - hd)))
        return w.reshape(L, a, b // hd * MLA_QK_PAD)
    if name == 'ffn_w_up':
        return jnp.pad(w, ((0, 0), (0, 0), (0, _round_up(b, CONV_COLS) - b)))
    return w


def _unprep_big(name, g, shape):
    L, a, b = shape
    if name == 'mla_w_uq':
        hd = MLA_NOPE + MLA_ROPE
        return g.reshape(L, a, b // hd, MLA_QK_PAD)[..., :hd].reshape(L, a, b)
    return g[:, :, :b]


def _rope_tables(positions):
    inv = 1.0 / (ROPE_THETA ** (jnp.arange(0, MLA_ROPE, 2, dtype=F32) / MLA_ROPE))
    ang = positions.astype(F32)[:, None] * inv
    cos, sin = jnp.cos(ang), jnp.sin(ang)
    one, zero = jnp.ones_like(cos), jnp.zeros_like(cos)
    a = jnp.concatenate([cos, cos, one, one], axis=1)
    up = jnp.concatenate([sin, zero, zero, zero], axis=1)
    down = jnp.concatenate([zero, sin, zero, zero], axis=1)
    return (a, -up, down), (a, up, -down)


def _rows_full(w):
    return w.reshape(w.shape[0] * w.shape[1], w.shape[2])


def _ops(depth):
    alpha = (2 * depth) ** 0.25
    return {'ln_res': rw_op(_ln_res_fn(alpha), "ln_res", 2, [F32], shadow=True),
            'ple': rw_op(_ple_fn, "ple", 3, [F32], shadow=True), 'gla_gate': rw_op(_gla_gate_fn, "gla_gate", 1, [F32])}


MLA_PRE_W = ['mla_w_in', 'mla_q_norm', 'mla_kv_norm', 'mla_w_uq', 'mla_w_uk', 'mla_w_uv']
FFN_IN_W = ['mla_w_o', 'ln1_g', 'ln1_b', 'ffn_conv_w', 'ffn_conv_b']
FFN_OUT_W = ['ffn_w_down', 'ln2_g', 'ln2_b', 'ple_w_gate', 'ple_w_proj', 'ple_b_gate']


def _mla_heads(wl, j):
    return N_DEV * wl['mla_w_uk'][j].shape[2] // MLA_NOPE


def _mla_pre(x, xb, wl, j):
    w_uq, w_uk, w_uv = wl['mla_w_uq'][j], wl['mla_w_uk'][j], wl['mla_w_uv'][j]
    h = linear_sh(x, xb, _rows_full(wl['mla_w_in'][j]), F32)
    mla_norm = rw_op(_mla_norm_fn(w_uq.shape[1], w_uk.shape[1]), "mla_norm", 1, [BF16, BF16, F32])
    cq, ckv, kr_raw = mla_norm(h, wl['mla_q_norm'][j], wl['mla_kv_norm'][j])
    return linear(cq, w_uq, BF16), linear(ckv, w_uk, BF16), linear(ckv, w_uv, BF16), kr_raw


def _gla_mixer(x, xb, wl, j, ops):
    w_in3, w_a2 = wl['gla_w_in'][j], wl['gla_w_a2'][j]
    w_o = _rows_full(wl['gla_w_o'][j])
    w_in = jnp.transpose(w_in3, (1, 0, 2)).reshape(w_in3.shape[1], N_DEV * w_in3.shape[2])
    heads = w_o.shape[0] // GLA_DV
    n_main = 2 * heads * GLA_DK + 2 * heads * GLA_DV
    w_a = jnp.pad(w_in[:, n_main:], ((0, 0), (0, LANES - GLA_RANK)))
    w_a2_p = jnp.pad(w_a2, ((0, LANES - GLA_RANK), (0, 0))).astype(BF16)
    hm = linear_sh(x, xb, w_in[:, :n_main], BF16)
    ha = linear_sh(x, xb, w_a, BF16)
    (la,) = ops['gla_gate'](linear(ha, w_a2_p, F32), wl['gla_b_a'][j])
    return linear(gla_core(hm, la, wl['gla_o_norm'][j], heads), w_o, F32)


def _ffn_in(x, m, wl, i, ops, bp):
    x1, x1b = ops['ln_res'](x, m, wl['ln1_g'][i], wl['ln1_b'][i])
    cw, cb = wl['ffn_conv_w'][i], wl['ffn_conv_b'][i]
    bu = cw.shape[1] // N_DEV
    cwp = jnp.pad(cw.reshape(CONV_W, N_DEV, bu), ((0, 0), (0, 0), (0, bp - bu))).reshape(CONV_W, N_DEV * bp)
    cbp = jnp.pad(cb.reshape(1, N_DEV, bu), ((0, 0), (0, 0), (0, bp - bu))).reshape(1, N_DEV * bp)
    return x1, lax.stop_gradient(x1b), cwp, cbp


def _ffn_out(x1, a, wl, p_i, i, ops):
    w_down3 = wl['ffn_w_down'][i]
    half, bu, d_model = N_DEV // 2, 2 * w_down3.shape[1], w_down3.shape[2]
    bp = a.shape[1] // half
    w_down = jnp.pad(w_down3.reshape(half, bu, d_model), ((0, 0), (0, bp - bu), (0, 0))).reshape(half * bp, d_model)
    f = linear(a, w_down, F32)
    x2, x2b = ops['ln_res'](x1, f, wl['ln2_g'][i], wl['ln2_b'][i])
    glog = linear_sh(x2, lax.stop_gradient(x2b), _rows_full(wl['ple_w_gate'][i]), F32)
    pp = linear(p_i, wl['ple_w_proj'][i], F32)
    x, xb = ops['ple'](x2, glog, pp, wl['ple_b_gate'][i])
    return x, lax.stop_gradient(xb)


def _layer(x, xb, wl, p_i, aux, i, ops):
    j = i // 2
    if i % 2 == 0:
        q_raw, kn, v, kr_raw = _mla_pre(x, xb, wl, j)
        o = make_attention(aux, _mla_heads(wl, j))[0](q_raw, kn, v, kr_raw)
        m = linear(o, _rows_full(wl['mla_w_o'][j]), F32)
    else:
        m = _gla_mixer(x, xb, wl, j, ops)
    x1, x1b, cwp, cbp = _ffn_in(x, m, wl, i, ops, wl['ffn_w_up'][i].shape[2])
    return _ffn_out(x1, ffn_hidden(x1, x1b, wl['ffn_w_up'][i], cwp, cbp), wl, p_i, i, ops)


def kernel(x, p, positions, mla_w_in, mla_q_norm, mla_kv_norm, mla_w_uq, mla_w_uk, mla_w_uv, mla_w_o, gla_w_in, gla_w_a2, gla_b_a, gla_o_norm, gla_w_o, ln1_g, ln1_b, ln2_g, ln2_b, ffn_w_up, ffn_conv_w, ffn_conv_b, ffn_w_down, ple_w_proj, ple_w_gate, ple_b_gate, loss_target, m_mla_w_in, m_mla_q_norm, m_mla_kv_norm, m_mla_w_uq, m_mla_w_uk, m_mla_w_uv, m_mla_w_o, m_gla_w_in, m_gla_w_a2, m_gla_b_a, m_gla_o_norm, m_gla_w_o, m_ln1_g, m_ln1_b, m_ln2_g, m_ln2_b, m_ffn_w_up, m_ffn_conv_w, m_ffn_conv_b, m_ffn_w_down, m_ple_w_proj, m_ple_w_gate, m_ple_b_gate, v_mla_w_in, v_mla_q_norm, v_mla_kv_norm, v_mla_w_uq, v_mla_w_uk, v_mla_w_uv, v_mla_w_o, v_gla_w_in, v_gla_w_a2, v_gla_b_a, v_gla_o_norm, v_gla_w_o, v_ln1_g, v_ln1_b, v_ln2_g, v_ln2_b, v_ffn_w_up, v_ffn_conv_w, v_ffn_conv_b, v_ffn_w_down, v_ple_w_proj, v_ple_w_gate, v_ple_b_gate):
    w = dict(zip(WEIGHTS, (mla_w_in, mla_q_norm, mla_kv_norm, mla_w_uq, mla_w_uk, mla_w_uv, mla_w_o, gla_w_in, gla_w_a2,
                           gla_b_a, gla_o_norm, gla_w_o, ln1_g, ln1_b, ln2_g, ln2_b, ffn_w_up, ffn_conv_w, ffn_conv_b,
                           ffn_w_down, ple_w_proj, ple_w_gate, ple_b_gate)))
    m_in = dict(zip(WEIGHTS, (m_mla_w_in, m_mla_q_norm, m_mla_kv_norm, m_mla_w_uq, m_mla_w_uk, m_mla_w_uv, m_mla_w_o,
                              m_gla_w_in, m_gla_w_a2, m_gla_b_a, m_gla_o_norm, m_gla_w_o, m_ln1_g, m_ln1_b, m_ln2_g,
                              m_ln2_b, m_ffn_w_up, m_ffn_conv_w, m_ffn_conv_b, m_ffn_w_down, m_ple_w_proj, m_ple_w_gate,
                              m_ple_b_gate)))
    v_in = dict(zip(WEIGHTS, (v_mla_w_in, v_mla_q_norm, v_mla_kv_norm, v_mla_w_uq, v_mla_w_uk, v_mla_w_uv, v_mla_w_o,
                              v_gla_w_in, v_gla_w_a2, v_gla_b_a, v_gla_o_norm, v_gla_w_o, v_ln1_g, v_ln1_b, v_ln2_g,
                              v_ln2_b, v_ffn_w_up, v_ffn_conv_w, v_ffn_conv_b, v_ffn_w_down, v_ple_w_proj, v_ple_w_gate,
                              v_ple_b_gate)))
    _uid[0] = itertools.count()
    x2d, target, pos = x[0], loss_target[0], positions[0]
    p3 = p[:, 0]
    dims = {'h_width': mla_w_uq.shape[1] + mla_w_uk.shape[1] + LANES}

    depth = ln1_g.shape[0]
    ops = _ops(depth)
    cid = pos // CHUNK
    aux = {'rope': _rope_tables(pos), 'cidq': cid[:, None], 'cidk': cid[None, :],
           'need': _mask_table(cid, _attn_tile(pos.shape[0]))}

    in_layer0 = set(MLA_PRE_W + FFN_IN_W + FFN_OUT_W + ['ffn_w_up'])
    prepped = {n: _prep_big(n, w[n], dims) for n in BIG}
    first_names = [n for n in BIG if n in MLA_PRE_W + ['mla_w_o']]
    mid_names = [n for n in BIG if n in in_layer0 and n not in first_names]
    rest_names = [n for n in BIG if prepped[n].shape[0] > (1 if n in in_layer0 else 0)]
    rest_from = {n: (1 if n in in_layer0 else 0) for n in rest_names}
    small3 = [_as3(w[n]) for n in SMALL]
    small_packed, small_sizes = _pack([s.reshape(1, -1) for s in small3], F32, 8)
    first = all_gather([prepped[n][:1] for n in first_names] + [small_packed])
    wl = {n: [None] * prepped[n].shape[0] for n in BIG}
    for n, g in zip(first_names, first):
        wl[n][0] = g[0]
    for n, s3, flat in zip(SMALL, small3, _unpack(first[-1][0], small_sizes)):
        wl[n] = _unshard(flat.reshape(N_DEV, *s3.shape), SHARD_AXIS[n] if w[n].ndim == 3 else 2)
    for n in REPL:
        wl[n] = [w[n][l][None, :] for l in range(w[n].shape[0])]

    def pick(names, layer0):
        return {n: [wl[n][l] if (l == 0 and n in in_layer0) == layer0 else None for l in range(len(wl[n]))] for n in names}

    heads0 = _mla_heads(wl, 0)
    _, attn_run, attn_bwd = make_attention(aux, heads0)
    x2db = x2d.astype(BF16)
    pre, vjp_pre = jax.vjp(lambda x_, wl_: _mla_pre(x_, x2db, wl_, 0), x2d, pick(MLA_PRE_W, True))
    o, attn_res, partial = attn_run(*pre, comm_gather_own([prepped[n][:1] for n in mid_names]
                                                          + [prepped[n][rest_from[n]:] for n in rest_names]))
    for n, g in zip(mid_names, _run_comm(comm_gather_pass(partial[:len(mid_names)]), "gather_pass")):
        wl[n][0] = g[0]
    partial = partial[len(mid_names):]
    w_up0 = wl['ffn_w_up'][0]

    def ffn_in(x_, o_, wl_):
        x1_, x1b_, cwp_, cbp_ = _ffn_in(x_, linear(o_, _rows_full(wl_['mla_w_o'][0]), F32), wl_, 0, ops, w_up0.shape[2])
        return (x1_, cwp_, cbp_), x1b_

    (x1, cwp, cbp), vjp_in, x1b = jax.vjp(ffn_in, x2d, o, pick(FFN_IN_W, True), has_aux=True)
    a, ffn_res, rest = _ffn_hidden_fwd(x1, x1b, w_up0, cwp, cbp, comm_gather_pass(partial))
    for n, g in zip(rest_names, rest):
        for l in range(g.shape[0]):
            wl[n][rest_from[n] + l] = g[l]
    x_l0, vjp_out, x_l0b = jax.vjp(lambda x1_, a_, wl_: _ffn_out(x1_, a_, wl_, p3[0], 0, ops), x1, a,
                                   pick(FFN_OUT_W, True), has_aux=True)

    def tail(x_, wl_):
        xb_ = x_l0b
        for i in range(1, depth):
            x_, xb_ = _layer(x_, xb_, wl_, p3[i], aux, i, ops)
        return x_

    y, vjp_tail = jax.vjp(tail, x_l0, pick(WEIGHTS, False))
    sq, dy = _loss_call(y, target)

    dwl = {n: [None] * len(wl[n]) for n in WEIGHTS}

    def keep(part):
        for n, per_layer in part.items():
            for l, g in enumerate(per_layer):
                if g is not None:
                    dwl[n][l] = g

    x_c, y_c, c_place = _place()
    c_idx = jnp.reshape(c_place, (1,)).astype(jnp.int32)
    chip_idx = jnp.reshape(2 * x_c + y_c, (1,)).astype(jnp.int32)
    dx_l0, d_tail = vjp_tail(dy)
    keep(d_tail)
    g_rest = [jnp.stack(dwl[n][rest_from[n]:], axis=0) for n in rest_names]
    dx1_out, da, d_out = vjp_out(dx_l0)
    keep(d_out)
    (dx1_ffn, _, dw_up0, dcwp, dcbp), recv1 = _ffn_hidden_bwd(ffn_res, da, comm_rs_pair(g_rest))
    dwl['ffn_w_up'][0] = dw_up0
    g_mid = [dwl[n][0][None] for n in mid_names]
    g_rest, recv1 = g_rest + g_mid, list(recv1) + list(_rs_pair_exchange(g_mid))
    p1 = [_rs_pair_add(g, r, c_idx) for g, r in zip(g_rest, recv1)]
    dx_in, do, d_in = vjp_in((dx1_out + dx1_ffn, dcwp, dcbp))
    keep(d_in)
    d_pre_in, recv2 = attn_bwd(attn_res, do, comm_rs_chip(p1))
    red_rest = [_rs_final_add(p_, r, chip_idx) for p_, r in zip(p1, recv2)]
    dx_pre, d_pre = vjp_pre(d_pre_in)
    keep(d_pre)
    dx = dx_pre + dx_in

    small_blocks = [_reshard(dwl[n], SHARD_AXIS[n] if w[n].ndim == 3 else 2).reshape(N_DEV, -1) for n in SMALL]
    small_grad_packed, _ = _pack(small_blocks, F32, 8)
    red_first = reduce_scatter([dwl[n][0][None] for n in first_names] + [small_grad_packed[None]])
    by_layer = {n: [] for n in BIG}
    for n, g in zip(first_names, red_first):
        by_layer[n].append(g)
    for n, g in zip(mid_names, red_rest[len(rest_names):]):
        by_layer[n].append(g)
    for n, g in zip(rest_names, red_rest):
        by_layer[n].append(g)
    grads = {n: _unprep_big(n, jnp.concatenate(by_layer[n], axis=0), w[n].shape) for n in BIG}
    for n, f in zip(SMALL, _unpack(red_first[-1][0], small_sizes)):
        grads[n] = f.reshape(w[n].shape)

    repl_flat = [jnp.concatenate([g.reshape(-1) for g in dwl[n]]).reshape(1, -1) for n in REPL]
    loss_part = 0.5 * jnp.sum(sq) / sq.shape[1]
    packed, repl_sizes = _pack(repl_flat + [loss_part.reshape(1, 1)], F32, 8)
    summed = _unpack(all_reduce_small(packed[0])[None], repl_sizes)
    for n, f in zip(REPL, summed[:-1]):
        grads[n] = f.reshape(w[n].shape)
    loss = summed[-1].reshape(())

    delta, new_m, new_v = {}, {}, {}
    for n in WEIGHTS:
        delta[n], new_m[n], new_v[n] = _adamw_call(w[n], grads[n], m_in[n], v_in[n])
    return (loss, dx[None], *[grads[n] for n in WEIGHTS], *[delta[n] for n in WEIGHTS],
            *[new_m[n] for n in WEIGHTS], *[new_v[n] for n in WEIGHTS])
```

```python
import functools
import itertools

import jax
import jax.numpy as jnp
from jax import lax
from jax.experimental import pallas as pl
from jax.experimental.pallas import tpu as pltpu

F32 = jnp.float32
BF16 = jnp.bfloat16
MESH = pl.DeviceIdType.MESH
N_DEV = 8

EPS = 1e-5
NEG_INF = -1e30
CHUNK = 64
Q_BLOCK = 128
MLA_NOPE = 128
MLA_ROPE = 64
MLA_V = 128
MLA_QK_PAD = 256
ROPE_THETA = 10000.0
GLA_DK = 128
GLA_DV = 256
GLA_RANK = 16
GLA_TAU = 16.0
CONV_W = 3
ADAM_LR = 0.001
ADAM_B1 = 0.9
ADAM_B2 = 0.999
ADAM_EPS = 1e-08
ADAM_WD = 0.01
ADAM_STEP = 10
LOG2E = 1.4426950408889634

LANES = 128
PACK_COLS = 1024
VMEM_LIMIT = 56 * 1024 * 1024
MM_VMEM_BUDGET = 38 * 1024 * 1024

WEIGHTS = ['mla_w_in', 'mla_q_norm', 'mla_kv_norm', 'mla_w_uq', 'mla_w_uk', 'mla_w_uv', 'mla_w_o', 'gla_w_in',
           'gla_w_a2', 'gla_b_a', 'gla_o_norm', 'gla_w_o', 'ln1_g', 'ln1_b', 'ln2_g', 'ln2_b', 'ffn_w_up',
           'ffn_conv_w', 'ffn_conv_b', 'ffn_w_down', 'ple_w_proj', 'ple_w_gate', 'ple_b_gate']
SHARD_AXIS = {'mla_w_in': 1, 'mla_q_norm': None, 'mla_kv_norm': None, 'mla_w_uq': 2, 'mla_w_uk': 2, 'mla_w_uv': 2,
              'mla_w_o': 1, 'gla_w_in': 2, 'gla_w_a2': 2, 'gla_b_a': 1, 'gla_o_norm': 1, 'gla_w_o': 1,
              'ln1_g': None, 'ln1_b': None, 'ln2_g': None, 'ln2_b': None, 'ffn_w_up': 2, 'ffn_conv_w': 2,
              'ffn_conv_b': None, 'ffn_w_down': 1, 'ple_w_proj': 2, 'ple_w_gate': 1, 'ple_b_gate': None}
BIG = ['mla_w_in', 'mla_w_uq', 'mla_w_uk', 'mla_w_uv', 'mla_w_o', 'gla_w_in', 'gla_w_o', 'ffn_w_up', 'ffn_w_down',
       'ple_w_proj', 'ple_w_gate']
SMALL = ['gla_w_a2', 'gla_b_a', 'gla_o_norm', 'ffn_conv_w']
REPL = [n for n in WEIGHTS if SHARD_AXIS[n] is None]

_uid = [itertools.count()]


def _nm(base):
    return f"{base}_{next(_uid[0])}"


def _cp(sem=None):
    return pltpu.CompilerParams(dimension_semantics=sem, vmem_limit_bytes=VMEM_LIMIT)


def _round_up(n, m):
    return -(-n // m) * m


class Comm:
    MID_AT = 0.6

    def __init__(self, inputs, out_shapes, aliases, n_send, n_recv, n_local, build):
        self.inputs, self.out_shapes, self.aliases = list(inputs), list(out_shapes), dict(aliases)
        self.n_send, self.n_recv, self.n_local, self.build = n_send, n_recv, n_local, build


def _pcall(body, name, grid, n_prefetch, in_specs, out_specs, out_shape, scratch, sem, args, comm=None):
    in_specs, out_specs, out_shape, scratch, args = list(in_specs), list(out_specs), list(out_shape), list(scratch), list(args)
    n_in, n_out, n_scr = len(in_specs), len(out_specs), len(scratch)
    aliases = {}
    kernel_body = body
    if comm is not None:
        ci, co = len(comm.inputs), len(comm.out_shapes)
        any_spec = pl.BlockSpec(memory_space=pl.ANY)
        in_specs += [any_spec] * ci
        out_specs += [any_spec] * co
        out_shape += comm.out_shapes
        scratch += [pltpu.SemaphoreType.DMA((comm.n_send,)), pltpu.SemaphoreType.DMA((comm.n_recv,)),
                    pltpu.SemaphoreType.DMA((comm.n_local,))]
        aliases = {n_prefetch + n_in + k: n_out + v for k, v in comm.aliases.items()}
        args += comm.inputs
        sem = ("arbitrary",) * len(grid)

        def kernel_body(*refs):
            pre, r = refs[:n_prefetch], refs[n_prefetch:]
            ins, cin = r[:n_in], r[n_in:n_in + ci]
            outs, cout = r[n_in + ci:n_in + ci + n_out], r[n_in + ci + n_out:n_in + ci + n_out + co]
            scr = r[n_in + ci + n_out + co:n_in + ci + n_out + co + n_scr]
            send_sems, recv_sems, local_sems = r[-3:]
            first = functools.reduce(lambda a, b: a & b, [pl.program_id(d) == 0 for d in range(len(grid))])
            last = functools.reduce(lambda a, b: a & b, [pl.program_id(d) == grid[d] - 1 for d in range(len(grid))])
            built = comm.build(cin, cout, send_sems, recv_sems, local_sems)
            starts, waits = built[0], built[-1]
            mid_waits, mid_starts = (built[1], built[2]) if len(built) == 4 else ([], [])

            @pl.when(first)
            def _():
                for cp in starts:
                    cp.start()

            if mid_starts:
                step = pl.program_id(0)
                for d in range(1, len(grid)):
                    step = step * grid[d] + pl.program_id(d)
                n_steps = functools.reduce(lambda a, b: a * b, grid)

                @pl.when(step == int(n_steps * comm.MID_AT))
                def _():
                    for wait in mid_waits:
                        wait()
                    for cp in mid_starts:
                        cp.start()

            body(*pre, *ins, *outs, *scr)

            @pl.when(last)
            def _():
                for wait in waits:
                    wait()

    grid_spec = pltpu.PrefetchScalarGridSpec(num_scalar_prefetch=n_prefetch, grid=grid, in_specs=in_specs,
                                             out_specs=out_specs, scratch_shapes=scratch)
    return pl.pallas_call(kernel_body, name=_nm(name), grid_spec=grid_spec, out_shape=tuple(out_shape),
                          input_output_aliases=aliases, compiler_params=_cp(sem))(*args)


def _run_comm(comm, name):
    return _pcall(lambda: None, name, (1,), 0, [], [], [], [], ("arbitrary",), [], comm)


def _divisor_tiles(n, cap):
    if n % LANES:
        return [n]
    out = [t for t in range(LANES, min(n, cap) + 1, LANES) if n % t == 0]
    return out or [n]


def _mm_tiles(M, N, K, abytes, bbytes, obytes, tn_fixed=None, tk_fixed=None):
    best = None
    for tm in _divisor_tiles(M, 2048):
        for tn in ([tn_fixed] if tn_fixed else _divisor_tiles(N, 1536)):
            for tk in ([tk_fixed] if tk_fixed else _divisor_tiles(K, 4096)):
                vmem = 2 * (tm * tk * abytes + tk * tn * bbytes + tm * tn * obytes) + tm * tn * 4
                if vmem > MM_VMEM_BUDGET:
                    continue
                key = (tm * tn * tk, tk)
                if best is None or key > best[0]:
                    best = (key, (tm, tn, tk))
    assert best is not None, (M, N, K)
    return best[1]


def _mm(a, b, mode, out_dtype, base="mm", blocks=None):
    blk0, nblk = blocks if blocks else (0, 1)
    tn_fixed = tk_fixed = None
    if mode == "nn":
        M, K = a.shape
        N = nblk * b.shape[2] if blocks else b.shape[1]
        tn_fixed = b.shape[2] if blocks else None
    elif mode == "nt":
        M, K = a.shape
        N = b.shape[1] if blocks else b.shape[0]
        kb = 2 if blocks and nblk % 2 == 0 and blk0 % 2 == 0 else 1
        tk_fixed = kb * b.shape[2] if blocks else None
        assert not blocks or K == nblk * b.shape[2]
    else:
        (K, M), N = a.shape, b.shape[1]
        tn_fixed = N // nblk if blocks else None
    tm, tn, tk = _mm_tiles(M, N, K, a.dtype.itemsize, b.dtype.itemsize, jnp.dtype(out_dtype).itemsize, tn_fixed, tk_fixed)
    nk = K // tk
    out_shape = jax.ShapeDtypeStruct((M, N), out_dtype)
    out_spec = pl.BlockSpec((tm, tn), lambda i, j, k: (i, j))
    if mode == "nn":
        a_spec = pl.BlockSpec((tm, tk), lambda i, j, k: (i, k))
        b_spec = (pl.BlockSpec((None, tk, tn), lambda i, j, k: (blk0 + j, k, 0)) if blocks
                  else pl.BlockSpec((tk, tn), lambda i, j, k: (k, j)))
        dims = (((1,), (0,)), ((), ()))
    elif mode == "nt":
        a_spec = pl.BlockSpec((tm, tk), lambda i, j, k: (i, k))
        b_spec = (pl.BlockSpec((kb, tn, tk // kb), lambda i, j, k: (blk0 // kb + k, j, 0)) if blocks
                  else pl.BlockSpec((tn, tk), lambda i, j, k: (j, k)))
        dims = (((1,), (1,)), ((), ()))
    else:
        a_spec = pl.BlockSpec((tk, tm), lambda i, j, k: (k, i))
        b_spec = pl.BlockSpec((tk, tn), lambda i, j, k: (k, j))
        dims = (((0,), (0,)), ((), ()))
        if blocks:
            out_shape = jax.ShapeDtypeStruct((nblk, M, tn), out_dtype)
            out_spec = pl.BlockSpec((None, tm, tn), lambda i, j, k: (j, i, 0))

    def body(a_ref, b_ref, o_ref, acc_ref):
        if mode == "nt" and blocks:
            bn = tk // kb
            part = sum(lax.dot_general(a_ref[:, q * bn:(q + 1) * bn].astype(BF16), b_ref[q].astype(BF16), dims,
                                       preferred_element_type=F32) for q in range(kb))
        else:
            part = lax.dot_general(a_ref[...].astype(BF16), b_ref[...].astype(BF16), dims, preferred_element_type=F32)
        if nk == 1:
            o_ref[...] = part.astype(o_ref.dtype)
        else:
            k = pl.program_id(2)

            @pl.when(k == 0)
            def _():
                acc_ref[...] = part

            @pl.when(k > 0)
            def _():
                acc_ref[...] += part

            @pl.when(k == nk - 1)
            def _():
                o_ref[...] = acc_ref[...].astype(o_ref.dtype)

    return pl.pallas_call(
        body, name=_nm(base), grid=(M // tm, N // tn, nk), out_shape=out_shape,
        in_specs=[a_spec, b_spec], out_specs=out_spec,
        scratch_shapes=[pltpu.VMEM((tm, tn) if nk > 1 else (8, LANES), F32)],
        compiler_params=_cp(("parallel", "parallel", "arbitrary")),
    )(a, b)


def _all_blocks(w):
    return (0, w.shape[0]) if w.ndim == 3 else None


@functools.partial(jax.custom_vjp, nondiff_argnums=(2,))
def linear(a, w, out_dtype):
    return _mm(a, w, "nn", out_dtype, "lin_fwd", _all_blocks(w))


def _linear_fwd(a, w, out_dtype):
    return _mm(a, w, "nn", out_dtype, "lin_fwd", _all_blocks(w)), (a, w)


def _linear_bwd(out_dtype, res, dy):
    a, w = res
    return (_mm(dy, w, "nt", a.dtype, "lin_dx", _all_blocks(w)), _mm(a, dy, "tn", w.dtype, "lin_dw", _all_blocks(w)))


linear.defvjp(_linear_fwd, _linear_bwd)


@functools.partial(jax.custom_vjp, nondiff_argnums=(3,))
def linear_sh(a, a_bf16, w, out_dtype):
    return _mm(a_bf16, w, "nn", out_dtype, "lin_fwd", _all_blocks(w))


def _linear_sh_fwd(a, a_bf16, w, out_dtype):
    return _mm(a_bf16, w, "nn", out_dtype, "lin_fwd", _all_blocks(w)), (a_bf16, w, jnp.zeros((), a.dtype))


def _linear_sh_bwd(out_dtype, res, dy):
    a_bf16, w, tok = res
    return (_mm(dy, w, "nt", tok.dtype, "lin_dx", _all_blocks(w)), jnp.zeros_like(a_bf16),
            _mm(a_bf16, dy, "tn", w.dtype, "lin_dw", _all_blocks(w)))


linear_sh.defvjp(_linear_sh_fwd, _linear_sh_bwd)


def _row_tile(S, width, forward=False):
    tr = 512 if width <= 1024 else 256
    return min(2 * tr if forward else tr, S)


def _rw_fwd(f, rows, params, out_dtypes, base):
    S = rows[0][0].shape[0]
    tr = _row_tile(S, max(w for _, w, _ in rows), forward=True)
    n_in = len(rows) + len(params)
    avals = [jax.ShapeDtypeStruct((tr, w), F32) for _, w, _ in rows] + [jax.ShapeDtypeStruct(p.shape, F32) for p in params]
    outs = jax.eval_shape(f, *avals)

    def body(*refs):
        vals = [r[...].astype(F32) for r in refs[:n_in]]
        for o_ref, r in zip(refs[n_in:], f(*vals)):
            o_ref[...] = r.astype(o_ref.dtype)

    in_specs = [pl.BlockSpec((tr, w), functools.partial(lambda i, cb: (i, cb), cb=cb)) for _, w, cb in rows]
    in_specs += [pl.BlockSpec(p.shape, lambda i: (0, 0)) for p in params]
    return pl.pallas_call(
        body, name=_nm(base), grid=(S // tr,),
        out_shape=tuple(jax.ShapeDtypeStruct((S, o.shape[1]), dt) for o, dt in zip(outs, out_dtypes)),
        in_specs=in_specs, out_specs=tuple(pl.BlockSpec((tr, o.shape[1]), lambda i: (i, 0)) for o in outs),
        compiler_params=_cp(("parallel",)),
    )(*[a for a, _, _ in rows], *params)


def _rw_bwd(f, rows, params, cts, row_grad_dtypes, base):
    S = rows[0][0].shape[0]
    tr = _row_tile(S, max(w for _, w, _ in rows))
    n_rows, n_par, n_ct = len(rows), len(params), len(cts)
    want = [k for k, dt in enumerate(row_grad_dtypes) if dt is not None]

    def body(*refs):
        in_refs = refs[:n_rows + n_par]
        ct_refs = refs[n_rows + n_par:n_rows + n_par + n_ct]
        out_refs = refs[n_rows + n_par + n_ct:]
        vals = [r[...].astype(F32) for r in in_refs]
        _, vjp_fn = jax.vjp(f, *vals)
        grads = vjp_fn(tuple(c[...].astype(F32) for c in ct_refs))
        for o_ref, k in zip(out_refs[:len(want)], want):
            o_ref[...] = grads[k].astype(o_ref.dtype)
        i = pl.program_id(0)
        for o_ref, g in zip(out_refs[len(want):], grads[n_rows:]):
            @pl.when(i == 0)
            def _(o_ref=o_ref, g=g):
                o_ref[...] = g

            @pl.when(i > 0)
            def _(o_ref=o_ref, g=g):
                o_ref[...] += g

    in_specs = [pl.BlockSpec((tr, w), functools.partial(lambda i, cb: (i, cb), cb=cb)) for _, w, cb in rows]
    in_specs += [pl.BlockSpec(p.shape, lambda i: (0, 0)) for p in params]
    in_specs += [pl.BlockSpec((tr, c.shape[1]), lambda i: (i, 0)) for c in cts]
    out_shape = [jax.ShapeDtypeStruct((S, rows[k][1]), row_grad_dtypes[k]) for k in want]
    out_specs = [pl.BlockSpec((tr, rows[k][1]), lambda i: (i, 0)) for k in want]
    out_shape += [jax.ShapeDtypeStruct(p.shape, F32) for p in params]
    out_specs += [pl.BlockSpec(p.shape, lambda i: (0, 0)) for p in params]
    res = pl.pallas_call(
        body, name=_nm(base), grid=(S // tr,), out_shape=tuple(out_shape),
        in_specs=in_specs, out_specs=tuple(out_specs), compiler_params=_cp(("arbitrary",)),
    )(*[a for a, _, _ in rows], *params, *cts)
    row_grads = [None] * n_rows
    for k, g in zip(want, res[:len(want)]):
        row_grads[k] = g
    return row_grads, list(res[len(want):])


def rw_op(f, base, n_rows, out_dtypes, shadow=False):
    f_fwd = (lambda *a: (lambda r: tuple(r) + (r[0],))(f(*a))) if shadow else f
    fwd_dtypes = list(out_dtypes) + ([BF16] if shadow else [])

    @jax.custom_vjp
    def op(*args):
        return fwd(*args)[0]

    def split(args):
        rows = [(a, a.shape[1], 0) for a in args[:n_rows]]
        return rows, list(args[n_rows:])

    def fwd(*args):
        rows, params = split(args)
        return tuple(_rw_fwd(f_fwd, rows, params, fwd_dtypes, base + "_fwd")), args

    def bwd(args, cts):
        rows, params = split(args)
        cts = list(cts)[:len(out_dtypes)]
        rg, pg = _rw_bwd(f, rows, params, cts, [a.dtype for a, _, _ in rows], base + "_bwd")
        return tuple(rg) + tuple(g.astype(p.dtype) for g, p in zip(pg, params))

    op.defvjp(fwd, bwd)
    return op


def _ln_res_fn(alpha):
    def f(x, m, g, b):
        z = alpha * x + m
        mu = jnp.mean(z, -1, keepdims=True)
        zc = z - mu
        var = jnp.mean(zc * zc, -1, keepdims=True)
        return (zc * lax.rsqrt(var + EPS) * g + b,)
    return f


def _rms(x, g):
    return x * lax.rsqrt(jnp.mean(x * x, -1, keepdims=True) + EPS) * g


def _mla_norm_fn(q_lora, kv_lora):
    def f(h, qn, kvn):
        return (_rms(h[:, :q_lora], qn), _rms(h[:, q_lora:q_lora + kv_lora], kvn),
                h[:, q_lora + kv_lora:q_lora + kv_lora + LANES])
    return f


def _log_sigmoid(z):
    return jnp.minimum(z, 0.0) - jnp.log(1.0 + jnp.exp(-jnp.abs(z)))


def _gla_gate_fn(z, b):
    return (_log_sigmoid(z + b) / GLA_TAU,)


def _ple_fn(x, glog, pp, b):
    return (x + jax.nn.sigmoid(glog + b) * pp,)


def _gla_out_fn(heads):
    def f(o, r, g):
        parts = []
        for h in range(heads):
            oh = o[:, h * GLA_DV:(h + 1) * GLA_DV]
            mu = jnp.mean(oh, -1, keepdims=True)
            oc = oh - mu
            var = jnp.mean(oc * oc, -1, keepdims=True)
            parts.append(oc * lax.rsqrt(var + EPS) * g[:, h * GLA_DV:(h + 1) * GLA_DV])
        return (jnp.concatenate(parts, axis=1) * (r * jax.nn.sigmoid(r)),)
    return f


def _rope_call(x, tabs, roped, out_dtype, base, fold=False, scale=None):
    S, C = x.shape
    nb = C // LANES
    tr = min(512 if C <= 1024 else 256, S)
    out_c = LANES if fold else C

    def rot(v, a, b1, b2):
        return v * a + pltpu.roll(v, 96, 1) * b1 + pltpu.roll(v, 32, 1) * b2

    def body(x_ref, a_ref, b1_ref, b2_ref, o_ref):
        a, b1, b2 = a_ref[...], b1_ref[...], b2_ref[...]
        if fold:
            v = x_ref[:, 0:LANES].astype(F32)
            for blk in range(1, nb):
                v = v + x_ref[:, blk * LANES:(blk + 1) * LANES].astype(F32)
            o_ref[...] = rot(v, a, b1, b2).astype(o_ref.dtype)
            return
        for blk in range(nb):
            v = x_ref[:, blk * LANES:(blk + 1) * LANES].astype(F32)
            if roped(blk):
                v = rot(v, a, b1, b2)
            if scale is not None:
                v = v * scale
            o_ref[:, blk * LANES:(blk + 1) * LANES] = v.astype(o_ref.dtype)

    row = lambda w: pl.BlockSpec((tr, w), lambda i: (i, 0))
    return pl.pallas_call(
        body, name=_nm(base), grid=(S // tr,), out_shape=jax.ShapeDtypeStruct((S, out_c), out_dtype),
        in_specs=[row(C), row(LANES), row(LANES), row(LANES)], out_specs=row(out_c),
        compiler_params=_cp(("parallel",)),
    )(x, *tabs)


def _attn_tile(S):
    return min(512, S)


def _tri_schedule(n, by_key):
    pairs = ([(i, j) for j in range(n) for i in range(j, n)] if by_key
             else [(i, j) for i in range(n) for j in range(i + 1)])
    return (jnp.asarray([p[0] for p in pairs], jnp.int32), jnp.asarray([p[1] for p in pairs], jnp.int32))


def _mask_table(cid, t):
    n = cid.shape[0] // t
    blocks = cid.reshape(n, t)
    cmin_q, cmax_k = jnp.min(blocks, axis=1), jnp.max(blocks, axis=1)
    need = (cmax_k[None, :] > cmin_q[:, None]) | jnp.eye(n, dtype=bool)
    return need.astype(jnp.int32).reshape(n * n)


def _attn_mask(cidq_ref, cidk_ref, i, j, t):
    qrow = i * t + lax.broadcasted_iota(jnp.int32, (t, 1), 0)
    kcol = j * t + lax.broadcasted_iota(jnp.int32, (1, t), 1)
    qlim = (qrow // Q_BLOCK + 1) * Q_BLOCK
    return (cidk_ref[...] <= cidq_ref[...]) & (kcol < qlim)


ATTN_FWD_HEADS = 8
ATTN_BWD_HEADS = 4
ATTN_SCALE = (MLA_NOPE + MLA_ROPE) ** -0.5
ATTN_SCALE2 = ATTN_SCALE * LOG2E


def _attn_fwd_call(q, kn, v, kr, aux, heads, comm=None):
    S = q.shape[0]
    t = _attn_tile(S)
    n = S // t
    hp = ATTN_FWD_HEADS if heads % ATTN_FWD_HEADS == 0 else 1
    qi_tab, kj_tab = _tri_schedule(n, False)

    def body(qi_ref, kj_ref, need_ref, q_ref, kn_ref, v_ref, kr_ref, cidq_ref, cidk_ref, o_ref, lse_ref,
             m_ref, l_ref, acc_ref):
        st = pl.program_id(1)
        i, j = qi_ref[st], kj_ref[st]

        @pl.when(j == 0)
        def _():
            m_ref[...] = jnp.full(m_ref.shape, NEG_INF, F32)
            l_ref[...] = jnp.zeros(l_ref.shape, F32)
            acc_ref[...] = jnp.zeros(acc_ref.shape, F32)

        def update(masked):
            mask = _attn_mask(cidq_ref, cidk_ref, i, j, t) if masked else None
            for hh in range(hp):
                lanes = slice(hh * LANES, (hh + 1) * LANES)
                k = jnp.concatenate([kn_ref[:, lanes], kr_ref[...]], axis=1)
                qh = q_ref[:, hh * MLA_QK_PAD:(hh + 1) * MLA_QK_PAD]
                s = lax.dot_general(qh, k, (((1,), (1,)), ((), ())), preferred_element_type=F32)
                if masked:
                    s = jnp.where(mask, s, NEG_INF)
                m_prev = m_ref[:, lanes]
                m_new = jnp.maximum(m_prev, jnp.max(s, axis=1, keepdims=True))
                alpha = jnp.exp2(m_prev - m_new)
                p = jnp.exp2(s - m_new[:, :1])
                l_ref[:, lanes] = alpha * l_ref[:, lanes] + jnp.sum(p, axis=1, keepdims=True)
                acc_ref[:, lanes] = alpha * acc_ref[:, lanes] + jnp.dot(p.astype(BF16), v_ref[:, lanes],
                                                                        preferred_element_type=F32)
                m_ref[:, lanes] = m_new

        need = need_ref[i * n + j]

        @pl.when(need != 0)
        def _():
            update(True)

        @pl.when(need == 0)
        def _():
            update(False)

        @pl.when(j == i)
        def _():
            o_ref[...] = (acc_ref[...] / l_ref[...]).astype(o_ref.dtype)
            lse_ref[...] = m_ref[...] + jnp.log(l_ref[...]) * LOG2E

    qmap = lambda h, s, qi, kj, need: (qi[s], h)
    kmap = lambda h, s, qi, kj, need: (kj[s], h)
    return _pcall(
        body, "attn_fwd", (heads // hp, qi_tab.shape[0]), 3,
        in_specs=[pl.BlockSpec((t, hp * MLA_QK_PAD), qmap), pl.BlockSpec((t, hp * MLA_NOPE), kmap),
                  pl.BlockSpec((t, hp * MLA_V), kmap),
                  pl.BlockSpec((t, LANES), lambda h, s, qi, kj, need: (kj[s], 0)),
                  pl.BlockSpec((t, 1), lambda h, s, qi, kj, need: (qi[s], 0)),
                  pl.BlockSpec((1, t), lambda h, s, qi, kj, need: (0, kj[s]))],
        out_specs=[pl.BlockSpec((t, hp * MLA_V), qmap), pl.BlockSpec((t, hp * LANES), qmap)],
        out_shape=[jax.ShapeDtypeStruct((S, heads * MLA_V), BF16), jax.ShapeDtypeStruct((S, heads * LANES), F32)],
        scratch=[pltpu.VMEM((t, hp * LANES), F32), pltpu.VMEM((t, hp * LANES), F32), pltpu.VMEM((t, hp * MLA_V), F32)],
        sem=("parallel", "arbitrary"),
        args=[qi_tab, kj_tab, aux['need'], q, kn, v, kr, aux['cidq'], aux['cidk']], comm=comm)


def _attn_bwd_call(q, kn, v, kr, o, lse, do, aux, heads, comm=None):
    S = q.shape[0]
    t = _attn_tile(S)
    n = S // t
    qi_tab, kj_tab = _tri_schedule(n, True)
    scale = ATTN_SCALE
    nt_dims = (((1,), (1,)), ((), ()))
    tn_dims = (((0,), (0,)), ((), ()))

    hp = ATTN_BWD_HEADS if heads % ATTN_BWD_HEADS == 0 else 1

    def body(qi_ref, kj_ref, need_ref, q_ref, kn_ref, v_ref, kr_ref, cidq_ref, cidk_ref, o_ref, lse_ref, do_ref,
             ta_ref, tb1_ref, tb2_ref, dq_ref, dkn_ref, dv_ref, dkr_ref, dk_acc, dv_acc, dq_acc):
        st = pl.program_id(1)
        i, j = qi_ref[st], kj_ref[st]

        @pl.when(st == 0)
        def _():
            dq_acc[...] = jnp.zeros(dq_acc.shape, F32)

        @pl.when(i == j)
        def _():
            dk_acc[...] = jnp.zeros(dk_acc.shape, F32)
            dv_acc[...] = jnp.zeros(dv_acc.shape, F32)

        rows = pl.ds(pl.multiple_of(i * t, t), t)

        def grads(masked):
            mask = _attn_mask(cidq_ref, cidk_ref, i, j, t) if masked else None
            for hh in range(hp):
                lanes = slice(hh * LANES, (hh + 1) * LANES)
                wide = slice(hh * MLA_QK_PAD, (hh + 1) * MLA_QK_PAD)
                k = jnp.concatenate([kn_ref[:, lanes], kr_ref[...]], axis=1)
                qt, do = q_ref[:, wide], do_ref[:, lanes]
                s = lax.dot_general(qt, k, nt_dims, preferred_element_type=F32)
                if masked:
                    s = jnp.where(mask, s, NEG_INF)
                dp = lax.dot_general(do, v_ref[:, lanes], nt_dims, preferred_element_type=F32)
                dsum = jnp.sum(do.astype(F32) * o_ref[:, lanes].astype(F32), axis=1, keepdims=True)
                p = jnp.exp2(s - lse_ref[:, hh * LANES:hh * LANES + 1])
                ds = (p * (dp - dsum) * scale).astype(BF16)
                dv_acc[:, lanes] += lax.dot_general(p.astype(BF16), do, tn_dims, preferred_element_type=F32)
                dk_acc[:, wide] += lax.dot_general(ds, qt, tn_dims, preferred_element_type=F32)
                dq_acc[rows, wide] += jnp.dot(ds, k, preferred_element_type=F32)

        need = need_ref[i * n + j]

        @pl.when(need != 0)
        def _():
            grads(True)

        @pl.when(need == 0)
        def _():
            grads(False)

        @pl.when(i == n - 1)
        def _():
            for hh in range(hp):
                lanes = slice(hh * LANES, (hh + 1) * LANES)
                off = hh * MLA_QK_PAD
                dkn_ref[:, lanes] = (dk_acc[:, off:off + MLA_NOPE] * (1.0 / ATTN_SCALE2)).astype(dkn_ref.dtype)
                dkr_ref[:, lanes] = dk_acc[:, off + MLA_NOPE:off + MLA_QK_PAD] * (1.0 / ATTN_SCALE2)
            dv_ref[...] = dv_acc[...].astype(dv_ref.dtype)

        @pl.when(i == j)
        def _():
            for hh in range(hp):
                off = hh * MLA_QK_PAD
                dq_ref[rows, off:off + MLA_NOPE] = dq_acc[rows, off:off + MLA_NOPE].astype(dq_ref.dtype)
                g = dq_acc[rows, off + MLA_NOPE:off + MLA_QK_PAD]
                g = g * ta_ref[...] + pltpu.roll(g, 96, 1) * tb1_ref[...] + pltpu.roll(g, 32, 1) * tb2_ref[...]
                dq_ref[rows, off + MLA_NOPE:off + MLA_QK_PAD] = g.astype(dq_ref.dtype)

    qmap = lambda h, s, qi, kj, need: (qi[s], h)
    kmap = lambda h, s, qi, kj, need: (kj[s], h)
    whole = pl.BlockSpec((t, LANES), lambda h, s, qi, kj, need: (qi[s], 0))
    return _pcall(
        body, "attn_bwd", (heads // hp, qi_tab.shape[0]), 3,
        in_specs=[pl.BlockSpec((t, hp * MLA_QK_PAD), qmap), pl.BlockSpec((t, hp * MLA_NOPE), kmap),
                  pl.BlockSpec((t, hp * MLA_V), kmap),
                  pl.BlockSpec((t, LANES), lambda h, s, qi, kj, need: (kj[s], 0)),
                  pl.BlockSpec((t, 1), lambda h, s, qi, kj, need: (qi[s], 0)),
                  pl.BlockSpec((1, t), lambda h, s, qi, kj, need: (0, kj[s])),
                  pl.BlockSpec((t, hp * MLA_V), qmap), pl.BlockSpec((t, hp * LANES), qmap),
                  pl.BlockSpec((t, hp * MLA_V), qmap), whole, whole, whole],
        out_specs=[pl.BlockSpec((S, hp * MLA_QK_PAD), lambda h, s, qi, kj, need: (0, h)),
                   pl.BlockSpec((t, hp * MLA_NOPE), kmap), pl.BlockSpec((t, hp * MLA_V), kmap),
                   pl.BlockSpec((t, hp * LANES), kmap)],
        out_shape=[jax.ShapeDtypeStruct((S, heads * MLA_QK_PAD), BF16), jax.ShapeDtypeStruct((S, heads * MLA_NOPE), BF16),
                   jax.ShapeDtypeStruct((S, heads * MLA_V), BF16), jax.ShapeDtypeStruct((S, heads * LANES), F32)],
        scratch=[pltpu.VMEM((t, hp * MLA_QK_PAD), F32), pltpu.VMEM((t, hp * MLA_V), F32),
                 pltpu.VMEM((S, hp * MLA_QK_PAD), F32)],
        sem=("parallel", "arbitrary"),
        args=[qi_tab, kj_tab, aux['need'], q, kn, v, kr, aux['cidq'], aux['cidk'], o, lse, do, *aux['rope'][1]],
        comm=comm)


def make_attention(aux, heads):
    tabs_f, tabs_b = aux['rope']
    odd, every = (lambda blk: blk % 2 == 1), (lambda blk: True)

    def run_host(q_raw, kn, v, kr_raw, comm=None):
        q = _rope_call(q_raw, tabs_f, odd, BF16, "rope_q", scale=ATTN_SCALE2)
        kr = _rope_call(kr_raw, tabs_f, every, BF16, "rope_k")
        o, lse, *carried = _attn_fwd_call(q, kn, v, kr, aux, heads, comm)
        return o, (q, kn, v, kr, o, lse), carried

    def bwd_host(res, do, comm=None):
        q, kn, v, kr, o, lse = res
        dq_raw, dkn, dv, dkr, *carried = _attn_bwd_call(q, kn, v, kr, o, lse, do, aux, heads, comm)
        return (dq_raw, dkn, dv, _rope_call(dkr, tabs_b, every, F32, "rope_dk", fold=True)), carried

    @jax.custom_vjp
    def attn(q_raw, kn, v, kr_raw):
        return run_host(q_raw, kn, v, kr_raw)[0]

    attn.defvjp(lambda *a: run_host(*a)[:2], lambda res, do: bwd_host(res, do)[0])
    return attn, run_host, bwd_host


GLA_ROWS = 256
GLA_HEADS_PER_STEP = 4


def _tri(lower):
    r = lax.broadcasted_iota(jnp.int32, (CHUNK, CHUNK), 0)
    c = lax.broadcasted_iota(jnp.int32, (CHUNK, CHUNK), 1)
    return jnp.where((c <= r) if lower else (c >= r), 1.0, 0.0).astype(F32)


def _gla_chunk(q_ref, k_ref, v_ref, la_ref, sl, hh):
    lk, lv = slice(hh * GLA_DK, (hh + 1) * GLA_DK), slice(hh * GLA_DV, (hh + 1) * GLA_DV)
    la = la_ref[sl, lk]
    cum = jnp.dot(_tri(True), la, preferred_element_type=F32, precision=lax.Precision.HIGHEST)
    tot = cum[CHUNK - 1:CHUNK, :]
    e = jnp.exp(tot - cum)
    k = k_ref[sl, lk].astype(F32)
    kdec = k * e
    v = v_ref[sl, lv]
    upd_t = lax.dot_general(v.astype(BF16), kdec.astype(BF16), (((0,), (0,)), ((), ())), preferred_element_type=F32)
    qs = (q_ref[sl, lk].astype(F32) * (GLA_DK ** -0.5)).astype(BF16)
    return e, k, kdec, v, upd_t, jnp.exp(tot), qs


def _gla_group(heads):
    return GLA_HEADS_PER_STEP if heads % GLA_HEADS_PER_STEP == 0 else 1


def _gla_specs(heads, hp, rows_map):
    groups = heads // hp
    return [pl.BlockSpec((GLA_ROWS, hp * GLA_DK), lambda h, b: (rows_map(b), h)),
            pl.BlockSpec((GLA_ROWS, hp * GLA_DK), lambda h, b: (rows_map(b), groups + h)),
            pl.BlockSpec((GLA_ROWS, hp * GLA_DV), lambda h, b: (rows_map(b), groups + h)),
            pl.BlockSpec((GLA_ROWS, hp * GLA_DK), lambda h, b: (rows_map(b), h))]


def _gla_fwd_call(hm, la, heads):
    S = hm.shape[0]
    assert S % GLA_ROWS == 0
    nb, cpb, hp = S // GLA_ROWS, GLA_ROWS // CHUNK, _gla_group(heads)

    def body(q_ref, k_ref, v_ref, la_ref, o_ref, sp_ref, st_ref):
        @pl.when(pl.program_id(1) == 0)
        def _():
            st_ref[...] = jnp.zeros(st_ref.shape, F32)

        for c in range(cpb):
            sl = slice(c * CHUNK, (c + 1) * CHUNK)
            for hh in range(hp):
                _, _, _, _, upd_t, decay, qs = _gla_chunk(q_ref, k_ref, v_ref, la_ref, sl, hh)
                state = st_ref[hh]
                sp_ref[hh, c] = state
                state = state * decay + upd_t
                st_ref[hh] = state
                o_ref[sl, hh * GLA_DV:(hh + 1) * GLA_DV] = lax.dot_general(
                    qs, state.astype(BF16), (((1,), (1,)), ((), ())), preferred_element_type=F32)

    return pl.pallas_call(
        body, name=_nm("gla_fwd"), grid=(heads // hp, nb),
        out_shape=(jax.ShapeDtypeStruct((S, heads * GLA_DV), F32),
                   jax.ShapeDtypeStruct((heads, S // CHUNK, GLA_DV, GLA_DK), F32)),
        in_specs=_gla_specs(heads, hp, lambda b: b),
        out_specs=(pl.BlockSpec((GLA_ROWS, hp * GLA_DV), lambda h, b: (b, h)),
                   pl.BlockSpec((hp, cpb, GLA_DV, GLA_DK), lambda h, b: (h, b, 0, 0))),
        scratch_shapes=[pltpu.VMEM((hp, GLA_DV, GLA_DK), F32)],
        compiler_params=_cp(("parallel", "arbitrary")),
    )(hm, hm, hm, la)


def _gla_bwd_call(hm, la, sprev, do, heads):
    S = hm.shape[0]
    nb, cpb, hp = S // GLA_ROWS, GLA_ROWS // CHUNK, _gla_group(heads)
    scale = GLA_DK ** -0.5

    def body(q_ref, k_ref, v_ref, la_ref, sp_ref, do_ref, dq_ref, dk_ref, dv_ref, dla_ref, carry_ref):
        @pl.when(pl.program_id(1) == 0)
        def _():
            carry_ref[...] = jnp.zeros(carry_ref.shape, F32)

        for c in reversed(range(cpb)):
            sl = slice(c * CHUNK, (c + 1) * CHUNK)
            for hh in range(hp):
                lk, lv = slice(hh * GLA_DK, (hh + 1) * GLA_DK), slice(hh * GLA_DV, (hh + 1) * GLA_DV)
                e, k, kdec, v, upd_t, decay, qs = _gla_chunk(q_ref, k_ref, v_ref, la_ref, sl, hh)
                sp = sp_ref[hh, c]
                s_n = sp * decay + upd_t
                dob = do_ref[sl, lv].astype(BF16)
                g = carry_ref[hh] + lax.dot_general(dob, qs, (((0,), (0,)), ((), ())), preferred_element_type=F32)
                gb = g.astype(BF16)
                dq_ref[sl, lk] = (jnp.dot(dob, s_n.astype(BF16), preferred_element_type=F32) * scale).astype(dq_ref.dtype)
                ddecay = jnp.sum(g * sp, axis=0, keepdims=True)
                dkdec = jnp.dot(v.astype(BF16), gb, preferred_element_type=F32)
                dv_ref[sl, lv] = lax.dot_general(kdec.astype(BF16), gb, (((1,), (1,)), ((), ())),
                                                 preferred_element_type=F32).astype(dv_ref.dtype)
                dk_ref[sl, lk] = (dkdec * e).astype(dk_ref.dtype)
                w = dkdec * k * e
                dtot = jnp.sum(w, axis=0, keepdims=True) + ddecay * decay
                last = lax.broadcasted_iota(jnp.int32, (CHUNK, 1), 0) == CHUNK - 1
                dcum = jnp.where(last, dtot - w, -w)
                dla_ref[sl, lk] = jnp.dot(_tri(False), dcum, preferred_element_type=F32, precision=lax.Precision.HIGHEST)
                carry_ref[hh] = g * decay

    rev = lambda b: nb - 1 - b
    narrow = pl.BlockSpec((GLA_ROWS, hp * GLA_DK), lambda h, b: (rev(b), h))
    wide = pl.BlockSpec((GLA_ROWS, hp * GLA_DV), lambda h, b: (rev(b), h))
    return pl.pallas_call(
        body, name=_nm("gla_bwd"), grid=(heads // hp, nb),
        out_shape=(jax.ShapeDtypeStruct((S, heads * GLA_DK), hm.dtype), jax.ShapeDtypeStruct((S, heads * GLA_DK), hm.dtype),
                   jax.ShapeDtypeStruct((S, heads * GLA_DV), hm.dtype), jax.ShapeDtypeStruct((S, heads * GLA_DK), F32)),
        in_specs=_gla_specs(heads, hp, rev) + [
            pl.BlockSpec((hp, cpb, GLA_DV, GLA_DK), lambda h, b: (h, rev(b), 0, 0)), wide],
        out_specs=(narrow, narrow, wide, narrow),
        scratch_shapes=[pltpu.VMEM((hp, GLA_DV, GLA_DK), F32)],
        compiler_params=_cp(("parallel", "arbitrary")),
    )(hm, hm, hm, la, sprev, do)


@functools.partial(jax.custom_vjp, nondiff_argnums=(3,))
def gla_core(hm, la, o_norm, heads):
    return _gla_core_fwd(hm, la, o_norm, heads)[0]


def _gla_core_fwd(hm, la, o_norm, heads):
    o, sprev = _gla_fwd_call(hm, la, heads)
    vd = heads * GLA_DV
    rows = [(o, vd, 0), (hm, vd, 2 * heads * GLA_DK // vd + 1)]
    (y,) = _rw_fwd(_gla_out_fn(heads), rows, [o_norm], [BF16], "gla_out_fwd")
    return y, (hm, la, o_norm, o, sprev)


def _gla_core_bwd(heads, res, dy):
    hm, la, o_norm, o, sprev = res
    vd = heads * GLA_DV
    rows = [(o, vd, 0), (hm, vd, 2 * heads * GLA_DK // vd + 1)]
    (do, dr), (dg,) = _rw_bwd(_gla_out_fn(heads), rows, [o_norm], [dy], [F32, hm.dtype], "gla_out_bwd")
    dq, dk, dv, dla = _gla_bwd_call(hm, la, sprev, do, heads)
    return jnp.concatenate([dq, dk, dv, dr], axis=1), dla, dg


gla_core.defvjp(_gla_core_fwd, _gla_core_bwd)


CONV_COLS = 256
HALO = 16


def _conv_rows(S):
    return min(512, S)


def _conv_taps(main_ref, halo_ref, i):
    prev = jnp.where(i > 0, halo_ref[...].astype(F32), 0.0)
    full = jnp.concatenate([prev, main_ref[...].astype(F32)], axis=0)
    return full[HALO:], pltpu.roll(full, 1, 0)[HALO:], pltpu.roll(full, 2, 0)[HALO:]


def _conv_apply(taps, w_ref, b_ref):
    x0, x1, x2 = taps
    return x2 * w_ref[0:1, :] + x1 * w_ref[1:2, :] + x0 * w_ref[2:3, :] + b_ref[...]


def _gelu_gate(uc, gc):
    return uc * jax.nn.gelu(gc)


def _conv_cols(dff, pref):
    return max(c for c in range(LANES, pref + 1, LANES) if dff % c == 0)


def _conv_in_specs(R, C, nj):
    hpr = R // HALO
    main = lambda off: pl.BlockSpec((R, C), lambda j, i: (i, j + off))
    halo = lambda off: pl.BlockSpec((HALO, C), lambda j, i: (jnp.maximum(i * hpr - 1, 0), j + off))
    par = lambda rows, off: pl.BlockSpec((rows, C), lambda j, i: (0, j + off))
    return [main(0), halo(0), main(nj), halo(nj), par(CONV_W, 0), par(CONV_W, nj), par(1, 0), par(1, nj)]


def _conv_fwd_call(h, cw, cb, comm=None):
    S, dff = h.shape[0], h.shape[1] // 2
    R, C = min(1024, S), _conv_cols(dff, 768)
    nj = dff // C

    def body(u_ref, uh_ref, g_ref, gh_ref, wu_ref, wg_ref, bu_ref, bg_ref, a_ref):
        i = pl.program_id(1)
        uc = _conv_apply(_conv_taps(u_ref, uh_ref, i), wu_ref, bu_ref)
        gc = _conv_apply(_conv_taps(g_ref, gh_ref, i), wg_ref, bg_ref)
        a_ref[...] = _gelu_gate(uc, gc).astype(a_ref.dtype)

    return _pcall(
        body, "conv_fwd", (nj, S // R), 0, in_specs=_conv_in_specs(R, C, nj),
        out_specs=[pl.BlockSpec((R, C), lambda j, i: (i, j))], out_shape=[jax.ShapeDtypeStruct((S, dff), BF16)],
        scratch=[], sem=("parallel", "parallel"), args=[h, h, h, h, cw, cw, cb, cb], comm=comm)


def _conv_bwd_gate_call(h, cw, cb, da, comm=None):
    S, dff = h.shape[0], h.shape[1] // 2
    R, C = _conv_rows(S), _conv_cols(dff, 768)
    nj = dff // C

    def body(u_ref, uh_ref, g_ref, gh_ref, wu_ref, wg_ref, bu_ref, bg_ref, da_ref,
             du_ref, dg_ref, dwu_ref, dwg_ref, dbu_ref, dbg_ref):
        i = pl.program_id(1)
        ut, gt = _conv_taps(u_ref, uh_ref, i), _conv_taps(g_ref, gh_ref, i)
        uc, gc = _conv_apply(ut, wu_ref, bu_ref), _conv_apply(gt, wg_ref, bg_ref)
        _, vjp_fn = jax.vjp(_gelu_gate, uc, gc)
        du, dg = vjp_fn(da_ref[...].astype(F32))
        du_ref[...] = du.astype(du_ref.dtype)
        dg_ref[...] = dg.astype(dg_ref.dtype)

        @pl.when(i == 0)
        def _():
            for r in (dwu_ref, dwg_ref, dbu_ref, dbg_ref):
                r[...] = jnp.zeros(r.shape, F32)

        for d, taps, dw_ref, db_ref in ((du, ut, dwu_ref, dbu_ref), (dg, gt, dwg_ref, dbg_ref)):
            x0, x1, x2 = taps
            dw_ref[0:1, :] += jnp.sum(d * x2, axis=0, keepdims=True)
            dw_ref[1:2, :] += jnp.sum(d * x1, axis=0, keepdims=True)
            dw_ref[2:3, :] += jnp.sum(d * x0, axis=0, keepdims=True)
            db_ref[...] += jnp.sum(d, axis=0, keepdims=True)

    tile = pl.BlockSpec((R, C), lambda j, i: (i, j))
    par = lambda rows: pl.BlockSpec((rows, C), lambda j, i: (0, j))
    return _pcall(
        body, "conv_bwd_gate", (nj, S // R), 0, in_specs=_conv_in_specs(R, C, nj) + [tile],
        out_specs=[tile, tile, par(CONV_W), par(CONV_W), par(1), par(1)],
        out_shape=[jax.ShapeDtypeStruct((S, dff), BF16), jax.ShapeDtypeStruct((S, dff), BF16),
                   jax.ShapeDtypeStruct((CONV_W, dff), F32), jax.ShapeDtypeStruct((CONV_W, dff), F32),
                   jax.ShapeDtypeStruct((1, dff), F32), jax.ShapeDtypeStruct((1, dff), F32)],
        scratch=[], sem=("parallel", "arbitrary"), args=[h, h, h, h, cw, cw, cb, cb, da], comm=comm)


def _conv_bwd_shift_call(dc, cw, into=None):
    S, dff = dc.shape
    R, C = min(1024, S), _conv_cols(dff, 1024)
    nj = dff // C
    col_off = 0 if into is None else nj
    hpr, last = R // HALO, S // HALO - 1
    ni = S // R

    def body(d_ref, nx_ref, w_ref, *rest):
        o_ref = rest[-1]
        i = pl.program_id(1)
        nxt = jnp.where(i < ni - 1, nx_ref[...].astype(F32), 0.0)
        full = jnp.concatenate([d_ref[...].astype(F32), nxt], axis=0)
        n = R + HALO
        y1, y2 = pltpu.roll(full, n - 1, 0)[:R], pltpu.roll(full, n - 2, 0)[:R]
        o_ref[...] = (full[:R] * w_ref[2:3, :] + y1 * w_ref[1:2, :] + y2 * w_ref[0:1, :]).astype(o_ref.dtype)

    in_specs = [pl.BlockSpec((R, C), lambda j, i: (i, j)),
                pl.BlockSpec((HALO, C), lambda j, i: (jnp.minimum((i + 1) * hpr, last), j)),
                pl.BlockSpec((CONV_W, C), lambda j, i: (0, j + col_off))]
    args = [dc, dc, cw]
    if into is not None:
        in_specs.append(pl.BlockSpec(memory_space=pl.ANY))
        args.append(into)
    return pl.pallas_call(
        body, name=_nm("conv_bwd_shift"), grid=(nj, ni), out_shape=jax.ShapeDtypeStruct((S, 2 * dff), BF16),
        in_specs=in_specs, out_specs=pl.BlockSpec((R, C), lambda j, i: (i, j + col_off)),
        input_output_aliases={} if into is None else {3: 0},
        compiler_params=_cp(("parallel", "parallel")),
    )(*args)


@jax.custom_vjp
def ffn_hidden(x1, x1_bf16, w3, cw, cb):
    return _ffn_hidden_fwd(x1, x1_bf16, w3, cw, cb)[0]


def _ffn_hidden_fwd(x1, x1_bf16, w3, cw, cb, comm=None):
    h = _mm(x1_bf16, w3, "nn", BF16, "up", _all_blocks(w3))
    a, *carried = _conv_fwd_call(h, cw, cb, comm)
    return (a, (x1_bf16, w3, cw, cb, h)) + ((carried,) if comm is not None else ())


def _ffn_hidden_bwd(res, da, comm=None):
    x1, w3, cw, cb, h = res
    du, dg, dwu, dwg, dbu, dbg, *carried = _conv_bwd_gate_call(h, cw, cb, da, comm)
    dh = _conv_bwd_shift_call(dg, cw, into=_conv_bwd_shift_call(du, cw))
    dx = _mm(dh, w3, "nt", F32, "up_dx", _all_blocks(w3))
    dw3 = _mm(x1, dh, "tn", w3.dtype, "up_dw", _all_blocks(w3))
    grads = (dx, jnp.zeros_like(x1), dw3, jnp.concatenate([dwu, dwg], axis=1), jnp.concatenate([dbu, dbg], axis=1))
    return (grads, carried) if comm is not None else grads


ffn_hidden.defvjp(_ffn_hidden_fwd, _ffn_hidden_bwd)


def _loss_call(y, target):
    S, D = y.shape
    tr = _row_tile(S, D)

    def body(y_ref, t_ref, sq_ref, dy_ref):
        diff = y_ref[...] - t_ref[...]
        dy_ref[...] = diff * (1.0 / D)
        part = jnp.sum(diff * diff, axis=0, keepdims=True)
        i = pl.program_id(0)

        @pl.when(i == 0)
        def _():
            sq_ref[...] = part

        @pl.when(i > 0)
        def _():
            sq_ref[...] += part

    row = pl.BlockSpec((tr, D), lambda i: (i, 0))
    return pl.pallas_call(
        body, name=_nm("loss"), grid=(S // tr,),
        out_shape=(jax.ShapeDtypeStruct((1, D), F32), jax.ShapeDtypeStruct((S, D), F32)),
        in_specs=[row, row], out_specs=(pl.BlockSpec((1, D), lambda i: (0, 0)), row),
        compiler_params=_cp(("arbitrary",)),
    )(y, target)


def _row_divisor(rows, cap=512):
    for cand in range(min(rows, cap), 15, -1):
        if rows % cand == 0 and cand % 16 == 0:
            return cand
    return rows


def _adamw_call(w, g, m, v):
    shape = w.shape
    w2, g2, m2, v2 = (a.reshape(-1, shape[-1]) for a in (w, g, m, v))
    rows, cols = w2.shape
    tr = _row_divisor(rows)

    def body(w_ref, g_ref, m_ref, v_ref, d_ref, nm_ref, nv_ref):
        g_ = g_ref[...]
        m_ = ADAM_B1 * m_ref[...] + (1.0 - ADAM_B1) * g_
        v_ = ADAM_B2 * v_ref[...] + (1.0 - ADAM_B2) * (g_ * g_)
        m_hat = m_ / (1.0 - ADAM_B1 ** ADAM_STEP)
        v_hat = v_ / (1.0 - ADAM_B2 ** ADAM_STEP)
        d_ref[...] = -ADAM_LR * (m_hat / (jnp.sqrt(v_hat) + ADAM_EPS) + ADAM_WD * w_ref[...])
        nm_ref[...] = m_
        nv_ref[...] = v_

    blk = pl.BlockSpec((tr, cols), lambda i: (i, 0))
    outs = pl.pallas_call(
        body, name=_nm("adamw"), grid=(rows // tr,),
        out_shape=tuple(jax.ShapeDtypeStruct((rows, cols), F32) for _ in range(3)),
        in_specs=[blk] * 4, out_specs=(blk,) * 3, compiler_params=_cp(("parallel",)),
    )(w2, g2, m2, v2)
    return tuple(o.reshape(shape) for o in outs)


ANY = pl.BlockSpec(memory_space=pl.ANY)


def _place():
    return lax.axis_index("x"), lax.axis_index("y"), lax.axis_index("c")


def all_gather(shards):
    n = len(shards)

    def body(*refs):
        x_refs, out_refs = refs[:n], refs[n:2 * n]
        send_sems, recv_sems, local_sems = refs[2 * n:]
        x, y, c = _place()
        me, sibling = (x, y, c), (x, y, 1 - c)
        chips = [(1 - x, y), (x, 1 - y), (1 - x, 1 - y)]

        def slot(a, px, py, pc):
            return out_refs[a].at[:, 4 * px + 2 * py + pc]

        def copy(a, k, block, to, own=False):
            return pltpu.make_async_remote_copy(
                src_ref=x_refs[a] if own else slot(a, *block), dst_ref=slot(a, *block),
                send_sem=send_sems.at[7 * a + k], recv_sem=recv_sems.at[7 * a + k], device_id=to, device_id_type=MESH)

        mine = [pltpu.make_async_copy(x_refs[a], slot(a, *me), local_sems.at[a]) for a in range(n)]
        first = []
        for a in range(n):
            mine[a].start()
            first.append(copy(a, 0, me, sibling, own=True))
            first += [copy(a, 1 + j, me, (*chip, c), own=True) for j, chip in enumerate(chips)]
        for cp in first:
            cp.start()
        passed = []
        for j, chip in enumerate(chips):
            for a in range(n):
                copy(a, 1 + j, (*chip, c), me).wait_recv()
                passed.append(copy(a, 4 + j, (*chip, c), sibling))
                passed[-1].start()
        for a in range(n):
            copy(a, 0, sibling, me).wait_recv()
        for j, chip in enumerate(chips):
            for a in range(n):
                copy(a, 4 + j, (*chip, 1 - c), me).wait_recv()
        for cp in first + passed:
            cp.wait_send()
        for cp in mine:
            cp.wait()

    return pl.pallas_call(
        body, name=_nm("all_gather"),
        out_shape=tuple(jax.ShapeDtypeStruct((s.shape[0], N_DEV) + s.shape[1:], s.dtype) for s in shards),
        in_specs=[ANY] * n, out_specs=(ANY,) * n,
        scratch_shapes=[pltpu.SemaphoreType.DMA((7 * n,)), pltpu.SemaphoreType.DMA((7 * n,)), pltpu.SemaphoreType.DMA((n,))],
    )(*shards)


def _rs_pair_exchange(gs):
    n = len(gs)

    def body(*refs):
        g_refs, recv_refs = refs[:n], refs[n:2 * n]
        send_sems, recv_sems = refs[2 * n:]
        x, y, c = _place()
        copies = [pltpu.make_async_remote_copy(
            src_ref=g_refs[a].at[:, 2 * j + (1 - c)], dst_ref=recv_refs[a].at[j], send_sem=send_sems.at[4 * a + j],
            recv_sem=recv_sems.at[4 * a + j], device_id=(x, y, 1 - c), device_id_type=MESH)
            for a in range(n) for j in range(4)]
        for cp in copies:
            cp.start()
        for cp in copies:
            cp.wait_recv()
        for cp in copies:
            cp.wait_send()

    return pl.pallas_call(
        body, name=_nm("rs_pair"),
        out_shape=tuple(jax.ShapeDtypeStruct((4, g.shape[0]) + g.shape[2:], g.dtype) for g in gs),
        in_specs=[ANY] * n, out_specs=(ANY,) * n,
        scratch_shapes=[pltpu.SemaphoreType.DMA((4 * n,)), pltpu.SemaphoreType.DMA((4 * n,))],
    )(*gs)


def _rs_chip_exchange(ps):
    n = len(ps)

    def body(*refs):
        p_refs, recv_refs = refs[:n], refs[n:2 * n]
        send_sems, recv_sems = refs[2 * n:]
        x, y, c = _place()
        chips = [(1 - x, y), (x, 1 - y), (1 - x, 1 - y)]
        copies = [pltpu.make_async_remote_copy(
            src_ref=p_refs[a].at[2 * cx + cy], dst_ref=recv_refs[a].at[k], send_sem=send_sems.at[3 * a + k],
            recv_sem=recv_sems.at[3 * a + k], device_id=(cx, cy, c), device_id_type=MESH)
            for a in range(n) for k, (cx, cy) in enumerate(chips)]
        for cp in copies:
            cp.start()
        for cp in copies:
            cp.wait_recv()
        for cp in copies:
            cp.wait_send()

    return pl.pallas_call(
        body, name=_nm("rs_chip"),
        out_shape=tuple(jax.ShapeDtypeStruct((3,) + p.shape[1:], p.dtype) for p in ps),
        in_specs=[ANY] * n, out_specs=(ANY,) * n,
        scratch_shapes=[pltpu.SemaphoreType.DMA((3 * n,)), pltpu.SemaphoreType.DMA((3 * n,))],
    )(*ps)


def _rs_pair_add(g, recv, c_idx):
    L, _, a, b = g.shape
    ta = _row_divisor(a, 1024)

    def body(c_ref, g_ref, r_ref, o_ref):
        o_ref[...] = (g_ref[...].astype(F32) + r_ref[...].astype(F32)).astype(o_ref.dtype)

    grid_spec = pltpu.PrefetchScalarGridSpec(
        num_scalar_prefetch=1, grid=(4, L, a // ta),
        in_specs=[pl.BlockSpec((None, None, ta, b), lambda j, l, i, c_ref: (l, 2 * j + c_ref[0], i, 0)),
                  pl.BlockSpec((None, None, ta, b), lambda j, l, i, c_ref: (j, l, i, 0))],
        out_specs=pl.BlockSpec((None, None, ta, b), lambda j, l, i, c_ref: (j, l, i, 0)))
    return pl.pallas_call(
        body, name=_nm("rs_pair_add"), grid_spec=grid_spec, out_shape=jax.ShapeDtypeStruct((4, L, a, b), g.dtype),
        compiler_params=_cp(("parallel", "parallel", "parallel")),
    )(c_idx, g, recv)


def _rs_final_add(p1, recv, chip_idx):
    _, L, a, b = p1.shape
    ta = _row_divisor(a, 1024)

    def body(chip_ref, p_ref, r_ref, o_ref):
        acc = p_ref[...].astype(F32)
        for k in range(3):
            acc = acc + r_ref[k].astype(F32)
        o_ref[...] = acc

    grid_spec = pltpu.PrefetchScalarGridSpec(
        num_scalar_prefetch=1, grid=(L, a // ta),
        in_specs=[pl.BlockSpec((None, None, ta, b), lambda l, i, chip_ref: (chip_ref[0], l, i, 0)),
                  pl.BlockSpec((3, None, ta, b), lambda l, i, chip_ref: (0, l, i, 0))],
        out_specs=pl.BlockSpec((None, ta, b), lambda l, i, chip_ref: (l, i, 0)))
    return pl.pallas_call(
        body, name=_nm("rs_final_add"), grid_spec=grid_spec, out_shape=jax.ShapeDtypeStruct((L, a, b), F32),
        compiler_params=_cp(("parallel", "parallel")),
    )(chip_idx, p1, recv)


def reduce_scatter(gs):
    x, y, c = _place()
    c_idx = jnp.reshape(c, (1,)).astype(jnp.int32)
    chip_idx = jnp.reshape(2 * x + y, (1,)).astype(jnp.int32)
    recv1 = _rs_pair_exchange(gs)
    p1 = [_rs_pair_add(g, r, c_idx) for g, r in zip(gs, recv1)]
    recv2 = _rs_chip_exchange(p1)
    return [_rs_final_add(p, r, chip_idx) for p, r in zip(p1, recv2)]


def all_reduce_small(v):
    r, C = v.shape

    def body(v_ref, out_ref, buf_ref, send_sems, recv_sems):
        x, y, c = _place()
        my_id = 4 * x + 2 * y + c
        buf_ref[my_id] = v_ref[...]
        copies = []
        for k in range(1, N_DEV):
            fx, fy, fc = (k >> 2) & 1, (k >> 1) & 1, k & 1
            peer = (x ^ fx, y ^ fy, c ^ fc)
            copies.append(pltpu.make_async_remote_copy(
                src_ref=v_ref, dst_ref=buf_ref.at[my_id], send_sem=send_sems.at[k - 1], recv_sem=recv_sems.at[k - 1],
                device_id=peer, device_id_type=MESH))
        for cp in copies:
            cp.start()
        for cp in copies:
            cp.wait_recv()
        for cp in copies:
            cp.wait_send()
        acc = buf_ref[0]
        for d in range(1, N_DEV):
            acc = acc + buf_ref[d]
        out_ref[...] = acc

    vm = pl.BlockSpec(memory_space=pltpu.VMEM)
    return pl.pallas_call(
        body, name=_nm("all_reduce_small"), out_shape=jax.ShapeDtypeStruct((r, C), F32),
        in_specs=[vm], out_specs=vm,
        scratch_shapes=[pltpu.VMEM((N_DEV, r, C), F32), pltpu.SemaphoreType.DMA((7,)), pltpu.SemaphoreType.DMA((7,))],
    )(v)


def _slot(ref, px, py, pc):
    return ref.at[:, 4 * px + 2 * py + pc]


def comm_gather_own(shards):
    n = len(shards)

    def build(cin, cout, send_sems, recv_sems, local_sems):
        x, y, c = _place()
        me = (x, y, c)
        direct = [(x, y, 1 - c), (1 - x, y, c), (x, 1 - y, c)]
        src_nb, dst_nb, diag = (x ^ (1 - c), y ^ c, c), (x ^ c, y ^ (1 - c), c), (1 - x, 1 - y, c)
        starts, mid_waits, mid_starts, waits = [], [], [], []
        for a in range(n):
            local = pltpu.make_async_copy(cin[a], _slot(cout[a], *me), local_sems.at[a])
            starts.append(local)
            waits.append(local.wait)
            for k, peer in enumerate(direct):
                send = pltpu.make_async_remote_copy(
                    src_ref=cin[a], dst_ref=_slot(cout[a], *me), send_sem=send_sems.at[4 * a + k],
                    recv_sem=recv_sems.at[4 * a + k], device_id=peer, device_id_type=MESH)
                arrive = pltpu.make_async_remote_copy(
                    src_ref=cin[a], dst_ref=_slot(cout[a], *peer), send_sem=send_sems.at[4 * a + k],
                    recv_sem=recv_sems.at[4 * a + k], device_id=peer, device_id_type=MESH)
                starts.append(send)
                (waits if k == 0 else mid_waits).append(arrive.wait_recv)
                waits.append(send.wait_send)
            onward = pltpu.make_async_remote_copy(
                src_ref=_slot(cout[a], *src_nb), dst_ref=_slot(cout[a], *src_nb), send_sem=send_sems.at[4 * a + 3],
                recv_sem=recv_sems.at[4 * a + 3], device_id=dst_nb, device_id_type=MESH)
            arrive = pltpu.make_async_remote_copy(
                src_ref=_slot(cout[a], *src_nb), dst_ref=_slot(cout[a], *diag), send_sem=send_sems.at[4 * a + 3],
                recv_sem=recv_sems.at[4 * a + 3], device_id=dst_nb, device_id_type=MESH)
            mid_starts.append(onward)
            waits += [arrive.wait_recv, onward.wait_send]
        return starts, mid_waits, mid_starts, waits

    out_shapes = [jax.ShapeDtypeStruct((s.shape[0], N_DEV) + s.shape[1:], s.dtype) for s in shards]
    return Comm(shards, out_shapes, {}, 4 * n, 4 * n, n, build)


def comm_gather_pass(partial):
    n = len(partial)

    def build(cin, cout, send_sems, recv_sems, local_sems):
        x, y, c = _place()
        chips = [(1 - x, y), (x, 1 - y), (1 - x, 1 - y)]
        starts, waits = [], []
        for a in range(n):
            for j, chip in enumerate(chips):
                send = pltpu.make_async_remote_copy(
                    src_ref=_slot(cout[a], *chip, c), dst_ref=_slot(cout[a], *chip, c), send_sem=send_sems.at[3 * a + j],
                    recv_sem=recv_sems.at[3 * a + j], device_id=(x, y, 1 - c), device_id_type=MESH)
                arrive = pltpu.make_async_remote_copy(
                    src_ref=_slot(cout[a], *chip, c), dst_ref=_slot(cout[a], *chip, 1 - c),
                    send_sem=send_sems.at[3 * a + j], recv_sem=recv_sems.at[3 * a + j],
                    device_id=(x, y, 1 - c), device_id_type=MESH)
                starts.append(send)
                waits += [arrive.wait_recv, send.wait_send]
        return starts, waits

    out_shapes = [jax.ShapeDtypeStruct(p.shape, p.dtype) for p in partial]
    return Comm(partial, out_shapes, {a: a for a in range(n)}, 3 * n, 3 * n, 1, build)


def comm_rs_pair(gs):
    n = len(gs)

    def build(cin, cout, send_sems, recv_sems, local_sems):
        x, y, c = _place()
        starts, waits = [], []
        for a in range(n):
            for j in range(4):
                cp = pltpu.make_async_remote_copy(
                    src_ref=cin[a].at[:, 2 * j + (1 - c)], dst_ref=cout[a].at[j], send_sem=send_sems.at[4 * a + j],
                    recv_sem=recv_sems.at[4 * a + j], device_id=(x, y, 1 - c), device_id_type=MESH)
                starts.append(cp)
                waits += [cp.wait_recv, cp.wait_send]
        return starts, waits

    out_shapes = [jax.ShapeDtypeStruct((4, g.shape[0]) + g.shape[2:], g.dtype) for g in gs]
    return Comm(gs, out_shapes, {}, 4 * n, 4 * n, 1, build)


def comm_rs_chip(ps):
    n = len(ps)

    def build(cin, cout, send_sems, recv_sems, local_sems):
        x, y, c = _place()
        chips = [(1 - x, y), (x, 1 - y), (1 - x, 1 - y)]
        starts, waits = [], []
        for a in range(n):
            for k, (cx, cy) in enumerate(chips):
                cp = pltpu.make_async_remote_copy(
                    src_ref=cin[a].at[2 * cx + cy], dst_ref=cout[a].at[k], send_sem=send_sems.at[3 * a + k],
                    recv_sem=recv_sems.at[3 * a + k], device_id=(cx, cy, c), device_id_type=MESH)
                starts.append(cp)
                waits += [cp.wait_recv, cp.wait_send]
        return starts, waits

    out_shapes = [jax.ShapeDtypeStruct((3,) + p.shape[1:], p.dtype) for p in ps]
    return Comm(ps, out_shapes, {}, 3 * n, 3 * n, 1, build)


def _pack(arrays, dtype, row_align):
    lead = arrays[0].shape[:-1]
    quantum = row_align * PACK_COLS
    parts, sizes = [], []
    for a in arrays:
        n = a.shape[-1]
        padded = _round_up(n, quantum)
        a = a.astype(dtype)
        if padded != n:
            a = jnp.pad(a, [(0, 0)] * len(lead) + [(0, padded - n)])
        parts.append(a.reshape(*lead, padded // PACK_COLS, PACK_COLS))
        sizes.append((n, padded // PACK_COLS))
    return jnp.concatenate(parts, axis=len(lead)), sizes


def _unpack(packed, sizes):
    lead = packed.shape[:-2]
    out, row = [], 0
    for n, rows in sizes:
        part = lax.slice_in_dim(packed, row, row + rows, axis=len(lead))
        out.append(part.reshape(*lead, rows * PACK_COLS)[..., :n])
        row += rows
    return out


def _unshard(gathered, axis):
    _, L, a, b = gathered.shape
    if axis == 1:
        return [gathered[:, l].reshape(N_DEV * a, b) for l in range(L)]
    return [jnp.transpose(gathered[:, l], (1, 0, 2)).reshape(a, N_DEV * b) for l in range(L)]


def _reshard(fulls, axis):
    blocks = []
    for f in fulls:
        A, B = f.shape
        if axis == 1:
            blocks.append(f.reshape(N_DEV, A // N_DEV, B))
        else:
            blocks.append(jnp.transpose(f.reshape(A, N_DEV, B // N_DEV), (1, 0, 2)))
    return jnp.stack(blocks, axis=1)


def _as3(a):
    return a if a.ndim == 3 else a[:, None, :]


def _prep_big(name, w, dims):
    w = w.astype(BF16)
    L, a, b = w.shape
    if name == 'mla_w_in':
        return jnp.pad(w, ((0, 0), (0, 0), (0, dims['h_width'] - b)))
    if name == 'mla_w_uq':
        hd = MLA_NOPE + MLA_ROPE
        w = jnp.pad(w.reshape(L, a, b // hd, hd), ((0, 0), (0, 0), (0, 0), (0, MLA_QK_PAD - hd)))
        return w.reshape(L, a, b // hd * MLA_QK_PAD)
    if name == 'ffn_w_up':
        return jnp.pad(w, ((0, 0), (0, 0), (0, _round_up(b, CONV_COLS) - b)))
    return w


def _unprep_big(name, g, shape):
    L, a, b = shape
    if name == 'mla_w_uq':
        hd = MLA_NOPE + MLA_ROPE
        return g.reshape(L, a, b // hd, MLA_QK_PAD)[..., :hd].reshape(L, a, b)
    return g[:, :, :b]


def _rope_tables(positions):
    inv = 1.0 / (ROPE_THETA ** (jnp.arange(0, MLA_ROPE, 2, dtype=F32) / MLA_ROPE))
    ang = positions.astype(F32)[:, None] * inv
    cos, sin = jnp.cos(ang), jnp.sin(ang)
    one, zero = jnp.ones_like(cos), jnp.zeros_like(cos)
    a = jnp.concatenate([cos, cos, one, one], axis=1)
    up = jnp.concatenate([sin, zero, zero, zero], axis=1)
    down = jnp.concatenate([zero, sin, zero, zero], axis=1)
    return (a, -up, down), (a, up, -down)


def _rows_full(w):
    return w.reshape(w.shape[0] * w.shape[1], w.shape[2])


def _ops(depth):
    alpha = (2 * depth) ** 0.25
    return {'ln_res': rw_op(_ln_res_fn(alpha), "ln_res", 2, [F32], shadow=True),
            'ple': rw_op(_ple_fn, "ple", 3, [F32], shadow=True), 'gla_gate': rw_op(_gla_gate_fn, "gla_gate", 1, [F32])}


MLA_PRE_W = ['mla_w_in', 'mla_q_norm', 'mla_kv_norm', 'mla_w_uq', 'mla_w_uk', 'mla_w_uv']
FFN_IN_W = ['mla_w_o', 'ln1_g', 'ln1_b', 'ffn_conv_w', 'ffn_conv_b']
FFN_OUT_W = ['ffn_w_down', 'ln2_g', 'ln2_b', 'ple_w_gate', 'ple_w_proj', 'ple_b_gate']


def _mla_heads(wl, j):
    return N_DEV * wl['mla_w_uk'][j].shape[2] // MLA_NOPE


def _mla_pre(x, xb, wl, j):
    w_uq, w_uk, w_uv = wl['mla_w_uq'][j], wl['mla_w_uk'][j], wl['mla_w_uv'][j]
    h = linear_sh(x, xb, _rows_full(wl['mla_w_in'][j]), F32)
    mla_norm = rw_op(_mla_norm_fn(w_uq.shape[1], w_uk.shape[1]), "mla_norm", 1, [BF16, BF16, F32])
    cq, ckv, kr_raw = mla_norm(h, wl['mla_q_norm'][j], wl['mla_kv_norm'][j])
    return linear(cq, w_uq, BF16), linear(ckv, w_uk, BF16), linear(ckv, w_uv, BF16), kr_raw


def _gla_mixer(x, xb, wl, j, ops):
    w_in3, w_a2 = wl['gla_w_in'][j], wl['gla_w_a2'][j]
    w_o = _rows_full(wl['gla_w_o'][j])
    w_in = jnp.transpose(w_in3, (1, 0, 2)).reshape(w_in3.shape[1], N_DEV * w_in3.shape[2])
    heads = w_o.shape[0] // GLA_DV
    n_main = 2 * heads * GLA_DK + 2 * heads * GLA_DV
    w_a = jnp.pad(w_in[:, n_main:], ((0, 0), (0, LANES - GLA_RANK)))
    w_a2_p = jnp.pad(w_a2, ((0, LANES - GLA_RANK), (0, 0))).astype(BF16)
    hm = linear_sh(x, xb, w_in[:, :n_main], BF16)
    ha = linear_sh(x, xb, w_a, BF16)
    (la,) = ops['gla_gate'](linear(ha, w_a2_p, F32), wl['gla_b_a'][j])
    return linear(gla_core(hm, la, wl['gla_o_norm'][j], heads), w_o, F32)


def _ffn_in(x, m, wl, i, ops, bp):
    x1, x1b = ops['ln_res'](x, m, wl['ln1_g'][i], wl['ln1_b'][i])
    cw, cb = wl['ffn_conv_w'][i], wl['ffn_conv_b'][i]
    bu = cw.shape[1] // N_DEV
    cwp = jnp.pad(cw.reshape(CONV_W, N_DEV, bu), ((0, 0), (0, 0), (0, bp - bu))).reshape(CONV_W, N_DEV * bp)
    cbp = jnp.pad(cb.reshape(1, N_DEV, bu), ((0, 0), (0, 0), (0, bp - bu))).reshape(1, N_DEV * bp)
    return x1, lax.stop_gradient(x1b), cwp, cbp


def _ffn_out(x1, a, wl, p_i, i, ops):
    w_down3 = wl['ffn_w_down'][i]
    half, bu, d_model = N_DEV // 2, 2 * w_down3.shape[1], w_down3.shape[2]
    bp = a.shape[1] // half
    w_down = jnp.pad(w_down3.reshape(half, bu, d_model), ((0, 0), (0, bp - bu), (0, 0))).reshape(half * bp, d_model)
    f = linear(a, w_down, F32)
    x2, x2b = ops['ln_res'](x1, f, wl['ln2_g'][i], wl['ln2_b'][i])
    glog = linear_sh(x2, lax.stop_gradient(x2b), _rows_full(wl['ple_w_gate'][i]), F32)
    pp = linear(p_i, wl['ple_w_proj'][i], F32)
    x, xb = ops['ple'](x2, glog, pp, wl['ple_b_gate'][i])
    return x, lax.stop_gradient(xb)


def _layer(x, xb, wl, p_i, aux, i, ops):
    j = i // 2
    if i % 2 == 0:
        q_raw, kn, v, kr_raw = _mla_pre(x, xb, wl, j)
        o = make_attention(aux, _mla_heads(wl, j))[0](q_raw, kn, v, kr_raw)
        m = linear(o, _rows_full(wl['mla_w_o'][j]), F32)
    else:
        m = _gla_mixer(x, xb, wl, j, ops)
    x1, x1b, cwp, cbp = _ffn_in(x, m, wl, i, ops, wl['ffn_w_up'][i].shape[2])
    return _ffn_out(x1, ffn_hidden(x1, x1b, wl['ffn_w_up'][i], cwp, cbp), wl, p_i, i, ops)


def kernel(x, p, positions, mla_w_in, mla_q_norm, mla_kv_norm, mla_w_uq, mla_w_uk, mla_w_uv, mla_w_o, gla_w_in, gla_w_a2, gla_b_a, gla_o_norm, gla_w_o, ln1_g, ln1_b, ln2_g, ln2_b, ffn_w_up, ffn_conv_w, ffn_conv_b, ffn_w_down, ple_w_proj, ple_w_gate, ple_b_gate, loss_target, m_mla_w_in, m_mla_q_norm, m_mla_kv_norm, m_mla_w_uq, m_mla_w_uk, m_mla_w_uv, m_mla_w_o, m_gla_w_in, m_gla_w_a2, m_gla_b_a, m_gla_o_norm, m_gla_w_o, m_ln1_g, m_ln1_b, m_ln2_g, m_ln2_b, m_ffn_w_up, m_ffn_conv_w, m_ffn_conv_b, m_ffn_w_down, m_ple_w_proj, m_ple_w_gate, m_ple_b_gate, v_mla_w_in, v_mla_q_norm, v_mla_kv_norm, v_mla_w_uq, v_mla_w_uk, v_mla_w_uv, v_mla_w_o, v_gla_w_in, v_gla_w_a2, v_gla_b_a, v_gla_o_norm, v_gla_w_o, v_ln1_g, v_ln1_b, v_ln2_g, v_ln2_b, v_ffn_w_up, v_ffn_conv_w, v_ffn_conv_b, v_ffn_w_down, v_ple_w_proj, v_ple_w_gate, v_ple_b_gate):
    w = dict(zip(WEIGHTS, (mla_w_in, mla_q_norm, mla_kv_norm, mla_w_uq, mla_w_uk, mla_w_uv, mla_w_o, gla_w_in, gla_w_a2,
                           gla_b_a, gla_o_norm, gla_w_o, ln1_g, ln1_b, ln2_g, ln2_b, ffn_w_up, ffn_conv_w, ffn_conv_b,
                           ffn_w_down, ple_w_proj, ple_w_gate, ple_b_gate)))
    m_in = dict(zip(WEIGHTS, (m_mla_w_in, m_mla_q_norm, m_mla_kv_norm, m_mla_w_uq, m_mla_w_uk, m_mla_w_uv, m_mla_w_o,
                              m_gla_w_in, m_gla_w_a2, m_gla_b_a, m_gla_o_norm, m_gla_w_o, m_ln1_g, m_ln1_b, m_ln2_g,
                              m_ln2_b, m_ffn_w_up, m_ffn_conv_w, m_ffn_conv_b, m_ffn_w_down, m_ple_w_proj, m_ple_w_gate,
                              m_ple_b_gate)))
    v_in = dict(zip(WEIGHTS, (v_mla_w_in, v_mla_q_norm, v_mla_kv_norm, v_mla_w_uq, v_mla_w_uk, v_mla_w_uv, v_mla_w_o,
                              v_gla_w_in, v_gla_w_a2, v_gla_b_a, v_gla_o_norm, v_gla_w_o, v_ln1_g, v_ln1_b, v_ln2_g,
                              v_ln2_b, v_ffn_w_up, v_ffn_conv_w, v_ffn_conv_b, v_ffn_w_down, v_ple_w_proj, v_ple_w_gate,
                              v_ple_b_gate)))
    _uid[0] = itertools.count()
    x2d, target, pos = x[0], loss_target[0], positions[0]
    p3 = p[:, 0]
    dims = {'h_width': mla_w_uq.shape[1] + mla_w_uk.shape[1] + LANES}

    depth = ln1_g.shape[0]
    ops = _ops(depth)
    cid = pos // CHUNK
    aux = {'rope': _rope_tables(pos), 'cidq': cid[:, None], 'cidk': cid[None, :],
           'need': _mask_table(cid, _attn_tile(pos.shape[0]))}

    in_layer0 = set(MLA_PRE_W + FFN_IN_W + FFN_OUT_W + ['ffn_w_up'])
    prepped = {n: _prep_big(n, w[n], dims) for n in BIG}
    first_names = [n for n in BIG if n in MLA_PRE_W + ['mla_w_o']]
    mid_names = [n for n in BIG if n in in_layer0 and n not in first_names]
    rest_names = [n for n in BIG if prepped[n].shape[0] > (1 if n in in_layer0 else 0)]
    rest_from = {n: (1 if n in in_layer0 else 0) for n in rest_names}
    small3 = [_as3(w[n]) for n in SMALL]
    small_packed, small_sizes = _pack([s.reshape(1, -1) for s in small3], F32, 8)
    first = all_gather([prepped[n][:1] for n in first_names] + [small_packed])
    wl = {n: [None] * prepped[n].shape[0] for n in BIG}
    for n, g in zip(first_names, first):
        wl[n][0] = g[0]
    for n, s3, flat in zip(SMALL, small3, _unpack(first[-1][0], small_sizes)):
        wl[n] = _unshard(flat.reshape(N_DEV, *s3.shape), SHARD_AXIS[n] if w[n].ndim == 3 else 2)
    for n in REPL:
        wl[n] = [w[n][l][None, :] for l in range(w[n].shape[0])]

    def pick(names, layer0):
        return {n: [wl[n][l] if (l == 0 and n in in_layer0) == layer0 else None for l in range(len(wl[n]))] for n in names}

    heads0 = _mla_heads(wl, 0)
    _, attn_run, attn_bwd = make_attention(aux, heads0)
    x2db = x2d.astype(BF16)
    pre, vjp_pre = jax.vjp(lambda x_, wl_: _mla_pre(x_, x2db, wl_, 0), x2d, pick(MLA_PRE_W, True))
    o, attn_res, partial = attn_run(*pre, comm_gather_own([prepped[n][:1] for n in mid_names]
                                                          + [prepped[n][rest_from[n]:] for n in rest_names]))
    for n, g in zip(mid_names, _run_comm(comm_gather_pass(partial[:len(mid_names)]), "gather_pass")):
        wl[n][0] = g[0]
    partial = partial[len(mid_names):]
    w_up0 = wl['ffn_w_up'][0]

    def ffn_in(x_, o_, wl_):
        x1_, x1b_, cwp_, cbp_ = _ffn_in(x_, linear(o_, _rows_full(wl_['mla_w_o'][0]), F32), wl_, 0, ops, w_up0.shape[2])
        return (x1_, cwp_, cbp_), x1b_

    (x1, cwp, cbp), vjp_in, x1b = jax.vjp(ffn_in, x2d, o, pick(FFN_IN_W, True), has_aux=True)
    a, ffn_res, rest = _ffn_hidden_fwd(x1, x1b, w_up0, cwp, cbp, comm_gather_pass(partial))
    for n, g in zip(rest_names, rest):
        for l in range(g.shape[0]):
            wl[n][rest_from[n] + l] = g[l]
    x_l0, vjp_out, x_l0b = jax.vjp(lambda x1_, a_, wl_: _ffn_out(x1_, a_, wl_, p3[0], 0, ops), x1, a,
                                   pick(FFN_OUT_W, True), has_aux=True)

    def tail(x_, wl_):
        xb_ = x_l0b
        for i in range(1, depth):
            x_, xb_ = _layer(x_, xb_, wl_, p3[i], aux, i, ops)
        return x_

    y, vjp_tail = jax.vjp(tail, x_l0, pick(WEIGHTS, False))
    sq, dy = _loss_call(y, target)

    dwl = {n: [None] * len(wl[n]) for n in WEIGHTS}

    def keep(part):
        for n, per_layer in part.items():
            for l, g in enumerate(per_layer):
                if g is not None:
                    dwl[n][l] = g

    x_c, y_c, c_place = _place()
    c_idx = jnp.reshape(c_place, (1,)).astype(jnp.int32)
    chip_idx = jnp.reshape(2 * x_c + y_c, (1,)).astype(jnp.int32)
    dx_l0, d_tail = vjp_tail(dy)
    keep(d_tail)
    g_rest = [jnp.stack(dwl[n][rest_from[n]:], axis=0) for n in rest_names]
    dx1_out, da, d_out = vjp_out(dx_l0)
    keep(d_out)
    (dx1_ffn, _, dw_up0, dcwp, dcbp), recv1 = _ffn_hidden_bwd(ffn_res, da, comm_rs_pair(g_rest))
    dwl['ffn_w_up'][0] = dw_up0
    g_mid = [dwl[n][0][None] for n in mid_names]
    g_rest, recv1 = g_rest + g_mid, list(recv1) + list(_rs_pair_exchange(g_mid))
    p1 = [_rs_pair_add(g, r, c_idx) for g, r in zip(g_rest, recv1)]
    dx_in, do, d_in = vjp_in((dx1_out + dx1_ffn, dcwp, dcbp))
    keep(d_in)
    d_pre_in, recv2 = attn_bwd(attn_res, do, comm_rs_chip(p1))
    red_rest = [_rs_final_add(p_, r, chip_idx) for p_, r in zip(p1, recv2)]
    dx_pre, d_pre = vjp_pre(d_pre_in)
    keep(d_pre)
    dx = dx_pre + dx_in

    small_blocks = [_reshard(dwl[n], SHARD_AXIS[n] if w[n].ndim == 3 else 2).reshape(N_DEV, -1) for n in SMALL]
    small_grad_packed, _ = _pack(small_blocks, F32, 8)
    red_first = reduce_scatter([dwl[n][0][None] for n in first_names] + [small_grad_packed[None]])
    by_layer = {n: [] for n in BIG}
    for n, g in zip(first_names, red_first):
        by_layer[n].append(g)
    for n, g in zip(mid_names, red_rest[len(rest_names):]):
        by_layer[n].append(g)
    for n, g in zip(rest_names, red_rest):
        by_layer[n].append(g)
    grads = {n: _unprep_big(n, jnp.concatenate(by_layer[n], axis=0), w[n].shape) for n in BIG}
    for n, f in zip(SMALL, _unpack(red_first[-1][0], small_sizes)):
        grads[n] = f.reshape(w[n].shape)

    repl_flat = [jnp.concatenate([g.reshape(-1) for g in dwl[n]]).reshape(1, -1) for n in REPL]
    loss_part = 0.5 * jnp.sum(sq) / sq.shape[1]
    packed, repl_sizes = _pack(repl_flat + [loss_part.reshape(1, 1)], F32, 8)
    summed = _unpack(all_reduce_small(packed[0])[None], repl_sizes)
    for n, f in zip(REPL, summed[:-1]):
        grads[n] = f.reshape(w[n].shape)
    loss = summed[-1].reshape(())

    delta, new_m, new_v = {}, {}, {}
    for n in WEIGHTS:
        delta[n], new_m[n], new_v[n] = _adamw_call(w[n], grads[n], m_in[n], v_in[n])
    return (loss, dx[None], *[grads[n] for n in WEIGHTS], *[delta[n] for n in WEIGHTS],
            *[new_m[n] for n in WEIGHTS], *[new_v[n] for n in WEIGHTS])
```

```python
import functools
import itertools

import jax
import jax.numpy as jnp
from jax import lax
from jax.experimental import pallas as pl
from jax.experimental.pallas import tpu as pltpu

F32 = jnp.float32
BF16 = jnp.bfloat16
MESH = pl.DeviceIdType.MESH
N_DEV = 8

EPS = 1e-5
NEG_INF = -1e30
CHUNK = 64
Q_BLOCK = 128
MLA_NOPE = 128
MLA_ROPE = 64
MLA_V = 128
MLA_QK_PAD = 256
ROPE_THETA = 10000.0
GLA_DK = 128
GLA_DV = 256
GLA_RANK = 16
GLA_TAU = 16.0
CONV_W = 3
ADAM_LR = 0.001
ADAM_B1 = 0.9
ADAM_B2 = 0.999
ADAM_EPS = 1e-08
ADAM_WD = 0.01
ADAM_STEP = 10
LOG2E = 1.4426950408889634

LANES = 128
PACK_COLS = 1024
VMEM_LIMIT = 56 * 1024 * 1024
MM_VMEM_BUDGET = 38 * 1024 * 1024

WEIGHTS = ['mla_w_in', 'mla_q_norm', 'mla_kv_norm', 'mla_w_uq', 'mla_w_uk', 'mla_w_uv', 'mla_w_o', 'gla_w_in',
           'gla_w_a2', 'gla_b_a', 'gla_o_norm', 'gla_w_o', 'ln1_g', 'ln1_b', 'ln2_g', 'ln2_b', 'ffn_w_up',
           'ffn_conv_w', 'ffn_conv_b', 'ffn_w_down', 'ple_w_proj', 'ple_w_gate', 'ple_b_gate']
SHARD_AXIS = {'mla_w_in': 1, 'mla_q_norm': None, 'mla_kv_norm': None, 'mla_w_uq': 2, 'mla_w_uk': 2, 'mla_w_uv': 2,
              'mla_w_o': 1, 'gla_w_in': 2, 'gla_w_a2': 2, 'gla_b_a': 1, 'gla_o_norm': 1, 'gla_w_o': 1,
              'ln1_g': None, 'ln1_b': None, 'ln2_g': None, 'ln2_b': None, 'ffn_w_up': 2, 'ffn_conv_w': 2,
              'ffn_conv_b': None, 'ffn_w_down': 1, 'ple_w_proj': 2, 'ple_w_gate': 1, 'ple_b_gate': None}
BIG = ['mla_w_in', 'mla_w_uq', 'mla_w_uk', 'mla_w_uv', 'mla_w_o', 'gla_w_in', 'gla_w_o', 'ffn_w_up', 'ffn_w_down',
       'ple_w_proj', 'ple_w_gate']
SMALL = ['gla_w_a2', 'gla_b_a', 'gla_o_norm', 'ffn_conv_w']
REPL = [n for n in WEIGHTS if SHARD_AXIS[n] is None]

_uid = [itertools.count()]


def _nm(base):
    return f"{base}_{next(_uid[0])}"


def _cp(sem=None):
    return pltpu.CompilerParams(dimension_semantics=sem, vmem_limit_bytes=VMEM_LIMIT)


def _round_up(n, m):
    return -(-n // m) * m


class Comm:
    MID_AT = 0.6

    def __init__(self, inputs, out_shapes, aliases, n_send, n_recv, n_local, build):
        self.inputs, self.out_shapes, self.aliases = list(inputs), list(out_shapes), dict(aliases)
        self.n_send, self.n_recv, self.n_local, self.build = n_send, n_recv, n_local, build


def _pcall(body, name, grid, n_prefetch, in_specs, out_specs, out_shape, scratch, sem, args, comm=None):
    in_specs, out_specs, out_shape, scratch, args = list(in_specs), list(out_specs), list(out_shape), list(scratch), list(args)
    n_in, n_out, n_scr = len(in_specs), len(out_specs), len(scratch)
    aliases = {}
    kernel_body = body
    if comm is not None:
        ci, co = len(comm.inputs), len(comm.out_shapes)
        any_spec = pl.BlockSpec(memory_space=pl.ANY)
        in_specs += [any_spec] * ci
        out_specs += [any_spec] * co
        out_shape += comm.out_shapes
        scratch += [pltpu.SemaphoreType.DMA((comm.n_send,)), pltpu.SemaphoreType.DMA((comm.n_recv,)),
                    pltpu.SemaphoreType.DMA((comm.n_local,))]
        aliases = {n_prefetch + n_in + k: n_out + v for k, v in comm.aliases.items()}
        args += comm.inputs
        sem = ("arbitrary",) * len(grid)

        def kernel_body(*refs):
            pre, r = refs[:n_prefetch], refs[n_prefetch:]
            ins, cin = r[:n_in], r[n_in:n_in + ci]
            outs, cout = r[n_in + ci:n_in + ci + n_out], r[n_in + ci + n_out:n_in + ci + n_out + co]
            scr = r[n_in + ci + n_out + co:n_in + ci + n_out + co + n_scr]
            send_sems, recv_sems, local_sems = r[-3:]
            first = functools.reduce(lambda a, b: a & b, [pl.program_id(d) == 0 for d in range(len(grid))])
            last = functools.reduce(lambda a, b: a & b, [pl.program_id(d) == grid[d] - 1 for d in range(len(grid))])
            built = comm.build(cin, cout, send_sems, recv_sems, local_sems)
            starts, waits = built[0], built[-1]
            mid_waits, mid_starts = (built[1], built[2]) if len(built) == 4 else ([], [])

            @pl.when(first)
            def _():
                for cp in starts:
                    cp.start()

            if mid_starts:
                step = pl.program_id(0)
                for d in range(1, len(grid)):
                    step = step * grid[d] + pl.program_id(d)
                n_steps = functools.reduce(lambda a, b: a * b, grid)

                @pl.when(step == int(n_steps * comm.MID_AT))
                def _():
                    for wait in mid_waits:
                        wait()
                    for cp in mid_starts:
                        cp.start()

            body(*pre, *ins, *outs, *scr)

            @pl.when(last)
            def _():
                for wait in waits:
                    wait()

    grid_spec = pltpu.PrefetchScalarGridSpec(num_scalar_prefetch=n_prefetch, grid=grid, in_specs=in_specs,
                                             out_specs=out_specs, scratch_shapes=scratch)
    return pl.pallas_call(kernel_body, name=_nm(name), grid_spec=grid_spec, out_shape=tuple(out_shape),
                          input_output_aliases=aliases, compiler_params=_cp(sem))(*args)


def _run_comm(comm, name):
    return _pcall(lambda: None, name, (1,), 0, [], [], [], [], ("arbitrary",), [], comm)


def _divisor_tiles(n, cap):
    if n % LANES:
        return [n]
    out = [t for t in range(LANES, min(n, cap) + 1, LANES) if n % t == 0]
    return out or [n]


def _mm_tiles(M, N, K, abytes, bbytes, obytes, tn_fixed=None, tk_fixed=None):
    best = None
    for tm in _divisor_tiles(M, 2048):
        for tn in ([tn_fixed] if tn_fixed else _divisor_tiles(N, 1536)):
            for tk in ([tk_fixed] if tk_fixed else _divisor_tiles(K, 4096)):
                vmem = 2 * (tm * tk * abytes + tk * tn * bbytes + tm * tn * obytes) + tm * tn * 4
                if vmem > MM_VMEM_BUDGET:
                    continue
                key = (tm * tn * tk, tk)
                if best is None or key > best[0]:
                    best = (key, (tm, tn, tk))
    assert best is not None, (M, N, K)
    return best[1]


def _mm(a, b, mode, out_dtype, base="mm", blocks=None):
    blk0, nblk = blocks if blocks else (0, 1)
    tn_fixed = tk_fixed = None
    if mode == "nn":
        M, K = a.shape
        N = nblk * b.shape[2] if blocks else b.shape[1]
        tn_fixed = b.shape[2] if blocks else None
    elif mode == "nt":
        M, K = a.shape
        N = b.shape[1] if blocks else b.shape[0]
        kb = 2 if blocks and nblk % 2 == 0 and blk0 % 2 == 0 else 1
        tk_fixed = kb * b.shape[2] if blocks else None
        assert not blocks or K == nblk * b.shape[2]
    else:
        (K, M), N = a.shape, b.shape[1]
        tn_fixed = N // nblk if blocks else None
    tm, tn, tk = _mm_tiles(M, N, K, a.dtype.itemsize, b.dtype.itemsize, jnp.dtype(out_dtype).itemsize, tn_fixed, tk_fixed)
    nk = K // tk
    out_shape = jax.ShapeDtypeStruct((M, N), out_dtype)
    out_spec = pl.BlockSpec((tm, tn), lambda i, j, k: (i, j))
    if mode == "nn":
        a_spec = pl.BlockSpec((tm, tk), lambda i, j, k: (i, k))
        b_spec = (pl.BlockSpec((None, tk, tn), lambda i, j, k: (blk0 + j, k, 0)) if blocks
                  else pl.BlockSpec((tk, tn), lambda i, j, k: (k, j)))
        dims = (((1,), (0,)), ((), ()))
    elif mode == "nt":
        a_spec = pl.BlockSpec((tm, tk), lambda i, j, k: (i, k))
        b_spec = (pl.BlockSpec((kb, tn, tk // kb), lambda i, j, k: (blk0 // kb + k, j, 0)) if blocks
                  else pl.BlockSpec((tn, tk), lambda i, j, k: (j, k)))
        dims = (((1,), (1,)), ((), ()))
    else:
        a_spec = pl.BlockSpec((tk, tm), lambda i, j, k: (k, i))
        b_spec = pl.BlockSpec((tk, tn), lambda i, j, k: (k, j))
        dims = (((0,), (0,)), ((), ()))
        if blocks:
            out_shape = jax.ShapeDtypeStruct((nblk, M, tn), out_dtype)
            out_spec = pl.BlockSpec((None, tm, tn), lambda i, j, k: (j, i, 0))

    def body(a_ref, b_ref, o_ref, acc_ref):
        if mode == "nt" and blocks:
            bn = tk // kb
            part = sum(lax.dot_general(a_ref[:, q * bn:(q + 1) * bn].astype(BF16), b_ref[q].astype(BF16), dims,
                                       preferred_element_type=F32) for q in range(kb))
        else:
            part = lax.dot_general(a_ref[...].astype(BF16), b_ref[...].astype(BF16), dims, preferred_element_type=F32)
        if nk == 1:
            o_ref[...] = part.astype(o_ref.dtype)
        else:
            k = pl.program_id(2)

            @pl.when(k == 0)
            def _():
                acc_ref[...] = part

            @pl.when(k > 0)
            def _():
                acc_ref[...] += part

            @pl.when(k == nk - 1)
            def _():
                o_ref[...] = acc_ref[...].astype(o_ref.dtype)

    return pl.pallas_call(
        body, name=_nm(base), grid=(M // tm, N // tn, nk), out_shape=out_shape,
        in_specs=[a_spec, b_spec], out_specs=out_spec,
        scratch_shapes=[pltpu.VMEM((tm, tn) if nk > 1 else (8, LANES), F32)],
        compiler_params=_cp(("parallel", "parallel", "arbitrary")),
    )(a, b)


def _all_blocks(w):
    return (0, w.shape[0]) if w.ndim == 3 else None


@functools.partial(jax.custom_vjp, nondiff_argnums=(2,))
def linear(a, w, out_dtype):
    return _mm(a, w, "nn", out_dtype, "lin_fwd", _all_blocks(w))


def _linear_fwd(a, w, out_dtype):
    return _mm(a, w, "nn", out_dtype, "lin_fwd", _all_blocks(w)), (a, w)


def _linear_bwd(out_dtype, res, dy):
    a, w = res
    return (_mm(dy, w, "nt", a.dtype, "lin_dx", _all_blocks(w)), _mm(a, dy, "tn", w.dtype, "lin_dw", _all_blocks(w)))


linear.defvjp(_linear_fwd, _linear_bwd)


@functools.partial(jax.custom_vjp, nondiff_argnums=(3,))
def linear_sh(a, a_bf16, w, out_dtype):
    return _mm(a_bf16, w, "nn", out_dtype, "lin_fwd", _all_blocks(w))


def _linear_sh_fwd(a, a_bf16, w, out_dtype):
    return _mm(a_bf16, w, "nn", out_dtype, "lin_fwd", _all_blocks(w)), (a_bf16, w, jnp.zeros((), a.dtype))


def _linear_sh_bwd(out_dtype, res, dy):
    a_bf16, w, tok = res
    return (_mm(dy, w, "nt", tok.dtype, "lin_dx", _all_blocks(w)), jnp.zeros_like(a_bf16),
            _mm(a_bf16, dy, "tn", w.dtype, "lin_dw", _all_blocks(w)))


linear_sh.defvjp(_linear_sh_fwd, _linear_sh_bwd)


def _row_tile(S, width, forward=False):
    tr = 512 if width <= 1024 else 256
    return min(2 * tr if forward else tr, S)


def _rw_fwd(f, rows, params, out_dtypes, base):
    S = rows[0][0].shape[0]
    tr = _row_tile(S, max(w for _, w, _ in rows), forward=True)
    n_in = len(rows) + len(params)
    avals = [jax.ShapeDtypeStruct((tr, w), F32) for _, w, _ in rows] + [jax.ShapeDtypeStruct(p.shape, F32) for p in params]
    outs = jax.eval_shape(f, *avals)

    def body(*refs):
        vals = [r[...].astype(F32) for r in refs[:n_in]]
        for o_ref, r in zip(refs[n_in:], f(*vals)):
            o_ref[...] = r.astype(o_ref.dtype)

    in_specs = [pl.BlockSpec((tr, w), functools.partial(lambda i, cb: (i, cb), cb=cb)) for _, w, cb in rows]
    in_specs += [pl.BlockSpec(p.shape, lambda i: (0, 0)) for p in params]
    return pl.pallas_call(
        body, name=_nm(base), grid=(S // tr,),
        out_shape=tuple(jax.ShapeDtypeStruct((S, o.shape[1]), dt) for o, dt in zip(outs, out_dtypes)),
        in_specs=in_specs, out_specs=tuple(pl.BlockSpec((tr, o.shape[1]), lambda i: (i, 0)) for o in outs),
        compiler_params=_cp(("parallel",)),
    )(*[a for a, _, _ in rows], *params)


def _rw_bwd(f, rows, params, cts, row_grad_dtypes, base):
    S = rows[0][0].shape[0]
    tr = _row_tile(S, max(w for _, w, _ in rows))
    n_rows, n_par, n_ct = len(rows), len(params), len(cts)
    want = [k for k, dt in enumerate(row_grad_dtypes) if dt is not None]

    def body(*refs):
        in_refs = refs[:n_rows + n_par]
        ct_refs = refs[n_rows + n_par:n_rows + n_par + n_ct]
        out_refs = refs[n_rows + n_par + n_ct:]
        vals = [r[...].astype(F32) for r in in_refs]
        _, vjp_fn = jax.vjp(f, *vals)
        grads = vjp_fn(tuple(c[...].astype(F32) for c in ct_refs))
        for o_ref, k in zip(out_refs[:len(want)], want):
            o_ref[...] = grads[k].astype(o_ref.dtype)
        i = pl.program_id(0)
        for o_ref, g in zip(out_refs[len(want):], grads[n_rows:]):
            @pl.when(i == 0)
            def _(o_ref=o_ref, g=g):
                o_ref[...] = g

            @pl.when(i > 0)
            def _(o_ref=o_ref, g=g):
                o_ref[...] += g

    in_specs = [pl.BlockSpec((tr, w), functools.partial(lambda i, cb: (i, cb), cb=cb)) for _, w, cb in rows]
    in_specs += [pl.BlockSpec(p.shape, lambda i: (0, 0)) for p in params]
    in_specs += [pl.BlockSpec((tr, c.shape[1]), lambda i: (i, 0)) for c in cts]
    out_shape = [jax.ShapeDtypeStruct((S, rows[k][1]), row_grad_dtypes[k]) for k in want]
    out_specs = [pl.BlockSpec((tr, rows[k][1]), lambda i: (i, 0)) for k in want]
    out_shape += [jax.ShapeDtypeStruct(p.shape, F32) for p in params]
    out_specs += [pl.BlockSpec(p.shape, lambda i: (0, 0)) for p in params]
    res = pl.pallas_call(
        body, name=_nm(base), grid=(S // tr,), out_shape=tuple(out_shape),
        in_specs=in_specs, out_specs=tuple(out_specs), compiler_params=_cp(("arbitrary",)),
    )(*[a for a, _, _ in rows], *params, *cts)
    row_grads = [None] * n_rows
    for k, g in zip(want, res[:len(want)]):
        row_grads[k] = g
    return row_grads, list(res[len(want):])


def rw_op(f, base, n_rows, out_dtypes, shadow=False):
    f_fwd = (lambda *a: (lambda r: tuple(r) + (r[0],))(f(*a))) if shadow else f
    fwd_dtypes = list(out_dtypes) + ([BF16] if shadow else [])

    @jax.custom_vjp
    def op(*args):
        return fwd(*args)[0]

    def split(args):
        rows = [(a, a.shape[1], 0) for a in args[:n_rows]]
        return rows, list(args[n_rows:])

    def fwd(*args):
        rows, params = split(args)
        return tuple(_rw_fwd(f_fwd, rows, params, fwd_dtypes, base + "_fwd")), args

    def bwd(args, cts):
        rows, params = split(args)
        cts = list(cts)[:len(out_dtypes)]
        rg, pg = _rw_bwd(f, rows, params, cts, [a.dtype for a, _, _ in rows], base + "_bwd")
        return tuple(rg) + tuple(g.astype(p.dtype) for g, p in zip(pg, params))

    op.defvjp(fwd, bwd)
    return op


def _ln_res_fn(alpha):
    def f(x, m, g, b):
        z = alpha * x + m
        mu = jnp.mean(z, -1, keepdims=True)
        zc = z - mu
        var = jnp.mean(zc * zc, -1, keepdims=True)
        return (zc * lax.rsqrt(var + EPS) * g + b,)
    return f


def _rms(x, g):
    return x * lax.rsqrt(jnp.mean(x * x, -1, keepdims=True) + EPS) * g


def _mla_norm_fn(q_lora, kv_lora):
    def f(h, qn, kvn):
        return (_rms(h[:, :q_lora], qn), _rms(h[:, q_lora:q_lora + kv_lora], kvn),
                h[:, q_lora + kv_lora:q_lora + kv_lora + LANES])
    return f


def _log_sigmoid(z):
    return jnp.minimum(z, 0.0) - jnp.log(1.0 + jnp.exp(-jnp.abs(z)))


def _gla_gate_fn(z, b):
    return (_log_sigmoid(z + b) / GLA_TAU,)


def _ple_fn(x, glog, pp, b):
    return (x + jax.nn.sigmoid(glog + b) * pp,)


def _gla_out_fn(heads):
    def f(o, r, g):
        parts = []
        for h in range(heads):
            oh = o[:, h * GLA_DV:(h + 1) * GLA_DV]
            mu = jnp.mean(oh, -1, keepdims=True)
            oc = oh - mu
            var = jnp.mean(oc * oc, -1, keepdims=True)
            parts.append(oc * lax.rsqrt(var + EPS) * g[:, h * GLA_DV:(h + 1) * GLA_DV])
        return (jnp.concatenate(parts, axis=1) * (r * jax.nn.sigmoid(r)),)
    return f


def _rope_call(x, tabs, roped, out_dtype, base, fold=False, scale=None):
    S, C = x.shape
    nb = C // LANES
    tr = min(512 if C <= 1024 else 256, S)
    out_c = LANES if fold else C

    def rot(v, a, b1, b2):
        return v * a + pltpu.roll(v, 96, 1) * b1 + pltpu.roll(v, 32, 1) * b2

    def body(x_ref, a_ref, b1_ref, b2_ref, o_ref):
        a, b1, b2 = a_ref[...], b1_ref[...], b2_ref[...]
        if fold:
            v = x_ref[:, 0:LANES].astype(F32)
            for blk in range(1, nb):
                v = v + x_ref[:, blk * LANES:(blk + 1) * LANES].astype(F32)
            o_ref[...] = rot(v, a, b1, b2).astype(o_ref.dtype)
            return
        for blk in range(nb):
            v = x_ref[:, blk * LANES:(blk + 1) * LANES].astype(F32)
            if roped(blk):
                v = rot(v, a, b1, b2)
            if scale is not None:
                v = v * scale
            o_ref[:, blk * LANES:(blk + 1) * LANES] = v.astype(o_ref.dtype)

    row = lambda w: pl.BlockSpec((tr, w), lambda i: (i, 0))
    return pl.pallas_call(
        body, name=_nm(base), grid=(S // tr,), out_shape=jax.ShapeDtypeStruct((S, out_c), out_dtype),
        in_specs=[row(C), row(LANES), row(LANES), row(LANES)], out_specs=row(out_c),
        compiler_params=_cp(("parallel",)),
    )(x, *tabs)


def _attn_tile(S):
    return min(512, S)


def _tri_schedule(n, by_key):
    pairs = ([(i, j) for j in range(n) for i in range(j, n)] if by_key
             else [(i, j) for i in range(n) for j in range(i + 1)])
    return (jnp.asarray([p[0] for p in pairs], jnp.int32), jnp.asarray([p[1] for p in pairs], jnp.int32))


def _mask_table(cid, t):
    n = cid.shape[0] // t
    blocks = cid.reshape(n, t)
    cmin_q, cmax_k = jnp.min(blocks, axis=1), jnp.max(blocks, axis=1)
    need = (cmax_k[None, :] > cmin_q[:, None]) | jnp.eye(n, dtype=bool)
    return need.astype(jnp.int32).reshape(n * n)


def _attn_mask(cidq_ref, cidk_ref, i, j, t):
    qrow = i * t + lax.broadcasted_iota(jnp.int32, (t, 1), 0)
    kcol = j * t + lax.broadcasted_iota(jnp.int32, (1, t), 1)
    qlim = (qrow // Q_BLOCK + 1) * Q_BLOCK
    return (cidk_ref[...] <= cidq_ref[...]) & (kcol < qlim)


ATTN_FWD_HEADS = 8
ATTN_BWD_HEADS = 4
ATTN_SCALE = (MLA_NOPE + MLA_ROPE) ** -0.5
ATTN_SCALE2 = ATTN_SCALE * LOG2E


def _attn_fwd_call(q, kn, v, kr, aux, heads, comm=None):
    S = q.shape[0]
    t = _attn_tile(S)
    n = S // t
    hp = ATTN_FWD_HEADS if heads % ATTN_FWD_HEADS == 0 else 1
    qi_tab, kj_tab = _tri_schedule(n, False)

    def body(qi_ref, kj_ref, need_ref, q_ref, kn_ref, v_ref, kr_ref, cidq_ref, cidk_ref, o_ref, lse_ref,
             m_ref, l_ref, acc_ref):
        st = pl.program_id(1)
        i, j = qi_ref[st], kj_ref[st]

        @pl.when(j == 0)
        def _():
            m_ref[...] = jnp.full(m_ref.shape, NEG_INF, F32)
            l_ref[...] = jnp.zeros(l_ref.shape, F32)
            acc_ref[...] = jnp.zeros(acc_ref.shape, F32)

        def update(masked):
            mask = _attn_mask(cidq_ref, cidk_ref, i, j, t) if masked else None
            for hh in range(hp):
                lanes = slice(hh * LANES, (hh + 1) * LANES)
                k = jnp.concatenate([kn_ref[:, lanes], kr_ref[...]], axis=1)
                qh = q_ref[:, hh * MLA_QK_PAD:(hh + 1) * MLA_QK_PAD]
                s = lax.dot_general(qh, k, (((1,), (1,)), ((), ())), preferred_element_type=F32)
                if masked:
                    s = jnp.where(mask, s, NEG_INF)
                m_prev = m_ref[:, lanes]
                m_new = jnp.maximum(m_prev, jnp.max(s, axis=1, keepdims=True))
                alpha = jnp.exp2(m_prev - m_new)
                p = jnp.exp2(s - m_new[:, :1])
                l_ref[:, lanes] = alpha * l_ref[:, lanes] + jnp.sum(p, axis=1, keepdims=True)
                acc_ref[:, lanes] = alpha * acc_ref[:, lanes] + jnp.dot(p.astype(BF16), v_ref[:, lanes],
                                                                        preferred_element_type=F32)
                m_ref[:, lanes] = m_new

        need = need_ref[i * n + j]

        @pl.when(need != 0)
        def _():
            update(True)

        @pl.when(need == 0)
        def _():
            update(False)

        @pl.when(j == i)
        def _():
            o_ref[...] = (acc_ref[...] / l_ref[...]).astype(o_ref.dtype)
            lse_ref[...] = m_ref[...] + jnp.log(l_ref[...]) * LOG2E

    qmap = lambda h, s, qi, kj, need: (qi[s], h)
    kmap = lambda h, s, qi, kj, need: (kj[s], h)
    return _pcall(
        body, "attn_fwd", (heads // hp, qi_tab.shape[0]), 3,
        in_specs=[pl.BlockSpec((t, hp * MLA_QK_PAD), qmap), pl.BlockSpec((t, hp * MLA_NOPE), kmap),
                  pl.BlockSpec((t, hp * MLA_V), kmap),
                  pl.BlockSpec((t, LANES), lambda h, s, qi, kj, need: (kj[s], 0)),
                  pl.BlockSpec((t, 1), lambda h, s, qi, kj, need: (qi[s], 0)),
                  pl.BlockSpec((1, t), lambda h, s, qi, kj, need: (0, kj[s]))],
        out_specs=[pl.BlockSpec((t, hp * MLA_V), qmap), pl.BlockSpec((t, hp * LANES), qmap)],
        out_shape=[jax.ShapeDtypeStruct((S, heads * MLA_V), BF16), jax.ShapeDtypeStruct((S, heads * LANES), F32)],
        scratch=[pltpu.VMEM((t, hp * LANES), F32), pltpu.VMEM((t, hp * LANES), F32), pltpu.VMEM((t, hp * MLA_V), F32)],
        sem=("parallel", "arbitrary"),
        args=[qi_tab, kj_tab, aux['need'], q, kn, v, kr, aux['cidq'], aux['cidk']], comm=comm)


def _attn_bwd_call(q, kn, v, kr, o, lse, do, aux, heads, comm=None):
    S = q.shape[0]
    t = _attn_tile(S)
    n = S // t
    qi_tab, kj_tab = _tri_schedule(n, True)
    scale = ATTN_SCALE
    nt_dims = (((1,), (1,)), ((), ()))
    tn_dims = (((0,), (0,)), ((), ()))

    hp = ATTN_BWD_HEADS if heads % ATTN_BWD_HEADS == 0 else 1

    def body(qi_ref, kj_ref, need_ref, q_ref, kn_ref, v_ref, kr_ref, cidq_ref, cidk_ref, o_ref, lse_ref, do_ref,
             ta_ref, tb1_ref, tb2_ref, dq_ref, dkn_ref, dv_ref, dkr_ref, dk_acc, dv_acc, dq_acc):
        st = pl.program_id(1)
        i, j = qi_ref[st], kj_ref[st]

        @pl.when(st == 0)
        def _():
            dq_acc[...] = jnp.zeros(dq_acc.shape, F32)

        @pl.when(i == j)
        def _():
            dk_acc[...] = jnp.zeros(dk_acc.shape, F32)
            dv_acc[...] = jnp.zeros(dv_acc.shape, F32)

        rows = pl.ds(pl.multiple_of(i * t, t), t)

        def grads(masked):
            mask = _attn_mask(cidq_ref, cidk_ref, i, j, t) if masked else None
            for hh in range(hp):
                lanes = slice(hh * LANES, (hh + 1) * LANES)
                wide = slice(hh * MLA_QK_PAD, (hh + 1) * MLA_QK_PAD)
                k = jnp.concatenate([kn_ref[:, lanes], kr_ref[...]], axis=1)
                qt, do = q_ref[:, wide], do_ref[:, lanes]
                s = lax.dot_general(qt, k, nt_dims, preferred_element_type=F32)
                if masked:
                    s = jnp.where(mask, s, NEG_INF)
                dp = lax.dot_general(do, v_ref[:, lanes], nt_dims, preferred_element_type=F32)
                dsum = jnp.sum(do.astype(F32) * o_ref[:, lanes].astype(F32), axis=1, keepdims=True)
                p = jnp.exp2(s - lse_ref[:, hh * LANES:hh * LANES + 1])
                ds = (p * (dp - dsum) * scale).astype(BF16)
                dv_acc[:, lanes] += lax.dot_general(p.astype(BF16), do, tn_dims, preferred_element_type=F32)
                dk_acc[:, wide] += lax.dot_general(ds, qt, tn_dims, preferred_element_type=F32)
                dq_acc[rows, wide] += jnp.dot(ds, k, preferred_element_type=F32)

        need = need_ref[i * n + j]

        @pl.when(need != 0)
        def _():
            grads(True)

        @pl.when(need == 0)
        def _():
            grads(False)

        @pl.when(i == n - 1)
        def _():
            for hh in range(hp):
                lanes = slice(hh * LANES, (hh + 1) * LANES)
                off = hh * MLA_QK_PAD
                dkn_ref[:, lanes] = (dk_acc[:, off:off + MLA_NOPE] * (1.0 / ATTN_SCALE2)).astype(dkn_ref.dtype)
                dkr_ref[:, lanes] = dk_acc[:, off + MLA_NOPE:off + MLA_QK_PAD] * (1.0 / ATTN_SCALE2)
            dv_ref[...] = dv_acc[...].astype(dv_ref.dtype)

        @pl.when(i == j)
        def _():
            for hh in range(hp):
                off = hh * MLA_QK_PAD
                dq_ref[rows, off:off + MLA_NOPE] = dq_acc[rows, off:off + MLA_NOPE].astype(dq_ref.dtype)
                g = dq_acc[rows, off + MLA_NOPE:off + MLA_QK_PAD]
                g = g * ta_ref[...] + pltpu.roll(g, 96, 1) * tb1_ref[...] + pltpu.roll(g, 32, 1) * tb2_ref[...]
                dq_ref[rows, off + MLA_NOPE:off + MLA_QK_PAD] = g.astype(dq_ref.dtype)

    qmap = lambda h, s, qi, kj, need: (qi[s], h)
    kmap = lambda h, s, qi, kj, need: (kj[s], h)
    whole = pl.BlockSpec((t, LANES), lambda h, s, qi, kj, need: (qi[s], 0))
    return _pcall(
        body, "attn_bwd", (heads // hp, qi_tab.shape[0]), 3,
        in_specs=[pl.BlockSpec((t, hp * MLA_QK_PAD), qmap), pl.BlockSpec((t, hp * MLA_NOPE), kmap),
                  pl.BlockSpec((t, hp * MLA_V), kmap),
                  pl.BlockSpec((t, LANES), lambda h, s, qi, kj, need: (kj[s], 0)),
                  pl.BlockSpec((t, 1), lambda h, s, qi, kj, need: (qi[s], 0)),
                  pl.BlockSpec((1, t), lambda h, s, qi, kj, need: (0, kj[s])),
                  pl.BlockSpec((t, hp * MLA_V), qmap), pl.BlockSpec((t, hp * LANES), qmap),
                  pl.BlockSpec((t, hp * MLA_V), qmap), whole, whole, whole],
        out_specs=[pl.BlockSpec((S, hp * MLA_QK_PAD), lambda h, s, qi, kj, need: (0, h)),
                   pl.BlockSpec((t, hp * MLA_NOPE), kmap), pl.BlockSpec((t, hp * MLA_V), kmap),
                   pl.BlockSpec((t, hp * LANES), kmap)],
        out_shape=[jax.ShapeDtypeStruct((S, heads * MLA_QK_PAD), BF16), jax.ShapeDtypeStruct((S, heads * MLA_NOPE), BF16),
                   jax.ShapeDtypeStruct((S, heads * MLA_V), BF16), jax.ShapeDtypeStruct((S, heads * LANES), F32)],
        scratch=[pltpu.VMEM((t, hp * MLA_QK_PAD), F32), pltpu.VMEM((t, hp * MLA_V), F32),
                 pltpu.VMEM((S, hp * MLA_QK_PAD), F32)],
        sem=("parallel", "arbitrary"),
        args=[qi_tab, kj_tab, aux['need'], q, kn, v, kr, aux['cidq'], aux['cidk'], o, lse, do, *aux['rope'][1]],
        comm=comm)


def make_attention(aux, heads):
    tabs_f, tabs_b = aux['rope']
    odd, every = (lambda blk: blk % 2 == 1), (lambda blk: True)

    def run_host(q_raw, kn, v, kr_raw, comm=None):
        q = _rope_call(q_raw, tabs_f, odd, BF16, "rope_q", scale=ATTN_SCALE2)
        kr = _rope_call(kr_raw, tabs_f, every, BF16, "rope_k")
        o, lse, *carried = _attn_fwd_call(q, kn, v, kr, aux, heads, comm)
        return o, (q, kn, v, kr, o, lse), carried

    def bwd_host(res, do, comm=None):
        q, kn, v, kr, o, lse = res
        dq_raw, dkn, dv, dkr, *carried = _attn_bwd_call(q, kn, v, kr, o, lse, do, aux, heads, comm)
        return (dq_raw, dkn, dv, _rope_call(dkr, tabs_b, every, F32, "rope_dk", fold=True)), carried

    @jax.custom_vjp
    def attn(q_raw, kn, v, kr_raw):
        return run_host(q_raw, kn, v, kr_raw)[0]

    attn.defvjp(lambda *a: run_host(*a)[:2], lambda res, do: bwd_host(res, do)[0])
    return attn, run_host, bwd_host


GLA_ROWS = 256
GLA_HEADS_PER_STEP = 4


def _tri(lower):
    r = lax.broadcasted_iota(jnp.int32, (CHUNK, CHUNK), 0)
    c = lax.broadcasted_iota(jnp.int32, (CHUNK, CHUNK), 1)
    return jnp.where((c <= r) if lower else (c >= r), 1.0, 0.0).astype(F32)


def _gla_chunk(q_ref, k_ref, v_ref, la_ref, sl, hh):
    lk, lv = slice(hh * GLA_DK, (hh + 1) * GLA_DK), slice(hh * GLA_DV, (hh + 1) * GLA_DV)
    la = la_ref[sl, lk]
    cum = jnp.dot(_tri(True), la, preferred_element_type=F32, precision=lax.Precision.HIGHEST)
    tot = cum[CHUNK - 1:CHUNK, :]
    e = jnp.exp(tot - cum)
    k = k_ref[sl, lk].astype(F32)
    kdec = k * e
    v = v_ref[sl, lv]
    upd_t = lax.dot_general(v.astype(BF16), kdec.astype(BF16), (((0,), (0,)), ((), ())), preferred_element_type=F32)
    qs = (q_ref[sl, lk].astype(F32) * (GLA_DK ** -0.5)).astype(BF16)
    return e, k, kdec, v, upd_t, jnp.exp(tot), qs


def _gla_group(heads):
    return GLA_HEADS_PER_STEP if heads % GLA_HEADS_PER_STEP == 0 else 1


def _gla_specs(heads, hp, rows_map):
    groups = heads // hp
    return [pl.BlockSpec((GLA_ROWS, hp * GLA_DK), lambda h, b: (rows_map(b), h)),
            pl.BlockSpec((GLA_ROWS, hp * GLA_DK), lambda h, b: (rows_map(b), groups + h)),
            pl.BlockSpec((GLA_ROWS, hp * GLA_DV), lambda h, b: (rows_map(b), groups + h)),
            pl.BlockSpec((GLA_ROWS, hp * GLA_DK), lambda h, b: (rows_map(b), h))]


def _gla_fwd_call(hm, la, heads):
    S = hm.shape[0]
    assert S % GLA_ROWS == 0
    nb, cpb, hp = S // GLA_ROWS, GLA_ROWS // CHUNK, _gla_group(heads)

    def body(q_ref, k_ref, v_ref, la_ref, o_ref, sp_ref, st_ref):
        @pl.when(pl.program_id(1) == 0)
        def _():
            st_ref[...] = jnp.zeros(st_ref.shape, F32)

        for c in range(cpb):
            sl = slice(c * CHUNK, (c + 1) * CHUNK)
            for hh in range(hp):
                _, _, _, _, upd_t, decay, qs = _gla_chunk(q_ref, k_ref, v_ref, la_ref, sl, hh)
                state = st_ref[hh]
                sp_ref[hh, c] = state
                state = state * decay + upd_t
                st_ref[hh] = state
                o_ref[sl, hh * GLA_DV:(hh + 1) * GLA_DV] = lax.dot_general(
                    qs, state.astype(BF16), (((1,), (1,)), ((), ())), preferred_element_type=F32)

    return pl.pallas_call(
        body, name=_nm("gla_fwd"), grid=(heads // hp, nb),
        out_shape=(jax.ShapeDtypeStruct((S, heads * GLA_DV), F32),
                   jax.ShapeDtypeStruct((heads, S // CHUNK, GLA_DV, GLA_DK), F32)),
        in_specs=_gla_specs(heads, hp, lambda b: b),
        out_specs=(pl.BlockSpec((GLA_ROWS, hp * GLA_DV), lambda h, b: (b, h)),
                   pl.BlockSpec((hp, cpb, GLA_DV, GLA_DK), lambda h, b: (h, b, 0, 0))),
        scratch_shapes=[pltpu.VMEM((hp, GLA_DV, GLA_DK), F32)],
        compiler_params=_cp(("parallel", "arbitrary")),
    )(hm, hm, hm, la)


def _gla_bwd_call(hm, la, sprev, do, heads):
    S = hm.shape[0]
    nb, cpb, hp = S // GLA_ROWS, GLA_ROWS // CHUNK, _gla_group(heads)
    scale = GLA_DK ** -0.5

    def body(q_ref, k_ref, v_ref, la_ref, sp_ref, do_ref, dq_ref, dk_ref, dv_ref, dla_ref, carry_ref):
        @pl.when(pl.program_id(1) == 0)
        def _():
            carry_ref[...] = jnp.zeros(carry_ref.shape, F32)

        for c in reversed(range(cpb)):
            sl = slice(c * CHUNK, (c + 1) * CHUNK)
            for hh in range(hp):
                lk, lv = slice(hh * GLA_DK, (hh + 1) * GLA_DK), slice(hh * GLA_DV, (hh + 1) * GLA_DV)
                e, k, kdec, v, upd_t, decay, qs = _gla_chunk(q_ref, k_ref, v_ref, la_ref, sl, hh)
                sp = sp_ref[hh, c]
                s_n = sp * decay + upd_t
                dob = do_ref[sl, lv].astype(BF16)
                g = carry_ref[hh] + lax.dot_general(dob, qs, (((0,), (0,)), ((), ())), preferred_element_type=F32)
                gb = g.astype(BF16)
                dq_ref[sl, lk] = (jnp.dot(dob, s_n.astype(BF16), preferred_element_type=F32) * scale).astype(dq_ref.dtype)
                ddecay = jnp.sum(g * sp, axis=0, keepdims=True)
                dkdec = jnp.dot(v.astype(BF16), gb, preferred_element_type=F32)
                dv_ref[sl, lv] = lax.dot_general(kdec.astype(BF16), gb, (((1,), (1,)), ((), ())),
                                                 preferred_element_type=F32).astype(dv_ref.dtype)
                dk_ref[sl, lk] = (dkdec * e).astype(dk_ref.dtype)
                w = dkdec * k * e
                dtot = jnp.sum(w, axis=0, keepdims=True) + ddecay * decay
                last = lax.broadcasted_iota(jnp.int32, (CHUNK, 1), 0) == CHUNK - 1
                dcum = jnp.where(last, dtot - w, -w)
                dla_ref[sl, lk] = jnp.dot(_tri(False), dcum, preferred_element_type=F32, precision=lax.Precision.HIGHEST)
                carry_ref[hh] = g * decay

    rev = lambda b: nb - 1 - b
    narrow = pl.BlockSpec((GLA_ROWS, hp * GLA_DK), lambda h, b: (rev(b), h))
    wide = pl.BlockSpec((GLA_ROWS, hp * GLA_DV), lambda h, b: (rev(b), h))
    return pl.pallas_call(
        body, name=_nm("gla_bwd"), grid=(heads // hp, nb),
        out_shape=(jax.ShapeDtypeStruct((S, heads * GLA_DK), hm.dtype), jax.ShapeDtypeStruct((S, heads * GLA_DK), hm.dtype),
                   jax.ShapeDtypeStruct((S, heads * GLA_DV), hm.dtype), jax.ShapeDtypeStruct((S, heads * GLA_DK), F32)),
        in_specs=_gla_specs(heads, hp, rev) + [
            pl.BlockSpec((hp, cpb, GLA_DV, GLA_DK), lambda h, b: (h, rev(b), 0, 0)), wide],
        out_specs=(narrow, narrow, wide, narrow),
        scratch_shapes=[pltpu.VMEM((hp, GLA_DV, GLA_DK), F32)],
        compiler_params=_cp(("parallel", "arbitrary")),
    )(hm, hm, hm, la, sprev, do)


@functools.partial(jax.custom_vjp, nondiff_argnums=(3,))
def gla_core(hm, la, o_norm, heads):
    return _gla_core_fwd(hm, la, o_norm, heads)[0]


def _gla_core_fwd(hm, la, o_norm, heads):
    o, sprev = _gla_fwd_call(hm, la, heads)
    vd = heads * GLA_DV
    rows = [(o, vd, 0), (hm, vd, 2 * heads * GLA_DK // vd + 1)]
    (y,) = _rw_fwd(_gla_out_fn(heads), rows, [o_norm], [BF16], "gla_out_fwd")
    return y, (hm, la, o_norm, o, sprev)


def _gla_core_bwd(heads, res, dy):
    hm, la, o_norm, o, sprev = res
    vd = heads * GLA_DV
    rows = [(o, vd, 0), (hm, vd, 2 * heads * GLA_DK // vd + 1)]
    (do, dr), (dg,) = _rw_bwd(_gla_out_fn(heads), rows, [o_norm], [dy], [F32, hm.dtype], "gla_out_bwd")
    dq, dk, dv, dla = _gla_bwd_call(hm, la, sprev, do, heads)
    return jnp.concatenate([dq, dk, dv, dr], axis=1), dla, dg


gla_core.defvjp(_gla_core_fwd, _gla_core_bwd)


CONV_COLS = 256
HALO = 16


def _conv_rows(S):
    return min(512, S)


def _conv_taps(main_ref, halo_ref, i):
    prev = jnp.where(i > 0, halo_ref[...].astype(F32), 0.0)
    full = jnp.concatenate([prev, main_ref[...].astype(F32)], axis=0)
    return full[HALO:], pltpu.roll(full, 1, 0)[HALO:], pltpu.roll(full, 2, 0)[HALO:]


def _conv_apply(taps, w_ref, b_ref):
    x0, x1, x2 = taps
    return x2 * w_ref[0:1, :] + x1 * w_ref[1:2, :] + x0 * w_ref[2:3, :] + b_ref[...]


def _gelu_gate(uc, gc):
    return uc * jax.nn.gelu(gc)


def _conv_cols(dff, pref):
    return max(c for c in range(LANES, pref + 1, LANES) if dff % c == 0)


def _conv_in_specs(R, C, nj):
    hpr = R // HALO
    main = lambda off: pl.BlockSpec((R, C), lambda j, i: (i, j + off))
    halo = lambda off: pl.BlockSpec((HALO, C), lambda j, i: (jnp.maximum(i * hpr - 1, 0), j + off))
    par = lambda rows, off: pl.BlockSpec((rows, C), lambda j, i: (0, j + off))
    return [main(0), halo(0), main(nj), halo(nj), par(CONV_W, 0), par(CONV_W, nj), par(1, 0), par(1, nj)]


def _conv_fwd_call(h, cw, cb, comm=None):
    S, dff = h.shape[0], h.shape[1] // 2
    R, C = _conv_rows(S), _conv_cols(dff, 768)
    nj = dff // C

    def body(u_ref, uh_ref, g_ref, gh_ref, wu_ref, wg_ref, bu_ref, bg_ref, a_ref):
        i = pl.program_id(1)
        uc = _conv_apply(_conv_taps(u_ref, uh_ref, i), wu_ref, bu_ref)
        gc = _conv_apply(_conv_taps(g_ref, gh_ref, i), wg_ref, bg_ref)
        a_ref[...] = _gelu_gate(uc, gc).astype(a_ref.dtype)

    return _pcall(
        body, "conv_fwd", (nj, S // R), 0, in_specs=_conv_in_specs(R, C, nj),
        out_specs=[pl.BlockSpec((R, C), lambda j, i: (i, j))], out_shape=[jax.ShapeDtypeStruct((S, dff), BF16)],
        scratch=[], sem=("parallel", "parallel"), args=[h, h, h, h, cw, cw, cb, cb], comm=comm)


def _conv_bwd_gate_call(h, cw, cb, da, comm=None):
    S, dff = h.shape[0], h.shape[1] // 2
    R, C = _conv_rows(S), _conv_cols(dff, 768)
    nj = dff // C

    def body(u_ref, uh_ref, g_ref, gh_ref, wu_ref, wg_ref, bu_ref, bg_ref, da_ref,
             du_ref, dg_ref, dwu_ref, dwg_ref, dbu_ref, dbg_ref):
        i = pl.program_id(1)
        ut, gt = _conv_taps(u_ref, uh_ref, i), _conv_taps(g_ref, gh_ref, i)
        uc, gc = _conv_apply(ut, wu_ref, bu_ref), _conv_apply(gt, wg_ref, bg_ref)
        _, vjp_fn = jax.vjp(_gelu_gate, uc, gc)
        du, dg = vjp_fn(da_ref[...].astype(F32))
        du_ref[...] = du.astype(du_ref.dtype)
        dg_ref[...] = dg.astype(dg_ref.dtype)

        @pl.when(i == 0)
        def _():
            for r in (dwu_ref, dwg_ref, dbu_ref, dbg_ref):
                r[...] = jnp.zeros(r.shape, F32)

        for d, taps, dw_ref, db_ref in ((du, ut, dwu_ref, dbu_ref), (dg, gt, dwg_ref, dbg_ref)):
            x0, x1, x2 = taps
            dw_ref[0:1, :] += jnp.sum(d * x2, axis=0, keepdims=True)
            dw_ref[1:2, :] += jnp.sum(d * x1, axis=0, keepdims=True)
            dw_ref[2:3, :] += jnp.sum(d * x0, axis=0, keepdims=True)
            db_ref[...] += jnp.sum(d, axis=0, keepdims=True)

    tile = pl.BlockSpec((R, C), lambda j, i: (i, j))
    par = lambda rows: pl.BlockSpec((rows, C), lambda j, i: (0, j))
    return _pcall(
        body, "conv_bwd_gate", (nj, S // R), 0, in_specs=_conv_in_specs(R, C, nj) + [tile],
        out_specs=[tile, tile, par(CONV_W), par(CONV_W), par(1), par(1)],
        out_shape=[jax.ShapeDtypeStruct((S, dff), BF16), jax.ShapeDtypeStruct((S, dff), BF16),
                   jax.ShapeDtypeStruct((CONV_W, dff), F32), jax.ShapeDtypeStruct((CONV_W, dff), F32),
                   jax.ShapeDtypeStruct((1, dff), F32), jax.ShapeDtypeStruct((1, dff), F32)],
        scratch=[], sem=("parallel", "arbitrary"), args=[h, h, h, h, cw, cw, cb, cb, da], comm=comm)


def _conv_bwd_shift_call(dc, cw, into=None):
    S, dff = dc.shape
    R, C = min(1024, S), _conv_cols(dff, 1024)
    nj = dff // C
    col_off = 0 if into is None else nj
    hpr, last = R // HALO, S // HALO - 1
    ni = S // R

    def body(d_ref, nx_ref, w_ref, *rest):
        o_ref = rest[-1]
        i = pl.program_id(1)
        nxt = jnp.where(i < ni - 1, nx_ref[...].astype(F32), 0.0)
        full = jnp.concatenate([d_ref[...].astype(F32), nxt], axis=0)
        n = R + HALO
        y1, y2 = pltpu.roll(full, n - 1, 0)[:R], pltpu.roll(full, n - 2, 0)[:R]
        o_ref[...] = (full[:R] * w_ref[2:3, :] + y1 * w_ref[1:2, :] + y2 * w_ref[0:1, :]).astype(o_ref.dtype)

    in_specs = [pl.BlockSpec((R, C), lambda j, i: (i, j)),
                pl.BlockSpec((HALO, C), lambda j, i: (jnp.minimum((i + 1) * hpr, last), j)),
                pl.BlockSpec((CONV_W, C), lambda j, i: (0, j + col_off))]
    args = [dc, dc, cw]
    if into is not None:
        in_specs.append(pl.BlockSpec(memory_space=pl.ANY))
        args.append(into)
    return pl.pallas_call(
        body, name=_nm("conv_bwd_shift"), grid=(nj, ni), out_shape=jax.ShapeDtypeStruct((S, 2 * dff), BF16),
        in_specs=in_specs, out_specs=pl.BlockSpec((R, C), lambda j, i: (i, j + col_off)),
        input_output_aliases={} if into is None else {3: 0},
        compiler_params=_cp(("parallel", "parallel")),
    )(*args)


@jax.custom_vjp
def ffn_hidden(x1, x1_bf16, w3, cw, cb):
    return _ffn_hidden_fwd(x1, x1_bf16, w3, cw, cb)[0]


def _ffn_hidden_fwd(x1, x1_bf16, w3, cw, cb, comm=None):
    h = _mm(x1_bf16, w3, "nn", BF16, "up", _all_blocks(w3))
    a, *carried = _conv_fwd_call(h, cw, cb, comm)
    return (a, (x1_bf16, w3, cw, cb, h)) + ((carried,) if comm is not None else ())


def _ffn_hidden_bwd(res, da, comm=None):
    x1, w3, cw, cb, h = res
    du, dg, dwu, dwg, dbu, dbg, *carried = _conv_bwd_gate_call(h, cw, cb, da, comm)
    dh = _conv_bwd_shift_call(dg, cw, into=_conv_bwd_shift_call(du, cw))
    dx = _mm(dh, w3, "nt", F32, "up_dx", _all_blocks(w3))
    dw3 = _mm(x1, dh, "tn", w3.dtype, "up_dw", _all_blocks(w3))
    grads = (dx, jnp.zeros_like(x1), dw3, jnp.concatenate([dwu, dwg], axis=1), jnp.concatenate([dbu, dbg], axis=1))
    return (grads, carried) if comm is not None else grads


ffn_hidden.defvjp(_ffn_hidden_fwd, _ffn_hidden_bwd)


def _loss_call(y, target):
    S, D = y.shape
    tr = _row_tile(S, D)

    def body(y_ref, t_ref, sq_ref, dy_ref):
        diff = y_ref[...] - t_ref[...]
        dy_ref[...] = diff * (1.0 / D)
        part = jnp.sum(diff * diff, axis=0, keepdims=True)
        i = pl.program_id(0)

        @pl.when(i == 0)
        def _():
            sq_ref[...] = part

        @pl.when(i > 0)
        def _():
            sq_ref[...] += part

    row = pl.BlockSpec((tr, D), lambda i: (i, 0))
    return pl.pallas_call(
        body, name=_nm("loss"), grid=(S // tr,),
        out_shape=(jax.ShapeDtypeStruct((1, D), F32), jax.ShapeDtypeStruct((S, D), F32)),
        in_specs=[row, row], out_specs=(pl.BlockSpec((1, D), lambda i: (0, 0)), row),
        compiler_params=_cp(("arbitrary",)),
    )(y, target)


def _row_divisor(rows, cap=512):
    for cand in range(min(rows, cap), 15, -1):
        if rows % cand == 0 and cand % 16 == 0:
            return cand
    return rows


def _adamw_call(w, g, m, v):
    shape = w.shape
    w2, g2, m2, v2 = (a.reshape(-1, shape[-1]) for a in (w, g, m, v))
    rows, cols = w2.shape
    tr = _row_divisor(rows)

    def body(w_ref, g_ref, m_ref, v_ref, d_ref, nm_ref, nv_ref):
        g_ = g_ref[...]
        m_ = ADAM_B1 * m_ref[...] + (1.0 - ADAM_B1) * g_
        v_ = ADAM_B2 * v_ref[...] + (1.0 - ADAM_B2) * (g_ * g_)
        m_hat = m_ / (1.0 - ADAM_B1 ** ADAM_STEP)
        v_hat = v_ / (1.0 - ADAM_B2 ** ADAM_STEP)
        d_ref[...] = -ADAM_LR * (m_hat / (jnp.sqrt(v_hat) + ADAM_EPS) + ADAM_WD * w_ref[...])
        nm_ref[...] = m_
        nv_ref[...] = v_

    blk = pl.BlockSpec((tr, cols), lambda i: (i, 0))
    outs = pl.pallas_call(
        body, name=_nm("adamw"), grid=(rows // tr,),
        out_shape=tuple(jax.ShapeDtypeStruct((rows, cols), F32) for _ in range(3)),
        in_specs=[blk] * 4, out_specs=(blk,) * 3, compiler_params=_cp(("parallel",)),
    )(w2, g2, m2, v2)
    return tuple(o.reshape(shape) for o in outs)


ANY = pl.BlockSpec(memory_space=pl.ANY)


def _place():
    return lax.axis_index("x"), lax.axis_index("y"), lax.axis_index("c")


def all_gather(shards):
    n = len(shards)

    def body(*refs):
        x_refs, out_refs = refs[:n], refs[n:2 * n]
        send_sems, recv_sems, local_sems = refs[2 * n:]
        x, y, c = _place()
        me, sibling = (x, y, c), (x, y, 1 - c)
        chips = [(1 - x, y), (x, 1 - y), (1 - x, 1 - y)]

        def slot(a, px, py, pc):
            return out_refs[a].at[:, 4 * px + 2 * py + pc]

        def copy(a, k, block, to, own=False):
            return pltpu.make_async_remote_copy(
                src_ref=x_refs[a] if own else slot(a, *block), dst_ref=slot(a, *block),
                send_sem=send_sems.at[7 * a + k], recv_sem=recv_sems.at[7 * a + k], device_id=to, device_id_type=MESH)

        mine = [pltpu.make_async_copy(x_refs[a], slot(a, *me), local_sems.at[a]) for a in range(n)]
        first = []
        for a in range(n):
            mine[a].start()
            first.append(copy(a, 0, me, sibling, own=True))
            first += [copy(a, 1 + j, me, (*chip, c), own=True) for j, chip in enumerate(chips)]
        for cp in first:
            cp.start()
        passed = []
        for j, chip in enumerate(chips):
            for a in range(n):
                copy(a, 1 + j, (*chip, c), me).wait_recv()
                passed.append(copy(a, 4 + j, (*chip, c), sibling))
                passed[-1].start()
        for a in range(n):
            copy(a, 0, sibling, me).wait_recv()
        for j, chip in enumerate(chips):
            for a in range(n):
                copy(a, 4 + j, (*chip, 1 - c), me).wait_recv()
        for cp in first + passed:
            cp.wait_send()
        for cp in mine:
            cp.wait()

    return pl.pallas_call(
        body, name=_nm("all_gather"),
        out_shape=tuple(jax.ShapeDtypeStruct((s.shape[0], N_DEV) + s.shape[1:], s.dtype) for s in shards),
        in_specs=[ANY] * n, out_specs=(ANY,) * n,
        scratch_shapes=[pltpu.SemaphoreType.DMA((7 * n,)), pltpu.SemaphoreType.DMA((7 * n,)), pltpu.SemaphoreType.DMA((n,))],
    )(*shards)


def _rs_pair_exchange(gs):
    n = len(gs)

    def body(*refs):
        g_refs, recv_refs = refs[:n], refs[n:2 * n]
        send_sems, recv_sems = refs[2 * n:]
        x, y, c = _place()
        copies = [pltpu.make_async_remote_copy(
            src_ref=g_refs[a].at[:, 2 * j + (1 - c)], dst_ref=recv_refs[a].at[j], send_sem=send_sems.at[4 * a + j],
            recv_sem=recv_sems.at[4 * a + j], device_id=(x, y, 1 - c), device_id_type=MESH)
            for a in range(n) for j in range(4)]
        for cp in copies:
            cp.start()
        for cp in copies:
            cp.wait_recv()
        for cp in copies:
            cp.wait_send()

    return pl.pallas_call(
        body, name=_nm("rs_pair"),
        out_shape=tuple(jax.ShapeDtypeStruct((4, g.shape[0]) + g.shape[2:], g.dtype) for g in gs),
        in_specs=[ANY] * n, out_specs=(ANY,) * n,
        scratch_shapes=[pltpu.SemaphoreType.DMA((4 * n,)), pltpu.SemaphoreType.DMA((4 * n,))],
    )(*gs)


def _rs_chip_exchange(ps):
    n = len(ps)

    def body(*refs):
        p_refs, recv_refs = refs[:n], refs[n:2 * n]
        send_sems, recv_sems = refs[2 * n:]
        x, y, c = _place()
        chips = [(1 - x, y), (x, 1 - y), (1 - x, 1 - y)]
        copies = [pltpu.make_async_remote_copy(
            src_ref=p_refs[a].at[2 * cx + cy], dst_ref=recv_refs[a].at[k], send_sem=send_sems.at[3 * a + k],
            recv_sem=recv_sems.at[3 * a + k], device_id=(cx, cy, c), device_id_type=MESH)
            for a in range(n) for k, (cx, cy) in enumerate(chips)]
        for cp in copies:
            cp.start()
        for cp in copies:
            cp.wait_recv()
        for cp in copies:
            cp.wait_send()

    return pl.pallas_call(
        body, name=_nm("rs_chip"),
        out_shape=tuple(jax.ShapeDtypeStruct((3,) + p.shape[1:], p.dtype) for p in ps),
        in_specs=[ANY] * n, out_specs=(ANY,) * n,
        scratch_shapes=[pltpu.SemaphoreType.DMA((3 * n,)), pltpu.SemaphoreType.DMA((3 * n,))],
    )(*ps)


def _rs_pair_add(g, recv, c_idx):
    L, _, a, b = g.shape
    ta = _row_divisor(a, 1024)

    def body(c_ref, g_ref, r_ref, o_ref):
        o_ref[...] = (g_ref[...].astype(F32) + r_ref[...].astype(F32)).astype(o_ref.dtype)

    grid_spec = pltpu.PrefetchScalarGridSpec(
        num_scalar_prefetch=1, grid=(4, L, a // ta),
        in_specs=[pl.BlockSpec((None, None, ta, b), lambda j, l, i, c_ref: (l, 2 * j + c_ref[0], i, 0)),
                  pl.BlockSpec((None, None, ta, b), lambda j, l, i, c_ref: (j, l, i, 0))],
        out_specs=pl.BlockSpec((None, None, ta, b), lambda j, l, i, c_ref: (j, l, i, 0)))
    return pl.pallas_call(
        body, name=_nm("rs_pair_add"), grid_spec=grid_spec, out_shape=jax.ShapeDtypeStruct((4, L, a, b), g.dtype),
        compiler_params=_cp(("parallel", "parallel", "parallel")),
    )(c_idx, g, recv)


def _rs_final_add(p1, recv, chip_idx):
    _, L, a, b = p1.shape
    ta = _row_divisor(a, 1024)

    def body(chip_ref, p_ref, r_ref, o_ref):
        acc = p_ref[...].astype(F32)
        for k in range(3):
            acc = acc + r_ref[k].astype(F32)
        o_ref[...] = acc

    grid_spec = pltpu.PrefetchScalarGridSpec(
        num_scalar_prefetch=1, grid=(L, a // ta),
        in_specs=[pl.BlockSpec((None, None, ta, b), lambda l, i, chip_ref: (chip_ref[0], l, i, 0)),
                  pl.BlockSpec((3, None, ta, b), lambda l, i, chip_ref: (0, l, i, 0))],
        out_specs=pl.BlockSpec((None, ta, b), lambda l, i, chip_ref: (l, i, 0)))
    return pl.pallas_call(
        body, name=_nm("rs_final_add"), grid_spec=grid_spec, out_shape=jax.ShapeDtypeStruct((L, a, b), F32),
        compiler_params=_cp(("parallel", "parallel")),
    )(chip_idx, p1, recv)


def reduce_scatter(gs):
    x, y, c = _place()
    c_idx = jnp.reshape(c, (1,)).astype(jnp.int32)
    chip_idx = jnp.reshape(2 * x + y, (1,)).astype(jnp.int32)
    recv1 = _rs_pair_exchange(gs)
    p1 = [_rs_pair_add(g, r, c_idx) for g, r in zip(gs, recv1)]
    recv2 = _rs_chip_exchange(p1)
    return [_rs_final_add(p, r, chip_idx) for p, r in zip(p1, recv2)]


def all_reduce_small(v):
    r, C = v.shape

    def body(v_ref, out_ref, buf_ref, send_sems, recv_sems):
        x, y, c = _place()
        my_id = 4 * x + 2 * y + c
        buf_ref[my_id] = v_ref[...]
        copies = []
        for k in range(1, N_DEV):
            fx, fy, fc = (k >> 2) & 1, (k >> 1) & 1, k & 1
            peer = (x ^ fx, y ^ fy, c ^ fc)
            copies.append(pltpu.make_async_remote_copy(
                src_ref=v_ref, dst_ref=buf_ref.at[my_id], send_sem=send_sems.at[k - 1], recv_sem=recv_sems.at[k - 1],
                device_id=peer, device_id_type=MESH))
        for cp in copies:
            cp.start()
        for cp in copies:
            cp.wait_recv()
        for cp in copies:
            cp.wait_send()
        acc = buf_ref[0]
        for d in range(1, N_DEV):
            acc = acc + buf_ref[d]
        out_ref[...] = acc

    vm = pl.BlockSpec(memory_space=pltpu.VMEM)
    return pl.pallas_call(
        body, name=_nm("all_reduce_small"), out_shape=jax.ShapeDtypeStruct((r, C), F32),
        in_specs=[vm], out_specs=vm,
        scratch_shapes=[pltpu.VMEM((N_DEV, r, C), F32), pltpu.SemaphoreType.DMA((7,)), pltpu.SemaphoreType.DMA((7,))],
    )(v)


def _slot(ref, px, py, pc):
    return ref.at[:, 4 * px + 2 * py + pc]


def comm_gather_own(shards):
    n = len(shards)

    def build(cin, cout, send_sems, recv_sems, local_sems):
        x, y, c = _place()
        me = (x, y, c)
        direct = [(x, y, 1 - c), (1 - x, y, c), (x, 1 - y, c)]
        src_nb, dst_nb, diag = (x ^ (1 - c), y ^ c, c), (x ^ c, y ^ (1 - c), c), (1 - x, 1 - y, c)
        starts, mid_waits, mid_starts, waits = [], [], [], []
        for a in range(n):
            local = pltpu.make_async_copy(cin[a], _slot(cout[a], *me), local_sems.at[a])
            starts.append(local)
            waits.append(local.wait)
            for k, peer in enumerate(direct):
                send = pltpu.make_async_remote_copy(
                    src_ref=cin[a], dst_ref=_slot(cout[a], *me), send_sem=send_sems.at[4 * a + k],
                    recv_sem=recv_sems.at[4 * a + k], device_id=peer, device_id_type=MESH)
                arrive = pltpu.make_async_remote_copy(
                    src_ref=cin[a], dst_ref=_slot(cout[a], *peer), send_sem=send_sems.at[4 * a + k],
                    recv_sem=recv_sems.at[4 * a + k], device_id=peer, device_id_type=MESH)
                starts.append(send)
                (waits if k == 0 else mid_waits).append(arrive.wait_recv)
                waits.append(send.wait_send)
            onward = pltpu.make_async_remote_copy(
                src_ref=_slot(cout[a], *src_nb), dst_ref=_slot(cout[a], *src_nb), send_sem=send_sems.at[4 * a + 3],
                recv_sem=recv_sems.at[4 * a + 3], device_id=dst_nb, device_id_type=MESH)
            arrive = pltpu.make_async_remote_copy(
                src_ref=_slot(cout[a], *src_nb), dst_ref=_slot(cout[a], *diag), send_sem=send_sems.at[4 * a + 3],
                recv_sem=recv_sems.at[4 * a + 3], device_id=dst_nb, device_id_type=MESH)
            mid_starts.append(onward)
            waits += [arrive.wait_recv, onward.wait_send]
        return starts, mid_waits, mid_starts, waits

    out_shapes = [jax.ShapeDtypeStruct((s.shape[0], N_DEV) + s.shape[1:], s.dtype) for s in shards]
    return Comm(shards, out_shapes, {}, 4 * n, 4 * n, n, build)


def comm_gather_pass(partial):
    n = len(partial)

    def build(cin, cout, send_sems, recv_sems, local_sems):
        x, y, c = _place()
        chips = [(1 - x, y), (x, 1 - y), (1 - x, 1 - y)]
        starts, waits = [], []
        for a in range(n):
            for j, chip in enumerate(chips):
                send = pltpu.make_async_remote_copy(
                    src_ref=_slot(cout[a], *chip, c), dst_ref=_slot(cout[a], *chip, c), send_sem=send_sems.at[3 * a + j],
                    recv_sem=recv_sems.at[3 * a + j], device_id=(x, y, 1 - c), device_id_type=MESH)
                arrive = pltpu.make_async_remote_copy(
                    src_ref=_slot(cout[a], *chip, c), dst_ref=_slot(cout[a], *chip, 1 - c),
                    send_sem=send_sems.at[3 * a + j], recv_sem=recv_sems.at[3 * a + j],
                    device_id=(x, y, 1 - c), device_id_type=MESH)
                starts.append(send)
                waits += [arrive.wait_recv, send.wait_send]
        return starts, waits

    out_shapes = [jax.ShapeDtypeStruct(p.shape, p.dtype) for p in partial]
    return Comm(partial, out_shapes, {a: a for a in range(n)}, 3 * n, 3 * n, 1, build)


def comm_rs_pair(gs):
    n = len(gs)

    def build(cin, cout, send_sems, recv_sems, local_sems):
        x, y, c = _place()
        starts, waits = [], []
        for a in range(n):
            for j in range(4):
                cp = pltpu.make_async_remote_copy(
                    src_ref=cin[a].at[:, 2 * j + (1 - c)], dst_ref=cout[a].at[j], send_sem=send_sems.at[4 * a + j],
                    recv_sem=recv_sems.at[4 * a + j], device_id=(x, y, 1 - c), device_id_type=MESH)
                starts.append(cp)
                waits += [cp.wait_recv, cp.wait_send]
        return starts, waits

    out_shapes = [jax.ShapeDtypeStruct((4, g.shape[0]) + g.shape[2:], g.dtype) for g in gs]
    return Comm(gs, out_shapes, {}, 4 * n, 4 * n, 1, build)


def comm_rs_chip(ps):
    n = len(ps)

    def build(cin, cout, send_sems, recv_sems, local_sems):
        x, y, c = _place()
        chips = [(1 - x, y), (x, 1 - y), (1 - x, 1 - y)]
        starts, waits = [], []
        for a in range(n):
            for k, (cx, cy) in enumerate(chips):
                cp = pltpu.make_async_remote_copy(
                    src_ref=cin[a].at[2 * cx + cy], dst_ref=cout[a].at[k], send_sem=send_sems.at[3 * a + k],
                    recv_sem=recv_sems.at[3 * a + k], device_id=(cx, cy, c), device_id_type=MESH)
                starts.append(cp)
                waits += [cp.wait_recv, cp.wait_send]
        return starts, waits

    out_shapes = [jax.ShapeDtypeStruct((3,) + p.shape[1:], p.dtype) for p in ps]
    return Comm(ps, out_shapes, {}, 3 * n, 3 * n, 1, build)


def _pack(arrays, dtype, row_align):
    lead = arrays[0].shape[:-1]
    quantum = row_align * PACK_COLS
    parts, sizes = [], []
    for a in arrays:
        n = a.shape[-1]
        padded = _round_up(n, quantum)
        a = a.astype(dtype)
        if padded != n:
            a = jnp.pad(a, [(0, 0)] * len(lead) + [(0, padded - n)])
        parts.append(a.reshape(*lead, padded // PACK_COLS, PACK_COLS))
        sizes.append((n, padded // PACK_COLS))
    return jnp.concatenate(parts, axis=len(lead)), sizes


def _unpack(packed, sizes):
    lead = packed.shape[:-2]
    out, row = [], 0
    for n, rows in sizes:
        part = lax.slice_in_dim(packed, row, row + rows, axis=len(lead))
        out.append(part.reshape(*lead, rows * PACK_COLS)[..., :n])
        row += rows
    return out


def _unshard(gathered, axis):
    _, L, a, b = gathered.shape
    if axis == 1:
        return [gathered[:, l].reshape(N_DEV * a, b) for l in range(L)]
    return [jnp.transpose(gathered[:, l], (1, 0, 2)).reshape(a, N_DEV * b) for l in range(L)]


def _reshard(fulls, axis):
    blocks = []
    for f in fulls:
        A, B = f.shape
        if axis == 1:
            blocks.append(f.reshape(N_DEV, A // N_DEV, B))
        else:
            blocks.append(jnp.transpose(f.reshape(A, N_DEV, B // N_DEV), (1, 0, 2)))
    return jnp.stack(blocks, axis=1)


def _as3(a):
    return a if a.ndim == 3 else a[:, None, :]


def _prep_big(name, w, dims):
    w = w.astype(BF16)
    L, a, b = w.shape
    if name == 'mla_w_in':
        return jnp.pad(w, ((0, 0), (0, 0), (0, dims['h_width'] - b)))
    if name == 'mla_w_uq':
        hd = MLA_NOPE + MLA_ROPE
        w = jnp.pad(w.reshape(L, a, b // hd, hd), ((0, 0), (0, 0), (0, 0), (0, MLA_QK_PAD - hd)))
        return w.reshape(L, a, b // hd * MLA_QK_PAD)
    if name == 'ffn_w_up':
        return jnp.pad(w, ((0, 0), (0, 0), (0, _round_up(b, CONV_COLS) - b)))
    return w


def _unprep_big(name, g, shape):
    L, a, b = shape
    if name == 'mla_w_uq':
        hd = MLA_NOPE + MLA_ROPE
        return g.reshape(L, a, b // hd, MLA_QK_PAD)[..., :hd].reshape(L, a, b)
    return g[:, :, :b]


def _rope_tables(positions):
    inv = 1.0 / (ROPE_THETA ** (jnp.arange(0, MLA_ROPE, 2, dtype=F32) / MLA_ROPE))
    ang = positions.astype(F32)[:, None] * inv
    cos, sin = jnp.cos(ang), jnp.sin(ang)
    one, zero = jnp.ones_like(cos), jnp.zeros_like(cos)
    a = jnp.concatenate([cos, cos, one, one], axis=1)
    up = jnp.concatenate([sin, zero, zero, zero], axis=1)
    down = jnp.concatenate([zero, sin, zero, zero], axis=1)
    return (a, -up, down), (a, up, -down)


def _rows_full(w):
    return w.reshape(w.shape[0] * w.shape[1], w.shape[2])


def _ops(depth):
    alpha = (2 * depth) ** 0.25
    return {'ln_res': rw_op(_ln_res_fn(alpha), "ln_res", 2, [F32], shadow=True),
            'ple': rw_op(_ple_fn, "ple", 3, [F32], shadow=True), 'gla_gate': rw_op(_gla_gate_fn, "gla_gate", 1, [F32])}


MLA_PRE_W = ['mla_w_in', 'mla_q_norm', 'mla_kv_norm', 'mla_w_uq', 'mla_w_uk', 'mla_w_uv']
FFN_IN_W = ['mla_w_o', 'ln1_g', 'ln1_b', 'ffn_conv_w', 'ffn_conv_b']
FFN_OUT_W = ['ffn_w_down', 'ln2_g', 'ln2_b', 'ple_w_gate', 'ple_w_proj', 'ple_b_gate']


def _mla_heads(wl, j):
    return N_DEV * wl['mla_w_uk'][j].shape[2] // MLA_NOPE


def _mla_pre(x, xb, wl, j):
    w_uq, w_uk, w_uv = wl['mla_w_uq'][j], wl['mla_w_uk'][j], wl['mla_w_uv'][j]
    h = linear_sh(x, xb, _rows_full(wl['mla_w_in'][j]), F32)
    mla_norm = rw_op(_mla_norm_fn(w_uq.shape[1], w_uk.shape[1]), "mla_norm", 1, [BF16, BF16, F32])
    cq, ckv, kr_raw = mla_norm(h, wl['mla_q_norm'][j], wl['mla_kv_norm'][j])
    return linear(cq, w_uq, BF16), linear(ckv, w_uk, BF16), linear(ckv, w_uv, BF16), kr_raw


def _gla_mixer(x, xb, wl, j, ops):
    w_in3, w_a2 = wl['gla_w_in'][j], wl['gla_w_a2'][j]
    w_o = _rows_full(wl['gla_w_o'][j])
    w_in = jnp.transpose(w_in3, (1, 0, 2)).reshape(w_in3.shape[1], N_DEV * w_in3.shape[2])
    heads = w_o.shape[0] // GLA_DV
    n_main = 2 * heads * GLA_DK + 2 * heads * GLA_DV
    w_a = jnp.pad(w_in[:, n_main:], ((0, 0), (0, LANES - GLA_RANK)))
    w_a2_p = jnp.pad(w_a2, ((0, LANES - GLA_RANK), (0, 0))).astype(BF16)
    hm = linear_sh(x, xb, w_in[:, :n_main], BF16)
    ha = linear_sh(x, xb, w_a, BF16)
    (la,) = ops['gla_gate'](linear(ha, w_a2_p, F32), wl['gla_b_a'][j])
    return linear(gla_core(hm, la, wl['gla_o_norm'][j], heads), w_o, F32)


def _ffn_in(x, m, wl, i, ops, bp):
    x1, x1b = ops['ln_res'](x, m, wl['ln1_g'][i], wl['ln1_b'][i])
    cw, cb = wl['ffn_conv_w'][i], wl['ffn_conv_b'][i]
    bu = cw.shape[1] // N_DEV
    cwp = jnp.pad(cw.reshape(CONV_W, N_DEV, bu), ((0, 0), (0, 0), (0, bp - bu))).reshape(CONV_W, N_DEV * bp)
    cbp = jnp.pad(cb.reshape(1, N_DEV, bu), ((0, 0), (0, 0), (0, bp - bu))).reshape(1, N_DEV * bp)
    return x1, lax.stop_gradient(x1b), cwp, cbp


def _ffn_out(x1, a, wl, p_i, i, ops):
    w_down3 = wl['ffn_w_down'][i]
    half, bu, d_model = N_DEV // 2, 2 * w_down3.shape[1], w_down3.shape[2]
    bp = a.shape[1] // half
    w_down = jnp.pad(w_down3.reshape(half, bu, d_model), ((0, 0), (0, bp - bu), (0, 0))).reshape(half * bp, d_model)
    f = linear(a, w_down, F32)
    x2, x2b = ops['ln_res'](x1, f, wl['ln2_g'][i], wl['ln2_b'][i])
    glog = linear_sh(x2, lax.stop_gradient(x2b), _rows_full(wl['ple_w_gate'][i]), F32)
    pp = linear(p_i, wl['ple_w_proj'][i], F32)
    x, xb = ops['ple'](x2, glog, pp, wl['ple_b_gate'][i])
    return x, lax.stop_gradient(xb)


def _layer(x, xb, wl, p_i, aux, i, ops):
    j = i // 2
    if i % 2 == 0:
        q_raw, kn, v, kr_raw = _mla_pre(x, xb, wl, j)
        o = make_attention(aux, _mla_heads(wl, j))[0](q_raw, kn, v, kr_raw)
        m = linear(o, _rows_full(wl['mla_w_o'][j]), F32)
    else:
        m = _gla_mixer(x, xb, wl, j, ops)
    x1, x1b, cwp, cbp = _ffn_in(x, m, wl, i, ops, wl['ffn_w_up'][i].shape[2])
    return _ffn_out(x1, ffn_hidden(x1, x1b, wl['ffn_w_up'][i], cwp, cbp), wl, p_i, i, ops)


def kernel(x, p, positions, mla_w_in, mla_q_norm, mla_kv_norm, mla_w_uq, mla_w_uk, mla_w_uv, mla_w_o, gla_w_in, gla_w_a2, gla_b_a, gla_o_norm, gla_w_o, ln1_g, ln1_b, ln2_g, ln2_b, ffn_w_up, ffn_conv_w, ffn_conv_b, ffn_w_down, ple_w_proj, ple_w_gate, ple_b_gate, loss_target, m_mla_w_in, m_mla_q_norm, m_mla_kv_norm, m_mla_w_uq, m_mla_w_uk, m_mla_w_uv, m_mla_w_o, m_gla_w_in, m_gla_w_a2, m_gla_b_a, m_gla_o_norm, m_gla_w_o, m_ln1_g, m_ln1_b, m_ln2_g, m_ln2_b, m_ffn_w_up, m_ffn_conv_w, m_ffn_conv_b, m_ffn_w_down, m_ple_w_proj, m_ple_w_gate, m_ple_b_gate, v_mla_w_in, v_mla_q_norm, v_mla_kv_norm, v_mla_w_uq, v_mla_w_uk, v_mla_w_uv, v_mla_w_o, v_gla_w_in, v_gla_w_a2, v_gla_b_a, v_gla_o_norm, v_gla_w_o, v_ln1_g, v_ln1_b, v_ln2_g, v_ln2_b, v_ffn_w_up, v_ffn_conv_w, v_ffn_conv_b, v_ffn_w_down, v_ple_w_proj, v_ple_w_gate, v_ple_b_gate):
    w = dict(zip(WEIGHTS, (mla_w_in, mla_q_norm, mla_kv_norm, mla_w_uq, mla_w_uk, mla_w_uv, mla_w_o, gla_w_in, gla_w_a2,
                           gla_b_a, gla_o_norm, gla_w_o, ln1_g, ln1_b, ln2_g, ln2_b, ffn_w_up, ffn_conv_w, ffn_conv_b,
                           ffn_w_down, ple_w_proj, ple_w_gate, ple_b_gate)))
    m_in = dict(zip(WEIGHTS, (m_mla_w_in, m_mla_q_norm, m_mla_kv_norm, m_mla_w_uq, m_mla_w_uk, m_mla_w_uv, m_mla_w_o,
                              m_gla_w_in, m_gla_w_a2, m_gla_b_a, m_gla_o_norm, m_gla_w_o, m_ln1_g, m_ln1_b, m_ln2_g,
                              m_ln2_b, m_ffn_w_up, m_ffn_conv_w, m_ffn_conv_b, m_ffn_w_down, m_ple_w_proj, m_ple_w_gate,
                              m_ple_b_gate)))
    v_in = dict(zip(WEIGHTS, (v_mla_w_in, v_mla_q_norm, v_mla_kv_norm, v_mla_w_uq, v_mla_w_uk, v_mla_w_uv, v_mla_w_o,
                              v_gla_w_in, v_gla_w_a2, v_gla_b_a, v_gla_o_norm, v_gla_w_o, v_ln1_g, v_ln1_b, v_ln2_g,
                              v_ln2_b, v_ffn_w_up, v_ffn_conv_w, v_ffn_conv_b, v_ffn_w_down, v_ple_w_proj, v_ple_w_gate,
                              v_ple_b_gate)))
    _uid[0] = itertools.count()
    x2d, target, pos = x[0], loss_target[0], positions[0]
    p3 = p[:, 0]
    dims = {'h_width': mla_w_uq.shape[1] + mla_w_uk.shape[1] + LANES}

    depth = ln1_g.shape[0]
    ops = _ops(depth)
    cid = pos // CHUNK
    aux = {'rope': _rope_tables(pos), 'cidq': cid[:, None], 'cidk': cid[None, :],
           'need': _mask_table(cid, _attn_tile(pos.shape[0]))}

    in_layer0 = set(MLA_PRE_W + FFN_IN_W + FFN_OUT_W + ['ffn_w_up'])
    prepped = {n: _prep_big(n, w[n], dims) for n in BIG}
    first_names = [n for n in BIG if n in MLA_PRE_W + ['mla_w_o']]
    mid_names = [n for n in BIG if n in in_layer0 and n not in first_names]
    rest_names = [n for n in BIG if prepped[n].shape[0] > (1 if n in in_layer0 else 0)]
    rest_from = {n: (1 if n in in_layer0 else 0) for n in rest_names}
    small3 = [_as3(w[n]) for n in SMALL]
    small_packed, small_sizes = _pack([s.reshape(1, -1) for s in small3], F32, 8)
    first = _run_comm(comm_gather_pass(_run_comm(comm_gather_own([prepped[n][:1] for n in first_names] + [small_packed]),
                                                 "gather_first_own")), "gather_first_pass")
    wl = {n: [None] * prepped[n].shape[0] for n in BIG}
    for n, g in zip(first_names, first):
        wl[n][0] = g[0]
    for n, s3, flat in zip(SMALL, small3, _unpack(first[-1][0], small_sizes)):
        wl[n] = _unshard(flat.reshape(N_DEV, *s3.shape), SHARD_AXIS[n] if w[n].ndim == 3 else 2)
    for n in REPL:
        wl[n] = [w[n][l][None, :] for l in range(w[n].shape[0])]

    def pick(names, layer0):
        return {n: [wl[n][l] if (l == 0 and n in in_layer0) == layer0 else None for l in range(len(wl[n]))] for n in names}

    heads0 = _mla_heads(wl, 0)
    _, attn_run, attn_bwd = make_attention(aux, heads0)
    x2db = x2d.astype(BF16)
    pre, vjp_pre = jax.vjp(lambda x_, wl_: _mla_pre(x_, x2db, wl_, 0), x2d, pick(MLA_PRE_W, True))
    o, attn_res, partial = attn_run(*pre, comm_gather_own([prepped[n][:1] for n in mid_names]
                                                          + [prepped[n][rest_from[n]:] for n in rest_names]))
    for n, g in zip(mid_names, _run_comm(comm_gather_pass(partial[:len(mid_names)]), "gather_pass")):
        wl[n][0] = g[0]
    partial = partial[len(mid_names):]
    w_up0 = wl['ffn_w_up'][0]

    def ffn_in(x_, o_, wl_):
        x1_, x1b_, cwp_, cbp_ = _ffn_in(x_, linear(o_, _rows_full(wl_['mla_w_o'][0]), F32), wl_, 0, ops, w_up0.shape[2])
        return (x1_, cwp_, cbp_), x1b_

    (x1, cwp, cbp), vjp_in, x1b = jax.vjp(ffn_in, x2d, o, pick(FFN_IN_W, True), has_aux=True)
    a, ffn_res, rest = _ffn_hidden_fwd(x1, x1b, w_up0, cwp, cbp, comm_gather_pass(partial))
    for n, g in zip(rest_names, rest):
        for l in range(g.shape[0]):
            wl[n][rest_from[n] + l] = g[l]
    x_l0, vjp_out, x_l0b = jax.vjp(lambda x1_, a_, wl_: _ffn_out(x1_, a_, wl_, p3[0], 0, ops), x1, a,
                                   pick(FFN_OUT_W, True), has_aux=True)

    def tail(x_, wl_):
        xb_ = x_l0b
        for i in range(1, depth):
            x_, xb_ = _layer(x_, xb_, wl_, p3[i], aux, i, ops)
        return x_

    y, vjp_tail = jax.vjp(tail, x_l0, pick(WEIGHTS, False))
    sq, dy = _loss_call(y, target)

    dwl = {n: [None] * len(wl[n]) for n in WEIGHTS}

    def keep(part):
        for n, per_layer in part.items():
            for l, g in enumerate(per_layer):
                if g is not None:
                    dwl[n][l] = g

    x_c, y_c, c_place = _place()
    c_idx = jnp.reshape(c_place, (1,)).astype(jnp.int32)
    chip_idx = jnp.reshape(2 * x_c + y_c, (1,)).astype(jnp.int32)
    dx_l0, d_tail = vjp_tail(dy)
    keep(d_tail)
    g_rest = [jnp.stack(dwl[n][rest_from[n]:], axis=0) for n in rest_names]
    dx1_out, da, d_out = vjp_out(dx_l0)
    keep(d_out)
    (dx1_ffn, _, dw_up0, dcwp, dcbp), recv1 = _ffn_hidden_bwd(ffn_res, da, comm_rs_pair(g_rest))
    dwl['ffn_w_up'][0] = dw_up0
    g_mid = [dwl[n][0][None] for n in mid_names]
    g_rest, recv1 = g_rest + g_mid, list(recv1) + list(_rs_pair_exchange(g_mid))
    p1 = [_rs_pair_add(g, r, c_idx) for g, r in zip(g_rest, recv1)]
    dx_in, do, d_in = vjp_in((dx1_out + dx1_ffn, dcwp, dcbp))
    keep(d_in)
    d_pre_in, recv2 = attn_bwd(attn_res, do, comm_rs_chip(p1))
    red_rest = [_rs_final_add(p_, r, chip_idx) for p_, r in zip(p1, recv2)]
    dx_pre, d_pre = vjp_pre(d_pre_in)
    keep(d_pre)
    dx = dx_pre + dx_in

    small_blocks = [_reshard(dwl[n], SHARD_AXIS[n] if w[n].ndim == 3 else 2).reshape(N_DEV, -1) for n in SMALL]
    small_grad_packed, _ = _pack(small_blocks, F32, 8)
    red_first = reduce_scatter([dwl[n][0][None] for n in first_names] + [small_grad_packed[None]])
    by_layer = {n: [] for n in BIG}
    for n, g in zip(first_names, red_first):
        by_layer[n].append(g)
    for n, g in zip(mid_names, red_rest[len(rest_names):]):
        by_layer[n].append(g)
    for n, g in zip(rest_names, red_rest):
        by_layer[n].append(g)
    grads = {n: _unprep_big(n, jnp.concatenate(by_layer[n], axis=0), w[n].shape) for n in BIG}
    for n, f in zip(SMALL, _unpack(red_first[-1][0], small_sizes)):
        grads[n] = f.reshape(w[n].shape)

    repl_flat = [jnp.concatenate([g.reshape(-1) for g in dwl[n]]).reshape(1, -1) for n in REPL]
    loss_part = 0.5 * jnp.sum(sq) / sq.shape[1]
    packed, repl_sizes = _pack(repl_flat + [loss_part.reshape(1, 1)], F32, 8)
    summed = _unpack(all_reduce_small(packed[0])[None], repl_sizes)
    for n, f in zip(REPL, summed[:-1]):
        grads[n] = f.reshape(w[n].shape)
    loss = summed[-1].reshape(())

    delta, new_m, new_v = {}, {}, {}
    for n in WEIGHTS:
        delta[n], new_m[n], new_v[n] = _adamw_call(w[n], grads[n], m_in[n], v_in[n])
    return (loss, dx[None], *[grads[n] for n in WEIGHTS], *[delta[n] for n in WEIGHTS],
            *[new_m[n] for n in WEIGHTS], *[new_v[n] for n in WEIGHTS])
```
